```python
import math
import jax, jax.numpy as jnp
from jax import lax
import numpy as np

D_MODEL = 1024
BATCH = 16
SEQ = 256
DEPTH = 1
DEC_BATCH = 4
DEC_SEQ = 2048
PAST_LEN = 256

GRID_W = 64
CONV_W = 512
CONV_K = 3
RET_HEADS = 4
RET_DK = 128
RET_DV = 128
RET_W = RET_HEADS * RET_DV
MIX_W = CONV_W + RET_W
IN_COLS = 3 * CONV_W + 2 * RET_HEADS * RET_DK + 2 * RET_W
CHUNK = 128
N_GROUPS = 4
EXPERTS_PER_GROUP = 8
N_EXPERTS = N_GROUPS * EXPERTS_PER_GROUP
TOP_K_INNER = 2
D_EXPERT = 256
ROPE_BASE = 10000.0
EPS = 1e-6

kernel_name = "hybrid_conv_retention_hmoe_diffusion_step"


def rmsnorm(x, g):
    xf = x.astype(jnp.float32)
    r = xf * lax.rsqrt(jnp.mean(xf * xf, axis=-1, keepdims=True) + EPS)
    return (r * g.astype(jnp.float32)).astype(x.dtype)


def modulation(cvec, w, b, dtype):
    h = jax.nn.silu(cvec.astype(jnp.float32)) @ w.astype(jnp.float32) + b.astype(jnp.float32)
    h = h.reshape(cvec.shape[0], 1, 6 * D_MODEL).astype(dtype)
    return jnp.split(h, 6, axis=-1)


def conv3(u, w):
    pad = [(0, 0)] * (u.ndim - 2) + [(1, 1), (0, 0)]
    up = jnp.pad(u, pad)
    return w[0] * up[..., :-2, :] + w[1] * up[..., 1:-1, :] + w[2] * up[..., 2:, :]


def grid_rope(L):
    pos = jnp.arange(L)
    row = (pos // GRID_W).astype(jnp.float32)
    col = (pos % GRID_W).astype(jnp.float32)
    n_pairs = RET_DK // 4
    freqs = ROPE_BASE ** (-(jnp.arange(n_pairs, dtype=jnp.float32) * 2.0 / (RET_DK // 2)))
    ang = jnp.concatenate([row[:, None] * freqs, col[:, None] * freqs], axis=-1)
    return jnp.cos(ang), jnp.sin(ang)


def apply_rope(x, cos, sin):
    xe = x[..., 0::2]
    xo = x[..., 1::2]
    return jnp.stack([xe * cos - xo * sin, xe * sin + xo * cos], axis=-1).reshape(x.shape)


def retention_chunked(q, k, v, log_gamma, s0):
    Bsz, H, L, dk = q.shape
    dv = v.shape[-1]
    n = L // CHUNK
    def to_chunks(t):
        return jnp.moveaxis(t.reshape(Bsz, H, n, CHUNK, t.shape[-1]), 2, 0)
    qs, ks, vs = to_chunks(q), to_chunks(k), to_chunks(v)
    idx = jnp.arange(CHUNK, dtype=jnp.float32)
    diff = idx[:, None] - idx[None, :]
    causal = diff >= 0
    lg = log_gamma[:, None, None]
    decay_mask = jnp.where(causal, jnp.exp(jnp.where(causal, diff, 0.0) * lg), 0.0)
    q_decay = jnp.exp((idx + 1.0)[None, :] * log_gamma[:, None])[..., None]
    k_decay = jnp.exp((CHUNK - 1.0 - idx)[None, :] * log_gamma[:, None])[..., None]
    chunk_decay = jnp.exp(CHUNK * log_gamma)[:, None, None]

    def step(S, inp):
        qc, kc, vc = inp
        scores = jnp.einsum('bhid,bhjd->bhij', qc, kc) * decay_mask
        inner = jnp.einsum('bhij,bhje->bhie', scores, vc)
        cross = jnp.einsum('bhid,bhde->bhie', qc * q_decay, S)
        S_new = S * chunk_decay + jnp.einsum('bhjd,bhje->bhde', kc * k_decay, vc)
        return S_new, inner + cross

    S_final, out = lax.scan(step, s0, (qs, ks, vs))
    out = jnp.moveaxis(out, 0, 2).reshape(Bsz, H, L, dv)
    return out, S_final


def mixer(xn, w_in, conv_w, w_out, a_fwd, a_bwd, s_f0, s_b0, is_grid):
    Bsz, L, _ = xn.shape
    proj = xn @ w_in
    qk_w = RET_HEADS * RET_DK
    cuts = [CONV_W, 2 * CONV_W, 3 * CONV_W, 3 * CONV_W + qk_w, 3 * CONV_W + 2 * qk_w, 3 * CONV_W + 2 * qk_w + RET_W]
    gb, gc, h, q, k, v, g = jnp.split(proj, cuts, axis=-1)

    u = gc * h
    if is_grid:
        rows = L // GRID_W
        conv = conv3(u.reshape(Bsz, rows, GRID_W, CONV_W), conv_w).reshape(Bsz, L, CONV_W)
    else:
        conv = conv3(u, conv_w)
    y_conv = gb * conv

    def heads(t, d):
        return t.reshape(Bsz, L, RET_HEADS, d).transpose(0, 2, 1, 3).astype(jnp.float32)
    qh = heads(q, RET_DK)
    kh = heads(k, RET_DK) * (RET_DK ** -0.5)
    vh = heads(v, RET_DV)
    if is_grid:
        cos, sin = grid_rope(L)
        qh = apply_rope(qh, cos, sin)
        kh = apply_rope(kh, cos, sin)
    lg_f = jnp.log1p(-jnp.exp(a_fwd.astype(jnp.float32)))
    lg_b = jnp.log1p(-jnp.exp(a_bwd.astype(jnp.float32)))
    o_f, s_f = retention_chunked(qh, kh, vh, lg_f, s_f0.astype(jnp.float32))
    flip = lambda t: jnp.flip(t, axis=2)
    o_b, s_b = retention_chunked(flip(qh), flip(kh), flip(vh), lg_b, s_b0.astype(jnp.float32))
    o = o_f + flip(o_b)
    o = o * lax.rsqrt(jnp.mean(o * o, axis=-1, keepdims=True) + EPS)
    o = o.transpose(0, 2, 1, 3).reshape(Bsz, L, RET_W).astype(xn.dtype)
    y_ret = jax.nn.silu(g) * o

    out = jnp.concatenate([y_conv, y_ret], axis=-1) @ w_out
    return out, s_f, s_b


def hier_moe(xn, w_rg, b_rg, w_re, b_re, w1, w3, w2):
    Bsz, L, D = xn.shape
    t = xn.reshape(-1, D)
    T = t.shape[0]
    lg_group = (t @ w_rg).astype(jnp.float32) + b_rg.astype(jnp.float32)
    p_group = jax.nn.softmax(lg_group, axis=-1)
    gidx = jnp.argmax(lg_group, axis=-1)
    p_sel = jnp.take_along_axis(p_group, gidx[:, None], axis=-1)[:, 0]
    lg_exp = ((t @ w_re).astype(jnp.float32) + b_re.astype(jnp.float32)).reshape(T, N_GROUPS, EXPERTS_PER_GROUP)
    lg_in = jnp.take_along_axis(lg_exp, gidx[:, None, None], axis=1)[:, 0]
    topv, topi = lax.top_k(lg_in, TOP_K_INNER)
    w_sel = jax.nn.softmax(topv, axis=-1) * p_sel[:, None]
    eidx = gidx[:, None] * EXPERTS_PER_GROUP + topi
    gates = jnp.sum(jax.nn.one_hot(eidx, N_EXPERTS, dtype=jnp.float32) * w_sel[..., None], axis=1)
    gates = gates.astype(t.dtype)

    def expert_step(acc, inp):
        w1e, w3e, w2e, ge = inp
        hid = jax.nn.silu(t @ w1e) * (t @ w3e)
        return acc + ge[:, None] * (hid @ w2e), None

    acc, _ = lax.scan(expert_step, jnp.zeros_like(t), (w1, w3, w2, gates.T))
    return acc.reshape(Bsz, L, D)


def block(x, mod, s_f0, s_b0, is_grid, p):
    (norm_mix_g, norm_ffn_g, w_in, conv_w, a_fwd, a_bwd, w_out,
     w_rg, b_rg, w_re, b_re, w1, w3, w2) = p
    sh1, sc1, g1, sh2, sc2, g2 = mod
    xn = rmsnorm(x, norm_mix_g) * (1 + sc1) + sh1
    m, s_f, s_b = mixer(xn, w_in, conv_w, w_out, a_fwd, a_bwd, s_f0, s_b0, is_grid)
    x = x + g1 * m
    xn = rmsnorm(x, norm_ffn_g) * (1 + sc2) + sh2
    x = x + g2 * hier_moe(xn, w_rg, b_rg, w_re, b_re, w1, w3, w2)
    return x, s_f, s_b


def setup_inputs(seed: int = 0) -> dict:
    key = jax.random.key(seed)
    ks = jax.random.split(key, 24)
    f32 = jnp.float32
    nrm = lambda k, shape, s: jax.random.normal(k, shape, f32) * s
    base_decay = jnp.linspace(math.log(1.0 / 32.0), math.log(1.0 / 512.0), RET_HEADS, dtype=f32)
    st_shape = (DEC_BATCH, DEPTH, RET_HEADS, RET_DK, RET_DV)
    return {
        "x_prompt": nrm(ks[0], (BATCH, SEQ, D_MODEL), 1.0),
        "x_sample": nrm(ks[1], (DEC_BATCH, DEC_SEQ, D_MODEL), 1.0),
        "state_ret_fwd": nrm(ks[2], st_shape, 0.5),
        "state_ret_bwd": nrm(ks[3], st_shape, 0.5),
        "c": nrm(ks[4], (DEC_BATCH, D_MODEL), 1.0),
        "c_ctx": nrm(ks[5], (D_MODEL,), 1.0),
        "norm_mix_g": 1.0 + nrm(ks[6], (DEPTH, D_MODEL), 0.02),
        "norm_ffn_g": 1.0 + nrm(ks[7], (DEPTH, D_MODEL), 0.02),
        "w_ada": nrm(ks[8], (DEPTH, D_MODEL, 6 * D_MODEL), 0.5 * D_MODEL ** -0.5),
        "b_ada": nrm(ks[9], (DEPTH, 6 * D_MODEL), 0.02),
        "w_in": nrm(ks[10], (DEPTH, D_MODEL, IN_COLS), D_MODEL ** -0.5),
        "conv_w": nrm(ks[11], (DEPTH, CONV_K, CONV_W), CONV_K ** -0.5),
        "ret_decay_fwd": base_decay[None, :] + nrm(ks[12], (DEPTH, RET_HEADS), 0.01),
        "ret_decay_bwd": base_decay[None, :] + nrm(ks[13], (DEPTH, RET_HEADS), 0.01),
        "w_out": nrm(ks[14], (DEPTH, MIX_W, D_MODEL), MIX_W ** -0.5),
        "w_router_group": nrm(ks[15], (DEPTH, D_MODEL, N_GROUPS), D_MODEL ** -0.5),
        "b_router_group": nrm(ks[16], (DEPTH, N_GROUPS), 0.01),
        "w_router_expert": nrm(ks[17], (DEPTH, D_MODEL, N_EXPERTS), D_MODEL ** -0.5),
        "b_router_expert": nrm(ks[18], (DEPTH, N_EXPERTS), 0.01),
        "w_gate_e": nrm(ks[19], (DEPTH, N_EXPERTS, D_MODEL, D_EXPERT), D_MODEL ** -0.5),
        "w_up_e": nrm(ks[20], (DEPTH, N_EXPERTS, D_MODEL, D_EXPERT), D_MODEL ** -0.5),
        "w_down_e": nrm(ks[21], (DEPTH, N_EXPERTS, D_EXPERT, D_MODEL), D_EXPERT ** -0.5),
        "final_norm_g": 1.0 + nrm(ks[22], (D_MODEL,), 0.02),
    }


def reference(x_prompt, x_sample, state_ret_fwd, state_ret_bwd, c, c_ctx,
              norm_mix_g, norm_ffn_g, w_ada, b_ada, w_in, conv_w,
              ret_decay_fwd, ret_decay_bwd, w_out,
              w_router_group, b_router_group, w_router_expert, b_router_expert,
              w_gate_e, w_up_e, w_down_e, final_norm_g):
    xp = x_prompt
    xs = x_sample
    new_f = []
    new_b = []
    for l in range(DEPTH):
        p = (norm_mix_g[l], norm_ffn_g[l], w_in[l], conv_w[l], ret_decay_fwd[l], ret_decay_bwd[l], w_out[l],
             w_router_group[l], b_router_group[l], w_router_expert[l], b_router_expert[l],
             w_gate_e[l], w_up_e[l], w_down_e[l])
        mod_ctx = modulation(c_ctx[None, :], w_ada[l], b_ada[l], xp.dtype)
        mod_lat = modulation(c, w_ada[l], b_ada[l], xs.dtype)
        zeros = jnp.zeros((xp.shape[0], RET_HEADS, RET_DK, RET_DV), jnp.float32)
        xp, s_f, s_b = block(xp, mod_ctx, zeros, zeros, False, p)
        xs, _, _ = block(xs, mod_lat, state_ret_fwd[:, l], state_ret_bwd[:, l], True, p)
        new_f.append(s_f)
        new_b.append(s_b)
    y_prompt = rmsnorm(xp, final_norm_g)
    y_sample = rmsnorm(xs, final_norm_g)
    new_state_ret_fwd = jnp.stack(new_f, axis=1).astype(x_prompt.dtype)
    new_state_ret_bwd = jnp.stack(new_b, axis=1).astype(x_prompt.dtype)
    return (y_prompt, y_sample, new_state_ret_fwd, new_state_ret_bwd)
```

```python
import functools
import math

import jax
import jax.numpy as jnp
from jax import lax
from jax.experimental import pallas as pl
from jax.experimental.pallas import tpu as pltpu

F32 = jnp.float32
BF16 = jnp.bfloat16

D_MODEL = 1024
GRID_W = 64
CONV_W = 512
RET_HEADS = 4
RET_DK = 128
RET_DV = 128
RET_W = RET_HEADS * RET_DV
QK_W = RET_HEADS * RET_DK
CHUNK = 128
N_GROUPS = 4
EXPERTS_PER_GROUP = 8
N_EXPERTS = N_GROUPS * EXPERTS_PER_GROUP
D_EXPERT = 256
ROPE_BASE = 10000.0
EPS = 1e-6

LANES = 128
TOKEN_TILE = 256
MOE_TILE = 1024
MOD_COLS = 1536
ROUTER_COLS = LANES
VMEM_LIMIT = 48 * 1024 * 1024


def _silu(x):
    return x * jax.nn.sigmoid(x)


def _rms(x):
    return x * lax.rsqrt(jnp.mean(x * x, axis=-1, keepdims=True) + EPS)


def _bdot(a, b):
    return jnp.dot(a.astype(BF16), b.astype(BF16), preferred_element_type=F32)


def _mod_kernel(c_ref, w_ref, b_ref, o_ref):
    o_ref[...] = _bdot(_silu(c_ref[...]), w_ref[...]) + b_ref[...]


def _modulation(cvec, w_ada, b_ada):
    rows = cvec.shape[0]
    n = w_ada.shape[1]
    return pl.pallas_call(
        _mod_kernel,
        grid=(n // MOD_COLS,),
        in_specs=[
            pl.BlockSpec((rows, D_MODEL), lambda j: (0, 0)),
            pl.BlockSpec((D_MODEL, MOD_COLS), lambda j: (0, j)),
            pl.BlockSpec((1, MOD_COLS), lambda j: (0, j)),
        ],
        out_specs=pl.BlockSpec((rows, MOD_COLS), lambda j: (0, j)),
        out_shape=jax.ShapeDtypeStruct((rows, n), F32),
        compiler_params=pltpu.CompilerParams(vmem_limit_bytes=VMEM_LIMIT),
        name="modulation",
    )(cvec, w_ada, b_ada)


def _inproj_kernel(seg, is_grid, x_ref, sh_ref, sc_ref, ng_ref, w_ref, cw_ref, *rest):
    if is_grid:
        cos_ref, sa_ref, sb_ref, yc_ref, q_ref, k_ref, v_ref, g_ref = rest
    else:
        yc_ref, q_ref, k_ref, v_ref, g_ref = rest
    x = x_ref[0]
    xn = (_rms(x) * ng_ref[...]) * (1.0 + sc_ref[0, 0]) + sh_ref[0, 0]
    xb = xn.astype(BF16)

    def proj(c0, n):
        return jnp.dot(xb, w_ref[:, c0:c0 + n], preferred_element_type=F32)

    gate_b = proj(0, CONV_W)
    u = proj(CONV_W, CONV_W) * proj(2 * CONV_W, CONV_W)
    rows = u.shape[0]
    pos = lax.broadcasted_iota(jnp.int32, u.shape, 0) & (seg - 1)
    u_prev = jnp.where(pos != 0, pltpu.roll(u, 1, 0), 0.0)
    u_next = jnp.where(pos != seg - 1, pltpu.roll(u, rows - 1, 0), 0.0)
    conv = cw_ref[0:1, :] * u_prev + cw_ref[1:2, :] * u + cw_ref[2:3, :] * u_next
    yc_ref[0] = (gate_b * conv).astype(yc_ref.dtype)

    q0 = 3 * CONV_W
    q = proj(q0, QK_W)
    k = proj(q0 + QK_W, QK_W)
    if is_grid:
        cos, sa, sb = cos_ref[...], sa_ref[...], sb_ref[...]

        def rope(t):
            out = []
            for h in range(RET_HEADS):
                th = t[:, h * RET_DK:(h + 1) * RET_DK]
                out.append(th * cos + pltpu.roll(th, RET_DK - 1, 1) * sa + pltpu.roll(th, 1, 1) * sb)
            return jnp.concatenate(out, axis=1)

        q, k = rope(q), rope(k)
    q_ref[0] = q
    k_ref[0] = k
    v_ref[0] = proj(q0 + 2 * QK_W, RET_W)
    g_ref[0] = proj(q0 + 2 * QK_W + RET_W, RET_W)


def _rope_tables(length):
    pos = jnp.arange(length)
    row = (pos // GRID_W).astype(F32)
    col = (pos % GRID_W).astype(F32)
    n_pairs = RET_DK // 4
    freqs = ROPE_BASE ** (-(jnp.arange(n_pairs, dtype=F32) * 2.0 / (RET_DK // 2)))
    ang = jnp.concatenate([row[:, None] * freqs, col[:, None] * freqs], axis=-1)
    cos = jnp.repeat(jnp.cos(ang), 2, axis=-1)
    sin = jnp.repeat(jnp.sin(ang), 2, axis=-1)
    even = (jnp.arange(RET_DK) % 2) == 0
    return cos, jnp.where(even, -sin, 0.0), jnp.where(even, 0.0, sin)


def _inproj(x, mod4, mod_row, norm_g, w_in_bf, conv_w, is_grid):
    bsz, length, _ = x.shape
    seg = GRID_W if is_grid else length
    assert TOKEN_TILE % seg == 0 and length % TOKEN_TILE == 0
    tiles = length // TOKEN_TILE

    def mod_spec(which):
        return pl.BlockSpec((1, 1, 1, D_MODEL), lambda b, i: (mod_row(b), which, 0, 0))

    def tok_spec(width):
        return pl.BlockSpec((1, TOKEN_TILE, width), lambda b, i: (b, i, 0))

    in_specs = [
        tok_spec(D_MODEL), mod_spec(0), mod_spec(1),
        pl.BlockSpec((1, D_MODEL), lambda b, i: (0, 0)),
        pl.BlockSpec(w_in_bf.shape, lambda b, i: (0, 0)),
        pl.BlockSpec(conv_w.shape, lambda b, i: (0, 0)),
    ]
    args = [x, mod4, mod4, norm_g, w_in_bf, conv_w]
    if is_grid:
        in_specs += [pl.BlockSpec((TOKEN_TILE, RET_DK), lambda b, i: (i, 0))] * 3
        args += list(_rope_tables(length))
    shp = lambda w, dt: jax.ShapeDtypeStruct((bsz, length, w), dt)
    return pl.pallas_call(
        functools.partial(_inproj_kernel, seg, is_grid),
        grid=(bsz, tiles),
        in_specs=in_specs,
        out_specs=[tok_spec(CONV_W), tok_spec(QK_W), tok_spec(QK_W), tok_spec(RET_W), tok_spec(RET_W)],
        out_shape=[shp(CONV_W, BF16), shp(QK_W, F32), shp(QK_W, F32), shp(RET_W, F32), shp(RET_W, F32)],
        compiler_params=pltpu.CompilerParams(
            dimension_semantics=("parallel", "parallel"), vmem_limit_bytes=VMEM_LIMIT),
        name="inproj_grid" if is_grid else "inproj_seq",
    )(*args)


def _ret_kernel(n_chunks, has_init, emit_state, a_ref, q_ref, k_ref, v_ref, g_ref, *rest):
    rest = list(rest)
    if has_init:
        sf0_ref, sb0_ref = rest[:2]
        rest = rest[2:]
    y_ref = rest.pop(0)
    if emit_state:
        sf_out, sb_out = rest[:2]
        rest = rest[2:]
    sb_in, sf_scr, sb_scr, dec = rest
    h = pl.program_id(1)
    c = CHUNK
    sq = (c, c)
    lg_f = jnp.log1p(-jnp.exp(a_ref[pl.ds(h, 1), :]))
    lg_b = jnp.log1p(-jnp.exp(a_ref[pl.ds(h + RET_HEADS, 1), :]))
    row = lax.broadcasted_iota(jnp.int32, sq, 0).astype(F32)
    col = lax.broadcasted_iota(jnp.int32, sq, 1).astype(F32)
    scale = RET_DK ** -0.5
    dec[0] = scale * (jnp.where(row >= col, jnp.exp(jnp.where(row >= col, row - col, 0.0) * lg_f), 0.0)
                      + jnp.where(col >= row, jnp.exp(jnp.where(col >= row, col - row, 0.0) * lg_b), 0.0))
    dec[1] = jnp.exp((row + 1.0) * lg_f)
    dec[2] = jnp.exp((c - row) * lg_b)
    dec[3] = scale * jnp.exp((c - 1.0 - row) * lg_f)
    dec[4] = scale * jnp.exp(row * lg_b)
    cd_f = jnp.exp(c * lg_f)
    cd_b = jnp.exp(c * lg_b)

    def chunk(ref, n):
        return ref[0, pl.ds(pl.multiple_of(n * c, c), c), :]

    def kv_update(n, s_ref, kd, cd):
        kt = jnp.transpose(chunk(k_ref, n) * kd)
        s_ref[...] = s_ref[...] * cd + _bdot(kt, chunk(v_ref, n))

    if has_init:
        sf_scr[...] = sf0_ref[0, 0, 0]
        sb_scr[...] = sb0_ref[0, 0, 0]
    else:
        sf_scr[...] = jnp.zeros(sq, F32)
        sb_scr[...] = jnp.zeros(sq, F32)

    def bwd_step(i, carry):
        n = n_chunks - 1 - i
        sb_in[n] = sb_scr[...]
        kv_update(n, sb_scr, dec[4], cd_b)
        return carry

    def fwd_step(n, carry):
        q = chunk(q_ref, n)
        scores = lax.dot_general(q.astype(BF16), chunk(k_ref, n).astype(BF16),
                                 (((1,), (1,)), ((), ())), preferred_element_type=F32)
        o = _bdot(scores * dec[0], chunk(v_ref, n))
        o = o + _bdot(q * dec[1], sf_scr[...]) + _bdot(q * dec[2], sb_in[n])
        o = _rms(o)
        y = _silu(chunk(g_ref, n)) * o
        y_ref[0, pl.ds(pl.multiple_of(n * c, c), c), :] = y.astype(y_ref.dtype)
        kv_update(n, sf_scr, dec[3], cd_f)
        return carry

    if n_chunks <= 2:
        for i in range(n_chunks):
            bwd_step(i, 0)
        for n in range(n_chunks):
            fwd_step(n, 0)
    else:
        lax.fori_loop(0, n_chunks, bwd_step, 0)
        lax.fori_loop(0, n_chunks, fwd_step, 0)
    if emit_state:
        sf_out[0, 0, 0] = sf_scr[...]
        sb_out[0, 0, 0] = sb_scr[...]


def _retention(q, k, v, g, decay_rows, s_f0, s_b0, emit_state):
    bsz, length, _ = q.shape
    n_chunks = length // CHUNK
    has_init = s_f0 is not None
    head_spec = pl.BlockSpec((1, length, RET_DK), lambda b, h: (b, 0, h))
    st_spec = pl.BlockSpec((1, 1, 1, RET_DK, RET_DV), lambda b, h: (b, 0, h, 0, 0))
    in_specs = [pl.BlockSpec(decay_rows.shape, lambda b, h: (0, 0))] + [head_spec] * 4
    args = [decay_rows, q, k, v, g]
    if has_init:
        in_specs += [st_spec, st_spec]
        args += [s_f0, s_b0]
    out_specs = [head_spec]
    out_shape = [jax.ShapeDtypeStruct((bsz, length, RET_W), BF16)]
    if emit_state:
        st_shape = jax.ShapeDtypeStruct((bsz, 1, RET_HEADS, RET_DK, RET_DV), F32)
        out_specs += [st_spec, st_spec]
        out_shape += [st_shape, st_shape]
    return pl.pallas_call(
        functools.partial(_ret_kernel, n_chunks, has_init, emit_state),
        grid=(bsz, RET_HEADS),
        in_specs=in_specs,
        out_specs=out_specs,
        out_shape=out_shape,
        scratch_shapes=[
            pltpu.VMEM((n_chunks, RET_DK, RET_DV), F32),
            pltpu.VMEM((RET_DK, RET_DV), F32),
            pltpu.VMEM((RET_DK, RET_DV), F32),
            pltpu.VMEM((5, CHUNK, CHUNK), F32),
        ],
        compiler_params=pltpu.CompilerParams(
            dimension_semantics=("parallel", "parallel"), vmem_limit_bytes=VMEM_LIMIT),
        name="retention_init" if has_init else "retention_zero",
    )(*args)


def _route(logits):
    lane = lax.broadcasted_iota(jnp.int32, logits.shape, 1)
    lane_f = lane.astype(F32)
    neg = -jnp.inf
    far = float(LANES)
    is_g = lane < N_GROUPS
    lg = jnp.where(is_g, logits, neg)
    g_max = jnp.max(lg, axis=1, keepdims=True)
    g_idx = jnp.min(jnp.where(lg == g_max, lane_f, far), axis=1, keepdims=True)
    p_sel = 1.0 / jnp.sum(jnp.where(is_g, jnp.exp(lg - g_max), 0.0), axis=1, keepdims=True)
    lane_group = ((lane - N_GROUPS) >> 3).astype(F32)
    sel = (lane >= N_GROUPS) & (lane < N_GROUPS + N_EXPERTS) & (lane_group == g_idx)
    le = jnp.where(sel, logits, neg)
    v1 = jnp.max(le, axis=1, keepdims=True)
    i1 = jnp.min(jnp.where(le == v1, lane_f, far), axis=1, keepdims=True)
    le2 = jnp.where(lane_f == i1, neg, le)
    v2 = jnp.max(le2, axis=1, keepdims=True)
    i2 = jnp.min(jnp.where(le2 == v2, lane_f, far), axis=1, keepdims=True)
    e2 = jnp.exp(v2 - v1)
    w1 = p_sel * (1.0 / (1.0 + e2))
    w2 = p_sel * (e2 / (1.0 + e2))
    return jnp.where(lane_f == i1, w1, 0.0) + jnp.where(lane_f == i2, w2, 0.0)


def _outproj_kernel(yc_ref, yr_ref, x_ref, g1_ref, sh_ref, sc_ref, ng_ref, wo_ref, wr_ref, br_ref,
                    x1_ref, xn_ref, gates_ref):
    m = (jnp.dot(yc_ref[0], wo_ref[0:CONV_W, :], preferred_element_type=F32)
         + jnp.dot(yr_ref[0], wo_ref[CONV_W:, :], preferred_element_type=F32))
    x1 = x_ref[0] + g1_ref[0, 0] * m
    x1_ref[0] = x1
    xn = (_rms(x1) * ng_ref[...]) * (1.0 + sc_ref[0, 0]) + sh_ref[0, 0]
    xb = xn.astype(BF16)
    xn_ref[0] = xb
    logits = jnp.dot(xb, wr_ref[...], preferred_element_type=F32) + br_ref[...]
    gates_ref[0] = _route(logits)


def _outproj(y_conv, y_ret, x, mod4, mod_row, norm_g, w_out_bf, w_router_bf, b_router):
    bsz, length, _ = x.shape
    tiles = length // TOKEN_TILE

    def mod_spec(which):
        return pl.BlockSpec((1, 1, 1, D_MODEL), lambda b, i: (mod_row(b), which, 0, 0))

    def tok_spec(width):
        return pl.BlockSpec((1, TOKEN_TILE, width), lambda b, i: (b, i, 0))

    full = lambda a: pl.BlockSpec(a.shape, lambda b, i: (0,) * a.ndim)
    shp = lambda w, dt: jax.ShapeDtypeStruct((bsz, length, w), dt)
    return pl.pallas_call(
        _outproj_kernel,
        grid=(bsz, tiles),
        in_specs=[tok_spec(CONV_W), tok_spec(RET_W), tok_spec(D_MODEL),
                  mod_spec(2), mod_spec(3), mod_spec(4),
                  full(norm_g), full(w_out_bf), full(w_router_bf), full(b_router)],
        out_specs=[tok_spec(D_MODEL), tok_spec(D_MODEL), tok_spec(ROUTER_COLS)],
        out_shape=[shp(D_MODEL, F32), shp(D_MODEL, BF16), shp(ROUTER_COLS, F32)],
        compiler_params=pltpu.CompilerParams(
            dimension_semantics=("parallel", "parallel"), vmem_limit_bytes=VMEM_LIMIT),
        name="outproj",
    )(y_conv, y_ret, x, mod4, mod4, mod4, norm_g, w_out_bf, w_router_bf, b_router)


def _moe_kernel(xn_ref, gates_ref, w1_ref, w3_ref, w2_ref, x1_ref, g2_ref, fg_ref, o_ref, acc_ref):
    e = pl.program_id(1)

    @pl.when(e == 0)
    def _():
        acc_ref[...] = jnp.zeros_like(acc_ref)

    x = xn_ref[...]
    hid = _silu(_bdot(x, w1_ref[0])) * _bdot(x, w3_ref[0])
    gates = gates_ref[...]
    lane = lax.broadcasted_iota(jnp.int32, gates.shape, 1)
    ge = jnp.sum(jnp.where(lane == e + N_GROUPS, gates, 0.0), axis=1, keepdims=True)
    acc_ref[...] += ge * _bdot(hid, w2_ref[0])

    @pl.when(e == N_EXPERTS - 1)
    def _():
        y = x1_ref[...] + g2_ref[0, 0] * acc_ref[...]
        o_ref[...] = _rms(y) * fg_ref[...]


def _moe(xn, gates, x1, mod4, mod_row_of_tile, w1, w3, w2, final_g):
    tokens = xn.shape[0]
    tile = MOE_TILE
    tok = lambda w: pl.BlockSpec((tile, w), lambda i, e: (i, 0))
    return pl.pallas_call(
        _moe_kernel,
        grid=(tokens // tile, N_EXPERTS),
        in_specs=[
            tok(D_MODEL), tok(ROUTER_COLS),
            pl.BlockSpec((1, D_MODEL, D_EXPERT), lambda i, e: (e, 0, 0)),
            pl.BlockSpec((1, D_MODEL, D_EXPERT), lambda i, e: (e, 0, 0)),
            pl.BlockSpec((1, D_EXPERT, D_MODEL), lambda i, e: (e, 0, 0)),
            tok(D_MODEL),
            pl.BlockSpec((1, 1, 1, D_MODEL), lambda i, e: (mod_row_of_tile(i), 5, 0, 0)),
            pl.BlockSpec((1, D_MODEL), lambda i, e: (0, 0)),
        ],
        out_specs=tok(D_MODEL),
        out_shape=jax.ShapeDtypeStruct((tokens, D_MODEL), F32),
        scratch_shapes=[pltpu.VMEM((tile, D_MODEL), F32)],
        compiler_params=pltpu.CompilerParams(
            dimension_semantics=("parallel", "arbitrary"), vmem_limit_bytes=VMEM_LIMIT),
        name="moe",
    )(xn, gates, w1, w3, w2, x1, mod4, final_g)


def _block(x, mod4, mod_row, moe_row, is_grid, s_f0, s_b0, p):
    (norm_mix_g, norm_ffn_g, w_in_bf, conv_w, decay_rows, w_out_bf, w_router_bf, b_router,
     w1, w3, w2, final_g) = p
    bsz, length, _ = x.shape
    y_conv, q, k, v, g = _inproj(x, mod4, mod_row, norm_mix_g, w_in_bf, conv_w, is_grid)
    ret = _retention(q, k, v, g, decay_rows, s_f0, s_b0, emit_state=not is_grid)
    x1, xn, gates = _outproj(y_conv, ret[0], x, mod4, mod_row, norm_ffn_g, w_out_bf, w_router_bf, b_router)
    flat = lambda a: a.reshape(bsz * length, a.shape[-1])
    y = _moe(flat(xn), flat(gates), flat(x1), mod4, moe_row, w1, w3, w2, final_g)
    return y.reshape(x.shape), ret[1:]


def kernel(x_prompt, x_sample, state_ret_fwd, state_ret_bwd, c, c_ctx, norm_mix_g, norm_ffn_g, w_ada, b_ada, w_in, conv_w, ret_decay_fwd, ret_decay_bwd, w_out, w_router_group, b_router_group, w_router_expert, b_router_expert, w_gate_e, w_up_e, w_down_e, final_norm_g):
    assert norm_mix_g.shape[0] == 1, "single-layer backbone"
    n_lat = c.shape[0]
    ctx_row = n_lat
    mod_rows = 8
    cvec = jnp.concatenate([c, c_ctx[None, :], jnp.zeros((mod_rows - n_lat - 1, D_MODEL), F32)], axis=0)
    mod = _modulation(cvec, w_ada[0], b_ada[0][None, :])
    mod4 = mod.reshape(mod_rows, 6, 1, D_MODEL)

    pad = ROUTER_COLS - N_GROUPS - N_EXPERTS
    w_router = jnp.concatenate(
        [w_router_group[0], w_router_expert[0], jnp.zeros((D_MODEL, pad), F32)], axis=1).astype(BF16)
    b_router = jnp.concatenate([b_router_group[0], b_router_expert[0], jnp.zeros((pad,), F32)])[None, :]
    decay_rows = jnp.broadcast_to(
        jnp.concatenate([ret_decay_fwd[0], ret_decay_bwd[0]])[:, None], (2 * RET_HEADS, LANES))
    p = (norm_mix_g, norm_ffn_g, w_in[0].astype(BF16), conv_w[0], decay_rows, w_out[0].astype(BF16),
         w_router, b_router, w_gate_e[0], w_up_e[0], w_down_e[0], final_norm_g[None, :])

    lat_tiles_per_batch = x_sample.shape[1] // MOE_TILE
    y_prompt, (s_f, s_b) = _block(
        x_prompt, mod4, lambda b: ctx_row, lambda i: ctx_row, False, None, None, p)
    y_sample, _ = _block(
        x_sample, mod4, lambda b: b, lambda i: i // lat_tiles_per_batch, True,
        state_ret_fwd, state_ret_bwd, p)
    return (y_prompt, y_sample, s_f.astype(x_prompt.dtype), s_b.astype(x_prompt.dtype))
```

```python
import functools
import math

import jax
import jax.numpy as jnp
from jax import lax
from jax.experimental import pallas as pl
from jax.experimental.pallas import tpu as pltpu

F32 = jnp.float32
BF16 = jnp.bfloat16

D_MODEL = 1024
GRID_W = 64
CONV_W = 512
RET_HEADS = 4
RET_DK = 128
RET_DV = 128
RET_W = RET_HEADS * RET_DV
QK_W = RET_HEADS * RET_DK
CHUNK = 128
N_GROUPS = 4
EXPERTS_PER_GROUP = 8
N_EXPERTS = N_GROUPS * EXPERTS_PER_GROUP
D_EXPERT = 256
ROPE_BASE = 10000.0
EPS = 1e-6

LANES = 128
TOKEN_TILE = 256
EXPERT_TILE = 256
MOD_COLS = 1536
ROUTER_COLS = LANES
VMEM_LIMIT = 48 * 1024 * 1024


def _silu(x):
    return x * jax.nn.sigmoid(x)


def _rms(x):
    return x * lax.rsqrt(jnp.mean(x * x, axis=-1, keepdims=True) + EPS)


def _bdot(a, b):
    return jnp.dot(a.astype(BF16), b.astype(BF16), preferred_element_type=F32)


def _mod_kernel(c_ref, w_ref, b_ref, o_ref):
    o_ref[...] = _bdot(_silu(c_ref[...]), w_ref[...]) + b_ref[...]


def _modulation(cvec, w_ada, b_ada):
    rows = cvec.shape[0]
    n = w_ada.shape[1]
    return pl.pallas_call(
        _mod_kernel,
        grid=(n // MOD_COLS,),
        in_specs=[
            pl.BlockSpec((rows, D_MODEL), lambda j: (0, 0)),
            pl.BlockSpec((D_MODEL, MOD_COLS), lambda j: (0, j)),
            pl.BlockSpec((1, MOD_COLS), lambda j: (0, j)),
        ],
        out_specs=pl.BlockSpec((rows, MOD_COLS), lambda j: (0, j)),
        out_shape=jax.ShapeDtypeStruct((rows, n), F32),
        compiler_params=pltpu.CompilerParams(vmem_limit_bytes=VMEM_LIMIT),
        name="modulation",
    )(cvec, w_ada, b_ada)


def _inproj_kernel(seg, is_grid, x_ref, sh_ref, sc_ref, ng_ref, w_ref, cw_ref, *rest):
    if is_grid:
        cos_ref, sa_ref, sb_ref, yc_ref, q_ref, k_ref, v_ref, g_ref = rest
    else:
        yc_ref, q_ref, k_ref, v_ref, g_ref = rest
    x = x_ref[0]
    xn = (_rms(x) * ng_ref[...]) * (1.0 + sc_ref[0, 0]) + sh_ref[0, 0]
    xb = xn.astype(BF16)

    def proj(c0, n):
        return jnp.dot(xb, w_ref[:, c0:c0 + n], preferred_element_type=F32)

    gate_b = proj(0, CONV_W)
    u = proj(CONV_W, CONV_W) * proj(2 * CONV_W, CONV_W)
    rows = u.shape[0]
    pos = lax.broadcasted_iota(jnp.int32, u.shape, 0) & (seg - 1)
    u_prev = jnp.where(pos != 0, pltpu.roll(u, 1, 0), 0.0)
    u_next = jnp.where(pos != seg - 1, pltpu.roll(u, rows - 1, 0), 0.0)
    conv = cw_ref[0:1, :] * u_prev + cw_ref[1:2, :] * u + cw_ref[2:3, :] * u_next
    yc_ref[0] = (gate_b * conv).astype(yc_ref.dtype)

    q0 = 3 * CONV_W
    q = proj(q0, QK_W)
    k = proj(q0 + QK_W, QK_W)
    if is_grid:
        cos, sa, sb = cos_ref[...], sa_ref[...], sb_ref[...]

        def rope(t):
            out = []
            for h in range(RET_HEADS):
                th = t[:, h * RET_DK:(h + 1) * RET_DK]
                out.append(th * cos + pltpu.roll(th, RET_DK - 1, 1) * sa + pltpu.roll(th, 1, 1) * sb)
            return jnp.concatenate(out, axis=1)

        q, k = rope(q), rope(k)
    q_ref[0] = q
    k_ref[0] = k
    v_ref[0] = proj(q0 + 2 * QK_W, RET_W)
    g_ref[0] = proj(q0 + 2 * QK_W + RET_W, RET_W)


def _rope_tables(length):
    pos = jnp.arange(length)
    row = (pos // GRID_W).astype(F32)
    col = (pos % GRID_W).astype(F32)
    n_pairs = RET_DK // 4
    freqs = ROPE_BASE ** (-(jnp.arange(n_pairs, dtype=F32) * 2.0 / (RET_DK // 2)))
    ang = jnp.concatenate([row[:, None] * freqs, col[:, None] * freqs], axis=-1)
    cos = jnp.repeat(jnp.cos(ang), 2, axis=-1)
    sin = jnp.repeat(jnp.sin(ang), 2, axis=-1)
    even = (jnp.arange(RET_DK) % 2) == 0
    return cos, jnp.where(even, -sin, 0.0), jnp.where(even, 0.0, sin)


def _inproj(x, mod4, mod_row, norm_g, w_in_bf, conv_w, is_grid):
    bsz, length, _ = x.shape
    seg = GRID_W if is_grid else length
    assert TOKEN_TILE % seg == 0 and length % TOKEN_TILE == 0
    tiles = length // TOKEN_TILE

    def mod_spec(which):
        return pl.BlockSpec((1, 1, 1, D_MODEL), lambda b, i: (mod_row(b), which, 0, 0))

    def tok_spec(width):
        return pl.BlockSpec((1, TOKEN_TILE, width), lambda b, i: (b, i, 0))

    in_specs = [
        tok_spec(D_MODEL), mod_spec(0), mod_spec(1),
        pl.BlockSpec((1, D_MODEL), lambda b, i: (0, 0)),
        pl.BlockSpec(w_in_bf.shape, lambda b, i: (0, 0)),
        pl.BlockSpec(conv_w.shape, lambda b, i: (0, 0)),
    ]
    args = [x, mod4, mod4, norm_g, w_in_bf, conv_w]
    if is_grid:
        in_specs += [pl.BlockSpec((TOKEN_TILE, RET_DK), lambda b, i: (i, 0))] * 3
        args += list(_rope_tables(length))
    shp = lambda w, dt: jax.ShapeDtypeStruct((bsz, length, w), dt)
    return pl.pallas_call(
        functools.partial(_inproj_kernel, seg, is_grid),
        grid=(bsz, tiles),
        in_specs=in_specs,
        out_specs=[tok_spec(CONV_W), tok_spec(QK_W), tok_spec(QK_W), tok_spec(RET_W), tok_spec(RET_W)],
        out_shape=[shp(CONV_W, BF16), shp(QK_W, F32), shp(QK_W, F32), shp(RET_W, F32), shp(RET_W, F32)],
        compiler_params=pltpu.CompilerParams(
            dimension_semantics=("parallel", "parallel"), vmem_limit_bytes=VMEM_LIMIT),
        name="inproj_grid" if is_grid else "inproj_seq",
    )(*args)


def _ret_kernel(n_chunks, has_init, emit_state, a_ref, q_ref, k_ref, v_ref, g_ref, *rest):
    rest = list(rest)
    if has_init:
        sf0_ref, sb0_ref = rest[:2]
        rest = rest[2:]
    y_ref = rest.pop(0)
    if emit_state:
        sf_out, sb_out = rest[:2]
        rest = rest[2:]
    sb_in, sf_scr, sb_scr, dec = rest
    h = pl.program_id(1)
    c = CHUNK
    sq = (c, c)
    lg_f = jnp.log1p(-jnp.exp(a_ref[pl.ds(h, 1), :]))
    lg_b = jnp.log1p(-jnp.exp(a_ref[pl.ds(h + RET_HEADS, 1), :]))
    row = lax.broadcasted_iota(jnp.int32, sq, 0).astype(F32)
    col = lax.broadcasted_iota(jnp.int32, sq, 1).astype(F32)
    scale = RET_DK ** -0.5
    dec[0] = scale * (jnp.where(row >= col, jnp.exp(jnp.where(row >= col, row - col, 0.0) * lg_f), 0.0)
                      + jnp.where(col >= row, jnp.exp(jnp.where(col >= row, col - row, 0.0) * lg_b), 0.0))
    dec[1] = jnp.exp((row + 1.0) * lg_f)
    dec[2] = jnp.exp((c - row) * lg_b)
    dec[3] = scale * jnp.exp((c - 1.0 - row) * lg_f)
    dec[4] = scale * jnp.exp(row * lg_b)
    cd_f = jnp.exp(c * lg_f)
    cd_b = jnp.exp(c * lg_b)

    def chunk(ref, n):
        return ref[0, pl.ds(pl.multiple_of(n * c, c), c), :]

    def kv_update(n, s_ref, kd, cd):
        kt = jnp.transpose(chunk(k_ref, n) * kd)
        s_ref[...] = s_ref[...] * cd + _bdot(kt, chunk(v_ref, n))

    if has_init:
        sf_scr[...] = sf0_ref[0, 0, 0]
        sb_scr[...] = sb0_ref[0, 0, 0]
    else:
        sf_scr[...] = jnp.zeros(sq, F32)
        sb_scr[...] = jnp.zeros(sq, F32)

    def bwd_step(i, carry):
        n = n_chunks - 1 - i
        sb_in[n] = sb_scr[...]
        kv_update(n, sb_scr, dec[4], cd_b)
        return carry

    def fwd_step(n, carry):
        q = chunk(q_ref, n)
        scores = lax.dot_general(q.astype(BF16), chunk(k_ref, n).astype(BF16),
                                 (((1,), (1,)), ((), ())), preferred_element_type=F32)
        o = _bdot(scores * dec[0], chunk(v_ref, n))
        o = o + _bdot(q * dec[1], sf_scr[...]) + _bdot(q * dec[2], sb_in[n])
        o = _rms(o)
        y = _silu(chunk(g_ref, n)) * o
        y_ref[0, pl.ds(pl.multiple_of(n * c, c), c), :] = y.astype(y_ref.dtype)
        kv_update(n, sf_scr, dec[3], cd_f)
        return carry

    if n_chunks <= 2:
        for i in range(n_chunks):
            bwd_step(i, 0)
        for n in range(n_chunks):
            fwd_step(n, 0)
    else:
        lax.fori_loop(0, n_chunks, bwd_step, 0)
        lax.fori_loop(0, n_chunks, fwd_step, 0)
    if emit_state:
        sf_out[0, 0, 0] = sf_scr[...]
        sb_out[0, 0, 0] = sb_scr[...]


def _retention(q, k, v, g, decay_rows, s_f0, s_b0, emit_state):
    bsz, length, _ = q.shape
    n_chunks = length // CHUNK
    has_init = s_f0 is not None
    head_spec = pl.BlockSpec((1, length, RET_DK), lambda b, h: (b, 0, h))
    st_spec = pl.BlockSpec((1, 1, 1, RET_DK, RET_DV), lambda b, h: (b, 0, h, 0, 0))
    in_specs = [pl.BlockSpec(decay_rows.shape, lambda b, h: (0, 0))] + [head_spec] * 4
    args = [decay_rows, q, k, v, g]
    if has_init:
        in_specs += [st_spec, st_spec]
        args += [s_f0, s_b0]
    out_specs = [head_spec]
    out_shape = [jax.ShapeDtypeStruct((bsz, length, RET_W), BF16)]
    if emit_state:
        st_shape = jax.ShapeDtypeStruct((bsz, 1, RET_HEADS, RET_DK, RET_DV), F32)
        out_specs += [st_spec, st_spec]
        out_shape += [st_shape, st_shape]
    return pl.pallas_call(
        functools.partial(_ret_kernel, n_chunks, has_init, emit_state),
        grid=(bsz, RET_HEADS),
        in_specs=in_specs,
        out_specs=out_specs,
        out_shape=out_shape,
        scratch_shapes=[
            pltpu.VMEM((n_chunks, RET_DK, RET_DV), F32),
            pltpu.VMEM((RET_DK, RET_DV), F32),
            pltpu.VMEM((RET_DK, RET_DV), F32),
            pltpu.VMEM((5, CHUNK, CHUNK), F32),
        ],
        compiler_params=pltpu.CompilerParams(
            dimension_semantics=("parallel", "parallel"), vmem_limit_bytes=VMEM_LIMIT),
        name="retention_init" if has_init else "retention_zero",
    )(*args)


def _route(logits):
    lane = lax.broadcasted_iota(jnp.int32, logits.shape, 1)
    lane_f = lane.astype(F32)
    neg = -jnp.inf
    far = float(LANES)
    is_g = lane < N_GROUPS
    lg = jnp.where(is_g, logits, neg)
    g_max = jnp.max(lg, axis=1, keepdims=True)
    g_idx = jnp.min(jnp.where(lg == g_max, lane_f, far), axis=1, keepdims=True)
    p_sel = 1.0 / jnp.sum(jnp.where(is_g, jnp.exp(lg - g_max), 0.0), axis=1, keepdims=True)
    lane_group = ((lane - N_GROUPS) >> 3).astype(F32)
    sel = (lane >= N_GROUPS) & (lane < N_GROUPS + N_EXPERTS) & (lane_group == g_idx)
    le = jnp.where(sel, logits, neg)
    v1 = jnp.max(le, axis=1, keepdims=True)
    i1 = jnp.min(jnp.where(le == v1, lane_f, far), axis=1, keepdims=True)
    le2 = jnp.where(lane_f == i1, neg, le)
    v2 = jnp.max(le2, axis=1, keepdims=True)
    i2 = jnp.min(jnp.where(le2 == v2, lane_f, far), axis=1, keepdims=True)
    e2 = jnp.exp(v2 - v1)
    w1 = p_sel * (1.0 / (1.0 + e2))
    w2 = p_sel * (e2 / (1.0 + e2))
    return lane, lane_f, i1, i2, w1, w2


def _outproj_kernel(yc_ref, yr_ref, x_ref, g1_ref, sh_ref, sc_ref, ng_ref, wo_ref, wr_ref, br_ref,
                    cnt_in_ref, x1_ref, xn_ref, route_ref, cnt_ref):
    m = (jnp.dot(yc_ref[...], wo_ref[0:CONV_W, :], preferred_element_type=F32)
         + jnp.dot(yr_ref[...], wo_ref[CONV_W:, :], preferred_element_type=F32))
    x1 = x_ref[...] + g1_ref[0, 0] * m
    x1_ref[...] = x1
    xn = (_rms(x1) * ng_ref[...]) * (1.0 + sc_ref[0, 0]) + sh_ref[0, 0]
    xn_ref[...] = xn
    logits = jnp.dot(xn.astype(BF16), wr_ref[...], preferred_element_type=F32) + br_ref[...]
    lane, lane_f, i1, i2, w1, w2 = _route(logits)

    @pl.when(pl.program_id(0) == 0)
    def _():
        cnt_ref[...] = cnt_in_ref[...]

    picks = jnp.where((lane_f == i1) | (lane_f == i2), 1.0, 0.0)
    rows = picks.shape[0]
    tri = (lax.broadcasted_iota(jnp.int32, (rows, rows), 0)
           > lax.broadcasted_iota(jnp.int32, (rows, rows), 1))
    before = jnp.dot(jnp.where(tri, 1.0, 0.0).astype(BF16), picks.astype(BF16),
                     preferred_element_type=F32) + cnt_ref[...]
    r1 = jnp.sum(jnp.where(lane_f == i1, before, 0.0), axis=1, keepdims=True)
    r2 = jnp.sum(jnp.where(lane_f == i2, before, 0.0), axis=1, keepdims=True)
    cnt_ref[...] += jnp.sum(picks, axis=0, keepdims=True)
    route = jnp.where(lane == 0, i1 - N_GROUPS, jnp.where(lane == 1, i2 - N_GROUPS, jnp.where(
        lane == 2, r1, jnp.where(lane == 3, r2, jnp.where(lane == 4, w1, jnp.where(lane == 5, w2, 0.0))))))
    route_ref[...] = route


def _outproj(y_conv, y_ret, x, mod4, mod_row_of_tile, norm_g, w_out_bf, w_router_bf, b_router,
             cnt_in):
    tokens = x.shape[0]
    tiles = tokens // TOKEN_TILE

    def mod_spec(which):
        return pl.BlockSpec((1, 1, 1, D_MODEL), lambda i: (mod_row_of_tile(i), which, 0, 0))

    tok = lambda w: pl.BlockSpec((TOKEN_TILE, w), lambda i: (i, 0))
    full = lambda a: pl.BlockSpec(a.shape, lambda i: (0,) * a.ndim)
    return pl.pallas_call(
        _outproj_kernel,
        grid=(tiles,),
        in_specs=[tok(CONV_W), tok(RET_W), tok(D_MODEL), mod_spec(2), mod_spec(3), mod_spec(4),
                  full(norm_g), full(w_out_bf), full(w_router_bf), full(b_router), full(cnt_in)],
        out_specs=[tok(D_MODEL), tok(D_MODEL), tok(ROUTER_COLS), full(cnt_in)],
        out_shape=[jax.ShapeDtypeStruct((tokens, D_MODEL), F32),
                   jax.ShapeDtypeStruct((tokens, D_MODEL), F32),
                   jax.ShapeDtypeStruct((tokens, ROUTER_COLS), F32),
                   jax.ShapeDtypeStruct(cnt_in.shape, F32)],
        compiler_params=pltpu.CompilerParams(
            dimension_semantics=("arbitrary",), vmem_limit_bytes=VMEM_LIMIT),
        name="outproj",
    )(y_conv, y_ret, x, mod4, mod4, mod4, norm_g, w_out_bf, w_router_bf, b_router, cnt_in)


def _row_copy(src, src_row, dst, dst_row, sem):
    return pltpu.make_async_copy(src.at[pl.ds(src_row, 1)], dst.at[pl.ds(dst_row, 1)], sem)


def _rows_wait(src, dst, rows, sem):
    pltpu.make_async_copy(src.at[pl.ds(0, rows)], dst.at[pl.ds(0, rows)], sem).wait()


def _dispatch_kernel(ctx_tiles, pad_lo_ref, pad_hi_ref, pos_ref, xn_ctx_hbm, xn_lat_hbm, zero_hbm,
                     xs_hbm, sem):
    i = pl.program_id(0)

    def issue_from(xn_hbm, base):
        def issue(r, carry):
            for k in range(2):
                _row_copy(xn_hbm, base + r, xs_hbm, pos_ref[0, 0, 2 * r + k], sem).start()
            return carry
        lax.fori_loop(0, TOKEN_TILE, issue, 0, unroll=8)

    @pl.when(i < ctx_tiles)
    def _():
        issue_from(xn_ctx_hbm, i * TOKEN_TILE)

    @pl.when(i >= ctx_tiles)
    def _():
        issue_from(xn_lat_hbm, (i - ctx_tiles) * TOKEN_TILE)

    @pl.when(i > 0)
    def _():
        _rows_wait(xn_lat_hbm, xs_hbm, 2 * TOKEN_TILE, sem)

    @pl.when(i == pl.num_programs(0) - 1)
    def _():
        _rows_wait(xn_lat_hbm, xs_hbm, 2 * TOKEN_TILE, sem)
        for e in range(N_EXPERTS):
            lo, hi = pad_lo_ref[e], pad_hi_ref[e]
            lax.fori_loop(lo, hi, lambda r, c: (_row_copy(zero_hbm, 0, xs_hbm, r, sem).start(), c)[1], 0)
            lax.fori_loop(lo, hi, lambda r, c: (_row_copy(zero_hbm, 0, xs_hbm, r, sem).wait(), c)[1], 0)


def _dispatch(xn_ctx, xn_lat, pos_tiles, pad_lo, pad_hi, slots):
    ctx_tiles = xn_ctx.shape[0] // TOKEN_TILE
    tiles = ctx_tiles + xn_lat.shape[0] // TOKEN_TILE
    grid_spec = pltpu.PrefetchScalarGridSpec(
        num_scalar_prefetch=2,
        grid=(tiles,),
        in_specs=[
            pl.BlockSpec((1, 1, 2 * TOKEN_TILE), lambda i, lo, hi: (i, 0, 0), memory_space=pltpu.SMEM),
            pl.BlockSpec(memory_space=pl.ANY),
            pl.BlockSpec(memory_space=pl.ANY),
            pl.BlockSpec(memory_space=pl.ANY),
        ],
        out_specs=pl.BlockSpec(memory_space=pl.ANY),
        scratch_shapes=[pltpu.SemaphoreType.DMA],
    )
    return pl.pallas_call(
        functools.partial(_dispatch_kernel, ctx_tiles),
        grid_spec=grid_spec,
        out_shape=jax.ShapeDtypeStruct((slots, D_MODEL), F32),
        compiler_params=pltpu.CompilerParams(dimension_semantics=("arbitrary",)),
        name="dispatch",
    )(pad_lo, pad_hi, pos_tiles, xn_ctx, xn_lat, jnp.zeros((1, D_MODEL), F32))


def _expert_kernel(tile_expert_ref, n_used_ref, xs_ref, w1_ref, w3_ref, w2_ref, ys_ref):
    @pl.when(pl.program_id(0) < n_used_ref[0])
    def _():
        x = xs_ref[...]
        hid = _silu(_bdot(x, w1_ref[0])) * _bdot(x, w3_ref[0])
        ys_ref[...] = _bdot(hid, w2_ref[0])


def _experts(xs, tile_expert, n_used, w1, w3, w2):
    slots = xs.shape[0]
    row_tile = lambda j, te, nu: (jnp.minimum(j, nu[0] - 1), 0)
    grid_spec = pltpu.PrefetchScalarGridSpec(
        num_scalar_prefetch=2,
        grid=(slots // EXPERT_TILE,),
        in_specs=[
            pl.BlockSpec((EXPERT_TILE, D_MODEL), row_tile),
            pl.BlockSpec((1, D_MODEL, D_EXPERT), lambda j, te, nu: (te[j], 0, 0)),
            pl.BlockSpec((1, D_MODEL, D_EXPERT), lambda j, te, nu: (te[j], 0, 0)),
            pl.BlockSpec((1, D_EXPERT, D_MODEL), lambda j, te, nu: (te[j], 0, 0)),
        ],
        out_specs=pl.BlockSpec((EXPERT_TILE, D_MODEL), row_tile),
    )
    return pl.pallas_call(
        _expert_kernel,
        grid_spec=grid_spec,
        out_shape=jax.ShapeDtypeStruct((slots, D_MODEL), F32),
        compiler_params=pltpu.CompilerParams(
            dimension_semantics=("arbitrary",), vmem_limit_bytes=VMEM_LIMIT),
        name="experts",
    )(tile_expert, n_used, xs, w1, w3, w2)


def _combine_kernel(n_tiles, pos_ref, pos_next_ref, x1_ref, route_ref, g2_ref, fg_ref, ys_hbm,
                    o_ref, buf, sem):
    i = pl.program_id(0)
    slot = i % 2

    def issue(p_ref, s):
        def body(r, carry):
            for k in range(2):
                pltpu.make_async_copy(ys_hbm.at[pl.ds(p_ref[0, 0, 2 * r + k], 1)],
                                      buf.at[s, k, pl.ds(r, 1)], sem.at[s]).start()
            return carry
        lax.fori_loop(0, TOKEN_TILE, body, 0, unroll=8)

    @pl.when(i == 0)
    def _():
        issue(pos_ref, 0)

    @pl.when(i + 1 < n_tiles)
    def _():
        issue(pos_next_ref, 1 - slot)

    for k in range(2):
        pltpu.make_async_copy(ys_hbm.at[pl.ds(0, TOKEN_TILE)], buf.at[slot, k], sem.at[slot]).wait()
    route = route_ref[...]
    lane = lax.broadcasted_iota(jnp.int32, route.shape, 1)
    w1 = jnp.sum(jnp.where(lane == 4, route, 0.0), axis=1, keepdims=True)
    w2 = jnp.sum(jnp.where(lane == 5, route, 0.0), axis=1, keepdims=True)
    y = x1_ref[...] + g2_ref[0, 0] * (w1 * buf[slot, 0] + w2 * buf[slot, 1])
    o_ref[...] = _rms(y) * fg_ref[...]


def _combine(ys, pos_tiles, x1, route, mod4, mod_row_of_tile, final_g, tile_base):
    tokens = x1.shape[0]
    tiles = tokens // TOKEN_TILE
    last = tile_base + tiles - 1
    pos_spec = lambda nxt: pl.BlockSpec(
        (1, 1, 2 * TOKEN_TILE), lambda i: (jnp.minimum(i + tile_base + nxt, last), 0, 0),
        memory_space=pltpu.SMEM)
    return pl.pallas_call(
        functools.partial(_combine_kernel, tiles),
        grid=(tiles,),
        in_specs=[
            pos_spec(0), pos_spec(1),
            pl.BlockSpec((TOKEN_TILE, D_MODEL), lambda i: (i, 0)),
            pl.BlockSpec((TOKEN_TILE, ROUTER_COLS), lambda i: (i, 0)),
            pl.BlockSpec((1, 1, 1, D_MODEL), lambda i: (mod_row_of_tile(i), 5, 0, 0)),
            pl.BlockSpec((1, D_MODEL), lambda i: (0, 0)),
            pl.BlockSpec(memory_space=pl.ANY),
        ],
        out_specs=pl.BlockSpec((TOKEN_TILE, D_MODEL), lambda i: (i, 0)),
        out_shape=jax.ShapeDtypeStruct((tokens, D_MODEL), F32),
        scratch_shapes=[pltpu.VMEM((2, 2, TOKEN_TILE, D_MODEL), F32), pltpu.SemaphoreType.DMA((2,))],
        compiler_params=pltpu.CompilerParams(
            dimension_semantics=("arbitrary",), vmem_limit_bytes=VMEM_LIMIT),
        name="combine",
    )(pos_tiles, pos_tiles, x1, route, mod4, final_g, ys)


def _routing_tables(routes, counts):
    counts = counts.astype(jnp.int32)
    padded = ((counts + EXPERT_TILE - 1) // EXPERT_TILE) * EXPERT_TILE
    ends = jnp.cumsum(padded)
    offs = ends - padded
    picks = jnp.concatenate([r[:, 0:4] for r in routes], axis=0).astype(jnp.int32)
    experts, ranks = picks[:, 0:2], picks[:, 2:4]
    onehot = experts[:, :, None] == jnp.arange(N_EXPERTS, dtype=jnp.int32)
    pos = jnp.sum(jnp.where(onehot, offs, 0), axis=-1) + ranks
    pos_tiles = pos.reshape(-1, 1, 2 * TOKEN_TILE)
    n_used = ends[-1] // EXPERT_TILE
    max_tiles = (2 * picks.shape[0]) // EXPERT_TILE + N_EXPERTS
    tile_ids = jnp.minimum(jnp.arange(max_tiles, dtype=jnp.int32), n_used - 1)
    tile_expert = jnp.sum(tile_ids[:, None] * EXPERT_TILE >= ends[None, :], axis=1).astype(jnp.int32)
    return pos_tiles, offs + counts, ends, tile_expert, n_used.reshape(1), max_tiles * EXPERT_TILE


def _mixer(x, mod4, mod_row, is_grid, s_f0, s_b0, p):
    norm_mix_g, w_in_bf, conv_w, decay_rows = p
    y_conv, q, k, v, g = _inproj(x, mod4, mod_row, norm_mix_g, w_in_bf, conv_w, is_grid)
    ret = _retention(q, k, v, g, decay_rows, s_f0, s_b0, emit_state=not is_grid)
    flat = lambda a: a.reshape(-1, a.shape[-1])
    return flat(y_conv), flat(ret[0]), ret[1:]


def kernel(x_prompt, x_sample, state_ret_fwd, state_ret_bwd, c, c_ctx, norm_mix_g, norm_ffn_g, w_ada, b_ada, w_in, conv_w, ret_decay_fwd, ret_decay_bwd, w_out, w_router_group, b_router_group, w_router_expert, b_router_expert, w_gate_e, w_up_e, w_down_e, final_norm_g):
    assert norm_mix_g.shape[0] == 1, "single-layer backbone"
    n_lat = c.shape[0]
    ctx_row = n_lat
    mod_rows = 8
    cvec = jnp.concatenate([c, c_ctx[None, :], jnp.zeros((mod_rows - n_lat - 1, D_MODEL), F32)], axis=0)
    mod = _modulation(cvec, w_ada[0], b_ada[0][None, :])
    mod4 = mod.reshape(mod_rows, 6, 1, D_MODEL)

    pad = ROUTER_COLS - N_GROUPS - N_EXPERTS
    w_router = jnp.concatenate(
        [w_router_group[0], w_router_expert[0], jnp.zeros((D_MODEL, pad), F32)], axis=1).astype(BF16)
    b_router = jnp.concatenate([b_router_group[0], b_router_expert[0], jnp.zeros((pad,), F32)])[None, :]
    decay_rows = jnp.broadcast_to(
        jnp.concatenate([ret_decay_fwd[0], ret_decay_bwd[0]])[:, None], (2 * RET_HEADS, LANES))
    p_mix = (norm_mix_g, w_in[0].astype(BF16), conv_w[0], decay_rows)
    w_out_bf = w_out[0].astype(BF16)
    final_g = final_norm_g[None, :]

    ctx_tiles = (x_prompt.shape[0] * x_prompt.shape[1]) // TOKEN_TILE
    lat_tiles_per_batch = x_sample.shape[1] // TOKEN_TILE
    ctx_mod = lambda i: ctx_row
    lat_mod = lambda i: i // lat_tiles_per_batch
    flat = lambda a: a.reshape(-1, a.shape[-1])

    yc_c, yr_c, (s_f, s_b) = _mixer(x_prompt, mod4, lambda b: ctx_row, False, None, None, p_mix)
    yc_l, yr_l, _ = _mixer(x_sample, mod4, lambda b: b, True, state_ret_fwd, state_ret_bwd, p_mix)

    cnt0 = jnp.zeros((1, ROUTER_COLS), F32)
    x1_c, xn_c, route_c, cnt_c = _outproj(
        yc_c, yr_c, flat(x_prompt), mod4, ctx_mod, norm_ffn_g, w_out_bf, w_router, b_router, cnt0)
    x1_l, xn_l, route_l, cnt = _outproj(
        yc_l, yr_l, flat(x_sample), mod4, lat_mod, norm_ffn_g, w_out_bf, w_router, b_router, cnt_c)

    pos_tiles, pad_lo, pad_hi, tile_expert, n_used, slots = _routing_tables(
        (route_c, route_l), cnt[0, N_GROUPS:N_GROUPS + N_EXPERTS])
    xs = _dispatch(xn_c, xn_l, pos_tiles, pad_lo, pad_hi, slots)
    ys = _experts(xs, tile_expert, n_used, w_gate_e[0], w_up_e[0], w_down_e[0])
    y_prompt = _combine(ys, pos_tiles, x1_c, route_c, mod4, ctx_mod, final_g, 0)
    y_sample = _combine(ys, pos_tiles, x1_l, route_l, mod4, lat_mod, final_g, ctx_tiles)
    return (y_prompt.reshape(x_prompt.shape), y_sample.reshape(x_sample.shape),
            s_f.astype(x_prompt.dtype), s_b.astype(x_prompt.dtype))
```

```python
import functools
import math

import jax
import jax.numpy as jnp
from jax import lax
from jax.experimental import pallas as pl
from jax.experimental.pallas import tpu as pltpu

F32 = jnp.float32
BF16 = jnp.bfloat16

D_MODEL = 1024
GRID_W = 64
CONV_W = 512
RET_HEADS = 4
RET_DK = 128
RET_DV = 128
RET_W = RET_HEADS * RET_DV
QK_W = RET_HEADS * RET_DK
CHUNK = 128
N_GROUPS = 4
EXPERTS_PER_GROUP = 8
N_EXPERTS = N_GROUPS * EXPERTS_PER_GROUP
D_EXPERT = 256
ROPE_BASE = 10000.0
EPS = 1e-6

LANES = 128
TOKEN_TILE = 256
EXPERT_TILE = 256
MOD_COLS = 1536
ROUTER_COLS = LANES
VMEM_LIMIT = 48 * 1024 * 1024


def _silu(x):
    return x * jax.nn.sigmoid(x)


def _rms(x):
    return x * lax.rsqrt(jnp.mean(x * x, axis=-1, keepdims=True) + EPS)


def _bdot(a, b):
    return jnp.dot(a.astype(BF16), b.astype(BF16), preferred_element_type=F32)


def _mod_kernel(c_ref, w_ref, b_ref, o_ref):
    o_ref[...] = _bdot(_silu(c_ref[...]), w_ref[...]) + b_ref[...]


def _modulation(cvec, w_ada, b_ada):
    rows = cvec.shape[0]
    n = w_ada.shape[1]
    return pl.pallas_call(
        _mod_kernel,
        grid=(n // MOD_COLS,),
        in_specs=[
            pl.BlockSpec((rows, D_MODEL), lambda j: (0, 0)),
            pl.BlockSpec((D_MODEL, MOD_COLS), lambda j: (0, j)),
            pl.BlockSpec((1, MOD_COLS), lambda j: (0, j)),
        ],
        out_specs=pl.BlockSpec((rows, MOD_COLS), lambda j: (0, j)),
        out_shape=jax.ShapeDtypeStruct((rows, n), F32),
        compiler_params=pltpu.CompilerParams(vmem_limit_bytes=VMEM_LIMIT),
        name="modulation",
    )(cvec, w_ada, b_ada)


def _inproj_kernel(seg, is_grid, x_ref, sh_ref, sc_ref, ng_ref, w_ref, cw_ref, *rest):
    if is_grid:
        cos_ref, sa_ref, sb_ref, yc_ref, q_ref, k_ref, v_ref, g_ref = rest
    else:
        yc_ref, q_ref, k_ref, v_ref, g_ref = rest
    x = x_ref[0]
    xn = (_rms(x) * ng_ref[...]) * (1.0 + sc_ref[0, 0]) + sh_ref[0, 0]
    xb = xn.astype(BF16)

    def proj(c0, n):
        return jnp.dot(xb, w_ref[:, c0:c0 + n], preferred_element_type=F32)

    gate_b = proj(0, CONV_W)
    u = proj(CONV_W, CONV_W) * proj(2 * CONV_W, CONV_W)
    rows = u.shape[0]
    pos = lax.broadcasted_iota(jnp.int32, u.shape, 0) & (seg - 1)
    u_prev = jnp.where(pos != 0, pltpu.roll(u, 1, 0), 0.0)
    u_next = jnp.where(pos != seg - 1, pltpu.roll(u, rows - 1, 0), 0.0)
    conv = cw_ref[0:1, :] * u_prev + cw_ref[1:2, :] * u + cw_ref[2:3, :] * u_next
    yc_ref[0] = (gate_b * conv).astype(yc_ref.dtype)

    q0 = 3 * CONV_W
    q = proj(q0, QK_W)
    k = proj(q0 + QK_W, QK_W)
    if is_grid:
        cos, sa, sb = cos_ref[...], sa_ref[...], sb_ref[...]

        def rope(t):
            out = []
            for h in range(RET_HEADS):
                th = t[:, h * RET_DK:(h + 1) * RET_DK]
                out.append(th * cos + pltpu.roll(th, RET_DK - 1, 1) * sa + pltpu.roll(th, 1, 1) * sb)
            return jnp.concatenate(out, axis=1)

        q, k = rope(q), rope(k)
    q_ref[0] = q
    k_ref[0] = k
    v_ref[0] = proj(q0 + 2 * QK_W, RET_W)
    g_ref[0] = proj(q0 + 2 * QK_W + RET_W, RET_W)


def _rope_tables(length):
    pos = jnp.arange(length)
    row = (pos // GRID_W).astype(F32)
    col = (pos % GRID_W).astype(F32)
    n_pairs = RET_DK // 4
    freqs = ROPE_BASE ** (-(jnp.arange(n_pairs, dtype=F32) * 2.0 / (RET_DK // 2)))
    ang = jnp.concatenate([row[:, None] * freqs, col[:, None] * freqs], axis=-1)
    cos = jnp.repeat(jnp.cos(ang), 2, axis=-1)
    sin = jnp.repeat(jnp.sin(ang), 2, axis=-1)
    even = (jnp.arange(RET_DK) % 2) == 0
    return cos, jnp.where(even, -sin, 0.0), jnp.where(even, 0.0, sin)


def _inproj(x, mod4, mod_row, norm_g, w_in_bf, conv_w, is_grid):
    bsz, length, _ = x.shape
    seg = GRID_W if is_grid else length
    assert TOKEN_TILE % seg == 0 and length % TOKEN_TILE == 0
    tiles = length // TOKEN_TILE

    def mod_spec(which):
        return pl.BlockSpec((1, 1, 1, D_MODEL), lambda b, i: (mod_row(b), which, 0, 0))

    def tok_spec(width):
        return pl.BlockSpec((1, TOKEN_TILE, width), lambda b, i: (b, i, 0))

    in_specs = [
        tok_spec(D_MODEL), mod_spec(0), mod_spec(1),
        pl.BlockSpec((1, D_MODEL), lambda b, i: (0, 0)),
        pl.BlockSpec(w_in_bf.shape, lambda b, i: (0, 0)),
        pl.BlockSpec(conv_w.shape, lambda b, i: (0, 0)),
    ]
    args = [x, mod4, mod4, norm_g, w_in_bf, conv_w]
    if is_grid:
        in_specs += [pl.BlockSpec((TOKEN_TILE, RET_DK), lambda b, i: (i, 0))] * 3
        args += list(_rope_tables(length))
    shp = lambda w, dt: jax.ShapeDtypeStruct((bsz, length, w), dt)
    return pl.pallas_call(
        functools.partial(_inproj_kernel, seg, is_grid),
        grid=(bsz, tiles),
        in_specs=in_specs,
        out_specs=[tok_spec(CONV_W), tok_spec(QK_W), tok_spec(QK_W), tok_spec(RET_W), tok_spec(RET_W)],
        out_shape=[shp(CONV_W, BF16), shp(QK_W, F32), shp(QK_W, F32), shp(RET_W, F32), shp(RET_W, F32)],
        compiler_params=pltpu.CompilerParams(
            dimension_semantics=("parallel", "parallel"), vmem_limit_bytes=VMEM_LIMIT),
        name="inproj_grid" if is_grid else "inproj_seq",
    )(*args)


def _ret_kernel(n_chunks, has_init, emit_state, a_ref, q_ref, k_ref, v_ref, g_ref, *rest):
    rest = list(rest)
    if has_init:
        sf0_ref, sb0_ref = rest[:2]
        rest = rest[2:]
    y_ref = rest.pop(0)
    if emit_state:
        sf_out, sb_out = rest[:2]
        rest = rest[2:]
    sb_in, sf_scr, sb_scr, dec = rest
    h = pl.program_id(1)
    c = CHUNK
    sq = (c, c)
    lg_f = jnp.log1p(-jnp.exp(a_ref[pl.ds(h, 1), :]))
    lg_b = jnp.log1p(-jnp.exp(a_ref[pl.ds(h + RET_HEADS, 1), :]))
    row = lax.broadcasted_iota(jnp.int32, sq, 0).astype(F32)
    col = lax.broadcasted_iota(jnp.int32, sq, 1).astype(F32)
    scale = RET_DK ** -0.5
    dec[0] = scale * (jnp.where(row >= col, jnp.exp(jnp.where(row >= col, row - col, 0.0) * lg_f), 0.0)
                      + jnp.where(col >= row, jnp.exp(jnp.where(col >= row, col - row, 0.0) * lg_b), 0.0))
    dec[1] = jnp.exp((row + 1.0) * lg_f)
    dec[2] = jnp.exp((c - row) * lg_b)
    dec[3] = scale * jnp.exp((c - 1.0 - row) * lg_f)
    dec[4] = scale * jnp.exp(row * lg_b)
    cd_f = jnp.exp(c * lg_f)
    cd_b = jnp.exp(c * lg_b)

    def chunk(ref, n):
        return ref[0, pl.ds(pl.multiple_of(n * c, c), c), :]

    def kv_update(n, s_ref, kd, cd):
        kt = jnp.transpose(chunk(k_ref, n) * kd)
        s_ref[...] = s_ref[...] * cd + _bdot(kt, chunk(v_ref, n))

    if has_init:
        sf_scr[...] = sf0_ref[0, 0, 0]
        sb_scr[...] = sb0_ref[0, 0, 0]
    else:
        sf_scr[...] = jnp.zeros(sq, F32)
        sb_scr[...] = jnp.zeros(sq, F32)

    def bwd_step(i, carry):
        n = n_chunks - 1 - i
        sb_in[n] = sb_scr[...]
        kv_update(n, sb_scr, dec[4], cd_b)
        return carry

    def fwd_step(n, carry):
        q = chunk(q_ref, n)
        scores = lax.dot_general(q.astype(BF16), chunk(k_ref, n).astype(BF16),
                                 (((1,), (1,)), ((), ())), preferred_element_type=F32)
        o = _bdot(scores * dec[0], chunk(v_ref, n))
        o = o + _bdot(q * dec[1], sf_scr[...]) + _bdot(q * dec[2], sb_in[n])
        o = _rms(o)
        y = _silu(chunk(g_ref, n)) * o
        y_ref[0, pl.ds(pl.multiple_of(n * c, c), c), :] = y.astype(y_ref.dtype)
        kv_update(n, sf_scr, dec[3], cd_f)
        return carry

    if n_chunks <= 2:
        for i in range(n_chunks):
            bwd_step(i, 0)
        for n in range(n_chunks):
            fwd_step(n, 0)
    else:
        lax.fori_loop(0, n_chunks, bwd_step, 0)
        lax.fori_loop(0, n_chunks, fwd_step, 0)
    if emit_state:
        sf_out[0, 0, 0] = sf_scr[...]
        sb_out[0, 0, 0] = sb_scr[...]


def _retention(q, k, v, g, decay_rows, s_f0, s_b0, emit_state):
    bsz, length, _ = q.shape
    n_chunks = length // CHUNK
    has_init = s_f0 is not None
    head_spec = pl.BlockSpec((1, length, RET_DK), lambda b, h: (b, 0, h))
    st_spec = pl.BlockSpec((1, 1, 1, RET_DK, RET_DV), lambda b, h: (b, 0, h, 0, 0))
    in_specs = [pl.BlockSpec(decay_rows.shape, lambda b, h: (0, 0))] + [head_spec] * 4
    args = [decay_rows, q, k, v, g]
    if has_init:
        in_specs += [st_spec, st_spec]
        args += [s_f0, s_b0]
    out_specs = [head_spec]
    out_shape = [jax.ShapeDtypeStruct((bsz, length, RET_W), BF16)]
    if emit_state:
        st_shape = jax.ShapeDtypeStruct((bsz, 1, RET_HEADS, RET_DK, RET_DV), F32)
        out_specs += [st_spec, st_spec]
        out_shape += [st_shape, st_shape]
    return pl.pallas_call(
        functools.partial(_ret_kernel, n_chunks, has_init, emit_state),
        grid=(bsz, RET_HEADS),
        in_specs=in_specs,
        out_specs=out_specs,
        out_shape=out_shape,
        scratch_shapes=[
            pltpu.VMEM((n_chunks, RET_DK, RET_DV), F32),
            pltpu.VMEM((RET_DK, RET_DV), F32),
            pltpu.VMEM((RET_DK, RET_DV), F32),
            pltpu.VMEM((5, CHUNK, CHUNK), F32),
        ],
        compiler_params=pltpu.CompilerParams(
            dimension_semantics=("parallel", "parallel"), vmem_limit_bytes=VMEM_LIMIT),
        name="retention_init" if has_init else "retention_zero",
    )(*args)


def _route(logits):
    lane = lax.broadcasted_iota(jnp.int32, logits.shape, 1)
    lane_f = lane.astype(F32)
    neg = -jnp.inf
    far = float(LANES)
    is_g = lane < N_GROUPS
    lg = jnp.where(is_g, logits, neg)
    g_max = jnp.max(lg, axis=1, keepdims=True)
    g_idx = jnp.min(jnp.where(lg == g_max, lane_f, far), axis=1, keepdims=True)
    p_sel = 1.0 / jnp.sum(jnp.where(is_g, jnp.exp(lg - g_max), 0.0), axis=1, keepdims=True)
    lane_group = ((lane - N_GROUPS) >> 3).astype(F32)
    sel = (lane >= N_GROUPS) & (lane < N_GROUPS + N_EXPERTS) & (lane_group == g_idx)
    le = jnp.where(sel, logits, neg)
    v1 = jnp.max(le, axis=1, keepdims=True)
    i1 = jnp.min(jnp.where(le == v1, lane_f, far), axis=1, keepdims=True)
    le2 = jnp.where(lane_f == i1, neg, le)
    v2 = jnp.max(le2, axis=1, keepdims=True)
    i2 = jnp.min(jnp.where(le2 == v2, lane_f, far), axis=1, keepdims=True)
    e2 = jnp.exp(v2 - v1)
    w1 = p_sel * (1.0 / (1.0 + e2))
    w2 = p_sel * (e2 / (1.0 + e2))
    return lane, lane_f, i1, i2, w1, w2


def _outproj_kernel(yc_ref, yr_ref, x_ref, g1_ref, sh_ref, sc_ref, ng_ref, wo_ref, wr_ref, br_ref,
                    cnt_in_ref, x1_ref, xn_ref, route_ref, cnt_ref):
    m = (jnp.dot(yc_ref[...], wo_ref[0:CONV_W, :], preferred_element_type=F32)
         + jnp.dot(yr_ref[...], wo_ref[CONV_W:, :], preferred_element_type=F32))
    x1 = x_ref[...] + g1_ref[0, 0] * m
    x1_ref[...] = x1
    xn = (_rms(x1) * ng_ref[...]) * (1.0 + sc_ref[0, 0]) + sh_ref[0, 0]
    xn_ref[...] = xn
    logits = jnp.dot(xn.astype(BF16), wr_ref[...], preferred_element_type=F32) + br_ref[...]
    lane, lane_f, i1, i2, w1, w2 = _route(logits)

    @pl.when(pl.program_id(0) == 0)
    def _():
        cnt_ref[...] = cnt_in_ref[...]

    picks = jnp.where((lane_f == i1) | (lane_f == i2), 1.0, 0.0)
    rows = picks.shape[0]
    tri = (lax.broadcasted_iota(jnp.int32, (rows, rows), 0)
           > lax.broadcasted_iota(jnp.int32, (rows, rows), 1))
    before = jnp.dot(jnp.where(tri, 1.0, 0.0).astype(BF16), picks.astype(BF16),
                     preferred_element_type=F32) + cnt_ref[...]
    r1 = jnp.sum(jnp.where(lane_f == i1, before, 0.0), axis=1, keepdims=True)
    r2 = jnp.sum(jnp.where(lane_f == i2, before, 0.0), axis=1, keepdims=True)
    cnt_ref[...] += jnp.sum(picks, axis=0, keepdims=True)
    route = jnp.where(lane == 0, i1 - N_GROUPS, jnp.where(lane == 1, i2 - N_GROUPS, jnp.where(
        lane == 2, r1, jnp.where(lane == 3, r2, jnp.where(lane == 4, w1, jnp.where(lane == 5, w2, 0.0))))))
    route_ref[...] = route


def _outproj(y_conv, y_ret, x, mod4, mod_row_of_tile, norm_g, w_out_bf, w_router_bf, b_router,
             cnt_in):
    tokens = x.shape[0]
    tiles = tokens // TOKEN_TILE

    def mod_spec(which):
        return pl.BlockSpec((1, 1, 1, D_MODEL), lambda i: (mod_row_of_tile(i), which, 0, 0))

    tok = lambda w: pl.BlockSpec((TOKEN_TILE, w), lambda i: (i, 0))
    full = lambda a: pl.BlockSpec(a.shape, lambda i: (0,) * a.ndim)
    return pl.pallas_call(
        _outproj_kernel,
        grid=(tiles,),
        in_specs=[tok(CONV_W), tok(RET_W), tok(D_MODEL), mod_spec(2), mod_spec(3), mod_spec(4),
                  full(norm_g), full(w_out_bf), full(w_router_bf), full(b_router), full(cnt_in)],
        out_specs=[tok(D_MODEL), tok(D_MODEL), tok(ROUTER_COLS), full(cnt_in)],
        out_shape=[jax.ShapeDtypeStruct((tokens, D_MODEL), F32),
                   jax.ShapeDtypeStruct((tokens, D_MODEL), F32),
                   jax.ShapeDtypeStruct((tokens, ROUTER_COLS), F32),
                   jax.ShapeDtypeStruct(cnt_in.shape, F32)],
        compiler_params=pltpu.CompilerParams(
            dimension_semantics=("arbitrary",), vmem_limit_bytes=VMEM_LIMIT),
        name="outproj",
    )(y_conv, y_ret, x, mod4, mod4, mod4, norm_g, w_out_bf, w_router_bf, b_router, cnt_in)


def _row_copy(src, src_row, dst, dst_row, sem):
    return pltpu.make_async_copy(src.at[pl.ds(src_row, 1)], dst.at[pl.ds(dst_row, 1)], sem)


def _rows_wait(src, dst, rows, sem):
    pltpu.make_async_copy(src.at[pl.ds(0, rows)], dst.at[pl.ds(0, rows)], sem).wait()


def _dispatch_kernel(ctx_tiles, pad_lo_ref, pad_hi_ref, pos_ref, xn_ctx_hbm, xn_lat_hbm,
                     xs_hbm, buf, zero_buf, in_sem, out_sem):
    i = pl.program_id(0)
    last = pl.num_programs(0) - 1
    slot = i % 2

    def tile_load(t, s):
        @pl.when(t < ctx_tiles)
        def _():
            pltpu.make_async_copy(xn_ctx_hbm.at[pl.ds(t * TOKEN_TILE, TOKEN_TILE)], buf.at[s], in_sem.at[s]).start()

        @pl.when(t >= ctx_tiles)
        def _():
            pltpu.make_async_copy(xn_lat_hbm.at[pl.ds((t - ctx_tiles) * TOKEN_TILE, TOKEN_TILE)],
                                  buf.at[s], in_sem.at[s]).start()

    def rows_out_wait(s):
        _rows_wait(buf.at[s], xs_hbm, TOKEN_TILE, out_sem.at[s])
        _rows_wait(buf.at[s], xs_hbm, TOKEN_TILE, out_sem.at[s])

    @pl.when(i == 0)
    def _():
        tile_load(i, 0)

    @pl.when(i > 0)
    def _():
        rows_out_wait(1 - slot)

    @pl.when(i < last)
    def _():
        tile_load(i + 1, 1 - slot)

    pltpu.make_async_copy(xn_lat_hbm.at[pl.ds(0, TOKEN_TILE)], buf.at[slot], in_sem.at[slot]).wait()

    def issue(r, carry):
        for k in range(2):
            _row_copy(buf.at[slot], r, xs_hbm, pos_ref[0, 0, 2 * r + k], out_sem.at[slot]).start()
        return carry

    lax.fori_loop(0, TOKEN_TILE, issue, 0, unroll=8)

    @pl.when(i == last)
    def _():
        rows_out_wait(slot)
        zero_buf[...] = jnp.zeros_like(zero_buf)
        for e in range(N_EXPERTS):
            lo, hi = pad_lo_ref[e], pad_hi_ref[e]
            pad = lambda r: _row_copy(zero_buf, 0, xs_hbm, r, out_sem.at[0])
            lax.fori_loop(lo, hi, lambda r, c: (pad(r).start(), c)[1], 0)
            lax.fori_loop(lo, hi, lambda r, c: (pad(r).wait(), c)[1], 0)


def _dispatch(xn_ctx, xn_lat, pos_tiles, pad_lo, pad_hi, slots):
    ctx_tiles = xn_ctx.shape[0] // TOKEN_TILE
    tiles = ctx_tiles + xn_lat.shape[0] // TOKEN_TILE
    grid_spec = pltpu.PrefetchScalarGridSpec(
        num_scalar_prefetch=2,
        grid=(tiles,),
        in_specs=[
            pl.BlockSpec((1, 1, 2 * TOKEN_TILE), lambda i, lo, hi: (i, 0, 0), memory_space=pltpu.SMEM),
            pl.BlockSpec(memory_space=pl.ANY),
            pl.BlockSpec(memory_space=pl.ANY),
        ],
        out_specs=pl.BlockSpec(memory_space=pl.ANY),
        scratch_shapes=[pltpu.VMEM((2, TOKEN_TILE, D_MODEL), F32), pltpu.VMEM((8, D_MODEL), F32),
                        pltpu.SemaphoreType.DMA((2,)), pltpu.SemaphoreType.DMA((2,))],
    )
    return pl.pallas_call(
        functools.partial(_dispatch_kernel, ctx_tiles),
        grid_spec=grid_spec,
        out_shape=jax.ShapeDtypeStruct((slots, D_MODEL), F32),
        compiler_params=pltpu.CompilerParams(dimension_semantics=("arbitrary",)),
        name="dispatch",
    )(pad_lo, pad_hi, pos_tiles, xn_ctx, xn_lat)


def _expert_kernel(tile_expert_ref, n_used_ref, xs_ref, w1_ref, w3_ref, w2_ref, ys_ref):
    @pl.when(pl.program_id(0) < n_used_ref[0])
    def _():
        x = xs_ref[...]
        hid = _silu(_bdot(x, w1_ref[0])) * _bdot(x, w3_ref[0])
        ys_ref[...] = _bdot(hid, w2_ref[0])


def _experts(xs, tile_expert, n_used, w1, w3, w2):
    slots = xs.shape[0]
    row_tile = lambda j, te, nu: (jnp.minimum(j, nu[0] - 1), 0)
    grid_spec = pltpu.PrefetchScalarGridSpec(
        num_scalar_prefetch=2,
        grid=(slots // EXPERT_TILE,),
        in_specs=[
            pl.BlockSpec((EXPERT_TILE, D_MODEL), row_tile),
            pl.BlockSpec((1, D_MODEL, D_EXPERT), lambda j, te, nu: (te[j], 0, 0)),
            pl.BlockSpec((1, D_MODEL, D_EXPERT), lambda j, te, nu: (te[j], 0, 0)),
            pl.BlockSpec((1, D_EXPERT, D_MODEL), lambda j, te, nu: (te[j], 0, 0)),
        ],
        out_specs=pl.BlockSpec((EXPERT_TILE, D_MODEL), row_tile),
    )
    return pl.pallas_call(
        _expert_kernel,
        grid_spec=grid_spec,
        out_shape=jax.ShapeDtypeStruct((slots, D_MODEL), F32),
        compiler_params=pltpu.CompilerParams(
            dimension_semantics=("arbitrary",), vmem_limit_bytes=VMEM_LIMIT),
        name="experts",
    )(tile_expert, n_used, xs, w1, w3, w2)


def _combine_kernel(n_tiles, pos_ref, pos_next_ref, x1_ref, route_ref, g2_ref, fg_ref, ys_hbm,
                    o_ref, buf, sem):
    i = pl.program_id(0)
    slot = i % 2

    def issue(p_ref, s):
        def body(r, carry):
            for k in range(2):
                pltpu.make_async_copy(ys_hbm.at[pl.ds(p_ref[0, 0, 2 * r + k], 1)],
                                      buf.at[s, k, pl.ds(r, 1)], sem.at[s]).start()
            return carry
        lax.fori_loop(0, TOKEN_TILE, body, 0, unroll=8)

    @pl.when(i == 0)
    def _():
        issue(pos_ref, 0)

    @pl.when(i + 1 < n_tiles)
    def _():
        issue(pos_next_ref, 1 - slot)

    for k in range(2):
        pltpu.make_async_copy(ys_hbm.at[pl.ds(0, TOKEN_TILE)], buf.at[slot, k], sem.at[slot]).wait()
    route = route_ref[...]
    lane = lax.broadcasted_iota(jnp.int32, route.shape, 1)
    w1 = jnp.sum(jnp.where(lane == 4, route, 0.0), axis=1, keepdims=True)
    w2 = jnp.sum(jnp.where(lane == 5, route, 0.0), axis=1, keepdims=True)
    y = x1_ref[...] + g2_ref[0, 0] * (w1 * buf[slot, 0] + w2 * buf[slot, 1])
    o_ref[...] = _rms(y) * fg_ref[...]


def _combine(ys, pos_tiles, x1, route, mod4, mod_row_of_tile, final_g, tile_base):
    tokens = x1.shape[0]
    tiles = tokens // TOKEN_TILE
    last = tile_base + tiles - 1
    pos_spec = lambda nxt: pl.BlockSpec(
        (1, 1, 2 * TOKEN_TILE), lambda i: (jnp.minimum(i + tile_base + nxt, last), 0, 0),
        memory_space=pltpu.SMEM)
    return pl.pallas_call(
        functools.partial(_combine_kernel, tiles),
        grid=(tiles,),
        in_specs=[
            pos_spec(0), pos_spec(1),
            pl.BlockSpec((TOKEN_TILE, D_MODEL), lambda i: (i, 0)),
            pl.BlockSpec((TOKEN_TILE, ROUTER_COLS), lambda i: (i, 0)),
            pl.BlockSpec((1, 1, 1, D_MODEL), lambda i: (mod_row_of_tile(i), 5, 0, 0)),
            pl.BlockSpec((1, D_MODEL), lambda i: (0, 0)),
            pl.BlockSpec(memory_space=pl.ANY),
        ],
        out_specs=pl.BlockSpec((TOKEN_TILE, D_MODEL), lambda i: (i, 0)),
        out_shape=jax.ShapeDtypeStruct((tokens, D_MODEL), F32),
        scratch_shapes=[pltpu.VMEM((2, 2, TOKEN_TILE, D_MODEL), F32), pltpu.SemaphoreType.DMA((2,))],
        compiler_params=pltpu.CompilerParams(
            dimension_semantics=("arbitrary",), vmem_limit_bytes=VMEM_LIMIT),
        name="combine",
    )(pos_tiles, pos_tiles, x1, route, mod4, final_g, ys)


def _routing_tables(routes, counts):
    counts = counts.astype(jnp.int32)
    padded = ((counts + EXPERT_TILE - 1) // EXPERT_TILE) * EXPERT_TILE
    ends = jnp.cumsum(padded)
    offs = ends - padded
    picks = jnp.concatenate([r[:, 0:4] for r in routes], axis=0).astype(jnp.int32)
    experts, ranks = picks[:, 0:2], picks[:, 2:4]
    onehot = experts[:, :, None] == jnp.arange(N_EXPERTS, dtype=jnp.int32)
    pos = jnp.sum(jnp.where(onehot, offs, 0), axis=-1) + ranks
    pos_tiles = pos.reshape(-1, 1, 2 * TOKEN_TILE)
    n_used = ends[-1] // EXPERT_TILE
    max_tiles = (2 * picks.shape[0]) // EXPERT_TILE + N_EXPERTS
    tile_ids = jnp.minimum(jnp.arange(max_tiles, dtype=jnp.int32), n_used - 1)
    tile_expert = jnp.sum(tile_ids[:, None] * EXPERT_TILE >= ends[None, :], axis=1).astype(jnp.int32)
    return pos_tiles, offs + counts, ends, tile_expert, n_used.reshape(1), max_tiles * EXPERT_TILE


def _mixer(x, mod4, mod_row, is_grid, s_f0, s_b0, p):
    norm_mix_g, w_in_bf, conv_w, decay_rows = p
    y_conv, q, k, v, g = _inproj(x, mod4, mod_row, norm_mix_g, w_in_bf, conv_w, is_grid)
    ret = _retention(q, k, v, g, decay_rows, s_f0, s_b0, emit_state=not is_grid)
    flat = lambda a: a.reshape(-1, a.shape[-1])
    return flat(y_conv), flat(ret[0]), ret[1:]


def kernel(x_prompt, x_sample, state_ret_fwd, state_ret_bwd, c, c_ctx, norm_mix_g, norm_ffn_g, w_ada, b_ada, w_in, conv_w, ret_decay_fwd, ret_decay_bwd, w_out, w_router_group, b_router_group, w_router_expert, b_router_expert, w_gate_e, w_up_e, w_down_e, final_norm_g):
    assert norm_mix_g.shape[0] == 1, "single-layer backbone"
    n_lat = c.shape[0]
    ctx_row = n_lat
    mod_rows = 8
    cvec = jnp.concatenate([c, c_ctx[None, :], jnp.zeros((mod_rows - n_lat - 1, D_MODEL), F32)], axis=0)
    mod = _modulation(cvec, w_ada[0], b_ada[0][None, :])
    mod4 = mod.reshape(mod_rows, 6, 1, D_MODEL)

    pad = ROUTER_COLS - N_GROUPS - N_EXPERTS
    w_router = jnp.concatenate(
        [w_router_group[0], w_router_expert[0], jnp.zeros((D_MODEL, pad), F32)], axis=1).astype(BF16)
    b_router = jnp.concatenate([b_router_group[0], b_router_expert[0], jnp.zeros((pad,), F32)])[None, :]
    decay_rows = jnp.broadcast_to(
        jnp.concatenate([ret_decay_fwd[0], ret_decay_bwd[0]])[:, None], (2 * RET_HEADS, LANES))
    p_mix = (norm_mix_g, w_in[0].astype(BF16), conv_w[0], decay_rows)
    w_out_bf = w_out[0].astype(BF16)
    final_g = final_norm_g[None, :]

    ctx_tiles = (x_prompt.shape[0] * x_prompt.shape[1]) // TOKEN_TILE
    lat_tiles_per_batch = x_sample.shape[1] // TOKEN_TILE
    ctx_mod = lambda i: ctx_row
    lat_mod = lambda i: i // lat_tiles_per_batch
    flat = lambda a: a.reshape(-1, a.shape[-1])

    yc_c, yr_c, (s_f, s_b) = _mixer(x_prompt, mod4, lambda b: ctx_row, False, None, None, p_mix)
    yc_l, yr_l, _ = _mixer(x_sample, mod4, lambda b: b, True, state_ret_fwd, state_ret_bwd, p_mix)

    cnt0 = jnp.zeros((1, ROUTER_COLS), F32)
    x1_c, xn_c, route_c, cnt_c = _outproj(
        yc_c, yr_c, flat(x_prompt), mod4, ctx_mod, norm_ffn_g, w_out_bf, w_router, b_router, cnt0)
    x1_l, xn_l, route_l, cnt = _outproj(
        yc_l, yr_l, flat(x_sample), mod4, lat_mod, norm_ffn_g, w_out_bf, w_router, b_router, cnt_c)

    pos_tiles, pad_lo, pad_hi, tile_expert, n_used, slots = _routing_tables(
        (route_c, route_l), cnt[0, N_GROUPS:N_GROUPS + N_EXPERTS])
    xs = _dispatch(xn_c, xn_l, pos_tiles, pad_lo, pad_hi, slots)
    ys = _experts(xs, tile_expert, n_used, w_gate_e[0], w_up_e[0], w_down_e[0])
    y_prompt = _combine(ys, pos_tiles, x1_c, route_c, mod4, ctx_mod, final_g, 0)
    y_sample = _combine(ys, pos_tiles, x1_l, route_l, mod4, lat_mod, final_g, ctx_tiles)
    return (y_prompt.reshape(x_prompt.shape), y_sample.reshape(x_sample.shape),
            s_f.astype(x_prompt.dtype), s_b.astype(x_prompt.dtype))
```

```python
import functools
import math

import jax
import jax.numpy as jnp
from jax import lax
from jax.experimental import pallas as pl
from jax.experimental.pallas import tpu as pltpu

F32 = jnp.float32
BF16 = jnp.bfloat16

D_MODEL = 1024
GRID_W = 64
CONV_W = 512
RET_HEADS = 4
RET_DK = 128
RET_DV = 128
RET_W = RET_HEADS * RET_DV
QK_W = RET_HEADS * RET_DK
CHUNK = 128
N_GROUPS = 4
EXPERTS_PER_GROUP = 8
N_EXPERTS = N_GROUPS * EXPERTS_PER_GROUP
D_EXPERT = 256
ROPE_BASE = 10000.0
EPS = 1e-6

LANES = 128
TOKEN_TILE = 256
EXPERT_TILE = 256
MOD_COLS = 1536
ROUTER_COLS = LANES
VMEM_LIMIT = 48 * 1024 * 1024


def _silu(x):
    return x * jax.nn.sigmoid(x)


def _rms(x):
    return x * lax.rsqrt(jnp.mean(x * x, axis=-1, keepdims=True) + EPS)


def _bdot(a, b):
    return jnp.dot(a.astype(BF16), b.astype(BF16), preferred_element_type=F32)


def _mod_kernel(c_ref, w_ref, b_ref, o_ref):
    o_ref[...] = _bdot(_silu(c_ref[...]), w_ref[...]) + b_ref[...]


def _modulation(cvec, w_ada, b_ada):
    rows = cvec.shape[0]
    n = w_ada.shape[1]
    return pl.pallas_call(
        _mod_kernel,
        grid=(n // MOD_COLS,),
        in_specs=[
            pl.BlockSpec((rows, D_MODEL), lambda j: (0, 0)),
            pl.BlockSpec((D_MODEL, MOD_COLS), lambda j: (0, j)),
            pl.BlockSpec((1, MOD_COLS), lambda j: (0, j)),
        ],
        out_specs=pl.BlockSpec((rows, MOD_COLS), lambda j: (0, j)),
        out_shape=jax.ShapeDtypeStruct((rows, n), F32),
        compiler_params=pltpu.CompilerParams(vmem_limit_bytes=VMEM_LIMIT),
        name="modulation",
    )(cvec, w_ada, b_ada)


def _inproj_kernel(seg, is_grid, x_ref, sh_ref, sc_ref, ng_ref, w_ref, cw_ref, *rest):
    if is_grid:
        cos_ref, sa_ref, sb_ref, yc_ref, q_ref, k_ref, v_ref, g_ref = rest
    else:
        yc_ref, q_ref, k_ref, v_ref, g_ref = rest
    x = x_ref[0]
    xn = (_rms(x) * ng_ref[...]) * (1.0 + sc_ref[0, 0]) + sh_ref[0, 0]
    xb = xn.astype(BF16)

    def proj(c0, n):
        return jnp.dot(xb, w_ref[:, c0:c0 + n], preferred_element_type=F32)

    gate_b = proj(0, CONV_W)
    u = proj(CONV_W, CONV_W) * proj(2 * CONV_W, CONV_W)
    rows = u.shape[0]
    pos = lax.broadcasted_iota(jnp.int32, u.shape, 0) & (seg - 1)
    u_prev = jnp.where(pos != 0, pltpu.roll(u, 1, 0), 0.0)
    u_next = jnp.where(pos != seg - 1, pltpu.roll(u, rows - 1, 0), 0.0)
    conv = cw_ref[0:1, :] * u_prev + cw_ref[1:2, :] * u + cw_ref[2:3, :] * u_next
    yc_ref[0] = (gate_b * conv).astype(yc_ref.dtype)

    q0 = 3 * CONV_W
    q = proj(q0, QK_W)
    k = proj(q0 + QK_W, QK_W)
    if is_grid:
        cos, sa, sb = cos_ref[...], sa_ref[...], sb_ref[...]

        def rope(t):
            out = []
            for h in range(RET_HEADS):
                th = t[:, h * RET_DK:(h + 1) * RET_DK]
                out.append(th * cos + pltpu.roll(th, RET_DK - 1, 1) * sa + pltpu.roll(th, 1, 1) * sb)
            return jnp.concatenate(out, axis=1)

        q, k = rope(q), rope(k)
    q_ref[0] = q
    k_ref[0] = k
    v_ref[0] = proj(q0 + 2 * QK_W, RET_W)
    g_ref[0] = proj(q0 + 2 * QK_W + RET_W, RET_W)


def _rope_tables(length):
    pos = jnp.arange(length)
    row = (pos // GRID_W).astype(F32)
    col = (pos % GRID_W).astype(F32)
    n_pairs = RET_DK // 4
    freqs = ROPE_BASE ** (-(jnp.arange(n_pairs, dtype=F32) * 2.0 / (RET_DK // 2)))
    ang = jnp.concatenate([row[:, None] * freqs, col[:, None] * freqs], axis=-1)
    cos = jnp.repeat(jnp.cos(ang), 2, axis=-1)
    sin = jnp.repeat(jnp.sin(ang), 2, axis=-1)
    even = (jnp.arange(RET_DK) % 2) == 0
    return cos, jnp.where(even, -sin, 0.0), jnp.where(even, 0.0, sin)


def _inproj(x, mod4, mod_row, norm_g, w_in_bf, conv_w, is_grid):
    bsz, length, _ = x.shape
    seg = GRID_W if is_grid else length
    assert TOKEN_TILE % seg == 0 and length % TOKEN_TILE == 0
    tiles = length // TOKEN_TILE

    def mod_spec(which):
        return pl.BlockSpec((1, 1, 1, D_MODEL), lambda b, i: (mod_row(b), which, 0, 0))

    def tok_spec(width):
        return pl.BlockSpec((1, TOKEN_TILE, width), lambda b, i: (b, i, 0))

    in_specs = [
        tok_spec(D_MODEL), mod_spec(0), mod_spec(1),
        pl.BlockSpec((1, D_MODEL), lambda b, i: (0, 0)),
        pl.BlockSpec(w_in_bf.shape, lambda b, i: (0, 0)),
        pl.BlockSpec(conv_w.shape, lambda b, i: (0, 0)),
    ]
    args = [x, mod4, mod4, norm_g, w_in_bf, conv_w]
    if is_grid:
        in_specs += [pl.BlockSpec((TOKEN_TILE, RET_DK), lambda b, i: (i, 0))] * 3
        args += list(_rope_tables(length))
    shp = lambda w, dt: jax.ShapeDtypeStruct((bsz, length, w), dt)
    return pl.pallas_call(
        functools.partial(_inproj_kernel, seg, is_grid),
        grid=(bsz, tiles),
        in_specs=in_specs,
        out_specs=[tok_spec(CONV_W), tok_spec(QK_W), tok_spec(QK_W), tok_spec(RET_W), tok_spec(RET_W)],
        out_shape=[shp(CONV_W, BF16), shp(QK_W, F32), shp(QK_W, F32), shp(RET_W, F32), shp(RET_W, F32)],
        compiler_params=pltpu.CompilerParams(
            dimension_semantics=("parallel", "parallel"), vmem_limit_bytes=VMEM_LIMIT),
        name="inproj_grid" if is_grid else "inproj_seq",
    )(*args)


def _ret_kernel(n_chunks, has_init, emit_state, a_ref, q_ref, k_ref, v_ref, g_ref, *rest):
    rest = list(rest)
    if has_init:
        sf0_ref, sb0_ref = rest[:2]
        rest = rest[2:]
    y_ref = rest.pop(0)
    if emit_state:
        sf_out, sb_out = rest[:2]
        rest = rest[2:]
    st_f, st_b, dec = rest
    h = pl.program_id(0)
    c = CHUNK
    sq = (c, c)
    lg_f = jnp.log1p(-jnp.exp(a_ref[pl.ds(h, 1), :]))
    lg_b = jnp.log1p(-jnp.exp(a_ref[pl.ds(h + RET_HEADS, 1), :]))
    cd_f = jnp.exp(c * lg_f)
    cd_b = jnp.exp(c * lg_b)

    @pl.when(pl.program_id(1) == 0)
    def _():
        row = lax.broadcasted_iota(jnp.int32, sq, 0).astype(F32)
        col = lax.broadcasted_iota(jnp.int32, sq, 1).astype(F32)
        scale = RET_DK ** -0.5
        dec[0] = scale * (jnp.where(row >= col, jnp.exp(jnp.where(row >= col, row - col, 0.0) * lg_f), 0.0)
                          + jnp.where(col >= row, jnp.exp(jnp.where(col >= row, col - row, 0.0) * lg_b), 0.0))
        dec[1] = jnp.exp((row + 1.0) * lg_f)
        dec[2] = jnp.exp((c - row) * lg_b)
        dec[3] = scale * jnp.exp((c - 1.0 - col) * lg_f)
        dec[4] = scale * jnp.exp(col * lg_b)

    def rows(n):
        return pl.ds(pl.multiple_of(n * c, c), c)

    def kv_step(n, carry):
        kt = jnp.transpose(k_ref[0, rows(n), :])
        lhs = jnp.concatenate([kt * dec[3], kt * dec[4]], axis=0)
        kv = _bdot(lhs, v_ref[0, rows(n), :])
        st_f[n] = kv[:RET_DK]
        st_b[n] = kv[RET_DK:]
        return carry

    def scan_f(n, s):
        kv = st_f[n]
        st_f[n] = s
        return s * cd_f + kv

    def scan_b(i, s):
        n = n_chunks - 1 - i
        kv = st_b[n]
        st_b[n] = s
        return s * cd_b + kv

    def out_step(n, carry):
        q = q_ref[0, rows(n), :]
        scores = lax.dot_general(q.astype(BF16), k_ref[0, rows(n), :].astype(BF16),
                                 (((1,), (1,)), ((), ())), preferred_element_type=F32)
        o = _bdot(scores * dec[0], v_ref[0, rows(n), :])
        q_dec = jnp.concatenate([q * dec[1], q * dec[2]], axis=1)
        o = o + _bdot(q_dec, jnp.concatenate([st_f[n], st_b[n]], axis=0))
        y = _silu(g_ref[0, rows(n), :]) * _rms(o)
        y_ref[0, rows(n), :] = y.astype(y_ref.dtype)
        return carry

    unroll = True if n_chunks <= 2 else 8
    lax.fori_loop(0, n_chunks, kv_step, 0, unroll=unroll)
    s_f = sf0_ref[0, 0, 0] if has_init else jnp.zeros(sq, F32)
    s_b = sb0_ref[0, 0, 0] if has_init else jnp.zeros(sq, F32)
    s_f = lax.fori_loop(0, n_chunks, scan_f, s_f, unroll=unroll)
    s_b = lax.fori_loop(0, n_chunks, scan_b, s_b, unroll=unroll)
    lax.fori_loop(0, n_chunks, out_step, 0, unroll=unroll)
    if emit_state:
        sf_out[0, 0, 0] = s_f
        sb_out[0, 0, 0] = s_b


def _retention(q, k, v, g, decay_rows, s_f0, s_b0, emit_state):
    bsz, length, _ = q.shape
    n_chunks = length // CHUNK
    has_init = s_f0 is not None
    head_spec = pl.BlockSpec((1, length, RET_DK), lambda h, b: (b, 0, h))
    st_spec = pl.BlockSpec((1, 1, 1, RET_DK, RET_DV), lambda h, b: (b, 0, h, 0, 0))
    in_specs = [pl.BlockSpec(decay_rows.shape, lambda h, b: (0, 0))] + [head_spec] * 4
    args = [decay_rows, q, k, v, g]
    if has_init:
        in_specs += [st_spec, st_spec]
        args += [s_f0, s_b0]
    out_specs = [head_spec]
    out_shape = [jax.ShapeDtypeStruct((bsz, length, RET_W), BF16)]
    if emit_state:
        st_shape = jax.ShapeDtypeStruct((bsz, 1, RET_HEADS, RET_DK, RET_DV), F32)
        out_specs += [st_spec, st_spec]
        out_shape += [st_shape, st_shape]
    return pl.pallas_call(
        functools.partial(_ret_kernel, n_chunks, has_init, emit_state),
        grid=(RET_HEADS, bsz),
        in_specs=in_specs,
        out_specs=out_specs,
        out_shape=out_shape,
        scratch_shapes=[
            pltpu.VMEM((n_chunks, RET_DK, RET_DV), F32),
            pltpu.VMEM((n_chunks, RET_DK, RET_DV), F32),
            pltpu.VMEM((5, CHUNK, CHUNK), F32),
        ],
        compiler_params=pltpu.CompilerParams(
            dimension_semantics=("arbitrary", "arbitrary"), vmem_limit_bytes=VMEM_LIMIT),
        name="retention_init" if has_init else "retention_zero",
    )(*args)


def _route(logits):
    lane = lax.broadcasted_iota(jnp.int32, logits.shape, 1)
    lane_f = lane.astype(F32)
    neg = -jnp.inf
    far = float(LANES)
    is_g = lane < N_GROUPS
    lg = jnp.where(is_g, logits, neg)
    g_max = jnp.max(lg, axis=1, keepdims=True)
    g_idx = jnp.min(jnp.where(lg == g_max, lane_f, far), axis=1, keepdims=True)
    p_sel = 1.0 / jnp.sum(jnp.where(is_g, jnp.exp(lg - g_max), 0.0), axis=1, keepdims=True)
    lane_group = ((lane - N_GROUPS) >> 3).astype(F32)
    sel = (lane >= N_GROUPS) & (lane < N_GROUPS + N_EXPERTS) & (lane_group == g_idx)
    le = jnp.where(sel, logits, neg)
    v1 = jnp.max(le, axis=1, keepdims=True)
    i1 = jnp.min(jnp.where(le == v1, lane_f, far), axis=1, keepdims=True)
    le2 = jnp.where(lane_f == i1, neg, le)
    v2 = jnp.max(le2, axis=1, keepdims=True)
    i2 = jnp.min(jnp.where(le2 == v2, lane_f, far), axis=1, keepdims=True)
    e2 = jnp.exp(v2 - v1)
    w1 = p_sel * (1.0 / (1.0 + e2))
    w2 = p_sel * (e2 / (1.0 + e2))
    return lane, lane_f, i1, i2, w1, w2


def _outproj_kernel(yc_ref, yr_ref, x_ref, g1_ref, sh_ref, sc_ref, ng_ref, wo_ref, wr_ref, br_ref,
                    cnt_in_ref, x1_ref, xn_ref, route_ref, cnt_ref):
    m = (jnp.dot(yc_ref[...], wo_ref[0:CONV_W, :], preferred_element_type=F32)
         + jnp.dot(yr_ref[...], wo_ref[CONV_W:, :], preferred_element_type=F32))
    x1 = x_ref[...] + g1_ref[0, 0] * m
    x1_ref[...] = x1
    xn = (_rms(x1) * ng_ref[...]) * (1.0 + sc_ref[0, 0]) + sh_ref[0, 0]
    xn_ref[...] = xn
    logits = jnp.dot(xn.astype(BF16), wr_ref[...], preferred_element_type=F32) + br_ref[...]
    lane, lane_f, i1, i2, w1, w2 = _route(logits)

    @pl.when(pl.program_id(0) == 0)
    def _():
        cnt_ref[...] = cnt_in_ref[...]

    picks = jnp.where((lane_f == i1) | (lane_f == i2), 1.0, 0.0)
    rows = picks.shape[0]
    tri = (lax.broadcasted_iota(jnp.int32, (rows, rows), 0)
           > lax.broadcasted_iota(jnp.int32, (rows, rows), 1))
    before = jnp.dot(jnp.where(tri, 1.0, 0.0).astype(BF16), picks.astype(BF16),
                     preferred_element_type=F32) + cnt_ref[...]
    r1 = jnp.sum(jnp.where(lane_f == i1, before, 0.0), axis=1, keepdims=True)
    r2 = jnp.sum(jnp.where(lane_f == i2, before, 0.0), axis=1, keepdims=True)
    cnt_ref[...] += jnp.sum(picks, axis=0, keepdims=True)
    route = jnp.where(lane == 0, i1 - N_GROUPS, jnp.where(lane == 1, i2 - N_GROUPS, jnp.where(
        lane == 2, r1, jnp.where(lane == 3, r2, jnp.where(lane == 4, w1, jnp.where(lane == 5, w2, 0.0))))))
    route_ref[...] = route


def _outproj(y_conv, y_ret, x, mod4, mod_row_of_tile, norm_g, w_out_bf, w_router_bf, b_router,
             cnt_in):
    tokens = x.shape[0]
    tiles = tokens // TOKEN_TILE

    def mod_spec(which):
        return pl.BlockSpec((1, 1, 1, D_MODEL), lambda i: (mod_row_of_tile(i), which, 0, 0))

    tok = lambda w: pl.BlockSpec((TOKEN_TILE, w), lambda i: (i, 0))
    full = lambda a: pl.BlockSpec(a.shape, lambda i: (0,) * a.ndim)
    return pl.pallas_call(
        _outproj_kernel,
        grid=(tiles,),
        in_specs=[tok(CONV_W), tok(RET_W), tok(D_MODEL), mod_spec(2), mod_spec(3), mod_spec(4),
                  full(norm_g), full(w_out_bf), full(w_router_bf), full(b_router), full(cnt_in)],
        out_specs=[tok(D_MODEL), tok(D_MODEL), tok(ROUTER_COLS), full(cnt_in)],
        out_shape=[jax.ShapeDtypeStruct((tokens, D_MODEL), F32),
                   jax.ShapeDtypeStruct((tokens, D_MODEL), F32),
                   jax.ShapeDtypeStruct((tokens, ROUTER_COLS), F32),
                   jax.ShapeDtypeStruct(cnt_in.shape, F32)],
        compiler_params=pltpu.CompilerParams(
            dimension_semantics=("arbitrary",), vmem_limit_bytes=VMEM_LIMIT),
        name="outproj",
    )(y_conv, y_ret, x, mod4, mod4, mod4, norm_g, w_out_bf, w_router_bf, b_router, cnt_in)


def _row_copy(src, src_row, dst, dst_row, sem):
    return pltpu.make_async_copy(src.at[pl.ds(src_row, 1)], dst.at[pl.ds(dst_row, 1)], sem)


def _rows_wait(src, dst, rows, sem):
    pltpu.make_async_copy(src.at[pl.ds(0, rows)], dst.at[pl.ds(0, rows)], sem).wait()


def _dispatch_kernel(ctx_tiles, pad_lo_ref, pad_hi_ref, pos_ref, xn_ctx_hbm, xn_lat_hbm,
                     xs_hbm, buf, zero_buf, in_sem, out_sem):
    i = pl.program_id(0)
    last = pl.num_programs(0) - 1
    slot = i % 2

    def tile_load(t, s):
        @pl.when(t < ctx_tiles)
        def _():
            pltpu.make_async_copy(xn_ctx_hbm.at[pl.ds(t * TOKEN_TILE, TOKEN_TILE)], buf.at[s], in_sem.at[s]).start()

        @pl.when(t >= ctx_tiles)
        def _():
            pltpu.make_async_copy(xn_lat_hbm.at[pl.ds((t - ctx_tiles) * TOKEN_TILE, TOKEN_TILE)],
                                  buf.at[s], in_sem.at[s]).start()

    def rows_out_wait(s):
        _rows_wait(buf.at[s], xs_hbm, TOKEN_TILE, out_sem.at[s])
        _rows_wait(buf.at[s], xs_hbm, TOKEN_TILE, out_sem.at[s])

    @pl.when(i == 0)
    def _():
        tile_load(i, 0)

    @pl.when(i > 0)
    def _():
        rows_out_wait(1 - slot)

    @pl.when(i < last)
    def _():
        tile_load(i + 1, 1 - slot)

    pltpu.make_async_copy(xn_lat_hbm.at[pl.ds(0, TOKEN_TILE)], buf.at[slot], in_sem.at[slot]).wait()

    def issue(r, carry):
        for k in range(2):
            _row_copy(buf.at[slot], r, xs_hbm, pos_ref[0, 0, 2 * r + k], out_sem.at[slot]).start()
        return carry

    lax.fori_loop(0, TOKEN_TILE, issue, 0, unroll=8)

    @pl.when(i == last)
    def _():
        rows_out_wait(slot)
        zero_buf[...] = jnp.zeros_like(zero_buf)
        for e in range(N_EXPERTS):
            lo, hi = pad_lo_ref[e], pad_hi_ref[e]
            pad = lambda r: _row_copy(zero_buf, 0, xs_hbm, r, out_sem.at[0])
            lax.fori_loop(lo, hi, lambda r, c: (pad(r).start(), c)[1], 0)
            lax.fori_loop(lo, hi, lambda r, c: (pad(r).wait(), c)[1], 0)


def _dispatch(xn_ctx, xn_lat, pos_tiles, pad_lo, pad_hi, slots):
    ctx_tiles = xn_ctx.shape[0] // TOKEN_TILE
    tiles = ctx_tiles + xn_lat.shape[0] // TOKEN_TILE
    grid_spec = pltpu.PrefetchScalarGridSpec(
        num_scalar_prefetch=2,
        grid=(tiles,),
        in_specs=[
            pl.BlockSpec((1, 1, 2 * TOKEN_TILE), lambda i, lo, hi: (i, 0, 0), memory_space=pltpu.SMEM),
            pl.BlockSpec(memory_space=pl.ANY),
            pl.BlockSpec(memory_space=pl.ANY),
        ],
        out_specs=pl.BlockSpec(memory_space=pl.ANY),
        scratch_shapes=[pltpu.VMEM((2, TOKEN_TILE, D_MODEL), F32), pltpu.VMEM((8, D_MODEL), F32),
                        pltpu.SemaphoreType.DMA((2,)), pltpu.SemaphoreType.DMA((2,))],
    )
    return pl.pallas_call(
        functools.partial(_dispatch_kernel, ctx_tiles),
        grid_spec=grid_spec,
        out_shape=jax.ShapeDtypeStruct((slots, D_MODEL), F32),
        compiler_params=pltpu.CompilerParams(dimension_semantics=("arbitrary",)),
        name="dispatch",
    )(pad_lo, pad_hi, pos_tiles, xn_ctx, xn_lat)


def _expert_kernel(tile_expert_ref, n_used_ref, xs_ref, w1_ref, w3_ref, w2_ref, ys_ref):
    @pl.when(pl.program_id(0) < n_used_ref[0])
    def _():
        x = xs_ref[...]
        hid = _silu(_bdot(x, w1_ref[0])) * _bdot(x, w3_ref[0])
        ys_ref[...] = _bdot(hid, w2_ref[0])


def _experts(xs, tile_expert, n_used, w1, w3, w2):
    slots = xs.shape[0]
    row_tile = lambda j, te, nu: (jnp.minimum(j, nu[0] - 1), 0)
    grid_spec = pltpu.PrefetchScalarGridSpec(
        num_scalar_prefetch=2,
        grid=(slots // EXPERT_TILE,),
        in_specs=[
            pl.BlockSpec((EXPERT_TILE, D_MODEL), row_tile),
            pl.BlockSpec((1, D_MODEL, D_EXPERT), lambda j, te, nu: (te[j], 0, 0)),
            pl.BlockSpec((1, D_MODEL, D_EXPERT), lambda j, te, nu: (te[j], 0, 0)),
            pl.BlockSpec((1, D_EXPERT, D_MODEL), lambda j, te, nu: (te[j], 0, 0)),
        ],
        out_specs=pl.BlockSpec((EXPERT_TILE, D_MODEL), row_tile),
    )
    return pl.pallas_call(
        _expert_kernel,
        grid_spec=grid_spec,
        out_shape=jax.ShapeDtypeStruct((slots, D_MODEL), F32),
        compiler_params=pltpu.CompilerParams(
            dimension_semantics=("arbitrary",), vmem_limit_bytes=VMEM_LIMIT),
        name="experts",
    )(tile_expert, n_used, xs, w1, w3, w2)


def _combine_kernel(n_tiles, pos_ref, pos_next_ref, x1_ref, route_ref, g2_ref, fg_ref, ys_hbm,
                    o_ref, buf, sem):
    i = pl.program_id(0)
    slot = i % 2

    def issue(p_ref, s):
        def body(r, carry):
            for k in range(2):
                pltpu.make_async_copy(ys_hbm.at[pl.ds(p_ref[0, 0, 2 * r + k], 1)],
                                      buf.at[s, k, pl.ds(r, 1)], sem.at[s]).start()
            return carry
        lax.fori_loop(0, TOKEN_TILE, body, 0, unroll=8)

    @pl.when(i == 0)
    def _():
        issue(pos_ref, 0)

    @pl.when(i + 1 < n_tiles)
    def _():
        issue(pos_next_ref, 1 - slot)

    for k in range(2):
        pltpu.make_async_copy(ys_hbm.at[pl.ds(0, TOKEN_TILE)], buf.at[slot, k], sem.at[slot]).wait()
    route = route_ref[...]
    lane = lax.broadcasted_iota(jnp.int32, route.shape, 1)
    w1 = jnp.sum(jnp.where(lane == 4, route, 0.0), axis=1, keepdims=True)
    w2 = jnp.sum(jnp.where(lane == 5, route, 0.0), axis=1, keepdims=True)
    y = x1_ref[...] + g2_ref[0, 0] * (w1 * buf[slot, 0] + w2 * buf[slot, 1])
    o_ref[...] = _rms(y) * fg_ref[...]


def _combine(ys, pos_tiles, x1, route, mod4, mod_row_of_tile, final_g, tile_base):
    tokens = x1.shape[0]
    tiles = tokens // TOKEN_TILE
    last = tile_base + tiles - 1
    pos_spec = lambda nxt: pl.BlockSpec(
        (1, 1, 2 * TOKEN_TILE), lambda i: (jnp.minimum(i + tile_base + nxt, last), 0, 0),
        memory_space=pltpu.SMEM)
    return pl.pallas_call(
        functools.partial(_combine_kernel, tiles),
        grid=(tiles,),
        in_specs=[
            pos_spec(0), pos_spec(1),
            pl.BlockSpec((TOKEN_TILE, D_MODEL), lambda i: (i, 0)),
            pl.BlockSpec((TOKEN_TILE, ROUTER_COLS), lambda i: (i, 0)),
            pl.BlockSpec((1, 1, 1, D_MODEL), lambda i: (mod_row_of_tile(i), 5, 0, 0)),
            pl.BlockSpec((1, D_MODEL), lambda i: (0, 0)),
            pl.BlockSpec(memory_space=pl.ANY),
        ],
        out_specs=pl.BlockSpec((TOKEN_TILE, D_MODEL), lambda i: (i, 0)),
        out_shape=jax.ShapeDtypeStruct((tokens, D_MODEL), F32),
        scratch_shapes=[pltpu.VMEM((2, 2, TOKEN_TILE, D_MODEL), F32), pltpu.SemaphoreType.DMA((2,))],
        compiler_params=pltpu.CompilerParams(
            dimension_semantics=("arbitrary",), vmem_limit_bytes=VMEM_LIMIT),
        name="combine",
    )(pos_tiles, pos_tiles, x1, route, mod4, final_g, ys)


def _routing_tables(routes, counts):
    counts = counts.astype(jnp.int32)
    padded = ((counts + EXPERT_TILE - 1) // EXPERT_TILE) * EXPERT_TILE
    ends = jnp.cumsum(padded)
    offs = ends - padded
    picks = jnp.concatenate([r[:, 0:4] for r in routes], axis=0).astype(jnp.int32)
    experts, ranks = picks[:, 0:2], picks[:, 2:4]
    onehot = experts[:, :, None] == jnp.arange(N_EXPERTS, dtype=jnp.int32)
    pos = jnp.sum(jnp.where(onehot, offs, 0), axis=-1) + ranks
    pos_tiles = pos.reshape(-1, 1, 2 * TOKEN_TILE)
    n_used = ends[-1] // EXPERT_TILE
    max_tiles = (2 * picks.shape[0]) // EXPERT_TILE + N_EXPERTS
    tile_ids = jnp.minimum(jnp.arange(max_tiles, dtype=jnp.int32), n_used - 1)
    tile_expert = jnp.sum(tile_ids[:, None] * EXPERT_TILE >= ends[None, :], axis=1).astype(jnp.int32)
    return pos_tiles, offs + counts, ends, tile_expert, n_used.reshape(1), max_tiles * EXPERT_TILE


def _mixer(x, mod4, mod_row, is_grid, s_f0, s_b0, p):
    norm_mix_g, w_in_bf, conv_w, decay_rows = p
    y_conv, q, k, v, g = _inproj(x, mod4, mod_row, norm_mix_g, w_in_bf, conv_w, is_grid)
    ret = _retention(q, k, v, g, decay_rows, s_f0, s_b0, emit_state=not is_grid)
    flat = lambda a: a.reshape(-1, a.shape[-1])
    return flat(y_conv), flat(ret[0]), ret[1:]


def kernel(x_prompt, x_sample, state_ret_fwd, state_ret_bwd, c, c_ctx, norm_mix_g, norm_ffn_g, w_ada, b_ada, w_in, conv_w, ret_decay_fwd, ret_decay_bwd, w_out, w_router_group, b_router_group, w_router_expert, b_router_expert, w_gate_e, w_up_e, w_down_e, final_norm_g):
    assert norm_mix_g.shape[0] == 1, "single-layer backbone"
    n_lat = c.shape[0]
    ctx_row = n_lat
    mod_rows = 8
    cvec = jnp.concatenate([c, c_ctx[None, :], jnp.zeros((mod_rows - n_lat - 1, D_MODEL), F32)], axis=0)
    mod = _modulation(cvec, w_ada[0], b_ada[0][None, :])
    mod4 = mod.reshape(mod_rows, 6, 1, D_MODEL)

    pad = ROUTER_COLS - N_GROUPS - N_EXPERTS
    w_router = jnp.concatenate(
        [w_router_group[0], w_router_expert[0], jnp.zeros((D_MODEL, pad), F32)], axis=1).astype(BF16)
    b_router = jnp.concatenate([b_router_group[0], b_router_expert[0], jnp.zeros((pad,), F32)])[None, :]
    decay_rows = jnp.broadcast_to(
        jnp.concatenate([ret_decay_fwd[0], ret_decay_bwd[0]])[:, None], (2 * RET_HEADS, LANES))
    p_mix = (norm_mix_g, w_in[0].astype(BF16), conv_w[0], decay_rows)
    w_out_bf = w_out[0].astype(BF16)
    final_g = final_norm_g[None, :]

    ctx_tiles = (x_prompt.shape[0] * x_prompt.shape[1]) // TOKEN_TILE
    lat_tiles_per_batch = x_sample.shape[1] // TOKEN_TILE
    ctx_mod = lambda i: ctx_row
    lat_mod = lambda i: i // lat_tiles_per_batch
    flat = lambda a: a.reshape(-1, a.shape[-1])

    yc_c, yr_c, (s_f, s_b) = _mixer(x_prompt, mod4, lambda b: ctx_row, False, None, None, p_mix)
    yc_l, yr_l, _ = _mixer(x_sample, mod4, lambda b: b, True, state_ret_fwd, state_ret_bwd, p_mix)

    cnt0 = jnp.zeros((1, ROUTER_COLS), F32)
    x1_c, xn_c, route_c, cnt_c = _outproj(
        yc_c, yr_c, flat(x_prompt), mod4, ctx_mod, norm_ffn_g, w_out_bf, w_router, b_router, cnt0)
    x1_l, xn_l, route_l, cnt = _outproj(
        yc_l, yr_l, flat(x_sample), mod4, lat_mod, norm_ffn_g, w_out_bf, w_router, b_router, cnt_c)

    pos_tiles, pad_lo, pad_hi, tile_expert, n_used, slots = _routing_tables(
        (route_c, route_l), cnt[0, N_GROUPS:N_GROUPS + N_EXPERTS])
    xs = _dispatch(xn_c, xn_l, pos_tiles, pad_lo, pad_hi, slots)
    ys = _experts(xs, tile_expert, n_used, w_gate_e[0], w_up_e[0], w_down_e[0])
    y_prompt = _combine(ys, pos_tiles, x1_c, route_c, mod4, ctx_mod, final_g, 0)
    y_sample = _combine(ys, pos_tiles, x1_l, route_l, mod4, lat_mod, final_g, ctx_tiles)
    return (y_prompt.reshape(x_prompt.shape), y_sample.reshape(x_sample.shape),
            s_f.astype(x_prompt.dtype), s_b.astype(x_prompt.dtype))
```

```python
import functools
import math

import jax
import jax.numpy as jnp
from jax import lax
from jax.experimental import pallas as pl
from jax.experimental.pallas import tpu as pltpu

F32 = jnp.float32
BF16 = jnp.bfloat16

D_MODEL = 1024
GRID_W = 64
CONV_W = 512
RET_HEADS = 4
RET_DK = 128
RET_DV = 128
RET_W = RET_HEADS * RET_DV
QK_W = RET_HEADS * RET_DK
CHUNK = 128
N_GROUPS = 4
EXPERTS_PER_GROUP = 8
N_EXPERTS = N_GROUPS * EXPERTS_PER_GROUP
D_EXPERT = 256
ROPE_BASE = 10000.0
EPS = 1e-6

LANES = 128
TOKEN_TILE = 256
GROUP_TILE = 512
ROW_W = D_MODEL + LANES
ROUTE_GROUP, ROUTE_RANK, ROUTE_E1, ROUTE_E2, ROUTE_W1, ROUTE_W2 = range(6)
MOD_COLS = 1536
ROUTER_COLS = LANES
VMEM_LIMIT = 48 * 1024 * 1024

def _silu(x):
    return x * jax.nn.sigmoid(x)


def _rms(x):
    return x * lax.rsqrt(jnp.mean(x * x, axis=-1, keepdims=True) + EPS)


def _bdot(a, b):
    return jnp.dot(a.astype(BF16), b.astype(BF16), preferred_element_type=F32)


def _mod_kernel(c_ref, w_ref, b_ref, o_ref):
    o_ref[...] = _bdot(_silu(c_ref[...]), w_ref[...]) + b_ref[...]


def _modulation(cvec, w_ada, b_ada):
    rows = cvec.shape[0]
    n = w_ada.shape[1]
    return pl.pallas_call(
        _mod_kernel,
        grid=(n // MOD_COLS,),
        in_specs=[
            pl.BlockSpec((rows, D_MODEL), lambda j: (0, 0)),
            pl.BlockSpec((D_MODEL, MOD_COLS), lambda j: (0, j)),
            pl.BlockSpec((1, MOD_COLS), lambda j: (0, j)),
        ],
        out_specs=pl.BlockSpec((rows, MOD_COLS), lambda j: (0, j)),
        out_shape=jax.ShapeDtypeStruct((rows, n), F32),
        compiler_params=pltpu.CompilerParams(vmem_limit_bytes=VMEM_LIMIT),
        name="modulation",
    )(cvec, w_ada, b_ada)


def _inproj_kernel(seg, is_grid, x_ref, sh_ref, sc_ref, ng_ref, w_ref, cw_ref, *rest):
    if is_grid:
        cos_ref, sa_ref, sb_ref, yc_ref, q_ref, k_ref, v_ref, g_ref = rest
    else:
        yc_ref, q_ref, k_ref, v_ref, g_ref = rest
    x = x_ref[0]
    xn = (_rms(x) * ng_ref[...]) * (1.0 + sc_ref[0, 0]) + sh_ref[0, 0]
    xb = xn.astype(BF16)

    def proj(c0, n):
        return jnp.dot(xb, w_ref[:, c0:c0 + n], preferred_element_type=F32)

    gate_b = proj(0, CONV_W)
    u = proj(CONV_W, CONV_W) * proj(2 * CONV_W, CONV_W)
    rows = u.shape[0]
    pos = lax.broadcasted_iota(jnp.int32, u.shape, 0) & (seg - 1)
    u_prev = jnp.where(pos != 0, pltpu.roll(u, 1, 0), 0.0)
    u_next = jnp.where(pos != seg - 1, pltpu.roll(u, rows - 1, 0), 0.0)
    conv = cw_ref[0:1, :] * u_prev + cw_ref[1:2, :] * u + cw_ref[2:3, :] * u_next
    yc_ref[0] = (gate_b * conv).astype(yc_ref.dtype)

    q0 = 3 * CONV_W
    q = proj(q0, QK_W)
    k = proj(q0 + QK_W, QK_W)
    if is_grid:
        cos, sa, sb = cos_ref[...], sa_ref[...], sb_ref[...]

        def rope(t):
            out = []
            for h in range(RET_HEADS):
                th = t[:, h * RET_DK:(h + 1) * RET_DK]
                out.append(th * cos + pltpu.roll(th, RET_DK - 1, 1) * sa + pltpu.roll(th, 1, 1) * sb)
            return jnp.concatenate(out, axis=1)

        q, k = rope(q), rope(k)
    q_ref[0] = q
    k_ref[0] = k
    v_ref[0] = proj(q0 + 2 * QK_W, RET_W)
    g_ref[0] = proj(q0 + 2 * QK_W + RET_W, RET_W)


def _rope_tables(length):
    pos = jnp.arange(length)
    row = (pos // GRID_W).astype(F32)
    col = (pos % GRID_W).astype(F32)
    n_pairs = RET_DK // 4
    freqs = ROPE_BASE ** (-(jnp.arange(n_pairs, dtype=F32) * 2.0 / (RET_DK // 2)))
    ang = jnp.concatenate([row[:, None] * freqs, col[:, None] * freqs], axis=-1)
    cos = jnp.repeat(jnp.cos(ang), 2, axis=-1)
    sin = jnp.repeat(jnp.sin(ang), 2, axis=-1)
    even = (jnp.arange(RET_DK) % 2) == 0
    return cos, jnp.where(even, -sin, 0.0), jnp.where(even, 0.0, sin)


def _inproj(x, mod4, mod_row, norm_g, w_in_bf, conv_w, is_grid):
    bsz, length, _ = x.shape
    seg = GRID_W if is_grid else length
    assert TOKEN_TILE % seg == 0 and length % TOKEN_TILE == 0
    tiles = length // TOKEN_TILE

    def mod_spec(which):
        return pl.BlockSpec((1, 1, 1, D_MODEL), lambda b, i: (mod_row(b), which, 0, 0))

    def tok_spec(width):
        return pl.BlockSpec((1, TOKEN_TILE, width), lambda b, i: (b, i, 0))

    in_specs = [
        tok_spec(D_MODEL), mod_spec(0), mod_spec(1),
        pl.BlockSpec((1, D_MODEL), lambda b, i: (0, 0)),
        pl.BlockSpec(w_in_bf.shape, lambda b, i: (0, 0)),
        pl.BlockSpec(conv_w.shape, lambda b, i: (0, 0)),
    ]
    args = [x, mod4, mod4, norm_g, w_in_bf, conv_w]
    if is_grid:
        in_specs += [pl.BlockSpec((TOKEN_TILE, RET_DK), lambda b, i: (i, 0))] * 3
        args += list(_rope_tables(length))
    shp = lambda w, dt: jax.ShapeDtypeStruct((bsz, length, w), dt)
    return pl.pallas_call(
        functools.partial(_inproj_kernel, seg, is_grid),
        grid=(bsz, tiles),
        in_specs=in_specs,
        out_specs=[tok_spec(CONV_W), tok_spec(QK_W), tok_spec(QK_W), tok_spec(RET_W), tok_spec(RET_W)],
        out_shape=[shp(CONV_W, BF16), shp(QK_W, F32), shp(QK_W, F32), shp(RET_W, F32), shp(RET_W, F32)],
        compiler_params=pltpu.CompilerParams(
            dimension_semantics=("parallel", "parallel"), vmem_limit_bytes=VMEM_LIMIT),
        name="inproj_grid" if is_grid else "inproj_seq",
    )(*args)


def _ret_kernel(n_chunks, has_init, emit_state, a_ref, q_ref, k_ref, v_ref, g_ref, *rest):
    rest = list(rest)
    if has_init:
        sf0_ref, sb0_ref = rest[:2]
        rest = rest[2:]
    y_ref = rest.pop(0)
    if emit_state:
        sf_out, sb_out = rest[:2]
        rest = rest[2:]
    st_f, st_b, dec = rest
    h = pl.program_id(0)
    c = CHUNK
    sq = (c, c)
    lg_f = jnp.log1p(-jnp.exp(a_ref[pl.ds(h, 1), :]))
    lg_b = jnp.log1p(-jnp.exp(a_ref[pl.ds(h + RET_HEADS, 1), :]))
    cd_f = jnp.exp(c * lg_f)
    cd_b = jnp.exp(c * lg_b)

    @pl.when(pl.program_id(1) == 0)
    def _():
        row = lax.broadcasted_iota(jnp.int32, sq, 0).astype(F32)
        col = lax.broadcasted_iota(jnp.int32, sq, 1).astype(F32)
        scale = RET_DK ** -0.5
        dec[0] = scale * (jnp.where(row >= col, jnp.exp(jnp.where(row >= col, row - col, 0.0) * lg_f), 0.0)
                          + jnp.where(col >= row, jnp.exp(jnp.where(col >= row, col - row, 0.0) * lg_b), 0.0))
        dec[1] = jnp.exp((row + 1.0) * lg_f)
        dec[2] = jnp.exp((c - row) * lg_b)
        dec[3] = scale * jnp.exp((c - 1.0 - col) * lg_f)
        dec[4] = scale * jnp.exp(col * lg_b)

    def rows(n):
        return pl.ds(pl.multiple_of(n * c, c), c)

    def kv_step(n, carry):
        kt = jnp.transpose(k_ref[0, rows(n), :])
        lhs = jnp.concatenate([kt * dec[3], kt * dec[4]], axis=0)
        kv = _bdot(lhs, v_ref[0, rows(n), :])
        st_f[n] = kv[:RET_DK]
        st_b[n] = kv[RET_DK:]
        return carry

    def scan_f(n, s):
        kv = st_f[n]
        st_f[n] = s
        return s * cd_f + kv

    def scan_b(i, s):
        n = n_chunks - 1 - i
        kv = st_b[n]
        st_b[n] = s
        return s * cd_b + kv

    def out_step(n, carry):
        q = q_ref[0, rows(n), :]
        scores = lax.dot_general(q.astype(BF16), k_ref[0, rows(n), :].astype(BF16),
                                 (((1,), (1,)), ((), ())), preferred_element_type=F32)
        o = _bdot(scores * dec[0], v_ref[0, rows(n), :])
        q_dec = jnp.concatenate([q * dec[1], q * dec[2]], axis=1)
        o = o + _bdot(q_dec, jnp.concatenate([st_f[n], st_b[n]], axis=0))
        y = _silu(g_ref[0, rows(n), :]) * _rms(o)
        y_ref[0, rows(n), :] = y.astype(y_ref.dtype)
        return carry

    unroll = True if n_chunks <= 2 else 8
    lax.fori_loop(0, n_chunks, kv_step, 0, unroll=unroll)
    s_f = sf0_ref[0, 0, 0] if has_init else jnp.zeros(sq, F32)
    s_b = sb0_ref[0, 0, 0] if has_init else jnp.zeros(sq, F32)
    s_f = lax.fori_loop(0, n_chunks, scan_f, s_f, unroll=unroll)
    s_b = lax.fori_loop(0, n_chunks, scan_b, s_b, unroll=unroll)
    lax.fori_loop(0, n_chunks, out_step, 0, unroll=unroll)
    if emit_state:
        sf_out[0, 0, 0] = s_f
        sb_out[0, 0, 0] = s_b


def _retention(q, k, v, g, decay_rows, s_f0, s_b0, emit_state):
    bsz, length, _ = q.shape
    n_chunks = length // CHUNK
    has_init = s_f0 is not None
    head_spec = pl.BlockSpec((1, length, RET_DK), lambda h, b: (b, 0, h))
    st_spec = pl.BlockSpec((1, 1, 1, RET_DK, RET_DV), lambda h, b: (b, 0, h, 0, 0))
    in_specs = [pl.BlockSpec(decay_rows.shape, lambda h, b: (0, 0))] + [head_spec] * 4
    args = [decay_rows, q, k, v, g]
    if has_init:
        in_specs += [st_spec, st_spec]
        args += [s_f0, s_b0]
    out_specs = [head_spec]
    out_shape = [jax.ShapeDtypeStruct((bsz, length, RET_W), BF16)]
    if emit_state:
        st_shape = jax.ShapeDtypeStruct((bsz, 1, RET_HEADS, RET_DK, RET_DV), F32)
        out_specs += [st_spec, st_spec]
        out_shape += [st_shape, st_shape]
    return pl.pallas_call(
        functools.partial(_ret_kernel, n_chunks, has_init, emit_state),
        grid=(RET_HEADS, bsz),
        in_specs=in_specs,
        out_specs=out_specs,
        out_shape=out_shape,
        scratch_shapes=[
            pltpu.VMEM((n_chunks, RET_DK, RET_DV), F32),
            pltpu.VMEM((n_chunks, RET_DK, RET_DV), F32),
            pltpu.VMEM((5, CHUNK, CHUNK), F32),
        ],
        compiler_params=pltpu.CompilerParams(
            dimension_semantics=("arbitrary", "arbitrary"), vmem_limit_bytes=VMEM_LIMIT),
        name="retention_init" if has_init else "retention_zero",
    )(*args)


def _route(logits):
    lane = lax.broadcasted_iota(jnp.int32, logits.shape, 1)
    lane_f = lane.astype(F32)
    neg = -jnp.inf
    far = float(LANES)
    is_g = lane < N_GROUPS
    lg = jnp.where(is_g, logits, neg)
    g_max = jnp.max(lg, axis=1, keepdims=True)
    g_idx = jnp.min(jnp.where(lg == g_max, lane_f, far), axis=1, keepdims=True)
    p_sel = 1.0 / jnp.sum(jnp.where(is_g, jnp.exp(lg - g_max), 0.0), axis=1, keepdims=True)
    lane_group = ((lane - N_GROUPS) >> 3).astype(F32)
    sel = (lane >= N_GROUPS) & (lane < N_GROUPS + N_EXPERTS) & (lane_group == g_idx)
    le = jnp.where(sel, logits, neg)
    v1 = jnp.max(le, axis=1, keepdims=True)
    i1 = jnp.min(jnp.where(le == v1, lane_f, far), axis=1, keepdims=True)
    le2 = jnp.where(lane_f == i1, neg, le)
    v2 = jnp.max(le2, axis=1, keepdims=True)
    i2 = jnp.min(jnp.where(le2 == v2, lane_f, far), axis=1, keepdims=True)
    e2 = jnp.exp(v2 - v1)
    w1 = p_sel * (1.0 / (1.0 + e2))
    w2 = p_sel * (e2 / (1.0 + e2))
    return lane, lane_f, g_idx, i1, i2, w1, w2


def _outproj_kernel(yc_ref, yr_ref, x_ref, g1_ref, sh_ref, sc_ref, ng_ref, wo_ref, wr_ref, br_ref,
                    cnt_in_ref, x1_ref, xr_ref, cnt_ref):
    m = (jnp.dot(yc_ref[...], wo_ref[0:CONV_W, :], preferred_element_type=F32)
         + jnp.dot(yr_ref[...], wo_ref[CONV_W:, :], preferred_element_type=F32))
    x1 = x_ref[...] + g1_ref[0, 0] * m
    x1_ref[...] = x1
    xn = (_rms(x1) * ng_ref[...]) * (1.0 + sc_ref[0, 0]) + sh_ref[0, 0]
    logits = jnp.dot(xn.astype(BF16), wr_ref[...], preferred_element_type=F32) + br_ref[...]
    lane, lane_f, g_idx, i1, i2, w1, w2 = _route(logits)

    @pl.when(pl.program_id(0) == 0)
    def _():
        cnt_ref[...] = cnt_in_ref[...]

    picks = jnp.where(lane_f == g_idx, 1.0, 0.0)
    rows = picks.shape[0]
    tri = (lax.broadcasted_iota(jnp.int32, (rows, rows), 0)
           > lax.broadcasted_iota(jnp.int32, (rows, rows), 1))
    before = jnp.dot(jnp.where(tri, 1.0, 0.0).astype(BF16), picks.astype(BF16),
                     preferred_element_type=F32) + cnt_ref[...]
    rank = jnp.sum(jnp.where(lane_f == g_idx, before, 0.0), axis=1, keepdims=True)
    cnt_ref[...] += jnp.sum(picks, axis=0, keepdims=True)
    route = jnp.where(lane == ROUTE_GROUP, g_idx, jnp.where(lane == ROUTE_RANK, rank, jnp.where(
        lane == ROUTE_E1, i1 - N_GROUPS, jnp.where(lane == ROUTE_E2, i2 - N_GROUPS, jnp.where(
            lane == ROUTE_W1, w1, jnp.where(lane == ROUTE_W2, w2, 0.0))))))
    xr_ref[:, :D_MODEL] = xn
    xr_ref[:, D_MODEL:] = route


def _outproj(y_conv, y_ret, x, mod4, mod_row_of_tile, norm_g, w_out_bf, w_router_bf, b_router,
             cnt_in):
    tokens = x.shape[0]
    tiles = tokens // TOKEN_TILE

    def mod_spec(which):
        return pl.BlockSpec((1, 1, 1, D_MODEL), lambda i: (mod_row_of_tile(i), which, 0, 0))

    tok = lambda w: pl.BlockSpec((TOKEN_TILE, w), lambda i: (i, 0))
    full = lambda a: pl.BlockSpec(a.shape, lambda i: (0,) * a.ndim)
    return pl.pallas_call(
        _outproj_kernel,
        grid=(tiles,),
        in_specs=[tok(CONV_W), tok(RET_W), tok(D_MODEL), mod_spec(2), mod_spec(3), mod_spec(4),
                  full(norm_g), full(w_out_bf), full(w_router_bf), full(b_router), full(cnt_in)],
        out_specs=[tok(D_MODEL), tok(ROW_W), full(cnt_in)],
        out_shape=[jax.ShapeDtypeStruct((tokens, D_MODEL), F32),
                   jax.ShapeDtypeStruct((tokens, ROW_W), F32),
                   jax.ShapeDtypeStruct(cnt_in.shape, F32)],
        compiler_params=pltpu.CompilerParams(
            dimension_semantics=("arbitrary",), vmem_limit_bytes=VMEM_LIMIT),
        name="outproj",
    )(y_conv, y_ret, x, mod4, mod4, mod4, norm_g, w_out_bf, w_router_bf, b_router, cnt_in)


def _row_copy(src, src_row, dst, dst_row, sem):
    return pltpu.make_async_copy(src.at[pl.ds(src_row, 1)], dst.at[pl.ds(dst_row, 1)], sem)


def _rows_wait(src, dst, rows, sem):
    pltpu.make_async_copy(src.at[pl.ds(0, rows)], dst.at[pl.ds(0, rows)], sem).wait()


def _dispatch_kernel(ctx_tiles, pad_lo_ref, pad_hi_ref, pos_ref, xr_ctx_hbm, xr_lat_hbm,
                     xs_hbm, buf, zero_buf, in_sem, out_sem):
    i = pl.program_id(0)
    last = pl.num_programs(0) - 1
    slot = i % 2

    def tile_load(t, s):
        @pl.when(t < ctx_tiles)
        def _():
            pltpu.make_async_copy(xr_ctx_hbm.at[pl.ds(t * TOKEN_TILE, TOKEN_TILE)], buf.at[s], in_sem.at[s]).start()

        @pl.when(t >= ctx_tiles)
        def _():
            pltpu.make_async_copy(xr_lat_hbm.at[pl.ds((t - ctx_tiles) * TOKEN_TILE, TOKEN_TILE)],
                                  buf.at[s], in_sem.at[s]).start()

    @pl.when(i == 0)
    def _():
        tile_load(i, 0)

    @pl.when(i > 0)
    def _():
        _rows_wait(buf.at[1 - slot], xs_hbm, TOKEN_TILE, out_sem.at[1 - slot])

    @pl.when(i < last)
    def _():
        tile_load(i + 1, 1 - slot)

    pltpu.make_async_copy(xr_lat_hbm.at[pl.ds(0, TOKEN_TILE)], buf.at[slot], in_sem.at[slot]).wait()

    def issue(r, carry):
        _row_copy(buf.at[slot], r, xs_hbm, pos_ref[0, 0, r], out_sem.at[slot]).start()
        return carry

    lax.fori_loop(0, TOKEN_TILE, issue, 0, unroll=8)

    @pl.when(i == last)
    def _():
        _rows_wait(buf.at[slot], xs_hbm, TOKEN_TILE, out_sem.at[slot])
        zero_buf[...] = jnp.zeros_like(zero_buf)
        for g in range(N_GROUPS):
            lo, hi = pad_lo_ref[g], pad_hi_ref[g]
            pad = lambda r: _row_copy(zero_buf, 0, xs_hbm, r, out_sem.at[0])
            lax.fori_loop(lo, hi, lambda r, c: (pad(r).start(), c)[1], 0)
            lax.fori_loop(lo, hi, lambda r, c: (pad(r).wait(), c)[1], 0)


def _dispatch(xr_ctx, xr_lat, pos_tiles, pad_lo, pad_hi, slots):
    ctx_tiles = xr_ctx.shape[0] // TOKEN_TILE
    tiles = ctx_tiles + xr_lat.shape[0] // TOKEN_TILE
    grid_spec = pltpu.PrefetchScalarGridSpec(
        num_scalar_prefetch=2,
        grid=(tiles,),
        in_specs=[
            pl.BlockSpec((1, 1, TOKEN_TILE), lambda i, lo, hi: (i, 0, 0), memory_space=pltpu.SMEM),
            pl.BlockSpec(memory_space=pl.ANY),
            pl.BlockSpec(memory_space=pl.ANY),
        ],
        out_specs=pl.BlockSpec(memory_space=pl.ANY),
        scratch_shapes=[pltpu.VMEM((2, TOKEN_TILE, ROW_W), F32), pltpu.VMEM((8, ROW_W), F32),
                        pltpu.SemaphoreType.DMA((2,)), pltpu.SemaphoreType.DMA((2,))],
    )
    return pl.pallas_call(
        functools.partial(_dispatch_kernel, ctx_tiles),
        grid_spec=grid_spec,
        out_shape=jax.ShapeDtypeStruct((slots, ROW_W), F32),
        compiler_params=pltpu.CompilerParams(dimension_semantics=("arbitrary",)),
        name="dispatch",
    )(pad_lo, pad_hi, pos_tiles, xr_ctx, xr_lat)


def _expert_kernel(tile_group_ref, n_used_ref, xs_ref, w1_ref, w3_ref, w2_ref, ys_ref, xb, gate_tabs):
    j = pl.program_id(0)
    ei = pl.program_id(1)

    @pl.when(j < n_used_ref[0])
    def _():
        @pl.when(ei == 0)
        def _():
            xb[...] = xs_ref[:, :D_MODEL].astype(BF16)
            route = xs_ref[:, D_MODEL:]
            lane = lax.broadcasted_iota(jnp.int32, route.shape, 1)
            for n, which in enumerate((ROUTE_E1, ROUTE_E2, ROUTE_W1, ROUTE_W2)):
                col = jnp.sum(jnp.where(lane == which, route, 0.0), axis=1, keepdims=True)
                gate_tabs[n] = jnp.broadcast_to(col, route.shape)

        expert = (tile_group_ref[j] * EXPERTS_PER_GROUP + ei).astype(F32)
        gate = (jnp.where(gate_tabs[0] == expert, gate_tabs[2], 0.0)
                + jnp.where(gate_tabs[1] == expert, gate_tabs[3], 0.0))
        x = xb[...]
        hid = _silu(jnp.dot(x, w1_ref[0].astype(BF16), preferred_element_type=F32)) * jnp.dot(
            x, w3_ref[0].astype(BF16), preferred_element_type=F32)
        y = jnp.dot(hid.astype(BF16), w2_ref[0].astype(BF16), preferred_element_type=F32)
        gated = jnp.concatenate(
            [gate * y[:, c * LANES:(c + 1) * LANES] for c in range(D_MODEL // LANES)], axis=1)

        @pl.when(ei == 0)
        def _():
            ys_ref[...] = gated

        @pl.when(ei > 0)
        def _():
            ys_ref[...] += gated


def _experts(xs, tile_group, n_used, w1, w3, w2):
    slots = xs.shape[0]
    row_tile = lambda j, ei, tg, nu: (jnp.minimum(j, nu[0] - 1), 0)
    last_e = EXPERTS_PER_GROUP - 1
    w_spec = lambda shape: pl.BlockSpec((1,) + shape, lambda j, ei, tg, nu: (
        tg[jnp.minimum(j, nu[0] - 1)] * EXPERTS_PER_GROUP + jnp.where(j < nu[0], ei, last_e), 0, 0))
    grid_spec = pltpu.PrefetchScalarGridSpec(
        num_scalar_prefetch=2,
        grid=(slots // GROUP_TILE, EXPERTS_PER_GROUP),
        in_specs=[
            pl.BlockSpec((GROUP_TILE, ROW_W), row_tile),
            w_spec((D_MODEL, D_EXPERT)), w_spec((D_MODEL, D_EXPERT)), w_spec((D_EXPERT, D_MODEL)),
        ],
        out_specs=pl.BlockSpec((GROUP_TILE, D_MODEL), row_tile),
        scratch_shapes=[pltpu.VMEM((GROUP_TILE, D_MODEL), BF16), pltpu.VMEM((4, GROUP_TILE, LANES), F32)],
    )
    return pl.pallas_call(
        _expert_kernel,
        grid_spec=grid_spec,
        out_shape=jax.ShapeDtypeStruct((slots, D_MODEL), F32),
        compiler_params=pltpu.CompilerParams(
            dimension_semantics=("arbitrary", "arbitrary"), vmem_limit_bytes=VMEM_LIMIT),
        name="experts",
    )(tile_group, n_used, xs, w1, w3, w2)


def _combine_kernel(n_tiles, pos_ref, pos_next_ref, x1_ref, g2_ref, fg_ref, ys_hbm, o_ref, buf, sem):
    i = pl.program_id(0)
    slot = i % 2

    def issue(p_ref, s):
        def body(r, carry):
            _row_copy(ys_hbm, p_ref[0, 0, r], buf.at[s], r, sem.at[s]).start()
            return carry
        lax.fori_loop(0, TOKEN_TILE, body, 0, unroll=8)

    @pl.when(i == 0)
    def _():
        issue(pos_ref, 0)

    @pl.when(i + 1 < n_tiles)
    def _():
        issue(pos_next_ref, 1 - slot)

    _rows_wait(ys_hbm, buf.at[slot], TOKEN_TILE, sem.at[slot])
    y = x1_ref[...] + g2_ref[0, 0] * buf[slot]
    o_ref[...] = _rms(y) * fg_ref[...]


def _combine(ys, pos_tiles, x1, mod4, mod_row_of_tile, final_g, tile_base):
    tokens = x1.shape[0]
    tiles = tokens // TOKEN_TILE
    last = tile_base + tiles - 1
    pos_spec = lambda nxt: pl.BlockSpec(
        (1, 1, TOKEN_TILE), lambda i: (jnp.minimum(i + tile_base + nxt, last), 0, 0),
        memory_space=pltpu.SMEM)
    return pl.pallas_call(
        functools.partial(_combine_kernel, tiles),
        grid=(tiles,),
        in_specs=[
            pos_spec(0), pos_spec(1),
            pl.BlockSpec((TOKEN_TILE, D_MODEL), lambda i: (i, 0)),
            pl.BlockSpec((1, 1, 1, D_MODEL), lambda i: (mod_row_of_tile(i), 5, 0, 0)),
            pl.BlockSpec((1, D_MODEL), lambda i: (0, 0)),
            pl.BlockSpec(memory_space=pl.ANY),
        ],
        out_specs=pl.BlockSpec((TOKEN_TILE, D_MODEL), lambda i: (i, 0)),
        out_shape=jax.ShapeDtypeStruct((tokens, D_MODEL), F32),
        scratch_shapes=[pltpu.VMEM((2, TOKEN_TILE, D_MODEL), F32), pltpu.SemaphoreType.DMA((2,))],
        compiler_params=pltpu.CompilerParams(
            dimension_semantics=("arbitrary",), vmem_limit_bytes=VMEM_LIMIT),
        name="combine",
    )(pos_tiles, pos_tiles, x1, mod4, final_g, ys)


def _routing_tables(rows, counts):
    counts = counts.astype(jnp.int32)
    padded = ((counts + GROUP_TILE - 1) // GROUP_TILE) * GROUP_TILE
    ends = jnp.cumsum(padded)
    offs = ends - padded
    route = jnp.concatenate([r[:, D_MODEL:D_MODEL + 2] for r in rows], axis=0).astype(jnp.int32)
    group, rank = route[:, ROUTE_GROUP], route[:, ROUTE_RANK]
    onehot = group[:, None] == jnp.arange(N_GROUPS, dtype=jnp.int32)
    pos = jnp.sum(jnp.where(onehot, offs, 0), axis=-1) + rank
    pos_tiles = pos.reshape(-1, 1, TOKEN_TILE)
    n_used = ends[-1] // GROUP_TILE
    max_tiles = pos.shape[0] // GROUP_TILE + N_GROUPS
    tile_ids = jnp.minimum(jnp.arange(max_tiles, dtype=jnp.int32), n_used - 1)
    tile_group = jnp.sum(tile_ids[:, None] * GROUP_TILE >= ends[None, :], axis=1).astype(jnp.int32)
    return pos_tiles, offs + counts, ends, tile_group, n_used.reshape(1), max_tiles * GROUP_TILE


def _mixer(x, mod4, mod_row, is_grid, s_f0, s_b0, p):
    norm_mix_g, w_in_bf, conv_w, decay_rows = p
    y_conv, q, k, v, g = _inproj(x, mod4, mod_row, norm_mix_g, w_in_bf, conv_w, is_grid)
    ret = _retention(q, k, v, g, decay_rows, s_f0, s_b0, emit_state=not is_grid)
    flat = lambda a: a.reshape(-1, a.shape[-1])
    return flat(y_conv), flat(ret[0]), ret[1:]


def kernel(x_prompt, x_sample, state_ret_fwd, state_ret_bwd, c, c_ctx, norm_mix_g, norm_ffn_g, w_ada, b_ada, w_in, conv_w, ret_decay_fwd, ret_decay_bwd, w_out, w_router_group, b_router_group, w_router_expert, b_router_expert, w_gate_e, w_up_e, w_down_e, final_norm_g):
    assert norm_mix_g.shape[0] == 1, "single-layer backbone"
    n_lat = c.shape[0]
    ctx_row = n_lat
    mod_rows = 8
    cvec = jnp.concatenate([c, c_ctx[None, :], jnp.zeros((mod_rows - n_lat - 1, D_MODEL), F32)], axis=0)
    mod = _modulation(cvec, w_ada[0], b_ada[0][None, :])
    mod4 = mod.reshape(mod_rows, 6, 1, D_MODEL)

    pad = ROUTER_COLS - N_GROUPS - N_EXPERTS
    w_router = jnp.concatenate(
        [w_router_group[0], w_router_expert[0], jnp.zeros((D_MODEL, pad), F32)], axis=1).astype(BF16)
    b_router = jnp.concatenate([b_router_group[0], b_router_expert[0], jnp.zeros((pad,), F32)])[None, :]
    decay_rows = jnp.broadcast_to(
        jnp.concatenate([ret_decay_fwd[0], ret_decay_bwd[0]])[:, None], (2 * RET_HEADS, LANES))
    p_mix = (norm_mix_g, w_in[0].astype(BF16), conv_w[0], decay_rows)
    w_out_bf = w_out[0].astype(BF16)
    final_g = final_norm_g[None, :]

    ctx_tiles = (x_prompt.shape[0] * x_prompt.shape[1]) // TOKEN_TILE
    lat_tiles_per_batch = x_sample.shape[1] // TOKEN_TILE
    ctx_mod = lambda i: ctx_row
    lat_mod = lambda i: i // lat_tiles_per_batch
    flat = lambda a: a.reshape(-1, a.shape[-1])

    yc_c, yr_c, (s_f, s_b) = _mixer(x_prompt, mod4, lambda b: ctx_row, False, None, None, p_mix)
    yc_l, yr_l, _ = _mixer(x_sample, mod4, lambda b: b, True, state_ret_fwd, state_ret_bwd, p_mix)

    cnt0 = jnp.zeros((1, ROUTER_COLS), F32)
    x1_c, xr_c, cnt_c = _outproj(
        yc_c, yr_c, flat(x_prompt), mod4, ctx_mod, norm_ffn_g, w_out_bf, w_router, b_router, cnt0)
    x1_l, xr_l, cnt = _outproj(
        yc_l, yr_l, flat(x_sample), mod4, lat_mod, norm_ffn_g, w_out_bf, w_router, b_router, cnt_c)

    pos_tiles, pad_lo, pad_hi, tile_group, n_used, slots = _routing_tables(
        (xr_c, xr_l), cnt[0, :N_GROUPS])
    xs = _dispatch(xr_c, xr_l, pos_tiles, pad_lo, pad_hi, slots)
    ys = _experts(xs, tile_group, n_used, w_gate_e[0], w_up_e[0], w_down_e[0])
    y_prompt = _combine(ys, pos_tiles, x1_c, mod4, ctx_mod, final_g, 0)
    y_sample = _combine(ys, pos_tiles, x1_l, mod4, lat_mod, final_g, ctx_tiles)
    return (y_prompt.reshape(x_prompt.shape), y_sample.reshape(x_sample.shape),
            s_f.astype(x_prompt.dtype), s_b.astype(x_prompt.dtype))
```

```python
import functools
import math

import jax
import jax.numpy as jnp
from jax import lax
from jax.experimental import pallas as pl
from jax.experimental.pallas import tpu as pltpu

F32 = jnp.float32
BF16 = jnp.bfloat16

D_MODEL = 1024
GRID_W = 64
CONV_W = 512
RET_HEADS = 4
RET_DK = 128
RET_DV = 128
RET_W = RET_HEADS * RET_DV
QK_W = RET_HEADS * RET_DK
CHUNK = 128
N_GROUPS = 4
EXPERTS_PER_GROUP = 8
N_EXPERTS = N_GROUPS * EXPERTS_PER_GROUP
D_EXPERT = 256
ROPE_BASE = 10000.0
EPS = 1e-6

LANES = 128
TOKEN_TILE = 256
GROUP_TILE = 1024
ROW_W = D_MODEL + LANES
ROUTE_GROUP, ROUTE_RANK, ROUTE_E1, ROUTE_E2, ROUTE_W1, ROUTE_W2 = range(6)
MOD_COLS = 1536
ROUTER_COLS = LANES
VMEM_LIMIT = 48 * 1024 * 1024

def _silu(x):
    return x * jax.nn.sigmoid(x)


def _rms(x):
    return x * lax.rsqrt(jnp.mean(x * x, axis=-1, keepdims=True) + EPS)


def _bdot(a, b):
    return jnp.dot(a.astype(BF16), b.astype(BF16), preferred_element_type=F32)


def _mod_kernel(c_ref, w_ref, b_ref, o_ref):
    o_ref[...] = _bdot(_silu(c_ref[...]), w_ref[...]) + b_ref[...]


def _modulation(cvec, w_ada, b_ada):
    rows = cvec.shape[0]
    n = w_ada.shape[1]
    return pl.pallas_call(
        _mod_kernel,
        grid=(n // MOD_COLS,),
        in_specs=[
            pl.BlockSpec((rows, D_MODEL), lambda j: (0, 0)),
            pl.BlockSpec((D_MODEL, MOD_COLS), lambda j: (0, j)),
            pl.BlockSpec((1, MOD_COLS), lambda j: (0, j)),
        ],
        out_specs=pl.BlockSpec((rows, MOD_COLS), lambda j: (0, j)),
        out_shape=jax.ShapeDtypeStruct((rows, n), F32),
        compiler_params=pltpu.CompilerParams(vmem_limit_bytes=VMEM_LIMIT),
        name="modulation",
    )(cvec, w_ada, b_ada)


def _inproj_kernel(seg, is_grid, x_ref, sh_ref, sc_ref, ng_ref, w_ref, cw_ref, *rest):
    if is_grid:
        cos_ref, sa_ref, sb_ref, yc_ref, q_ref, k_ref, v_ref, g_ref = rest
    else:
        yc_ref, q_ref, k_ref, v_ref, g_ref = rest
    x = x_ref[0]
    xn = (_rms(x) * ng_ref[...]) * (1.0 + sc_ref[0, 0]) + sh_ref[0, 0]
    xb = xn.astype(BF16)

    def proj(c0, n):
        return jnp.dot(xb, w_ref[:, c0:c0 + n], preferred_element_type=F32)

    gate_b = proj(0, CONV_W)
    u = proj(CONV_W, CONV_W) * proj(2 * CONV_W, CONV_W)
    rows = u.shape[0]
    pos = lax.broadcasted_iota(jnp.int32, u.shape, 0) & (seg - 1)
    u_prev = jnp.where(pos != 0, pltpu.roll(u, 1, 0), 0.0)
    u_next = jnp.where(pos != seg - 1, pltpu.roll(u, rows - 1, 0), 0.0)
    conv = cw_ref[0:1, :] * u_prev + cw_ref[1:2, :] * u + cw_ref[2:3, :] * u_next
    yc_ref[0] = (gate_b * conv).astype(yc_ref.dtype)

    q0 = 3 * CONV_W
    q = proj(q0, QK_W)
    k = proj(q0 + QK_W, QK_W)
    if is_grid:
        cos, sa, sb = cos_ref[...], sa_ref[...], sb_ref[...]

        def rope(t):
            out = []
            for h in range(RET_HEADS):
                th = t[:, h * RET_DK:(h + 1) * RET_DK]
                out.append(th * cos + pltpu.roll(th, RET_DK - 1, 1) * sa + pltpu.roll(th, 1, 1) * sb)
            return jnp.concatenate(out, axis=1)

        q, k = rope(q), rope(k)
    q_ref[0] = q
    k_ref[0] = k
    v_ref[0] = proj(q0 + 2 * QK_W, RET_W)
    g_ref[0] = proj(q0 + 2 * QK_W + RET_W, RET_W)


def _rope_tables(length):
    pos = jnp.arange(length)
    row = (pos // GRID_W).astype(F32)
    col = (pos % GRID_W).astype(F32)
    n_pairs = RET_DK // 4
    freqs = ROPE_BASE ** (-(jnp.arange(n_pairs, dtype=F32) * 2.0 / (RET_DK // 2)))
    ang = jnp.concatenate([row[:, None] * freqs, col[:, None] * freqs], axis=-1)
    cos = jnp.repeat(jnp.cos(ang), 2, axis=-1)
    sin = jnp.repeat(jnp.sin(ang), 2, axis=-1)
    even = (jnp.arange(RET_DK) % 2) == 0
    return cos, jnp.where(even, -sin, 0.0), jnp.where(even, 0.0, sin)


def _inproj(x, mod4, mod_row, norm_g, w_in_bf, conv_w, is_grid):
    bsz, length, _ = x.shape
    seg = GRID_W if is_grid else length
    assert TOKEN_TILE % seg == 0 and length % TOKEN_TILE == 0
    tiles = length // TOKEN_TILE

    def mod_spec(which):
        return pl.BlockSpec((1, 1, 1, D_MODEL), lambda b, i: (mod_row(b), which, 0, 0))

    def tok_spec(width):
        return pl.BlockSpec((1, TOKEN_TILE, width), lambda b, i: (b, i, 0))

    in_specs = [
        tok_spec(D_MODEL), mod_spec(0), mod_spec(1),
        pl.BlockSpec((1, D_MODEL), lambda b, i: (0, 0)),
        pl.BlockSpec(w_in_bf.shape, lambda b, i: (0, 0)),
        pl.BlockSpec(conv_w.shape, lambda b, i: (0, 0)),
    ]
    args = [x, mod4, mod4, norm_g, w_in_bf, conv_w]
    if is_grid:
        in_specs += [pl.BlockSpec((TOKEN_TILE, RET_DK), lambda b, i: (i, 0))] * 3
        args += list(_rope_tables(length))
    shp = lambda w, dt: jax.ShapeDtypeStruct((bsz, length, w), dt)
    return pl.pallas_call(
        functools.partial(_inproj_kernel, seg, is_grid),
        grid=(bsz, tiles),
        in_specs=in_specs,
        out_specs=[tok_spec(CONV_W), tok_spec(QK_W), tok_spec(QK_W), tok_spec(RET_W), tok_spec(RET_W)],
        out_shape=[shp(CONV_W, BF16), shp(QK_W, F32), shp(QK_W, F32), shp(RET_W, F32), shp(RET_W, F32)],
        compiler_params=pltpu.CompilerParams(
            dimension_semantics=("parallel", "parallel"), vmem_limit_bytes=VMEM_LIMIT),
        name="inproj_grid" if is_grid else "inproj_seq",
    )(*args)


def _ret_kernel(n_chunks, has_init, emit_state, a_ref, q_ref, k_ref, v_ref, g_ref, *rest):
    rest = list(rest)
    if has_init:
        sf0_ref, sb0_ref = rest[:2]
        rest = rest[2:]
    y_ref = rest.pop(0)
    if emit_state:
        sf_out, sb_out = rest[:2]
        rest = rest[2:]
    st_f, st_b, dec = rest
    h = pl.program_id(0)
    c = CHUNK
    sq = (c, c)
    lg_f = jnp.log1p(-jnp.exp(a_ref[pl.ds(h, 1), :]))
    lg_b = jnp.log1p(-jnp.exp(a_ref[pl.ds(h + RET_HEADS, 1), :]))
    cd_f = jnp.exp(c * lg_f)
    cd_b = jnp.exp(c * lg_b)

    @pl.when(pl.program_id(1) == 0)
    def _():
        row = lax.broadcasted_iota(jnp.int32, sq, 0).astype(F32)
        col = lax.broadcasted_iota(jnp.int32, sq, 1).astype(F32)
        scale = RET_DK ** -0.5
        dec[0] = scale * (jnp.where(row >= col, jnp.exp(jnp.where(row >= col, row - col, 0.0) * lg_f), 0.0)
                          + jnp.where(col >= row, jnp.exp(jnp.where(col >= row, col - row, 0.0) * lg_b), 0.0))
        dec[1] = jnp.exp((row + 1.0) * lg_f)
        dec[2] = jnp.exp((c - row) * lg_b)
        dec[3] = scale * jnp.exp((c - 1.0 - col) * lg_f)
        dec[4] = scale * jnp.exp(col * lg_b)

    def rows(n):
        return pl.ds(pl.multiple_of(n * c, c), c)

    def kv_step(n, carry):
        kt = jnp.transpose(k_ref[0, rows(n), :])
        lhs = jnp.concatenate([kt * dec[3], kt * dec[4]], axis=0)
        kv = _bdot(lhs, v_ref[0, rows(n), :])
        st_f[n] = kv[:RET_DK]
        st_b[n] = kv[RET_DK:]
        return carry

    def scan_f(n, s):
        kv = st_f[n]
        st_f[n] = s
        return s * cd_f + kv

    def scan_b(i, s):
        n = n_chunks - 1 - i
        kv = st_b[n]
        st_b[n] = s
        return s * cd_b + kv

    def out_step(n, carry):
        q = q_ref[0, rows(n), :]
        scores = lax.dot_general(q.astype(BF16), k_ref[0, rows(n), :].astype(BF16),
                                 (((1,), (1,)), ((), ())), preferred_element_type=F32)
        o = _bdot(scores * dec[0], v_ref[0, rows(n), :])
        q_dec = jnp.concatenate([q * dec[1], q * dec[2]], axis=1)
        o = o + _bdot(q_dec, jnp.concatenate([st_f[n], st_b[n]], axis=0))
        y = _silu(g_ref[0, rows(n), :]) * _rms(o)
        y_ref[0, rows(n), :] = y.astype(y_ref.dtype)
        return carry

    unroll = True if n_chunks <= 2 else 8
    lax.fori_loop(0, n_chunks, kv_step, 0, unroll=unroll)
    s_f = sf0_ref[0, 0, 0] if has_init else jnp.zeros(sq, F32)
    s_b = sb0_ref[0, 0, 0] if has_init else jnp.zeros(sq, F32)
    s_f = lax.fori_loop(0, n_chunks, scan_f, s_f, unroll=unroll)
    s_b = lax.fori_loop(0, n_chunks, scan_b, s_b, unroll=unroll)
    lax.fori_loop(0, n_chunks, out_step, 0, unroll=unroll)
    if emit_state:
        sf_out[0, 0, 0] = s_f
        sb_out[0, 0, 0] = s_b


def _retention(q, k, v, g, decay_rows, s_f0, s_b0, emit_state):
    bsz, length, _ = q.shape
    n_chunks = length // CHUNK
    has_init = s_f0 is not None
    head_spec = pl.BlockSpec((1, length, RET_DK), lambda h, b: (b, 0, h))
    st_spec = pl.BlockSpec((1, 1, 1, RET_DK, RET_DV), lambda h, b: (b, 0, h, 0, 0))
    in_specs = [pl.BlockSpec(decay_rows.shape, lambda h, b: (0, 0))] + [head_spec] * 4
    args = [decay_rows, q, k, v, g]
    if has_init:
        in_specs += [st_spec, st_spec]
        args += [s_f0, s_b0]
    out_specs = [head_spec]
    out_shape = [jax.ShapeDtypeStruct((bsz, length, RET_W), BF16)]
    if emit_state:
        st_shape = jax.ShapeDtypeStruct((bsz, 1, RET_HEADS, RET_DK, RET_DV), F32)
        out_specs += [st_spec, st_spec]
        out_shape += [st_shape, st_shape]
    return pl.pallas_call(
        functools.partial(_ret_kernel, n_chunks, has_init, emit_state),
        grid=(RET_HEADS, bsz),
        in_specs=in_specs,
        out_specs=out_specs,
        out_shape=out_shape,
        scratch_shapes=[
            pltpu.VMEM((n_chunks, RET_DK, RET_DV), F32),
            pltpu.VMEM((n_chunks, RET_DK, RET_DV), F32),
            pltpu.VMEM((5, CHUNK, CHUNK), F32),
        ],
        compiler_params=pltpu.CompilerParams(
            dimension_semantics=("arbitrary", "arbitrary"), vmem_limit_bytes=VMEM_LIMIT),
        name="retention_init" if has_init else "retention_zero",
    )(*args)


def _route(logits):
    lane = lax.broadcasted_iota(jnp.int32, logits.shape, 1)
    lane_f = lane.astype(F32)
    neg = -jnp.inf
    far = float(LANES)
    is_g = lane < N_GROUPS
    lg = jnp.where(is_g, logits, neg)
    g_max = jnp.max(lg, axis=1, keepdims=True)
    g_idx = jnp.min(jnp.where(lg == g_max, lane_f, far), axis=1, keepdims=True)
    p_sel = 1.0 / jnp.sum(jnp.where(is_g, jnp.exp(lg - g_max), 0.0), axis=1, keepdims=True)
    lane_group = ((lane - N_GROUPS) >> 3).astype(F32)
    sel = (lane >= N_GROUPS) & (lane < N_GROUPS + N_EXPERTS) & (lane_group == g_idx)
    le = jnp.where(sel, logits, neg)
    v1 = jnp.max(le, axis=1, keepdims=True)
    i1 = jnp.min(jnp.where(le == v1, lane_f, far), axis=1, keepdims=True)
    le2 = jnp.where(lane_f == i1, neg, le)
    v2 = jnp.max(le2, axis=1, keepdims=True)
    i2 = jnp.min(jnp.where(le2 == v2, lane_f, far), axis=1, keepdims=True)
    e2 = jnp.exp(v2 - v1)
    w1 = p_sel * (1.0 / (1.0 + e2))
    w2 = p_sel * (e2 / (1.0 + e2))
    return lane, lane_f, g_idx, i1, i2, w1, w2


def _outproj_kernel(yc_ref, yr_ref, x_ref, g1_ref, sh_ref, sc_ref, ng_ref, wo_ref, wr_ref, br_ref,
                    cnt_in_ref, x1_ref, xr_ref, cnt_ref):
    m = (jnp.dot(yc_ref[...], wo_ref[0:CONV_W, :], preferred_element_type=F32)
         + jnp.dot(yr_ref[...], wo_ref[CONV_W:, :], preferred_element_type=F32))
    x1 = x_ref[...] + g1_ref[0, 0] * m
    x1_ref[...] = x1
    xn = (_rms(x1) * ng_ref[...]) * (1.0 + sc_ref[0, 0]) + sh_ref[0, 0]
    logits = jnp.dot(xn.astype(BF16), wr_ref[...], preferred_element_type=F32) + br_ref[...]
    lane, lane_f, g_idx, i1, i2, w1, w2 = _route(logits)

    @pl.when(pl.program_id(0) == 0)
    def _():
        cnt_ref[...] = cnt_in_ref[...]

    picks = jnp.where(lane_f == g_idx, 1.0, 0.0)
    rows = picks.shape[0]
    tri = (lax.broadcasted_iota(jnp.int32, (rows, rows), 0)
           > lax.broadcasted_iota(jnp.int32, (rows, rows), 1))
    before = jnp.dot(jnp.where(tri, 1.0, 0.0).astype(BF16), picks.astype(BF16),
                     preferred_element_type=F32) + cnt_ref[...]
    rank = jnp.sum(jnp.where(lane_f == g_idx, before, 0.0), axis=1, keepdims=True)
    cnt_ref[...] += jnp.sum(picks, axis=0, keepdims=True)
    route = jnp.where(lane == ROUTE_GROUP, g_idx, jnp.where(lane == ROUTE_RANK, rank, jnp.where(
        lane == ROUTE_E1, i1 - N_GROUPS, jnp.where(lane == ROUTE_E2, i2 - N_GROUPS, jnp.where(
            lane == ROUTE_W1, w1, jnp.where(lane == ROUTE_W2, w2, 0.0))))))
    xr_ref[:, :D_MODEL] = xn
    xr_ref[:, D_MODEL:] = route


def _outproj(y_conv, y_ret, x, mod4, mod_row_of_tile, norm_g, w_out_bf, w_router_bf, b_router,
             cnt_in):
    tokens = x.shape[0]
    tiles = tokens // TOKEN_TILE

    def mod_spec(which):
        return pl.BlockSpec((1, 1, 1, D_MODEL), lambda i: (mod_row_of_tile(i), which, 0, 0))

    tok = lambda w: pl.BlockSpec((TOKEN_TILE, w), lambda i: (i, 0))
    full = lambda a: pl.BlockSpec(a.shape, lambda i: (0,) * a.ndim)
    return pl.pallas_call(
        _outproj_kernel,
        grid=(tiles,),
        in_specs=[tok(CONV_W), tok(RET_W), tok(D_MODEL), mod_spec(2), mod_spec(3), mod_spec(4),
                  full(norm_g), full(w_out_bf), full(w_router_bf), full(b_router), full(cnt_in)],
        out_specs=[tok(D_MODEL), tok(ROW_W), full(cnt_in)],
        out_shape=[jax.ShapeDtypeStruct((tokens, D_MODEL), F32),
                   jax.ShapeDtypeStruct((tokens, ROW_W), F32),
                   jax.ShapeDtypeStruct(cnt_in.shape, F32)],
        compiler_params=pltpu.CompilerParams(
            dimension_semantics=("arbitrary",), vmem_limit_bytes=VMEM_LIMIT),
        name="outproj",
    )(y_conv, y_ret, x, mod4, mod4, mod4, norm_g, w_out_bf, w_router_bf, b_router, cnt_in)


def _row_copy(src, src_row, dst, dst_row, sem):
    return pltpu.make_async_copy(src.at[pl.ds(src_row, 1)], dst.at[pl.ds(dst_row, 1)], sem)


def _rows_wait(src, dst, rows, sem):
    pltpu.make_async_copy(src.at[pl.ds(0, rows)], dst.at[pl.ds(0, rows)], sem).wait()


def _dispatch_kernel(ctx_tiles, pad_lo_ref, pad_hi_ref, pos_ref, xr_ctx_hbm, xr_lat_hbm,
                     xs_hbm, buf, zero_buf, in_sem, out_sem):
    i = pl.program_id(0)
    last = pl.num_programs(0) - 1
    slot = i % 2

    def tile_load(t, s):
        @pl.when(t < ctx_tiles)
        def _():
            pltpu.make_async_copy(xr_ctx_hbm.at[pl.ds(t * TOKEN_TILE, TOKEN_TILE)], buf.at[s], in_sem.at[s]).start()

        @pl.when(t >= ctx_tiles)
        def _():
            pltpu.make_async_copy(xr_lat_hbm.at[pl.ds((t - ctx_tiles) * TOKEN_TILE, TOKEN_TILE)],
                                  buf.at[s], in_sem.at[s]).start()

    @pl.when(i == 0)
    def _():
        tile_load(i, 0)

    @pl.when(i > 0)
    def _():
        _rows_wait(buf.at[1 - slot], xs_hbm, TOKEN_TILE, out_sem.at[1 - slot])

    @pl.when(i < last)
    def _():
        tile_load(i + 1, 1 - slot)

    pltpu.make_async_copy(xr_lat_hbm.at[pl.ds(0, TOKEN_TILE)], buf.at[slot], in_sem.at[slot]).wait()

    def issue(r, carry):
        _row_copy(buf.at[slot], r, xs_hbm, pos_ref[0, 0, r], out_sem.at[slot]).start()
        return carry

    lax.fori_loop(0, TOKEN_TILE, issue, 0, unroll=8)

    @pl.when(i == last)
    def _():
        _rows_wait(buf.at[slot], xs_hbm, TOKEN_TILE, out_sem.at[slot])
        zero_buf[...] = jnp.zeros_like(zero_buf)
        for g in range(N_GROUPS):
            lo, hi = pad_lo_ref[g], pad_hi_ref[g]
            pad = lambda r: _row_copy(zero_buf, 0, xs_hbm, r, out_sem.at[0])
            lax.fori_loop(lo, hi, lambda r, c: (pad(r).start(), c)[1], 0)
            lax.fori_loop(lo, hi, lambda r, c: (pad(r).wait(), c)[1], 0)


def _dispatch(xr_ctx, xr_lat, pos_tiles, pad_lo, pad_hi, slots):
    ctx_tiles = xr_ctx.shape[0] // TOKEN_TILE
    tiles = ctx_tiles + xr_lat.shape[0] // TOKEN_TILE
    grid_spec = pltpu.PrefetchScalarGridSpec(
        num_scalar_prefetch=2,
        grid=(tiles,),
        in_specs=[
            pl.BlockSpec((1, 1, TOKEN_TILE), lambda i, lo, hi: (i, 0, 0), memory_space=pltpu.SMEM),
            pl.BlockSpec(memory_space=pl.ANY),
            pl.BlockSpec(memory_space=pl.ANY),
        ],
        out_specs=pl.BlockSpec(memory_space=pl.ANY),
        scratch_shapes=[pltpu.VMEM((2, TOKEN_TILE, ROW_W), F32), pltpu.VMEM((8, ROW_W), F32),
                        pltpu.SemaphoreType.DMA((2,)), pltpu.SemaphoreType.DMA((2,))],
    )
    return pl.pallas_call(
        functools.partial(_dispatch_kernel, ctx_tiles),
        grid_spec=grid_spec,
        out_shape=jax.ShapeDtypeStruct((slots, ROW_W), F32),
        compiler_params=pltpu.CompilerParams(dimension_semantics=("arbitrary",)),
        name="dispatch",
    )(pad_lo, pad_hi, pos_tiles, xr_ctx, xr_lat)


def _expert_kernel(tile_group_ref, n_used_ref, xs_ref, w1_ref, w3_ref, w2_ref, ys_ref, xb, gate_tabs):
    j = pl.program_id(0)
    ei = pl.program_id(1)

    @pl.when(j < n_used_ref[0])
    def _():
        @pl.when(ei == 0)
        def _():
            xb[...] = xs_ref[:, :D_MODEL].astype(BF16)
            route = xs_ref[:, D_MODEL:]
            lane = lax.broadcasted_iota(jnp.int32, route.shape, 1)
            for n, which in enumerate((ROUTE_E1, ROUTE_E2, ROUTE_W1, ROUTE_W2)):
                col = jnp.sum(jnp.where(lane == which, route, 0.0), axis=1, keepdims=True)
                gate_tabs[n] = jnp.broadcast_to(col, route.shape)

        expert = (tile_group_ref[j] * EXPERTS_PER_GROUP + ei).astype(F32)
        gate = (jnp.where(gate_tabs[0] == expert, gate_tabs[2], 0.0)
                + jnp.where(gate_tabs[1] == expert, gate_tabs[3], 0.0))
        x = xb[...]
        hid = _silu(jnp.dot(x, w1_ref[0].astype(BF16), preferred_element_type=F32)) * jnp.dot(
            x, w3_ref[0].astype(BF16), preferred_element_type=F32)
        y = jnp.dot(hid.astype(BF16), w2_ref[0].astype(BF16), preferred_element_type=F32)
        gated = jnp.concatenate(
            [gate * y[:, c * LANES:(c + 1) * LANES] for c in range(D_MODEL // LANES)], axis=1)

        @pl.when(ei == 0)
        def _():
            ys_ref[...] = gated

        @pl.when(ei > 0)
        def _():
            ys_ref[...] += gated


def _experts(xs, tile_group, n_used, w1, w3, w2):
    slots = xs.shape[0]
    row_tile = lambda j, ei, tg, nu: (jnp.minimum(j, nu[0] - 1), 0)
    last_e = EXPERTS_PER_GROUP - 1
    w_spec = lambda shape: pl.BlockSpec((1,) + shape, lambda j, ei, tg, nu: (
        tg[jnp.minimum(j, nu[0] - 1)] * EXPERTS_PER_GROUP + jnp.where(j < nu[0], ei, last_e), 0, 0))
    grid_spec = pltpu.PrefetchScalarGridSpec(
        num_scalar_prefetch=2,
        grid=(slots // GROUP_TILE, EXPERTS_PER_GROUP),
        in_specs=[
            pl.BlockSpec((GROUP_TILE, ROW_W), row_tile),
            w_spec((D_MODEL, D_EXPERT)), w_spec((D_MODEL, D_EXPERT)), w_spec((D_EXPERT, D_MODEL)),
        ],
        out_specs=pl.BlockSpec((GROUP_TILE, D_MODEL), row_tile),
        scratch_shapes=[pltpu.VMEM((GROUP_TILE, D_MODEL), BF16), pltpu.VMEM((4, GROUP_TILE, LANES), F32)],
    )
    return pl.pallas_call(
        _expert_kernel,
        grid_spec=grid_spec,
        out_shape=jax.ShapeDtypeStruct((slots, D_MODEL), F32),
        compiler_params=pltpu.CompilerParams(
            dimension_semantics=("arbitrary", "arbitrary"), vmem_limit_bytes=VMEM_LIMIT),
        name="experts",
    )(tile_group, n_used, xs, w1, w3, w2)


def _combine_kernel(n_tiles, pos_ref, pos_next_ref, x1_ref, g2_ref, fg_ref, ys_hbm, o_ref, buf, sem):
    i = pl.program_id(0)
    slot = i % 2

    def issue(p_ref, s):
        def body(r, carry):
            _row_copy(ys_hbm, p_ref[0, 0, r], buf.at[s], r, sem.at[s]).start()
            return carry
        lax.fori_loop(0, TOKEN_TILE, body, 0, unroll=8)

    @pl.when(i == 0)
    def _():
        issue(pos_ref, 0)

    @pl.when(i + 1 < n_tiles)
    def _():
        issue(pos_next_ref, 1 - slot)

    _rows_wait(ys_hbm, buf.at[slot], TOKEN_TILE, sem.at[slot])
    y = x1_ref[...] + g2_ref[0, 0] * buf[slot]
    o_ref[...] = _rms(y) * fg_ref[...]


def _combine(ys, pos_tiles, x1, mod4, mod_row_of_tile, final_g, tile_base):
    tokens = x1.shape[0]
    tiles = tokens // TOKEN_TILE
    last = tile_base + tiles - 1
    pos_spec = lambda nxt: pl.BlockSpec(
        (1, 1, TOKEN_TILE), lambda i: (jnp.minimum(i + tile_base + nxt, last), 0, 0),
        memory_space=pltpu.SMEM)
    return pl.pallas_call(
        functools.partial(_combine_kernel, tiles),
        grid=(tiles,),
        in_specs=[
            pos_spec(0), pos_spec(1),
            pl.BlockSpec((TOKEN_TILE, D_MODEL), lambda i: (i, 0)),
            pl.BlockSpec((1, 1, 1, D_MODEL), lambda i: (mod_row_of_tile(i), 5, 0, 0)),
            pl.BlockSpec((1, D_MODEL), lambda i: (0, 0)),
            pl.BlockSpec(memory_space=pl.ANY),
        ],
        out_specs=pl.BlockSpec((TOKEN_TILE, D_MODEL), lambda i: (i, 0)),
        out_shape=jax.ShapeDtypeStruct((tokens, D_MODEL), F32),
        scratch_shapes=[pltpu.VMEM((2, TOKEN_TILE, D_MODEL), F32), pltpu.SemaphoreType.DMA((2,))],
        compiler_params=pltpu.CompilerParams(
            dimension_semantics=("arbitrary",), vmem_limit_bytes=VMEM_LIMIT),
        name="combine",
    )(pos_tiles, pos_tiles, x1, mod4, final_g, ys)


def _routing_tables(rows, counts):
    counts = counts.astype(jnp.int32)
    padded = ((counts + GROUP_TILE - 1) // GROUP_TILE) * GROUP_TILE
    ends = jnp.cumsum(padded)
    offs = ends - padded
    route = jnp.concatenate([r[:, D_MODEL:D_MODEL + 2] for r in rows], axis=0).astype(jnp.int32)
    group, rank = route[:, ROUTE_GROUP], route[:, ROUTE_RANK]
    onehot = group[:, None] == jnp.arange(N_GROUPS, dtype=jnp.int32)
    pos = jnp.sum(jnp.where(onehot, offs, 0), axis=-1) + rank
    pos_tiles = pos.reshape(-1, 1, TOKEN_TILE)
    n_used = ends[-1] // GROUP_TILE
    max_tiles = pos.shape[0] // GROUP_TILE + N_GROUPS
    tile_ids = jnp.minimum(jnp.arange(max_tiles, dtype=jnp.int32), n_used - 1)
    tile_group = jnp.sum(tile_ids[:, None] * GROUP_TILE >= ends[None, :], axis=1).astype(jnp.int32)
    return pos_tiles, offs + counts, ends, tile_group, n_used.reshape(1), max_tiles * GROUP_TILE


def _mixer(x, mod4, mod_row, is_grid, s_f0, s_b0, p):
    norm_mix_g, w_in_bf, conv_w, decay_rows = p
    y_conv, q, k, v, g = _inproj(x, mod4, mod_row, norm_mix_g, w_in_bf, conv_w, is_grid)
    ret = _retention(q, k, v, g, decay_rows, s_f0, s_b0, emit_state=not is_grid)
    flat = lambda a: a.reshape(-1, a.shape[-1])
    return flat(y_conv), flat(ret[0]), ret[1:]


def kernel(x_prompt, x_sample, state_ret_fwd, state_ret_bwd, c, c_ctx, norm_mix_g, norm_ffn_g, w_ada, b_ada, w_in, conv_w, ret_decay_fwd, ret_decay_bwd, w_out, w_router_group, b_router_group, w_router_expert, b_router_expert, w_gate_e, w_up_e, w_down_e, final_norm_g):
    assert norm_mix_g.shape[0] == 1, "single-layer backbone"
    n_lat = c.shape[0]
    ctx_row = n_lat
    mod_rows = 8
    cvec = jnp.concatenate([c, c_ctx[None, :], jnp.zeros((mod_rows - n_lat - 1, D_MODEL), F32)], axis=0)
    mod = _modulation(cvec, w_ada[0], b_ada[0][None, :])
    mod4 = mod.reshape(mod_rows, 6, 1, D_MODEL)

    pad = ROUTER_COLS - N_GROUPS - N_EXPERTS
    w_router = jnp.concatenate(
        [w_router_group[0], w_router_expert[0], jnp.zeros((D_MODEL, pad), F32)], axis=1).astype(BF16)
    b_router = jnp.concatenate([b_router_group[0], b_router_expert[0], jnp.zeros((pad,), F32)])[None, :]
    decay_rows = jnp.broadcast_to(
        jnp.concatenate([ret_decay_fwd[0], ret_decay_bwd[0]])[:, None], (2 * RET_HEADS, LANES))
    p_mix = (norm_mix_g, w_in[0].astype(BF16), conv_w[0], decay_rows)
    w_out_bf = w_out[0].astype(BF16)
    final_g = final_norm_g[None, :]

    ctx_tiles = (x_prompt.shape[0] * x_prompt.shape[1]) // TOKEN_TILE
    lat_tiles_per_batch = x_sample.shape[1] // TOKEN_TILE
    ctx_mod = lambda i: ctx_row
    lat_mod = lambda i: i // lat_tiles_per_batch
    flat = lambda a: a.reshape(-1, a.shape[-1])

    yc_c, yr_c, (s_f, s_b) = _mixer(x_prompt, mod4, lambda b: ctx_row, False, None, None, p_mix)
    yc_l, yr_l, _ = _mixer(x_sample, mod4, lambda b: b, True, state_ret_fwd, state_ret_bwd, p_mix)

    cnt0 = jnp.zeros((1, ROUTER_COLS), F32)
    x1_c, xr_c, cnt_c = _outproj(
        yc_c, yr_c, flat(x_prompt), mod4, ctx_mod, norm_ffn_g, w_out_bf, w_router, b_router, cnt0)
    x1_l, xr_l, cnt = _outproj(
        yc_l, yr_l, flat(x_sample), mod4, lat_mod, norm_ffn_g, w_out_bf, w_router, b_router, cnt_c)

    pos_tiles, pad_lo, pad_hi, tile_group, n_used, slots = _routing_tables(
        (xr_c, xr_l), cnt[0, :N_GROUPS])
    xs = _dispatch(xr_c, xr_l, pos_tiles, pad_lo, pad_hi, slots)
    ys = _experts(xs, tile_group, n_used, w_gate_e[0], w_up_e[0], w_down_e[0])
    y_prompt = _combine(ys, pos_tiles, x1_c, mod4, ctx_mod, final_g, 0)
    y_sample = _combine(ys, pos_tiles, x1_l, mod4, lat_mod, final_g, ctx_tiles)
    return (y_prompt.reshape(x_prompt.shape), y_sample.reshape(x_sample.shape),
            s_f.astype(x_prompt.dtype), s_b.astype(x_prompt.dtype))
```

```python
import functools
import math

import jax
import jax.numpy as jnp
from jax import lax
from jax.experimental import pallas as pl
from jax.experimental.pallas import tpu as pltpu

F32 = jnp.float32
BF16 = jnp.bfloat16

D_MODEL = 1024
GRID_W = 64
CONV_W = 512
RET_HEADS = 4
RET_DK = 128
RET_DV = 128
RET_W = RET_HEADS * RET_DV
QK_W = RET_HEADS * RET_DK
CHUNK = 128
N_GROUPS = 4
EXPERTS_PER_GROUP = 8
N_EXPERTS = N_GROUPS * EXPERTS_PER_GROUP
D_EXPERT = 256
ROPE_BASE = 10000.0
EPS = 1e-6

LANES = 128
TOKEN_TILE = 256
GROUP_TILE = 1024
ROW_W = D_MODEL + LANES
ROUTE_GROUP, ROUTE_RANK, ROUTE_E1, ROUTE_E2, ROUTE_W1, ROUTE_W2 = range(6)
MOD_COLS = 1536
ROUTER_COLS = LANES
VMEM_LIMIT = 48 * 1024 * 1024
EXPERT_VMEM_LIMIT = 56 * 1024 * 1024

def _silu(x):
    return x * jax.nn.sigmoid(x)


def _rms(x):
    return x * lax.rsqrt(jnp.mean(x * x, axis=-1, keepdims=True) + EPS)


def _bdot(a, b):
    return jnp.dot(a.astype(BF16), b.astype(BF16), preferred_element_type=F32)


def _mod_kernel(c_ref, w_ref, b_ref, o_ref):
    o_ref[...] = _bdot(_silu(c_ref[...]), w_ref[...]) + b_ref[...]


def _modulation(cvec, w_ada, b_ada):
    rows = cvec.shape[0]
    n = w_ada.shape[1]
    return pl.pallas_call(
        _mod_kernel,
        grid=(n // MOD_COLS,),
        in_specs=[
            pl.BlockSpec((rows, D_MODEL), lambda j: (0, 0)),
            pl.BlockSpec((D_MODEL, MOD_COLS), lambda j: (0, j)),
            pl.BlockSpec((1, MOD_COLS), lambda j: (0, j)),
        ],
        out_specs=pl.BlockSpec((rows, MOD_COLS), lambda j: (0, j)),
        out_shape=jax.ShapeDtypeStruct((rows, n), F32),
        compiler_params=pltpu.CompilerParams(vmem_limit_bytes=VMEM_LIMIT),
        name="modulation",
    )(cvec, w_ada, b_ada)


def _inproj_kernel(seg, is_grid, x_ref, sh_ref, sc_ref, ng_ref, w_ref, cw_ref, *rest):
    if is_grid:
        cos_ref, sa_ref, sb_ref, yc_ref, q_ref, k_ref, v_ref, g_ref = rest
    else:
        yc_ref, q_ref, k_ref, v_ref, g_ref = rest
    x = x_ref[0]
    xn = (_rms(x) * ng_ref[...]) * (1.0 + sc_ref[0, 0]) + sh_ref[0, 0]
    xb = xn.astype(BF16)

    def proj(c0, n):
        return jnp.dot(xb, w_ref[:, c0:c0 + n], preferred_element_type=F32)

    gate_b = proj(0, CONV_W)
    u = proj(CONV_W, CONV_W) * proj(2 * CONV_W, CONV_W)
    rows = u.shape[0]
    pos = lax.broadcasted_iota(jnp.int32, u.shape, 0) & (seg - 1)
    u_prev = jnp.where(pos != 0, pltpu.roll(u, 1, 0), 0.0)
    u_next = jnp.where(pos != seg - 1, pltpu.roll(u, rows - 1, 0), 0.0)
    conv = cw_ref[0:1, :] * u_prev + cw_ref[1:2, :] * u + cw_ref[2:3, :] * u_next
    yc_ref[0] = (gate_b * conv).astype(yc_ref.dtype)

    q0 = 3 * CONV_W
    q = proj(q0, QK_W)
    k = proj(q0 + QK_W, QK_W)
    if is_grid:
        cos, sa, sb = cos_ref[...], sa_ref[...], sb_ref[...]

        def rope(t):
            out = []
            for h in range(RET_HEADS):
                th = t[:, h * RET_DK:(h + 1) * RET_DK]
                out.append(th * cos + pltpu.roll(th, RET_DK - 1, 1) * sa + pltpu.roll(th, 1, 1) * sb)
            return jnp.concatenate(out, axis=1)

        q, k = rope(q), rope(k)
    q_ref[0] = q
    k_ref[0] = k
    v_ref[0] = proj(q0 + 2 * QK_W, RET_W)
    g_ref[0] = proj(q0 + 2 * QK_W + RET_W, RET_W)


def _rope_tables(length):
    pos = jnp.arange(length)
    row = (pos // GRID_W).astype(F32)
    col = (pos % GRID_W).astype(F32)
    n_pairs = RET_DK // 4
    freqs = ROPE_BASE ** (-(jnp.arange(n_pairs, dtype=F32) * 2.0 / (RET_DK // 2)))
    ang = jnp.concatenate([row[:, None] * freqs, col[:, None] * freqs], axis=-1)
    cos = jnp.repeat(jnp.cos(ang), 2, axis=-1)
    sin = jnp.repeat(jnp.sin(ang), 2, axis=-1)
    even = (jnp.arange(RET_DK) % 2) == 0
    return cos, jnp.where(even, -sin, 0.0), jnp.where(even, 0.0, sin)


def _inproj(x, mod4, mod_row, norm_g, w_in_bf, conv_w, is_grid):
    bsz, length, _ = x.shape
    seg = GRID_W if is_grid else length
    assert TOKEN_TILE % seg == 0 and length % TOKEN_TILE == 0
    tiles = length // TOKEN_TILE

    def mod_spec(which):
        return pl.BlockSpec((1, 1, 1, D_MODEL), lambda b, i: (mod_row(b), which, 0, 0))

    def tok_spec(width):
        return pl.BlockSpec((1, TOKEN_TILE, width), lambda b, i: (b, i, 0))

    in_specs = [
        tok_spec(D_MODEL), mod_spec(0), mod_spec(1),
        pl.BlockSpec((1, D_MODEL), lambda b, i: (0, 0)),
        pl.BlockSpec(w_in_bf.shape, lambda b, i: (0, 0)),
        pl.BlockSpec(conv_w.shape, lambda b, i: (0, 0)),
    ]
    args = [x, mod4, mod4, norm_g, w_in_bf, conv_w]
    if is_grid:
        in_specs += [pl.BlockSpec((TOKEN_TILE, RET_DK), lambda b, i: (i, 0))] * 3
        args += list(_rope_tables(length))
    shp = lambda w, dt: jax.ShapeDtypeStruct((bsz, length, w), dt)
    return pl.pallas_call(
        functools.partial(_inproj_kernel, seg, is_grid),
        grid=(bsz, tiles),
        in_specs=in_specs,
        out_specs=[tok_spec(CONV_W), tok_spec(QK_W), tok_spec(QK_W), tok_spec(RET_W), tok_spec(RET_W)],
        out_shape=[shp(CONV_W, BF16), shp(QK_W, F32), shp(QK_W, F32), shp(RET_W, F32), shp(RET_W, F32)],
        compiler_params=pltpu.CompilerParams(
            dimension_semantics=("parallel", "parallel"), vmem_limit_bytes=VMEM_LIMIT),
        name="inproj_grid" if is_grid else "inproj_seq",
    )(*args)


def _ret_kernel(n_chunks, has_init, emit_state, a_ref, q_ref, k_ref, v_ref, g_ref, *rest):
    rest = list(rest)
    if has_init:
        sf0_ref, sb0_ref = rest[:2]
        rest = rest[2:]
    y_ref = rest.pop(0)
    if emit_state:
        sf_out, sb_out = rest[:2]
        rest = rest[2:]
    st_f, st_b, dec = rest
    h = pl.program_id(0)
    c = CHUNK
    sq = (c, c)
    lg_f = jnp.log1p(-jnp.exp(a_ref[pl.ds(h, 1), :]))
    lg_b = jnp.log1p(-jnp.exp(a_ref[pl.ds(h + RET_HEADS, 1), :]))
    cd_f = jnp.exp(c * lg_f)
    cd_b = jnp.exp(c * lg_b)

    @pl.when(pl.program_id(1) == 0)
    def _():
        row = lax.broadcasted_iota(jnp.int32, sq, 0).astype(F32)
        col = lax.broadcasted_iota(jnp.int32, sq, 1).astype(F32)
        scale = RET_DK ** -0.5
        dec[0] = scale * (jnp.where(row >= col, jnp.exp(jnp.where(row >= col, row - col, 0.0) * lg_f), 0.0)
                          + jnp.where(col >= row, jnp.exp(jnp.where(col >= row, col - row, 0.0) * lg_b), 0.0))
        dec[1] = jnp.exp((row + 1.0) * lg_f)
        dec[2] = jnp.exp((c - row) * lg_b)
        dec[3] = scale * jnp.exp((c - 1.0 - col) * lg_f)
        dec[4] = scale * jnp.exp(col * lg_b)

    def rows(n):
        return pl.ds(pl.multiple_of(n * c, c), c)

    def kv_step(n, carry):
        kt = jnp.transpose(k_ref[0, rows(n), :])
        lhs = jnp.concatenate([kt * dec[3], kt * dec[4]], axis=0)
        kv = _bdot(lhs, v_ref[0, rows(n), :])
        st_f[n] = kv[:RET_DK]
        st_b[n] = kv[RET_DK:]
        return carry

    def scan_f(n, s):
        kv = st_f[n]
        st_f[n] = s
        return s * cd_f + kv

    def scan_b(i, s):
        n = n_chunks - 1 - i
        kv = st_b[n]
        st_b[n] = s
        return s * cd_b + kv

    def out_step(n, carry):
        q = q_ref[0, rows(n), :]
        scores = lax.dot_general(q.astype(BF16), k_ref[0, rows(n), :].astype(BF16),
                                 (((1,), (1,)), ((), ())), preferred_element_type=F32)
        o = _bdot(scores * dec[0], v_ref[0, rows(n), :])
        q_dec = jnp.concatenate([q * dec[1], q * dec[2]], axis=1)
        o = o + _bdot(q_dec, jnp.concatenate([st_f[n], st_b[n]], axis=0))
        y = _silu(g_ref[0, rows(n), :]) * _rms(o)
        y_ref[0, rows(n), :] = y.astype(y_ref.dtype)
        return carry

    unroll = True if n_chunks <= 2 else 8
    lax.fori_loop(0, n_chunks, kv_step, 0, unroll=unroll)
    s_f = sf0_ref[0, 0, 0] if has_init else jnp.zeros(sq, F32)
    s_b = sb0_ref[0, 0, 0] if has_init else jnp.zeros(sq, F32)
    s_f = lax.fori_loop(0, n_chunks, scan_f, s_f, unroll=unroll)
    s_b = lax.fori_loop(0, n_chunks, scan_b, s_b, unroll=unroll)
    lax.fori_loop(0, n_chunks, out_step, 0, unroll=unroll)
    if emit_state:
        sf_out[0, 0, 0] = s_f
        sb_out[0, 0, 0] = s_b


def _retention(q, k, v, g, decay_rows, s_f0, s_b0, emit_state):
    bsz, length, _ = q.shape
    n_chunks = length // CHUNK
    has_init = s_f0 is not None
    head_spec = pl.BlockSpec((1, length, RET_DK), lambda h, b: (b, 0, h))
    st_spec = pl.BlockSpec((1, 1, 1, RET_DK, RET_DV), lambda h, b: (b, 0, h, 0, 0))
    in_specs = [pl.BlockSpec(decay_rows.shape, lambda h, b: (0, 0))] + [head_spec] * 4
    args = [decay_rows, q, k, v, g]
    if has_init:
        in_specs += [st_spec, st_spec]
        args += [s_f0, s_b0]
    out_specs = [head_spec]
    out_shape = [jax.ShapeDtypeStruct((bsz, length, RET_W), BF16)]
    if emit_state:
        st_shape = jax.ShapeDtypeStruct((bsz, 1, RET_HEADS, RET_DK, RET_DV), F32)
        out_specs += [st_spec, st_spec]
        out_shape += [st_shape, st_shape]
    return pl.pallas_call(
        functools.partial(_ret_kernel, n_chunks, has_init, emit_state),
        grid=(RET_HEADS, bsz),
        in_specs=in_specs,
        out_specs=out_specs,
        out_shape=out_shape,
        scratch_shapes=[
            pltpu.VMEM((n_chunks, RET_DK, RET_DV), F32),
            pltpu.VMEM((n_chunks, RET_DK, RET_DV), F32),
            pltpu.VMEM((5, CHUNK, CHUNK), F32),
        ],
        compiler_params=pltpu.CompilerParams(
            dimension_semantics=("arbitrary", "arbitrary"), vmem_limit_bytes=VMEM_LIMIT),
        name="retention_init" if has_init else "retention_zero",
    )(*args)


def _route(logits):
    lane = lax.broadcasted_iota(jnp.int32, logits.shape, 1)
    lane_f = lane.astype(F32)
    neg = -jnp.inf
    far = float(LANES)
    is_g = lane < N_GROUPS
    lg = jnp.where(is_g, logits, neg)
    g_max = jnp.max(lg, axis=1, keepdims=True)
    g_idx = jnp.min(jnp.where(lg == g_max, lane_f, far), axis=1, keepdims=True)
    p_sel = 1.0 / jnp.sum(jnp.where(is_g, jnp.exp(lg - g_max), 0.0), axis=1, keepdims=True)
    lane_group = ((lane - N_GROUPS) >> 3).astype(F32)
    sel = (lane >= N_GROUPS) & (lane < N_GROUPS + N_EXPERTS) & (lane_group == g_idx)
    le = jnp.where(sel, logits, neg)
    v1 = jnp.max(le, axis=1, keepdims=True)
    i1 = jnp.min(jnp.where(le == v1, lane_f, far), axis=1, keepdims=True)
    le2 = jnp.where(lane_f == i1, neg, le)
    v2 = jnp.max(le2, axis=1, keepdims=True)
    i2 = jnp.min(jnp.where(le2 == v2, lane_f, far), axis=1, keepdims=True)
    e2 = jnp.exp(v2 - v1)
    w1 = p_sel * (1.0 / (1.0 + e2))
    w2 = p_sel * (e2 / (1.0 + e2))
    return lane, lane_f, g_idx, i1, i2, w1, w2


def _outproj_kernel(ctx_tiles, yc_c, yr_c, x_c, yc_l, yr_l, x_l, *rest):
    @pl.when(pl.program_id(0) < ctx_tiles)
    def _():
        _outproj_tile(yc_c, yr_c, x_c, *rest)

    @pl.when(pl.program_id(0) >= ctx_tiles)
    def _():
        _outproj_tile(yc_l, yr_l, x_l, *rest)


def _outproj_tile(yc_ref, yr_ref, x_ref, g1_ref, sh_ref, sc_ref, ng_ref, wo_ref, wr_ref, br_ref,
                  x1_ref, xr_ref, cnt_ref):
    m = (jnp.dot(yc_ref[...], wo_ref[0:CONV_W, :], preferred_element_type=F32)
         + jnp.dot(yr_ref[...], wo_ref[CONV_W:, :], preferred_element_type=F32))
    x1 = x_ref[...] + g1_ref[0, 0] * m
    x1_ref[...] = x1
    xn = (_rms(x1) * ng_ref[...]) * (1.0 + sc_ref[0, 0]) + sh_ref[0, 0]
    logits = jnp.dot(xn.astype(BF16), wr_ref[...], preferred_element_type=F32) + br_ref[...]
    lane, lane_f, g_idx, i1, i2, w1, w2 = _route(logits)

    @pl.when(pl.program_id(0) == 0)
    def _():
        cnt_ref[...] = jnp.zeros_like(cnt_ref)

    picks = jnp.where(lane_f == g_idx, 1.0, 0.0)
    rows = picks.shape[0]
    tri = (lax.broadcasted_iota(jnp.int32, (rows, rows), 0)
           > lax.broadcasted_iota(jnp.int32, (rows, rows), 1))
    before = jnp.dot(jnp.where(tri, 1.0, 0.0).astype(BF16), picks.astype(BF16),
                     preferred_element_type=F32) + cnt_ref[...]
    rank = jnp.sum(jnp.where(lane_f == g_idx, before, 0.0), axis=1, keepdims=True)
    cnt_ref[...] += jnp.sum(picks, axis=0, keepdims=True)
    route = jnp.where(lane == ROUTE_GROUP, g_idx, jnp.where(lane == ROUTE_RANK, rank, jnp.where(
        lane == ROUTE_E1, i1 - N_GROUPS, jnp.where(lane == ROUTE_E2, i2 - N_GROUPS, jnp.where(
            lane == ROUTE_W1, w1, jnp.where(lane == ROUTE_W2, w2, 0.0))))))
    xr_ref[:, :D_MODEL] = xn
    xr_ref[:, D_MODEL:] = route


def _outproj(ctx, lat, mod4, mod_row_of_tile, norm_g, w_out_bf, w_router_bf, b_router):
    ctx_tiles = ctx[2].shape[0] // TOKEN_TILE
    lat_tiles = lat[2].shape[0] // TOKEN_TILE
    tiles = ctx_tiles + lat_tiles
    tokens = tiles * TOKEN_TILE

    def mod_spec(which):
        return pl.BlockSpec((1, 1, 1, D_MODEL), lambda i: (mod_row_of_tile(i), which, 0, 0))

    ctx_tok = lambda w: pl.BlockSpec((TOKEN_TILE, w), lambda i: (jnp.minimum(i, ctx_tiles - 1), 0))
    lat_tok = lambda w: pl.BlockSpec((TOKEN_TILE, w), lambda i: (jnp.maximum(i - ctx_tiles, 0), 0))
    tok = lambda w: pl.BlockSpec((TOKEN_TILE, w), lambda i: (i, 0))
    full = lambda a: pl.BlockSpec(a.shape, lambda i: (0,) * a.ndim)
    widths = (CONV_W, RET_W, D_MODEL)
    cnt_shape = jax.ShapeDtypeStruct((1, ROUTER_COLS), F32)
    return pl.pallas_call(
        functools.partial(_outproj_kernel, ctx_tiles),
        grid=(tiles,),
        in_specs=[ctx_tok(w) for w in widths] + [lat_tok(w) for w in widths] + [
            mod_spec(2), mod_spec(3), mod_spec(4),
            full(norm_g), full(w_out_bf), full(w_router_bf), full(b_router)],
        out_specs=[tok(D_MODEL), tok(ROW_W), full(cnt_shape)],
        out_shape=[jax.ShapeDtypeStruct((tokens, D_MODEL), F32),
                   jax.ShapeDtypeStruct((tokens, ROW_W), F32),
                   cnt_shape],
        compiler_params=pltpu.CompilerParams(
            dimension_semantics=("arbitrary",), vmem_limit_bytes=VMEM_LIMIT),
        name="outproj",
    )(*ctx, *lat, mod4, mod4, mod4, norm_g, w_out_bf, w_router_bf, b_router)


def _row_copy(src, src_row, dst, dst_row, sem):
    return pltpu.make_async_copy(src.at[pl.ds(src_row, 1)], dst.at[pl.ds(dst_row, 1)], sem)


def _rows_wait(src, dst, rows, sem):
    pltpu.make_async_copy(src.at[pl.ds(0, rows)], dst.at[pl.ds(0, rows)], sem).wait()


def _slot_table_kernel(n_tokens, pad_lo_ref, pad_hi_ref, pos_ref, tab_ref):
    i = pl.program_id(0)
    base = i * TOKEN_TILE

    def place(r, carry):
        tab_ref[pos_ref[0, 0, r]] = base + r
        return carry

    lax.fori_loop(0, TOKEN_TILE, place, 0, unroll=16)

    @pl.when(i == pl.num_programs(0) - 1)
    def _():
        def fill(p, carry):
            tab_ref[p] = n_tokens + p
            return carry
        for g in range(pad_lo_ref.shape[0]):
            lax.fori_loop(pad_lo_ref[g], pad_hi_ref[g], fill, 0)


def _slot_table(pos_tiles, pad_lo, pad_hi, slots):
    tiles = pos_tiles.shape[0]
    grid_spec = pltpu.PrefetchScalarGridSpec(
        num_scalar_prefetch=2,
        grid=(tiles,),
        in_specs=[pl.BlockSpec((1, 1, TOKEN_TILE), lambda i, lo, hi: (i, 0, 0), memory_space=pltpu.SMEM)],
        out_specs=pl.BlockSpec((slots,), lambda i, lo, hi: (0,), memory_space=pltpu.SMEM),
    )
    return pl.pallas_call(
        functools.partial(_slot_table_kernel, tiles * TOKEN_TILE),
        grid_spec=grid_spec,
        out_shape=jax.ShapeDtypeStruct((slots,), jnp.int32),
        compiler_params=pltpu.CompilerParams(dimension_semantics=("arbitrary",)),
        name="slot_table",
    )(pad_lo, pad_hi, pos_tiles)


def _expert_kernel(n_tokens, tile_group_ref, n_used_ref, tab_prev_ref, tab_next_ref, xr_hbm,
                   w1_ref, w3_ref, w2_ref, y_hbm,
                   xbuf0, xbuf1, obuf0, obuf1, xb, gate_tabs, in_sem, out_sem):
    j = pl.program_id(0)
    ei = pl.program_id(1)
    n_used = n_used_ref[0]
    share = GROUP_TILE // EXPERTS_PER_GROUP
    first = ei * share
    xbufs, obufs = (xbuf0, xbuf1), (obuf0, obuf1)

    def fetch_row(tab_ref, row, xbuf, sem):
        tok = jnp.minimum(tab_ref[0, 0, row], n_tokens - 1)
        return _row_copy(xr_hbm, tok, xbuf, row, sem)

    def fetch_share(tab_ref, xbuf, sem):
        for r in range(share):
            fetch_row(tab_ref, first + r, xbuf, sem).start()

    def send_share(tab_ref, obuf, sem):
        for r in range(share):
            _row_copy(obuf, first + r, y_hbm, tab_ref[0, 0, first + r], sem).start()

    def prepare(xbuf):
        xb[...] = xbuf[:, :D_MODEL].astype(BF16)
        route = xbuf[:, D_MODEL:]
        lane = lax.broadcasted_iota(jnp.int32, route.shape, 1)
        for n, which in enumerate((ROUTE_E1, ROUTE_E2, ROUTE_W1, ROUTE_W2)):
            col = jnp.sum(jnp.where(lane == which, route, 0.0), axis=1, keepdims=True)
            gate_tabs[n] = jnp.broadcast_to(col, route.shape)

    def compute(obuf):
        expert = (tile_group_ref[j] * EXPERTS_PER_GROUP + ei).astype(F32)
        gate = (jnp.where(gate_tabs[0] == expert, gate_tabs[2], 0.0)
                + jnp.where(gate_tabs[1] == expert, gate_tabs[3], 0.0))
        x = xb[...]
        hid = _silu(jnp.dot(x, w1_ref[0].astype(BF16), preferred_element_type=F32)) * jnp.dot(
            x, w3_ref[0].astype(BF16), preferred_element_type=F32)
        y = jnp.dot(hid.astype(BF16), w2_ref[0].astype(BF16), preferred_element_type=F32)
        gated = jnp.concatenate(
            [gate * y[:, c * LANES:(c + 1) * LANES] for c in range(D_MODEL // LANES)], axis=1)

        @pl.when(ei == 0)
        def _():
            obuf[...] = gated

        @pl.when(ei > 0)
        def _():
            obuf[...] += gated

    @pl.when((j == 0) & (ei == 0))
    def _():
        def body(r, carry):
            fetch_row(tab_prev_ref, r, xbuf0, in_sem.at[0]).start()
            return carry
        lax.fori_loop(0, GROUP_TILE, body, 0, unroll=8)

    for parity in range(2):
        mine = j % 2 == parity
        other = 1 - parity

        @pl.when(mine & (ei == 0) & (j >= 2) & (j <= n_used + 1))
        def _():
            _rows_wait(obufs[parity], y_hbm, GROUP_TILE, out_sem.at[parity])

        @pl.when(mine & (ei == 0) & (j == n_used))
        def _():
            _rows_wait(xr_hbm, xbufs[parity], GROUP_TILE, in_sem.at[parity])

        @pl.when(mine & (ei == 0) & (j < n_used))
        def _():
            _rows_wait(xr_hbm, xbufs[parity], GROUP_TILE, in_sem.at[parity])
            prepare(xbufs[parity])

        if parity == 0:
            @pl.when(j == 0)
            def _():
                fetch_share(tab_next_ref, xbufs[other], in_sem.at[other])
                compute(obufs[parity])

        @pl.when(mine & (j >= 1) & (j < n_used))
        def _():
            send_share(tab_prev_ref, obufs[other], out_sem.at[other])
            fetch_share(tab_next_ref, xbufs[other], in_sem.at[other])
            compute(obufs[parity])

        @pl.when(mine & (j == n_used))
        def _():
            send_share(tab_prev_ref, obufs[other], out_sem.at[other])


def _experts(tab, tile_group, n_used, xr, w1, w3, w2):
    slots = tab.shape[0]
    n_tokens = xr.shape[0]
    tab_tiles = tab.reshape(slots // GROUP_TILE, 1, GROUP_TILE)

    def tab_spec(shift):
        return pl.BlockSpec((1, 1, GROUP_TILE), lambda j, ei, tg, nu: (jnp.clip(j + shift, 0, nu[0] - 1), 0, 0),
                            memory_space=pltpu.SMEM)

    last_e = EXPERTS_PER_GROUP - 1
    w_spec = lambda shape: pl.BlockSpec((1,) + shape, lambda j, ei, tg, nu: (
        tg[jnp.minimum(j, nu[0] - 1)] * EXPERTS_PER_GROUP + jnp.where(j < nu[0], ei, last_e), 0, 0))
    grid_spec = pltpu.PrefetchScalarGridSpec(
        num_scalar_prefetch=2,
        grid=(tile_group.shape[0], EXPERTS_PER_GROUP),
        in_specs=[
            tab_spec(-1), tab_spec(1), pl.BlockSpec(memory_space=pl.ANY),
            w_spec((D_MODEL, D_EXPERT)), w_spec((D_MODEL, D_EXPERT)), w_spec((D_EXPERT, D_MODEL)),
        ],
        out_specs=pl.BlockSpec(memory_space=pl.ANY),
        scratch_shapes=[
            pltpu.VMEM((GROUP_TILE, ROW_W), F32), pltpu.VMEM((GROUP_TILE, ROW_W), F32),
            pltpu.VMEM((GROUP_TILE, D_MODEL), F32), pltpu.VMEM((GROUP_TILE, D_MODEL), F32),
            pltpu.VMEM((GROUP_TILE, D_MODEL), BF16), pltpu.VMEM((4, GROUP_TILE, LANES), F32),
            pltpu.SemaphoreType.DMA((2,)), pltpu.SemaphoreType.DMA((2,)),
        ],
    )
    return pl.pallas_call(
        functools.partial(_expert_kernel, n_tokens),
        grid_spec=grid_spec,
        out_shape=jax.ShapeDtypeStruct((n_tokens + slots, D_MODEL), F32),
        compiler_params=pltpu.CompilerParams(
            dimension_semantics=("arbitrary", "arbitrary"), vmem_limit_bytes=EXPERT_VMEM_LIMIT),
        name="experts",
    )(tile_group, n_used, tab_tiles, tab_tiles, xr, w1, w3, w2)


def _combine_kernel(x1_ref, y_ref, g2_ref, fg_ref, o_ref):
    y = x1_ref[...] + g2_ref[0, 0] * y_ref[...]
    o_ref[...] = _rms(y) * fg_ref[...]


def _combine(y_tok, x1, tokens, mod4, mod_row_of_tile, final_g, tile_base):
    tok = pl.BlockSpec((TOKEN_TILE, D_MODEL), lambda i: (i + tile_base, 0))
    return pl.pallas_call(
        _combine_kernel,
        grid=(tokens // TOKEN_TILE,),
        in_specs=[
            tok, tok,
            pl.BlockSpec((1, 1, 1, D_MODEL), lambda i: (mod_row_of_tile(i + tile_base), 5, 0, 0)),
            pl.BlockSpec((1, D_MODEL), lambda i: (0, 0)),
        ],
        out_specs=pl.BlockSpec((TOKEN_TILE, D_MODEL), lambda i: (i, 0)),
        out_shape=jax.ShapeDtypeStruct((tokens, D_MODEL), F32),
        compiler_params=pltpu.CompilerParams(
            dimension_semantics=("parallel",), vmem_limit_bytes=VMEM_LIMIT),
        name="combine",
    )(x1, y_tok, mod4, final_g)


def _routing_tables(xr, counts):
    counts = counts.astype(jnp.int32)
    padded = ((counts + GROUP_TILE - 1) // GROUP_TILE) * GROUP_TILE
    ends = jnp.cumsum(padded)
    offs = ends - padded
    route = xr[:, D_MODEL:D_MODEL + 2].astype(jnp.int32)
    group, rank = route[:, ROUTE_GROUP], route[:, ROUTE_RANK]
    onehot = group[:, None] == jnp.arange(N_GROUPS, dtype=jnp.int32)
    pos = jnp.sum(jnp.where(onehot, offs, 0), axis=-1) + rank
    pos_tiles = pos.reshape(-1, 1, TOKEN_TILE)
    n_used = ends[-1] // GROUP_TILE
    max_tiles = pos.shape[0] // GROUP_TILE + N_GROUPS
    slots = max_tiles * GROUP_TILE
    tile_ids = jnp.minimum(jnp.arange(max_tiles + 2, dtype=jnp.int32), n_used - 1)
    tile_group = jnp.sum(tile_ids[:, None] * GROUP_TILE >= ends[None, :], axis=1).astype(jnp.int32)
    pad_lo = jnp.concatenate([offs + counts, ends[-1:]])
    pad_hi = jnp.concatenate([ends, jnp.full((1,), slots, jnp.int32)])
    return pos_tiles, pad_lo, pad_hi, tile_group, n_used.reshape(1), slots


def _mixer(x, mod4, mod_row, is_grid, s_f0, s_b0, p):
    norm_mix_g, w_in_bf, conv_w, decay_rows = p
    y_conv, q, k, v, g = _inproj(x, mod4, mod_row, norm_mix_g, w_in_bf, conv_w, is_grid)
    ret = _retention(q, k, v, g, decay_rows, s_f0, s_b0, emit_state=not is_grid)
    flat = lambda a: a.reshape(-1, a.shape[-1])
    return flat(y_conv), flat(ret[0]), ret[1:]


def kernel(x_prompt, x_sample, state_ret_fwd, state_ret_bwd, c, c_ctx, norm_mix_g, norm_ffn_g, w_ada, b_ada, w_in, conv_w, ret_decay_fwd, ret_decay_bwd, w_out, w_router_group, b_router_group, w_router_expert, b_router_expert, w_gate_e, w_up_e, w_down_e, final_norm_g):
    assert norm_mix_g.shape[0] == 1, "single-layer backbone"
    n_lat = c.shape[0]
    ctx_row = n_lat
    mod_rows = 8
    cvec = jnp.concatenate([c, c_ctx[None, :], jnp.zeros((mod_rows - n_lat - 1, D_MODEL), F32)], axis=0)
    mod = _modulation(cvec, w_ada[0], b_ada[0][None, :])
    mod4 = mod.reshape(mod_rows, 6, 1, D_MODEL)

    pad = ROUTER_COLS - N_GROUPS - N_EXPERTS
    w_router = jnp.concatenate(
        [w_router_group[0], w_router_expert[0], jnp.zeros((D_MODEL, pad), F32)], axis=1).astype(BF16)
    b_router = jnp.concatenate([b_router_group[0], b_router_expert[0], jnp.zeros((pad,), F32)])[None, :]
    decay_rows = jnp.broadcast_to(
        jnp.concatenate([ret_decay_fwd[0], ret_decay_bwd[0]])[:, None], (2 * RET_HEADS, LANES))
    p_mix = (norm_mix_g, w_in[0].astype(BF16), conv_w[0], decay_rows)
    w_out_bf = w_out[0].astype(BF16)
    final_g = final_norm_g[None, :]

    ctx_tokens = x_prompt.shape[0] * x_prompt.shape[1]
    lat_tokens = x_sample.shape[0] * x_sample.shape[1]
    ctx_tiles = ctx_tokens // TOKEN_TILE
    lat_tiles_per_batch = x_sample.shape[1] // TOKEN_TILE
    tile_mod = lambda i: jnp.where(i < ctx_tiles, ctx_row, (i - ctx_tiles) // lat_tiles_per_batch)
    flat = lambda a: a.reshape(-1, a.shape[-1])

    yc_c, yr_c, (s_f, s_b) = _mixer(x_prompt, mod4, lambda b: ctx_row, False, None, None, p_mix)
    yc_l, yr_l, _ = _mixer(x_sample, mod4, lambda b: b, True, state_ret_fwd, state_ret_bwd, p_mix)

    x1, xr, cnt = _outproj((yc_c, yr_c, flat(x_prompt)), (yc_l, yr_l, flat(x_sample)), mod4, tile_mod,
                           norm_ffn_g, w_out_bf, w_router, b_router)
    pos_tiles, pad_lo, pad_hi, tile_group, n_used, slots = _routing_tables(xr, cnt[0, :N_GROUPS])
    tab = _slot_table(pos_tiles, pad_lo, pad_hi, slots)
    y_tok = _experts(tab, tile_group, n_used, xr, w_gate_e[0], w_up_e[0], w_down_e[0])
    y_prompt = _combine(y_tok, x1, ctx_tokens, mod4, tile_mod, final_g, 0)
    y_sample = _combine(y_tok, x1, lat_tokens, mod4, tile_mod, final_g, ctx_tiles)
    return (y_prompt.reshape(x_prompt.shape), y_sample.reshape(x_sample.shape),
            s_f.astype(x_prompt.dtype), s_b.astype(x_prompt.dtype))
```

```python
import functools

import jax
import jax.numpy as jnp
from jax import lax
from jax.experimental import pallas as pl
from jax.experimental.pallas import tpu as pltpu

F32 = jnp.float32
BF16 = jnp.bfloat16

D_MODEL = 1024
GRID_W = 64
CONV_W = 512
RET_HEADS = 4
RET_DK = 128
RET_DV = 128
RET_W = RET_HEADS * RET_DV
QK_W = RET_HEADS * RET_DK
CHUNK = 128
N_GROUPS = 4
EXPERTS_PER_GROUP = 8
N_EXPERTS = N_GROUPS * EXPERTS_PER_GROUP
D_EXPERT = 256
ROPE_BASE = 10000.0
EPS = 1e-6

LANES = 128
TOKEN_TILE = 256
OUTPROJ_TILE = 512
GROUP_TILE = 1024
EXPERTS_PER_STEP = 2
RET_UNROLL = 8
ROW_W = D_MODEL + LANES
ROUTE_GROUP, ROUTE_RANK, ROUTE_E1, ROUTE_E2, ROUTE_W1, ROUTE_W2 = range(6)
MOD_COLS = 1536
ROUTER_COLS = LANES
VMEM_LIMIT = 48 * 1024 * 1024
EXPERT_VMEM_LIMIT = 56 * 1024 * 1024


def _silu(x):
    return x * jax.nn.sigmoid(x)


def _rms(x):
    return x * lax.rsqrt(jnp.mean(x * x, axis=-1, keepdims=True) + EPS)


def _bdot(a, b):
    return jnp.dot(a.astype(BF16), b.astype(BF16), preferred_element_type=F32)


def _mod_kernel(c_ref, w_ref, b_ref, o_ref):
    o_ref[...] = _bdot(_silu(c_ref[...]), w_ref[...]) + b_ref[...]


def _modulation(cvec, w_ada, b_ada):
    rows = cvec.shape[0]
    n = w_ada.shape[1]
    return pl.pallas_call(
        _mod_kernel,
        grid=(n // MOD_COLS,),
        in_specs=[
            pl.BlockSpec((rows, D_MODEL), lambda j: (0, 0)),
            pl.BlockSpec((D_MODEL, MOD_COLS), lambda j: (0, j)),
            pl.BlockSpec((1, MOD_COLS), lambda j: (0, j)),
        ],
        out_specs=pl.BlockSpec((rows, MOD_COLS), lambda j: (0, j)),
        out_shape=jax.ShapeDtypeStruct((rows, n), F32),
        compiler_params=pltpu.CompilerParams(vmem_limit_bytes=VMEM_LIMIT),
        name="modulation",
    )(cvec, w_ada, b_ada)


def _inproj_kernel(seg, is_grid, x_ref, sh_ref, sc_ref, ng_ref, w_ref, cw_ref, *rest):
    if is_grid:
        cos_ref, sa_ref, sb_ref, yc_ref, q_ref, k_ref, v_ref, g_ref = rest
    else:
        yc_ref, q_ref, k_ref, v_ref, g_ref = rest
    x = x_ref[0]
    xn = (_rms(x) * ng_ref[...]) * (1.0 + sc_ref[0, 0]) + sh_ref[0, 0]
    xb = xn.astype(BF16)

    def proj(c0, n):
        return jnp.dot(xb, w_ref[:, c0:c0 + n], preferred_element_type=F32)

    gate_b = proj(0, CONV_W)
    u = proj(CONV_W, CONV_W) * proj(2 * CONV_W, CONV_W)
    rows = u.shape[0]
    pos = lax.broadcasted_iota(jnp.int32, u.shape, 0) & (seg - 1)
    u_prev = jnp.where(pos != 0, pltpu.roll(u, 1, 0), 0.0)
    u_next = jnp.where(pos != seg - 1, pltpu.roll(u, rows - 1, 0), 0.0)
    conv = cw_ref[0:1, :] * u_prev + cw_ref[1:2, :] * u + cw_ref[2:3, :] * u_next
    yc_ref[0] = (gate_b * conv).astype(yc_ref.dtype)

    q0 = 3 * CONV_W
    q = proj(q0, QK_W)
    k = proj(q0 + QK_W, QK_W)
    if is_grid:
        cos, sa, sb = cos_ref[...], sa_ref[...], sb_ref[...]

        def rope(t):
            out = []
            for h in range(RET_HEADS):
                th = t[:, h * RET_DK:(h + 1) * RET_DK]
                out.append(th * cos + pltpu.roll(th, RET_DK - 1, 1) * sa + pltpu.roll(th, 1, 1) * sb)
            return jnp.concatenate(out, axis=1)

        q, k = rope(q), rope(k)
    q_ref[0] = q
    k_ref[0] = k
    v_ref[0] = proj(q0 + 2 * QK_W, RET_W)
    g_ref[0] = proj(q0 + 2 * QK_W + RET_W, RET_W)


def _rope_tables(length):
    pos = jnp.arange(length)
    row = (pos // GRID_W).astype(F32)
    col = (pos % GRID_W).astype(F32)
    n_pairs = RET_DK // 4
    freqs = ROPE_BASE ** (-(jnp.arange(n_pairs, dtype=F32) * 2.0 / (RET_DK // 2)))
    ang = jnp.concatenate([row[:, None] * freqs, col[:, None] * freqs], axis=-1)
    cos = jnp.repeat(jnp.cos(ang), 2, axis=-1)
    sin = jnp.repeat(jnp.sin(ang), 2, axis=-1)
    even = (jnp.arange(RET_DK) % 2) == 0
    return cos, jnp.where(even, -sin, 0.0), jnp.where(even, 0.0, sin)


def _inproj(x, mod4, mod_row, norm_g, w_in_bf, conv_w, is_grid):
    bsz, length, _ = x.shape
    seg = GRID_W if is_grid else length
    assert TOKEN_TILE % seg == 0 and length % TOKEN_TILE == 0
    tiles = length // TOKEN_TILE

    def mod_spec(which):
        return pl.BlockSpec((1, 1, 1, D_MODEL), lambda b, i: (mod_row(b), which, 0, 0))

    def tok_spec(width):
        return pl.BlockSpec((1, TOKEN_TILE, width), lambda b, i: (b, i, 0))

    in_specs = [
        tok_spec(D_MODEL), mod_spec(0), mod_spec(1),
        pl.BlockSpec((1, D_MODEL), lambda b, i: (0, 0)),
        pl.BlockSpec(w_in_bf.shape, lambda b, i: (0, 0)),
        pl.BlockSpec(conv_w.shape, lambda b, i: (0, 0)),
    ]
    args = [x, mod4, mod4, norm_g, w_in_bf, conv_w]
    if is_grid:
        in_specs += [pl.BlockSpec((TOKEN_TILE, RET_DK), lambda b, i: (i, 0))] * 3
        args += list(_rope_tables(length))
    shp = lambda w, dt: jax.ShapeDtypeStruct((bsz, length, w), dt)
    return pl.pallas_call(
        functools.partial(_inproj_kernel, seg, is_grid),
        grid=(bsz, tiles),
        in_specs=in_specs,
        out_specs=[tok_spec(CONV_W), tok_spec(QK_W), tok_spec(QK_W), tok_spec(RET_W), tok_spec(RET_W)],
        out_shape=[shp(CONV_W, BF16), shp(QK_W, F32), shp(QK_W, F32), shp(RET_W, F32), shp(RET_W, F32)],
        compiler_params=pltpu.CompilerParams(
            dimension_semantics=("parallel", "parallel"), vmem_limit_bytes=VMEM_LIMIT),
        name="inproj_grid" if is_grid else "inproj_seq",
    )(*args)


def _ret_kernel(n_chunks, heads, has_init, emit_state, a_ref, q_ref, k_ref, v_ref, g_ref, *rest):
    rest = list(rest)
    if has_init:
        sf0_ref, sb0_ref = rest[:2]
        rest = rest[2:]
    y_ref = rest.pop(0)
    if emit_state:
        sf_out, sb_out = rest[:2]
        rest = rest[2:]
    st_f, st_b, dec = rest
    c = CHUNK
    sq = (c, c)
    head0 = pl.program_id(0) * heads

    def log_decays(hh):
        lg_f = jnp.log1p(-jnp.exp(a_ref[pl.ds(head0 + hh, 1), :]))
        lg_b = jnp.log1p(-jnp.exp(a_ref[pl.ds(head0 + hh + RET_HEADS, 1), :]))
        return lg_f, lg_b

    @pl.when(pl.program_id(1) == 0)
    def _():
        row = lax.broadcasted_iota(jnp.int32, sq, 0).astype(F32)
        col = lax.broadcasted_iota(jnp.int32, sq, 1).astype(F32)
        scale = RET_DK ** -0.5
        for hh in range(heads):
            lg_f, lg_b = log_decays(hh)
            dec[hh, 0] = scale * (
                jnp.where(row >= col, jnp.exp(jnp.where(row >= col, row - col, 0.0) * lg_f), 0.0)
                + jnp.where(col >= row, jnp.exp(jnp.where(col >= row, col - row, 0.0) * lg_b), 0.0))
            dec[hh, 1] = jnp.exp((row + 1.0) * lg_f)
            dec[hh, 2] = jnp.exp((c - row) * lg_b)
            dec[hh, 3] = scale * jnp.exp((c - 1.0 - col) * lg_f)
            dec[hh, 4] = scale * jnp.exp(col * lg_b)

    def rows(n):
        return pl.ds(pl.multiple_of(n * c, c), c) if not isinstance(n, int) else pl.ds(n * c, c)

    def cols(hh):
        return slice(hh * RET_DK, (hh + 1) * RET_DK)

    def kv_step(hh, n):
        kt = jnp.transpose(k_ref[0, rows(n), cols(hh)])
        lhs = jnp.concatenate([kt * dec[hh, 3], kt * dec[hh, 4]], axis=0)
        kv = _bdot(lhs, v_ref[0, rows(n), cols(hh)])
        st_f[hh, n] = kv[:RET_DK]
        st_b[hh, n] = kv[RET_DK:]

    def scan(hh, st, decay, order, s):
        def step(i, s):
            n = order(i)
            kv = st[hh, n]
            st[hh, n] = s
            return s * decay + kv
        if n_chunks <= RET_UNROLL:
            for i in range(n_chunks):
                s = step(i, s)
            return s
        return lax.fori_loop(0, n_chunks, step, s, unroll=RET_UNROLL)

    def out_step(hh, n):
        q = q_ref[0, rows(n), cols(hh)]
        scores = lax.dot_general(q.astype(BF16), k_ref[0, rows(n), cols(hh)].astype(BF16),
                                 (((1,), (1,)), ((), ())), preferred_element_type=F32)
        o = _bdot(scores * dec[hh, 0], v_ref[0, rows(n), cols(hh)])
        q_dec = jnp.concatenate([q * dec[hh, 1], q * dec[hh, 2]], axis=1)
        o = o + _bdot(q_dec, jnp.concatenate([st_f[hh, n], st_b[hh, n]], axis=0))
        y = _silu(g_ref[0, rows(n), cols(hh)]) * _rms(o)
        y_ref[0, rows(n), cols(hh)] = y.astype(y_ref.dtype)

    def over_chunks(step):
        if n_chunks * heads <= RET_UNROLL:
            for hh in range(heads):
                for n in range(n_chunks):
                    step(hh, n)
        else:
            for hh in range(heads):
                lax.fori_loop(0, n_chunks, lambda n, carry: (step(hh, n), carry)[1], 0, unroll=RET_UNROLL)

    over_chunks(kv_step)
    finals = []
    for hh in range(heads):
        lg_f, lg_b = log_decays(hh)
        s_f = sf0_ref[0, 0, hh] if has_init else jnp.zeros(sq, F32)
        s_b = sb0_ref[0, 0, hh] if has_init else jnp.zeros(sq, F32)
        s_f = scan(hh, st_f, jnp.exp(c * lg_f), lambda i: i, s_f)
        s_b = scan(hh, st_b, jnp.exp(c * lg_b), lambda i: n_chunks - 1 - i, s_b)
        finals.append((s_f, s_b))
    over_chunks(out_step)
    if emit_state:
        for hh, (s_f, s_b) in enumerate(finals):
            sf_out[0, 0, hh] = s_f
            sb_out[0, 0, hh] = s_b


def _retention(q, k, v, g, decay_rows, s_f0, s_b0, emit_state):
    bsz, length, _ = q.shape
    n_chunks = length // CHUNK
    has_init = s_f0 is not None
    heads = RET_HEADS if n_chunks * RET_HEADS <= RET_UNROLL else 1
    head_spec = pl.BlockSpec((1, length, heads * RET_DK), lambda h, b: (b, 0, h))
    st_spec = pl.BlockSpec((1, 1, heads, RET_DK, RET_DV), lambda h, b: (b, 0, h, 0, 0))
    in_specs = [pl.BlockSpec(decay_rows.shape, lambda h, b: (0, 0))] + [head_spec] * 4
    args = [decay_rows, q, k, v, g]
    if has_init:
        in_specs += [st_spec, st_spec]
        args += [s_f0, s_b0]
    out_specs = [head_spec]
    out_shape = [jax.ShapeDtypeStruct((bsz, length, RET_W), BF16)]
    if emit_state:
        st_shape = jax.ShapeDtypeStruct((bsz, 1, RET_HEADS, RET_DK, RET_DV), F32)
        out_specs += [st_spec, st_spec]
        out_shape += [st_shape, st_shape]
    return pl.pallas_call(
        functools.partial(_ret_kernel, n_chunks, heads, has_init, emit_state),
        grid=(RET_HEADS // heads, bsz),
        in_specs=in_specs,
        out_specs=out_specs,
        out_shape=out_shape,
        scratch_shapes=[
            pltpu.VMEM((heads, n_chunks, RET_DK, RET_DV), F32),
            pltpu.VMEM((heads, n_chunks, RET_DK, RET_DV), F32),
            pltpu.VMEM((heads, 5, CHUNK, CHUNK), F32),
        ],
        compiler_params=pltpu.CompilerParams(
            dimension_semantics=("arbitrary", "arbitrary"), vmem_limit_bytes=VMEM_LIMIT),
        name="retention_init" if has_init else "retention_zero",
    )(*args)


def _route(logits):
    lane = lax.broadcasted_iota(jnp.int32, logits.shape, 1)
    lane_f = lane.astype(F32)
    neg = -jnp.inf
    far = float(LANES)
    is_g = lane < N_GROUPS
    lg = jnp.where(is_g, logits, neg)
    g_max = jnp.max(lg, axis=1, keepdims=True)
    g_idx = jnp.min(jnp.where(lg == g_max, lane_f, far), axis=1, keepdims=True)
    p_sel = 1.0 / jnp.sum(jnp.where(is_g, jnp.exp(lg - g_max), 0.0), axis=1, keepdims=True)
    lane_group = ((lane - N_GROUPS) >> 3).astype(F32)
    sel = (lane >= N_GROUPS) & (lane < N_GROUPS + N_EXPERTS) & (lane_group == g_idx)
    le = jnp.where(sel, logits, neg)
    v1 = jnp.max(le, axis=1, keepdims=True)
    i1 = jnp.min(jnp.where(le == v1, lane_f, far), axis=1, keepdims=True)
    le2 = jnp.where(lane_f == i1, neg, le)
    v2 = jnp.max(le2, axis=1, keepdims=True)
    i2 = jnp.min(jnp.where(le2 == v2, lane_f, far), axis=1, keepdims=True)
    e2 = jnp.exp(v2 - v1)
    w1 = p_sel * (1.0 / (1.0 + e2))
    w2 = p_sel * (e2 / (1.0 + e2))
    return lane, lane_f, g_idx, i1, i2, w1, w2


def _outproj_kernel(yc_ref, yr_ref, x_ref, g1_ref, sh_ref, sc_ref, ng_ref, wo_ref, wr_ref, br_ref,
                    cnt_in_ref, x1_ref, xr_ref, cnt_ref):
    m = (jnp.dot(yc_ref[...], wo_ref[0:CONV_W, :], preferred_element_type=F32)
         + jnp.dot(yr_ref[...], wo_ref[CONV_W:, :], preferred_element_type=F32))
    x1 = x_ref[...] + g1_ref[0, 0] * m
    x1_ref[...] = x1
    xn = (_rms(x1) * ng_ref[...]) * (1.0 + sc_ref[0, 0]) + sh_ref[0, 0]
    logits = jnp.dot(xn.astype(BF16), wr_ref[...], preferred_element_type=F32) + br_ref[...]
    lane, lane_f, g_idx, i1, i2, w1, w2 = _route(logits)

    @pl.when(pl.program_id(0) == 0)
    def _():
        cnt_ref[...] = cnt_in_ref[...]

    picks = jnp.where(lane_f == g_idx, 1.0, 0.0)
    rows = picks.shape[0]
    tri = (lax.broadcasted_iota(jnp.int32, (rows, rows), 0)
           > lax.broadcasted_iota(jnp.int32, (rows, rows), 1))
    before = jnp.dot(jnp.where(tri, 1.0, 0.0).astype(BF16), picks.astype(BF16),
                     preferred_element_type=F32) + cnt_ref[...]
    rank = jnp.sum(jnp.where(lane_f == g_idx, before, 0.0), axis=1, keepdims=True)
    cnt_ref[...] += jnp.sum(picks, axis=0, keepdims=True)
    route = jnp.where(lane == ROUTE_GROUP, g_idx, jnp.where(lane == ROUTE_RANK, rank, jnp.where(
        lane == ROUTE_E1, i1 - N_GROUPS, jnp.where(lane == ROUTE_E2, i2 - N_GROUPS, jnp.where(
            lane == ROUTE_W1, w1, jnp.where(lane == ROUTE_W2, w2, 0.0))))))
    xr_ref[:, :D_MODEL] = xn
    xr_ref[:, D_MODEL:] = route


def _outproj(y_conv, y_ret, x, mod4, mod_row_of_tile, norm_g, w_out_bf, w_router_bf, b_router,
             cnt_in):
    tokens = x.shape[0]
    tiles = tokens // OUTPROJ_TILE

    def mod_spec(which):
        return pl.BlockSpec((1, 1, 1, D_MODEL), lambda i: (mod_row_of_tile(i), which, 0, 0))

    tok = lambda w: pl.BlockSpec((OUTPROJ_TILE, w), lambda i: (i, 0))
    full = lambda a: pl.BlockSpec(a.shape, lambda i: (0,) * a.ndim)
    return pl.pallas_call(
        _outproj_kernel,
        grid=(tiles,),
        in_specs=[tok(CONV_W), tok(RET_W), tok(D_MODEL), mod_spec(2), mod_spec(3), mod_spec(4),
                  full(norm_g), full(w_out_bf), full(w_router_bf), full(b_router), full(cnt_in)],
        out_specs=[tok(D_MODEL), tok(ROW_W), full(cnt_in)],
        out_shape=[jax.ShapeDtypeStruct((tokens, D_MODEL), F32),
                   jax.ShapeDtypeStruct((tokens, ROW_W), F32),
                   jax.ShapeDtypeStruct(cnt_in.shape, F32)],
        compiler_params=pltpu.CompilerParams(
            dimension_semantics=("arbitrary",), vmem_limit_bytes=VMEM_LIMIT),
        name="outproj",
    )(y_conv, y_ret, x, mod4, mod4, mod4, norm_g, w_out_bf, w_router_bf, b_router, cnt_in)


def _row_copy(src, src_row, dst, dst_row, sem):
    return pltpu.make_async_copy(src.at[pl.ds(src_row, 1)], dst.at[pl.ds(dst_row, 1)], sem)


def _rows_wait(src, dst, rows, sem):
    pltpu.make_async_copy(src.at[pl.ds(0, rows)], dst.at[pl.ds(0, rows)], sem).wait()


def _dispatch_kernel(ctx_tiles, pad_lo_ref, pad_hi_ref, pos_ref, xr_ctx_hbm, xr_lat_hbm,
                     xs_hbm, buf, zero_buf, in_sem, out_sem):
    i = pl.program_id(0)
    last = pl.num_programs(0) - 1
    slot = i % 2

    def tile_load(t, s):
        @pl.when(t < ctx_tiles)
        def _():
            pltpu.make_async_copy(xr_ctx_hbm.at[pl.ds(t * TOKEN_TILE, TOKEN_TILE)], buf.at[s], in_sem.at[s]).start()

        @pl.when(t >= ctx_tiles)
        def _():
            pltpu.make_async_copy(xr_lat_hbm.at[pl.ds((t - ctx_tiles) * TOKEN_TILE, TOKEN_TILE)],
                                  buf.at[s], in_sem.at[s]).start()

    @pl.when(i == 0)
    def _():
        tile_load(i, 0)

    @pl.when(i > 0)
    def _():
        _rows_wait(buf.at[1 - slot], xs_hbm, TOKEN_TILE, out_sem.at[1 - slot])

    @pl.when(i < last)
    def _():
        tile_load(i + 1, 1 - slot)

    pltpu.make_async_copy(xr_lat_hbm.at[pl.ds(0, TOKEN_TILE)], buf.at[slot], in_sem.at[slot]).wait()

    def issue(r, carry):
        _row_copy(buf.at[slot], r, xs_hbm, pos_ref[0, 0, r], out_sem.at[slot]).start()
        return carry

    lax.fori_loop(0, TOKEN_TILE, issue, 0, unroll=8)

    @pl.when(i == last)
    def _():
        _rows_wait(buf.at[slot], xs_hbm, TOKEN_TILE, out_sem.at[slot])
        zero_buf[...] = jnp.zeros_like(zero_buf)
        for g in range(N_GROUPS):
            lo, hi = pad_lo_ref[g], pad_hi_ref[g]
            pad = lambda r: _row_copy(zero_buf, 0, xs_hbm, r, out_sem.at[0])
            lax.fori_loop(lo, hi, lambda r, c: (pad(r).start(), c)[1], 0)
            lax.fori_loop(lo, hi, lambda r, c: (pad(r).wait(), c)[1], 0)


def _dispatch(xr_ctx, xr_lat, pos_tiles, pad_lo, pad_hi, slots):
    ctx_tiles = xr_ctx.shape[0] // TOKEN_TILE
    tiles = ctx_tiles + xr_lat.shape[0] // TOKEN_TILE
    grid_spec = pltpu.PrefetchScalarGridSpec(
        num_scalar_prefetch=2,
        grid=(tiles,),
        in_specs=[
            pl.BlockSpec((1, 1, TOKEN_TILE), lambda i, lo, hi: (i, 0, 0), memory_space=pltpu.SMEM),
            pl.BlockSpec(memory_space=pl.ANY),
            pl.BlockSpec(memory_space=pl.ANY),
        ],
        out_specs=pl.BlockSpec(memory_space=pl.ANY),
        scratch_shapes=[pltpu.VMEM((2, TOKEN_TILE, ROW_W), F32), pltpu.VMEM((8, ROW_W), F32),
                        pltpu.SemaphoreType.DMA((2,)), pltpu.SemaphoreType.DMA((2,))],
    )
    return pl.pallas_call(
        functools.partial(_dispatch_kernel, ctx_tiles),
        grid_spec=grid_spec,
        out_shape=jax.ShapeDtypeStruct((slots, ROW_W), F32),
        compiler_params=pltpu.CompilerParams(dimension_semantics=("arbitrary",)),
        name="dispatch",
    )(pad_lo, pad_hi, pos_tiles, xr_ctx, xr_lat)


def _expert_kernel(tile_group_ref, n_used_ref, xs_ref, w1_ref, w3_ref, w2_ref, ys_ref, xb, gate_tabs):
    j = pl.program_id(0)
    step = pl.program_id(1)

    @pl.when(j < n_used_ref[0])
    def _():
        @pl.when(step == 0)
        def _():
            xb[...] = xs_ref[:, :D_MODEL].astype(BF16)
            route = xs_ref[:, D_MODEL:]
            lane = lax.broadcasted_iota(jnp.int32, route.shape, 1)
            for n, which in enumerate((ROUTE_E1, ROUTE_E2, ROUTE_W1, ROUTE_W2)):
                col = jnp.sum(jnp.where(lane == which, route, 0.0), axis=1, keepdims=True)
                gate_tabs[n] = jnp.broadcast_to(col, route.shape)

        x = xb[...]
        total = None
        for s in range(EXPERTS_PER_STEP):
            expert = (tile_group_ref[j] * EXPERTS_PER_GROUP + step * EXPERTS_PER_STEP + s).astype(F32)
            gate = (jnp.where(gate_tabs[0] == expert, gate_tabs[2], 0.0)
                    + jnp.where(gate_tabs[1] == expert, gate_tabs[3], 0.0))
            hid = _silu(jnp.dot(x, w1_ref[0, s].astype(BF16), preferred_element_type=F32)) * jnp.dot(
                x, w3_ref[0, s].astype(BF16), preferred_element_type=F32)
            y = jnp.dot(hid.astype(BF16), w2_ref[0, s].astype(BF16), preferred_element_type=F32)
            gated = jnp.concatenate(
                [gate * y[:, c * LANES:(c + 1) * LANES] for c in range(D_MODEL // LANES)], axis=1)
            total = gated if total is None else total + gated

        @pl.when(step == 0)
        def _():
            ys_ref[...] = total

        @pl.when(step > 0)
        def _():
            ys_ref[...] += total


def _experts(xs, tile_group, n_used, w1, w3, w2):
    slots = xs.shape[0]
    steps = EXPERTS_PER_GROUP // EXPERTS_PER_STEP
    paired = lambda w: w.reshape((N_EXPERTS // EXPERTS_PER_STEP, EXPERTS_PER_STEP) + w.shape[1:])
    row_tile = lambda j, s, tg, nu: (jnp.minimum(j, nu[0] - 1), 0)
    w_spec = lambda shape: pl.BlockSpec((1, EXPERTS_PER_STEP) + shape, lambda j, s, tg, nu: (
        tg[jnp.minimum(j, nu[0] - 1)] * steps + jnp.where(j < nu[0], s, steps - 1), 0, 0, 0))
    grid_spec = pltpu.PrefetchScalarGridSpec(
        num_scalar_prefetch=2,
        grid=(slots // GROUP_TILE, steps),
        in_specs=[
            pl.BlockSpec((GROUP_TILE, ROW_W), row_tile),
            w_spec((D_MODEL, D_EXPERT)), w_spec((D_MODEL, D_EXPERT)), w_spec((D_EXPERT, D_MODEL)),
        ],
        out_specs=pl.BlockSpec((GROUP_TILE, D_MODEL), row_tile),
        scratch_shapes=[pltpu.VMEM((GROUP_TILE, D_MODEL), BF16), pltpu.VMEM((4, GROUP_TILE, LANES), F32)],
    )
    return pl.pallas_call(
        _expert_kernel,
        grid_spec=grid_spec,
        out_shape=jax.ShapeDtypeStruct((slots, D_MODEL), F32),
        compiler_params=pltpu.CompilerParams(
            dimension_semantics=("arbitrary", "arbitrary"), vmem_limit_bytes=EXPERT_VMEM_LIMIT),
        name="experts",
    )(tile_group, n_used, xs, paired(w1), paired(w3), paired(w2))


def _combine_kernel(n_tiles, pos_ref, pos_next_ref, x1_ref, g2_ref, fg_ref, ys_hbm, o_ref, buf, sem):
    i = pl.program_id(0)
    slot = i % 2

    def issue(p_ref, s):
        def body(r, carry):
            _row_copy(ys_hbm, p_ref[0, 0, r], buf.at[s], r, sem.at[s]).start()
            return carry
        lax.fori_loop(0, TOKEN_TILE, body, 0, unroll=8)

    @pl.when(i == 0)
    def _():
        issue(pos_ref, 0)

    @pl.when(i + 1 < n_tiles)
    def _():
        issue(pos_next_ref, 1 - slot)

    _rows_wait(ys_hbm, buf.at[slot], TOKEN_TILE, sem.at[slot])
    y = x1_ref[...] + g2_ref[0, 0] * buf[slot]
    o_ref[...] = _rms(y) * fg_ref[...]


def _combine(ys, pos_tiles, x1, mod4, mod_row_of_tile, final_g, tile_base):
    tokens = x1.shape[0]
    tiles = tokens // TOKEN_TILE
    last = tile_base + tiles - 1
    pos_spec = lambda nxt: pl.BlockSpec(
        (1, 1, TOKEN_TILE), lambda i: (jnp.minimum(i + tile_base + nxt, last), 0, 0),
        memory_space=pltpu.SMEM)
    return pl.pallas_call(
        functools.partial(_combine_kernel, tiles),
        grid=(tiles,),
        in_specs=[
            pos_spec(0), pos_spec(1),
            pl.BlockSpec((TOKEN_TILE, D_MODEL), lambda i: (i, 0)),
            pl.BlockSpec((1, 1, 1, D_MODEL), lambda i: (mod_row_of_tile(i), 5, 0, 0)),
            pl.BlockSpec((1, D_MODEL), lambda i: (0, 0)),
            pl.BlockSpec(memory_space=pl.ANY),
        ],
        out_specs=pl.BlockSpec((TOKEN_TILE, D_MODEL), lambda i: (i, 0)),
        out_shape=jax.ShapeDtypeStruct((tokens, D_MODEL), F32),
        scratch_shapes=[pltpu.VMEM((2, TOKEN_TILE, D_MODEL), F32), pltpu.SemaphoreType.DMA((2,))],
        compiler_params=pltpu.CompilerParams(
            dimension_semantics=("arbitrary",), vmem_limit_bytes=VMEM_LIMIT),
        name="combine",
    )(pos_tiles, pos_tiles, x1, mod4, final_g, ys)


def _routing_tables(rows, counts):
    counts = counts.astype(jnp.int32)
    padded = ((counts + GROUP_TILE - 1) // GROUP_TILE) * GROUP_TILE
    ends = jnp.cumsum(padded)
    offs = ends - padded
    route = jnp.concatenate([r[:, D_MODEL:D_MODEL + 2] for r in rows], axis=0).astype(jnp.int32)
    group, rank = route[:, ROUTE_GROUP], route[:, ROUTE_RANK]
    onehot = group[:, None] == jnp.arange(N_GROUPS, dtype=jnp.int32)
    pos = jnp.sum(jnp.where(onehot, offs, 0), axis=-1) + rank
    pos_tiles = pos.reshape(-1, 1, TOKEN_TILE)
    n_used = ends[-1] // GROUP_TILE
    max_tiles = pos.shape[0] // GROUP_TILE + N_GROUPS
    tile_ids = jnp.minimum(jnp.arange(max_tiles, dtype=jnp.int32), n_used - 1)
    tile_group = jnp.sum(tile_ids[:, None] * GROUP_TILE >= ends[None, :], axis=1).astype(jnp.int32)
    return pos_tiles, offs + counts, ends, tile_group, n_used.reshape(1), max_tiles * GROUP_TILE


def _mixer(x, mod4, mod_row, is_grid, s_f0, s_b0, p):
    norm_mix_g, w_in_bf, conv_w, decay_rows = p
    y_conv, q, k, v, g = _inproj(x, mod4, mod_row, norm_mix_g, w_in_bf, conv_w, is_grid)
    ret = _retention(q, k, v, g, decay_rows, s_f0, s_b0, emit_state=not is_grid)
    flat = lambda a: a.reshape(-1, a.shape[-1])
    return flat(y_conv), flat(ret[0]), ret[1:]


def kernel(x_prompt, x_sample, state_ret_fwd, state_ret_bwd, c, c_ctx, norm_mix_g, norm_ffn_g, w_ada, b_ada, w_in, conv_w, ret_decay_fwd, ret_decay_bwd, w_out, w_router_group, b_router_group, w_router_expert, b_router_expert, w_gate_e, w_up_e, w_down_e, final_norm_g):
    assert norm_mix_g.shape[0] == 1, "single-layer backbone"
    n_lat = c.shape[0]
    ctx_row = n_lat
    mod_rows = 8
    cvec = jnp.concatenate([c, c_ctx[None, :], jnp.zeros((mod_rows - n_lat - 1, D_MODEL), F32)], axis=0)
    mod = _modulation(cvec, w_ada[0], b_ada[0][None, :])
    mod4 = mod.reshape(mod_rows, 6, 1, D_MODEL)

    pad = ROUTER_COLS - N_GROUPS - N_EXPERTS
    w_router = jnp.concatenate(
        [w_router_group[0], w_router_expert[0], jnp.zeros((D_MODEL, pad), F32)], axis=1).astype(BF16)
    b_router = jnp.concatenate([b_router_group[0], b_router_expert[0], jnp.zeros((pad,), F32)])[None, :]
    decay_rows = jnp.broadcast_to(
        jnp.concatenate([ret_decay_fwd[0], ret_decay_bwd[0]])[:, None], (2 * RET_HEADS, LANES))
    p_mix = (norm_mix_g, w_in[0].astype(BF16), conv_w[0], decay_rows)
    w_out_bf = w_out[0].astype(BF16)
    final_g = final_norm_g[None, :]

    ctx_tiles = (x_prompt.shape[0] * x_prompt.shape[1]) // TOKEN_TILE
    lat_len = x_sample.shape[1]
    ctx_mod = lambda i: ctx_row
    flat = lambda a: a.reshape(-1, a.shape[-1])

    yc_c, yr_c, (s_f, s_b) = _mixer(x_prompt, mod4, lambda b: ctx_row, False, None, None, p_mix)
    yc_l, yr_l, _ = _mixer(x_sample, mod4, lambda b: b, True, state_ret_fwd, state_ret_bwd, p_mix)

    cnt0 = jnp.zeros((1, ROUTER_COLS), F32)
    x1_c, xr_c, cnt_c = _outproj(
        yc_c, yr_c, flat(x_prompt), mod4, ctx_mod, norm_ffn_g, w_out_bf, w_router, b_router, cnt0)
    x1_l, xr_l, cnt = _outproj(
        yc_l, yr_l, flat(x_sample), mod4, lambda i: i // (lat_len // OUTPROJ_TILE), norm_ffn_g,
        w_out_bf, w_router, b_router, cnt_c)

    pos_tiles, pad_lo, pad_hi, tile_group, n_used, slots = _routing_tables(
        (xr_c, xr_l), cnt[0, :N_GROUPS])
    xs = _dispatch(xr_c, xr_l, pos_tiles, pad_lo, pad_hi, slots)
    ys = _experts(xs, tile_group, n_used, w_gate_e[0], w_up_e[0], w_down_e[0])
    y_prompt = _combine(ys, pos_tiles, x1_c, mod4, ctx_mod, final_g, 0)
    y_sample = _combine(ys, pos_tiles, x1_l, mod4, lambda i: i // (lat_len // TOKEN_TILE), final_g, ctx_tiles)
    return (y_prompt.reshape(x_prompt.shape), y_sample.reshape(x_sample.shape),
            s_f.astype(x_prompt.dtype), s_b.astype(x_prompt.dtype))
```

```python
import functools

import jax
import jax.numpy as jnp
from jax import lax
from jax.experimental import pallas as pl
from jax.experimental.pallas import tpu as pltpu

F32 = jnp.float32
BF16 = jnp.bfloat16

D_MODEL = 1024
GRID_W = 64
CONV_W = 512
RET_HEADS = 4
RET_DK = 128
RET_DV = 128
RET_W = RET_HEADS * RET_DV
QK_W = RET_HEADS * RET_DK
CHUNK = 128
N_GROUPS = 4
EXPERTS_PER_GROUP = 8
N_EXPERTS = N_GROUPS * EXPERTS_PER_GROUP
D_EXPERT = 256
ROPE_BASE = 10000.0
EPS = 1e-6

LANES = 128
TOKEN_TILE = 256
OUTPROJ_TILE = 512
GROUP_TILE = 1024
EXPERTS_PER_STEP = 2
RET_UNROLL = 8
SUBLANES = 8
LOCAL_ROWS = OUTPROJ_TILE + LANES
ROW_W = D_MODEL + LANES
ROUTE_GROUP, ROUTE_LOCAL, ROUTE_E1, ROUTE_E2, ROUTE_W1, ROUTE_W2 = range(6)
assert LOCAL_ROWS >= OUTPROJ_TILE + N_GROUPS * (SUBLANES - 1)
MOD_COLS = 1536
ROUTER_COLS = LANES
VMEM_LIMIT = 48 * 1024 * 1024
EXPERT_VMEM_LIMIT = 56 * 1024 * 1024


def _silu(x):
    return x * jax.nn.sigmoid(x)


def _rms(x):
    return x * lax.rsqrt(jnp.mean(x * x, axis=-1, keepdims=True) + EPS)


def _bdot(a, b):
    return jnp.dot(a.astype(BF16), b.astype(BF16), preferred_element_type=F32)


def _mod_kernel(c_ref, w_ref, b_ref, o_ref):
    o_ref[...] = _bdot(_silu(c_ref[...]), w_ref[...]) + b_ref[...]


def _modulation(cvec, w_ada, b_ada):
    rows = cvec.shape[0]
    n = w_ada.shape[1]
    return pl.pallas_call(
        _mod_kernel,
        grid=(n // MOD_COLS,),
        in_specs=[
            pl.BlockSpec((rows, D_MODEL), lambda j: (0, 0)),
            pl.BlockSpec((D_MODEL, MOD_COLS), lambda j: (0, j)),
            pl.BlockSpec((1, MOD_COLS), lambda j: (0, j)),
        ],
        out_specs=pl.BlockSpec((rows, MOD_COLS), lambda j: (0, j)),
        out_shape=jax.ShapeDtypeStruct((rows, n), F32),
        compiler_params=pltpu.CompilerParams(vmem_limit_bytes=VMEM_LIMIT),
        name="modulation",
    )(cvec, w_ada, b_ada)


def _inproj_kernel(seg, is_grid, x_ref, sh_ref, sc_ref, ng_ref, w_ref, cw_ref, *rest):
    if is_grid:
        cos_ref, sa_ref, sb_ref, yc_ref, q_ref, k_ref, v_ref, g_ref = rest
    else:
        yc_ref, q_ref, k_ref, v_ref, g_ref = rest
    x = x_ref[0]
    xn = (_rms(x) * ng_ref[...]) * (1.0 + sc_ref[0, 0]) + sh_ref[0, 0]
    xb = xn.astype(BF16)

    def proj(c0, n):
        return jnp.dot(xb, w_ref[:, c0:c0 + n], preferred_element_type=F32)

    gate_b = proj(0, CONV_W)
    u = proj(CONV_W, CONV_W) * proj(2 * CONV_W, CONV_W)
    rows = u.shape[0]
    pos = lax.broadcasted_iota(jnp.int32, u.shape, 0) & (seg - 1)
    u_prev = jnp.where(pos != 0, pltpu.roll(u, 1, 0), 0.0)
    u_next = jnp.where(pos != seg - 1, pltpu.roll(u, rows - 1, 0), 0.0)
    conv = cw_ref[0:1, :] * u_prev + cw_ref[1:2, :] * u + cw_ref[2:3, :] * u_next
    yc_ref[0] = (gate_b * conv).astype(yc_ref.dtype)

    q0 = 3 * CONV_W
    q = proj(q0, QK_W)
    k = proj(q0 + QK_W, QK_W)
    if is_grid:
        cos, sa, sb = cos_ref[...], sa_ref[...], sb_ref[...]

        def rope(t):
            out = []
            for h in range(RET_HEADS):
                th = t[:, h * RET_DK:(h + 1) * RET_DK]
                out.append(th * cos + pltpu.roll(th, RET_DK - 1, 1) * sa + pltpu.roll(th, 1, 1) * sb)
            return jnp.concatenate(out, axis=1)

        q, k = rope(q), rope(k)
    q_ref[0] = q
    k_ref[0] = k
    v_ref[0] = proj(q0 + 2 * QK_W, RET_W)
    g_ref[0] = proj(q0 + 2 * QK_W + RET_W, RET_W)


def _rope_tables(length):
    pos = jnp.arange(length)
    row = (pos // GRID_W).astype(F32)
    col = (pos % GRID_W).astype(F32)
    n_pairs = RET_DK // 4
    freqs = ROPE_BASE ** (-(jnp.arange(n_pairs, dtype=F32) * 2.0 / (RET_DK // 2)))
    ang = jnp.concatenate([row[:, None] * freqs, col[:, None] * freqs], axis=-1)
    cos = jnp.repeat(jnp.cos(ang), 2, axis=-1)
    sin = jnp.repeat(jnp.sin(ang), 2, axis=-1)
    even = (jnp.arange(RET_DK) % 2) == 0
    return cos, jnp.where(even, -sin, 0.0), jnp.where(even, 0.0, sin)


def _inproj(x, mod4, mod_row, norm_g, w_in_bf, conv_w, is_grid):
    bsz, length, _ = x.shape
    seg = GRID_W if is_grid else length
    assert TOKEN_TILE % seg == 0 and length % TOKEN_TILE == 0
    tiles = length // TOKEN_TILE

    def mod_spec(which):
        return pl.BlockSpec((1, 1, 1, D_MODEL), lambda b, i: (mod_row(b), which, 0, 0))

    def tok_spec(width):
        return pl.BlockSpec((1, TOKEN_TILE, width), lambda b, i: (b, i, 0))

    in_specs = [
        tok_spec(D_MODEL), mod_spec(0), mod_spec(1),
        pl.BlockSpec((1, D_MODEL), lambda b, i: (0, 0)),
        pl.BlockSpec(w_in_bf.shape, lambda b, i: (0, 0)),
        pl.BlockSpec(conv_w.shape, lambda b, i: (0, 0)),
    ]
    args = [x, mod4, mod4, norm_g, w_in_bf, conv_w]
    if is_grid:
        in_specs += [pl.BlockSpec((TOKEN_TILE, RET_DK), lambda b, i: (i, 0))] * 3
        args += list(_rope_tables(length))
    shp = lambda w, dt: jax.ShapeDtypeStruct((bsz, length, w), dt)
    return pl.pallas_call(
        functools.partial(_inproj_kernel, seg, is_grid),
        grid=(bsz, tiles),
        in_specs=in_specs,
        out_specs=[tok_spec(CONV_W), tok_spec(QK_W), tok_spec(QK_W), tok_spec(RET_W), tok_spec(RET_W)],
        out_shape=[shp(CONV_W, BF16), shp(QK_W, F32), shp(QK_W, F32), shp(RET_W, F32), shp(RET_W, F32)],
        compiler_params=pltpu.CompilerParams(
            dimension_semantics=("parallel", "parallel"), vmem_limit_bytes=VMEM_LIMIT),
        name="inproj_grid" if is_grid else "inproj_seq",
    )(*args)


def _ret_kernel(n_chunks, heads, has_init, emit_state, a_ref, q_ref, k_ref, v_ref, g_ref, *rest):
    rest = list(rest)
    if has_init:
        sf0_ref, sb0_ref = rest[:2]
        rest = rest[2:]
    y_ref = rest.pop(0)
    if emit_state:
        sf_out, sb_out = rest[:2]
        rest = rest[2:]
    st_f, st_b, dec = rest
    c = CHUNK
    sq = (c, c)
    head0 = pl.program_id(0) * heads

    def log_decays(hh):
        lg_f = jnp.log1p(-jnp.exp(a_ref[pl.ds(head0 + hh, 1), :]))
        lg_b = jnp.log1p(-jnp.exp(a_ref[pl.ds(head0 + hh + RET_HEADS, 1), :]))
        return lg_f, lg_b

    @pl.when(pl.program_id(1) == 0)
    def _():
        row = lax.broadcasted_iota(jnp.int32, sq, 0).astype(F32)
        col = lax.broadcasted_iota(jnp.int32, sq, 1).astype(F32)
        scale = RET_DK ** -0.5
        for hh in range(heads):
            lg_f, lg_b = log_decays(hh)
            dec[hh, 0] = scale * (
                jnp.where(row >= col, jnp.exp(jnp.where(row >= col, row - col, 0.0) * lg_f), 0.0)
                + jnp.where(col >= row, jnp.exp(jnp.where(col >= row, col - row, 0.0) * lg_b), 0.0))
            dec[hh, 1] = jnp.exp((row + 1.0) * lg_f)
            dec[hh, 2] = jnp.exp((c - row) * lg_b)
            dec[hh, 3] = scale * jnp.exp((c - 1.0 - col) * lg_f)
            dec[hh, 4] = scale * jnp.exp(col * lg_b)

    def rows(n):
        return pl.ds(pl.multiple_of(n * c, c), c) if not isinstance(n, int) else pl.ds(n * c, c)

    def cols(hh):
        return slice(hh * RET_DK, (hh + 1) * RET_DK)

    def kv_step(hh, n):
        kt = jnp.transpose(k_ref[0, rows(n), cols(hh)])
        lhs = jnp.concatenate([kt * dec[hh, 3], kt * dec[hh, 4]], axis=0)
        kv = _bdot(lhs, v_ref[0, rows(n), cols(hh)])
        st_f[hh, n] = kv[:RET_DK]
        st_b[hh, n] = kv[RET_DK:]

    def scan(hh, st, decay, order, s):
        def step(i, s):
            n = order(i)
            kv = st[hh, n]
            st[hh, n] = s
            return s * decay + kv
        if n_chunks <= RET_UNROLL:
            for i in range(n_chunks):
                s = step(i, s)
            return s
        return lax.fori_loop(0, n_chunks, step, s, unroll=RET_UNROLL)

    def out_step(hh, n):
        q = q_ref[0, rows(n), cols(hh)]
        scores = lax.dot_general(q.astype(BF16), k_ref[0, rows(n), cols(hh)].astype(BF16),
                                 (((1,), (1,)), ((), ())), preferred_element_type=F32)
        o = _bdot(scores * dec[hh, 0], v_ref[0, rows(n), cols(hh)])
        q_dec = jnp.concatenate([q * dec[hh, 1], q * dec[hh, 2]], axis=1)
        o = o + _bdot(q_dec, jnp.concatenate([st_f[hh, n], st_b[hh, n]], axis=0))
        y = _silu(g_ref[0, rows(n), cols(hh)]) * _rms(o)
        y_ref[0, rows(n), cols(hh)] = y.astype(y_ref.dtype)

    def over_chunks(step):
        if n_chunks * heads <= RET_UNROLL:
            for hh in range(heads):
                for n in range(n_chunks):
                    step(hh, n)
        else:
            for hh in range(heads):
                lax.fori_loop(0, n_chunks, lambda n, carry: (step(hh, n), carry)[1], 0, unroll=RET_UNROLL)

    over_chunks(kv_step)
    finals = []
    for hh in range(heads):
        lg_f, lg_b = log_decays(hh)
        s_f = sf0_ref[0, 0, hh] if has_init else jnp.zeros(sq, F32)
        s_b = sb0_ref[0, 0, hh] if has_init else jnp.zeros(sq, F32)
        s_f = scan(hh, st_f, jnp.exp(c * lg_f), lambda i: i, s_f)
        s_b = scan(hh, st_b, jnp.exp(c * lg_b), lambda i: n_chunks - 1 - i, s_b)
        finals.append((s_f, s_b))
    over_chunks(out_step)
    if emit_state:
        for hh, (s_f, s_b) in enumerate(finals):
            sf_out[0, 0, hh] = s_f
            sb_out[0, 0, hh] = s_b


def _retention(q, k, v, g, decay_rows, s_f0, s_b0, emit_state):
    bsz, length, _ = q.shape
    n_chunks = length // CHUNK
    has_init = s_f0 is not None
    heads = RET_HEADS if n_chunks * RET_HEADS <= RET_UNROLL else 1
    head_spec = pl.BlockSpec((1, length, heads * RET_DK), lambda h, b: (b, 0, h))
    st_spec = pl.BlockSpec((1, 1, heads, RET_DK, RET_DV), lambda h, b: (b, 0, h, 0, 0))
    in_specs = [pl.BlockSpec(decay_rows.shape, lambda h, b: (0, 0))] + [head_spec] * 4
    args = [decay_rows, q, k, v, g]
    if has_init:
        in_specs += [st_spec, st_spec]
        args += [s_f0, s_b0]
    out_specs = [head_spec]
    out_shape = [jax.ShapeDtypeStruct((bsz, length, RET_W), BF16)]
    if emit_state:
        st_shape = jax.ShapeDtypeStruct((bsz, 1, RET_HEADS, RET_DK, RET_DV), F32)
        out_specs += [st_spec, st_spec]
        out_shape += [st_shape, st_shape]
    return pl.pallas_call(
        functools.partial(_ret_kernel, n_chunks, heads, has_init, emit_state),
        grid=(RET_HEADS // heads, bsz),
        in_specs=in_specs,
        out_specs=out_specs,
        out_shape=out_shape,
        scratch_shapes=[
            pltpu.VMEM((heads, n_chunks, RET_DK, RET_DV), F32),
            pltpu.VMEM((heads, n_chunks, RET_DK, RET_DV), F32),
            pltpu.VMEM((heads, 5, CHUNK, CHUNK), F32),
        ],
        compiler_params=pltpu.CompilerParams(
            dimension_semantics=("arbitrary", "arbitrary"), vmem_limit_bytes=VMEM_LIMIT),
        name="retention_init" if has_init else "retention_zero",
    )(*args)


def _route(logits):
    lane = lax.broadcasted_iota(jnp.int32, logits.shape, 1)
    lane_f = lane.astype(F32)
    neg = -jnp.inf
    far = float(LANES)
    is_g = lane < N_GROUPS
    lg = jnp.where(is_g, logits, neg)
    g_max = jnp.max(lg, axis=1, keepdims=True)
    g_idx = jnp.min(jnp.where(lg == g_max, lane_f, far), axis=1, keepdims=True)
    p_sel = 1.0 / jnp.sum(jnp.where(is_g, jnp.exp(lg - g_max), 0.0), axis=1, keepdims=True)
    lane_group = ((lane - N_GROUPS) >> 3).astype(F32)
    sel = (lane >= N_GROUPS) & (lane < N_GROUPS + N_EXPERTS) & (lane_group == g_idx)
    le = jnp.where(sel, logits, neg)
    v1 = jnp.max(le, axis=1, keepdims=True)
    i1 = jnp.min(jnp.where(le == v1, lane_f, far), axis=1, keepdims=True)
    le2 = jnp.where(lane_f == i1, neg, le)
    v2 = jnp.max(le2, axis=1, keepdims=True)
    i2 = jnp.min(jnp.where(le2 == v2, lane_f, far), axis=1, keepdims=True)
    e2 = jnp.exp(v2 - v1)
    w1 = p_sel * (1.0 / (1.0 + e2))
    w2 = p_sel * (e2 / (1.0 + e2))
    return lane, lane_f, g_idx, i1, i2, w1, w2


def _outproj_kernel(yc_ref, yr_ref, x_ref, g1_ref, sh_ref, sc_ref, ng_ref, wo_ref, wr_ref, br_ref,
                    x1_ref, xloc_ref, route_ref, cnt_ref):
    m = (jnp.dot(yc_ref[...], wo_ref[0:CONV_W, :], preferred_element_type=F32)
         + jnp.dot(yr_ref[...], wo_ref[CONV_W:, :], preferred_element_type=F32))
    x1 = x_ref[...] + g1_ref[0, 0] * m
    x1_ref[...] = x1
    xn = (_rms(x1) * ng_ref[...]) * (1.0 + sc_ref[0, 0]) + sh_ref[0, 0]
    xb = xn.astype(BF16)
    logits = jnp.dot(xb, wr_ref[...], preferred_element_type=F32) + br_ref[...]
    lane, lane_f, g_idx, i1, i2, w1, w2 = _route(logits)

    picks = jnp.where(lane_f == g_idx, 1.0, 0.0)
    rows = picks.shape[0]
    tri = (lax.broadcasted_iota(jnp.int32, (rows, rows), 0)
           > lax.broadcasted_iota(jnp.int32, (rows, rows), 1))
    before = jnp.dot(jnp.where(tri, 1.0, 0.0).astype(BF16), picks.astype(BF16),
                     preferred_element_type=F32)
    count = jnp.sum(picks, axis=0, keepdims=True)
    cnt_ref[0] = count
    count8 = jnp.broadcast_to(jnp.floor((count + (SUBLANES - 1)) * (1.0 / SUBLANES)) * SUBLANES,
                              (SUBLANES, LANES))
    lane8 = lane[:SUBLANES]
    start = sum(jnp.where(lane8 >= k, pltpu.roll(count8, k, 1), 0.0) for k in range(1, N_GROUPS))
    local = jnp.sum(jnp.where(lane_f == g_idx, before + start[0:1], 0.0), axis=1, keepdims=True)
    route = jnp.where(lane == ROUTE_GROUP, g_idx, jnp.where(lane == ROUTE_LOCAL, local, jnp.where(
        lane == ROUTE_E1, i1 - N_GROUPS, jnp.where(lane == ROUTE_E2, i2 - N_GROUPS, jnp.where(
            lane == ROUTE_W1, w1, jnp.where(lane == ROUTE_W2, w2, 0.0))))))
    route_ref[...] = route

    local_row = jnp.transpose(jnp.broadcast_to(local, (rows, LANES)))[0:1, :]
    place = jnp.where(lax.broadcasted_iota(jnp.int32, (LOCAL_ROWS, rows), 0).astype(F32) == local_row,
                      1.0, 0.0).astype(BF16)
    xloc_ref[0, :, :D_MODEL] = jnp.dot(place, xb, preferred_element_type=F32)
    hi = route.astype(BF16)
    rest = route - hi.astype(F32)
    mid = rest.astype(BF16)
    low = (rest - mid.astype(F32)).astype(BF16)
    xloc_ref[0, :, D_MODEL:] = (jnp.dot(place, hi, preferred_element_type=F32)
                                + jnp.dot(place, mid, preferred_element_type=F32)
                                + jnp.dot(place, low, preferred_element_type=F32))


def _outproj(y_conv, y_ret, x, mod4, mod_row_of_tile, norm_g, w_out_bf, w_router_bf, b_router):
    tokens = x.shape[0]
    tiles = tokens // OUTPROJ_TILE

    def mod_spec(which):
        return pl.BlockSpec((1, 1, 1, D_MODEL), lambda i: (mod_row_of_tile(i), which, 0, 0))

    tok = lambda w: pl.BlockSpec((OUTPROJ_TILE, w), lambda i: (i, 0))
    full = lambda a: pl.BlockSpec(a.shape, lambda i: (0,) * a.ndim)
    return pl.pallas_call(
        _outproj_kernel,
        grid=(tiles,),
        in_specs=[tok(CONV_W), tok(RET_W), tok(D_MODEL), mod_spec(2), mod_spec(3), mod_spec(4),
                  full(norm_g), full(w_out_bf), full(w_router_bf), full(b_router)],
        out_specs=[tok(D_MODEL),
                   pl.BlockSpec((1, LOCAL_ROWS, ROW_W), lambda i: (i, 0, 0)),
                   tok(ROUTER_COLS),
                   pl.BlockSpec((1, 1, ROUTER_COLS), lambda i: (i, 0, 0))],
        out_shape=[jax.ShapeDtypeStruct((tokens, D_MODEL), F32),
                   jax.ShapeDtypeStruct((tiles, LOCAL_ROWS, ROW_W), F32),
                   jax.ShapeDtypeStruct((tokens, ROUTER_COLS), F32),
                   jax.ShapeDtypeStruct((tiles, 1, ROUTER_COLS), F32)],
        compiler_params=pltpu.CompilerParams(
            dimension_semantics=("parallel",), vmem_limit_bytes=VMEM_LIMIT),
        name="outproj",
    )(y_conv, y_ret, x, mod4, mod4, mod4, norm_g, w_out_bf, w_router_bf, b_router)


def _row_copy(src, src_row, dst, dst_row, sem):
    return pltpu.make_async_copy(src.at[pl.ds(src_row, 1)], dst.at[pl.ds(dst_row, 1)], sem)


def _rows_wait(src, dst, rows, sem):
    pltpu.make_async_copy(src.at[pl.ds(0, rows)], dst.at[pl.ds(0, rows)], sem).wait()


RUN_PIECES = tuple(SUBLANES << b for b in reversed(range((OUTPROJ_TILE // SUBLANES).bit_length())))


def _dispatch_kernel(run_len_ref, run_src_ref, run_dst_ref, pad_lo_ref, pad_hi_ref, xloc_ctx_hbm, xloc_lat_hbm,
                     xs_hbm, zero_buf, sem):
    ctx_tiles = xloc_ctx_hbm.shape[0]
    lat_tiles = xloc_lat_hbm.shape[0]

    def run_copies(xloc_hbm, local_tile, tile, act):
        for g in range(N_GROUPS):
            n = run_len_ref[tile * N_GROUPS + g]
            src = run_src_ref[tile * N_GROUPS + g]
            dst = run_dst_ref[tile * N_GROUPS + g]
            for size in RUN_PIECES:
                done = n & (-2 * size)

                @pl.when((n & size) != 0)
                def _():
                    act(pltpu.make_async_copy(
                        xloc_hbm.at[local_tile, pl.ds(pl.multiple_of(src + done, SUBLANES), size)],
                        xs_hbm.at[pl.ds(pl.multiple_of(dst + done, SUBLANES), size)], sem))

    def zero_copy(r):
        return pltpu.make_async_copy(zero_buf, xs_hbm.at[pl.ds(pl.multiple_of(r * SUBLANES, SUBLANES), SUBLANES)], sem)

    def every_copy(act):
        def ctx_body(t, carry):
            run_copies(xloc_ctx_hbm, t, t, act)
            return carry

        def lat_body(t, carry):
            run_copies(xloc_lat_hbm, t, ctx_tiles + t, act)
            return carry

        lax.fori_loop(0, ctx_tiles, ctx_body, 0)
        lax.fori_loop(0, lat_tiles, lat_body, 0)
        for g in range(N_GROUPS):
            lax.fori_loop(pad_lo_ref[g], pad_hi_ref[g], lambda r, carry: (act(zero_copy(r)), carry)[1], 0)

    zero_buf[...] = jnp.zeros_like(zero_buf)
    every_copy(lambda cp: cp.start())
    every_copy(lambda cp: cp.wait())


def _dispatch(xloc_ctx, xloc_lat, run_len, run_src, run_dst, pad_lo, pad_hi, slots):
    grid_spec = pltpu.PrefetchScalarGridSpec(
        num_scalar_prefetch=5,
        grid=(1,),
        in_specs=[pl.BlockSpec(memory_space=pl.ANY), pl.BlockSpec(memory_space=pl.ANY)],
        out_specs=pl.BlockSpec(memory_space=pl.ANY),
        scratch_shapes=[pltpu.VMEM((SUBLANES, ROW_W), F32), pltpu.SemaphoreType.DMA],
    )
    return pl.pallas_call(
        _dispatch_kernel,
        grid_spec=grid_spec,
        out_shape=jax.ShapeDtypeStruct((slots, ROW_W), F32),
        compiler_params=pltpu.CompilerParams(dimension_semantics=("arbitrary",)),
        name="dispatch",
    )(run_len, run_src, run_dst, pad_lo, pad_hi, xloc_ctx, xloc_lat)


def _expert_kernel(tile_group_ref, n_used_ref, xs_ref, w1_ref, w3_ref, w2_ref, ys_ref, xb, gate_tabs):
    j = pl.program_id(0)
    step = pl.program_id(1)

    @pl.when(j < n_used_ref[0])
    def _():
        @pl.when(step == 0)
        def _():
            xb[...] = xs_ref[:, :D_MODEL].astype(BF16)
            route = xs_ref[:, D_MODEL:]
            lane = lax.broadcasted_iota(jnp.int32, route.shape, 1)
            for n, which in enumerate((ROUTE_E1, ROUTE_E2, ROUTE_W1, ROUTE_W2)):
                col = jnp.sum(jnp.where(lane == which, route, 0.0), axis=1, keepdims=True)
                gate_tabs[n] = jnp.broadcast_to(col, route.shape)

        x = xb[...]
        total = None
        for s in range(EXPERTS_PER_STEP):
            expert = (tile_group_ref[j] * EXPERTS_PER_GROUP + step * EXPERTS_PER_STEP + s).astype(F32)
            gate = (jnp.where(gate_tabs[0] == expert, gate_tabs[2], 0.0)
                    + jnp.where(gate_tabs[1] == expert, gate_tabs[3], 0.0))
            hid = _silu(jnp.dot(x, w1_ref[0, s].astype(BF16), preferred_element_type=F32)) * jnp.dot(
                x, w3_ref[0, s].astype(BF16), preferred_element_type=F32)
            y = jnp.dot(hid.astype(BF16), w2_ref[0, s].astype(BF16), preferred_element_type=F32)
            gated = jnp.concatenate(
                [gate * y[:, c * LANES:(c + 1) * LANES] for c in range(D_MODEL // LANES)], axis=1)
            total = gated if total is None else total + gated

        @pl.when(step == 0)
        def _():
            ys_ref[...] = total

        @pl.when(step > 0)
        def _():
            ys_ref[...] += total


def _experts(xs, tile_group, n_used, w1, w3, w2):
    slots = xs.shape[0]
    steps = EXPERTS_PER_GROUP // EXPERTS_PER_STEP
    paired = lambda w: w.reshape((N_EXPERTS // EXPERTS_PER_STEP, EXPERTS_PER_STEP) + w.shape[1:])
    row_tile = lambda j, s, tg, nu: (jnp.minimum(j, nu[0] - 1), 0)
    w_spec = lambda shape: pl.BlockSpec((1, EXPERTS_PER_STEP) + shape, lambda j, s, tg, nu: (
        tg[jnp.minimum(j, nu[0] - 1)] * steps + jnp.where(j < nu[0], s, steps - 1), 0, 0, 0))
    grid_spec = pltpu.PrefetchScalarGridSpec(
        num_scalar_prefetch=2,
        grid=(slots // GROUP_TILE, steps),
        in_specs=[
            pl.BlockSpec((GROUP_TILE, ROW_W), row_tile),
            w_spec((D_MODEL, D_EXPERT)), w_spec((D_MODEL, D_EXPERT)), w_spec((D_EXPERT, D_MODEL)),
        ],
        out_specs=pl.BlockSpec((GROUP_TILE, D_MODEL), row_tile),
        scratch_shapes=[pltpu.VMEM((GROUP_TILE, D_MODEL), BF16), pltpu.VMEM((4, GROUP_TILE, LANES), F32)],
    )
    return pl.pallas_call(
        _expert_kernel,
        grid_spec=grid_spec,
        out_shape=jax.ShapeDtypeStruct((slots, D_MODEL), F32),
        compiler_params=pltpu.CompilerParams(
            dimension_semantics=("arbitrary", "arbitrary"), vmem_limit_bytes=EXPERT_VMEM_LIMIT),
        name="experts",
    )(tile_group, n_used, xs, paired(w1), paired(w3), paired(w2))


def _combine_kernel(n_tiles, pos_ref, pos_next_ref, x1_ref, g2_ref, fg_ref, ys_hbm, o_ref, buf, sem):
    i = pl.program_id(0)
    slot = i % 2

    def issue(p_ref, s):
        def body(r, carry):
            _row_copy(ys_hbm, p_ref[0, 0, r], buf.at[s], r, sem.at[s]).start()
            return carry
        lax.fori_loop(0, TOKEN_TILE, body, 0, unroll=8)

    @pl.when(i == 0)
    def _():
        issue(pos_ref, 0)

    @pl.when(i + 1 < n_tiles)
    def _():
        issue(pos_next_ref, 1 - slot)

    _rows_wait(ys_hbm, buf.at[slot], TOKEN_TILE, sem.at[slot])
    y = x1_ref[...] + g2_ref[0, 0] * buf[slot]
    o_ref[...] = _rms(y) * fg_ref[...]


def _combine(ys, pos_tiles, x1, mod4, mod_row_of_tile, final_g, tile_base):
    tokens = x1.shape[0]
    tiles = tokens // TOKEN_TILE
    last = tile_base + tiles - 1
    pos_spec = lambda nxt: pl.BlockSpec(
        (1, 1, TOKEN_TILE), lambda i: (jnp.minimum(i + tile_base + nxt, last), 0, 0),
        memory_space=pltpu.SMEM)
    return pl.pallas_call(
        functools.partial(_combine_kernel, tiles),
        grid=(tiles,),
        in_specs=[
            pos_spec(0), pos_spec(1),
            pl.BlockSpec((TOKEN_TILE, D_MODEL), lambda i: (i, 0)),
            pl.BlockSpec((1, 1, 1, D_MODEL), lambda i: (mod_row_of_tile(i), 5, 0, 0)),
            pl.BlockSpec((1, D_MODEL), lambda i: (0, 0)),
            pl.BlockSpec(memory_space=pl.ANY),
        ],
        out_specs=pl.BlockSpec((TOKEN_TILE, D_MODEL), lambda i: (i, 0)),
        out_shape=jax.ShapeDtypeStruct((tokens, D_MODEL), F32),
        scratch_shapes=[pltpu.VMEM((2, TOKEN_TILE, D_MODEL), F32), pltpu.SemaphoreType.DMA((2,))],
        compiler_params=pltpu.CompilerParams(
            dimension_semantics=("arbitrary",), vmem_limit_bytes=VMEM_LIMIT),
        name="combine",
    )(pos_tiles, pos_tiles, x1, mod4, final_g, ys)


def _routing_tables(routes, counts):
    counts = counts.astype(jnp.int32)
    tiles = counts.shape[0]
    run_len = ((counts + SUBLANES - 1) // SUBLANES) * SUBLANES
    run_src = jnp.cumsum(run_len, axis=1) - run_len
    group_rows = jnp.sum(run_len, axis=0)
    padded = ((group_rows + GROUP_TILE - 1) // GROUP_TILE) * GROUP_TILE
    ends = jnp.cumsum(padded)
    offs = ends - padded
    run_dst = offs[None, :] + jnp.cumsum(run_len, axis=0) - run_len

    route = jnp.concatenate([r[:, :2] for r in routes], axis=0).astype(jnp.int32)
    group = route[:, ROUTE_GROUP].reshape(tiles, OUTPROJ_TILE)
    local = route[:, ROUTE_LOCAL].reshape(tiles, OUTPROJ_TILE)
    onehot = group[:, :, None] == jnp.arange(N_GROUPS, dtype=jnp.int32)
    pos = local + jnp.sum(jnp.where(onehot, (run_dst - run_src)[:, None, :], 0), axis=-1)
    pos_tiles = pos.reshape(-1, 1, TOKEN_TILE)

    n_used = ends[-1] // GROUP_TILE
    max_rows = tiles * (OUTPROJ_TILE + N_GROUPS * (SUBLANES - 1))
    max_tiles = -(-max_rows // GROUP_TILE) + N_GROUPS
    tile_ids = jnp.minimum(jnp.arange(max_tiles, dtype=jnp.int32), n_used - 1)
    tile_group = jnp.sum(tile_ids[:, None] * GROUP_TILE >= ends[None, :], axis=1).astype(jnp.int32)
    flat = lambda a: a.reshape(-1)
    pads = ((offs + group_rows) // SUBLANES, ends // SUBLANES)
    return pos_tiles, (flat(run_len), flat(run_src), flat(run_dst)) + pads, tile_group, n_used.reshape(1), max_tiles * GROUP_TILE


def _mixer(x, mod4, mod_row, is_grid, s_f0, s_b0, p):
    norm_mix_g, w_in_bf, conv_w, decay_rows = p
    y_conv, q, k, v, g = _inproj(x, mod4, mod_row, norm_mix_g, w_in_bf, conv_w, is_grid)
    ret = _retention(q, k, v, g, decay_rows, s_f0, s_b0, emit_state=not is_grid)
    flat = lambda a: a.reshape(-1, a.shape[-1])
    return flat(y_conv), flat(ret[0]), ret[1:]


def kernel(x_prompt, x_sample, state_ret_fwd, state_ret_bwd, c, c_ctx, norm_mix_g, norm_ffn_g, w_ada, b_ada, w_in, conv_w, ret_decay_fwd, ret_decay_bwd, w_out, w_router_group, b_router_group, w_router_expert, b_router_expert, w_gate_e, w_up_e, w_down_e, final_norm_g):
    assert norm_mix_g.shape[0] == 1, "single-layer backbone"
    n_lat = c.shape[0]
    ctx_row = n_lat
    mod_rows = 8
    cvec = jnp.concatenate([c, c_ctx[None, :], jnp.zeros((mod_rows - n_lat - 1, D_MODEL), F32)], axis=0)
    mod = _modulation(cvec, w_ada[0], b_ada[0][None, :])
    mod4 = mod.reshape(mod_rows, 6, 1, D_MODEL)

    pad = ROUTER_COLS - N_GROUPS - N_EXPERTS
    w_router = jnp.concatenate(
        [w_router_group[0], w_router_expert[0], jnp.zeros((D_MODEL, pad), F32)], axis=1).astype(BF16)
    b_router = jnp.concatenate([b_router_group[0], b_router_expert[0], jnp.zeros((pad,), F32)])[None, :]
    decay_rows = jnp.broadcast_to(
        jnp.concatenate([ret_decay_fwd[0], ret_decay_bwd[0]])[:, None], (2 * RET_HEADS, LANES))
    p_mix = (norm_mix_g, w_in[0].astype(BF16), conv_w[0], decay_rows)
    w_out_bf = w_out[0].astype(BF16)
    final_g = final_norm_g[None, :]

    ctx_tiles = (x_prompt.shape[0] * x_prompt.shape[1]) // TOKEN_TILE
    lat_len = x_sample.shape[1]
    ctx_mod = lambda i: ctx_row
    flat = lambda a: a.reshape(-1, a.shape[-1])

    yc_c, yr_c, (s_f, s_b) = _mixer(x_prompt, mod4, lambda b: ctx_row, False, None, None, p_mix)
    yc_l, yr_l, _ = _mixer(x_sample, mod4, lambda b: b, True, state_ret_fwd, state_ret_bwd, p_mix)

    x1_c, xloc_c, route_c, cnt_c = _outproj(
        yc_c, yr_c, flat(x_prompt), mod4, ctx_mod, norm_ffn_g, w_out_bf, w_router, b_router)
    x1_l, xloc_l, route_l, cnt_l = _outproj(
        yc_l, yr_l, flat(x_sample), mod4, lambda i: i // (lat_len // OUTPROJ_TILE), norm_ffn_g,
        w_out_bf, w_router, b_router)

    counts = jnp.concatenate([cnt_c, cnt_l], axis=0)[:, 0, :N_GROUPS]
    pos_tiles, runs, tile_group, n_used, slots = _routing_tables((route_c, route_l), counts)
    xs = _dispatch(xloc_c, xloc_l, *runs, slots)
    ys = _experts(xs, tile_group, n_used, w_gate_e[0], w_up_e[0], w_down_e[0])
    y_prompt = _combine(ys, pos_tiles, x1_c, mod4, ctx_mod, final_g, 0)
    y_sample = _combine(ys, pos_tiles, x1_l, mod4, lambda i: i // (lat_len // TOKEN_TILE), final_g, ctx_tiles)
    return (y_prompt.reshape(x_prompt.shape), y_sample.reshape(x_sample.shape),
            s_f.astype(x_prompt.dtype), s_b.astype(x_prompt.dtype))
```

```python
import functools

import jax
import jax.numpy as jnp
from jax import lax
from jax.experimental import pallas as pl
from jax.experimental.pallas import tpu as pltpu

F32 = jnp.float32
BF16 = jnp.bfloat16

D_MODEL = 1024
GRID_W = 64
CONV_W = 512
RET_HEADS = 4
RET_DK = 128
RET_DV = 128
RET_W = RET_HEADS * RET_DV
QK_W = RET_HEADS * RET_DK
CHUNK = 128
N_GROUPS = 4
EXPERTS_PER_GROUP = 8
N_EXPERTS = N_GROUPS * EXPERTS_PER_GROUP
D_EXPERT = 256
ROPE_BASE = 10000.0
EPS = 1e-6

LANES = 128
TOKEN_TILE = 256
OUTPROJ_TILE = 512
GROUP_TILE = 1024
EXPERTS_PER_STEP = 2
RET_UNROLL = 8
SUBLANES = 8
LOCAL_ROWS = OUTPROJ_TILE + LANES
ROW_W = D_MODEL + LANES
ROUTE_GROUP, ROUTE_LOCAL, ROUTE_E1, ROUTE_E2, ROUTE_W1, ROUTE_W2 = range(6)
assert LOCAL_ROWS >= OUTPROJ_TILE + N_GROUPS * (SUBLANES - 1)
MOD_COLS = 1536
ROUTER_COLS = LANES
VMEM_LIMIT = 48 * 1024 * 1024
EXPERT_VMEM_LIMIT = 56 * 1024 * 1024


def _silu(x):
    return x * jax.nn.sigmoid(x)


def _rms(x):
    return x * lax.rsqrt(jnp.mean(x * x, axis=-1, keepdims=True) + EPS)


def _bdot(a, b):
    return jnp.dot(a.astype(BF16), b.astype(BF16), preferred_element_type=F32)


def _mod_kernel(c_ref, w_ref, b_ref, o_ref):
    o_ref[...] = _bdot(_silu(c_ref[...]), w_ref[...]) + b_ref[...]


def _modulation(cvec, w_ada, b_ada):
    rows = cvec.shape[0]
    n = w_ada.shape[1]
    return pl.pallas_call(
        _mod_kernel,
        grid=(n // MOD_COLS,),
        in_specs=[
            pl.BlockSpec((rows, D_MODEL), lambda j: (0, 0)),
            pl.BlockSpec((D_MODEL, MOD_COLS), lambda j: (0, j)),
            pl.BlockSpec((1, MOD_COLS), lambda j: (0, j)),
        ],
        out_specs=pl.BlockSpec((rows, MOD_COLS), lambda j: (0, j)),
        out_shape=jax.ShapeDtypeStruct((rows, n), F32),
        compiler_params=pltpu.CompilerParams(vmem_limit_bytes=VMEM_LIMIT),
        name="modulation",
    )(cvec, w_ada, b_ada)


def _inproj_kernel(seg, is_grid, x_ref, sh_ref, sc_ref, ng_ref, w_ref, cw_ref, *rest):
    if is_grid:
        cos_ref, sa_ref, sb_ref, yc_ref, q_ref, k_ref, v_ref, g_ref = rest
    else:
        yc_ref, q_ref, k_ref, v_ref, g_ref = rest
    x = x_ref[0]
    xn = (_rms(x) * ng_ref[...]) * (1.0 + sc_ref[0, 0]) + sh_ref[0, 0]
    xb = xn.astype(BF16)

    def proj(c0, n):
        return jnp.dot(xb, w_ref[:, c0:c0 + n], preferred_element_type=F32)

    gate_b = proj(0, CONV_W)
    u = proj(CONV_W, CONV_W) * proj(2 * CONV_W, CONV_W)
    rows = u.shape[0]
    pos = lax.broadcasted_iota(jnp.int32, u.shape, 0) & (seg - 1)
    u_prev = jnp.where(pos != 0, pltpu.roll(u, 1, 0), 0.0)
    u_next = jnp.where(pos != seg - 1, pltpu.roll(u, rows - 1, 0), 0.0)
    conv = cw_ref[0:1, :] * u_prev + cw_ref[1:2, :] * u + cw_ref[2:3, :] * u_next
    yc_ref[0] = (gate_b * conv).astype(yc_ref.dtype)

    q0 = 3 * CONV_W
    q = proj(q0, QK_W)
    k = proj(q0 + QK_W, QK_W)
    if is_grid:
        cos, sa, sb = cos_ref[...], sa_ref[...], sb_ref[...]

        def rope(t):
            out = []
            for h in range(RET_HEADS):
                th = t[:, h * RET_DK:(h + 1) * RET_DK]
                out.append(th * cos + pltpu.roll(th, RET_DK - 1, 1) * sa + pltpu.roll(th, 1, 1) * sb)
            return jnp.concatenate(out, axis=1)

        q, k = rope(q), rope(k)
    q_ref[0] = q
    k_ref[0] = k
    v_ref[0] = proj(q0 + 2 * QK_W, RET_W)
    g_ref[0] = proj(q0 + 2 * QK_W + RET_W, RET_W)


def _rope_tables(length):
    pos = jnp.arange(length)
    row = (pos // GRID_W).astype(F32)
    col = (pos % GRID_W).astype(F32)
    n_pairs = RET_DK // 4
    freqs = ROPE_BASE ** (-(jnp.arange(n_pairs, dtype=F32) * 2.0 / (RET_DK // 2)))
    ang = jnp.concatenate([row[:, None] * freqs, col[:, None] * freqs], axis=-1)
    cos = jnp.repeat(jnp.cos(ang), 2, axis=-1)
    sin = jnp.repeat(jnp.sin(ang), 2, axis=-1)
    even = (jnp.arange(RET_DK) % 2) == 0
    return cos, jnp.where(even, -sin, 0.0), jnp.where(even, 0.0, sin)


def _inproj(x, mod4, mod_row, norm_g, w_in_bf, conv_w, is_grid):
    bsz, length, _ = x.shape
    seg = GRID_W if is_grid else length
    assert TOKEN_TILE % seg == 0 and length % TOKEN_TILE == 0
    tiles = length // TOKEN_TILE

    def mod_spec(which):
        return pl.BlockSpec((1, 1, 1, D_MODEL), lambda b, i: (mod_row(b), which, 0, 0))

    def tok_spec(width):
        return pl.BlockSpec((1, TOKEN_TILE, width), lambda b, i: (b, i, 0))

    in_specs = [
        tok_spec(D_MODEL), mod_spec(0), mod_spec(1),
        pl.BlockSpec((1, D_MODEL), lambda b, i: (0, 0)),
        pl.BlockSpec(w_in_bf.shape, lambda b, i: (0, 0)),
        pl.BlockSpec(conv_w.shape, lambda b, i: (0, 0)),
    ]
    args = [x, mod4, mod4, norm_g, w_in_bf, conv_w]
    if is_grid:
        in_specs += [pl.BlockSpec((TOKEN_TILE, RET_DK), lambda b, i: (i, 0))] * 3
        args += list(_rope_tables(length))
    shp = lambda w, dt: jax.ShapeDtypeStruct((bsz, length, w), dt)
    return pl.pallas_call(
        functools.partial(_inproj_kernel, seg, is_grid),
        grid=(bsz, tiles),
        in_specs=in_specs,
        out_specs=[tok_spec(CONV_W), tok_spec(QK_W), tok_spec(QK_W), tok_spec(RET_W), tok_spec(RET_W)],
        out_shape=[shp(CONV_W, BF16), shp(QK_W, F32), shp(QK_W, F32), shp(RET_W, F32), shp(RET_W, F32)],
        compiler_params=pltpu.CompilerParams(
            dimension_semantics=("parallel", "parallel"), vmem_limit_bytes=VMEM_LIMIT),
        name="inproj_grid" if is_grid else "inproj_seq",
    )(*args)


def _ret_kernel(n_chunks, heads, has_init, emit_state, a_ref, q_ref, k_ref, v_ref, g_ref, *rest):
    rest = list(rest)
    if has_init:
        sf0_ref, sb0_ref = rest[:2]
        rest = rest[2:]
    y_ref = rest.pop(0)
    if emit_state:
        sf_out, sb_out = rest[:2]
        rest = rest[2:]
    st_f, st_b, dec = rest
    c = CHUNK
    sq = (c, c)
    head0 = pl.program_id(0) * heads

    def log_decays(hh):
        lg_f = jnp.log1p(-jnp.exp(a_ref[pl.ds(head0 + hh, 1), :]))
        lg_b = jnp.log1p(-jnp.exp(a_ref[pl.ds(head0 + hh + RET_HEADS, 1), :]))
        return lg_f, lg_b

    @pl.when(pl.program_id(1) == 0)
    def _():
        row = lax.broadcasted_iota(jnp.int32, sq, 0).astype(F32)
        col = lax.broadcasted_iota(jnp.int32, sq, 1).astype(F32)
        scale = RET_DK ** -0.5
        for hh in range(heads):
            lg_f, lg_b = log_decays(hh)
            dec[hh, 0] = scale * (
                jnp.where(row >= col, jnp.exp(jnp.where(row >= col, row - col, 0.0) * lg_f), 0.0)
                + jnp.where(col >= row, jnp.exp(jnp.where(col >= row, col - row, 0.0) * lg_b), 0.0))
            dec[hh, 1] = jnp.exp((row + 1.0) * lg_f)
            dec[hh, 2] = jnp.exp((c - row) * lg_b)
            dec[hh, 3] = scale * jnp.exp((c - 1.0 - col) * lg_f)
            dec[hh, 4] = scale * jnp.exp(col * lg_b)

    def rows(n):
        return pl.ds(pl.multiple_of(n * c, c), c) if not isinstance(n, int) else pl.ds(n * c, c)

    def cols(hh):
        return slice(hh * RET_DK, (hh + 1) * RET_DK)

    def kv_step(hh, n):
        kt = jnp.transpose(k_ref[0, rows(n), cols(hh)])
        lhs = jnp.concatenate([kt * dec[hh, 3], kt * dec[hh, 4]], axis=0)
        kv = _bdot(lhs, v_ref[0, rows(n), cols(hh)])
        st_f[hh, n] = kv[:RET_DK]
        st_b[hh, n] = kv[RET_DK:]

    def scan(hh, st, decay, order, s):
        def step(i, s):
            n = order(i)
            kv = st[hh, n]
            st[hh, n] = s
            return s * decay + kv
        if n_chunks <= RET_UNROLL:
            for i in range(n_chunks):
                s = step(i, s)
            return s
        return lax.fori_loop(0, n_chunks, step, s, unroll=RET_UNROLL)

    def out_step(hh, n):
        q = q_ref[0, rows(n), cols(hh)]
        scores = lax.dot_general(q.astype(BF16), k_ref[0, rows(n), cols(hh)].astype(BF16),
                                 (((1,), (1,)), ((), ())), preferred_element_type=F32)
        o = _bdot(scores * dec[hh, 0], v_ref[0, rows(n), cols(hh)])
        q_dec = jnp.concatenate([q * dec[hh, 1], q * dec[hh, 2]], axis=1)
        o = o + _bdot(q_dec, jnp.concatenate([st_f[hh, n], st_b[hh, n]], axis=0))
        y = _silu(g_ref[0, rows(n), cols(hh)]) * _rms(o)
        y_ref[0, rows(n), cols(hh)] = y.astype(y_ref.dtype)

    def over_chunks(step):
        if n_chunks * heads <= RET_UNROLL:
            for hh in range(heads):
                for n in range(n_chunks):
                    step(hh, n)
        else:
            for hh in range(heads):
                lax.fori_loop(0, n_chunks, lambda n, carry: (step(hh, n), carry)[1], 0, unroll=RET_UNROLL)

    over_chunks(kv_step)
    finals = []
    for hh in range(heads):
        lg_f, lg_b = log_decays(hh)
        s_f = sf0_ref[0, 0, hh] if has_init else jnp.zeros(sq, F32)
        s_b = sb0_ref[0, 0, hh] if has_init else jnp.zeros(sq, F32)
        s_f = scan(hh, st_f, jnp.exp(c * lg_f), lambda i: i, s_f)
        s_b = scan(hh, st_b, jnp.exp(c * lg_b), lambda i: n_chunks - 1 - i, s_b)
        finals.append((s_f, s_b))
    over_chunks(out_step)
    if emit_state:
        for hh, (s_f, s_b) in enumerate(finals):
            sf_out[0, 0, hh] = s_f
            sb_out[0, 0, hh] = s_b


def _retention(q, k, v, g, decay_rows, s_f0, s_b0, emit_state):
    bsz, length, _ = q.shape
    n_chunks = length // CHUNK
    has_init = s_f0 is not None
    heads = RET_HEADS if n_chunks * RET_HEADS <= RET_UNROLL else 1
    head_spec = pl.BlockSpec((1, length, heads * RET_DK), lambda h, b: (b, 0, h))
    st_spec = pl.BlockSpec((1, 1, heads, RET_DK, RET_DV), lambda h, b: (b, 0, h, 0, 0))
    in_specs = [pl.BlockSpec(decay_rows.shape, lambda h, b: (0, 0))] + [head_spec] * 4
    args = [decay_rows, q, k, v, g]
    if has_init:
        in_specs += [st_spec, st_spec]
        args += [s_f0, s_b0]
    out_specs = [head_spec]
    out_shape = [jax.ShapeDtypeStruct((bsz, length, RET_W), BF16)]
    if emit_state:
        st_shape = jax.ShapeDtypeStruct((bsz, 1, RET_HEADS, RET_DK, RET_DV), F32)
        out_specs += [st_spec, st_spec]
        out_shape += [st_shape, st_shape]
    return pl.pallas_call(
        functools.partial(_ret_kernel, n_chunks, heads, has_init, emit_state),
        grid=(RET_HEADS // heads, bsz),
        in_specs=in_specs,
        out_specs=out_specs,
        out_shape=out_shape,
        scratch_shapes=[
            pltpu.VMEM((heads, n_chunks, RET_DK, RET_DV), F32),
            pltpu.VMEM((heads, n_chunks, RET_DK, RET_DV), F32),
            pltpu.VMEM((heads, 5, CHUNK, CHUNK), F32),
        ],
        compiler_params=pltpu.CompilerParams(
            dimension_semantics=("arbitrary", "arbitrary"), vmem_limit_bytes=VMEM_LIMIT),
        name="retention_init" if has_init else "retention_zero",
    )(*args)


def _route(logits):
    lane = lax.broadcasted_iota(jnp.int32, logits.shape, 1)
    lane_f = lane.astype(F32)
    neg = -jnp.inf
    far = float(LANES)
    is_g = lane < N_GROUPS
    lg = jnp.where(is_g, logits, neg)
    g_max = jnp.max(lg, axis=1, keepdims=True)
    g_idx = jnp.min(jnp.where(lg == g_max, lane_f, far), axis=1, keepdims=True)
    p_sel = 1.0 / jnp.sum(jnp.where(is_g, jnp.exp(lg - g_max), 0.0), axis=1, keepdims=True)
    lane_group = ((lane - N_GROUPS) >> 3).astype(F32)
    sel = (lane >= N_GROUPS) & (lane < N_GROUPS + N_EXPERTS) & (lane_group == g_idx)
    le = jnp.where(sel, logits, neg)
    v1 = jnp.max(le, axis=1, keepdims=True)
    i1 = jnp.min(jnp.where(le == v1, lane_f, far), axis=1, keepdims=True)
    le2 = jnp.where(lane_f == i1, neg, le)
    v2 = jnp.max(le2, axis=1, keepdims=True)
    i2 = jnp.min(jnp.where(le2 == v2, lane_f, far), axis=1, keepdims=True)
    e2 = jnp.exp(v2 - v1)
    w1 = p_sel * (1.0 / (1.0 + e2))
    w2 = p_sel * (e2 / (1.0 + e2))
    return lane, lane_f, g_idx, i1, i2, w1, w2


def _outproj_kernel(yc_ref, yr_ref, x_ref, g1_ref, sh_ref, sc_ref, ng_ref, wo_ref, wr_ref, br_ref,
                    x1_ref, xloc_ref, route_ref, cnt_ref):
    m = (jnp.dot(yc_ref[...], wo_ref[0:CONV_W, :], preferred_element_type=F32)
         + jnp.dot(yr_ref[...], wo_ref[CONV_W:, :], preferred_element_type=F32))
    x1 = x_ref[...] + g1_ref[0, 0] * m
    x1_ref[...] = x1
    xn = (_rms(x1) * ng_ref[...]) * (1.0 + sc_ref[0, 0]) + sh_ref[0, 0]
    xb = xn.astype(BF16)
    logits = jnp.dot(xb, wr_ref[...], preferred_element_type=F32) + br_ref[...]
    lane, lane_f, g_idx, i1, i2, w1, w2 = _route(logits)

    picks = jnp.where(lane_f == g_idx, 1.0, 0.0)
    rows = picks.shape[0]
    tri = (lax.broadcasted_iota(jnp.int32, (rows, rows), 0)
           > lax.broadcasted_iota(jnp.int32, (rows, rows), 1))
    before = jnp.dot(jnp.where(tri, 1.0, 0.0).astype(BF16), picks.astype(BF16),
                     preferred_element_type=F32)
    count = jnp.sum(picks, axis=0, keepdims=True)
    cnt_ref[0] = count
    count8 = jnp.broadcast_to(jnp.floor((count + (SUBLANES - 1)) * (1.0 / SUBLANES)) * SUBLANES,
                              (SUBLANES, LANES))
    lane8 = lane[:SUBLANES]
    start = sum(jnp.where(lane8 >= k, pltpu.roll(count8, k, 1), 0.0) for k in range(1, N_GROUPS))
    local = jnp.sum(jnp.where(lane_f == g_idx, before + start[0:1], 0.0), axis=1, keepdims=True)
    route = jnp.where(lane == ROUTE_GROUP, g_idx, jnp.where(lane == ROUTE_LOCAL, local, jnp.where(
        lane == ROUTE_E1, i1 - N_GROUPS, jnp.where(lane == ROUTE_E2, i2 - N_GROUPS, jnp.where(
            lane == ROUTE_W1, w1, jnp.where(lane == ROUTE_W2, w2, 0.0))))))
    route_ref[...] = route

    local_row = jnp.transpose(jnp.broadcast_to(local, (rows, LANES)))[0:1, :]
    place = jnp.where(lax.broadcasted_iota(jnp.int32, (LOCAL_ROWS, rows), 0).astype(F32) == local_row,
                      1.0, 0.0).astype(BF16)
    xloc_ref[0, :, :D_MODEL] = jnp.dot(place, xb, preferred_element_type=F32)
    hi = route.astype(BF16)
    rest = route - hi.astype(F32)
    mid = rest.astype(BF16)
    low = (rest - mid.astype(F32)).astype(BF16)
    xloc_ref[0, :, D_MODEL:] = (jnp.dot(place, hi, preferred_element_type=F32)
                                + jnp.dot(place, mid, preferred_element_type=F32)
                                + jnp.dot(place, low, preferred_element_type=F32))


def _outproj(y_conv, y_ret, x, mod4, mod_row_of_tile, norm_g, w_out_bf, w_router_bf, b_router):
    tokens = x.shape[0]
    tiles = tokens // OUTPROJ_TILE

    def mod_spec(which):
        return pl.BlockSpec((1, 1, 1, D_MODEL), lambda i: (mod_row_of_tile(i), which, 0, 0))

    tok = lambda w: pl.BlockSpec((OUTPROJ_TILE, w), lambda i: (i, 0))
    full = lambda a: pl.BlockSpec(a.shape, lambda i: (0,) * a.ndim)
    return pl.pallas_call(
        _outproj_kernel,
        grid=(tiles,),
        in_specs=[tok(CONV_W), tok(RET_W), tok(D_MODEL), mod_spec(2), mod_spec(3), mod_spec(4),
                  full(norm_g), full(w_out_bf), full(w_router_bf), full(b_router)],
        out_specs=[tok(D_MODEL),
                   pl.BlockSpec((1, LOCAL_ROWS, ROW_W), lambda i: (i, 0, 0)),
                   tok(ROUTER_COLS),
                   pl.BlockSpec((1, 1, ROUTER_COLS), lambda i: (i, 0, 0))],
        out_shape=[jax.ShapeDtypeStruct((tokens, D_MODEL), F32),
                   jax.ShapeDtypeStruct((tiles, LOCAL_ROWS, ROW_W), F32),
                   jax.ShapeDtypeStruct((tokens, ROUTER_COLS), F32),
                   jax.ShapeDtypeStruct((tiles, 1, ROUTER_COLS), F32)],
        compiler_params=pltpu.CompilerParams(
            dimension_semantics=("parallel",), vmem_limit_bytes=VMEM_LIMIT),
        name="outproj",
    )(y_conv, y_ret, x, mod4, mod4, mod4, norm_g, w_out_bf, w_router_bf, b_router)


def _row_copy(src, src_row, dst, dst_row, sem):
    return pltpu.make_async_copy(src.at[pl.ds(src_row, 1)], dst.at[pl.ds(dst_row, 1)], sem)


def _rows_wait(src, dst, rows, sem):
    pltpu.make_async_copy(src.at[pl.ds(0, rows)], dst.at[pl.ds(0, rows)], sem).wait()


RUN_PIECES = tuple(SUBLANES << b for b in reversed(range((OUTPROJ_TILE // SUBLANES).bit_length())))


DISPATCH_BUFFERS = 3


def _dispatch_kernel(ctx_tiles, run_len_ref, run_src_ref, run_dst_ref, pad_lo_ref, pad_hi_ref,
                     xloc_ctx_hbm, xloc_lat_hbm, xs_hbm, buf, zero_buf, in_sem, out_sem):
    i = pl.program_id(0)
    tiles = pl.num_programs(0)

    def run_copies(tile, act):
        slot = tile % DISPATCH_BUFFERS
        for g in range(N_GROUPS):
            n = run_len_ref[tile * N_GROUPS + g]
            src = run_src_ref[tile * N_GROUPS + g]
            dst = run_dst_ref[tile * N_GROUPS + g]
            for size in RUN_PIECES:
                done = n & (-2 * size)

                @pl.when((n & size) != 0)
                def _():
                    act(pltpu.make_async_copy(
                        buf.at[slot, pl.ds(pl.multiple_of(src + done, SUBLANES), size)],
                        xs_hbm.at[pl.ds(pl.multiple_of(dst + done, SUBLANES), size)], out_sem.at[slot]))

    def load(tile):
        slot = tile % DISPATCH_BUFFERS

        @pl.when(tile < ctx_tiles)
        def _():
            pltpu.make_async_copy(xloc_ctx_hbm.at[tile], buf.at[slot], in_sem.at[slot]).start()

        @pl.when(tile >= ctx_tiles)
        def _():
            pltpu.make_async_copy(xloc_lat_hbm.at[tile - ctx_tiles], buf.at[slot], in_sem.at[slot]).start()

    def zero_copy(r):
        return pltpu.make_async_copy(
            zero_buf, xs_hbm.at[pl.ds(pl.multiple_of(r * SUBLANES, SUBLANES), SUBLANES)], out_sem.at[0])

    start = lambda cp: cp.start()
    wait = lambda cp: cp.wait()

    @pl.when(i == 0)
    def _():
        load(i)

    @pl.when(i >= 2)
    def _():
        run_copies(i - 2, wait)

    @pl.when(i + 1 < tiles)
    def _():
        load(i + 1)

    slot = i % DISPATCH_BUFFERS
    pltpu.make_async_copy(xloc_lat_hbm.at[0], buf.at[slot], in_sem.at[slot]).wait()
    run_copies(i, start)

    @pl.when(i == tiles - 1)
    def _():
        @pl.when(i >= 1)
        def _():
            run_copies(i - 1, wait)
        run_copies(i, wait)
        zero_buf[...] = jnp.zeros_like(zero_buf)
        for act in (start, wait):
            for g in range(N_GROUPS):
                lax.fori_loop(pad_lo_ref[g], pad_hi_ref[g], lambda r, carry: (act(zero_copy(r)), carry)[1], 0)


def _dispatch(xloc_ctx, xloc_lat, run_len, run_src, run_dst, pad_lo, pad_hi, slots):
    ctx_tiles = xloc_ctx.shape[0]
    grid_spec = pltpu.PrefetchScalarGridSpec(
        num_scalar_prefetch=5,
        grid=(ctx_tiles + xloc_lat.shape[0],),
        in_specs=[pl.BlockSpec(memory_space=pl.ANY), pl.BlockSpec(memory_space=pl.ANY)],
        out_specs=pl.BlockSpec(memory_space=pl.ANY),
        scratch_shapes=[pltpu.VMEM((DISPATCH_BUFFERS, LOCAL_ROWS, ROW_W), F32), pltpu.VMEM((SUBLANES, ROW_W), F32),
                        pltpu.SemaphoreType.DMA((DISPATCH_BUFFERS,)), pltpu.SemaphoreType.DMA((DISPATCH_BUFFERS,))],
    )
    return pl.pallas_call(
        functools.partial(_dispatch_kernel, ctx_tiles),
        grid_spec=grid_spec,
        out_shape=jax.ShapeDtypeStruct((slots, ROW_W), F32),
        compiler_params=pltpu.CompilerParams(dimension_semantics=("arbitrary",)),
        name="dispatch",
    )(run_len, run_src, run_dst, pad_lo, pad_hi, xloc_ctx, xloc_lat)


def _expert_kernel(tile_group_ref, n_used_ref, xs_ref, w1_ref, w3_ref, w2_ref, ys_ref, xb, gate_tabs):
    j = pl.program_id(0)
    step = pl.program_id(1)

    @pl.when(j < n_used_ref[0])
    def _():
        @pl.when(step == 0)
        def _():
            xb[...] = xs_ref[:, :D_MODEL].astype(BF16)
            route = xs_ref[:, D_MODEL:]
            lane = lax.broadcasted_iota(jnp.int32, route.shape, 1)
            for n, which in enumerate((ROUTE_E1, ROUTE_E2, ROUTE_W1, ROUTE_W2)):
                col = jnp.sum(jnp.where(lane == which, route, 0.0), axis=1, keepdims=True)
                gate_tabs[n] = jnp.broadcast_to(col, route.shape)

        x = xb[...]
        total = None
        for s in range(EXPERTS_PER_STEP):
            expert = (tile_group_ref[j] * EXPERTS_PER_GROUP + step * EXPERTS_PER_STEP + s).astype(F32)
            gate = (jnp.where(gate_tabs[0] == expert, gate_tabs[2], 0.0)
                    + jnp.where(gate_tabs[1] == expert, gate_tabs[3], 0.0))
            hid = _silu(jnp.dot(x, w1_ref[0, s].astype(BF16), preferred_element_type=F32)) * jnp.dot(
                x, w3_ref[0, s].astype(BF16), preferred_element_type=F32)
            y = jnp.dot(hid.astype(BF16), w2_ref[0, s].astype(BF16), preferred_element_type=F32)
            gated = jnp.concatenate(
                [gate * y[:, c * LANES:(c + 1) * LANES] for c in range(D_MODEL // LANES)], axis=1)
            total = gated if total is None else total + gated

        @pl.when(step == 0)
        def _():
            ys_ref[...] = total

        @pl.when(step > 0)
        def _():
            ys_ref[...] += total


def _experts(xs, tile_group, n_used, w1, w3, w2):
    slots = xs.shape[0]
    steps = EXPERTS_PER_GROUP // EXPERTS_PER_STEP
    paired = lambda w: w.reshape((N_EXPERTS // EXPERTS_PER_STEP, EXPERTS_PER_STEP) + w.shape[1:])
    row_tile = lambda j, s, tg, nu: (jnp.minimum(j, nu[0] - 1), 0)
    w_spec = lambda shape: pl.BlockSpec((1, EXPERTS_PER_STEP) + shape, lambda j, s, tg, nu: (
        tg[jnp.minimum(j, nu[0] - 1)] * steps + jnp.where(j < nu[0], s, steps - 1), 0, 0, 0))
    grid_spec = pltpu.PrefetchScalarGridSpec(
        num_scalar_prefetch=2,
        grid=(slots // GROUP_TILE, steps),
        in_specs=[
            pl.BlockSpec((GROUP_TILE, ROW_W), row_tile),
            w_spec((D_MODEL, D_EXPERT)), w_spec((D_MODEL, D_EXPERT)), w_spec((D_EXPERT, D_MODEL)),
        ],
        out_specs=pl.BlockSpec((GROUP_TILE, D_MODEL), row_tile),
        scratch_shapes=[pltpu.VMEM((GROUP_TILE, D_MODEL), BF16), pltpu.VMEM((4, GROUP_TILE, LANES), F32)],
    )
    return pl.pallas_call(
        _expert_kernel,
        grid_spec=grid_spec,
        out_shape=jax.ShapeDtypeStruct((slots, D_MODEL), F32),
        compiler_params=pltpu.CompilerParams(
            dimension_semantics=("arbitrary", "arbitrary"), vmem_limit_bytes=EXPERT_VMEM_LIMIT),
        name="experts",
    )(tile_group, n_used, xs, paired(w1), paired(w3), paired(w2))


def _combine_kernel(n_tiles, pos_ref, pos_next_ref, x1_ref, g2_ref, fg_ref, ys_hbm, o_ref, buf, sem):
    i = pl.program_id(0)
    slot = i % 2

    def issue(p_ref, s):
        def body(r, carry):
            _row_copy(ys_hbm, p_ref[0, 0, r], buf.at[s], r, sem.at[s]).start()
            return carry
        lax.fori_loop(0, TOKEN_TILE, body, 0, unroll=8)

    @pl.when(i == 0)
    def _():
        issue(pos_ref, 0)

    @pl.when(i + 1 < n_tiles)
    def _():
        issue(pos_next_ref, 1 - slot)

    _rows_wait(ys_hbm, buf.at[slot], TOKEN_TILE, sem.at[slot])
    y = x1_ref[...] + g2_ref[0, 0] * buf[slot]
    o_ref[...] = _rms(y) * fg_ref[...]


def _combine(ys, pos_tiles, x1, mod4, mod_row_of_tile, final_g, tile_base):
    tokens = x1.shape[0]
    tiles = tokens // TOKEN_TILE
    last = tile_base + tiles - 1
    pos_spec = lambda nxt: pl.BlockSpec(
        (1, 1, TOKEN_TILE), lambda i: (jnp.minimum(i + tile_base + nxt, last), 0, 0),
        memory_space=pltpu.SMEM)
    return pl.pallas_call(
        functools.partial(_combine_kernel, tiles),
        grid=(tiles,),
        in_specs=[
            pos_spec(0), pos_spec(1),
            pl.BlockSpec((TOKEN_TILE, D_MODEL), lambda i: (i, 0)),
            pl.BlockSpec((1, 1, 1, D_MODEL), lambda i: (mod_row_of_tile(i), 5, 0, 0)),
            pl.BlockSpec((1, D_MODEL), lambda i: (0, 0)),
            pl.BlockSpec(memory_space=pl.ANY),
        ],
        out_specs=pl.BlockSpec((TOKEN_TILE, D_MODEL), lambda i: (i, 0)),
        out_shape=jax.ShapeDtypeStruct((tokens, D_MODEL), F32),
        scratch_shapes=[pltpu.VMEM((2, TOKEN_TILE, D_MODEL), F32), pltpu.SemaphoreType.DMA((2,))],
        compiler_params=pltpu.CompilerParams(
            dimension_semantics=("arbitrary",), vmem_limit_bytes=VMEM_LIMIT),
        name="combine",
    )(pos_tiles, pos_tiles, x1, mod4, final_g, ys)


def _routing_tables(routes, counts):
    counts = counts.astype(jnp.int32)
    tiles = counts.shape[0]
    run_len = ((counts + SUBLANES - 1) // SUBLANES) * SUBLANES
    run_src = jnp.cumsum(run_len, axis=1) - run_len
    group_rows = jnp.sum(run_len, axis=0)
    padded = ((group_rows + GROUP_TILE - 1) // GROUP_TILE) * GROUP_TILE
    ends = jnp.cumsum(padded)
    offs = ends - padded
    run_dst = offs[None, :] + jnp.cumsum(run_len, axis=0) - run_len

    route = jnp.concatenate([r[:, :2] for r in routes], axis=0).astype(jnp.int32)
    group = route[:, ROUTE_GROUP].reshape(tiles, OUTPROJ_TILE)
    local = route[:, ROUTE_LOCAL].reshape(tiles, OUTPROJ_TILE)
    onehot = group[:, :, None] == jnp.arange(N_GROUPS, dtype=jnp.int32)
    pos = local + jnp.sum(jnp.where(onehot, (run_dst - run_src)[:, None, :], 0), axis=-1)
    pos_tiles = pos.reshape(-1, 1, TOKEN_TILE)

    n_used = ends[-1] // GROUP_TILE
    max_rows = tiles * (OUTPROJ_TILE + N_GROUPS * (SUBLANES - 1))
    max_tiles = -(-max_rows // GROUP_TILE) + N_GROUPS
    tile_ids = jnp.minimum(jnp.arange(max_tiles, dtype=jnp.int32), n_used - 1)
    tile_group = jnp.sum(tile_ids[:, None] * GROUP_TILE >= ends[None, :], axis=1).astype(jnp.int32)
    flat = lambda a: a.reshape(-1)
    pads = ((offs + group_rows) // SUBLANES, ends // SUBLANES)
    return pos_tiles, (flat(run_len), flat(run_src), flat(run_dst)) + pads, tile_group, n_used.reshape(1), max_tiles * GROUP_TILE


def _mixer(x, mod4, mod_row, is_grid, s_f0, s_b0, p):
    norm_mix_g, w_in_bf, conv_w, decay_rows = p
    y_conv, q, k, v, g = _inproj(x, mod4, mod_row, norm_mix_g, w_in_bf, conv_w, is_grid)
    ret = _retention(q, k, v, g, decay_rows, s_f0, s_b0, emit_state=not is_grid)
    flat = lambda a: a.reshape(-1, a.shape[-1])
    return flat(y_conv), flat(ret[0]), ret[1:]


def kernel(x_prompt, x_sample, state_ret_fwd, state_ret_bwd, c, c_ctx, norm_mix_g, norm_ffn_g, w_ada, b_ada, w_in, conv_w, ret_decay_fwd, ret_decay_bwd, w_out, w_router_group, b_router_group, w_router_expert, b_router_expert, w_gate_e, w_up_e, w_down_e, final_norm_g):
    assert norm_mix_g.shape[0] == 1, "single-layer backbone"
    n_lat = c.shape[0]
    ctx_row = n_lat
    mod_rows = 8
    cvec = jnp.concatenate([c, c_ctx[None, :], jnp.zeros((mod_rows - n_lat - 1, D_MODEL), F32)], axis=0)
    mod = _modulation(cvec, w_ada[0], b_ada[0][None, :])
    mod4 = mod.reshape(mod_rows, 6, 1, D_MODEL)

    pad = ROUTER_COLS - N_GROUPS - N_EXPERTS
    w_router = jnp.concatenate(
        [w_router_group[0], w_router_expert[0], jnp.zeros((D_MODEL, pad), F32)], axis=1).astype(BF16)
    b_router = jnp.concatenate([b_router_group[0], b_router_expert[0], jnp.zeros((pad,), F32)])[None, :]
    decay_rows = jnp.broadcast_to(
        jnp.concatenate([ret_decay_fwd[0], ret_decay_bwd[0]])[:, None], (2 * RET_HEADS, LANES))
    p_mix = (norm_mix_g, w_in[0].astype(BF16), conv_w[0], decay_rows)
    w_out_bf = w_out[0].astype(BF16)
    final_g = final_norm_g[None, :]

    ctx_tiles = (x_prompt.shape[0] * x_prompt.shape[1]) // TOKEN_TILE
    lat_len = x_sample.shape[1]
    ctx_mod = lambda i: ctx_row
    flat = lambda a: a.reshape(-1, a.shape[-1])

    yc_c, yr_c, (s_f, s_b) = _mixer(x_prompt, mod4, lambda b: ctx_row, False, None, None, p_mix)
    yc_l, yr_l, _ = _mixer(x_sample, mod4, lambda b: b, True, state_ret_fwd, state_ret_bwd, p_mix)

    x1_c, xloc_c, route_c, cnt_c = _outproj(
        yc_c, yr_c, flat(x_prompt), mod4, ctx_mod, norm_ffn_g, w_out_bf, w_router, b_router)
    x1_l, xloc_l, route_l, cnt_l = _outproj(
        yc_l, yr_l, flat(x_sample), mod4, lambda i: i // (lat_len // OUTPROJ_TILE), norm_ffn_g,
        w_out_bf, w_router, b_router)

    counts = jnp.concatenate([cnt_c, cnt_l], axis=0)[:, 0, :N_GROUPS]
    pos_tiles, runs, tile_group, n_used, slots = _routing_tables((route_c, route_l), counts)
    xs = _dispatch(xloc_c, xloc_l, *runs, slots)
    ys = _experts(xs, tile_group, n_used, w_gate_e[0], w_up_e[0], w_down_e[0])
    y_prompt = _combine(ys, pos_tiles, x1_c, mod4, ctx_mod, final_g, 0)
    y_sample = _combine(ys, pos_tiles, x1_l, mod4, lambda i: i // (lat_len // TOKEN_TILE), final_g, ctx_tiles)
    return (y_prompt.reshape(x_prompt.shape), y_sample.reshape(x_sample.shape),
            s_f.astype(x_prompt.dtype), s_b.astype(x_prompt.dtype))
```

```python
import functools

import jax
import jax.numpy as jnp
from jax import lax
from jax.experimental import pallas as pl
from jax.experimental.pallas import tpu as pltpu

F32 = jnp.float32
BF16 = jnp.bfloat16

D_MODEL = 1024
GRID_W = 64
CONV_W = 512
RET_HEADS = 4
RET_DK = 128
RET_DV = 128
RET_W = RET_HEADS * RET_DV
QK_W = RET_HEADS * RET_DK
CHUNK = 128
N_GROUPS = 4
EXPERTS_PER_GROUP = 8
N_EXPERTS = N_GROUPS * EXPERTS_PER_GROUP
D_EXPERT = 256
ROPE_BASE = 10000.0
EPS = 1e-6

LANES = 128
TOKEN_TILE = 256
OUTPROJ_TILE = 512
GROUP_TILE = 1024
EXPERTS_PER_STEP = 2
RET_UNROLL = 8
SUBLANES = 8
LOCAL_ROWS = OUTPROJ_TILE + LANES
ROW_W = D_MODEL + LANES
ROUTE_GROUP, ROUTE_LOCAL, ROUTE_E1, ROUTE_E2, ROUTE_W1, ROUTE_W2 = range(6)
assert LOCAL_ROWS >= OUTPROJ_TILE + N_GROUPS * (SUBLANES - 1)
MOD_COLS = 1536
ROUTER_COLS = LANES
VMEM_LIMIT = 48 * 1024 * 1024
EXPERT_VMEM_LIMIT = 56 * 1024 * 1024


def _silu(x):
    return x * jax.nn.sigmoid(x)


def _rms(x):
    return x * lax.rsqrt(jnp.mean(x * x, axis=-1, keepdims=True) + EPS)


def _bdot(a, b):
    return jnp.dot(a.astype(BF16), b.astype(BF16), preferred_element_type=F32)


def _split3(x):
    hi = x.astype(BF16)
    rest = x - hi.astype(F32)
    mid = rest.astype(BF16)
    return hi, mid, (rest - mid.astype(F32)).astype(BF16)


def _mod_kernel(c_ref, w_ref, b_ref, o_ref):
    o_ref[...] = _bdot(_silu(c_ref[...]), w_ref[...]) + b_ref[...]


def _modulation(cvec, w_ada, b_ada):
    rows = cvec.shape[0]
    n = w_ada.shape[1]
    return pl.pallas_call(
        _mod_kernel,
        grid=(n // MOD_COLS,),
        in_specs=[
            pl.BlockSpec((rows, D_MODEL), lambda j: (0, 0)),
            pl.BlockSpec((D_MODEL, MOD_COLS), lambda j: (0, j)),
            pl.BlockSpec((1, MOD_COLS), lambda j: (0, j)),
        ],
        out_specs=pl.BlockSpec((rows, MOD_COLS), lambda j: (0, j)),
        out_shape=jax.ShapeDtypeStruct((rows, n), F32),
        compiler_params=pltpu.CompilerParams(vmem_limit_bytes=VMEM_LIMIT),
        name="modulation",
    )(cvec, w_ada, b_ada)


def _inproj_kernel(seg, is_grid, x_ref, sh_ref, sc_ref, ng_ref, w_ref, cw_ref, *rest):
    if is_grid:
        cos_ref, sa_ref, sb_ref, yc_ref, q_ref, k_ref, v_ref, g_ref = rest
    else:
        yc_ref, q_ref, k_ref, v_ref, g_ref = rest
    x = x_ref[0]
    xn = (_rms(x) * ng_ref[...]) * (1.0 + sc_ref[0, 0]) + sh_ref[0, 0]
    xb = xn.astype(BF16)

    def proj(c0, n):
        return jnp.dot(xb, w_ref[:, c0:c0 + n], preferred_element_type=F32)

    gate_b = proj(0, CONV_W)
    u = proj(CONV_W, CONV_W) * proj(2 * CONV_W, CONV_W)
    rows = u.shape[0]
    pos = lax.broadcasted_iota(jnp.int32, u.shape, 0) & (seg - 1)
    u_prev = jnp.where(pos != 0, pltpu.roll(u, 1, 0), 0.0)
    u_next = jnp.where(pos != seg - 1, pltpu.roll(u, rows - 1, 0), 0.0)
    conv = cw_ref[0:1, :] * u_prev + cw_ref[1:2, :] * u + cw_ref[2:3, :] * u_next
    yc_ref[0] = (gate_b * conv).astype(yc_ref.dtype)

    q0 = 3 * CONV_W
    q = proj(q0, QK_W)
    k = proj(q0 + QK_W, QK_W)
    if is_grid:
        cos, sa, sb = cos_ref[...], sa_ref[...], sb_ref[...]

        def rope(t):
            out = []
            for h in range(RET_HEADS):
                th = t[:, h * RET_DK:(h + 1) * RET_DK]
                out.append(th * cos + pltpu.roll(th, RET_DK - 1, 1) * sa + pltpu.roll(th, 1, 1) * sb)
            return jnp.concatenate(out, axis=1)

        q, k = rope(q), rope(k)
    q_ref[0] = q
    k_ref[0] = k
    v_ref[0] = proj(q0 + 2 * QK_W, RET_W)
    g_ref[0] = proj(q0 + 2 * QK_W + RET_W, RET_W)


def _rope_tables(length):
    pos = jnp.arange(length)
    row = (pos // GRID_W).astype(F32)
    col = (pos % GRID_W).astype(F32)
    n_pairs = RET_DK // 4
    freqs = ROPE_BASE ** (-(jnp.arange(n_pairs, dtype=F32) * 2.0 / (RET_DK // 2)))
    ang = jnp.concatenate([row[:, None] * freqs, col[:, None] * freqs], axis=-1)
    cos = jnp.repeat(jnp.cos(ang), 2, axis=-1)
    sin = jnp.repeat(jnp.sin(ang), 2, axis=-1)
    even = (jnp.arange(RET_DK) % 2) == 0
    return cos, jnp.where(even, -sin, 0.0), jnp.where(even, 0.0, sin)


def _inproj(x, mod4, mod_row, norm_g, w_in_bf, conv_w, is_grid):
    bsz, length, _ = x.shape
    seg = GRID_W if is_grid else length
    assert TOKEN_TILE % seg == 0 and length % TOKEN_TILE == 0
    tiles = length // TOKEN_TILE

    def mod_spec(which):
        return pl.BlockSpec((1, 1, 1, D_MODEL), lambda b, i: (mod_row(b), which, 0, 0))

    def tok_spec(width):
        return pl.BlockSpec((1, TOKEN_TILE, width), lambda b, i: (b, i, 0))

    in_specs = [
        tok_spec(D_MODEL), mod_spec(0), mod_spec(1),
        pl.BlockSpec((1, D_MODEL), lambda b, i: (0, 0)),
        pl.BlockSpec(w_in_bf.shape, lambda b, i: (0, 0)),
        pl.BlockSpec(conv_w.shape, lambda b, i: (0, 0)),
    ]
    args = [x, mod4, mod4, norm_g, w_in_bf, conv_w]
    if is_grid:
        in_specs += [pl.BlockSpec((TOKEN_TILE, RET_DK), lambda b, i: (i, 0))] * 3
        args += list(_rope_tables(length))
    shp = lambda w, dt: jax.ShapeDtypeStruct((bsz, length, w), dt)
    return pl.pallas_call(
        functools.partial(_inproj_kernel, seg, is_grid),
        grid=(bsz, tiles),
        in_specs=in_specs,
        out_specs=[tok_spec(CONV_W), tok_spec(QK_W), tok_spec(QK_W), tok_spec(RET_W), tok_spec(RET_W)],
        out_shape=[shp(CONV_W, BF16), shp(QK_W, F32), shp(QK_W, F32), shp(RET_W, F32), shp(RET_W, F32)],
        compiler_params=pltpu.CompilerParams(
            dimension_semantics=("parallel", "parallel"), vmem_limit_bytes=VMEM_LIMIT),
        name="inproj_grid" if is_grid else "inproj_seq",
    )(*args)


def _ret_kernel(n_chunks, heads, has_init, emit_state, a_ref, q_ref, k_ref, v_ref, g_ref, *rest):
    rest = list(rest)
    if has_init:
        sf0_ref, sb0_ref = rest[:2]
        rest = rest[2:]
    y_ref = rest.pop(0)
    if emit_state:
        sf_out, sb_out = rest[:2]
        rest = rest[2:]
    st_f, st_b, dec = rest
    c = CHUNK
    sq = (c, c)
    head0 = pl.program_id(0) * heads

    def log_decays(hh):
        lg_f = jnp.log1p(-jnp.exp(a_ref[pl.ds(head0 + hh, 1), :]))
        lg_b = jnp.log1p(-jnp.exp(a_ref[pl.ds(head0 + hh + RET_HEADS, 1), :]))
        return lg_f, lg_b

    @pl.when(pl.program_id(1) == 0)
    def _():
        row = lax.broadcasted_iota(jnp.int32, sq, 0).astype(F32)
        col = lax.broadcasted_iota(jnp.int32, sq, 1).astype(F32)
        scale = RET_DK ** -0.5
        for hh in range(heads):
            lg_f, lg_b = log_decays(hh)
            dec[hh, 0] = scale * (
                jnp.where(row >= col, jnp.exp(jnp.where(row >= col, row - col, 0.0) * lg_f), 0.0)
                + jnp.where(col >= row, jnp.exp(jnp.where(col >= row, col - row, 0.0) * lg_b), 0.0))
            dec[hh, 1] = jnp.exp((row + 1.0) * lg_f)
            dec[hh, 2] = jnp.exp((c - row) * lg_b)
            dec[hh, 3] = scale * jnp.exp((c - 1.0 - col) * lg_f)
            dec[hh, 4] = scale * jnp.exp(col * lg_b)

    def rows(n):
        return pl.ds(pl.multiple_of(n * c, c), c) if not isinstance(n, int) else pl.ds(n * c, c)

    def cols(hh):
        return slice(hh * RET_DK, (hh + 1) * RET_DK)

    def kv_step(hh, n):
        kt = jnp.transpose(k_ref[0, rows(n), cols(hh)])
        lhs = jnp.concatenate([kt * dec[hh, 3], kt * dec[hh, 4]], axis=0)
        kv = _bdot(lhs, v_ref[0, rows(n), cols(hh)])
        st_f[hh, n] = kv[:RET_DK]
        st_b[hh, n] = kv[RET_DK:]

    def scan(hh, st, decay, order, s):
        def step(i, s):
            n = order(i)
            kv = st[hh, n]
            st[hh, n] = s
            return s * decay + kv
        if n_chunks <= RET_UNROLL:
            for i in range(n_chunks):
                s = step(i, s)
            return s
        return lax.fori_loop(0, n_chunks, step, s, unroll=RET_UNROLL)

    def out_step(hh, n):
        q = q_ref[0, rows(n), cols(hh)]
        scores = lax.dot_general(q.astype(BF16), k_ref[0, rows(n), cols(hh)].astype(BF16),
                                 (((1,), (1,)), ((), ())), preferred_element_type=F32)
        o = _bdot(scores * dec[hh, 0], v_ref[0, rows(n), cols(hh)])
        q_dec = jnp.concatenate([q * dec[hh, 1], q * dec[hh, 2]], axis=1)
        o = o + _bdot(q_dec, jnp.concatenate([st_f[hh, n], st_b[hh, n]], axis=0))
        y = _silu(g_ref[0, rows(n), cols(hh)]) * _rms(o)
        y_ref[0, rows(n), cols(hh)] = y.astype(y_ref.dtype)

    def over_chunks(step):
        if n_chunks * heads <= RET_UNROLL:
            for hh in range(heads):
                for n in range(n_chunks):
                    step(hh, n)
        else:
            for hh in range(heads):
                lax.fori_loop(0, n_chunks, lambda n, carry: (step(hh, n), carry)[1], 0, unroll=RET_UNROLL)

    over_chunks(kv_step)
    finals = []
    for hh in range(heads):
        lg_f, lg_b = log_decays(hh)
        s_f = sf0_ref[0, 0, hh] if has_init else jnp.zeros(sq, F32)
        s_b = sb0_ref[0, 0, hh] if has_init else jnp.zeros(sq, F32)
        s_f = scan(hh, st_f, jnp.exp(c * lg_f), lambda i: i, s_f)
        s_b = scan(hh, st_b, jnp.exp(c * lg_b), lambda i: n_chunks - 1 - i, s_b)
        finals.append((s_f, s_b))
    over_chunks(out_step)
    if emit_state:
        for hh, (s_f, s_b) in enumerate(finals):
            sf_out[0, 0, hh] = s_f
            sb_out[0, 0, hh] = s_b


def _retention(q, k, v, g, decay_rows, s_f0, s_b0, emit_state):
    bsz, length, _ = q.shape
    n_chunks = length // CHUNK
    has_init = s_f0 is not None
    heads = RET_HEADS if n_chunks * RET_HEADS <= RET_UNROLL else 1
    head_spec = pl.BlockSpec((1, length, heads * RET_DK), lambda h, b: (b, 0, h))
    st_spec = pl.BlockSpec((1, 1, heads, RET_DK, RET_DV), lambda h, b: (b, 0, h, 0, 0))
    in_specs = [pl.BlockSpec(decay_rows.shape, lambda h, b: (0, 0))] + [head_spec] * 4
    args = [decay_rows, q, k, v, g]
    if has_init:
        in_specs += [st_spec, st_spec]
        args += [s_f0, s_b0]
    out_specs = [head_spec]
    out_shape = [jax.ShapeDtypeStruct((bsz, length, RET_W), BF16)]
    if emit_state:
        st_shape = jax.ShapeDtypeStruct((bsz, 1, RET_HEADS, RET_DK, RET_DV), F32)
        out_specs += [st_spec, st_spec]
        out_shape += [st_shape, st_shape]
    return pl.pallas_call(
        functools.partial(_ret_kernel, n_chunks, heads, has_init, emit_state),
        grid=(RET_HEADS // heads, bsz),
        in_specs=in_specs,
        out_specs=out_specs,
        out_shape=out_shape,
        scratch_shapes=[
            pltpu.VMEM((heads, n_chunks, RET_DK, RET_DV), F32),
            pltpu.VMEM((heads, n_chunks, RET_DK, RET_DV), F32),
            pltpu.VMEM((heads, 5, CHUNK, CHUNK), F32),
        ],
        compiler_params=pltpu.CompilerParams(
            dimension_semantics=("arbitrary", "arbitrary"), vmem_limit_bytes=VMEM_LIMIT),
        name="retention_init" if has_init else "retention_zero",
    )(*args)


def _route(logits):
    lane = lax.broadcasted_iota(jnp.int32, logits.shape, 1)
    lane_f = lane.astype(F32)
    neg = -jnp.inf
    far = float(LANES)
    is_g = lane < N_GROUPS
    lg = jnp.where(is_g, logits, neg)
    g_max = jnp.max(lg, axis=1, keepdims=True)
    g_idx = jnp.min(jnp.where(lg == g_max, lane_f, far), axis=1, keepdims=True)
    p_sel = 1.0 / jnp.sum(jnp.where(is_g, jnp.exp(lg - g_max), 0.0), axis=1, keepdims=True)
    lane_group = ((lane - N_GROUPS) >> 3).astype(F32)
    sel = (lane >= N_GROUPS) & (lane < N_GROUPS + N_EXPERTS) & (lane_group == g_idx)
    le = jnp.where(sel, logits, neg)
    v1 = jnp.max(le, axis=1, keepdims=True)
    i1 = jnp.min(jnp.where(le == v1, lane_f, far), axis=1, keepdims=True)
    le2 = jnp.where(lane_f == i1, neg, le)
    v2 = jnp.max(le2, axis=1, keepdims=True)
    i2 = jnp.min(jnp.where(le2 == v2, lane_f, far), axis=1, keepdims=True)
    e2 = jnp.exp(v2 - v1)
    w1 = p_sel * (1.0 / (1.0 + e2))
    w2 = p_sel * (e2 / (1.0 + e2))
    return lane, lane_f, g_idx, i1, i2, w1, w2


def _outproj_kernel(yc_ref, yr_ref, x_ref, g1_ref, sh_ref, sc_ref, ng_ref, wo_ref, wr_ref, br_ref,
                    x1_ref, xloc_ref, route_ref, cnt_ref):
    m = (jnp.dot(yc_ref[...], wo_ref[0:CONV_W, :], preferred_element_type=F32)
         + jnp.dot(yr_ref[...], wo_ref[CONV_W:, :], preferred_element_type=F32))
    x1 = x_ref[...] + g1_ref[0, 0] * m
    x1_ref[...] = x1
    xn = (_rms(x1) * ng_ref[...]) * (1.0 + sc_ref[0, 0]) + sh_ref[0, 0]
    xb = xn.astype(BF16)
    logits = jnp.dot(xb, wr_ref[...], preferred_element_type=F32) + br_ref[...]
    lane, lane_f, g_idx, i1, i2, w1, w2 = _route(logits)

    picks = jnp.where(lane_f == g_idx, 1.0, 0.0)
    rows = picks.shape[0]
    tri = (lax.broadcasted_iota(jnp.int32, (rows, rows), 0)
           > lax.broadcasted_iota(jnp.int32, (rows, rows), 1))
    before = jnp.dot(jnp.where(tri, 1.0, 0.0).astype(BF16), picks.astype(BF16),
                     preferred_element_type=F32)
    count = jnp.sum(picks, axis=0, keepdims=True)
    cnt_ref[0] = count
    count8 = jnp.broadcast_to(jnp.floor((count + (SUBLANES - 1)) * (1.0 / SUBLANES)) * SUBLANES,
                              (SUBLANES, LANES))
    lane8 = lane[:SUBLANES]
    start = sum(jnp.where(lane8 >= k, pltpu.roll(count8, k, 1), 0.0) for k in range(1, N_GROUPS))
    local = jnp.sum(jnp.where(lane_f == g_idx, before + start[0:1], 0.0), axis=1, keepdims=True)
    route = jnp.where(lane == ROUTE_GROUP, g_idx, jnp.where(lane == ROUTE_LOCAL, local, jnp.where(
        lane == ROUTE_E1, i1 - N_GROUPS, jnp.where(lane == ROUTE_E2, i2 - N_GROUPS, jnp.where(
            lane == ROUTE_W1, w1, jnp.where(lane == ROUTE_W2, w2, 0.0))))))
    route_ref[...] = route

    local_row = jnp.transpose(jnp.broadcast_to(local, (rows, LANES)))[0:1, :]
    place = jnp.where(lax.broadcasted_iota(jnp.int32, (LOCAL_ROWS, rows), 0).astype(F32) == local_row,
                      1.0, 0.0).astype(BF16)
    xloc_ref[0, :, :D_MODEL] = jnp.dot(place, xb, preferred_element_type=F32)
    xloc_ref[0, :, D_MODEL:] = sum(
        jnp.dot(place, piece, preferred_element_type=F32) for piece in _split3(route))


def _outproj(y_conv, y_ret, x, mod4, mod_row_of_tile, norm_g, w_out_bf, w_router_bf, b_router):
    tokens = x.shape[0]
    tiles = tokens // OUTPROJ_TILE

    def mod_spec(which):
        return pl.BlockSpec((1, 1, 1, D_MODEL), lambda i: (mod_row_of_tile(i), which, 0, 0))

    tok = lambda w: pl.BlockSpec((OUTPROJ_TILE, w), lambda i: (i, 0))
    full = lambda a: pl.BlockSpec(a.shape, lambda i: (0,) * a.ndim)
    return pl.pallas_call(
        _outproj_kernel,
        grid=(tiles,),
        in_specs=[tok(CONV_W), tok(RET_W), tok(D_MODEL), mod_spec(2), mod_spec(3), mod_spec(4),
                  full(norm_g), full(w_out_bf), full(w_router_bf), full(b_router)],
        out_specs=[tok(D_MODEL),
                   pl.BlockSpec((1, LOCAL_ROWS, ROW_W), lambda i: (i, 0, 0)),
                   tok(ROUTER_COLS),
                   pl.BlockSpec((1, 1, ROUTER_COLS), lambda i: (i, 0, 0))],
        out_shape=[jax.ShapeDtypeStruct((tokens, D_MODEL), F32),
                   jax.ShapeDtypeStruct((tiles, LOCAL_ROWS, ROW_W), F32),
                   jax.ShapeDtypeStruct((tokens, ROUTER_COLS), F32),
                   jax.ShapeDtypeStruct((tiles, 1, ROUTER_COLS), F32)],
        compiler_params=pltpu.CompilerParams(
            dimension_semantics=("parallel",), vmem_limit_bytes=VMEM_LIMIT),
        name="outproj",
    )(y_conv, y_ret, x, mod4, mod4, mod4, norm_g, w_out_bf, w_router_bf, b_router)


RUN_PIECES = tuple(SUBLANES << b for b in reversed(range((OUTPROJ_TILE // SUBLANES).bit_length())))


DISPATCH_BUFFERS = 3


def _dispatch_kernel(ctx_tiles, run_len_ref, run_src_ref, run_dst_ref, pad_lo_ref, pad_hi_ref,
                     xloc_ctx_hbm, xloc_lat_hbm, xs_hbm, buf, zero_buf, in_sem, out_sem):
    i = pl.program_id(0)
    tiles = pl.num_programs(0)

    def run_copies(tile, act):
        slot = tile % DISPATCH_BUFFERS
        for g in range(N_GROUPS):
            n = run_len_ref[tile * N_GROUPS + g]
            src = run_src_ref[tile * N_GROUPS + g]
            dst = run_dst_ref[tile * N_GROUPS + g]
            for size in RUN_PIECES:
                done = n & (-2 * size)

                @pl.when((n & size) != 0)
                def _():
                    act(pltpu.make_async_copy(
                        buf.at[slot, pl.ds(pl.multiple_of(src + done, SUBLANES), size)],
                        xs_hbm.at[pl.ds(pl.multiple_of(dst + done, SUBLANES), size)], out_sem.at[slot]))

    def load(tile):
        slot = tile % DISPATCH_BUFFERS

        @pl.when(tile < ctx_tiles)
        def _():
            pltpu.make_async_copy(xloc_ctx_hbm.at[tile], buf.at[slot], in_sem.at[slot]).start()

        @pl.when(tile >= ctx_tiles)
        def _():
            pltpu.make_async_copy(xloc_lat_hbm.at[tile - ctx_tiles], buf.at[slot], in_sem.at[slot]).start()

    def zero_copy(r):
        return pltpu.make_async_copy(
            zero_buf, xs_hbm.at[pl.ds(pl.multiple_of(r * SUBLANES, SUBLANES), SUBLANES)], out_sem.at[0])

    start = lambda cp: cp.start()
    wait = lambda cp: cp.wait()

    @pl.when(i == 0)
    def _():
        load(i)

    @pl.when(i >= 2)
    def _():
        run_copies(i - 2, wait)

    @pl.when(i + 1 < tiles)
    def _():
        load(i + 1)

    slot = i % DISPATCH_BUFFERS
    pltpu.make_async_copy(xloc_lat_hbm.at[0], buf.at[slot], in_sem.at[slot]).wait()
    run_copies(i, start)

    @pl.when(i == tiles - 1)
    def _():
        @pl.when(i >= 1)
        def _():
            run_copies(i - 1, wait)
        run_copies(i, wait)
        zero_buf[...] = jnp.zeros_like(zero_buf)
        for act in (start, wait):
            for g in range(N_GROUPS):
                lax.fori_loop(pad_lo_ref[g], pad_hi_ref[g], lambda r, carry: (act(zero_copy(r)), carry)[1], 0)


def _dispatch(xloc_ctx, xloc_lat, run_len, run_src, run_dst, pad_lo, pad_hi, slots):
    ctx_tiles = xloc_ctx.shape[0]
    grid_spec = pltpu.PrefetchScalarGridSpec(
        num_scalar_prefetch=5,
        grid=(ctx_tiles + xloc_lat.shape[0],),
        in_specs=[pl.BlockSpec(memory_space=pl.ANY), pl.BlockSpec(memory_space=pl.ANY)],
        out_specs=pl.BlockSpec(memory_space=pl.ANY),
        scratch_shapes=[pltpu.VMEM((DISPATCH_BUFFERS, LOCAL_ROWS, ROW_W), F32), pltpu.VMEM((SUBLANES, ROW_W), F32),
                        pltpu.SemaphoreType.DMA((DISPATCH_BUFFERS,)), pltpu.SemaphoreType.DMA((DISPATCH_BUFFERS,))],
    )
    return pl.pallas_call(
        functools.partial(_dispatch_kernel, ctx_tiles),
        grid_spec=grid_spec,
        out_shape=jax.ShapeDtypeStruct((slots, ROW_W), F32),
        compiler_params=pltpu.CompilerParams(dimension_semantics=("arbitrary",)),
        name="dispatch",
    )(run_len, run_src, run_dst, pad_lo, pad_hi, xloc_ctx, xloc_lat)


def _expert_kernel(tile_group_ref, n_used_ref, xs_ref, w1_ref, w3_ref, w2_ref, ys_ref, xb, gate_tabs):
    j = pl.program_id(0)
    step = pl.program_id(1)

    @pl.when(j < n_used_ref[0])
    def _():
        @pl.when(step == 0)
        def _():
            xb[...] = xs_ref[:, :D_MODEL].astype(BF16)
            route = xs_ref[:, D_MODEL:]
            lane = lax.broadcasted_iota(jnp.int32, route.shape, 1)
            for n, which in enumerate((ROUTE_E1, ROUTE_E2, ROUTE_W1, ROUTE_W2)):
                col = jnp.sum(jnp.where(lane == which, route, 0.0), axis=1, keepdims=True)
                gate_tabs[n] = jnp.broadcast_to(col, route.shape)

        x = xb[...]
        total = None
        for s in range(EXPERTS_PER_STEP):
            expert = (tile_group_ref[j] * EXPERTS_PER_GROUP + step * EXPERTS_PER_STEP + s).astype(F32)
            gate = (jnp.where(gate_tabs[0] == expert, gate_tabs[2], 0.0)
                    + jnp.where(gate_tabs[1] == expert, gate_tabs[3], 0.0))
            hid = _silu(jnp.dot(x, w1_ref[0, s].astype(BF16), preferred_element_type=F32)) * jnp.dot(
                x, w3_ref[0, s].astype(BF16), preferred_element_type=F32)
            y = jnp.dot(hid.astype(BF16), w2_ref[0, s].astype(BF16), preferred_element_type=F32)
            gated = jnp.concatenate(
                [gate * y[:, c * LANES:(c + 1) * LANES] for c in range(D_MODEL // LANES)], axis=1)
            total = gated if total is None else total + gated

        @pl.when(step == 0)
        def _():
            ys_ref[...] = total

        @pl.when(step > 0)
        def _():
            ys_ref[...] += total


def _experts(xs, tile_group, n_used, w1, w3, w2):
    slots = xs.shape[0]
    steps = EXPERTS_PER_GROUP // EXPERTS_PER_STEP
    paired = lambda w: w.reshape((N_EXPERTS // EXPERTS_PER_STEP, EXPERTS_PER_STEP) + w.shape[1:])
    row_tile = lambda j, s, tg, nu: (jnp.minimum(j, nu[0] - 1), 0)
    w_spec = lambda shape: pl.BlockSpec((1, EXPERTS_PER_STEP) + shape, lambda j, s, tg, nu: (
        tg[jnp.minimum(j, nu[0] - 1)] * steps + jnp.where(j < nu[0], s, steps - 1), 0, 0, 0))
    grid_spec = pltpu.PrefetchScalarGridSpec(
        num_scalar_prefetch=2,
        grid=(slots // GROUP_TILE, steps),
        in_specs=[
            pl.BlockSpec((GROUP_TILE, ROW_W), row_tile),
            w_spec((D_MODEL, D_EXPERT)), w_spec((D_MODEL, D_EXPERT)), w_spec((D_EXPERT, D_MODEL)),
        ],
        out_specs=pl.BlockSpec((GROUP_TILE, D_MODEL), row_tile),
        scratch_shapes=[pltpu.VMEM((GROUP_TILE, D_MODEL), BF16), pltpu.VMEM((4, GROUP_TILE, LANES), F32)],
    )
    return pl.pallas_call(
        _expert_kernel,
        grid_spec=grid_spec,
        out_shape=jax.ShapeDtypeStruct((slots, D_MODEL), F32),
        compiler_params=pltpu.CompilerParams(
            dimension_semantics=("arbitrary", "arbitrary"), vmem_limit_bytes=EXPERT_VMEM_LIMIT),
        name="experts",
    )(tile_group, n_used, xs, paired(w1), paired(w3), paired(w2))


def _combine_kernel(n_tiles, tile_base, run_len_ref, run_src_ref, run_dst_ref,
                    x1_ref, route_ref, g2_ref, fg_ref, ys_hbm, o_ref, buf, sem):
    i = pl.program_id(0)
    slot = i % 2

    def run_copies(local_tile, act):
        s = local_tile % 2
        tile = local_tile + tile_base
        for g in range(N_GROUPS):
            n = run_len_ref[tile * N_GROUPS + g]
            src = run_src_ref[tile * N_GROUPS + g]
            dst = run_dst_ref[tile * N_GROUPS + g]
            for size in RUN_PIECES:
                done = n & (-2 * size)

                @pl.when((n & size) != 0)
                def _():
                    act(pltpu.make_async_copy(
                        ys_hbm.at[pl.ds(pl.multiple_of(dst + done, SUBLANES), size)],
                        buf.at[s, pl.ds(pl.multiple_of(src + done, SUBLANES), size)], sem.at[s]))

    @pl.when(i == 0)
    def _():
        buf[...] = jnp.zeros_like(buf)
        run_copies(i, lambda cp: cp.start())

    @pl.when(i + 1 < n_tiles)
    def _():
        run_copies(i + 1, lambda cp: cp.start())

    run_copies(i, lambda cp: cp.wait())
    route = route_ref[...]
    lane = lax.broadcasted_iota(jnp.int32, route.shape, 1)
    local = jnp.sum(jnp.where(lane == ROUTE_LOCAL, route, 0.0), axis=1, keepdims=True)
    pick = jnp.where(lax.broadcasted_iota(jnp.int32, (route.shape[0], LOCAL_ROWS), 1).astype(F32) == local,
                     1.0, 0.0).astype(BF16)
    moe = sum(jnp.dot(pick, piece, preferred_element_type=F32) for piece in _split3(buf[slot]))
    y = x1_ref[...] + g2_ref[0, 0] * moe
    o_ref[...] = _rms(y) * fg_ref[...]


def _combine(ys, runs, x1, route, mod4, mod_row_of_tile, final_g, tile_base):
    tokens = x1.shape[0]
    tiles = tokens // OUTPROJ_TILE
    tok = lambda w: pl.BlockSpec((OUTPROJ_TILE, w), lambda i, *_: (i, 0))
    grid_spec = pltpu.PrefetchScalarGridSpec(
        num_scalar_prefetch=3,
        grid=(tiles,),
        in_specs=[
            tok(D_MODEL), tok(ROUTER_COLS),
            pl.BlockSpec((1, 1, 1, D_MODEL), lambda i, *_: (mod_row_of_tile(i), 5, 0, 0)),
            pl.BlockSpec((1, D_MODEL), lambda i, *_: (0, 0)),
            pl.BlockSpec(memory_space=pl.ANY),
        ],
        out_specs=tok(D_MODEL),
        scratch_shapes=[pltpu.VMEM((2, LOCAL_ROWS, D_MODEL), F32), pltpu.SemaphoreType.DMA((2,))],
    )
    return pl.pallas_call(
        functools.partial(_combine_kernel, tiles, tile_base),
        grid_spec=grid_spec,
        out_shape=jax.ShapeDtypeStruct((tokens, D_MODEL), F32),
        compiler_params=pltpu.CompilerParams(
            dimension_semantics=("arbitrary",), vmem_limit_bytes=VMEM_LIMIT),
        name="combine",
    )(*runs, x1, route, mod4, final_g, ys)


def _routing_tables(counts):
    counts = counts.astype(jnp.int32)
    tiles = counts.shape[0]
    run_len = ((counts + SUBLANES - 1) // SUBLANES) * SUBLANES
    run_src = jnp.cumsum(run_len, axis=1) - run_len
    group_rows = jnp.sum(run_len, axis=0)
    padded = ((group_rows + GROUP_TILE - 1) // GROUP_TILE) * GROUP_TILE
    ends = jnp.cumsum(padded)
    offs = ends - padded
    run_dst = offs[None, :] + jnp.cumsum(run_len, axis=0) - run_len

    n_used = ends[-1] // GROUP_TILE
    max_rows = tiles * (OUTPROJ_TILE + N_GROUPS * (SUBLANES - 1))
    max_tiles = -(-max_rows // GROUP_TILE) + N_GROUPS
    tile_ids = jnp.minimum(jnp.arange(max_tiles, dtype=jnp.int32), n_used - 1)
    tile_group = jnp.sum(tile_ids[:, None] * GROUP_TILE >= ends[None, :], axis=1).astype(jnp.int32)
    flat = lambda a: a.reshape(-1)
    pads = ((offs + group_rows) // SUBLANES, ends // SUBLANES)
    return (flat(run_len), flat(run_src), flat(run_dst)), pads, tile_group, n_used.reshape(1), max_tiles * GROUP_TILE


def _mixer(x, mod4, mod_row, is_grid, s_f0, s_b0, p):
    norm_mix_g, w_in_bf, conv_w, decay_rows = p
    y_conv, q, k, v, g = _inproj(x, mod4, mod_row, norm_mix_g, w_in_bf, conv_w, is_grid)
    ret = _retention(q, k, v, g, decay_rows, s_f0, s_b0, emit_state=not is_grid)
    flat = lambda a: a.reshape(-1, a.shape[-1])
    return flat(y_conv), flat(ret[0]), ret[1:]


def kernel(x_prompt, x_sample, state_ret_fwd, state_ret_bwd, c, c_ctx, norm_mix_g, norm_ffn_g, w_ada, b_ada, w_in, conv_w, ret_decay_fwd, ret_decay_bwd, w_out, w_router_group, b_router_group, w_router_expert, b_router_expert, w_gate_e, w_up_e, w_down_e, final_norm_g):
    assert norm_mix_g.shape[0] == 1, "single-layer backbone"
    n_lat = c.shape[0]
    ctx_row = n_lat
    mod_rows = 8
    cvec = jnp.concatenate([c, c_ctx[None, :], jnp.zeros((mod_rows - n_lat - 1, D_MODEL), F32)], axis=0)
    mod = _modulation(cvec, w_ada[0], b_ada[0][None, :])
    mod4 = mod.reshape(mod_rows, 6, 1, D_MODEL)

    pad = ROUTER_COLS - N_GROUPS - N_EXPERTS
    w_router = jnp.concatenate(
        [w_router_group[0], w_router_expert[0], jnp.zeros((D_MODEL, pad), F32)], axis=1).astype(BF16)
    b_router = jnp.concatenate([b_router_group[0], b_router_expert[0], jnp.zeros((pad,), F32)])[None, :]
    decay_rows = jnp.broadcast_to(
        jnp.concatenate([ret_decay_fwd[0], ret_decay_bwd[0]])[:, None], (2 * RET_HEADS, LANES))
    p_mix = (norm_mix_g, w_in[0].astype(BF16), conv_w[0], decay_rows)
    w_out_bf = w_out[0].astype(BF16)
    final_g = final_norm_g[None, :]

    ctx_mod = lambda i: ctx_row
    lat_mod = lambda i: i // (x_sample.shape[1] // OUTPROJ_TILE)
    flat = lambda a: a.reshape(-1, a.shape[-1])

    yc_c, yr_c, (s_f, s_b) = _mixer(x_prompt, mod4, lambda b: ctx_row, False, None, None, p_mix)
    yc_l, yr_l, _ = _mixer(x_sample, mod4, lambda b: b, True, state_ret_fwd, state_ret_bwd, p_mix)

    x1_c, xloc_c, route_c, cnt_c = _outproj(
        yc_c, yr_c, flat(x_prompt), mod4, ctx_mod, norm_ffn_g, w_out_bf, w_router, b_router)
    x1_l, xloc_l, route_l, cnt_l = _outproj(
        yc_l, yr_l, flat(x_sample), mod4, lat_mod, norm_ffn_g, w_out_bf, w_router, b_router)

    counts = jnp.concatenate([cnt_c, cnt_l], axis=0)[:, 0, :N_GROUPS]
    runs, pads, tile_group, n_used, slots = _routing_tables(counts)
    xs = _dispatch(xloc_c, xloc_l, *runs, *pads, slots)
    ys = _experts(xs, tile_group, n_used, w_gate_e[0], w_up_e[0], w_down_e[0])
    y_prompt = _combine(ys, runs, x1_c, route_c, mod4, ctx_mod, final_g, 0)
    y_sample = _combine(ys, runs, x1_l, route_l, mod4, lat_mod, final_g, cnt_c.shape[0])
    return (y_prompt.reshape(x_prompt.shape), y_sample.reshape(x_sample.shape),
            s_f.astype(x_prompt.dtype), s_b.astype(x_prompt.dtype))
```

```python
import functools

import jax
import jax.numpy as jnp
from jax import lax
from jax.experimental import pallas as pl
from jax.experimental.pallas import tpu as pltpu

F32 = jnp.float32
BF16 = jnp.bfloat16

D_MODEL = 1024
GRID_W = 64
CONV_W = 512
RET_HEADS = 4
RET_DK = 128
RET_DV = 128
RET_W = RET_HEADS * RET_DV
QK_W = RET_HEADS * RET_DK
CHUNK = 128
N_GROUPS = 4
EXPERTS_PER_GROUP = 8
N_EXPERTS = N_GROUPS * EXPERTS_PER_GROUP
D_EXPERT = 256
ROPE_BASE = 10000.0
EPS = 1e-6

LANES = 128
TOKEN_TILE = 256
OUTPROJ_TILE = 512
GROUP_TILE = 1024
EXPERTS_PER_STEP = 2
RET_UNROLL = 8
SUBLANES = 8
LOCAL_ROWS = OUTPROJ_TILE + LANES
ROW_W = D_MODEL + LANES
ROUTE_GROUP, ROUTE_LOCAL, ROUTE_E1, ROUTE_E2, ROUTE_W1, ROUTE_W2 = range(6)
assert LOCAL_ROWS >= OUTPROJ_TILE + N_GROUPS * (SUBLANES - 1)
MOD_COLS = 1536
ROUTER_COLS = LANES
VMEM_LIMIT = 48 * 1024 * 1024
EXPERT_VMEM_LIMIT = 56 * 1024 * 1024


def _silu(x):
    return x * jax.nn.sigmoid(x)


def _rms(x):
    return x * lax.rsqrt(jnp.mean(x * x, axis=-1, keepdims=True) + EPS)


def _bdot(a, b):
    return jnp.dot(a.astype(BF16), b.astype(BF16), preferred_element_type=F32)


def _split3(x):
    hi = x.astype(BF16)
    rest = x - hi.astype(F32)
    mid = rest.astype(BF16)
    return hi, mid, (rest - mid.astype(F32)).astype(BF16)


def _mod_kernel(c_ref, w_ref, b_ref, o_ref):
    o_ref[...] = _bdot(_silu(c_ref[...]), w_ref[...]) + b_ref[...]


def _modulation(cvec, w_ada, b_ada):
    rows = cvec.shape[0]
    n = w_ada.shape[1]
    return pl.pallas_call(
        _mod_kernel,
        grid=(n // MOD_COLS,),
        in_specs=[
            pl.BlockSpec((rows, D_MODEL), lambda j: (0, 0)),
            pl.BlockSpec((D_MODEL, MOD_COLS), lambda j: (0, j)),
            pl.BlockSpec((1, MOD_COLS), lambda j: (0, j)),
        ],
        out_specs=pl.BlockSpec((rows, MOD_COLS), lambda j: (0, j)),
        out_shape=jax.ShapeDtypeStruct((rows, n), F32),
        compiler_params=pltpu.CompilerParams(vmem_limit_bytes=VMEM_LIMIT),
        name="modulation",
    )(cvec, w_ada, b_ada)


def _inproj_kernel(seg, is_grid, x_ref, sh_ref, sc_ref, ng_ref, w_ref, cw_ref, *rest):
    if is_grid:
        cos_ref, sa_ref, sb_ref, yc_ref, q_ref, k_ref, v_ref, g_ref = rest
    else:
        yc_ref, q_ref, k_ref, v_ref, g_ref = rest
    x = x_ref[0]
    xn = (_rms(x) * ng_ref[...]) * (1.0 + sc_ref[0, 0]) + sh_ref[0, 0]
    xb = xn.astype(BF16)

    def proj(c0, n):
        return jnp.dot(xb, w_ref[:, c0:c0 + n], preferred_element_type=F32)

    gate_b = proj(0, CONV_W)
    u = proj(CONV_W, CONV_W) * proj(2 * CONV_W, CONV_W)
    rows = u.shape[0]
    pos = lax.broadcasted_iota(jnp.int32, u.shape, 0) & (seg - 1)
    u_prev = jnp.where(pos != 0, pltpu.roll(u, 1, 0), 0.0)
    u_next = jnp.where(pos != seg - 1, pltpu.roll(u, rows - 1, 0), 0.0)
    conv = cw_ref[0:1, :] * u_prev + cw_ref[1:2, :] * u + cw_ref[2:3, :] * u_next
    yc_ref[0] = (gate_b * conv).astype(yc_ref.dtype)

    q0 = 3 * CONV_W
    q = proj(q0, QK_W)
    k = proj(q0 + QK_W, QK_W)
    if is_grid:
        cos, sa, sb = cos_ref[...], sa_ref[...], sb_ref[...]

        def rope(t):
            out = []
            for h in range(RET_HEADS):
                th = t[:, h * RET_DK:(h + 1) * RET_DK]
                out.append(th * cos + pltpu.roll(th, RET_DK - 1, 1) * sa + pltpu.roll(th, 1, 1) * sb)
            return jnp.concatenate(out, axis=1)

        q, k = rope(q), rope(k)
    q_ref[0] = q.astype(q_ref.dtype)
    k_ref[0] = k.astype(k_ref.dtype)
    v_ref[0] = proj(q0 + 2 * QK_W, RET_W).astype(v_ref.dtype)
    g_ref[0] = proj(q0 + 2 * QK_W + RET_W, RET_W).astype(g_ref.dtype)


def _rope_tables(length):
    pos = jnp.arange(length)
    row = (pos // GRID_W).astype(F32)
    col = (pos % GRID_W).astype(F32)
    n_pairs = RET_DK // 4
    freqs = ROPE_BASE ** (-(jnp.arange(n_pairs, dtype=F32) * 2.0 / (RET_DK // 2)))
    ang = jnp.concatenate([row[:, None] * freqs, col[:, None] * freqs], axis=-1)
    cos = jnp.repeat(jnp.cos(ang), 2, axis=-1)
    sin = jnp.repeat(jnp.sin(ang), 2, axis=-1)
    even = (jnp.arange(RET_DK) % 2) == 0
    return cos, jnp.where(even, -sin, 0.0), jnp.where(even, 0.0, sin)


def _inproj(x, mod4, mod_row, norm_g, w_in_bf, conv_w, is_grid):
    bsz, length, _ = x.shape
    seg = GRID_W if is_grid else length
    assert TOKEN_TILE % seg == 0 and length % TOKEN_TILE == 0
    tiles = length // TOKEN_TILE

    def mod_spec(which):
        return pl.BlockSpec((1, 1, 1, D_MODEL), lambda b, i: (mod_row(b), which, 0, 0))

    def tok_spec(width):
        return pl.BlockSpec((1, TOKEN_TILE, width), lambda b, i: (b, i, 0))

    in_specs = [
        tok_spec(D_MODEL), mod_spec(0), mod_spec(1),
        pl.BlockSpec((1, D_MODEL), lambda b, i: (0, 0)),
        pl.BlockSpec(w_in_bf.shape, lambda b, i: (0, 0)),
        pl.BlockSpec(conv_w.shape, lambda b, i: (0, 0)),
    ]
    args = [x, mod4, mod4, norm_g, w_in_bf, conv_w]
    if is_grid:
        in_specs += [pl.BlockSpec((TOKEN_TILE, RET_DK), lambda b, i: (i, 0))] * 3
        args += list(_rope_tables(length))
    shp = lambda w, dt: jax.ShapeDtypeStruct((bsz, length, w), dt)
    return pl.pallas_call(
        functools.partial(_inproj_kernel, seg, is_grid),
        grid=(bsz, tiles),
        in_specs=in_specs,
        out_specs=[tok_spec(CONV_W), tok_spec(QK_W), tok_spec(QK_W), tok_spec(RET_W), tok_spec(RET_W)],
        out_shape=[shp(CONV_W, BF16), shp(QK_W, BF16), shp(QK_W, BF16), shp(RET_W, BF16), shp(RET_W, BF16)],
        compiler_params=pltpu.CompilerParams(
            dimension_semantics=("parallel", "parallel"), vmem_limit_bytes=VMEM_LIMIT),
        name="inproj_grid" if is_grid else "inproj_seq",
    )(*args)


def _ret_kernel(n_chunks, heads, has_init, emit_state, a_ref, q_ref, k_ref, v_ref, g_ref, *rest):
    rest = list(rest)
    if has_init:
        sf0_ref, sb0_ref = rest[:2]
        rest = rest[2:]
    y_ref = rest.pop(0)
    if emit_state:
        sf_out, sb_out = rest[:2]
        rest = rest[2:]
    st_f, st_b, dec = rest
    c = CHUNK
    sq = (c, c)
    head0 = pl.program_id(0) * heads

    def log_decays(hh):
        lg_f = jnp.log1p(-jnp.exp(a_ref[pl.ds(head0 + hh, 1), :]))
        lg_b = jnp.log1p(-jnp.exp(a_ref[pl.ds(head0 + hh + RET_HEADS, 1), :]))
        return lg_f, lg_b

    @pl.when(pl.program_id(1) == 0)
    def _():
        row = lax.broadcasted_iota(jnp.int32, sq, 0).astype(F32)
        col = lax.broadcasted_iota(jnp.int32, sq, 1).astype(F32)
        scale = RET_DK ** -0.5
        for hh in range(heads):
            lg_f, lg_b = log_decays(hh)
            dec[hh, 0] = scale * (
                jnp.where(row >= col, jnp.exp(jnp.where(row >= col, row - col, 0.0) * lg_f), 0.0)
                + jnp.where(col >= row, jnp.exp(jnp.where(col >= row, col - row, 0.0) * lg_b), 0.0))
            dec[hh, 1] = jnp.exp((row + 1.0) * lg_f)
            dec[hh, 2] = jnp.exp((c - row) * lg_b)
            dec[hh, 3] = scale * jnp.exp((c - 1.0 - col) * lg_f)
            dec[hh, 4] = scale * jnp.exp(col * lg_b)

    def rows(n):
        return pl.ds(pl.multiple_of(n * c, c), c) if not isinstance(n, int) else pl.ds(n * c, c)

    def cols(hh):
        return slice(hh * RET_DK, (hh + 1) * RET_DK)

    def kv_step(hh, n):
        kt = jnp.transpose(k_ref[0, rows(n), cols(hh)].astype(F32))
        lhs = jnp.concatenate([kt * dec[hh, 3], kt * dec[hh, 4]], axis=0)
        kv = _bdot(lhs, v_ref[0, rows(n), cols(hh)])
        st_f[hh, n] = kv[:RET_DK]
        st_b[hh, n] = kv[RET_DK:]

    def scan(hh, st, decay, order, s):
        def step(i, s):
            n = order(i)
            kv = st[hh, n]
            st[hh, n] = s
            return s * decay + kv
        if n_chunks <= RET_UNROLL:
            for i in range(n_chunks):
                s = step(i, s)
            return s
        return lax.fori_loop(0, n_chunks, step, s, unroll=RET_UNROLL)

    def out_step(hh, n):
        q = q_ref[0, rows(n), cols(hh)].astype(F32)
        scores = lax.dot_general(q.astype(BF16), k_ref[0, rows(n), cols(hh)].astype(BF16),
                                 (((1,), (1,)), ((), ())), preferred_element_type=F32)
        o = _bdot(scores * dec[hh, 0], v_ref[0, rows(n), cols(hh)])
        q_dec = jnp.concatenate([q * dec[hh, 1], q * dec[hh, 2]], axis=1)
        o = o + _bdot(q_dec, jnp.concatenate([st_f[hh, n], st_b[hh, n]], axis=0))
        y = _silu(g_ref[0, rows(n), cols(hh)].astype(F32)) * _rms(o)
        y_ref[0, rows(n), cols(hh)] = y.astype(y_ref.dtype)

    def over_chunks(step):
        if n_chunks * heads <= RET_UNROLL:
            for hh in range(heads):
                for n in range(n_chunks):
                    step(hh, n)
        else:
            for hh in range(heads):
                lax.fori_loop(0, n_chunks, lambda n, carry: (step(hh, n), carry)[1], 0, unroll=RET_UNROLL)

    over_chunks(kv_step)
    finals = []
    for hh in range(heads):
        lg_f, lg_b = log_decays(hh)
        s_f = sf0_ref[0, 0, hh] if has_init else jnp.zeros(sq, F32)
        s_b = sb0_ref[0, 0, hh] if has_init else jnp.zeros(sq, F32)
        s_f = scan(hh, st_f, jnp.exp(c * lg_f), lambda i: i, s_f)
        s_b = scan(hh, st_b, jnp.exp(c * lg_b), lambda i: n_chunks - 1 - i, s_b)
        finals.append((s_f, s_b))
    over_chunks(out_step)
    if emit_state:
        for hh, (s_f, s_b) in enumerate(finals):
            sf_out[0, 0, hh] = s_f
            sb_out[0, 0, hh] = s_b


def _retention(q, k, v, g, decay_rows, s_f0, s_b0, emit_state):
    bsz, length, _ = q.shape
    n_chunks = length // CHUNK
    has_init = s_f0 is not None
    heads = RET_HEADS if n_chunks * RET_HEADS <= RET_UNROLL else 1
    head_spec = pl.BlockSpec((1, length, heads * RET_DK), lambda h, b: (b, 0, h))
    st_spec = pl.BlockSpec((1, 1, heads, RET_DK, RET_DV), lambda h, b: (b, 0, h, 0, 0))
    in_specs = [pl.BlockSpec(decay_rows.shape, lambda h, b: (0, 0))] + [head_spec] * 4
    args = [decay_rows, q, k, v, g]
    if has_init:
        in_specs += [st_spec, st_spec]
        args += [s_f0, s_b0]
    out_specs = [head_spec]
    out_shape = [jax.ShapeDtypeStruct((bsz, length, RET_W), BF16)]
    if emit_state:
        st_shape = jax.ShapeDtypeStruct((bsz, 1, RET_HEADS, RET_DK, RET_DV), F32)
        out_specs += [st_spec, st_spec]
        out_shape += [st_shape, st_shape]
    return pl.pallas_call(
        functools.partial(_ret_kernel, n_chunks, heads, has_init, emit_state),
        grid=(RET_HEADS // heads, bsz),
        in_specs=in_specs,
        out_specs=out_specs,
        out_shape=out_shape,
        scratch_shapes=[
            pltpu.VMEM((heads, n_chunks, RET_DK, RET_DV), F32),
            pltpu.VMEM((heads, n_chunks, RET_DK, RET_DV), F32),
            pltpu.VMEM((heads, 5, CHUNK, CHUNK), F32),
        ],
        compiler_params=pltpu.CompilerParams(
            dimension_semantics=("arbitrary", "arbitrary"), vmem_limit_bytes=VMEM_LIMIT),
        name="retention_init" if has_init else "retention_zero",
    )(*args)


def _route(logits):
    lane = lax.broadcasted_iota(jnp.int32, logits.shape, 1)
    lane_f = lane.astype(F32)
    neg = -jnp.inf
    far = float(LANES)
    is_g = lane < N_GROUPS
    lg = jnp.where(is_g, logits, neg)
    g_max = jnp.max(lg, axis=1, keepdims=True)
    g_idx = jnp.min(jnp.where(lg == g_max, lane_f, far), axis=1, keepdims=True)
    p_sel = 1.0 / jnp.sum(jnp.where(is_g, jnp.exp(lg - g_max), 0.0), axis=1, keepdims=True)
    lane_group = ((lane - N_GROUPS) >> 3).astype(F32)
    sel = (lane >= N_GROUPS) & (lane < N_GROUPS + N_EXPERTS) & (lane_group == g_idx)
    le = jnp.where(sel, logits, neg)
    v1 = jnp.max(le, axis=1, keepdims=True)
    i1 = jnp.min(jnp.where(le == v1, lane_f, far), axis=1, keepdims=True)
    le2 = jnp.where(lane_f == i1, neg, le)
    v2 = jnp.max(le2, axis=1, keepdims=True)
    i2 = jnp.min(jnp.where(le2 == v2, lane_f, far), axis=1, keepdims=True)
    e2 = jnp.exp(v2 - v1)
    w1 = p_sel * (1.0 / (1.0 + e2))
    w2 = p_sel * (e2 / (1.0 + e2))
    return lane, lane_f, g_idx, i1, i2, w1, w2


def _outproj_kernel(yc_ref, yr_ref, x_ref, g1_ref, sh_ref, sc_ref, ng_ref, wo_ref, wr_ref, br_ref,
                    x1_ref, xloc_ref, route_ref, cnt_ref):
    m = (jnp.dot(yc_ref[...], wo_ref[0:CONV_W, :], preferred_element_type=F32)
         + jnp.dot(yr_ref[...], wo_ref[CONV_W:, :], preferred_element_type=F32))
    x1 = x_ref[...] + g1_ref[0, 0] * m
    x1_ref[...] = x1
    xn = (_rms(x1) * ng_ref[...]) * (1.0 + sc_ref[0, 0]) + sh_ref[0, 0]
    xb = xn.astype(BF16)
    logits = jnp.dot(xb, wr_ref[...], preferred_element_type=F32) + br_ref[...]
    lane, lane_f, g_idx, i1, i2, w1, w2 = _route(logits)

    picks = jnp.where(lane_f == g_idx, 1.0, 0.0)
    rows = picks.shape[0]
    tri = (lax.broadcasted_iota(jnp.int32, (rows, rows), 0)
           > lax.broadcasted_iota(jnp.int32, (rows, rows), 1))
    before = jnp.dot(jnp.where(tri, 1.0, 0.0).astype(BF16), picks.astype(BF16),
                     preferred_element_type=F32)
    count = jnp.sum(picks, axis=0, keepdims=True)
    cnt_ref[0] = count
    count8 = jnp.broadcast_to(jnp.floor((count + (SUBLANES - 1)) * (1.0 / SUBLANES)) * SUBLANES,
                              (SUBLANES, LANES))
    lane8 = lane[:SUBLANES]
    start = sum(jnp.where(lane8 >= k, pltpu.roll(count8, k, 1), 0.0) for k in range(1, N_GROUPS))
    local = jnp.sum(jnp.where(lane_f == g_idx, before + start[0:1], 0.0), axis=1, keepdims=True)
    route = jnp.where(lane == ROUTE_GROUP, g_idx, jnp.where(lane == ROUTE_LOCAL, local, jnp.where(
        lane == ROUTE_E1, i1 - N_GROUPS, jnp.where(lane == ROUTE_E2, i2 - N_GROUPS, jnp.where(
            lane == ROUTE_W1, w1, jnp.where(lane == ROUTE_W2, w2, 0.0))))))
    route_ref[...] = route

    local_row = jnp.transpose(jnp.broadcast_to(local, (rows, LANES)))[0:1, :]
    place = jnp.where(lax.broadcasted_iota(jnp.int32, (LOCAL_ROWS, rows), 0).astype(F32) == local_row,
                      1.0, 0.0).astype(BF16)
    xloc_ref[0, :, :D_MODEL] = jnp.dot(place, xb, preferred_element_type=F32)
    xloc_ref[0, :, D_MODEL:] = sum(
        jnp.dot(place, piece, preferred_element_type=F32) for piece in _split3(route))


def _outproj(y_conv, y_ret, x, mod4, mod_row_of_tile, norm_g, w_out_bf, w_router_bf, b_router):
    tokens = x.shape[0]
    tiles = tokens // OUTPROJ_TILE

    def mod_spec(which):
        return pl.BlockSpec((1, 1, 1, D_MODEL), lambda i: (mod_row_of_tile(i), which, 0, 0))

    tok = lambda w: pl.BlockSpec((OUTPROJ_TILE, w), lambda i: (i, 0))
    full = lambda a: pl.BlockSpec(a.shape, lambda i: (0,) * a.ndim)
    return pl.pallas_call(
        _outproj_kernel,
        grid=(tiles,),
        in_specs=[tok(CONV_W), tok(RET_W), tok(D_MODEL), mod_spec(2), mod_spec(3), mod_spec(4),
                  full(norm_g), full(w_out_bf), full(w_router_bf), full(b_router)],
        out_specs=[tok(D_MODEL),
                   pl.BlockSpec((1, LOCAL_ROWS, ROW_W), lambda i: (i, 0, 0)),
                   tok(ROUTER_COLS),
                   pl.BlockSpec((1, 1, ROUTER_COLS), lambda i: (i, 0, 0))],
        out_shape=[jax.ShapeDtypeStruct((tokens, D_MODEL), F32),
                   jax.ShapeDtypeStruct((tiles, LOCAL_ROWS, ROW_W), F32),
                   jax.ShapeDtypeStruct((tokens, ROUTER_COLS), F32),
                   jax.ShapeDtypeStruct((tiles, 1, ROUTER_COLS), F32)],
        compiler_params=pltpu.CompilerParams(
            dimension_semantics=("parallel",), vmem_limit_bytes=VMEM_LIMIT),
        name="outproj",
    )(y_conv, y_ret, x, mod4, mod4, mod4, norm_g, w_out_bf, w_router_bf, b_router)


RUN_PIECES = tuple(SUBLANES << b for b in reversed(range((OUTPROJ_TILE // SUBLANES).bit_length())))


DISPATCH_BUFFERS = 3


def _dispatch_kernel(ctx_tiles, run_len_ref, run_src_ref, run_dst_ref, pad_lo_ref, pad_hi_ref,
                     xloc_ctx_hbm, xloc_lat_hbm, xs_hbm, buf, zero_buf, in_sem, out_sem):
    i = pl.program_id(0)
    tiles = pl.num_programs(0)

    def run_copies(tile, act):
        slot = tile % DISPATCH_BUFFERS
        for g in range(N_GROUPS):
            n = run_len_ref[tile * N_GROUPS + g]
            src = run_src_ref[tile * N_GROUPS + g]
            dst = run_dst_ref[tile * N_GROUPS + g]
            for size in RUN_PIECES:
                done = n & (-2 * size)

                @pl.when((n & size) != 0)
                def _():
                    act(pltpu.make_async_copy(
                        buf.at[slot, pl.ds(pl.multiple_of(src + done, SUBLANES), size)],
                        xs_hbm.at[pl.ds(pl.multiple_of(dst + done, SUBLANES), size)], out_sem.at[slot]))

    def load(tile):
        slot = tile % DISPATCH_BUFFERS

        @pl.when(tile < ctx_tiles)
        def _():
            pltpu.make_async_copy(xloc_ctx_hbm.at[tile], buf.at[slot], in_sem.at[slot]).start()

        @pl.when(tile >= ctx_tiles)
        def _():
            pltpu.make_async_copy(xloc_lat_hbm.at[tile - ctx_tiles], buf.at[slot], in_sem.at[slot]).start()

    def zero_copy(r):
        return pltpu.make_async_copy(
            zero_buf, xs_hbm.at[pl.ds(pl.multiple_of(r * SUBLANES, SUBLANES), SUBLANES)], out_sem.at[0])

    start = lambda cp: cp.start()
    wait = lambda cp: cp.wait()

    @pl.when(i == 0)
    def _():
        load(i)

    @pl.when(i >= 2)
    def _():
        run_copies(i - 2, wait)

    @pl.when(i + 1 < tiles)
    def _():
        load(i + 1)

    slot = i % DISPATCH_BUFFERS
    pltpu.make_async_copy(xloc_lat_hbm.at[0], buf.at[slot], in_sem.at[slot]).wait()
    run_copies(i, start)

    @pl.when(i == tiles - 1)
    def _():
        @pl.when(i >= 1)
        def _():
            run_copies(i - 1, wait)
        run_copies(i, wait)
        zero_buf[...] = jnp.zeros_like(zero_buf)
        for act in (start, wait):
            for g in range(N_GROUPS):
                lax.fori_loop(pad_lo_ref[g], pad_hi_ref[g], lambda r, carry: (act(zero_copy(r)), carry)[1], 0)


def _dispatch(xloc_ctx, xloc_lat, run_len, run_src, run_dst, pad_lo, pad_hi, slots):
    ctx_tiles = xloc_ctx.shape[0]
    grid_spec = pltpu.PrefetchScalarGridSpec(
        num_scalar_prefetch=5,
        grid=(ctx_tiles + xloc_lat.shape[0],),
        in_specs=[pl.BlockSpec(memory_space=pl.ANY), pl.BlockSpec(memory_space=pl.ANY)],
        out_specs=pl.BlockSpec(memory_space=pl.ANY),
        scratch_shapes=[pltpu.VMEM((DISPATCH_BUFFERS, LOCAL_ROWS, ROW_W), F32), pltpu.VMEM((SUBLANES, ROW_W), F32),
                        pltpu.SemaphoreType.DMA((DISPATCH_BUFFERS,)), pltpu.SemaphoreType.DMA((DISPATCH_BUFFERS,))],
    )
    return pl.pallas_call(
        functools.partial(_dispatch_kernel, ctx_tiles),
        grid_spec=grid_spec,
        out_shape=jax.ShapeDtypeStruct((slots, ROW_W), F32),
        compiler_params=pltpu.CompilerParams(dimension_semantics=("arbitrary",)),
        name="dispatch",
    )(run_len, run_src, run_dst, pad_lo, pad_hi, xloc_ctx, xloc_lat)


def _expert_kernel(tile_group_ref, n_used_ref, xs_ref, w1_ref, w3_ref, w2_ref, ys_ref, xb, gate_tabs):
    j = pl.program_id(0)
    step = pl.program_id(1)

    @pl.when(j < n_used_ref[0])
    def _():
        @pl.when(step == 0)
        def _():
            xb[...] = xs_ref[:, :D_MODEL].astype(BF16)
            route = xs_ref[:, D_MODEL:]
            lane = lax.broadcasted_iota(jnp.int32, route.shape, 1)
            for n, which in enumerate((ROUTE_E1, ROUTE_E2, ROUTE_W1, ROUTE_W2)):
                col = jnp.sum(jnp.where(lane == which, route, 0.0), axis=1, keepdims=True)
                gate_tabs[n] = jnp.broadcast_to(col, route.shape)

        x = xb[...]
        total = None
        for s in range(EXPERTS_PER_STEP):
            expert = (tile_group_ref[j] * EXPERTS_PER_GROUP + step * EXPERTS_PER_STEP + s).astype(F32)
            gate = (jnp.where(gate_tabs[0] == expert, gate_tabs[2], 0.0)
                    + jnp.where(gate_tabs[1] == expert, gate_tabs[3], 0.0))
            hid = _silu(jnp.dot(x, w1_ref[0, s].astype(BF16), preferred_element_type=F32)) * jnp.dot(
                x, w3_ref[0, s].astype(BF16), preferred_element_type=F32)
            y = jnp.dot(hid.astype(BF16), w2_ref[0, s].astype(BF16), preferred_element_type=F32)
            gated = jnp.concatenate(
                [gate * y[:, c * LANES:(c + 1) * LANES] for c in range(D_MODEL // LANES)], axis=1)
            total = gated if total is None else total + gated

        @pl.when(step == 0)
        def _():
            ys_ref[...] = total

        @pl.when(step > 0)
        def _():
            ys_ref[...] += total


def _experts(xs, tile_group, n_used, w1, w3, w2):
    slots = xs.shape[0]
    steps = EXPERTS_PER_GROUP // EXPERTS_PER_STEP
    paired = lambda w: w.reshape((N_EXPERTS // EXPERTS_PER_STEP, EXPERTS_PER_STEP) + w.shape[1:])
    row_tile = lambda j, s, tg, nu: (jnp.minimum(j, nu[0] - 1), 0)
    w_spec = lambda shape: pl.BlockSpec((1, EXPERTS_PER_STEP) + shape, lambda j, s, tg, nu: (
        tg[jnp.minimum(j, nu[0] - 1)] * steps + jnp.where(j < nu[0], s, steps - 1), 0, 0, 0))
    grid_spec = pltpu.PrefetchScalarGridSpec(
        num_scalar_prefetch=2,
        grid=(slots // GROUP_TILE, steps),
        in_specs=[
            pl.BlockSpec((GROUP_TILE, ROW_W), row_tile),
            w_spec((D_MODEL, D_EXPERT)), w_spec((D_MODEL, D_EXPERT)), w_spec((D_EXPERT, D_MODEL)),
        ],
        out_specs=pl.BlockSpec((GROUP_TILE, D_MODEL), row_tile),
        scratch_shapes=[pltpu.VMEM((GROUP_TILE, D_MODEL), BF16), pltpu.VMEM((4, GROUP_TILE, LANES), F32)],
    )
    return pl.pallas_call(
        _expert_kernel,
        grid_spec=grid_spec,
        out_shape=jax.ShapeDtypeStruct((slots, D_MODEL), F32),
        compiler_params=pltpu.CompilerParams(
            dimension_semantics=("arbitrary", "arbitrary"), vmem_limit_bytes=EXPERT_VMEM_LIMIT),
        name="experts",
    )(tile_group, n_used, xs, paired(w1), paired(w3), paired(w2))


def _combine_kernel(n_tiles, tile_base, run_len_ref, run_src_ref, run_dst_ref,
                    x1_ref, route_ref, g2_ref, fg_ref, ys_hbm, o_ref, buf, sem):
    i = pl.program_id(0)
    slot = i % 2

    def run_copies(local_tile, act):
        s = local_tile % 2
        tile = local_tile + tile_base
        for g in range(N_GROUPS):
            n = run_len_ref[tile * N_GROUPS + g]
            src = run_src_ref[tile * N_GROUPS + g]
            dst = run_dst_ref[tile * N_GROUPS + g]
            for size in RUN_PIECES:
                done = n & (-2 * size)

                @pl.when((n & size) != 0)
                def _():
                    act(pltpu.make_async_copy(
                        ys_hbm.at[pl.ds(pl.multiple_of(dst + done, SUBLANES), size)],
                        buf.at[s, pl.ds(pl.multiple_of(src + done, SUBLANES), size)], sem.at[s]))

    @pl.when(i == 0)
    def _():
        buf[...] = jnp.zeros_like(buf)
        run_copies(i, lambda cp: cp.start())

    @pl.when(i + 1 < n_tiles)
    def _():
        run_copies(i + 1, lambda cp: cp.start())

    run_copies(i, lambda cp: cp.wait())
    route = route_ref[...]
    lane = lax.broadcasted_iota(jnp.int32, route.shape, 1)
    local = jnp.sum(jnp.where(lane == ROUTE_LOCAL, route, 0.0), axis=1, keepdims=True)
    pick = jnp.where(lax.broadcasted_iota(jnp.int32, (route.shape[0], LOCAL_ROWS), 1).astype(F32) == local,
                     1.0, 0.0).astype(BF16)
    moe = sum(jnp.dot(pick, piece, preferred_element_type=F32) for piece in _split3(buf[slot]))
    y = x1_ref[...] + g2_ref[0, 0] * moe
    o_ref[...] = _rms(y) * fg_ref[...]


def _combine(ys, runs, x1, route, mod4, mod_row_of_tile, final_g, tile_base):
    tokens = x1.shape[0]
    tiles = tokens // OUTPROJ_TILE
    tok = lambda w: pl.BlockSpec((OUTPROJ_TILE, w), lambda i, *_: (i, 0))
    grid_spec = pltpu.PrefetchScalarGridSpec(
        num_scalar_prefetch=3,
        grid=(tiles,),
        in_specs=[
            tok(D_MODEL), tok(ROUTER_COLS),
            pl.BlockSpec((1, 1, 1, D_MODEL), lambda i, *_: (mod_row_of_tile(i), 5, 0, 0)),
            pl.BlockSpec((1, D_MODEL), lambda i, *_: (0, 0)),
            pl.BlockSpec(memory_space=pl.ANY),
        ],
        out_specs=tok(D_MODEL),
        scratch_shapes=[pltpu.VMEM((2, LOCAL_ROWS, D_MODEL), F32), pltpu.SemaphoreType.DMA((2,))],
    )
    return pl.pallas_call(
        functools.partial(_combine_kernel, tiles, tile_base),
        grid_spec=grid_spec,
        out_shape=jax.ShapeDtypeStruct((tokens, D_MODEL), F32),
        compiler_params=pltpu.CompilerParams(
            dimension_semantics=("arbitrary",), vmem_limit_bytes=VMEM_LIMIT),
        name="combine",
    )(*runs, x1, route, mod4, final_g, ys)


def _routing_tables(counts):
    counts = counts.astype(jnp.int32)
    tiles = counts.shape[0]
    run_len = ((counts + SUBLANES - 1) // SUBLANES) * SUBLANES
    run_src = jnp.cumsum(run_len, axis=1) - run_len
    group_rows = jnp.sum(run_len, axis=0)
    padded = ((group_rows + GROUP_TILE - 1) // GROUP_TILE) * GROUP_TILE
    ends = jnp.cumsum(padded)
    offs = ends - padded
    run_dst = offs[None, :] + jnp.cumsum(run_len, axis=0) - run_len

    n_used = ends[-1] // GROUP_TILE
    max_rows = tiles * (OUTPROJ_TILE + N_GROUPS * (SUBLANES - 1))
    max_tiles = -(-max_rows // GROUP_TILE) + N_GROUPS
    tile_ids = jnp.minimum(jnp.arange(max_tiles, dtype=jnp.int32), n_used - 1)
    tile_group = jnp.sum(tile_ids[:, None] * GROUP_TILE >= ends[None, :], axis=1).astype(jnp.int32)
    flat = lambda a: a.reshape(-1)
    pads = ((offs + group_rows) // SUBLANES, ends // SUBLANES)
    return (flat(run_len), flat(run_src), flat(run_dst)), pads, tile_group, n_used.reshape(1), max_tiles * GROUP_TILE


def _mixer(x, mod4, mod_row, is_grid, s_f0, s_b0, p):
    norm_mix_g, w_in_bf, conv_w, decay_rows = p
    y_conv, q, k, v, g = _inproj(x, mod4, mod_row, norm_mix_g, w_in_bf, conv_w, is_grid)
    ret = _retention(q, k, v, g, decay_rows, s_f0, s_b0, emit_state=not is_grid)
    flat = lambda a: a.reshape(-1, a.shape[-1])
    return flat(y_conv), flat(ret[0]), ret[1:]


def kernel(x_prompt, x_sample, state_ret_fwd, state_ret_bwd, c, c_ctx, norm_mix_g, norm_ffn_g, w_ada, b_ada, w_in, conv_w, ret_decay_fwd, ret_decay_bwd, w_out, w_router_group, b_router_group, w_router_expert, b_router_expert, w_gate_e, w_up_e, w_down_e, final_norm_g):
    assert norm_mix_g.shape[0] == 1, "single-layer backbone"
    n_lat = c.shape[0]
    ctx_row = n_lat
    mod_rows = 8
    cvec = jnp.concatenate([c, c_ctx[None, :], jnp.zeros((mod_rows - n_lat - 1, D_MODEL), F32)], axis=0)
    mod = _modulation(cvec, w_ada[0], b_ada[0][None, :])
    mod4 = mod.reshape(mod_rows, 6, 1, D_MODEL)

    pad = ROUTER_COLS - N_GROUPS - N_EXPERTS
    w_router = jnp.concatenate(
        [w_router_group[0], w_router_expert[0], jnp.zeros((D_MODEL, pad), F32)], axis=1).astype(BF16)
    b_router = jnp.concatenate([b_router_group[0], b_router_expert[0], jnp.zeros((pad,), F32)])[None, :]
    decay_rows = jnp.broadcast_to(
        jnp.concatenate([ret_decay_fwd[0], ret_decay_bwd[0]])[:, None], (2 * RET_HEADS, LANES))
    p_mix = (norm_mix_g, w_in[0].astype(BF16), conv_w[0], decay_rows)
    w_out_bf = w_out[0].astype(BF16)
    final_g = final_norm_g[None, :]

    ctx_mod = lambda i: ctx_row
    lat_mod = lambda i: i // (x_sample.shape[1] // OUTPROJ_TILE)
    flat = lambda a: a.reshape(-1, a.shape[-1])

    yc_c, yr_c, (s_f, s_b) = _mixer(x_prompt, mod4, lambda b: ctx_row, False, None, None, p_mix)
    yc_l, yr_l, _ = _mixer(x_sample, mod4, lambda b: b, True, state_ret_fwd, state_ret_bwd, p_mix)

    x1_c, xloc_c, route_c, cnt_c = _outproj(
        yc_c, yr_c, flat(x_prompt), mod4, ctx_mod, norm_ffn_g, w_out_bf, w_router, b_router)
    x1_l, xloc_l, route_l, cnt_l = _outproj(
        yc_l, yr_l, flat(x_sample), mod4, lat_mod, norm_ffn_g, w_out_bf, w_router, b_router)

    counts = jnp.concatenate([cnt_c, cnt_l], axis=0)[:, 0, :N_GROUPS]
    runs, pads, tile_group, n_used, slots = _routing_tables(counts)
    xs = _dispatch(xloc_c, xloc_l, *runs, *pads, slots)
    ys = _experts(xs, tile_group, n_used, w_gate_e[0], w_up_e[0], w_down_e[0])
    y_prompt = _combine(ys, runs, x1_c, route_c, mod4, ctx_mod, final_g, 0)
    y_sample = _combine(ys, runs, x1_l, route_l, mod4, lat_mod, final_g, cnt_c.shape[0])
    return (y_prompt.reshape(x_prompt.shape), y_sample.reshape(x_sample.shape),
            s_f.astype(x_prompt.dtype), s_b.astype(x_prompt.dtype))
```

```python
import functools

import jax
import jax.numpy as jnp
from jax import lax
from jax.experimental import pallas as pl
from jax.experimental.pallas import tpu as pltpu

F32 = jnp.float32
BF16 = jnp.bfloat16

D_MODEL = 1024
GRID_W = 64
CONV_W = 512
RET_HEADS = 4
RET_DK = 128
RET_DV = 128
RET_W = RET_HEADS * RET_DV
QK_W = RET_HEADS * RET_DK
CHUNK = 128
N_GROUPS = 4
EXPERTS_PER_GROUP = 8
N_EXPERTS = N_GROUPS * EXPERTS_PER_GROUP
D_EXPERT = 256
ROPE_BASE = 10000.0
EPS = 1e-6

LANES = 128
TOKEN_TILE = 256
OUTPROJ_TILE = 512
GROUP_TILE = 1024
EXPERTS_PER_STEP = 2
RET_UNROLL = 8
SUBLANES = 8
LOCAL_ROWS = OUTPROJ_TILE + LANES
ROW_W = D_MODEL + LANES
ROUTE_GROUP, ROUTE_LOCAL, ROUTE_E1, ROUTE_E2, ROUTE_W1, ROUTE_W2 = range(6)
assert LOCAL_ROWS >= OUTPROJ_TILE + N_GROUPS * (SUBLANES - 1)
MOD_COLS = 1536
ROUTER_COLS = LANES
VMEM_LIMIT = 48 * 1024 * 1024
EXPERT_VMEM_LIMIT = 56 * 1024 * 1024


def _silu(x):
    return x * jax.nn.sigmoid(x)


def _rms(x):
    return x * lax.rsqrt(jnp.mean(x * x, axis=-1, keepdims=True) + EPS)


def _bdot(a, b):
    return jnp.dot(a.astype(BF16), b.astype(BF16), preferred_element_type=F32)


def _split3(x):
    hi = x.astype(BF16)
    rest = x - hi.astype(F32)
    mid = rest.astype(BF16)
    return hi, mid, (rest - mid.astype(F32)).astype(BF16)


def _mod_kernel(c_ref, w_ref, b_ref, o_ref):
    o_ref[...] = _bdot(_silu(c_ref[...]), w_ref[...]) + b_ref[...]


def _modulation(cvec, w_ada, b_ada):
    rows = cvec.shape[0]
    n = w_ada.shape[1]
    return pl.pallas_call(
        _mod_kernel,
        grid=(n // MOD_COLS,),
        in_specs=[
            pl.BlockSpec((rows, D_MODEL), lambda j: (0, 0)),
            pl.BlockSpec((D_MODEL, MOD_COLS), lambda j: (0, j)),
            pl.BlockSpec((1, MOD_COLS), lambda j: (0, j)),
        ],
        out_specs=pl.BlockSpec((rows, MOD_COLS), lambda j: (0, j)),
        out_shape=jax.ShapeDtypeStruct((rows, n), F32),
        compiler_params=pltpu.CompilerParams(vmem_limit_bytes=VMEM_LIMIT),
        name="modulation",
    )(cvec, w_ada, b_ada)


def _inproj_kernel(seg, is_grid, x_ref, sh_ref, sc_ref, ng_ref, w_ref, cw_ref, *rest):
    if is_grid:
        cos_ref, sa_ref, sb_ref, yc_ref, q_ref, k_ref, v_ref, g_ref = rest
    else:
        yc_ref, q_ref, k_ref, v_ref, g_ref = rest
    x = x_ref[0]
    xn = (_rms(x) * ng_ref[...]) * (1.0 + sc_ref[0, 0]) + sh_ref[0, 0]
    xb = xn.astype(BF16)

    def proj(c0, n):
        return jnp.dot(xb, w_ref[:, c0:c0 + n], preferred_element_type=F32)

    gate_b = proj(0, CONV_W)
    u = proj(CONV_W, CONV_W) * proj(2 * CONV_W, CONV_W)
    rows = u.shape[0]
    pos = lax.broadcasted_iota(jnp.int32, u.shape, 0) & (seg - 1)
    u_prev = jnp.where(pos != 0, pltpu.roll(u, 1, 0), 0.0)
    u_next = jnp.where(pos != seg - 1, pltpu.roll(u, rows - 1, 0), 0.0)
    conv = cw_ref[0:1, :] * u_prev + cw_ref[1:2, :] * u + cw_ref[2:3, :] * u_next
    yc_ref[0] = (gate_b * conv).astype(yc_ref.dtype)

    q0 = 3 * CONV_W
    q = proj(q0, QK_W)
    k = proj(q0 + QK_W, QK_W)
    if is_grid:
        cos, sa, sb = cos_ref[...], sa_ref[...], sb_ref[...]

        def rope(t):
            out = []
            for h in range(RET_HEADS):
                th = t[:, h * RET_DK:(h + 1) * RET_DK]
                out.append(th * cos + pltpu.roll(th, RET_DK - 1, 1) * sa + pltpu.roll(th, 1, 1) * sb)
            return jnp.concatenate(out, axis=1)

        q, k = rope(q), rope(k)
    q_ref[0] = q
    k_ref[0] = k
    v_ref[0] = proj(q0 + 2 * QK_W, RET_W)
    g_ref[0] = proj(q0 + 2 * QK_W + RET_W, RET_W)


def _rope_tables(length):
    pos = jnp.arange(length)
    row = (pos // GRID_W).astype(F32)
    col = (pos % GRID_W).astype(F32)
    n_pairs = RET_DK // 4
    freqs = ROPE_BASE ** (-(jnp.arange(n_pairs, dtype=F32) * 2.0 / (RET_DK // 2)))
    ang = jnp.concatenate([row[:, None] * freqs, col[:, None] * freqs], axis=-1)
    cos = jnp.repeat(jnp.cos(ang), 2, axis=-1)
    sin = jnp.repeat(jnp.sin(ang), 2, axis=-1)
    even = (jnp.arange(RET_DK) % 2) == 0
    return cos, jnp.where(even, -sin, 0.0), jnp.where(even, 0.0, sin)


def _inproj(x, mod4, mod_row, norm_g, w_in_bf, conv_w, is_grid):
    bsz, length, _ = x.shape
    seg = GRID_W if is_grid else length
    assert TOKEN_TILE % seg == 0 and length % TOKEN_TILE == 0
    tiles = length // TOKEN_TILE

    def mod_spec(which):
        return pl.BlockSpec((1, 1, 1, D_MODEL), lambda b, i: (mod_row(b), which, 0, 0))

    def tok_spec(width):
        return pl.BlockSpec((1, TOKEN_TILE, width), lambda b, i: (b, i, 0))

    in_specs = [
        tok_spec(D_MODEL), mod_spec(0), mod_spec(1),
        pl.BlockSpec((1, D_MODEL), lambda b, i: (0, 0)),
        pl.BlockSpec(w_in_bf.shape, lambda b, i: (0, 0)),
        pl.BlockSpec(conv_w.shape, lambda b, i: (0, 0)),
    ]
    args = [x, mod4, mod4, norm_g, w_in_bf, conv_w]
    if is_grid:
        in_specs += [pl.BlockSpec((TOKEN_TILE, RET_DK), lambda b, i: (i, 0))] * 3
        args += list(_rope_tables(length))
    shp = lambda w, dt: jax.ShapeDtypeStruct((bsz, length, w), dt)
    return pl.pallas_call(
        functools.partial(_inproj_kernel, seg, is_grid),
        grid=(bsz, tiles),
        in_specs=in_specs,
        out_specs=[tok_spec(CONV_W), tok_spec(QK_W), tok_spec(QK_W), tok_spec(RET_W), tok_spec(RET_W)],
        out_shape=[shp(CONV_W, BF16), shp(QK_W, F32), shp(QK_W, F32), shp(RET_W, F32), shp(RET_W, F32)],
        compiler_params=pltpu.CompilerParams(
            dimension_semantics=("parallel", "parallel"), vmem_limit_bytes=VMEM_LIMIT),
        name="inproj_grid" if is_grid else "inproj_seq",
    )(*args)


def _ret_kernel(n_chunks, heads, has_init, emit_state, a_ref, q_ref, k_ref, v_ref, g_ref, *rest):
    rest = list(rest)
    if has_init:
        sf0_ref, sb0_ref = rest[:2]
        rest = rest[2:]
    y_ref = rest.pop(0)
    if emit_state:
        sf_out, sb_out = rest[:2]
        rest = rest[2:]
    st_f, st_b, dec = rest
    c = CHUNK
    sq = (c, c)
    head0 = pl.program_id(0) * heads

    def log_decays(hh):
        lg_f = jnp.log1p(-jnp.exp(a_ref[pl.ds(head0 + hh, 1), :]))
        lg_b = jnp.log1p(-jnp.exp(a_ref[pl.ds(head0 + hh + RET_HEADS, 1), :]))
        return lg_f, lg_b

    @pl.when(pl.program_id(1) == 0)
    def _():
        row = lax.broadcasted_iota(jnp.int32, sq, 0).astype(F32)
        col = lax.broadcasted_iota(jnp.int32, sq, 1).astype(F32)
        scale = RET_DK ** -0.5
        for hh in range(heads):
            lg_f, lg_b = log_decays(hh)
            dec[hh, 0] = scale * (
                jnp.where(row >= col, jnp.exp(jnp.where(row >= col, row - col, 0.0) * lg_f), 0.0)
                + jnp.where(col >= row, jnp.exp(jnp.where(col >= row, col - row, 0.0) * lg_b), 0.0))
            dec[hh, 1] = jnp.exp((row + 1.0) * lg_f)
            dec[hh, 2] = jnp.exp((c - row) * lg_b)
            dec[hh, 3] = scale * jnp.exp((c - 1.0 - col) * lg_f)
            dec[hh, 4] = scale * jnp.exp(col * lg_b)

    def rows(n):
        return pl.ds(pl.multiple_of(n * c, c), c) if not isinstance(n, int) else pl.ds(n * c, c)

    def cols(hh):
        return slice(hh * RET_DK, (hh + 1) * RET_DK)

    def kv_step(hh, n):
        kt = jnp.transpose(k_ref[0, rows(n), cols(hh)])
        lhs = jnp.concatenate([kt * dec[hh, 3], kt * dec[hh, 4]], axis=0)
        kv = _bdot(lhs, v_ref[0, rows(n), cols(hh)])
        st_f[hh, n] = kv[:RET_DK]
        st_b[hh, n] = kv[RET_DK:]

    def scan(hh, st, decay, order, s):
        def step(i, s):
            n = order(i)
            kv = st[hh, n]
            st[hh, n] = s
            return s * decay + kv
        if n_chunks <= RET_UNROLL:
            for i in range(n_chunks):
                s = step(i, s)
            return s
        return lax.fori_loop(0, n_chunks, step, s, unroll=RET_UNROLL)

    def out_step(hh, n):
        q = q_ref[0, rows(n), cols(hh)]
        scores = lax.dot_general(q.astype(BF16), k_ref[0, rows(n), cols(hh)].astype(BF16),
                                 (((1,), (1,)), ((), ())), preferred_element_type=F32)
        o = _bdot(scores * dec[hh, 0], v_ref[0, rows(n), cols(hh)])
        q_dec = jnp.concatenate([q * dec[hh, 1], q * dec[hh, 2]], axis=1)
        o = o + _bdot(q_dec, jnp.concatenate([st_f[hh, n], st_b[hh, n]], axis=0))
        y = _silu(g_ref[0, rows(n), cols(hh)]) * _rms(o)
        y_ref[0, rows(n), cols(hh)] = y.astype(y_ref.dtype)

    def over_chunks(step):
        if n_chunks * heads <= RET_UNROLL:
            for hh in range(heads):
                for n in range(n_chunks):
                    step(hh, n)
        else:
            for hh in range(heads):
                lax.fori_loop(0, n_chunks, lambda n, carry: (step(hh, n), carry)[1], 0, unroll=RET_UNROLL)

    over_chunks(kv_step)
    finals = []
    for hh in range(heads):
        lg_f, lg_b = log_decays(hh)
        s_f = sf0_ref[0, 0, hh] if has_init else jnp.zeros(sq, F32)
        s_b = sb0_ref[0, 0, hh] if has_init else jnp.zeros(sq, F32)
        s_f = scan(hh, st_f, jnp.exp(c * lg_f), lambda i: i, s_f)
        s_b = scan(hh, st_b, jnp.exp(c * lg_b), lambda i: n_chunks - 1 - i, s_b)
        finals.append((s_f, s_b))
    over_chunks(out_step)
    if emit_state:
        for hh, (s_f, s_b) in enumerate(finals):
            sf_out[0, 0, hh] = s_f
            sb_out[0, 0, hh] = s_b


def _retention(q, k, v, g, decay_rows, s_f0, s_b0, emit_state):
    bsz, length, _ = q.shape
    n_chunks = length // CHUNK
    has_init = s_f0 is not None
    heads = RET_HEADS if n_chunks * RET_HEADS <= RET_UNROLL else 1
    head_spec = pl.BlockSpec((1, length, heads * RET_DK), lambda h, b: (b, 0, h))
    st_spec = pl.BlockSpec((1, 1, heads, RET_DK, RET_DV), lambda h, b: (b, 0, h, 0, 0))
    in_specs = [pl.BlockSpec(decay_rows.shape, lambda h, b: (0, 0))] + [head_spec] * 4
    args = [decay_rows, q, k, v, g]
    if has_init:
        in_specs += [st_spec, st_spec]
        args += [s_f0, s_b0]
    out_specs = [head_spec]
    out_shape = [jax.ShapeDtypeStruct((bsz, length, RET_W), BF16)]
    if emit_state:
        st_shape = jax.ShapeDtypeStruct((bsz, 1, RET_HEADS, RET_DK, RET_DV), F32)
        out_specs += [st_spec, st_spec]
        out_shape += [st_shape, st_shape]
    return pl.pallas_call(
        functools.partial(_ret_kernel, n_chunks, heads, has_init, emit_state),
        grid=(RET_HEADS // heads, bsz),
        in_specs=in_specs,
        out_specs=out_specs,
        out_shape=out_shape,
        scratch_shapes=[
            pltpu.VMEM((heads, n_chunks, RET_DK, RET_DV), F32),
            pltpu.VMEM((heads, n_chunks, RET_DK, RET_DV), F32),
            pltpu.VMEM((heads, 5, CHUNK, CHUNK), F32),
        ],
        compiler_params=pltpu.CompilerParams(
            dimension_semantics=("arbitrary", "arbitrary"), vmem_limit_bytes=VMEM_LIMIT),
        name="retention_init" if has_init else "retention_zero",
    )(*args)


def _route(logits):
    lane = lax.broadcasted_iota(jnp.int32, logits.shape, 1)
    lane_f = lane.astype(F32)
    neg = -jnp.inf
    far = float(LANES)
    is_g = lane < N_GROUPS
    lg = jnp.where(is_g, logits, neg)
    g_max = jnp.max(lg, axis=1, keepdims=True)
    g_idx = jnp.min(jnp.where(lg == g_max, lane_f, far), axis=1, keepdims=True)
    p_sel = 1.0 / jnp.sum(jnp.where(is_g, jnp.exp(lg - g_max), 0.0), axis=1, keepdims=True)
    lane_group = ((lane - N_GROUPS) >> 3).astype(F32)
    sel = (lane >= N_GROUPS) & (lane < N_GROUPS + N_EXPERTS) & (lane_group == g_idx)
    le = jnp.where(sel, logits, neg)
    v1 = jnp.max(le, axis=1, keepdims=True)
    i1 = jnp.min(jnp.where(le == v1, lane_f, far), axis=1, keepdims=True)
    le2 = jnp.where(lane_f == i1, neg, le)
    v2 = jnp.max(le2, axis=1, keepdims=True)
    i2 = jnp.min(jnp.where(le2 == v2, lane_f, far), axis=1, keepdims=True)
    e2 = jnp.exp(v2 - v1)
    w1 = p_sel * (1.0 / (1.0 + e2))
    w2 = p_sel * (e2 / (1.0 + e2))
    return lane, lane_f, g_idx, i1, i2, w1, w2


def _outproj_kernel(yc_ref, yr_ref, x_ref, g1_ref, sh_ref, sc_ref, ng_ref, wo_ref, wr_ref, br_ref,
                    x1_ref, xloc_ref, route_ref, cnt_ref):
    m = (jnp.dot(yc_ref[...], wo_ref[0:CONV_W, :], preferred_element_type=F32)
         + jnp.dot(yr_ref[...], wo_ref[CONV_W:, :], preferred_element_type=F32))
    x1 = x_ref[...] + g1_ref[0, 0] * m
    x1_ref[...] = x1
    xn = (_rms(x1) * ng_ref[...]) * (1.0 + sc_ref[0, 0]) + sh_ref[0, 0]
    xb = xn.astype(BF16)
    logits = jnp.dot(xb, wr_ref[...], preferred_element_type=F32) + br_ref[...]
    lane, lane_f, g_idx, i1, i2, w1, w2 = _route(logits)

    picks = jnp.where(lane_f == g_idx, 1.0, 0.0)
    rows = picks.shape[0]
    tri = (lax.broadcasted_iota(jnp.int32, (rows, rows), 0)
           > lax.broadcasted_iota(jnp.int32, (rows, rows), 1))
    before = jnp.dot(jnp.where(tri, 1.0, 0.0).astype(BF16), picks.astype(BF16),
                     preferred_element_type=F32)
    count = jnp.sum(picks, axis=0, keepdims=True)
    cnt_ref[0] = count
    count8 = jnp.broadcast_to(jnp.floor((count + (SUBLANES - 1)) * (1.0 / SUBLANES)) * SUBLANES,
                              (SUBLANES, LANES))
    lane8 = lane[:SUBLANES]
    start = sum(jnp.where(lane8 >= k, pltpu.roll(count8, k, 1), 0.0) for k in range(1, N_GROUPS))
    local = jnp.sum(jnp.where(lane_f == g_idx, before + start[0:1], 0.0), axis=1, keepdims=True)
    route = jnp.where(lane == ROUTE_GROUP, g_idx, jnp.where(lane == ROUTE_LOCAL, local, jnp.where(
        lane == ROUTE_E1, i1 - N_GROUPS, jnp.where(lane == ROUTE_E2, i2 - N_GROUPS, jnp.where(
            lane == ROUTE_W1, w1, jnp.where(lane == ROUTE_W2, w2, 0.0))))))
    route_ref[...] = route

    local_row = jnp.transpose(jnp.broadcast_to(local, (rows, LANES)))[0:1, :]
    place = jnp.where(lax.broadcasted_iota(jnp.int32, (LOCAL_ROWS, rows), 0).astype(F32) == local_row,
                      1.0, 0.0).astype(BF16)
    xloc_ref[0, :, :D_MODEL] = jnp.dot(place, xb, preferred_element_type=F32)
    xloc_ref[0, :, D_MODEL:] = sum(
        jnp.dot(place, piece, preferred_element_type=F32) for piece in _split3(route))


def _outproj(y_conv, y_ret, x, mod4, mod_row_of_tile, norm_g, w_out_bf, w_router_bf, b_router):
    tokens = x.shape[0]
    tiles = tokens // OUTPROJ_TILE

    def mod_spec(which):
        return pl.BlockSpec((1, 1, 1, D_MODEL), lambda i: (mod_row_of_tile(i), which, 0, 0))

    tok = lambda w: pl.BlockSpec((OUTPROJ_TILE, w), lambda i: (i, 0))
    full = lambda a: pl.BlockSpec(a.shape, lambda i: (0,) * a.ndim)
    return pl.pallas_call(
        _outproj_kernel,
        grid=(tiles,),
        in_specs=[tok(CONV_W), tok(RET_W), tok(D_MODEL), mod_spec(2), mod_spec(3), mod_spec(4),
                  full(norm_g), full(w_out_bf), full(w_router_bf), full(b_router)],
        out_specs=[tok(D_MODEL),
                   pl.BlockSpec((1, LOCAL_ROWS, ROW_W), lambda i: (i, 0, 0)),
                   tok(ROUTER_COLS),
                   pl.BlockSpec((1, 1, ROUTER_COLS), lambda i: (i, 0, 0))],
        out_shape=[jax.ShapeDtypeStruct((tokens, D_MODEL), F32),
                   jax.ShapeDtypeStruct((tiles, LOCAL_ROWS, ROW_W), F32),
                   jax.ShapeDtypeStruct((tokens, ROUTER_COLS), F32),
                   jax.ShapeDtypeStruct((tiles, 1, ROUTER_COLS), F32)],
        compiler_params=pltpu.CompilerParams(
            dimension_semantics=("parallel",), vmem_limit_bytes=VMEM_LIMIT),
        name="outproj",
    )(y_conv, y_ret, x, mod4, mod4, mod4, norm_g, w_out_bf, w_router_bf, b_router)


RUN_PIECES = tuple(SUBLANES << b for b in reversed(range((OUTPROJ_TILE // SUBLANES).bit_length())))


def _expert_kernel(ctx_tiles, tile_group_ref, n_used_ref, run_len_ref, run_src_ref, run_dst_ref,
                   xloc_ctx_hbm, xloc_lat_hbm, w1_ref, w3_ref, w2_ref, ys_ref, xbuf, xb, gate_tabs, sem):
    j = pl.program_id(0)
    step = pl.program_id(1)
    n_used = n_used_ref[0]

    def tile_copies(tile, act):
        slot = tile % 2
        group = tile_group_ref[tile]
        row0 = tile * GROUP_TILE

        def from_token_tile(b, carry):
            run = b * N_GROUPS + group
            lo = jnp.maximum(run_dst_ref[run], row0)
            hi = jnp.minimum(run_dst_ref[run] + run_len_ref[run], row0 + GROUP_TILE)
            n = jnp.maximum(hi - lo, 0)
            src = run_src_ref[run] + lo - run_dst_ref[run]
            dst = lo - row0
            for size in RUN_PIECES:
                done = n & (-2 * size)
                take = (n & size) != 0
                into = xbuf.at[slot, pl.ds(pl.multiple_of(dst + done, SUBLANES), size)]
                rows = pl.ds(pl.multiple_of(src + done, SUBLANES), size)

                @pl.when(take & (b < ctx_tiles))
                def _():
                    act(pltpu.make_async_copy(xloc_ctx_hbm.at[b, rows], into, sem.at[slot]))

                @pl.when(take & (b >= ctx_tiles))
                def _():
                    act(pltpu.make_async_copy(xloc_lat_hbm.at[b - ctx_tiles, rows], into, sem.at[slot]))
            return carry

        lax.fori_loop(0, ctx_tiles + xloc_lat_hbm.shape[0], from_token_tile, 0)

    start = lambda cp: cp.start()
    wait = lambda cp: cp.wait()

    @pl.when(j < n_used)
    def _():
        @pl.when(step == 0)
        def _():
            @pl.when(j == 0)
            def _():
                xbuf[...] = jnp.zeros_like(xbuf)
                tile_copies(j, start)

            tile_copies(j, wait)
            rows_in = xbuf[j % 2]
            xb[...] = rows_in[:, :D_MODEL].astype(BF16)
            route = rows_in[:, D_MODEL:]
            lane = lax.broadcasted_iota(jnp.int32, route.shape, 1)
            for n, which in enumerate((ROUTE_E1, ROUTE_E2, ROUTE_W1, ROUTE_W2)):
                col = jnp.sum(jnp.where(lane == which, route, 0.0), axis=1, keepdims=True)
                gate_tabs[n] = jnp.broadcast_to(col, route.shape)

            @pl.when(j + 1 < n_used)
            def _():
                tile_copies(j + 1, start)

        x = xb[...]
        total = None
        for s in range(EXPERTS_PER_STEP):
            expert = (tile_group_ref[j] * EXPERTS_PER_GROUP + step * EXPERTS_PER_STEP + s).astype(F32)
            gate = (jnp.where(gate_tabs[0] == expert, gate_tabs[2], 0.0)
                    + jnp.where(gate_tabs[1] == expert, gate_tabs[3], 0.0))
            hid = _silu(jnp.dot(x, w1_ref[0, s].astype(BF16), preferred_element_type=F32)) * jnp.dot(
                x, w3_ref[0, s].astype(BF16), preferred_element_type=F32)
            y = jnp.dot(hid.astype(BF16), w2_ref[0, s].astype(BF16), preferred_element_type=F32)
            gated = jnp.concatenate(
                [gate * y[:, c * LANES:(c + 1) * LANES] for c in range(D_MODEL // LANES)], axis=1)
            total = gated if total is None else total + gated

        @pl.when(step == 0)
        def _():
            ys_ref[...] = total

        @pl.when(step > 0)
        def _():
            ys_ref[...] += total


def _experts(xloc_ctx, xloc_lat, tile_tables, runs, slots, w1, w3, w2):
    steps = EXPERTS_PER_GROUP // EXPERTS_PER_STEP
    paired = lambda w: w.reshape((N_EXPERTS // EXPERTS_PER_STEP, EXPERTS_PER_STEP) + w.shape[1:])
    w_spec = lambda shape: pl.BlockSpec((1, EXPERTS_PER_STEP) + shape, lambda j, s, tg, nu, *_: (
        tg[jnp.minimum(j, nu[0] - 1)] * steps + jnp.where(j < nu[0], s, steps - 1), 0, 0, 0))
    grid_spec = pltpu.PrefetchScalarGridSpec(
        num_scalar_prefetch=5,
        grid=(slots // GROUP_TILE, steps),
        in_specs=[
            pl.BlockSpec(memory_space=pl.ANY), pl.BlockSpec(memory_space=pl.ANY),
            w_spec((D_MODEL, D_EXPERT)), w_spec((D_MODEL, D_EXPERT)), w_spec((D_EXPERT, D_MODEL)),
        ],
        out_specs=pl.BlockSpec((GROUP_TILE, D_MODEL), lambda j, s, tg, nu, *_: (jnp.minimum(j, nu[0] - 1), 0)),
        scratch_shapes=[pltpu.VMEM((2, GROUP_TILE, ROW_W), F32), pltpu.VMEM((GROUP_TILE, D_MODEL), BF16),
                        pltpu.VMEM((4, GROUP_TILE, LANES), F32), pltpu.SemaphoreType.DMA((2,))],
    )
    return pl.pallas_call(
        functools.partial(_expert_kernel, xloc_ctx.shape[0]),
        grid_spec=grid_spec,
        out_shape=jax.ShapeDtypeStruct((slots, D_MODEL), F32),
        compiler_params=pltpu.CompilerParams(
            dimension_semantics=("arbitrary", "arbitrary"), vmem_limit_bytes=EXPERT_VMEM_LIMIT),
        name="experts",
    )(*tile_tables, *runs, xloc_ctx, xloc_lat, paired(w1), paired(w3), paired(w2))


def _combine_kernel(n_tiles, tile_base, run_len_ref, run_src_ref, run_dst_ref,
                    x1_ref, route_ref, g2_ref, fg_ref, ys_hbm, o_ref, buf, sem):
    i = pl.program_id(0)
    slot = i % 2

    def run_copies(local_tile, act):
        s = local_tile % 2
        tile = local_tile + tile_base
        for g in range(N_GROUPS):
            n = run_len_ref[tile * N_GROUPS + g]
            src = run_src_ref[tile * N_GROUPS + g]
            dst = run_dst_ref[tile * N_GROUPS + g]
            for size in RUN_PIECES:
                done = n & (-2 * size)

                @pl.when((n & size) != 0)
                def _():
                    act(pltpu.make_async_copy(
                        ys_hbm.at[pl.ds(pl.multiple_of(dst + done, SUBLANES), size)],
                        buf.at[s, pl.ds(pl.multiple_of(src + done, SUBLANES), size)], sem.at[s]))

    @pl.when(i == 0)
    def _():
        buf[...] = jnp.zeros_like(buf)
        run_copies(i, lambda cp: cp.start())

    @pl.when(i + 1 < n_tiles)
    def _():
        run_copies(i + 1, lambda cp: cp.start())

    run_copies(i, lambda cp: cp.wait())
    route = route_ref[...]
    lane = lax.broadcasted_iota(jnp.int32, route.shape, 1)
    local = jnp.sum(jnp.where(lane == ROUTE_LOCAL, route, 0.0), axis=1, keepdims=True)
    pick = jnp.where(lax.broadcasted_iota(jnp.int32, (route.shape[0], LOCAL_ROWS), 1).astype(F32) == local,
                     1.0, 0.0).astype(BF16)
    moe = sum(jnp.dot(pick, piece, preferred_element_type=F32) for piece in _split3(buf[slot]))
    y = x1_ref[...] + g2_ref[0, 0] * moe
    o_ref[...] = _rms(y) * fg_ref[...]


def _combine(ys, runs, x1, route, mod4, mod_row_of_tile, final_g, tile_base):
    tokens = x1.shape[0]
    tiles = tokens // OUTPROJ_TILE
    tok = lambda w: pl.BlockSpec((OUTPROJ_TILE, w), lambda i, *_: (i, 0))
    grid_spec = pltpu.PrefetchScalarGridSpec(
        num_scalar_prefetch=3,
        grid=(tiles,),
        in_specs=[
            tok(D_MODEL), tok(ROUTER_COLS),
            pl.BlockSpec((1, 1, 1, D_MODEL), lambda i, *_: (mod_row_of_tile(i), 5, 0, 0)),
            pl.BlockSpec((1, D_MODEL), lambda i, *_: (0, 0)),
            pl.BlockSpec(memory_space=pl.ANY),
        ],
        out_specs=tok(D_MODEL),
        scratch_shapes=[pltpu.VMEM((2, LOCAL_ROWS, D_MODEL), F32), pltpu.SemaphoreType.DMA((2,))],
    )
    return pl.pallas_call(
        functools.partial(_combine_kernel, tiles, tile_base),
        grid_spec=grid_spec,
        out_shape=jax.ShapeDtypeStruct((tokens, D_MODEL), F32),
        compiler_params=pltpu.CompilerParams(
            dimension_semantics=("arbitrary",), vmem_limit_bytes=VMEM_LIMIT),
        name="combine",
    )(*runs, x1, route, mod4, final_g, ys)


def _routing_tables(counts):
    counts = counts.astype(jnp.int32)
    tiles = counts.shape[0]
    run_len = ((counts + SUBLANES - 1) // SUBLANES) * SUBLANES
    run_src = jnp.cumsum(run_len, axis=1) - run_len
    group_rows = jnp.sum(run_len, axis=0)
    padded = ((group_rows + GROUP_TILE - 1) // GROUP_TILE) * GROUP_TILE
    ends = jnp.cumsum(padded)
    offs = ends - padded
    run_dst = offs[None, :] + jnp.cumsum(run_len, axis=0) - run_len

    n_used = ends[-1] // GROUP_TILE
    max_rows = tiles * (OUTPROJ_TILE + N_GROUPS * (SUBLANES - 1))
    max_tiles = -(-max_rows // GROUP_TILE) + N_GROUPS
    tile_ids = jnp.minimum(jnp.arange(max_tiles, dtype=jnp.int32), n_used - 1)
    tile_group = jnp.sum(tile_ids[:, None] * GROUP_TILE >= ends[None, :], axis=1).astype(jnp.int32)
    flat = lambda a: a.reshape(-1)
    return (tile_group, n_used.reshape(1)), (flat(run_len), flat(run_src), flat(run_dst)), max_tiles * GROUP_TILE


def _mixer(x, mod4, mod_row, is_grid, s_f0, s_b0, p):
    norm_mix_g, w_in_bf, conv_w, decay_rows = p
    y_conv, q, k, v, g = _inproj(x, mod4, mod_row, norm_mix_g, w_in_bf, conv_w, is_grid)
    ret = _retention(q, k, v, g, decay_rows, s_f0, s_b0, emit_state=not is_grid)
    flat = lambda a: a.reshape(-1, a.shape[-1])
    return flat(y_conv), flat(ret[0]), ret[1:]


def kernel(x_prompt, x_sample, state_ret_fwd, state_ret_bwd, c, c_ctx, norm_mix_g, norm_ffn_g, w_ada, b_ada, w_in, conv_w, ret_decay_fwd, ret_decay_bwd, w_out, w_router_group, b_router_group, w_router_expert, b_router_expert, w_gate_e, w_up_e, w_down_e, final_norm_g):
    assert norm_mix_g.shape[0] == 1, "single-layer backbone"
    n_lat = c.shape[0]
    ctx_row = n_lat
    mod_rows = 8
    cvec = jnp.concatenate([c, c_ctx[None, :], jnp.zeros((mod_rows - n_lat - 1, D_MODEL), F32)], axis=0)
    mod = _modulation(cvec, w_ada[0], b_ada[0][None, :])
    mod4 = mod.reshape(mod_rows, 6, 1, D_MODEL)

    pad = ROUTER_COLS - N_GROUPS - N_EXPERTS
    w_router = jnp.concatenate(
        [w_router_group[0], w_router_expert[0], jnp.zeros((D_MODEL, pad), F32)], axis=1).astype(BF16)
    b_router = jnp.concatenate([b_router_group[0], b_router_expert[0], jnp.zeros((pad,), F32)])[None, :]
    decay_rows = jnp.broadcast_to(
        jnp.concatenate([ret_decay_fwd[0], ret_decay_bwd[0]])[:, None], (2 * RET_HEADS, LANES))
    p_mix = (norm_mix_g, w_in[0].astype(BF16), conv_w[0], decay_rows)
    w_out_bf = w_out[0].astype(BF16)
    final_g = final_norm_g[None, :]

    ctx_mod = lambda i: ctx_row
    lat_mod = lambda i: i // (x_sample.shape[1] // OUTPROJ_TILE)
    flat = lambda a: a.reshape(-1, a.shape[-1])

    yc_c, yr_c, (s_f, s_b) = _mixer(x_prompt, mod4, lambda b: ctx_row, False, None, None, p_mix)
    yc_l, yr_l, _ = _mixer(x_sample, mod4, lambda b: b, True, state_ret_fwd, state_ret_bwd, p_mix)

    x1_c, xloc_c, route_c, cnt_c = _outproj(
        yc_c, yr_c, flat(x_prompt), mod4, ctx_mod, norm_ffn_g, w_out_bf, w_router, b_router)
    x1_l, xloc_l, route_l, cnt_l = _outproj(
        yc_l, yr_l, flat(x_sample), mod4, lat_mod, norm_ffn_g, w_out_bf, w_router, b_router)

    counts = jnp.concatenate([cnt_c, cnt_l], axis=0)[:, 0, :N_GROUPS]
    tile_tables, runs, slots = _routing_tables(counts)
    ys = _experts(xloc_c, xloc_l, tile_tables, runs, slots, w_gate_e[0], w_up_e[0], w_down_e[0])
    y_prompt = _combine(ys, runs, x1_c, route_c, mod4, ctx_mod, final_g, 0)
    y_sample = _combine(ys, runs, x1_l, route_l, mod4, lat_mod, final_g, cnt_c.shape[0])
    return (y_prompt.reshape(x_prompt.shape), y_sample.reshape(x_sample.shape),
            s_f.astype(x_prompt.dtype), s_b.astype(x_prompt.dtype))
```

```python
import functools

import jax
import jax.numpy as jnp
from jax import lax
from jax.experimental import pallas as pl
from jax.experimental.pallas import tpu as pltpu

F32 = jnp.float32
BF16 = jnp.bfloat16

D_MODEL = 1024
GRID_W = 64
CONV_W = 512
RET_HEADS = 4
RET_DK = 128
RET_DV = 128
RET_W = RET_HEADS * RET_DV
QK_W = RET_HEADS * RET_DK
CHUNK = 128
N_GROUPS = 4
EXPERTS_PER_GROUP = 8
N_EXPERTS = N_GROUPS * EXPERTS_PER_GROUP
D_EXPERT = 256
ROPE_BASE = 10000.0
EPS = 1e-6

LANES = 128
TOKEN_TILE = 256
OUTPROJ_TILE = 512
GROUP_TILE = 1024
EXPERTS_PER_STEP = 2
RET_UNROLL = 8
SUBLANES = 8
LOCAL_ROWS = OUTPROJ_TILE + LANES
ROW_W = D_MODEL + LANES
ROUTE_GROUP, ROUTE_LOCAL, ROUTE_E1, ROUTE_E2, ROUTE_W1, ROUTE_W2 = range(6)
assert LOCAL_ROWS >= OUTPROJ_TILE + N_GROUPS * (SUBLANES - 1)
MOD_COLS = 1536
ROUTER_COLS = LANES
VMEM_LIMIT = 48 * 1024 * 1024
EXPERT_VMEM_LIMIT = 56 * 1024 * 1024


def _silu(x):
    return x * jax.nn.sigmoid(x)


def _rms(x):
    return x * lax.rsqrt(jnp.mean(x * x, axis=-1, keepdims=True) + EPS)


def _bdot(a, b):
    return jnp.dot(a.astype(BF16), b.astype(BF16), preferred_element_type=F32)


def _split3(x):
    hi = x.astype(BF16)
    rest = x - hi.astype(F32)
    mid = rest.astype(BF16)
    return hi, mid, (rest - mid.astype(F32)).astype(BF16)


def _mod_kernel(c_ref, w_ref, b_ref, o_ref):
    o_ref[...] = _bdot(_silu(c_ref[...]), w_ref[...]) + b_ref[...]


def _modulation(cvec, w_ada, b_ada):
    rows = cvec.shape[0]
    n = w_ada.shape[1]
    return pl.pallas_call(
        _mod_kernel,
        grid=(n // MOD_COLS,),
        in_specs=[
            pl.BlockSpec((rows, D_MODEL), lambda j: (0, 0)),
            pl.BlockSpec((D_MODEL, MOD_COLS), lambda j: (0, j)),
            pl.BlockSpec((1, MOD_COLS), lambda j: (0, j)),
        ],
        out_specs=pl.BlockSpec((rows, MOD_COLS), lambda j: (0, j)),
        out_shape=jax.ShapeDtypeStruct((rows, n), F32),
        compiler_params=pltpu.CompilerParams(vmem_limit_bytes=VMEM_LIMIT),
        name="modulation",
    )(cvec, w_ada, b_ada)


def _inproj_kernel(seg, is_grid, x_ref, sh_ref, sc_ref, ng_ref, w_ref, cw_ref, *rest):
    if is_grid:
        cos_ref, sa_ref, sb_ref, yc_ref, q_ref, k_ref, v_ref, g_ref = rest
    else:
        yc_ref, q_ref, k_ref, v_ref, g_ref = rest
    x = x_ref[0]
    xn = (_rms(x) * ng_ref[...]) * (1.0 + sc_ref[0, 0]) + sh_ref[0, 0]
    xb = xn.astype(BF16)

    def proj(c0, n):
        return jnp.dot(xb, w_ref[:, c0:c0 + n], preferred_element_type=F32)

    gate_b = proj(0, CONV_W)
    u = proj(CONV_W, CONV_W) * proj(2 * CONV_W, CONV_W)
    rows = u.shape[0]
    pos = lax.broadcasted_iota(jnp.int32, u.shape, 0) & (seg - 1)
    u_prev = jnp.where(pos != 0, pltpu.roll(u, 1, 0), 0.0)
    u_next = jnp.where(pos != seg - 1, pltpu.roll(u, rows - 1, 0), 0.0)
    conv = cw_ref[0:1, :] * u_prev + cw_ref[1:2, :] * u + cw_ref[2:3, :] * u_next
    yc_ref[0] = (gate_b * conv).astype(yc_ref.dtype)

    q0 = 3 * CONV_W
    q = proj(q0, QK_W)
    k = proj(q0 + QK_W, QK_W)
    if is_grid:
        cos, sa, sb = cos_ref[...], sa_ref[...], sb_ref[...]

        def rope(t):
            out = []
            for h in range(RET_HEADS):
                th = t[:, h * RET_DK:(h + 1) * RET_DK]
                out.append(th * cos + pltpu.roll(th, RET_DK - 1, 1) * sa + pltpu.roll(th, 1, 1) * sb)
            return jnp.concatenate(out, axis=1)

        q, k = rope(q), rope(k)
    q_ref[0] = q
    k_ref[0] = k
    v_ref[0] = proj(q0 + 2 * QK_W, RET_W)
    g_ref[0] = proj(q0 + 2 * QK_W + RET_W, RET_W)


def _rope_tables(length):
    pos = jnp.arange(length)
    row = (pos // GRID_W).astype(F32)
    col = (pos % GRID_W).astype(F32)
    n_pairs = RET_DK // 4
    freqs = ROPE_BASE ** (-(jnp.arange(n_pairs, dtype=F32) * 2.0 / (RET_DK // 2)))
    ang = jnp.concatenate([row[:, None] * freqs, col[:, None] * freqs], axis=-1)
    cos = jnp.repeat(jnp.cos(ang), 2, axis=-1)
    sin = jnp.repeat(jnp.sin(ang), 2, axis=-1)
    even = (jnp.arange(RET_DK) % 2) == 0
    return cos, jnp.where(even, -sin, 0.0), jnp.where(even, 0.0, sin)


def _inproj(x, mod4, mod_row, norm_g, w_in_bf, conv_w, is_grid):
    bsz, length, _ = x.shape
    seg = GRID_W if is_grid else length
    assert TOKEN_TILE % seg == 0 and length % TOKEN_TILE == 0
    tiles = length // TOKEN_TILE

    def mod_spec(which):
        return pl.BlockSpec((1, 1, 1, D_MODEL), lambda b, i: (mod_row(b), which, 0, 0))

    def tok_spec(width):
        return pl.BlockSpec((1, TOKEN_TILE, width), lambda b, i: (b, i, 0))

    in_specs = [
        tok_spec(D_MODEL), mod_spec(0), mod_spec(1),
        pl.BlockSpec((1, D_MODEL), lambda b, i: (0, 0)),
        pl.BlockSpec(w_in_bf.shape, lambda b, i: (0, 0)),
        pl.BlockSpec(conv_w.shape, lambda b, i: (0, 0)),
    ]
    args = [x, mod4, mod4, norm_g, w_in_bf, conv_w]
    if is_grid:
        in_specs += [pl.BlockSpec((TOKEN_TILE, RET_DK), lambda b, i: (i, 0))] * 3
        args += list(_rope_tables(length))
    shp = lambda w, dt: jax.ShapeDtypeStruct((bsz, length, w), dt)
    return pl.pallas_call(
        functools.partial(_inproj_kernel, seg, is_grid),
        grid=(bsz, tiles),
        in_specs=in_specs,
        out_specs=[tok_spec(CONV_W), tok_spec(QK_W), tok_spec(QK_W), tok_spec(RET_W), tok_spec(RET_W)],
        out_shape=[shp(CONV_W, BF16), shp(QK_W, F32), shp(QK_W, F32), shp(RET_W, F32), shp(RET_W, F32)],
        compiler_params=pltpu.CompilerParams(
            dimension_semantics=("parallel", "parallel"), vmem_limit_bytes=VMEM_LIMIT),
        name="inproj_grid" if is_grid else "inproj_seq",
    )(*args)


def _ret_kernel(n_chunks, heads, has_init, emit_state, a_ref, q_ref, k_ref, v_ref, g_ref, *rest):
    rest = list(rest)
    if has_init:
        sf0_ref, sb0_ref = rest[:2]
        rest = rest[2:]
    y_ref = rest.pop(0)
    if emit_state:
        sf_out, sb_out = rest[:2]
        rest = rest[2:]
    st_f, st_b, dec = rest
    c = CHUNK
    sq = (c, c)
    head0 = pl.program_id(0) * heads

    def log_decays(hh):
        lg_f = jnp.log1p(-jnp.exp(a_ref[pl.ds(head0 + hh, 1), :]))
        lg_b = jnp.log1p(-jnp.exp(a_ref[pl.ds(head0 + hh + RET_HEADS, 1), :]))
        return lg_f, lg_b

    @pl.when(pl.program_id(1) == 0)
    def _():
        row = lax.broadcasted_iota(jnp.int32, sq, 0).astype(F32)
        col = lax.broadcasted_iota(jnp.int32, sq, 1).astype(F32)
        scale = RET_DK ** -0.5
        for hh in range(heads):
            lg_f, lg_b = log_decays(hh)
            dec[hh, 0] = scale * (
                jnp.where(row >= col, jnp.exp(jnp.where(row >= col, row - col, 0.0) * lg_f), 0.0)
                + jnp.where(col >= row, jnp.exp(jnp.where(col >= row, col - row, 0.0) * lg_b), 0.0))
            dec[hh, 1] = jnp.exp((row + 1.0) * lg_f)
            dec[hh, 2] = jnp.exp((c - row) * lg_b)
            dec[hh, 3] = scale * jnp.exp((c - 1.0 - col) * lg_f)
            dec[hh, 4] = scale * jnp.exp(col * lg_b)

    def rows(n):
        return pl.ds(pl.multiple_of(n * c, c), c) if not isinstance(n, int) else pl.ds(n * c, c)

    def cols(hh):
        return slice(hh * RET_DK, (hh + 1) * RET_DK)

    def kv_step(hh, n):
        kt = jnp.transpose(k_ref[0, rows(n), cols(hh)])
        lhs = jnp.concatenate([kt * dec[hh, 3], kt * dec[hh, 4]], axis=0)
        kv = _bdot(lhs, v_ref[0, rows(n), cols(hh)])
        st_f[hh, n] = kv[:RET_DK]
        st_b[hh, n] = kv[RET_DK:]

    def scan(hh, st, decay, order, s):
        def step(i, s):
            n = order(i)
            kv = st[hh, n]
            st[hh, n] = s
            return s * decay + kv
        if n_chunks <= RET_UNROLL:
            for i in range(n_chunks):
                s = step(i, s)
            return s
        return lax.fori_loop(0, n_chunks, step, s, unroll=RET_UNROLL)

    def out_step(hh, n):
        q = q_ref[0, rows(n), cols(hh)]
        scores = lax.dot_general(q.astype(BF16), k_ref[0, rows(n), cols(hh)].astype(BF16),
                                 (((1,), (1,)), ((), ())), preferred_element_type=F32)
        o = _bdot(scores * dec[hh, 0], v_ref[0, rows(n), cols(hh)])
        q_dec = jnp.concatenate([q * dec[hh, 1], q * dec[hh, 2]], axis=1)
        o = o + _bdot(q_dec, jnp.concatenate([st_f[hh, n], st_b[hh, n]], axis=0))
        y = _silu(g_ref[0, rows(n), cols(hh)]) * _rms(o)
        y_ref[0, rows(n), cols(hh)] = y.astype(y_ref.dtype)

    def over_chunks(step):
        if n_chunks * heads <= RET_UNROLL:
            for hh in range(heads):
                for n in range(n_chunks):
                    step(hh, n)
        else:
            for hh in range(heads):
                lax.fori_loop(0, n_chunks, lambda n, carry: (step(hh, n), carry)[1], 0, unroll=RET_UNROLL)

    over_chunks(kv_step)
    finals = []
    for hh in range(heads):
        lg_f, lg_b = log_decays(hh)
        s_f = sf0_ref[0, 0, hh] if has_init else jnp.zeros(sq, F32)
        s_b = sb0_ref[0, 0, hh] if has_init else jnp.zeros(sq, F32)
        s_f = scan(hh, st_f, jnp.exp(c * lg_f), lambda i: i, s_f)
        s_b = scan(hh, st_b, jnp.exp(c * lg_b), lambda i: n_chunks - 1 - i, s_b)
        finals.append((s_f, s_b))
    over_chunks(out_step)
    if emit_state:
        for hh, (s_f, s_b) in enumerate(finals):
            sf_out[0, 0, hh] = s_f
            sb_out[0, 0, hh] = s_b


def _retention(q, k, v, g, decay_rows, s_f0, s_b0, emit_state):
    bsz, length, _ = q.shape
    n_chunks = length // CHUNK
    has_init = s_f0 is not None
    heads = RET_HEADS if n_chunks * RET_HEADS <= RET_UNROLL else 1
    head_spec = pl.BlockSpec((1, length, heads * RET_DK), lambda h, b: (b, 0, h))
    st_spec = pl.BlockSpec((1, 1, heads, RET_DK, RET_DV), lambda h, b: (b, 0, h, 0, 0))
    in_specs = [pl.BlockSpec(decay_rows.shape, lambda h, b: (0, 0))] + [head_spec] * 4
    args = [decay_rows, q, k, v, g]
    if has_init:
        in_specs += [st_spec, st_spec]
        args += [s_f0, s_b0]
    out_specs = [head_spec]
    out_shape = [jax.ShapeDtypeStruct((bsz, length, RET_W), BF16)]
    if emit_state:
        st_shape = jax.ShapeDtypeStruct((bsz, 1, RET_HEADS, RET_DK, RET_DV), F32)
        out_specs += [st_spec, st_spec]
        out_shape += [st_shape, st_shape]
    return pl.pallas_call(
        functools.partial(_ret_kernel, n_chunks, heads, has_init, emit_state),
        grid=(RET_HEADS // heads, bsz),
        in_specs=in_specs,
        out_specs=out_specs,
        out_shape=out_shape,
        scratch_shapes=[
            pltpu.VMEM((heads, n_chunks, RET_DK, RET_DV), F32),
            pltpu.VMEM((heads, n_chunks, RET_DK, RET_DV), F32),
            pltpu.VMEM((heads, 5, CHUNK, CHUNK), F32),
        ],
        compiler_params=pltpu.CompilerParams(
            dimension_semantics=("arbitrary", "arbitrary"), vmem_limit_bytes=VMEM_LIMIT),
        name="retention_init" if has_init else "retention_zero",
    )(*args)


def _route(logits):
    lane = lax.broadcasted_iota(jnp.int32, logits.shape, 1)
    lane_f = lane.astype(F32)
    neg = -jnp.inf
    far = float(LANES)
    is_g = lane < N_GROUPS
    lg = jnp.where(is_g, logits, neg)
    g_max = jnp.max(lg, axis=1, keepdims=True)
    g_idx = jnp.min(jnp.where(lg == g_max, lane_f, far), axis=1, keepdims=True)
    p_sel = 1.0 / jnp.sum(jnp.where(is_g, jnp.exp(lg - g_max), 0.0), axis=1, keepdims=True)
    lane_group = ((lane - N_GROUPS) >> 3).astype(F32)
    sel = (lane >= N_GROUPS) & (lane < N_GROUPS + N_EXPERTS) & (lane_group == g_idx)
    le = jnp.where(sel, logits, neg)
    v1 = jnp.max(le, axis=1, keepdims=True)
    i1 = jnp.min(jnp.where(le == v1, lane_f, far), axis=1, keepdims=True)
    le2 = jnp.where(lane_f == i1, neg, le)
    v2 = jnp.max(le2, axis=1, keepdims=True)
    i2 = jnp.min(jnp.where(le2 == v2, lane_f, far), axis=1, keepdims=True)
    e2 = jnp.exp(v2 - v1)
    w1 = p_sel * (1.0 / (1.0 + e2))
    w2 = p_sel * (e2 / (1.0 + e2))
    return lane, lane_f, g_idx, i1, i2, w1, w2


def _outproj_kernel(yc_ref, yr_ref, x_ref, g1_ref, sh_ref, sc_ref, ng_ref, wo_ref, wr_ref, br_ref,
                    x1_ref, xloc_ref, route_ref, cnt_ref):
    m = (jnp.dot(yc_ref[...], wo_ref[0:CONV_W, :], preferred_element_type=F32)
         + jnp.dot(yr_ref[...], wo_ref[CONV_W:, :], preferred_element_type=F32))
    x1 = x_ref[...] + g1_ref[0, 0] * m
    x1_ref[...] = x1
    xn = (_rms(x1) * ng_ref[...]) * (1.0 + sc_ref[0, 0]) + sh_ref[0, 0]
    xb = xn.astype(BF16)
    logits = jnp.dot(xb, wr_ref[...], preferred_element_type=F32) + br_ref[...]
    lane, lane_f, g_idx, i1, i2, w1, w2 = _route(logits)

    picks = jnp.where(lane_f == g_idx, 1.0, 0.0)
    rows = picks.shape[0]
    tri = (lax.broadcasted_iota(jnp.int32, (rows, rows), 0)
           > lax.broadcasted_iota(jnp.int32, (rows, rows), 1))
    before = jnp.dot(jnp.where(tri, 1.0, 0.0).astype(BF16), picks.astype(BF16),
                     preferred_element_type=F32)
    count = jnp.sum(picks, axis=0, keepdims=True)
    cnt_ref[0] = count
    count8 = jnp.broadcast_to(jnp.floor((count + (SUBLANES - 1)) * (1.0 / SUBLANES)) * SUBLANES,
                              (SUBLANES, LANES))
    lane8 = lane[:SUBLANES]
    start = sum(jnp.where(lane8 >= k, pltpu.roll(count8, k, 1), 0.0) for k in range(1, N_GROUPS))
    local = jnp.sum(jnp.where(lane_f == g_idx, before + start[0:1], 0.0), axis=1, keepdims=True)
    route = jnp.where(lane == ROUTE_GROUP, g_idx, jnp.where(lane == ROUTE_LOCAL, local, jnp.where(
        lane == ROUTE_E1, i1 - N_GROUPS, jnp.where(lane == ROUTE_E2, i2 - N_GROUPS, jnp.where(
            lane == ROUTE_W1, w1, jnp.where(lane == ROUTE_W2, w2, 0.0))))))
    route_ref[...] = route

    local_row = jnp.transpose(jnp.broadcast_to(local, (rows, LANES)))[0:1, :]
    place = jnp.where(lax.broadcasted_iota(jnp.int32, (LOCAL_ROWS, rows), 0).astype(F32) == local_row,
                      1.0, 0.0).astype(BF16)
    xloc_ref[0, :, :D_MODEL] = jnp.dot(place, xb, preferred_element_type=F32)
    xloc_ref[0, :, D_MODEL:] = sum(
        jnp.dot(place, piece, preferred_element_type=F32) for piece in _split3(route))


def _outproj(y_conv, y_ret, x, mod4, mod_row_of_tile, norm_g, w_out_bf, w_router_bf, b_router):
    tokens = x.shape[0]
    tiles = tokens // OUTPROJ_TILE

    def mod_spec(which):
        return pl.BlockSpec((1, 1, 1, D_MODEL), lambda i: (mod_row_of_tile(i), which, 0, 0))

    tok = lambda w: pl.BlockSpec((OUTPROJ_TILE, w), lambda i: (i, 0))
    full = lambda a: pl.BlockSpec(a.shape, lambda i: (0,) * a.ndim)
    return pl.pallas_call(
        _outproj_kernel,
        grid=(tiles,),
        in_specs=[tok(CONV_W), tok(RET_W), tok(D_MODEL), mod_spec(2), mod_spec(3), mod_spec(4),
                  full(norm_g), full(w_out_bf), full(w_router_bf), full(b_router)],
        out_specs=[tok(D_MODEL),
                   pl.BlockSpec((1, LOCAL_ROWS, ROW_W), lambda i: (i, 0, 0)),
                   tok(ROUTER_COLS),
                   pl.BlockSpec((1, 1, ROUTER_COLS), lambda i: (i, 0, 0))],
        out_shape=[jax.ShapeDtypeStruct((tokens, D_MODEL), F32),
                   jax.ShapeDtypeStruct((tiles, LOCAL_ROWS, ROW_W), F32),
                   jax.ShapeDtypeStruct((tokens, ROUTER_COLS), F32),
                   jax.ShapeDtypeStruct((tiles, 1, ROUTER_COLS), F32)],
        compiler_params=pltpu.CompilerParams(
            dimension_semantics=("parallel",), vmem_limit_bytes=VMEM_LIMIT),
        name="outproj",
    )(y_conv, y_ret, x, mod4, mod4, mod4, norm_g, w_out_bf, w_router_bf, b_router)


RUN_PIECES = tuple(SUBLANES << b for b in reversed(range((OUTPROJ_TILE // SUBLANES).bit_length())))


def _expert_kernel(ctx_tiles, tile_group_ref, n_used_ref, first_ref, last_ref, run_len_ref, run_src_ref, run_dst_ref,
                   xloc_ctx_hbm, xloc_lat_hbm, w1_ref, w3_ref, w2_ref, ys_ref, xbuf, xb, gate_tabs, sem):
    j = pl.program_id(0)
    step = pl.program_id(1)
    n_used = n_used_ref[0]

    def tile_copies(tile, act):
        slot = tile % 2
        group = tile_group_ref[tile]
        row0 = tile * GROUP_TILE

        def from_token_tile(b, carry):
            run = b * N_GROUPS + group
            lo = jnp.maximum(run_dst_ref[run], row0)
            hi = jnp.minimum(run_dst_ref[run] + run_len_ref[run], row0 + GROUP_TILE)
            n = jnp.maximum(hi - lo, 0)
            src = run_src_ref[run] + lo - run_dst_ref[run]
            dst = lo - row0
            for size in RUN_PIECES:
                done = n & (-2 * size)
                take = (n & size) != 0
                into = xbuf.at[slot, pl.ds(pl.multiple_of(dst + done, SUBLANES), size)]
                rows = pl.ds(pl.multiple_of(src + done, SUBLANES), size)

                @pl.when(take & (b < ctx_tiles))
                def _():
                    act(pltpu.make_async_copy(xloc_ctx_hbm.at[b, rows], into, sem.at[slot]))

                @pl.when(take & (b >= ctx_tiles))
                def _():
                    act(pltpu.make_async_copy(xloc_lat_hbm.at[b - ctx_tiles, rows], into, sem.at[slot]))
            return carry

        lax.fori_loop(first_ref[tile], last_ref[tile] + 1, from_token_tile, 0)

    start = lambda cp: cp.start()
    wait = lambda cp: cp.wait()

    @pl.when(j < n_used)
    def _():
        @pl.when(step == 0)
        def _():
            @pl.when(j == 0)
            def _():
                xbuf[...] = jnp.zeros_like(xbuf)
                tile_copies(j, start)

            tile_copies(j, wait)
            rows_in = xbuf[j % 2]
            xb[...] = rows_in[:, :D_MODEL].astype(BF16)
            route = rows_in[:, D_MODEL:]
            lane = lax.broadcasted_iota(jnp.int32, route.shape, 1)
            for n, which in enumerate((ROUTE_E1, ROUTE_E2, ROUTE_W1, ROUTE_W2)):
                col = jnp.sum(jnp.where(lane == which, route, 0.0), axis=1, keepdims=True)
                gate_tabs[n] = jnp.broadcast_to(col, route.shape)

            @pl.when(j + 1 < n_used)
            def _():
                tile_copies(j + 1, start)

        x = xb[...]
        total = None
        for s in range(EXPERTS_PER_STEP):
            expert = (tile_group_ref[j] * EXPERTS_PER_GROUP + step * EXPERTS_PER_STEP + s).astype(F32)
            gate = (jnp.where(gate_tabs[0] == expert, gate_tabs[2], 0.0)
                    + jnp.where(gate_tabs[1] == expert, gate_tabs[3], 0.0))
            hid = _silu(jnp.dot(x, w1_ref[0, s].astype(BF16), preferred_element_type=F32)) * jnp.dot(
                x, w3_ref[0, s].astype(BF16), preferred_element_type=F32)
            y = jnp.dot(hid.astype(BF16), w2_ref[0, s].astype(BF16), preferred_element_type=F32)
            gated = jnp.concatenate(
                [gate * y[:, c * LANES:(c + 1) * LANES] for c in range(D_MODEL // LANES)], axis=1)
            total = gated if total is None else total + gated

        @pl.when(step == 0)
        def _():
            ys_ref[...] = total

        @pl.when(step > 0)
        def _():
            ys_ref[...] += total


def _experts(xloc_ctx, xloc_lat, tile_tables, runs, slots, w1, w3, w2):
    steps = EXPERTS_PER_GROUP // EXPERTS_PER_STEP
    paired = lambda w: w.reshape((N_EXPERTS // EXPERTS_PER_STEP, EXPERTS_PER_STEP) + w.shape[1:])
    w_spec = lambda shape: pl.BlockSpec((1, EXPERTS_PER_STEP) + shape, lambda j, s, tg, nu, *_: (
        tg[jnp.minimum(j, nu[0] - 1)] * steps + jnp.where(j < nu[0], s, steps - 1), 0, 0, 0))
    grid_spec = pltpu.PrefetchScalarGridSpec(
        num_scalar_prefetch=7,
        grid=(slots // GROUP_TILE, steps),
        in_specs=[
            pl.BlockSpec(memory_space=pl.ANY), pl.BlockSpec(memory_space=pl.ANY),
            w_spec((D_MODEL, D_EXPERT)), w_spec((D_MODEL, D_EXPERT)), w_spec((D_EXPERT, D_MODEL)),
        ],
        out_specs=pl.BlockSpec((GROUP_TILE, D_MODEL), lambda j, s, tg, nu, *_: (jnp.minimum(j, nu[0] - 1), 0)),
        scratch_shapes=[pltpu.VMEM((2, GROUP_TILE, ROW_W), F32), pltpu.VMEM((GROUP_TILE, D_MODEL), BF16),
                        pltpu.VMEM((4, GROUP_TILE, LANES), F32), pltpu.SemaphoreType.DMA((2,))],
    )
    return pl.pallas_call(
        functools.partial(_expert_kernel, xloc_ctx.shape[0]),
        grid_spec=grid_spec,
        out_shape=jax.ShapeDtypeStruct((slots, D_MODEL), F32),
        compiler_params=pltpu.CompilerParams(
            dimension_semantics=("arbitrary", "arbitrary"), vmem_limit_bytes=EXPERT_VMEM_LIMIT),
        name="experts",
    )(*tile_tables, *runs, xloc_ctx, xloc_lat, paired(w1), paired(w3), paired(w2))


def _combine_kernel(n_tiles, tile_base, run_len_ref, run_src_ref, run_dst_ref,
                    x1_ref, route_ref, g2_ref, fg_ref, ys_hbm, o_ref, buf, sem):
    i = pl.program_id(0)
    slot = i % 2

    def run_copies(local_tile, act):
        s = local_tile % 2
        tile = local_tile + tile_base
        for g in range(N_GROUPS):
            n = run_len_ref[tile * N_GROUPS + g]
            src = run_src_ref[tile * N_GROUPS + g]
            dst = run_dst_ref[tile * N_GROUPS + g]
            for size in RUN_PIECES:
                done = n & (-2 * size)

                @pl.when((n & size) != 0)
                def _():
                    act(pltpu.make_async_copy(
                        ys_hbm.at[pl.ds(pl.multiple_of(dst + done, SUBLANES), size)],
                        buf.at[s, pl.ds(pl.multiple_of(src + done, SUBLANES), size)], sem.at[s]))

    @pl.when(i == 0)
    def _():
        buf[...] = jnp.zeros_like(buf)
        run_copies(i, lambda cp: cp.start())

    @pl.when(i + 1 < n_tiles)
    def _():
        run_copies(i + 1, lambda cp: cp.start())

    run_copies(i, lambda cp: cp.wait())
    route = route_ref[...]
    lane = lax.broadcasted_iota(jnp.int32, route.shape, 1)
    local = jnp.sum(jnp.where(lane == ROUTE_LOCAL, route, 0.0), axis=1, keepdims=True)
    pick = jnp.where(lax.broadcasted_iota(jnp.int32, (route.shape[0], LOCAL_ROWS), 1).astype(F32) == local,
                     1.0, 0.0).astype(BF16)
    moe = sum(jnp.dot(pick, piece, preferred_element_type=F32) for piece in _split3(buf[slot]))
    y = x1_ref[...] + g2_ref[0, 0] * moe
    o_ref[...] = _rms(y) * fg_ref[...]


def _combine(ys, runs, x1, route, mod4, mod_row_of_tile, final_g, tile_base):
    tokens = x1.shape[0]
    tiles = tokens // OUTPROJ_TILE
    tok = lambda w: pl.BlockSpec((OUTPROJ_TILE, w), lambda i, *_: (i, 0))
    grid_spec = pltpu.PrefetchScalarGridSpec(
        num_scalar_prefetch=3,
        grid=(tiles,),
        in_specs=[
            tok(D_MODEL), tok(ROUTER_COLS),
            pl.BlockSpec((1, 1, 1, D_MODEL), lambda i, *_: (mod_row_of_tile(i), 5, 0, 0)),
            pl.BlockSpec((1, D_MODEL), lambda i, *_: (0, 0)),
            pl.BlockSpec(memory_space=pl.ANY),
        ],
        out_specs=tok(D_MODEL),
        scratch_shapes=[pltpu.VMEM((2, LOCAL_ROWS, D_MODEL), F32), pltpu.SemaphoreType.DMA((2,))],
    )
    return pl.pallas_call(
        functools.partial(_combine_kernel, tiles, tile_base),
        grid_spec=grid_spec,
        out_shape=jax.ShapeDtypeStruct((tokens, D_MODEL), F32),
        compiler_params=pltpu.CompilerParams(
            dimension_semantics=("arbitrary",), vmem_limit_bytes=VMEM_LIMIT),
        name="combine",
    )(*runs, x1, route, mod4, final_g, ys)


def _routing_tables(counts):
    counts = counts.astype(jnp.int32)
    tiles = counts.shape[0]
    run_len = ((counts + SUBLANES - 1) // SUBLANES) * SUBLANES
    run_src = jnp.cumsum(run_len, axis=1) - run_len
    group_rows = jnp.sum(run_len, axis=0)
    padded = ((group_rows + GROUP_TILE - 1) // GROUP_TILE) * GROUP_TILE
    ends = jnp.cumsum(padded)
    offs = ends - padded
    run_dst = offs[None, :] + jnp.cumsum(run_len, axis=0) - run_len

    n_used = ends[-1] // GROUP_TILE
    max_rows = tiles * (OUTPROJ_TILE + N_GROUPS * (SUBLANES - 1))
    max_tiles = -(-max_rows // GROUP_TILE) + N_GROUPS
    tile_ids = jnp.minimum(jnp.arange(max_tiles, dtype=jnp.int32), n_used - 1)
    tile_group = jnp.sum(tile_ids[:, None] * GROUP_TILE >= ends[None, :], axis=1).astype(jnp.int32)
    of_group = (tile_group[:, None] == jnp.arange(N_GROUPS, dtype=jnp.int32))[:, None, :]
    start = jnp.sum(jnp.where(of_group, run_dst[None], 0), axis=-1)
    stop = start + jnp.sum(jnp.where(of_group, run_len[None], 0), axis=-1)
    row0 = (tile_ids * GROUP_TILE)[:, None]
    first = jnp.sum(stop <= row0, axis=1).astype(jnp.int32)
    last = jnp.sum(start < row0 + GROUP_TILE, axis=1).astype(jnp.int32) - 1
    flat = lambda a: a.reshape(-1)
    return ((tile_group, n_used.reshape(1), first, last), (flat(run_len), flat(run_src), flat(run_dst)),
            max_tiles * GROUP_TILE)


def _mixer(x, mod4, mod_row, is_grid, s_f0, s_b0, p):
    norm_mix_g, w_in_bf, conv_w, decay_rows = p
    y_conv, q, k, v, g = _inproj(x, mod4, mod_row, norm_mix_g, w_in_bf, conv_w, is_grid)
    ret = _retention(q, k, v, g, decay_rows, s_f0, s_b0, emit_state=not is_grid)
    flat = lambda a: a.reshape(-1, a.shape[-1])
    return flat(y_conv), flat(ret[0]), ret[1:]


def kernel(x_prompt, x_sample, state_ret_fwd, state_ret_bwd, c, c_ctx, norm_mix_g, norm_ffn_g, w_ada, b_ada, w_in, conv_w, ret_decay_fwd, ret_decay_bwd, w_out, w_router_group, b_router_group, w_router_expert, b_router_expert, w_gate_e, w_up_e, w_down_e, final_norm_g):
    assert norm_mix_g.shape[0] == 1, "single-layer backbone"
    n_lat = c.shape[0]
    ctx_row = n_lat
    mod_rows = 8
    cvec = jnp.concatenate([c, c_ctx[None, :], jnp.zeros((mod_rows - n_lat - 1, D_MODEL), F32)], axis=0)
    mod = _modulation(cvec, w_ada[0], b_ada[0][None, :])
    mod4 = mod.reshape(mod_rows, 6, 1, D_MODEL)

    pad = ROUTER_COLS - N_GROUPS - N_EXPERTS
    w_router = jnp.concatenate(
        [w_router_group[0], w_router_expert[0], jnp.zeros((D_MODEL, pad), F32)], axis=1).astype(BF16)
    b_router = jnp.concatenate([b_router_group[0], b_router_expert[0], jnp.zeros((pad,), F32)])[None, :]
    decay_rows = jnp.broadcast_to(
        jnp.concatenate([ret_decay_fwd[0], ret_decay_bwd[0]])[:, None], (2 * RET_HEADS, LANES))
    p_mix = (norm_mix_g, w_in[0].astype(BF16), conv_w[0], decay_rows)
    w_out_bf = w_out[0].astype(BF16)
    final_g = final_norm_g[None, :]

    ctx_mod = lambda i: ctx_row
    lat_mod = lambda i: i // (x_sample.shape[1] // OUTPROJ_TILE)
    flat = lambda a: a.reshape(-1, a.shape[-1])

    yc_c, yr_c, (s_f, s_b) = _mixer(x_prompt, mod4, lambda b: ctx_row, False, None, None, p_mix)
    yc_l, yr_l, _ = _mixer(x_sample, mod4, lambda b: b, True, state_ret_fwd, state_ret_bwd, p_mix)

    x1_c, xloc_c, route_c, cnt_c = _outproj(
        yc_c, yr_c, flat(x_prompt), mod4, ctx_mod, norm_ffn_g, w_out_bf, w_router, b_router)
    x1_l, xloc_l, route_l, cnt_l = _outproj(
        yc_l, yr_l, flat(x_sample), mod4, lat_mod, norm_ffn_g, w_out_bf, w_router, b_router)

    counts = jnp.concatenate([cnt_c, cnt_l], axis=0)[:, 0, :N_GROUPS]
    tile_tables, runs, slots = _routing_tables(counts)
    ys = _experts(xloc_c, xloc_l, tile_tables, runs, slots, w_gate_e[0], w_up_e[0], w_down_e[0])
    y_prompt = _combine(ys, runs, x1_c, route_c, mod4, ctx_mod, final_g, 0)
    y_sample = _combine(ys, runs, x1_l, route_l, mod4, lat_mod, final_g, cnt_c.shape[0])
    return (y_prompt.reshape(x_prompt.shape), y_sample.reshape(x_sample.shape),
            s_f.astype(x_prompt.dtype), s_b.astype(x_prompt.dtype))
```

```python
import functools

import jax
import jax.numpy as jnp
from jax import lax
from jax.experimental import pallas as pl
from jax.experimental.pallas import tpu as pltpu

F32 = jnp.float32
BF16 = jnp.bfloat16

D_MODEL = 1024
GRID_W = 64
CONV_W = 512
RET_HEADS = 4
RET_DK = 128
RET_DV = 128
RET_W = RET_HEADS * RET_DV
QK_W = RET_HEADS * RET_DK
CHUNK = 128
N_GROUPS = 4
EXPERTS_PER_GROUP = 8
N_EXPERTS = N_GROUPS * EXPERTS_PER_GROUP
D_EXPERT = 256
ROPE_BASE = 10000.0
EPS = 1e-6

LANES = 128
TOKEN_TILE = 256
OUTPROJ_TILE = 512
GROUP_TILE = 1024
EXPERTS_PER_STEP = 2
RET_UNROLL = 8
SUBLANES = 8
LOCAL_ROWS = OUTPROJ_TILE + LANES
ROW_W = D_MODEL + LANES
ROUTE_GROUP, ROUTE_LOCAL, ROUTE_E1, ROUTE_E2, ROUTE_W1, ROUTE_W2 = range(6)
assert LOCAL_ROWS >= OUTPROJ_TILE + N_GROUPS * (SUBLANES - 1)
MOD_COLS = 1536
ROUTER_COLS = LANES
VMEM_LIMIT = 48 * 1024 * 1024
EXPERT_VMEM_LIMIT = 56 * 1024 * 1024


def _silu(x):
    return x * jax.nn.sigmoid(x)


def _rms(x):
    return x * lax.rsqrt(jnp.mean(x * x, axis=-1, keepdims=True) + EPS)


def _bdot(a, b):
    return jnp.dot(a.astype(BF16), b.astype(BF16), preferred_element_type=F32)


def _split3(x):
    hi = x.astype(BF16)
    rest = x - hi.astype(F32)
    mid = rest.astype(BF16)
    return hi, mid, (rest - mid.astype(F32)).astype(BF16)


def _mod_kernel(c_ref, w_ref, b_ref, o_ref):
    o_ref[...] = _bdot(_silu(c_ref[...]), w_ref[...]) + b_ref[...]


def _modulation(cvec, w_ada, b_ada):
    rows = cvec.shape[0]
    n = w_ada.shape[1]
    return pl.pallas_call(
        _mod_kernel,
        grid=(n // MOD_COLS,),
        in_specs=[
            pl.BlockSpec((rows, D_MODEL), lambda j: (0, 0)),
            pl.BlockSpec((D_MODEL, MOD_COLS), lambda j: (0, j)),
            pl.BlockSpec((1, MOD_COLS), lambda j: (0, j)),
        ],
        out_specs=pl.BlockSpec((rows, MOD_COLS), lambda j: (0, j)),
        out_shape=jax.ShapeDtypeStruct((rows, n), F32),
        compiler_params=pltpu.CompilerParams(vmem_limit_bytes=VMEM_LIMIT),
        name="modulation",
    )(cvec, w_ada, b_ada)


def _inproj_kernel(seg, is_grid, x_ref, sh_ref, sc_ref, ng_ref, w_ref, cw_ref, *rest):
    if is_grid:
        cos_ref, sa_ref, sb_ref, yc_ref, q_ref, k_ref, v_ref, g_ref = rest
    else:
        yc_ref, q_ref, k_ref, v_ref, g_ref = rest
    x = x_ref[0]
    xn = (_rms(x) * ng_ref[...]) * (1.0 + sc_ref[0, 0]) + sh_ref[0, 0]
    xb = xn.astype(BF16)

    def proj(c0, n):
        return jnp.dot(xb, w_ref[:, c0:c0 + n], preferred_element_type=F32)

    gate_b = proj(0, CONV_W)
    u = proj(CONV_W, CONV_W) * proj(2 * CONV_W, CONV_W)
    rows = u.shape[0]
    pos = lax.broadcasted_iota(jnp.int32, u.shape, 0) & (seg - 1)
    u_prev = jnp.where(pos != 0, pltpu.roll(u, 1, 0), 0.0)
    u_next = jnp.where(pos != seg - 1, pltpu.roll(u, rows - 1, 0), 0.0)
    conv = cw_ref[0:1, :] * u_prev + cw_ref[1:2, :] * u + cw_ref[2:3, :] * u_next
    yc_ref[0] = (gate_b * conv).astype(yc_ref.dtype)

    q0 = 3 * CONV_W
    q = proj(q0, QK_W)
    k = proj(q0 + QK_W, QK_W)
    if is_grid:
        cos, sa, sb = cos_ref[...], sa_ref[...], sb_ref[...]

        def rope(t):
            out = []
            for h in range(RET_HEADS):
                th = t[:, h * RET_DK:(h + 1) * RET_DK]
                out.append(th * cos + pltpu.roll(th, RET_DK - 1, 1) * sa + pltpu.roll(th, 1, 1) * sb)
            return jnp.concatenate(out, axis=1)

        q, k = rope(q), rope(k)
    q_ref[0] = q
    k_ref[0] = k
    v_ref[0] = proj(q0 + 2 * QK_W, RET_W)
    g_ref[0] = proj(q0 + 2 * QK_W + RET_W, RET_W)


def _rope_tables(length):
    pos = jnp.arange(length)
    row = (pos // GRID_W).astype(F32)
    col = (pos % GRID_W).astype(F32)
    n_pairs = RET_DK // 4
    freqs = ROPE_BASE ** (-(jnp.arange(n_pairs, dtype=F32) * 2.0 / (RET_DK // 2)))
    ang = jnp.concatenate([row[:, None] * freqs, col[:, None] * freqs], axis=-1)
    cos = jnp.repeat(jnp.cos(ang), 2, axis=-1)
    sin = jnp.repeat(jnp.sin(ang), 2, axis=-1)
    even = (jnp.arange(RET_DK) % 2) == 0
    return cos, jnp.where(even, -sin, 0.0), jnp.where(even, 0.0, sin)


def _inproj(x, mod4, mod_row, norm_g, w_in_bf, conv_w, is_grid):
    bsz, length, _ = x.shape
    seg = GRID_W if is_grid else length
    assert TOKEN_TILE % seg == 0 and length % TOKEN_TILE == 0
    tiles = length // TOKEN_TILE

    def mod_spec(which):
        return pl.BlockSpec((1, 1, 1, D_MODEL), lambda b, i: (mod_row(b), which, 0, 0))

    def tok_spec(width):
        return pl.BlockSpec((1, TOKEN_TILE, width), lambda b, i: (b, i, 0))

    in_specs = [
        tok_spec(D_MODEL), mod_spec(0), mod_spec(1),
        pl.BlockSpec((1, D_MODEL), lambda b, i: (0, 0)),
        pl.BlockSpec(w_in_bf.shape, lambda b, i: (0, 0)),
        pl.BlockSpec(conv_w.shape, lambda b, i: (0, 0)),
    ]
    args = [x, mod4, mod4, norm_g, w_in_bf, conv_w]
    if is_grid:
        in_specs += [pl.BlockSpec((TOKEN_TILE, RET_DK), lambda b, i: (i, 0))] * 3
        args += list(_rope_tables(length))
    shp = lambda w, dt: jax.ShapeDtypeStruct((bsz, length, w), dt)
    return pl.pallas_call(
        functools.partial(_inproj_kernel, seg, is_grid),
        grid=(bsz, tiles),
        in_specs=in_specs,
        out_specs=[tok_spec(CONV_W), tok_spec(QK_W), tok_spec(QK_W), tok_spec(RET_W), tok_spec(RET_W)],
        out_shape=[shp(CONV_W, BF16), shp(QK_W, F32), shp(QK_W, F32), shp(RET_W, F32), shp(RET_W, F32)],
        compiler_params=pltpu.CompilerParams(
            dimension_semantics=("parallel", "parallel"), vmem_limit_bytes=VMEM_LIMIT),
        name="inproj_grid" if is_grid else "inproj_seq",
    )(*args)


def _ret_kernel(n_chunks, heads, has_init, emit_state, a_ref, q_ref, k_ref, v_ref, g_ref, *rest):
    rest = list(rest)
    if has_init:
        sf0_ref, sb0_ref = rest[:2]
        rest = rest[2:]
    y_ref = rest.pop(0)
    if emit_state:
        sf_out, sb_out = rest[:2]
        rest = rest[2:]
    st_f, st_b, dec = rest
    c = CHUNK
    sq = (c, c)
    head0 = pl.program_id(0) * heads

    def log_decays(hh):
        lg_f = jnp.log1p(-jnp.exp(a_ref[pl.ds(head0 + hh, 1), :]))
        lg_b = jnp.log1p(-jnp.exp(a_ref[pl.ds(head0 + hh + RET_HEADS, 1), :]))
        return lg_f, lg_b

    @pl.when(pl.program_id(1) == 0)
    def _():
        row = lax.broadcasted_iota(jnp.int32, sq, 0).astype(F32)
        col = lax.broadcasted_iota(jnp.int32, sq, 1).astype(F32)
        scale = RET_DK ** -0.5
        for hh in range(heads):
            lg_f, lg_b = log_decays(hh)
            dec[hh, 0] = scale * (
                jnp.where(row >= col, jnp.exp(jnp.where(row >= col, row - col, 0.0) * lg_f), 0.0)
                + jnp.where(col >= row, jnp.exp(jnp.where(col >= row, col - row, 0.0) * lg_b), 0.0))
            dec[hh, 1] = jnp.exp((row + 1.0) * lg_f)
            dec[hh, 2] = jnp.exp((c - row) * lg_b)
            dec[hh, 3] = scale * jnp.exp((c - 1.0 - col) * lg_f)
            dec[hh, 4] = scale * jnp.exp(col * lg_b)

    def rows(n):
        return pl.ds(pl.multiple_of(n * c, c), c) if not isinstance(n, int) else pl.ds(n * c, c)

    def cols(hh):
        return slice(hh * RET_DK, (hh + 1) * RET_DK)

    def kv_step(hh, n):
        kt = jnp.transpose(k_ref[0, rows(n), cols(hh)])
        lhs = jnp.concatenate([kt * dec[hh, 3], kt * dec[hh, 4]], axis=0)
        kv = _bdot(lhs, v_ref[0, rows(n), cols(hh)])
        st_f[hh, n] = kv[:RET_DK]
        st_b[hh, n] = kv[RET_DK:]

    def scan(hh, st, decay, order, s):
        def step(i, s):
            n = order(i)
            kv = st[hh, n]
            st[hh, n] = s
            return s * decay + kv
        if n_chunks <= RET_UNROLL:
            for i in range(n_chunks):
                s = step(i, s)
            return s
        return lax.fori_loop(0, n_chunks, step, s, unroll=RET_UNROLL)

    def out_step(hh, n):
        q = q_ref[0, rows(n), cols(hh)]
        scores = lax.dot_general(q.astype(BF16), k_ref[0, rows(n), cols(hh)].astype(BF16),
                                 (((1,), (1,)), ((), ())), preferred_element_type=F32)
        o = _bdot(scores * dec[hh, 0], v_ref[0, rows(n), cols(hh)])
        q_dec = jnp.concatenate([q * dec[hh, 1], q * dec[hh, 2]], axis=1)
        o = o + _bdot(q_dec, jnp.concatenate([st_f[hh, n], st_b[hh, n]], axis=0))
        y = _silu(g_ref[0, rows(n), cols(hh)]) * _rms(o)
        y_ref[0, rows(n), cols(hh)] = y.astype(y_ref.dtype)

    def over_chunks(step):
        if n_chunks * heads <= RET_UNROLL:
            for hh in range(heads):
                for n in range(n_chunks):
                    step(hh, n)
        else:
            for hh in range(heads):
                lax.fori_loop(0, n_chunks, lambda n, carry: (step(hh, n), carry)[1], 0, unroll=RET_UNROLL)

    over_chunks(kv_step)
    finals = []
    for hh in range(heads):
        lg_f, lg_b = log_decays(hh)
        s_f = sf0_ref[0, 0, hh] if has_init else jnp.zeros(sq, F32)
        s_b = sb0_ref[0, 0, hh] if has_init else jnp.zeros(sq, F32)
        s_f = scan(hh, st_f, jnp.exp(c * lg_f), lambda i: i, s_f)
        s_b = scan(hh, st_b, jnp.exp(c * lg_b), lambda i: n_chunks - 1 - i, s_b)
        finals.append((s_f, s_b))
    over_chunks(out_step)
    if emit_state:
        for hh, (s_f, s_b) in enumerate(finals):
            sf_out[0, 0, hh] = s_f
            sb_out[0, 0, hh] = s_b


def _retention(q, k, v, g, decay_rows, s_f0, s_b0, emit_state):
    bsz, length, _ = q.shape
    n_chunks = length // CHUNK
    has_init = s_f0 is not None
    heads = RET_HEADS if n_chunks * RET_HEADS <= RET_UNROLL else 1
    head_spec = pl.BlockSpec((1, length, heads * RET_DK), lambda h, b: (b, 0, h))
    st_spec = pl.BlockSpec((1, 1, heads, RET_DK, RET_DV), lambda h, b: (b, 0, h, 0, 0))
    in_specs = [pl.BlockSpec(decay_rows.shape, lambda h, b: (0, 0))] + [head_spec] * 4
    args = [decay_rows, q, k, v, g]
    if has_init:
        in_specs += [st_spec, st_spec]
        args += [s_f0, s_b0]
    out_specs = [head_spec]
    out_shape = [jax.ShapeDtypeStruct((bsz, length, RET_W), BF16)]
    if emit_state:
        st_shape = jax.ShapeDtypeStruct((bsz, 1, RET_HEADS, RET_DK, RET_DV), F32)
        out_specs += [st_spec, st_spec]
        out_shape += [st_shape, st_shape]
    return pl.pallas_call(
        functools.partial(_ret_kernel, n_chunks, heads, has_init, emit_state),
        grid=(RET_HEADS // heads, bsz),
        in_specs=in_specs,
        out_specs=out_specs,
        out_shape=out_shape,
        scratch_shapes=[
            pltpu.VMEM((heads, n_chunks, RET_DK, RET_DV), F32),
            pltpu.VMEM((heads, n_chunks, RET_DK, RET_DV), F32),
            pltpu.VMEM((heads, 5, CHUNK, CHUNK), F32),
        ],
        compiler_params=pltpu.CompilerParams(
            dimension_semantics=("arbitrary", "arbitrary"), vmem_limit_bytes=VMEM_LIMIT),
        name="retention_init" if has_init else "retention_zero",
    )(*args)


def _route(logits):
    lane = lax.broadcasted_iota(jnp.int32, logits.shape, 1)
    lane_f = lane.astype(F32)
    neg = -jnp.inf
    far = float(LANES)
    is_g = lane < N_GROUPS
    lg = jnp.where(is_g, logits, neg)
    g_max = jnp.max(lg, axis=1, keepdims=True)
    g_idx = jnp.min(jnp.where(lg == g_max, lane_f, far), axis=1, keepdims=True)
    p_sel = 1.0 / jnp.sum(jnp.where(is_g, jnp.exp(lg - g_max), 0.0), axis=1, keepdims=True)
    lane_group = ((lane - N_GROUPS) >> 3).astype(F32)
    sel = (lane >= N_GROUPS) & (lane < N_GROUPS + N_EXPERTS) & (lane_group == g_idx)
    le = jnp.where(sel, logits, neg)
    v1 = jnp.max(le, axis=1, keepdims=True)
    i1 = jnp.min(jnp.where(le == v1, lane_f, far), axis=1, keepdims=True)
    le2 = jnp.where(lane_f == i1, neg, le)
    v2 = jnp.max(le2, axis=1, keepdims=True)
    i2 = jnp.min(jnp.where(le2 == v2, lane_f, far), axis=1, keepdims=True)
    e2 = jnp.exp(v2 - v1)
    w1 = p_sel * (1.0 / (1.0 + e2))
    w2 = p_sel * (e2 / (1.0 + e2))
    return lane, lane_f, g_idx, i1, i2, w1, w2


def _outproj_kernel(yc_ref, yr_ref, x_ref, g1_ref, sh_ref, sc_ref, ng_ref, wo_ref, wr_ref, br_ref,
                    x1_ref, xloc_ref, route_ref, cnt_ref):
    m = (jnp.dot(yc_ref[...], wo_ref[0:CONV_W, :], preferred_element_type=F32)
         + jnp.dot(yr_ref[...], wo_ref[CONV_W:, :], preferred_element_type=F32))
    x1 = x_ref[...] + g1_ref[0, 0] * m
    x1_ref[...] = x1
    xn = (_rms(x1) * ng_ref[...]) * (1.0 + sc_ref[0, 0]) + sh_ref[0, 0]
    xb = xn.astype(BF16)
    logits = jnp.dot(xb, wr_ref[...], preferred_element_type=F32) + br_ref[...]
    lane, lane_f, g_idx, i1, i2, w1, w2 = _route(logits)

    picks = jnp.where(lane_f == g_idx, 1.0, 0.0)
    rows = picks.shape[0]
    tri = (lax.broadcasted_iota(jnp.int32, (rows, rows), 0)
           > lax.broadcasted_iota(jnp.int32, (rows, rows), 1))
    before = jnp.dot(jnp.where(tri, 1.0, 0.0).astype(BF16), picks.astype(BF16),
                     preferred_element_type=F32)
    count = jnp.sum(picks, axis=0, keepdims=True)
    cnt_ref[0] = count
    count8 = jnp.broadcast_to(jnp.floor((count + (SUBLANES - 1)) * (1.0 / SUBLANES)) * SUBLANES,
                              (SUBLANES, LANES))
    lane8 = lane[:SUBLANES]
    start = sum(jnp.where(lane8 >= k, pltpu.roll(count8, k, 1), 0.0) for k in range(1, N_GROUPS))
    local = jnp.sum(jnp.where(lane_f == g_idx, before + start[0:1], 0.0), axis=1, keepdims=True)
    route = jnp.where(lane == ROUTE_GROUP, g_idx, jnp.where(lane == ROUTE_LOCAL, local, jnp.where(
        lane == ROUTE_E1, i1 - N_GROUPS, jnp.where(lane == ROUTE_E2, i2 - N_GROUPS, jnp.where(
            lane == ROUTE_W1, w1, jnp.where(lane == ROUTE_W2, w2, 0.0))))))
    route_ref[...] = route

    local_row = jnp.transpose(jnp.broadcast_to(local, (rows, LANES)))[0:1, :]
    place = jnp.where(lax.broadcasted_iota(jnp.int32, (LOCAL_ROWS, rows), 0).astype(F32) == local_row,
                      1.0, 0.0).astype(BF16)
    xloc_ref[0, :, :D_MODEL] = jnp.dot(place, xb, preferred_element_type=F32)
    xloc_ref[0, :, D_MODEL:] = sum(
        jnp.dot(place, piece, preferred_element_type=F32) for piece in _split3(route))


def _outproj(y_conv, y_ret, x, mod4, mod_row_of_tile, norm_g, w_out_bf, w_router_bf, b_router):
    tokens = x.shape[0]
    tiles = tokens // OUTPROJ_TILE

    def mod_spec(which):
        return pl.BlockSpec((1, 1, 1, D_MODEL), lambda i: (mod_row_of_tile(i), which, 0, 0))

    tok = lambda w: pl.BlockSpec((OUTPROJ_TILE, w), lambda i: (i, 0))
    full = lambda a: pl.BlockSpec(a.shape, lambda i: (0,) * a.ndim)
    return pl.pallas_call(
        _outproj_kernel,
        grid=(tiles,),
        in_specs=[tok(CONV_W), tok(RET_W), tok(D_MODEL), mod_spec(2), mod_spec(3), mod_spec(4),
                  full(norm_g), full(w_out_bf), full(w_router_bf), full(b_router)],
        out_specs=[tok(D_MODEL),
                   pl.BlockSpec((1, LOCAL_ROWS, ROW_W), lambda i: (i, 0, 0)),
                   tok(ROUTER_COLS),
                   pl.BlockSpec((1, 1, ROUTER_COLS), lambda i: (i, 0, 0))],
        out_shape=[jax.ShapeDtypeStruct((tokens, D_MODEL), F32),
                   jax.ShapeDtypeStruct((tiles, LOCAL_ROWS, ROW_W), F32),
                   jax.ShapeDtypeStruct((tokens, ROUTER_COLS), F32),
                   jax.ShapeDtypeStruct((tiles, 1, ROUTER_COLS), F32)],
        compiler_params=pltpu.CompilerParams(
            dimension_semantics=("parallel",), vmem_limit_bytes=VMEM_LIMIT),
        name="outproj",
    )(y_conv, y_ret, x, mod4, mod4, mod4, norm_g, w_out_bf, w_router_bf, b_router)


RUN_PIECES = tuple(SUBLANES << b for b in reversed(range((OUTPROJ_TILE // SUBLANES).bit_length())))


def _expert_kernel(ctx_tiles, tile_group_ref, n_used_ref, first_ref, last_ref, run_len_ref, run_src_ref, run_dst_ref,
                   xloc_ctx_hbm, xloc_lat_hbm, w1_ref, w3_ref, w2_ref, ys_ref, xbuf, xb, gate_tabs, sem):
    j = pl.program_id(0)
    step = pl.program_id(1)
    n_used = n_used_ref[0]

    def tile_copies(tile, act):
        slot = tile % 2
        group = tile_group_ref[tile]
        row0 = tile * GROUP_TILE

        def from_token_tile(b, carry):
            run = b * N_GROUPS + group
            lo = jnp.maximum(run_dst_ref[run], row0)
            hi = jnp.minimum(run_dst_ref[run] + run_len_ref[run], row0 + GROUP_TILE)
            n = jnp.maximum(hi - lo, 0)
            src = run_src_ref[run] + lo - run_dst_ref[run]
            dst = lo - row0
            for size in RUN_PIECES:
                done = n & (-2 * size)
                take = (n & size) != 0
                into = xbuf.at[slot, pl.ds(pl.multiple_of(dst + done, SUBLANES), size)]
                rows = pl.ds(pl.multiple_of(src + done, SUBLANES), size)

                @pl.when(take & (b < ctx_tiles))
                def _():
                    act(pltpu.make_async_copy(xloc_ctx_hbm.at[b, rows], into, sem.at[slot]))

                @pl.when(take & (b >= ctx_tiles))
                def _():
                    act(pltpu.make_async_copy(xloc_lat_hbm.at[b - ctx_tiles, rows], into, sem.at[slot]))
            return carry

        lax.fori_loop(first_ref[tile], last_ref[tile] + 1, from_token_tile, 0)

    start = lambda cp: cp.start()
    wait = lambda cp: cp.wait()

    @pl.when(j < n_used)
    def _():
        @pl.when(step == 0)
        def _():
            @pl.when(j == 0)
            def _():
                xbuf[...] = jnp.zeros_like(xbuf)
                tile_copies(j, start)

            tile_copies(j, wait)
            rows_in = xbuf[j % 2]
            xb[...] = rows_in[:, :D_MODEL].astype(BF16)
            route = rows_in[:, D_MODEL:]
            lane = lax.broadcasted_iota(jnp.int32, route.shape, 1)
            for n, which in enumerate((ROUTE_E1, ROUTE_E2, ROUTE_W1, ROUTE_W2)):
                col = jnp.sum(jnp.where(lane == which, route, 0.0), axis=1, keepdims=True)
                gate_tabs[n] = jnp.broadcast_to(col, route.shape)

            @pl.when(j + 1 < n_used)
            def _():
                tile_copies(j + 1, start)

        x = xb[...]
        total = None
        for s in range(EXPERTS_PER_STEP):
            expert = (tile_group_ref[j] * EXPERTS_PER_GROUP + step * EXPERTS_PER_STEP + s).astype(F32)
            gate = (jnp.where(gate_tabs[0] == expert, gate_tabs[2], 0.0)
                    + jnp.where(gate_tabs[1] == expert, gate_tabs[3], 0.0))
            hid = _silu(jnp.dot(x, w1_ref[0, s].astype(BF16), preferred_element_type=F32)) * jnp.dot(
                x, w3_ref[0, s].astype(BF16), preferred_element_type=F32)
            y = jnp.dot(hid.astype(BF16), w2_ref[0, s].astype(BF16), preferred_element_type=F32)
            gated = jnp.concatenate(
                [gate * y[:, c * LANES:(c + 1) * LANES] for c in range(D_MODEL // LANES)], axis=1)
            total = gated if total is None else total + gated

        @pl.when(step == 0)
        def _():
            ys_ref[...] = total

        @pl.when(step > 0)
        def _():
            ys_ref[...] += total


def _experts(xloc_ctx, xloc_lat, tile_tables, runs, slots, w1, w3, w2):
    steps = EXPERTS_PER_GROUP // EXPERTS_PER_STEP
    paired = lambda w: w.reshape((N_EXPERTS // EXPERTS_PER_STEP, EXPERTS_PER_STEP) + w.shape[1:])
    w_spec = lambda shape: pl.BlockSpec((1, EXPERTS_PER_STEP) + shape, lambda j, s, tg, nu, *_: (
        tg[jnp.minimum(j, nu[0] - 1)] * steps + jnp.where(j < nu[0], s, steps - 1), 0, 0, 0))
    grid_spec = pltpu.PrefetchScalarGridSpec(
        num_scalar_prefetch=7,
        grid=(slots // GROUP_TILE, steps),
        in_specs=[
            pl.BlockSpec(memory_space=pl.ANY), pl.BlockSpec(memory_space=pl.ANY),
            w_spec((D_MODEL, D_EXPERT)), w_spec((D_MODEL, D_EXPERT)), w_spec((D_EXPERT, D_MODEL)),
        ],
        out_specs=pl.BlockSpec((GROUP_TILE, D_MODEL), lambda j, s, tg, nu, *_: (jnp.minimum(j, nu[0] - 1), 0)),
        scratch_shapes=[pltpu.VMEM((2, GROUP_TILE, ROW_W), F32), pltpu.VMEM((GROUP_TILE, D_MODEL), BF16),
                        pltpu.VMEM((4, GROUP_TILE, LANES), F32), pltpu.SemaphoreType.DMA((2,))],
    )
    return pl.pallas_call(
        functools.partial(_expert_kernel, xloc_ctx.shape[0]),
        grid_spec=grid_spec,
        out_shape=jax.ShapeDtypeStruct((slots, D_MODEL), F32),
        compiler_params=pltpu.CompilerParams(
            dimension_semantics=("arbitrary", "arbitrary"), vmem_limit_bytes=EXPERT_VMEM_LIMIT),
        name="experts",
    )(*tile_tables, *runs, xloc_ctx, xloc_lat, paired(w1), paired(w3), paired(w2))


def _combine_kernel(n_tiles, tile_base, run_len_ref, run_src_ref, run_dst_ref,
                    x1_ref, route_ref, g2_ref, fg_ref, ys_hbm, o_ref, buf, sem):
    i = pl.program_id(0)
    slot = i % 2

    def run_copies(local_tile, act):
        s = local_tile % 2
        tile = local_tile + tile_base
        for g in range(N_GROUPS):
            n = run_len_ref[tile * N_GROUPS + g]
            src = run_src_ref[tile * N_GROUPS + g]
            dst = run_dst_ref[tile * N_GROUPS + g]
            for size in RUN_PIECES:
                done = n & (-2 * size)

                @pl.when((n & size) != 0)
                def _():
                    act(pltpu.make_async_copy(
                        ys_hbm.at[pl.ds(pl.multiple_of(dst + done, SUBLANES), size)],
                        buf.at[s, pl.ds(pl.multiple_of(src + done, SUBLANES), size)], sem.at[s]))

    @pl.when(i == 0)
    def _():
        buf[...] = jnp.zeros_like(buf)
        run_copies(i, lambda cp: cp.start())

    @pl.when(i + 1 < n_tiles)
    def _():
        run_copies(i + 1, lambda cp: cp.start())

    run_copies(i, lambda cp: cp.wait())
    route = route_ref[...]
    lane = lax.broadcasted_iota(jnp.int32, route.shape, 1)
    local = jnp.sum(jnp.where(lane == ROUTE_LOCAL, route, 0.0), axis=1, keepdims=True)
    pick = jnp.where(lax.broadcasted_iota(jnp.int32, (route.shape[0], LOCAL_ROWS), 1).astype(F32) == local,
                     1.0, 0.0).astype(BF16)
    moe = sum(jnp.dot(pick, piece, preferred_element_type=F32) for piece in _split3(buf[slot])[:2])
    y = x1_ref[...] + g2_ref[0, 0] * moe
    o_ref[...] = _rms(y) * fg_ref[...]


def _combine(ys, runs, x1, route, mod4, mod_row_of_tile, final_g, tile_base):
    tokens = x1.shape[0]
    tiles = tokens // OUTPROJ_TILE
    tok = lambda w: pl.BlockSpec((OUTPROJ_TILE, w), lambda i, *_: (i, 0))
    grid_spec = pltpu.PrefetchScalarGridSpec(
        num_scalar_prefetch=3,
        grid=(tiles,),
        in_specs=[
            tok(D_MODEL), tok(ROUTER_COLS),
            pl.BlockSpec((1, 1, 1, D_MODEL), lambda i, *_: (mod_row_of_tile(i), 5, 0, 0)),
            pl.BlockSpec((1, D_MODEL), lambda i, *_: (0, 0)),
            pl.BlockSpec(memory_space=pl.ANY),
        ],
        out_specs=tok(D_MODEL),
        scratch_shapes=[pltpu.VMEM((2, LOCAL_ROWS, D_MODEL), F32), pltpu.SemaphoreType.DMA((2,))],
    )
    return pl.pallas_call(
        functools.partial(_combine_kernel, tiles, tile_base),
        grid_spec=grid_spec,
        out_shape=jax.ShapeDtypeStruct((tokens, D_MODEL), F32),
        compiler_params=pltpu.CompilerParams(
            dimension_semantics=("arbitrary",), vmem_limit_bytes=VMEM_LIMIT),
        name="combine",
    )(*runs, x1, route, mod4, final_g, ys)


def _routing_tables(counts):
    counts = counts.astype(jnp.int32)
    tiles = counts.shape[0]
    run_len = ((counts + SUBLANES - 1) // SUBLANES) * SUBLANES
    run_src = jnp.cumsum(run_len, axis=1) - run_len
    group_rows = jnp.sum(run_len, axis=0)
    padded = ((group_rows + GROUP_TILE - 1) // GROUP_TILE) * GROUP_TILE
    ends = jnp.cumsum(padded)
    offs = ends - padded
    run_dst = offs[None, :] + jnp.cumsum(run_len, axis=0) - run_len

    n_used = ends[-1] // GROUP_TILE
    max_rows = tiles * (OUTPROJ_TILE + N_GROUPS * (SUBLANES - 1))
    max_tiles = -(-max_rows // GROUP_TILE) + N_GROUPS
    tile_ids = jnp.minimum(jnp.arange(max_tiles, dtype=jnp.int32), n_used - 1)
    tile_group = jnp.sum(tile_ids[:, None] * GROUP_TILE >= ends[None, :], axis=1).astype(jnp.int32)
    of_group = (tile_group[:, None] == jnp.arange(N_GROUPS, dtype=jnp.int32))[:, None, :]
    start = jnp.sum(jnp.where(of_group, run_dst[None], 0), axis=-1)
    stop = start + jnp.sum(jnp.where(of_group, run_len[None], 0), axis=-1)
    row0 = (tile_ids * GROUP_TILE)[:, None]
    first = jnp.sum(stop <= row0, axis=1).astype(jnp.int32)
    last = jnp.sum(start < row0 + GROUP_TILE, axis=1).astype(jnp.int32) - 1
    flat = lambda a: a.reshape(-1)
    return ((tile_group, n_used.reshape(1), first, last), (flat(run_len), flat(run_src), flat(run_dst)),
            max_tiles * GROUP_TILE)


def _mixer(x, mod4, mod_row, is_grid, s_f0, s_b0, p):
    norm_mix_g, w_in_bf, conv_w, decay_rows = p
    y_conv, q, k, v, g = _inproj(x, mod4, mod_row, norm_mix_g, w_in_bf, conv_w, is_grid)
    ret = _retention(q, k, v, g, decay_rows, s_f0, s_b0, emit_state=not is_grid)
    flat = lambda a: a.reshape(-1, a.shape[-1])
    return flat(y_conv), flat(ret[0]), ret[1:]


def kernel(x_prompt, x_sample, state_ret_fwd, state_ret_bwd, c, c_ctx, norm_mix_g, norm_ffn_g, w_ada, b_ada, w_in, conv_w, ret_decay_fwd, ret_decay_bwd, w_out, w_router_group, b_router_group, w_router_expert, b_router_expert, w_gate_e, w_up_e, w_down_e, final_norm_g):
    assert norm_mix_g.shape[0] == 1, "single-layer backbone"
    n_lat = c.shape[0]
    ctx_row = n_lat
    mod_rows = 8
    cvec = jnp.concatenate([c, c_ctx[None, :], jnp.zeros((mod_rows - n_lat - 1, D_MODEL), F32)], axis=0)
    mod = _modulation(cvec, w_ada[0], b_ada[0][None, :])
    mod4 = mod.reshape(mod_rows, 6, 1, D_MODEL)

    pad = ROUTER_COLS - N_GROUPS - N_EXPERTS
    w_router = jnp.concatenate(
        [w_router_group[0], w_router_expert[0], jnp.zeros((D_MODEL, pad), F32)], axis=1).astype(BF16)
    b_router = jnp.concatenate([b_router_group[0], b_router_expert[0], jnp.zeros((pad,), F32)])[None, :]
    decay_rows = jnp.broadcast_to(
        jnp.concatenate([ret_decay_fwd[0], ret_decay_bwd[0]])[:, None], (2 * RET_HEADS, LANES))
    p_mix = (norm_mix_g, w_in[0].astype(BF16), conv_w[0], decay_rows)
    w_out_bf = w_out[0].astype(BF16)
    final_g = final_norm_g[None, :]

    ctx_mod = lambda i: ctx_row
    lat_mod = lambda i: i // (x_sample.shape[1] // OUTPROJ_TILE)
    flat = lambda a: a.reshape(-1, a.shape[-1])

    yc_c, yr_c, (s_f, s_b) = _mixer(x_prompt, mod4, lambda b: ctx_row, False, None, None, p_mix)
    yc_l, yr_l, _ = _mixer(x_sample, mod4, lambda b: b, True, state_ret_fwd, state_ret_bwd, p_mix)

    x1_c, xloc_c, route_c, cnt_c = _outproj(
        yc_c, yr_c, flat(x_prompt), mod4, ctx_mod, norm_ffn_g, w_out_bf, w_router, b_router)
    x1_l, xloc_l, route_l, cnt_l = _outproj(
        yc_l, yr_l, flat(x_sample), mod4, lat_mod, norm_ffn_g, w_out_bf, w_router, b_router)

    counts = jnp.concatenate([cnt_c, cnt_l], axis=0)[:, 0, :N_GROUPS]
    tile_tables, runs, slots = _routing_tables(counts)
    ys = _experts(xloc_c, xloc_l, tile_tables, runs, slots, w_gate_e[0], w_up_e[0], w_down_e[0])
    y_prompt = _combine(ys, runs, x1_c, route_c, mod4, ctx_mod, final_g, 0)
    y_sample = _combine(ys, runs, x1_l, route_l, mod4, lat_mod, final_g, cnt_c.shape[0])
    return (y_prompt.reshape(x_prompt.shape), y_sample.reshape(x_sample.shape),
            s_f.astype(x_prompt.dtype), s_b.astype(x_prompt.dtype))
```

```python
import functools

import jax
import jax.numpy as jnp
from jax import lax
from jax.experimental import pallas as pl
from jax.experimental.pallas import tpu as pltpu

F32 = jnp.float32
BF16 = jnp.bfloat16

D_MODEL = 1024
GRID_W = 64
CONV_W = 512
RET_HEADS = 4
RET_DK = 128
RET_DV = 128
RET_W = RET_HEADS * RET_DV
QK_W = RET_HEADS * RET_DK
CHUNK = 128
N_GROUPS = 4
EXPERTS_PER_GROUP = 8
N_EXPERTS = N_GROUPS * EXPERTS_PER_GROUP
D_EXPERT = 256
ROPE_BASE = 10000.0
EPS = 1e-6

LANES = 128
TOKEN_TILE = 256
OUTPROJ_TILE = 512
GROUP_TILE = 1024
EXPERTS_PER_STEP = 2
RET_UNROLL = 8
SUBLANES = 8
LOCAL_ROWS = OUTPROJ_TILE + LANES
ROW_W = D_MODEL + LANES
ROUTE_GROUP, ROUTE_LOCAL, ROUTE_E1, ROUTE_E2, ROUTE_W1, ROUTE_W2 = range(6)
assert LOCAL_ROWS >= OUTPROJ_TILE + N_GROUPS * (SUBLANES - 1)
MOD_COLS = 1536
ROUTER_COLS = LANES
VMEM_LIMIT = 48 * 1024 * 1024
EXPERT_VMEM_LIMIT = 56 * 1024 * 1024


def _silu(x):
    return x * jax.nn.sigmoid(x)


def _rms(x):
    return x * lax.rsqrt(jnp.mean(x * x, axis=-1, keepdims=True) + EPS)


def _bdot(a, b):
    return jnp.dot(a.astype(BF16), b.astype(BF16), preferred_element_type=F32)


def _split3(x):
    hi = x.astype(BF16)
    rest = x - hi.astype(F32)
    mid = rest.astype(BF16)
    return hi, mid, (rest - mid.astype(F32)).astype(BF16)


def _mod_kernel(c_ref, w_ref, b_ref, o_ref):
    o_ref[...] = _bdot(_silu(c_ref[...]), w_ref[...]) + b_ref[...]


def _modulation(cvec, w_ada, b_ada):
    rows = cvec.shape[0]
    n = w_ada.shape[1]
    return pl.pallas_call(
        _mod_kernel,
        grid=(n // MOD_COLS,),
        in_specs=[
            pl.BlockSpec((rows, D_MODEL), lambda j: (0, 0)),
            pl.BlockSpec((D_MODEL, MOD_COLS), lambda j: (0, j)),
            pl.BlockSpec((1, MOD_COLS), lambda j: (0, j)),
        ],
        out_specs=pl.BlockSpec((rows, MOD_COLS), lambda j: (0, j)),
        out_shape=jax.ShapeDtypeStruct((rows, n), F32),
        compiler_params=pltpu.CompilerParams(vmem_limit_bytes=VMEM_LIMIT),
        name="modulation",
    )(cvec, w_ada, b_ada)


def _inproj_kernel(seg, is_grid, x_ref, sh_ref, sc_ref, ng_ref, w_ref, cw_ref, *rest):
    if is_grid:
        cos_ref, sa_ref, sb_ref, yc_ref, q_ref, k_ref, v_ref, g_ref = rest
    else:
        yc_ref, q_ref, k_ref, v_ref, g_ref = rest
    x = x_ref[0]
    xn = (_rms(x) * ng_ref[...]) * (1.0 + sc_ref[0, 0]) + sh_ref[0, 0]
    xb = xn.astype(BF16)

    def proj(c0, n):
        return jnp.dot(xb, w_ref[:, c0:c0 + n], preferred_element_type=F32)

    gate_b = proj(0, CONV_W)
    u = proj(CONV_W, CONV_W) * proj(2 * CONV_W, CONV_W)
    rows = u.shape[0]
    pos = lax.broadcasted_iota(jnp.int32, u.shape, 0) & (seg - 1)
    u_prev = jnp.where(pos != 0, pltpu.roll(u, 1, 0), 0.0)
    u_next = jnp.where(pos != seg - 1, pltpu.roll(u, rows - 1, 0), 0.0)
    conv = cw_ref[0:1, :] * u_prev + cw_ref[1:2, :] * u + cw_ref[2:3, :] * u_next
    yc_ref[0] = (gate_b * conv).astype(yc_ref.dtype)

    q0 = 3 * CONV_W
    q = proj(q0, QK_W)
    k = proj(q0 + QK_W, QK_W)
    if is_grid:
        cos, sa, sb = cos_ref[...], sa_ref[...], sb_ref[...]

        def rope(t):
            out = []
            for h in range(RET_HEADS):
                th = t[:, h * RET_DK:(h + 1) * RET_DK]
                out.append(th * cos + pltpu.roll(th, RET_DK - 1, 1) * sa + pltpu.roll(th, 1, 1) * sb)
            return jnp.concatenate(out, axis=1)

        q, k = rope(q), rope(k)
    q_ref[0] = q
    k_ref[0] = k
    v_ref[0] = proj(q0 + 2 * QK_W, RET_W)
    g_ref[0] = proj(q0 + 2 * QK_W + RET_W, RET_W)


def _rope_tables(length):
    pos = jnp.arange(length)
    row = (pos // GRID_W).astype(F32)
    col = (pos % GRID_W).astype(F32)
    n_pairs = RET_DK // 4
    freqs = ROPE_BASE ** (-(jnp.arange(n_pairs, dtype=F32) * 2.0 / (RET_DK // 2)))
    ang = jnp.concatenate([row[:, None] * freqs, col[:, None] * freqs], axis=-1)
    cos = jnp.repeat(jnp.cos(ang), 2, axis=-1)
    sin = jnp.repeat(jnp.sin(ang), 2, axis=-1)
    even = (jnp.arange(RET_DK) % 2) == 0
    return cos, jnp.where(even, -sin, 0.0), jnp.where(even, 0.0, sin)


def _inproj(x, mod4, mod_row, norm_g, w_in_bf, conv_w, is_grid):
    bsz, length, _ = x.shape
    seg = GRID_W if is_grid else length
    assert TOKEN_TILE % seg == 0 and length % TOKEN_TILE == 0
    tiles = length // TOKEN_TILE

    def mod_spec(which):
        return pl.BlockSpec((1, 1, 1, D_MODEL), lambda b, i: (mod_row(b), which, 0, 0))

    def tok_spec(width):
        return pl.BlockSpec((1, TOKEN_TILE, width), lambda b, i: (b, i, 0))

    in_specs = [
        tok_spec(D_MODEL), mod_spec(0), mod_spec(1),
        pl.BlockSpec((1, D_MODEL), lambda b, i: (0, 0)),
        pl.BlockSpec(w_in_bf.shape, lambda b, i: (0, 0)),
        pl.BlockSpec(conv_w.shape, lambda b, i: (0, 0)),
    ]
    args = [x, mod4, mod4, norm_g, w_in_bf, conv_w]
    if is_grid:
        in_specs += [pl.BlockSpec((TOKEN_TILE, RET_DK), lambda b, i: (i, 0))] * 3
        args += list(_rope_tables(length))
    shp = lambda w, dt: jax.ShapeDtypeStruct((bsz, length, w), dt)
    return pl.pallas_call(
        functools.partial(_inproj_kernel, seg, is_grid),
        grid=(bsz, tiles),
        in_specs=in_specs,
        out_specs=[tok_spec(CONV_W), tok_spec(QK_W), tok_spec(QK_W), tok_spec(RET_W), tok_spec(RET_W)],
        out_shape=[shp(CONV_W, BF16), shp(QK_W, F32), shp(QK_W, F32), shp(RET_W, F32), shp(RET_W, F32)],
        compiler_params=pltpu.CompilerParams(
            dimension_semantics=("parallel", "parallel"), vmem_limit_bytes=VMEM_LIMIT),
        name="inproj_grid" if is_grid else "inproj_seq",
    )(*args)


def _ret_kernel(n_chunks, heads, has_init, emit_state, a_ref, q_ref, k_ref, v_ref, g_ref, *rest):
    rest = list(rest)
    if has_init:
        sf0_ref, sb0_ref = rest[:2]
        rest = rest[2:]
    y_ref = rest.pop(0)
    if emit_state:
        sf_out, sb_out = rest[:2]
        rest = rest[2:]
    st_f, st_b, dec = rest
    c = CHUNK
    sq = (c, c)
    head0 = pl.program_id(0) * heads

    def log_decays(hh):
        lg_f = jnp.log1p(-jnp.exp(a_ref[pl.ds(head0 + hh, 1), :]))
        lg_b = jnp.log1p(-jnp.exp(a_ref[pl.ds(head0 + hh + RET_HEADS, 1), :]))
        return lg_f, lg_b

    @pl.when(pl.program_id(1) == 0)
    def _():
        row = lax.broadcasted_iota(jnp.int32, sq, 0).astype(F32)
        col = lax.broadcasted_iota(jnp.int32, sq, 1).astype(F32)
        scale = RET_DK ** -0.5
        for hh in range(heads):
            lg_f, lg_b = log_decays(hh)
            dec[hh, 0] = scale * (
                jnp.where(row >= col, jnp.exp(jnp.where(row >= col, row - col, 0.0) * lg_f), 0.0)
                + jnp.where(col >= row, jnp.exp(jnp.where(col >= row, col - row, 0.0) * lg_b), 0.0))
            dec[hh, 1] = jnp.exp((row + 1.0) * lg_f)
            dec[hh, 2] = jnp.exp((c - row) * lg_b)
            dec[hh, 3] = scale * jnp.exp((c - 1.0 - col) * lg_f)
            dec[hh, 4] = scale * jnp.exp(col * lg_b)

    def rows(n):
        return pl.ds(pl.multiple_of(n * c, c), c) if not isinstance(n, int) else pl.ds(n * c, c)

    def cols(hh):
        return slice(hh * RET_DK, (hh + 1) * RET_DK)

    def kv_step(hh, n):
        kt = jnp.transpose(k_ref[0, rows(n), cols(hh)])
        lhs = jnp.concatenate([kt * dec[hh, 3], kt * dec[hh, 4]], axis=0)
        kv = _bdot(lhs, v_ref[0, rows(n), cols(hh)])
        st_f[hh, n] = kv[:RET_DK]
        st_b[hh, n] = kv[RET_DK:]

    def scan(hh, st, decay, order, s):
        def step(i, s):
            n = order(i)
            kv = st[hh, n]
            st[hh, n] = s
            return s * decay + kv
        if n_chunks <= RET_UNROLL:
            for i in range(n_chunks):
                s = step(i, s)
            return s
        return lax.fori_loop(0, n_chunks, step, s, unroll=RET_UNROLL)

    def out_step(hh, n):
        q = q_ref[0, rows(n), cols(hh)]
        scores = lax.dot_general(q.astype(BF16), k_ref[0, rows(n), cols(hh)].astype(BF16),
                                 (((1,), (1,)), ((), ())), preferred_element_type=F32)
        o = _bdot(scores * dec[hh, 0], v_ref[0, rows(n), cols(hh)])
        q_dec = jnp.concatenate([q * dec[hh, 1], q * dec[hh, 2]], axis=1)
        o = o + _bdot(q_dec, jnp.concatenate([st_f[hh, n], st_b[hh, n]], axis=0))
        y = _silu(g_ref[0, rows(n), cols(hh)]) * _rms(o)
        y_ref[0, rows(n), cols(hh)] = y.astype(y_ref.dtype)

    def over_chunks(step):
        if n_chunks * heads <= RET_UNROLL:
            for hh in range(heads):
                for n in range(n_chunks):
                    step(hh, n)
        else:
            for hh in range(heads):
                lax.fori_loop(0, n_chunks, lambda n, carry: (step(hh, n), carry)[1], 0, unroll=RET_UNROLL)

    over_chunks(kv_step)
    finals = []
    for hh in range(heads):
        lg_f, lg_b = log_decays(hh)
        s_f = sf0_ref[0, 0, hh] if has_init else jnp.zeros(sq, F32)
        s_b = sb0_ref[0, 0, hh] if has_init else jnp.zeros(sq, F32)
        s_f = scan(hh, st_f, jnp.exp(c * lg_f), lambda i: i, s_f)
        s_b = scan(hh, st_b, jnp.exp(c * lg_b), lambda i: n_chunks - 1 - i, s_b)
        finals.append((s_f, s_b))
    over_chunks(out_step)
    if emit_state:
        for hh, (s_f, s_b) in enumerate(finals):
            sf_out[0, 0, hh] = s_f
            sb_out[0, 0, hh] = s_b


def _retention(q, k, v, g, decay_rows, s_f0, s_b0, emit_state):
    bsz, length, _ = q.shape
    n_chunks = length // CHUNK
    has_init = s_f0 is not None
    heads = RET_HEADS if n_chunks * RET_HEADS <= RET_UNROLL else 1
    head_spec = pl.BlockSpec((1, length, heads * RET_DK), lambda h, b: (b, 0, h))
    st_spec = pl.BlockSpec((1, 1, heads, RET_DK, RET_DV), lambda h, b: (b, 0, h, 0, 0))
    in_specs = [pl.BlockSpec(decay_rows.shape, lambda h, b: (0, 0))] + [head_spec] * 4
    args = [decay_rows, q, k, v, g]
    if has_init:
        in_specs += [st_spec, st_spec]
        args += [s_f0, s_b0]
    out_specs = [head_spec]
    out_shape = [jax.ShapeDtypeStruct((bsz, length, RET_W), BF16)]
    if emit_state:
        st_shape = jax.ShapeDtypeStruct((bsz, 1, RET_HEADS, RET_DK, RET_DV), F32)
        out_specs += [st_spec, st_spec]
        out_shape += [st_shape, st_shape]
    return pl.pallas_call(
        functools.partial(_ret_kernel, n_chunks, heads, has_init, emit_state),
        grid=(RET_HEADS // heads, bsz),
        in_specs=in_specs,
        out_specs=out_specs,
        out_shape=out_shape,
        scratch_shapes=[
            pltpu.VMEM((heads, n_chunks, RET_DK, RET_DV), F32),
            pltpu.VMEM((heads, n_chunks, RET_DK, RET_DV), F32),
            pltpu.VMEM((heads, 5, CHUNK, CHUNK), F32),
        ],
        compiler_params=pltpu.CompilerParams(
            dimension_semantics=("arbitrary", "arbitrary"), vmem_limit_bytes=VMEM_LIMIT),
        name="retention_init" if has_init else "retention_zero",
    )(*args)


def _route(logits):
    lane = lax.broadcasted_iota(jnp.int32, logits.shape, 1)
    lane_f = lane.astype(F32)
    neg = -jnp.inf
    far = float(LANES)
    is_g = lane < N_GROUPS
    lg = jnp.where(is_g, logits, neg)
    g_max = jnp.max(lg, axis=1, keepdims=True)
    g_idx = jnp.min(jnp.where(lg == g_max, lane_f, far), axis=1, keepdims=True)
    p_sel = 1.0 / jnp.sum(jnp.where(is_g, jnp.exp(lg - g_max), 0.0), axis=1, keepdims=True)
    lane_group = ((lane - N_GROUPS) >> 3).astype(F32)
    sel = (lane >= N_GROUPS) & (lane < N_GROUPS + N_EXPERTS) & (lane_group == g_idx)
    le = jnp.where(sel, logits, neg)
    v1 = jnp.max(le, axis=1, keepdims=True)
    i1 = jnp.min(jnp.where(le == v1, lane_f, far), axis=1, keepdims=True)
    le2 = jnp.where(lane_f == i1, neg, le)
    v2 = jnp.max(le2, axis=1, keepdims=True)
    i2 = jnp.min(jnp.where(le2 == v2, lane_f, far), axis=1, keepdims=True)
    e2 = jnp.exp(v2 - v1)
    w1 = p_sel * (1.0 / (1.0 + e2))
    w2 = p_sel * (e2 / (1.0 + e2))
    return lane, lane_f, g_idx, i1, i2, w1, w2


def _outproj_kernel(yc_ref, yr_ref, x_ref, g1_ref, sh_ref, sc_ref, ng_ref, wo_ref, wr_ref, br_ref,
                    x1_ref, xloc_ref, route_ref, cnt_ref):
    m = (jnp.dot(yc_ref[...], wo_ref[0:CONV_W, :], preferred_element_type=F32)
         + jnp.dot(yr_ref[...], wo_ref[CONV_W:, :], preferred_element_type=F32))
    x1 = x_ref[...] + g1_ref[0, 0] * m
    x1_ref[...] = x1
    xn = (_rms(x1) * ng_ref[...]) * (1.0 + sc_ref[0, 0]) + sh_ref[0, 0]
    xb = xn.astype(BF16)
    logits = jnp.dot(xb, wr_ref[...], preferred_element_type=F32) + br_ref[...]
    lane, lane_f, g_idx, i1, i2, w1, w2 = _route(logits)

    picks = jnp.where(lane_f == g_idx, 1.0, 0.0)
    rows = picks.shape[0]
    tri = (lax.broadcasted_iota(jnp.int32, (rows, rows), 0)
           > lax.broadcasted_iota(jnp.int32, (rows, rows), 1))
    before = jnp.dot(jnp.where(tri, 1.0, 0.0).astype(BF16), picks.astype(BF16),
                     preferred_element_type=F32)
    count = jnp.sum(picks, axis=0, keepdims=True)
    cnt_ref[0] = count
    count8 = jnp.broadcast_to(jnp.floor((count + (SUBLANES - 1)) * (1.0 / SUBLANES)) * SUBLANES,
                              (SUBLANES, LANES))
    lane8 = lane[:SUBLANES]
    start = sum(jnp.where(lane8 >= k, pltpu.roll(count8, k, 1), 0.0) for k in range(1, N_GROUPS))
    local = jnp.sum(jnp.where(lane_f == g_idx, before + start[0:1], 0.0), axis=1, keepdims=True)
    route = jnp.where(lane == ROUTE_GROUP, g_idx, jnp.where(lane == ROUTE_LOCAL, local, jnp.where(
        lane == ROUTE_E1, i1 - N_GROUPS, jnp.where(lane == ROUTE_E2, i2 - N_GROUPS, jnp.where(
            lane == ROUTE_W1, w1, jnp.where(lane == ROUTE_W2, w2, 0.0))))))
    route_ref[...] = route

    local_row = jnp.transpose(jnp.broadcast_to(local, (rows, LANES)))[0:1, :]
    place = jnp.where(lax.broadcasted_iota(jnp.int32, (LOCAL_ROWS, rows), 0).astype(F32) == local_row,
                      1.0, 0.0).astype(BF16)
    xloc_ref[0, :, :D_MODEL] = jnp.dot(place, xb, preferred_element_type=F32)
    xloc_ref[0, :, D_MODEL:] = sum(
        jnp.dot(place, piece, preferred_element_type=F32) for piece in _split3(route))


def _outproj(y_conv, y_ret, x, mod4, mod_row_of_tile, norm_g, w_out_bf, w_router_bf, b_router):
    tokens = x.shape[0]
    tiles = tokens // OUTPROJ_TILE

    def mod_spec(which):
        return pl.BlockSpec((1, 1, 1, D_MODEL), lambda i: (mod_row_of_tile(i), which, 0, 0))

    tok = lambda w: pl.BlockSpec((OUTPROJ_TILE, w), lambda i: (i, 0))
    full = lambda a: pl.BlockSpec(a.shape, lambda i: (0,) * a.ndim)
    return pl.pallas_call(
        _outproj_kernel,
        grid=(tiles,),
        in_specs=[tok(CONV_W), tok(RET_W), tok(D_MODEL), mod_spec(2), mod_spec(3), mod_spec(4),
                  full(norm_g), full(w_out_bf), full(w_router_bf), full(b_router)],
        out_specs=[tok(D_MODEL),
                   pl.BlockSpec((1, LOCAL_ROWS, ROW_W), lambda i: (i, 0, 0)),
                   tok(ROUTER_COLS),
                   pl.BlockSpec((1, 1, ROUTER_COLS), lambda i: (i, 0, 0))],
        out_shape=[jax.ShapeDtypeStruct((tokens, D_MODEL), F32),
                   jax.ShapeDtypeStruct((tiles, LOCAL_ROWS, ROW_W), F32),
                   jax.ShapeDtypeStruct((tokens, ROUTER_COLS), F32),
                   jax.ShapeDtypeStruct((tiles, 1, ROUTER_COLS), F32)],
        compiler_params=pltpu.CompilerParams(
            dimension_semantics=("parallel",), vmem_limit_bytes=VMEM_LIMIT),
        name="outproj",
    )(y_conv, y_ret, x, mod4, mod4, mod4, norm_g, w_out_bf, w_router_bf, b_router)


RUN_PIECES = tuple(SUBLANES << b for b in reversed(range((OUTPROJ_TILE // SUBLANES).bit_length())))


def _expert_kernel(ctx_tiles, tile_group_ref, n_used_ref, first_ref, last_ref, fill_ref,
                   run_len_ref, run_src_ref, run_dst_ref,
                   xloc_ctx_hbm, xloc_lat_hbm, w1_ref, w3_ref, w2_ref, ys_ref, xbuf, xb, gate_tabs, sem):
    j = pl.program_id(0)
    step = pl.program_id(1)
    n_used = n_used_ref[0]

    def tile_copies(tile, act):
        slot = tile % 2
        group = tile_group_ref[tile]
        row0 = tile * GROUP_TILE

        def from_token_tile(b, carry):
            run = b * N_GROUPS + group
            lo = jnp.maximum(run_dst_ref[run], row0)
            hi = jnp.minimum(run_dst_ref[run] + run_len_ref[run], row0 + GROUP_TILE)
            n = jnp.maximum(hi - lo, 0)
            src = run_src_ref[run] + lo - run_dst_ref[run]
            dst = lo - row0
            for size in RUN_PIECES:
                done = n & (-2 * size)
                take = (n & size) != 0
                into = xbuf.at[slot, pl.ds(pl.multiple_of(dst + done, SUBLANES), size)]
                rows = pl.ds(pl.multiple_of(src + done, SUBLANES), size)

                @pl.when(take & (b < ctx_tiles))
                def _():
                    act(pltpu.make_async_copy(xloc_ctx_hbm.at[b, rows], into, sem.at[slot]))

                @pl.when(take & (b >= ctx_tiles))
                def _():
                    act(pltpu.make_async_copy(xloc_lat_hbm.at[b - ctx_tiles, rows], into, sem.at[slot]))
            return carry

        lax.fori_loop(first_ref[tile], last_ref[tile] + 1, from_token_tile, 0)

    start = lambda cp: cp.start()
    wait = lambda cp: cp.wait()

    @pl.when(j < n_used)
    def _():
        @pl.when(step == 0)
        def _():
            @pl.when(j == 0)
            def _():
                xbuf[...] = jnp.zeros_like(xbuf)
                tile_copies(j, start)

            tile_copies(j, wait)
            rows_in = xbuf[j % 2]
            xb[...] = rows_in[:, :D_MODEL].astype(BF16)
            route = rows_in[:, D_MODEL:]
            lane = lax.broadcasted_iota(jnp.int32, route.shape, 1)
            for n, which in enumerate((ROUTE_E1, ROUTE_E2, ROUTE_W1, ROUTE_W2)):
                col = jnp.sum(jnp.where(lane == which, route, 0.0), axis=1, keepdims=True)
                gate_tabs[n] = jnp.broadcast_to(col, route.shape)

            @pl.when(j + 1 < n_used)
            def _():
                tile_copies(j + 1, start)

        def evaluate(rows):
            x = xb[:rows]
            total = None
            for s in range(EXPERTS_PER_STEP):
                expert = (tile_group_ref[j] * EXPERTS_PER_GROUP + step * EXPERTS_PER_STEP + s).astype(F32)
                gate = (jnp.where(gate_tabs[0, :rows] == expert, gate_tabs[2, :rows], 0.0)
                        + jnp.where(gate_tabs[1, :rows] == expert, gate_tabs[3, :rows], 0.0))
                hid = _silu(jnp.dot(x, w1_ref[0, s].astype(BF16), preferred_element_type=F32)) * jnp.dot(
                    x, w3_ref[0, s].astype(BF16), preferred_element_type=F32)
                y = jnp.dot(hid.astype(BF16), w2_ref[0, s].astype(BF16), preferred_element_type=F32)
                gated = jnp.concatenate(
                    [gate * y[:, c * LANES:(c + 1) * LANES] for c in range(D_MODEL // LANES)], axis=1)
                total = gated if total is None else total + gated

            @pl.when(step == 0)
            def _():
                ys_ref[:rows] = total

            @pl.when(step > 0)
            def _():
                ys_ref[:rows] += total

        half = GROUP_TILE // 2

        @pl.when(fill_ref[j] > half)
        def _():
            evaluate(GROUP_TILE)

        @pl.when(fill_ref[j] <= half)
        def _():
            evaluate(half)


def _experts(xloc_ctx, xloc_lat, tile_tables, runs, slots, w1, w3, w2):
    steps = EXPERTS_PER_GROUP // EXPERTS_PER_STEP
    paired = lambda w: w.reshape((N_EXPERTS // EXPERTS_PER_STEP, EXPERTS_PER_STEP) + w.shape[1:])
    w_spec = lambda shape: pl.BlockSpec((1, EXPERTS_PER_STEP) + shape, lambda j, s, tg, nu, *_: (
        tg[jnp.minimum(j, nu[0] - 1)] * steps + jnp.where(j < nu[0], s, steps - 1), 0, 0, 0))
    grid_spec = pltpu.PrefetchScalarGridSpec(
        num_scalar_prefetch=8,
        grid=(slots // GROUP_TILE, steps),
        in_specs=[
            pl.BlockSpec(memory_space=pl.ANY), pl.BlockSpec(memory_space=pl.ANY),
            w_spec((D_MODEL, D_EXPERT)), w_spec((D_MODEL, D_EXPERT)), w_spec((D_EXPERT, D_MODEL)),
        ],
        out_specs=pl.BlockSpec((GROUP_TILE, D_MODEL), lambda j, s, tg, nu, *_: (jnp.minimum(j, nu[0] - 1), 0)),
        scratch_shapes=[pltpu.VMEM((2, GROUP_TILE, ROW_W), F32), pltpu.VMEM((GROUP_TILE, D_MODEL), BF16),
                        pltpu.VMEM((4, GROUP_TILE, LANES), F32), pltpu.SemaphoreType.DMA((2,))],
    )
    return pl.pallas_call(
        functools.partial(_expert_kernel, xloc_ctx.shape[0]),
        grid_spec=grid_spec,
        out_shape=jax.ShapeDtypeStruct((slots, D_MODEL), F32),
        compiler_params=pltpu.CompilerParams(
            dimension_semantics=("arbitrary", "arbitrary"), vmem_limit_bytes=EXPERT_VMEM_LIMIT),
        name="experts",
    )(*tile_tables, *runs, xloc_ctx, xloc_lat, paired(w1), paired(w3), paired(w2))


def _combine_kernel(n_tiles, tile_base, run_len_ref, run_src_ref, run_dst_ref,
                    x1_ref, route_ref, g2_ref, fg_ref, ys_hbm, o_ref, buf, sem):
    i = pl.program_id(0)
    slot = i % 2

    def run_copies(local_tile, act):
        s = local_tile % 2
        tile = local_tile + tile_base
        for g in range(N_GROUPS):
            n = run_len_ref[tile * N_GROUPS + g]
            src = run_src_ref[tile * N_GROUPS + g]
            dst = run_dst_ref[tile * N_GROUPS + g]
            for size in RUN_PIECES:
                done = n & (-2 * size)

                @pl.when((n & size) != 0)
                def _():
                    act(pltpu.make_async_copy(
                        ys_hbm.at[pl.ds(pl.multiple_of(dst + done, SUBLANES), size)],
                        buf.at[s, pl.ds(pl.multiple_of(src + done, SUBLANES), size)], sem.at[s]))

    @pl.when(i == 0)
    def _():
        buf[...] = jnp.zeros_like(buf)
        run_copies(i, lambda cp: cp.start())

    @pl.when(i + 1 < n_tiles)
    def _():
        run_copies(i + 1, lambda cp: cp.start())

    run_copies(i, lambda cp: cp.wait())
    route = route_ref[...]
    lane = lax.broadcasted_iota(jnp.int32, route.shape, 1)
    local = jnp.sum(jnp.where(lane == ROUTE_LOCAL, route, 0.0), axis=1, keepdims=True)
    pick = jnp.where(lax.broadcasted_iota(jnp.int32, (route.shape[0], LOCAL_ROWS), 1).astype(F32) == local,
                     1.0, 0.0).astype(BF16)
    moe = sum(jnp.dot(pick, piece, preferred_element_type=F32) for piece in _split3(buf[slot])[:2])
    y = x1_ref[...] + g2_ref[0, 0] * moe
    o_ref[...] = _rms(y) * fg_ref[...]


def _combine(ys, runs, x1, route, mod4, mod_row_of_tile, final_g, tile_base):
    tokens = x1.shape[0]
    tiles = tokens // OUTPROJ_TILE
    tok = lambda w: pl.BlockSpec((OUTPROJ_TILE, w), lambda i, *_: (i, 0))
    grid_spec = pltpu.PrefetchScalarGridSpec(
        num_scalar_prefetch=3,
        grid=(tiles,),
        in_specs=[
            tok(D_MODEL), tok(ROUTER_COLS),
            pl.BlockSpec((1, 1, 1, D_MODEL), lambda i, *_: (mod_row_of_tile(i), 5, 0, 0)),
            pl.BlockSpec((1, D_MODEL), lambda i, *_: (0, 0)),
            pl.BlockSpec(memory_space=pl.ANY),
        ],
        out_specs=tok(D_MODEL),
        scratch_shapes=[pltpu.VMEM((2, LOCAL_ROWS, D_MODEL), F32), pltpu.SemaphoreType.DMA((2,))],
    )
    return pl.pallas_call(
        functools.partial(_combine_kernel, tiles, tile_base),
        grid_spec=grid_spec,
        out_shape=jax.ShapeDtypeStruct((tokens, D_MODEL), F32),
        compiler_params=pltpu.CompilerParams(
            dimension_semantics=("arbitrary",), vmem_limit_bytes=VMEM_LIMIT),
        name="combine",
    )(*runs, x1, route, mod4, final_g, ys)


def _routing_tables(counts):
    counts = counts.astype(jnp.int32)
    tiles = counts.shape[0]
    run_len = ((counts + SUBLANES - 1) // SUBLANES) * SUBLANES
    run_src = jnp.cumsum(run_len, axis=1) - run_len
    group_rows = jnp.sum(run_len, axis=0)
    padded = ((group_rows + GROUP_TILE - 1) // GROUP_TILE) * GROUP_TILE
    ends = jnp.cumsum(padded)
    offs = ends - padded
    run_dst = offs[None, :] + jnp.cumsum(run_len, axis=0) - run_len

    n_used = ends[-1] // GROUP_TILE
    max_rows = tiles * (OUTPROJ_TILE + N_GROUPS * (SUBLANES - 1))
    max_tiles = -(-max_rows // GROUP_TILE) + N_GROUPS
    tile_ids = jnp.minimum(jnp.arange(max_tiles, dtype=jnp.int32), n_used - 1)
    tile_group = jnp.sum(tile_ids[:, None] * GROUP_TILE >= ends[None, :], axis=1).astype(jnp.int32)
    of_group = (tile_group[:, None] == jnp.arange(N_GROUPS, dtype=jnp.int32))[:, None, :]
    start = jnp.sum(jnp.where(of_group, run_dst[None], 0), axis=-1)
    stop = start + jnp.sum(jnp.where(of_group, run_len[None], 0), axis=-1)
    row0 = (tile_ids * GROUP_TILE)[:, None]
    first = jnp.sum(stop <= row0, axis=1).astype(jnp.int32)
    last = jnp.sum(start < row0 + GROUP_TILE, axis=1).astype(jnp.int32) - 1
    group_end = jnp.sum(jnp.where(of_group[:, 0, :], (offs + group_rows)[None, :], 0), axis=-1)
    fill = jnp.clip(group_end - row0[:, 0], 0, GROUP_TILE).astype(jnp.int32)
    flat = lambda a: a.reshape(-1)
    return ((tile_group, n_used.reshape(1), first, last, fill), (flat(run_len), flat(run_src), flat(run_dst)),
            max_tiles * GROUP_TILE)


def _mixer(x, mod4, mod_row, is_grid, s_f0, s_b0, p):
    norm_mix_g, w_in_bf, conv_w, decay_rows = p
    y_conv, q, k, v, g = _inproj(x, mod4, mod_row, norm_mix_g, w_in_bf, conv_w, is_grid)
    ret = _retention(q, k, v, g, decay_rows, s_f0, s_b0, emit_state=not is_grid)
    flat = lambda a: a.reshape(-1, a.shape[-1])
    return flat(y_conv), flat(ret[0]), ret[1:]


def kernel(x_prompt, x_sample, state_ret_fwd, state_ret_bwd, c, c_ctx, norm_mix_g, norm_ffn_g, w_ada, b_ada, w_in, conv_w, ret_decay_fwd, ret_decay_bwd, w_out, w_router_group, b_router_group, w_router_expert, b_router_expert, w_gate_e, w_up_e, w_down_e, final_norm_g):
    assert norm_mix_g.shape[0] == 1, "single-layer backbone"
    n_lat = c.shape[0]
    ctx_row = n_lat
    mod_rows = 8
    cvec = jnp.concatenate([c, c_ctx[None, :], jnp.zeros((mod_rows - n_lat - 1, D_MODEL), F32)], axis=0)
    mod = _modulation(cvec, w_ada[0], b_ada[0][None, :])
    mod4 = mod.reshape(mod_rows, 6, 1, D_MODEL)

    pad = ROUTER_COLS - N_GROUPS - N_EXPERTS
    w_router = jnp.concatenate(
        [w_router_group[0], w_router_expert[0], jnp.zeros((D_MODEL, pad), F32)], axis=1).astype(BF16)
    b_router = jnp.concatenate([b_router_group[0], b_router_expert[0], jnp.zeros((pad,), F32)])[None, :]
    decay_rows = jnp.broadcast_to(
        jnp.concatenate([ret_decay_fwd[0], ret_decay_bwd[0]])[:, None], (2 * RET_HEADS, LANES))
    p_mix = (norm_mix_g, w_in[0].astype(BF16), conv_w[0], decay_rows)
    w_out_bf = w_out[0].astype(BF16)
    final_g = final_norm_g[None, :]

    ctx_mod = lambda i: ctx_row
    lat_mod = lambda i: i // (x_sample.shape[1] // OUTPROJ_TILE)
    flat = lambda a: a.reshape(-1, a.shape[-1])

    yc_c, yr_c, (s_f, s_b) = _mixer(x_prompt, mod4, lambda b: ctx_row, False, None, None, p_mix)
    yc_l, yr_l, _ = _mixer(x_sample, mod4, lambda b: b, True, state_ret_fwd, state_ret_bwd, p_mix)

    x1_c, xloc_c, route_c, cnt_c = _outproj(
        yc_c, yr_c, flat(x_prompt), mod4, ctx_mod, norm_ffn_g, w_out_bf, w_router, b_router)
    x1_l, xloc_l, route_l, cnt_l = _outproj(
        yc_l, yr_l, flat(x_sample), mod4, lat_mod, norm_ffn_g, w_out_bf, w_router, b_router)

    counts = jnp.concatenate([cnt_c, cnt_l], axis=0)[:, 0, :N_GROUPS]
    tile_tables, runs, slots = _routing_tables(counts)
    ys = _experts(xloc_c, xloc_l, tile_tables, runs, slots, w_gate_e[0], w_up_e[0], w_down_e[0])
    y_prompt = _combine(ys, runs, x1_c, route_c, mod4, ctx_mod, final_g, 0)
    y_sample = _combine(ys, runs, x1_l, route_l, mod4, lat_mod, final_g, cnt_c.shape[0])
    return (y_prompt.reshape(x_prompt.shape), y_sample.reshape(x_sample.shape),
            s_f.astype(x_prompt.dtype), s_b.astype(x_prompt.dtype))
```

```python
import functools

import jax
import jax.numpy as jnp
from jax import lax
from jax.experimental import pallas as pl
from jax.experimental.pallas import tpu as pltpu

F32 = jnp.float32
BF16 = jnp.bfloat16

D_MODEL = 1024
GRID_W = 64
CONV_W = 512
RET_HEADS = 4
RET_DK = 128
RET_DV = 128
RET_W = RET_HEADS * RET_DV
QK_W = RET_HEADS * RET_DK
CHUNK = 128
N_GROUPS = 4
EXPERTS_PER_GROUP = 8
N_EXPERTS = N_GROUPS * EXPERTS_PER_GROUP
D_EXPERT = 256
ROPE_BASE = 10000.0
EPS = 1e-6

LANES = 128
TOKEN_TILE = 512
OUTPROJ_TILE = 512
GROUP_TILE = 1024
EXPERTS_PER_STEP = 2
RET_UNROLL = 8
SUBLANES = 8
LOCAL_ROWS = OUTPROJ_TILE + LANES
ROW_W = D_MODEL + LANES
ROUTE_GROUP, ROUTE_LOCAL, ROUTE_E1, ROUTE_E2, ROUTE_W1, ROUTE_W2 = range(6)
assert LOCAL_ROWS >= OUTPROJ_TILE + N_GROUPS * (SUBLANES - 1)
MOD_COLS = 1536
ROUTER_COLS = LANES
VMEM_LIMIT = 48 * 1024 * 1024
EXPERT_VMEM_LIMIT = 56 * 1024 * 1024


def _silu(x):
    return x * jax.nn.sigmoid(x)


def _rms(x):
    return x * lax.rsqrt(jnp.mean(x * x, axis=-1, keepdims=True) + EPS)


def _bdot(a, b):
    return jnp.dot(a.astype(BF16), b.astype(BF16), preferred_element_type=F32)


def _split3(x):
    hi = x.astype(BF16)
    rest = x - hi.astype(F32)
    mid = rest.astype(BF16)
    return hi, mid, (rest - mid.astype(F32)).astype(BF16)


def _mod_kernel(c_ref, w_ref, b_ref, o_ref):
    o_ref[...] = _bdot(_silu(c_ref[...]), w_ref[...]) + b_ref[...]


def _modulation(cvec, w_ada, b_ada):
    rows = cvec.shape[0]
    n = w_ada.shape[1]
    return pl.pallas_call(
        _mod_kernel,
        grid=(n // MOD_COLS,),
        in_specs=[
            pl.BlockSpec((rows, D_MODEL), lambda j: (0, 0)),
            pl.BlockSpec((D_MODEL, MOD_COLS), lambda j: (0, j)),
            pl.BlockSpec((1, MOD_COLS), lambda j: (0, j)),
        ],
        out_specs=pl.BlockSpec((rows, MOD_COLS), lambda j: (0, j)),
        out_shape=jax.ShapeDtypeStruct((rows, n), F32),
        compiler_params=pltpu.CompilerParams(vmem_limit_bytes=VMEM_LIMIT),
        name="modulation",
    )(cvec, w_ada, b_ada)


def _inproj_kernel(seg, is_grid, x_ref, sh_ref, sc_ref, ng_ref, w_ref, cw_ref, *rest):
    if is_grid:
        cos_ref, sa_ref, sb_ref, yc_ref, q_ref, k_ref, v_ref, g_ref = rest
    else:
        yc_ref, q_ref, k_ref, v_ref, g_ref = rest
    x = x_ref[...]
    xn = (_rms(x) * ng_ref[...]) * (1.0 + sc_ref[0, 0]) + sh_ref[0, 0]
    xb = xn.astype(BF16)

    def proj(c0, n):
        return jnp.dot(xb, w_ref[:, c0:c0 + n], preferred_element_type=F32)

    gate_b = proj(0, CONV_W)
    u = proj(CONV_W, CONV_W) * proj(2 * CONV_W, CONV_W)
    rows = u.shape[0]
    pos = lax.broadcasted_iota(jnp.int32, u.shape, 0) & (seg - 1)
    u_prev = jnp.where(pos != 0, pltpu.roll(u, 1, 0), 0.0)
    u_next = jnp.where(pos != seg - 1, pltpu.roll(u, rows - 1, 0), 0.0)
    conv = cw_ref[0:1, :] * u_prev + cw_ref[1:2, :] * u + cw_ref[2:3, :] * u_next
    yc_ref[...] = (gate_b * conv).astype(yc_ref.dtype)

    q0 = 3 * CONV_W
    q = proj(q0, QK_W)
    k = proj(q0 + QK_W, QK_W)
    if is_grid:
        cos, sa, sb = cos_ref[...], sa_ref[...], sb_ref[...]

        def rope(t):
            out = []
            for h in range(RET_HEADS):
                th = t[:, h * RET_DK:(h + 1) * RET_DK]
                out.append(th * cos + pltpu.roll(th, RET_DK - 1, 1) * sa + pltpu.roll(th, 1, 1) * sb)
            return jnp.concatenate(out, axis=1)

        q, k = rope(q), rope(k)
    q_ref[...] = q
    k_ref[...] = k
    v_ref[...] = proj(q0 + 2 * QK_W, RET_W)
    g_ref[...] = proj(q0 + 2 * QK_W + RET_W, RET_W)


def _rope_tables(length):
    pos = jnp.arange(length)
    row = (pos // GRID_W).astype(F32)
    col = (pos % GRID_W).astype(F32)
    n_pairs = RET_DK // 4
    freqs = ROPE_BASE ** (-(jnp.arange(n_pairs, dtype=F32) * 2.0 / (RET_DK // 2)))
    ang = jnp.concatenate([row[:, None] * freqs, col[:, None] * freqs], axis=-1)
    cos = jnp.repeat(jnp.cos(ang), 2, axis=-1)
    sin = jnp.repeat(jnp.sin(ang), 2, axis=-1)
    even = (jnp.arange(RET_DK) % 2) == 0
    return cos, jnp.where(even, -sin, 0.0), jnp.where(even, 0.0, sin)


def _inproj(x, mod4, mod_row, norm_g, w_in_bf, conv_w, is_grid):
    bsz, length, _ = x.shape
    seg = GRID_W if is_grid else length
    assert TOKEN_TILE % seg == 0 and (length % TOKEN_TILE == 0 or TOKEN_TILE % length == 0)
    tokens = bsz * length
    tiles_per_seq = max(length // TOKEN_TILE, 1)
    seqs_per_tile = max(TOKEN_TILE // length, 1)
    batch_of = lambda i: (i // tiles_per_seq) * seqs_per_tile

    def mod_spec(which):
        return pl.BlockSpec((1, 1, 1, D_MODEL), lambda i: (mod_row(batch_of(i)), which, 0, 0))

    def tok_spec(width):
        return pl.BlockSpec((TOKEN_TILE, width), lambda i: (i, 0))

    in_specs = [
        tok_spec(D_MODEL), mod_spec(0), mod_spec(1),
        pl.BlockSpec((1, D_MODEL), lambda i: (0, 0)),
        pl.BlockSpec(w_in_bf.shape, lambda i: (0, 0)),
        pl.BlockSpec(conv_w.shape, lambda i: (0, 0)),
    ]
    args = [x.reshape(tokens, D_MODEL), mod4, mod4, norm_g, w_in_bf, conv_w]
    if is_grid:
        assert length % TOKEN_TILE == 0
        in_specs += [pl.BlockSpec((TOKEN_TILE, RET_DK), lambda i: (i % tiles_per_seq, 0))] * 3
        args += list(_rope_tables(length))
    shp = lambda w, dt: jax.ShapeDtypeStruct((tokens, w), dt)
    return pl.pallas_call(
        functools.partial(_inproj_kernel, seg, is_grid),
        grid=(tokens // TOKEN_TILE,),
        in_specs=in_specs,
        out_specs=[tok_spec(CONV_W), tok_spec(QK_W), tok_spec(QK_W), tok_spec(RET_W), tok_spec(RET_W)],
        out_shape=[shp(CONV_W, BF16), shp(QK_W, F32), shp(QK_W, F32), shp(RET_W, F32), shp(RET_W, F32)],
        compiler_params=pltpu.CompilerParams(
            dimension_semantics=("parallel",), vmem_limit_bytes=VMEM_LIMIT),
        name="inproj_grid" if is_grid else "inproj_seq",
    )(*args)


def _ret_kernel(n_chunks, heads, has_init, emit_state, a_ref, q_ref, k_ref, v_ref, g_ref, *rest):
    rest = list(rest)
    if has_init:
        sf0_ref, sb0_ref = rest[:2]
        rest = rest[2:]
    y_ref = rest.pop(0)
    if emit_state:
        sf_out, sb_out = rest[:2]
        rest = rest[2:]
    st_f, st_b, dec = rest
    c = CHUNK
    sq = (c, c)
    head0 = pl.program_id(0) * heads

    def log_decays(hh):
        lg_f = jnp.log1p(-jnp.exp(a_ref[pl.ds(head0 + hh, 1), :]))
        lg_b = jnp.log1p(-jnp.exp(a_ref[pl.ds(head0 + hh + RET_HEADS, 1), :]))
        return lg_f, lg_b

    @pl.when(pl.program_id(1) == 0)
    def _():
        row = lax.broadcasted_iota(jnp.int32, sq, 0).astype(F32)
        col = lax.broadcasted_iota(jnp.int32, sq, 1).astype(F32)
        scale = RET_DK ** -0.5
        for hh in range(heads):
            lg_f, lg_b = log_decays(hh)
            dec[hh, 0] = scale * (
                jnp.where(row >= col, jnp.exp(jnp.where(row >= col, row - col, 0.0) * lg_f), 0.0)
                + jnp.where(col >= row, jnp.exp(jnp.where(col >= row, col - row, 0.0) * lg_b), 0.0))
            dec[hh, 1] = jnp.exp((row + 1.0) * lg_f)
            dec[hh, 2] = jnp.exp((c - row) * lg_b)
            dec[hh, 3] = scale * jnp.exp((c - 1.0 - col) * lg_f)
            dec[hh, 4] = scale * jnp.exp(col * lg_b)

    def rows(n):
        return pl.ds(pl.multiple_of(n * c, c), c) if not isinstance(n, int) else pl.ds(n * c, c)

    def cols(hh):
        return slice(hh * RET_DK, (hh + 1) * RET_DK)

    def kv_step(hh, n):
        kt = jnp.transpose(k_ref[0, rows(n), cols(hh)])
        lhs = jnp.concatenate([kt * dec[hh, 3], kt * dec[hh, 4]], axis=0)
        kv = _bdot(lhs, v_ref[0, rows(n), cols(hh)])
        st_f[hh, n] = kv[:RET_DK]
        st_b[hh, n] = kv[RET_DK:]

    def scan(hh, st, decay, order, s):
        def step(i, s):
            n = order(i)
            kv = st[hh, n]
            st[hh, n] = s
            return s * decay + kv
        if n_chunks <= RET_UNROLL:
            for i in range(n_chunks):
                s = step(i, s)
            return s
        return lax.fori_loop(0, n_chunks, step, s, unroll=RET_UNROLL)

    def out_step(hh, n):
        q = q_ref[0, rows(n), cols(hh)]
        scores = lax.dot_general(q.astype(BF16), k_ref[0, rows(n), cols(hh)].astype(BF16),
                                 (((1,), (1,)), ((), ())), preferred_element_type=F32)
        o = _bdot(scores * dec[hh, 0], v_ref[0, rows(n), cols(hh)])
        q_dec = jnp.concatenate([q * dec[hh, 1], q * dec[hh, 2]], axis=1)
        o = o + _bdot(q_dec, jnp.concatenate([st_f[hh, n], st_b[hh, n]], axis=0))
        y = _silu(g_ref[0, rows(n), cols(hh)]) * _rms(o)
        y_ref[0, rows(n), cols(hh)] = y.astype(y_ref.dtype)

    def over_chunks(step):
        if n_chunks * heads <= RET_UNROLL:
            for hh in range(heads):
                for n in range(n_chunks):
                    step(hh, n)
        else:
            for hh in range(heads):
                lax.fori_loop(0, n_chunks, lambda n, carry: (step(hh, n), carry)[1], 0, unroll=RET_UNROLL)

    over_chunks(kv_step)
    finals = []
    for hh in range(heads):
        lg_f, lg_b = log_decays(hh)
        s_f = sf0_ref[0, 0, hh] if has_init else jnp.zeros(sq, F32)
        s_b = sb0_ref[0, 0, hh] if has_init else jnp.zeros(sq, F32)
        s_f = scan(hh, st_f, jnp.exp(c * lg_f), lambda i: i, s_f)
        s_b = scan(hh, st_b, jnp.exp(c * lg_b), lambda i: n_chunks - 1 - i, s_b)
        finals.append((s_f, s_b))
    over_chunks(out_step)
    if emit_state:
        for hh, (s_f, s_b) in enumerate(finals):
            sf_out[0, 0, hh] = s_f
            sb_out[0, 0, hh] = s_b


def _retention(q, k, v, g, decay_rows, s_f0, s_b0, emit_state):
    bsz, length, _ = q.shape
    n_chunks = length // CHUNK
    has_init = s_f0 is not None
    heads = RET_HEADS if n_chunks * RET_HEADS <= RET_UNROLL else 1
    head_spec = pl.BlockSpec((1, length, heads * RET_DK), lambda h, b: (b, 0, h))
    st_spec = pl.BlockSpec((1, 1, heads, RET_DK, RET_DV), lambda h, b: (b, 0, h, 0, 0))
    in_specs = [pl.BlockSpec(decay_rows.shape, lambda h, b: (0, 0))] + [head_spec] * 4
    args = [decay_rows, q, k, v, g]
    if has_init:
        in_specs += [st_spec, st_spec]
        args += [s_f0, s_b0]
    out_specs = [head_spec]
    out_shape = [jax.ShapeDtypeStruct((bsz, length, RET_W), BF16)]
    if emit_state:
        st_shape = jax.ShapeDtypeStruct((bsz, 1, RET_HEADS, RET_DK, RET_DV), F32)
        out_specs += [st_spec, st_spec]
        out_shape += [st_shape, st_shape]
    return pl.pallas_call(
        functools.partial(_ret_kernel, n_chunks, heads, has_init, emit_state),
        grid=(RET_HEADS // heads, bsz),
        in_specs=in_specs,
        out_specs=out_specs,
        out_shape=out_shape,
        scratch_shapes=[
            pltpu.VMEM((heads, n_chunks, RET_DK, RET_DV), F32),
            pltpu.VMEM((heads, n_chunks, RET_DK, RET_DV), F32),
            pltpu.VMEM((heads, 5, CHUNK, CHUNK), F32),
        ],
        compiler_params=pltpu.CompilerParams(
            dimension_semantics=("arbitrary", "arbitrary"), vmem_limit_bytes=VMEM_LIMIT),
        name="retention_init" if has_init else "retention_zero",
    )(*args)


def _route(logits):
    lane = lax.broadcasted_iota(jnp.int32, logits.shape, 1)
    lane_f = lane.astype(F32)
    neg = -jnp.inf
    far = float(LANES)
    is_g = lane < N_GROUPS
    lg = jnp.where(is_g, logits, neg)
    g_max = jnp.max(lg, axis=1, keepdims=True)
    g_idx = jnp.min(jnp.where(lg == g_max, lane_f, far), axis=1, keepdims=True)
    p_sel = 1.0 / jnp.sum(jnp.where(is_g, jnp.exp(lg - g_max), 0.0), axis=1, keepdims=True)
    lane_group = ((lane - N_GROUPS) >> 3).astype(F32)
    sel = (lane >= N_GROUPS) & (lane < N_GROUPS + N_EXPERTS) & (lane_group == g_idx)
    le = jnp.where(sel, logits, neg)
    v1 = jnp.max(le, axis=1, keepdims=True)
    i1 = jnp.min(jnp.where(le == v1, lane_f, far), axis=1, keepdims=True)
    le2 = jnp.where(lane_f == i1, neg, le)
    v2 = jnp.max(le2, axis=1, keepdims=True)
    i2 = jnp.min(jnp.where(le2 == v2, lane_f, far), axis=1, keepdims=True)
    e2 = jnp.exp(v2 - v1)
    w1 = p_sel * (1.0 / (1.0 + e2))
    w2 = p_sel * (e2 / (1.0 + e2))
    return lane, lane_f, g_idx, i1, i2, w1, w2


def _outproj_kernel(yc_ref, yr_ref, x_ref, g1_ref, sh_ref, sc_ref, ng_ref, wo_ref, wr_ref, br_ref,
                    x1_ref, xloc_ref, route_ref, cnt_ref):
    m = (jnp.dot(yc_ref[...], wo_ref[0:CONV_W, :], preferred_element_type=F32)
         + jnp.dot(yr_ref[...], wo_ref[CONV_W:, :], preferred_element_type=F32))
    x1 = x_ref[...] + g1_ref[0, 0] * m
    x1_ref[...] = x1
    xn = (_rms(x1) * ng_ref[...]) * (1.0 + sc_ref[0, 0]) + sh_ref[0, 0]
    xb = xn.astype(BF16)
    logits = jnp.dot(xb, wr_ref[...], preferred_element_type=F32) + br_ref[...]
    lane, lane_f, g_idx, i1, i2, w1, w2 = _route(logits)

    picks = jnp.where(lane_f == g_idx, 1.0, 0.0)
    rows = picks.shape[0]
    tri = (lax.broadcasted_iota(jnp.int32, (rows, rows), 0)
           > lax.broadcasted_iota(jnp.int32, (rows, rows), 1))
    before = jnp.dot(jnp.where(tri, 1.0, 0.0).astype(BF16), picks.astype(BF16),
                     preferred_element_type=F32)
    count = jnp.sum(picks, axis=0, keepdims=True)
    cnt_ref[0] = count
    count8 = jnp.broadcast_to(jnp.floor((count + (SUBLANES - 1)) * (1.0 / SUBLANES)) * SUBLANES,
                              (SUBLANES, LANES))
    lane8 = lane[:SUBLANES]
    start = sum(jnp.where(lane8 >= k, pltpu.roll(count8, k, 1), 0.0) for k in range(1, N_GROUPS))
    local = jnp.sum(jnp.where(lane_f == g_idx, before + start[0:1], 0.0), axis=1, keepdims=True)
    route = jnp.where(lane == ROUTE_GROUP, g_idx, jnp.where(lane == ROUTE_LOCAL, local, jnp.where(
        lane == ROUTE_E1, i1 - N_GROUPS, jnp.where(lane == ROUTE_E2, i2 - N_GROUPS, jnp.where(
            lane == ROUTE_W1, w1, jnp.where(lane == ROUTE_W2, w2, 0.0))))))
    route_ref[...] = route

    local_row = jnp.transpose(jnp.broadcast_to(local, (rows, LANES)))[0:1, :]
    place = jnp.where(lax.broadcasted_iota(jnp.int32, (LOCAL_ROWS, rows), 0).astype(F32) == local_row,
                      1.0, 0.0).astype(BF16)
    xloc_ref[0, :, :D_MODEL] = jnp.dot(place, xb, preferred_element_type=F32)
    xloc_ref[0, :, D_MODEL:] = sum(
        jnp.dot(place, piece, preferred_element_type=F32) for piece in _split3(route))


def _outproj(y_conv, y_ret, x, mod4, mod_row_of_tile, norm_g, w_out_bf, w_router_bf, b_router):
    tokens = x.shape[0]
    tiles = tokens // OUTPROJ_TILE

    def mod_spec(which):
        return pl.BlockSpec((1, 1, 1, D_MODEL), lambda i: (mod_row_of_tile(i), which, 0, 0))

    tok = lambda w: pl.BlockSpec((OUTPROJ_TILE, w), lambda i: (i, 0))
    full = lambda a: pl.BlockSpec(a.shape, lambda i: (0,) * a.ndim)
    return pl.pallas_call(
        _outproj_kernel,
        grid=(tiles,),
        in_specs=[tok(CONV_W), tok(RET_W), tok(D_MODEL), mod_spec(2), mod_spec(3), mod_spec(4),
                  full(norm_g), full(w_out_bf), full(w_router_bf), full(b_router)],
        out_specs=[tok(D_MODEL),
                   pl.BlockSpec((1, LOCAL_ROWS, ROW_W), lambda i: (i, 0, 0)),
                   tok(ROUTER_COLS),
                   pl.BlockSpec((1, 1, ROUTER_COLS), lambda i: (i, 0, 0))],
        out_shape=[jax.ShapeDtypeStruct((tokens, D_MODEL), F32),
                   jax.ShapeDtypeStruct((tiles, LOCAL_ROWS, ROW_W), F32),
                   jax.ShapeDtypeStruct((tokens, ROUTER_COLS), F32),
                   jax.ShapeDtypeStruct((tiles, 1, ROUTER_COLS), F32)],
        compiler_params=pltpu.CompilerParams(
            dimension_semantics=("parallel",), vmem_limit_bytes=VMEM_LIMIT),
        name="outproj",
    )(y_conv, y_ret, x, mod4, mod4, mod4, norm_g, w_out_bf, w_router_bf, b_router)


RUN_PIECES = tuple(SUBLANES << b for b in reversed(range((OUTPROJ_TILE // SUBLANES).bit_length())))


def _expert_kernel(ctx_tiles, tile_group_ref, n_used_ref, first_ref, last_ref, fill_ref,
                   run_len_ref, run_src_ref, run_dst_ref,
                   xloc_ctx_hbm, xloc_lat_hbm, w1_ref, w3_ref, w2_ref, ys_ref, xbuf, xb, gate_tabs, sem):
    j = pl.program_id(0)
    step = pl.program_id(1)
    n_used = n_used_ref[0]

    def tile_copies(tile, act):
        slot = tile % 2
        group = tile_group_ref[tile]
        row0 = tile * GROUP_TILE

        def from_token_tile(b, carry):
            run = b * N_GROUPS + group
            lo = jnp.maximum(run_dst_ref[run], row0)
            hi = jnp.minimum(run_dst_ref[run] + run_len_ref[run], row0 + GROUP_TILE)
            n = jnp.maximum(hi - lo, 0)
            src = run_src_ref[run] + lo - run_dst_ref[run]
            dst = lo - row0
            for size in RUN_PIECES:
                done = n & (-2 * size)
                take = (n & size) != 0
                into = xbuf.at[slot, pl.ds(pl.multiple_of(dst + done, SUBLANES), size)]
                rows = pl.ds(pl.multiple_of(src + done, SUBLANES), size)

                @pl.when(take & (b < ctx_tiles))
                def _():
                    act(pltpu.make_async_copy(xloc_ctx_hbm.at[b, rows], into, sem.at[slot]))

                @pl.when(take & (b >= ctx_tiles))
                def _():
                    act(pltpu.make_async_copy(xloc_lat_hbm.at[b - ctx_tiles, rows], into, sem.at[slot]))
            return carry

        lax.fori_loop(first_ref[tile], last_ref[tile] + 1, from_token_tile, 0)

    start = lambda cp: cp.start()
    wait = lambda cp: cp.wait()

    @pl.when(j < n_used)
    def _():
        @pl.when(step == 0)
        def _():
            @pl.when(j == 0)
            def _():
                xbuf[...] = jnp.zeros_like(xbuf)
                tile_copies(j, start)

            tile_copies(j, wait)
            rows_in = xbuf[j % 2]
            xb[...] = rows_in[:, :D_MODEL].astype(BF16)
            route = rows_in[:, D_MODEL:]
            lane = lax.broadcasted_iota(jnp.int32, route.shape, 1)
            for n, which in enumerate((ROUTE_E1, ROUTE_E2, ROUTE_W1, ROUTE_W2)):
                col = jnp.sum(jnp.where(lane == which, route, 0.0), axis=1, keepdims=True)
                gate_tabs[n] = jnp.broadcast_to(col, route.shape)

            @pl.when(j + 1 < n_used)
            def _():
                tile_copies(j + 1, start)

        def evaluate(rows):
            x = xb[:rows]
            total = None
            for s in range(EXPERTS_PER_STEP):
                expert = (tile_group_ref[j] * EXPERTS_PER_GROUP + step * EXPERTS_PER_STEP + s).astype(F32)
                gate = (jnp.where(gate_tabs[0, :rows] == expert, gate_tabs[2, :rows], 0.0)
                        + jnp.where(gate_tabs[1, :rows] == expert, gate_tabs[3, :rows], 0.0))
                hid = _silu(jnp.dot(x, w1_ref[0, s].astype(BF16), preferred_element_type=F32)) * jnp.dot(
                    x, w3_ref[0, s].astype(BF16), preferred_element_type=F32)
                y = jnp.dot(hid.astype(BF16), w2_ref[0, s].astype(BF16), preferred_element_type=F32)
                gated = jnp.concatenate(
                    [gate * y[:, c * LANES:(c + 1) * LANES] for c in range(D_MODEL // LANES)], axis=1)
                total = gated if total is None else total + gated

            @pl.when(step == 0)
            def _():
                ys_ref[:rows] = total

            @pl.when(step > 0)
            def _():
                ys_ref[:rows] += total

        half = GROUP_TILE // 2

        @pl.when(fill_ref[j] > half)
        def _():
            evaluate(GROUP_TILE)

        @pl.when(fill_ref[j] <= half)
        def _():
            evaluate(half)


def _experts(xloc_ctx, xloc_lat, tile_tables, runs, slots, w1, w3, w2):
    steps = EXPERTS_PER_GROUP // EXPERTS_PER_STEP
    paired = lambda w: w.reshape((N_EXPERTS // EXPERTS_PER_STEP, EXPERTS_PER_STEP) + w.shape[1:])
    w_spec = lambda shape: pl.BlockSpec((1, EXPERTS_PER_STEP) + shape, lambda j, s, tg, nu, *_: (
        tg[jnp.minimum(j, nu[0] - 1)] * steps + jnp.where(j < nu[0], s, steps - 1), 0, 0, 0))
    grid_spec = pltpu.PrefetchScalarGridSpec(
        num_scalar_prefetch=8,
        grid=(slots // GROUP_TILE, steps),
        in_specs=[
            pl.BlockSpec(memory_space=pl.ANY), pl.BlockSpec(memory_space=pl.ANY),
            w_spec((D_MODEL, D_EXPERT)), w_spec((D_MODEL, D_EXPERT)), w_spec((D_EXPERT, D_MODEL)),
        ],
        out_specs=pl.BlockSpec((GROUP_TILE, D_MODEL), lambda j, s, tg, nu, *_: (jnp.minimum(j, nu[0] - 1), 0)),
        scratch_shapes=[pltpu.VMEM((2, GROUP_TILE, ROW_W), F32), pltpu.VMEM((GROUP_TILE, D_MODEL), BF16),
                        pltpu.VMEM((4, GROUP_TILE, LANES), F32), pltpu.SemaphoreType.DMA((2,))],
    )
    return pl.pallas_call(
        functools.partial(_expert_kernel, xloc_ctx.shape[0]),
        grid_spec=grid_spec,
        out_shape=jax.ShapeDtypeStruct((slots, D_MODEL), F32),
        compiler_params=pltpu.CompilerParams(
            dimension_semantics=("arbitrary", "arbitrary"), vmem_limit_bytes=EXPERT_VMEM_LIMIT),
        name="experts",
    )(*tile_tables, *runs, xloc_ctx, xloc_lat, paired(w1), paired(w3), paired(w2))


def _combine_kernel(n_tiles, tile_base, run_len_ref, run_src_ref, run_dst_ref,
                    x1_ref, route_ref, g2_ref, fg_ref, ys_hbm, o_ref, buf, sem):
    i = pl.program_id(0)
    slot = i % 2

    def run_copies(local_tile, act):
        s = local_tile % 2
        tile = local_tile + tile_base
        for g in range(N_GROUPS):
            n = run_len_ref[tile * N_GROUPS + g]
            src = run_src_ref[tile * N_GROUPS + g]
            dst = run_dst_ref[tile * N_GROUPS + g]
            for size in RUN_PIECES:
                done = n & (-2 * size)

                @pl.when((n & size) != 0)
                def _():
                    act(pltpu.make_async_copy(
                        ys_hbm.at[pl.ds(pl.multiple_of(dst + done, SUBLANES), size)],
                        buf.at[s, pl.ds(pl.multiple_of(src + done, SUBLANES), size)], sem.at[s]))

    @pl.when(i == 0)
    def _():
        buf[...] = jnp.zeros_like(buf)
        run_copies(i, lambda cp: cp.start())

    @pl.when(i + 1 < n_tiles)
    def _():
        run_copies(i + 1, lambda cp: cp.start())

    run_copies(i, lambda cp: cp.wait())
    route = route_ref[...]
    lane = lax.broadcasted_iota(jnp.int32, route.shape, 1)
    local = jnp.sum(jnp.where(lane == ROUTE_LOCAL, route, 0.0), axis=1, keepdims=True)
    pick = jnp.where(lax.broadcasted_iota(jnp.int32, (route.shape[0], LOCAL_ROWS), 1).astype(F32) == local,
                     1.0, 0.0).astype(BF16)
    moe = sum(jnp.dot(pick, piece, preferred_element_type=F32) for piece in _split3(buf[slot])[:2])
    y = x1_ref[...] + g2_ref[0, 0] * moe
    o_ref[...] = _rms(y) * fg_ref[...]


def _combine(ys, runs, x1, route, mod4, mod_row_of_tile, final_g, tile_base):
    tokens = x1.shape[0]
    tiles = tokens // OUTPROJ_TILE
    tok = lambda w: pl.BlockSpec((OUTPROJ_TILE, w), lambda i, *_: (i, 0))
    grid_spec = pltpu.PrefetchScalarGridSpec(
        num_scalar_prefetch=3,
        grid=(tiles,),
        in_specs=[
            tok(D_MODEL), tok(ROUTER_COLS),
            pl.BlockSpec((1, 1, 1, D_MODEL), lambda i, *_: (mod_row_of_tile(i), 5, 0, 0)),
            pl.BlockSpec((1, D_MODEL), lambda i, *_: (0, 0)),
            pl.BlockSpec(memory_space=pl.ANY),
        ],
        out_specs=tok(D_MODEL),
        scratch_shapes=[pltpu.VMEM((2, LOCAL_ROWS, D_MODEL), F32), pltpu.SemaphoreType.DMA((2,))],
    )
    return pl.pallas_call(
        functools.partial(_combine_kernel, tiles, tile_base),
        grid_spec=grid_spec,
        out_shape=jax.ShapeDtypeStruct((tokens, D_MODEL), F32),
        compiler_params=pltpu.CompilerParams(
            dimension_semantics=("arbitrary",), vmem_limit_bytes=VMEM_LIMIT),
        name="combine",
    )(*runs, x1, route, mod4, final_g, ys)


def _routing_tables(counts):
    counts = counts.astype(jnp.int32)
    tiles = counts.shape[0]
    run_len = ((counts + SUBLANES - 1) // SUBLANES) * SUBLANES
    run_src = jnp.cumsum(run_len, axis=1) - run_len
    group_rows = jnp.sum(run_len, axis=0)
    padded = ((group_rows + GROUP_TILE - 1) // GROUP_TILE) * GROUP_TILE
    ends = jnp.cumsum(padded)
    offs = ends - padded
    run_dst = offs[None, :] + jnp.cumsum(run_len, axis=0) - run_len

    n_used = ends[-1] // GROUP_TILE
    max_rows = tiles * (OUTPROJ_TILE + N_GROUPS * (SUBLANES - 1))
    max_tiles = -(-max_rows // GROUP_TILE) + N_GROUPS
    tile_ids = jnp.minimum(jnp.arange(max_tiles, dtype=jnp.int32), n_used - 1)
    tile_group = jnp.sum(tile_ids[:, None] * GROUP_TILE >= ends[None, :], axis=1).astype(jnp.int32)
    of_group = (tile_group[:, None] == jnp.arange(N_GROUPS, dtype=jnp.int32))[:, None, :]
    start = jnp.sum(jnp.where(of_group, run_dst[None], 0), axis=-1)
    stop = start + jnp.sum(jnp.where(of_group, run_len[None], 0), axis=-1)
    row0 = (tile_ids * GROUP_TILE)[:, None]
    first = jnp.sum(stop <= row0, axis=1).astype(jnp.int32)
    last = jnp.sum(start < row0 + GROUP_TILE, axis=1).astype(jnp.int32) - 1
    group_end = jnp.sum(jnp.where(of_group[:, 0, :], (offs + group_rows)[None, :], 0), axis=-1)
    fill = jnp.clip(group_end - row0[:, 0], 0, GROUP_TILE).astype(jnp.int32)
    flat = lambda a: a.reshape(-1)
    return ((tile_group, n_used.reshape(1), first, last, fill), (flat(run_len), flat(run_src), flat(run_dst)),
            max_tiles * GROUP_TILE)


def _mixer(x, mod4, mod_row, is_grid, s_f0, s_b0, p):
    norm_mix_g, w_in_bf, conv_w, decay_rows = p
    y_conv, *qkvg = _inproj(x, mod4, mod_row, norm_mix_g, w_in_bf, conv_w, is_grid)
    per_seq = lambda a: a.reshape(x.shape[0], x.shape[1], a.shape[-1])
    ret = _retention(*map(per_seq, qkvg), decay_rows, s_f0, s_b0, emit_state=not is_grid)
    return y_conv, ret[0].reshape(-1, RET_W), ret[1:]


def kernel(x_prompt, x_sample, state_ret_fwd, state_ret_bwd, c, c_ctx, norm_mix_g, norm_ffn_g, w_ada, b_ada, w_in, conv_w, ret_decay_fwd, ret_decay_bwd, w_out, w_router_group, b_router_group, w_router_expert, b_router_expert, w_gate_e, w_up_e, w_down_e, final_norm_g):
    assert norm_mix_g.shape[0] == 1, "single-layer backbone"
    n_lat = c.shape[0]
    ctx_row = n_lat
    mod_rows = 8
    cvec = jnp.concatenate([c, c_ctx[None, :], jnp.zeros((mod_rows - n_lat - 1, D_MODEL), F32)], axis=0)
    mod = _modulation(cvec, w_ada[0], b_ada[0][None, :])
    mod4 = mod.reshape(mod_rows, 6, 1, D_MODEL)

    pad = ROUTER_COLS - N_GROUPS - N_EXPERTS
    w_router = jnp.concatenate(
        [w_router_group[0], w_router_expert[0], jnp.zeros((D_MODEL, pad), F32)], axis=1).astype(BF16)
    b_router = jnp.concatenate([b_router_group[0], b_router_expert[0], jnp.zeros((pad,), F32)])[None, :]
    decay_rows = jnp.broadcast_to(
        jnp.concatenate([ret_decay_fwd[0], ret_decay_bwd[0]])[:, None], (2 * RET_HEADS, LANES))
    p_mix = (norm_mix_g, w_in[0].astype(BF16), conv_w[0], decay_rows)
    w_out_bf = w_out[0].astype(BF16)
    final_g = final_norm_g[None, :]

    ctx_mod = lambda i: ctx_row
    lat_mod = lambda i: i // (x_sample.shape[1] // OUTPROJ_TILE)
    flat = lambda a: a.reshape(-1, a.shape[-1])

    yc_c, yr_c, (s_f, s_b) = _mixer(x_prompt, mod4, lambda b: ctx_row, False, None, None, p_mix)
    yc_l, yr_l, _ = _mixer(x_sample, mod4, lambda b: b, True, state_ret_fwd, state_ret_bwd, p_mix)

    x1_c, xloc_c, route_c, cnt_c = _outproj(
        yc_c, yr_c, flat(x_prompt), mod4, ctx_mod, norm_ffn_g, w_out_bf, w_router, b_router)
    x1_l, xloc_l, route_l, cnt_l = _outproj(
        yc_l, yr_l, flat(x_sample), mod4, lat_mod, norm_ffn_g, w_out_bf, w_router, b_router)

    counts = jnp.concatenate([cnt_c, cnt_l], axis=0)[:, 0, :N_GROUPS]
    tile_tables, runs, slots = _routing_tables(counts)
    ys = _experts(xloc_c, xloc_l, tile_tables, runs, slots, w_gate_e[0], w_up_e[0], w_down_e[0])
    y_prompt = _combine(ys, runs, x1_c, route_c, mod4, ctx_mod, final_g, 0)
    y_sample = _combine(ys, runs, x1_l, route_l, mod4, lat_mod, final_g, cnt_c.shape[0])
    return (y_prompt.reshape(x_prompt.shape), y_sample.reshape(x_sample.shape),
            s_f.astype(x_prompt.dtype), s_b.astype(x_prompt.dtype))
```

```python
import functools

import jax
import jax.numpy as jnp
from jax import lax
from jax.experimental import pallas as pl
from jax.experimental.pallas import tpu as pltpu

F32 = jnp.float32
BF16 = jnp.bfloat16

D_MODEL = 1024
GRID_W = 64
CONV_W = 512
RET_HEADS = 4
RET_DK = 128
RET_DV = 128
RET_W = RET_HEADS * RET_DV
QK_W = RET_HEADS * RET_DK
CHUNK = 128
N_GROUPS = 4
EXPERTS_PER_GROUP = 8
N_EXPERTS = N_GROUPS * EXPERTS_PER_GROUP
D_EXPERT = 256
ROPE_BASE = 10000.0
EPS = 1e-6

LANES = 128
TOKEN_TILE = 512
OUTPROJ_TILE = 512
GROUP_TILE = 1024
EXPERTS_PER_STEP = 2
RET_UNROLL = 8
SUBLANES = 8
XLOC_ROWS = OUTPROJ_TILE + N_GROUPS * SUBLANES
LOCAL_ROWS = OUTPROJ_TILE + LANES
ROW_W = D_MODEL + LANES
ROUTE_GROUP, ROUTE_LOCAL, ROUTE_E1, ROUTE_E2, ROUTE_W1, ROUTE_W2 = range(6)
MOD_COLS = 1536
ROUTER_COLS = LANES
VMEM_LIMIT = 48 * 1024 * 1024
EXPERT_VMEM_LIMIT = 56 * 1024 * 1024


def _silu(x):
    return x * jax.nn.sigmoid(x)


def _rms(x):
    return x * lax.rsqrt(jnp.mean(x * x, axis=-1, keepdims=True) + EPS)


def _bdot(a, b):
    return jnp.dot(a.astype(BF16), b.astype(BF16), preferred_element_type=F32)


def _split3(x):
    hi = x.astype(BF16)
    rest = x - hi.astype(F32)
    mid = rest.astype(BF16)
    return hi, mid, (rest - mid.astype(F32)).astype(BF16)


def _mod_kernel(c_ref, w_ref, b_ref, o_ref):
    o_ref[...] = _bdot(_silu(c_ref[...]), w_ref[...]) + b_ref[...]


def _modulation(cvec, w_ada, b_ada):
    rows = cvec.shape[0]
    n = w_ada.shape[1]
    return pl.pallas_call(
        _mod_kernel,
        grid=(n // MOD_COLS,),
        in_specs=[
            pl.BlockSpec((rows, D_MODEL), lambda j: (0, 0)),
            pl.BlockSpec((D_MODEL, MOD_COLS), lambda j: (0, j)),
            pl.BlockSpec((1, MOD_COLS), lambda j: (0, j)),
        ],
        out_specs=pl.BlockSpec((rows, MOD_COLS), lambda j: (0, j)),
        out_shape=jax.ShapeDtypeStruct((rows, n), F32),
        compiler_params=pltpu.CompilerParams(vmem_limit_bytes=VMEM_LIMIT),
        name="modulation",
    )(cvec, w_ada, b_ada)


def _inproj_kernel(seg, is_grid, x_ref, sh_ref, sc_ref, ng_ref, w_ref, cw_ref, *rest):
    if is_grid:
        cos_ref, sa_ref, sb_ref, yc_ref, q_ref, k_ref, v_ref, g_ref = rest
    else:
        yc_ref, q_ref, k_ref, v_ref, g_ref = rest
    x = x_ref[...]
    xn = (_rms(x) * ng_ref[...]) * (1.0 + sc_ref[0, 0]) + sh_ref[0, 0]
    xb = xn.astype(BF16)

    def proj(c0, n):
        return jnp.dot(xb, w_ref[:, c0:c0 + n], preferred_element_type=F32)

    gate_b = proj(0, CONV_W)
    u = proj(CONV_W, CONV_W) * proj(2 * CONV_W, CONV_W)
    rows = u.shape[0]
    pos = lax.broadcasted_iota(jnp.int32, u.shape, 0) & (seg - 1)
    u_prev = jnp.where(pos != 0, pltpu.roll(u, 1, 0), 0.0)
    u_next = jnp.where(pos != seg - 1, pltpu.roll(u, rows - 1, 0), 0.0)
    conv = cw_ref[0:1, :] * u_prev + cw_ref[1:2, :] * u + cw_ref[2:3, :] * u_next
    yc_ref[...] = (gate_b * conv).astype(yc_ref.dtype)

    q0 = 3 * CONV_W
    q = proj(q0, QK_W)
    k = proj(q0 + QK_W, QK_W)
    if is_grid:
        cos, sa, sb = cos_ref[...], sa_ref[...], sb_ref[...]

        def rope(t):
            out = []
            for h in range(RET_HEADS):
                th = t[:, h * RET_DK:(h + 1) * RET_DK]
                out.append(th * cos + pltpu.roll(th, RET_DK - 1, 1) * sa + pltpu.roll(th, 1, 1) * sb)
            return jnp.concatenate(out, axis=1)

        q, k = rope(q), rope(k)
    q_ref[...] = q
    k_ref[...] = k
    v_ref[...] = proj(q0 + 2 * QK_W, RET_W)
    g_ref[...] = proj(q0 + 2 * QK_W + RET_W, RET_W)


def _rope_tables(length):
    pos = jnp.arange(length)
    row = (pos // GRID_W).astype(F32)
    col = (pos % GRID_W).astype(F32)
    n_pairs = RET_DK // 4
    freqs = ROPE_BASE ** (-(jnp.arange(n_pairs, dtype=F32) * 2.0 / (RET_DK // 2)))
    ang = jnp.concatenate([row[:, None] * freqs, col[:, None] * freqs], axis=-1)
    cos = jnp.repeat(jnp.cos(ang), 2, axis=-1)
    sin = jnp.repeat(jnp.sin(ang), 2, axis=-1)
    even = (jnp.arange(RET_DK) % 2) == 0
    return cos, jnp.where(even, -sin, 0.0), jnp.where(even, 0.0, sin)


def _inproj(x, mod4, mod_row, norm_g, w_in_bf, conv_w, is_grid):
    bsz, length, _ = x.shape
    seg = GRID_W if is_grid else length
    assert TOKEN_TILE % seg == 0 and (length % TOKEN_TILE == 0 or TOKEN_TILE % length == 0)
    tokens = bsz * length
    tiles_per_seq = max(length // TOKEN_TILE, 1)
    seqs_per_tile = max(TOKEN_TILE // length, 1)
    batch_of = lambda i: (i // tiles_per_seq) * seqs_per_tile

    def mod_spec(which):
        return pl.BlockSpec((1, 1, 1, D_MODEL), lambda i: (mod_row(batch_of(i)), which, 0, 0))

    def tok_spec(width):
        return pl.BlockSpec((TOKEN_TILE, width), lambda i: (i, 0))

    in_specs = [
        tok_spec(D_MODEL), mod_spec(0), mod_spec(1),
        pl.BlockSpec((1, D_MODEL), lambda i: (0, 0)),
        pl.BlockSpec(w_in_bf.shape, lambda i: (0, 0)),
        pl.BlockSpec(conv_w.shape, lambda i: (0, 0)),
    ]
    args = [x.reshape(tokens, D_MODEL), mod4, mod4, norm_g, w_in_bf, conv_w]
    if is_grid:
        assert length % TOKEN_TILE == 0
        in_specs += [pl.BlockSpec((TOKEN_TILE, RET_DK), lambda i: (i % tiles_per_seq, 0))] * 3
        args += list(_rope_tables(length))
    shp = lambda w, dt: jax.ShapeDtypeStruct((tokens, w), dt)
    return pl.pallas_call(
        functools.partial(_inproj_kernel, seg, is_grid),
        grid=(tokens // TOKEN_TILE,),
        in_specs=in_specs,
        out_specs=[tok_spec(CONV_W), tok_spec(QK_W), tok_spec(QK_W), tok_spec(RET_W), tok_spec(RET_W)],
        out_shape=[shp(CONV_W, BF16), shp(QK_W, F32), shp(QK_W, F32), shp(RET_W, F32), shp(RET_W, F32)],
        compiler_params=pltpu.CompilerParams(
            dimension_semantics=("parallel",), vmem_limit_bytes=VMEM_LIMIT),
        name="inproj_grid" if is_grid else "inproj_seq",
    )(*args)


def _ret_kernel(n_chunks, heads, has_init, emit_state, a_ref, q_ref, k_ref, v_ref, g_ref, *rest):
    rest = list(rest)
    if has_init:
        sf0_ref, sb0_ref = rest[:2]
        rest = rest[2:]
    y_ref = rest.pop(0)
    if emit_state:
        sf_out, sb_out = rest[:2]
        rest = rest[2:]
    st_f, st_b, dec = rest
    c = CHUNK
    sq = (c, c)
    head0 = pl.program_id(0) * heads

    def log_decays(hh):
        lg_f = jnp.log1p(-jnp.exp(a_ref[pl.ds(head0 + hh, 1), :]))
        lg_b = jnp.log1p(-jnp.exp(a_ref[pl.ds(head0 + hh + RET_HEADS, 1), :]))
        return lg_f, lg_b

    @pl.when(pl.program_id(1) == 0)
    def _():
        row = lax.broadcasted_iota(jnp.int32, sq, 0).astype(F32)
        col = lax.broadcasted_iota(jnp.int32, sq, 1).astype(F32)
        scale = RET_DK ** -0.5
        for hh in range(heads):
            lg_f, lg_b = log_decays(hh)
            dec[hh, 0] = scale * (
                jnp.where(row >= col, jnp.exp(jnp.where(row >= col, row - col, 0.0) * lg_f), 0.0)
                + jnp.where(col >= row, jnp.exp(jnp.where(col >= row, col - row, 0.0) * lg_b), 0.0))
            dec[hh, 1] = jnp.exp((row + 1.0) * lg_f)
            dec[hh, 2] = jnp.exp((c - row) * lg_b)
            dec[hh, 3] = scale * jnp.exp((c - 1.0 - col) * lg_f)
            dec[hh, 4] = scale * jnp.exp(col * lg_b)

    def rows(n):
        return pl.ds(pl.multiple_of(n * c, c), c) if not isinstance(n, int) else pl.ds(n * c, c)

    def cols(hh):
        return slice(hh * RET_DK, (hh + 1) * RET_DK)

    def kv_step(hh, n):
        kt = jnp.transpose(k_ref[0, rows(n), cols(hh)])
        lhs = jnp.concatenate([kt * dec[hh, 3], kt * dec[hh, 4]], axis=0)
        kv = _bdot(lhs, v_ref[0, rows(n), cols(hh)])
        st_f[hh, n] = kv[:RET_DK]
        st_b[hh, n] = kv[RET_DK:]

    def scan(hh, st, decay, order, s):
        def step(i, s):
            n = order(i)
            kv = st[hh, n]
            st[hh, n] = s
            return s * decay + kv
        if n_chunks <= RET_UNROLL:
            for i in range(n_chunks):
                s = step(i, s)
            return s
        return lax.fori_loop(0, n_chunks, step, s, unroll=RET_UNROLL)

    def out_step(hh, n):
        q = q_ref[0, rows(n), cols(hh)]
        scores = lax.dot_general(q.astype(BF16), k_ref[0, rows(n), cols(hh)].astype(BF16),
                                 (((1,), (1,)), ((), ())), preferred_element_type=F32)
        o = _bdot(scores * dec[hh, 0], v_ref[0, rows(n), cols(hh)])
        q_dec = jnp.concatenate([q * dec[hh, 1], q * dec[hh, 2]], axis=1)
        o = o + _bdot(q_dec, jnp.concatenate([st_f[hh, n], st_b[hh, n]], axis=0))
        y = _silu(g_ref[0, rows(n), cols(hh)]) * _rms(o)
        y_ref[0, rows(n), cols(hh)] = y.astype(y_ref.dtype)

    def over_chunks(step):
        if n_chunks * heads <= RET_UNROLL:
            for hh in range(heads):
                for n in range(n_chunks):
                    step(hh, n)
        else:
            for hh in range(heads):
                lax.fori_loop(0, n_chunks, lambda n, carry: (step(hh, n), carry)[1], 0, unroll=RET_UNROLL)

    over_chunks(kv_step)
    finals = []
    for hh in range(heads):
        lg_f, lg_b = log_decays(hh)
        s_f = sf0_ref[0, 0, hh] if has_init else jnp.zeros(sq, F32)
        s_b = sb0_ref[0, 0, hh] if has_init else jnp.zeros(sq, F32)
        s_f = scan(hh, st_f, jnp.exp(c * lg_f), lambda i: i, s_f)
        s_b = scan(hh, st_b, jnp.exp(c * lg_b), lambda i: n_chunks - 1 - i, s_b)
        finals.append((s_f, s_b))
    over_chunks(out_step)
    if emit_state:
        for hh, (s_f, s_b) in enumerate(finals):
            sf_out[0, 0, hh] = s_f
            sb_out[0, 0, hh] = s_b


def _retention(q, k, v, g, decay_rows, s_f0, s_b0, emit_state):
    bsz, length, _ = q.shape
    n_chunks = length // CHUNK
    has_init = s_f0 is not None
    heads = RET_HEADS if n_chunks * RET_HEADS <= RET_UNROLL else 1
    head_spec = pl.BlockSpec((1, length, heads * RET_DK), lambda h, b: (b, 0, h))
    st_spec = pl.BlockSpec((1, 1, heads, RET_DK, RET_DV), lambda h, b: (b, 0, h, 0, 0))
    in_specs = [pl.BlockSpec(decay_rows.shape, lambda h, b: (0, 0))] + [head_spec] * 4
    args = [decay_rows, q, k, v, g]
    if has_init:
        in_specs += [st_spec, st_spec]
        args += [s_f0, s_b0]
    out_specs = [head_spec]
    out_shape = [jax.ShapeDtypeStruct((bsz, length, RET_W), BF16)]
    if emit_state:
        st_shape = jax.ShapeDtypeStruct((bsz, 1, RET_HEADS, RET_DK, RET_DV), F32)
        out_specs += [st_spec, st_spec]
        out_shape += [st_shape, st_shape]
    return pl.pallas_call(
        functools.partial(_ret_kernel, n_chunks, heads, has_init, emit_state),
        grid=(RET_HEADS // heads, bsz),
        in_specs=in_specs,
        out_specs=out_specs,
        out_shape=out_shape,
        scratch_shapes=[
            pltpu.VMEM((heads, n_chunks, RET_DK, RET_DV), F32),
            pltpu.VMEM((heads, n_chunks, RET_DK, RET_DV), F32),
            pltpu.VMEM((heads, 5, CHUNK, CHUNK), F32),
        ],
        compiler_params=pltpu.CompilerParams(
            dimension_semantics=("arbitrary", "arbitrary"), vmem_limit_bytes=VMEM_LIMIT),
        name="retention_init" if has_init else "retention_zero",
    )(*args)


def _route(logits):
    lane = lax.broadcasted_iota(jnp.int32, logits.shape, 1)
    lane_f = lane.astype(F32)
    neg = -jnp.inf
    far = float(LANES)
    is_g = lane < N_GROUPS
    lg = jnp.where(is_g, logits, neg)
    g_max = jnp.max(lg, axis=1, keepdims=True)
    g_idx = jnp.min(jnp.where(lg == g_max, lane_f, far), axis=1, keepdims=True)
    p_sel = 1.0 / jnp.sum(jnp.where(is_g, jnp.exp(lg - g_max), 0.0), axis=1, keepdims=True)
    lane_group = ((lane - N_GROUPS) >> 3).astype(F32)
    sel = (lane >= N_GROUPS) & (lane < N_GROUPS + N_EXPERTS) & (lane_group == g_idx)
    le = jnp.where(sel, logits, neg)
    v1 = jnp.max(le, axis=1, keepdims=True)
    i1 = jnp.min(jnp.where(le == v1, lane_f, far), axis=1, keepdims=True)
    le2 = jnp.where(lane_f == i1, neg, le)
    v2 = jnp.max(le2, axis=1, keepdims=True)
    i2 = jnp.min(jnp.where(le2 == v2, lane_f, far), axis=1, keepdims=True)
    e2 = jnp.exp(v2 - v1)
    w1 = p_sel * (1.0 / (1.0 + e2))
    w2 = p_sel * (e2 / (1.0 + e2))
    return lane, lane_f, g_idx, i1, i2, w1, w2


def _outproj_kernel(ctx_tiles, yc_c, yr_c, x_c, yc_l, yr_l, x_l, *rest):
    @pl.when(pl.program_id(0) < ctx_tiles)
    def _():
        _outproj_tile(yc_c, yr_c, x_c, *rest)

    @pl.when(pl.program_id(0) >= ctx_tiles)
    def _():
        _outproj_tile(yc_l, yr_l, x_l, *rest)


def _outproj_tile(yc_ref, yr_ref, x_ref, g1_ref, sh_ref, sc_ref, ng_ref, wo_ref, wr_ref, br_ref,
                  x1_ref, xloc_ref, route_ref, cnt_ref):
    m = (jnp.dot(yc_ref[...], wo_ref[0:CONV_W, :], preferred_element_type=F32)
         + jnp.dot(yr_ref[...], wo_ref[CONV_W:, :], preferred_element_type=F32))
    x1 = x_ref[...] + g1_ref[0, 0] * m
    x1_ref[...] = x1
    xn = (_rms(x1) * ng_ref[...]) * (1.0 + sc_ref[0, 0]) + sh_ref[0, 0]
    xb = xn.astype(BF16)
    logits = jnp.dot(xb, wr_ref[...], preferred_element_type=F32) + br_ref[...]
    lane, lane_f, g_idx, i1, i2, w1, w2 = _route(logits)

    picks = jnp.where(lane_f == g_idx, 1.0, 0.0)
    rows = picks.shape[0]
    tri = (lax.broadcasted_iota(jnp.int32, (rows, rows), 0)
           > lax.broadcasted_iota(jnp.int32, (rows, rows), 1))
    before = jnp.dot(jnp.where(tri, 1.0, 0.0).astype(BF16), picks.astype(BF16),
                     preferred_element_type=F32)
    count = jnp.sum(picks, axis=0, keepdims=True)
    cnt_ref[0] = count
    count8 = jnp.broadcast_to(jnp.floor((count + (SUBLANES - 1)) * (1.0 / SUBLANES)) * SUBLANES,
                              (SUBLANES, LANES))
    lane8 = lane[:SUBLANES]
    start = sum(jnp.where(lane8 >= k, pltpu.roll(count8, k, 1), 0.0) for k in range(1, N_GROUPS))
    local = jnp.sum(jnp.where(lane_f == g_idx, before + start[0:1], 0.0), axis=1, keepdims=True)
    route = jnp.where(lane == ROUTE_GROUP, g_idx, jnp.where(lane == ROUTE_LOCAL, local, jnp.where(
        lane == ROUTE_E1, i1 - N_GROUPS, jnp.where(lane == ROUTE_E2, i2 - N_GROUPS, jnp.where(
            lane == ROUTE_W1, w1, jnp.where(lane == ROUTE_W2, w2, 0.0))))))
    route_ref[...] = route

    local_row = jnp.transpose(jnp.broadcast_to(local, (rows, LANES)))[0:1, :]
    place = jnp.where(lax.broadcasted_iota(jnp.int32, (XLOC_ROWS, rows), 0).astype(F32) == local_row,
                      1.0, 0.0).astype(BF16)
    xloc_ref[0, :, :D_MODEL] = jnp.dot(place, xb, preferred_element_type=F32)
    xloc_ref[0, :, D_MODEL:] = sum(
        jnp.dot(place, piece, preferred_element_type=F32) for piece in _split3(route))


def _outproj(ctx, lat, mod4, mod_row_of_tile, norm_g, w_out_bf, w_router_bf, b_router):
    ctx_tiles = ctx[2].shape[0] // OUTPROJ_TILE
    tiles = ctx_tiles + lat[2].shape[0] // OUTPROJ_TILE
    tokens = tiles * OUTPROJ_TILE

    def mod_spec(which):
        return pl.BlockSpec((1, 1, 1, D_MODEL), lambda i: (mod_row_of_tile(i), which, 0, 0))

    ctx_tok = lambda w: pl.BlockSpec((OUTPROJ_TILE, w), lambda i: (jnp.minimum(i, ctx_tiles - 1), 0))
    lat_tok = lambda w: pl.BlockSpec((OUTPROJ_TILE, w), lambda i: (jnp.maximum(i - ctx_tiles, 0), 0))
    tok = lambda w: pl.BlockSpec((OUTPROJ_TILE, w), lambda i: (i, 0))
    full = lambda a: pl.BlockSpec(a.shape, lambda i: (0,) * a.ndim)
    widths = (CONV_W, RET_W, D_MODEL)
    return pl.pallas_call(
        functools.partial(_outproj_kernel, ctx_tiles),
        grid=(tiles,),
        in_specs=[ctx_tok(w) for w in widths] + [lat_tok(w) for w in widths] + [
            mod_spec(2), mod_spec(3), mod_spec(4),
            full(norm_g), full(w_out_bf), full(w_router_bf), full(b_router)],
        out_specs=[tok(D_MODEL),
                   pl.BlockSpec((1, XLOC_ROWS, ROW_W), lambda i: (i, 0, 0)),
                   tok(ROUTER_COLS),
                   pl.BlockSpec((1, 1, ROUTER_COLS), lambda i: (i, 0, 0))],
        out_shape=[jax.ShapeDtypeStruct((tokens, D_MODEL), F32),
                   jax.ShapeDtypeStruct((tiles, XLOC_ROWS, ROW_W), F32),
                   jax.ShapeDtypeStruct((tokens, ROUTER_COLS), F32),
                   jax.ShapeDtypeStruct((tiles, 1, ROUTER_COLS), F32)],
        compiler_params=pltpu.CompilerParams(
            dimension_semantics=("parallel",), vmem_limit_bytes=VMEM_LIMIT),
        name="outproj",
    )(*ctx, *lat, mod4, mod4, mod4, norm_g, w_out_bf, w_router_bf, b_router)


RUN_PIECES = tuple(SUBLANES << b for b in reversed(range((OUTPROJ_TILE // SUBLANES).bit_length())))


def _expert_kernel(tile_group_ref, n_used_ref, first_ref, last_ref, fill_ref,
                   run_len_ref, run_src_ref, run_dst_ref,
                   xloc_hbm, w1_ref, w3_ref, w2_ref, ys_ref, xbuf, xb, gate_tabs, sem):
    j = pl.program_id(0)
    step = pl.program_id(1)
    n_used = n_used_ref[0]

    def tile_copies(tile, act):
        slot = tile % 2
        group = tile_group_ref[tile]
        row0 = tile * GROUP_TILE

        def from_token_tile(b, carry):
            run = b * N_GROUPS + group
            lo = jnp.maximum(run_dst_ref[run], row0)
            hi = jnp.minimum(run_dst_ref[run] + run_len_ref[run], row0 + GROUP_TILE)
            n = jnp.maximum(hi - lo, 0)
            src = run_src_ref[run] + lo - run_dst_ref[run]
            dst = lo - row0
            for size in RUN_PIECES:
                done = n & (-2 * size)

                @pl.when((n & size) != 0)
                def _():
                    act(pltpu.make_async_copy(
                        xloc_hbm.at[b, pl.ds(pl.multiple_of(src + done, SUBLANES), size)],
                        xbuf.at[slot, pl.ds(pl.multiple_of(dst + done, SUBLANES), size)], sem.at[slot]))
            return carry

        lax.fori_loop(first_ref[tile], last_ref[tile] + 1, from_token_tile, 0)

    start = lambda cp: cp.start()
    wait = lambda cp: cp.wait()

    @pl.when(j < n_used)
    def _():
        @pl.when(step == 0)
        def _():
            @pl.when(j == 0)
            def _():
                xbuf[...] = jnp.zeros_like(xbuf)
                tile_copies(j, start)

            tile_copies(j, wait)
            rows_in = xbuf[j % 2]
            xb[...] = rows_in[:, :D_MODEL].astype(BF16)
            route = rows_in[:, D_MODEL:]
            lane = lax.broadcasted_iota(jnp.int32, route.shape, 1)
            for n, which in enumerate((ROUTE_E1, ROUTE_E2, ROUTE_W1, ROUTE_W2)):
                col = jnp.sum(jnp.where(lane == which, route, 0.0), axis=1, keepdims=True)
                gate_tabs[n] = jnp.broadcast_to(col, route.shape)

            @pl.when(j + 1 < n_used)
            def _():
                tile_copies(j + 1, start)

        def evaluate(rows):
            x = xb[:rows]
            total = None
            for s in range(EXPERTS_PER_STEP):
                expert = (tile_group_ref[j] * EXPERTS_PER_GROUP + step * EXPERTS_PER_STEP + s).astype(F32)
                gate = (jnp.where(gate_tabs[0, :rows] == expert, gate_tabs[2, :rows], 0.0)
                        + jnp.where(gate_tabs[1, :rows] == expert, gate_tabs[3, :rows], 0.0))
                hid = _silu(jnp.dot(x, w1_ref[0, s].astype(BF16), preferred_element_type=F32)) * jnp.dot(
                    x, w3_ref[0, s].astype(BF16), preferred_element_type=F32)
                y = jnp.dot(hid.astype(BF16), w2_ref[0, s].astype(BF16), preferred_element_type=F32)
                gated = jnp.concatenate(
                    [gate * y[:, c * LANES:(c + 1) * LANES] for c in range(D_MODEL // LANES)], axis=1)
                total = gated if total is None else total + gated

            @pl.when(step == 0)
            def _():
                ys_ref[:rows] = total

            @pl.when(step > 0)
            def _():
                ys_ref[:rows] += total

        half = GROUP_TILE // 2

        @pl.when(fill_ref[j] > half)
        def _():
            evaluate(GROUP_TILE)

        @pl.when(fill_ref[j] <= half)
        def _():
            evaluate(half)


def _experts(xloc, tile_tables, runs, slots, w1, w3, w2):
    steps = EXPERTS_PER_GROUP // EXPERTS_PER_STEP
    paired = lambda w: w.reshape((N_EXPERTS // EXPERTS_PER_STEP, EXPERTS_PER_STEP) + w.shape[1:])
    w_spec = lambda shape: pl.BlockSpec((1, EXPERTS_PER_STEP) + shape, lambda j, s, tg, nu, *_: (
        tg[jnp.minimum(j, nu[0] - 1)] * steps + jnp.where(j < nu[0], s, steps - 1), 0, 0, 0))
    grid_spec = pltpu.PrefetchScalarGridSpec(
        num_scalar_prefetch=8,
        grid=(slots // GROUP_TILE, steps),
        in_specs=[
            pl.BlockSpec(memory_space=pl.ANY),
            w_spec((D_MODEL, D_EXPERT)), w_spec((D_MODEL, D_EXPERT)), w_spec((D_EXPERT, D_MODEL)),
        ],
        out_specs=pl.BlockSpec((GROUP_TILE, D_MODEL), lambda j, s, tg, nu, *_: (jnp.minimum(j, nu[0] - 1), 0)),
        scratch_shapes=[pltpu.VMEM((2, GROUP_TILE, ROW_W), F32), pltpu.VMEM((GROUP_TILE, D_MODEL), BF16),
                        pltpu.VMEM((4, GROUP_TILE, LANES), F32), pltpu.SemaphoreType.DMA((2,))],
    )
    return pl.pallas_call(
        _expert_kernel,
        grid_spec=grid_spec,
        out_shape=jax.ShapeDtypeStruct((slots, D_MODEL), F32),
        compiler_params=pltpu.CompilerParams(
            dimension_semantics=("arbitrary", "arbitrary"), vmem_limit_bytes=EXPERT_VMEM_LIMIT),
        name="experts",
    )(*tile_tables, *runs, xloc, paired(w1), paired(w3), paired(w2))


def _combine_kernel(n_tiles, tile_base, run_len_ref, run_src_ref, run_dst_ref,
                    x1_ref, route_ref, g2_ref, fg_ref, ys_hbm, o_ref, buf, sem):
    i = pl.program_id(0)
    slot = i % 2

    def run_copies(local_tile, act):
        s = local_tile % 2
        tile = local_tile + tile_base
        for g in range(N_GROUPS):
            n = run_len_ref[tile * N_GROUPS + g]
            src = run_src_ref[tile * N_GROUPS + g]
            dst = run_dst_ref[tile * N_GROUPS + g]
            for size in RUN_PIECES:
                done = n & (-2 * size)

                @pl.when((n & size) != 0)
                def _():
                    act(pltpu.make_async_copy(
                        ys_hbm.at[pl.ds(pl.multiple_of(dst + done, SUBLANES), size)],
                        buf.at[s, pl.ds(pl.multiple_of(src + done, SUBLANES), size)], sem.at[s]))

    @pl.when(i == 0)
    def _():
        buf[...] = jnp.zeros_like(buf)
        run_copies(i, lambda cp: cp.start())

    @pl.when(i + 1 < n_tiles)
    def _():
        run_copies(i + 1, lambda cp: cp.start())

    run_copies(i, lambda cp: cp.wait())
    route = route_ref[...]
    lane = lax.broadcasted_iota(jnp.int32, route.shape, 1)
    local = jnp.sum(jnp.where(lane == ROUTE_LOCAL, route, 0.0), axis=1, keepdims=True)
    pick = jnp.where(lax.broadcasted_iota(jnp.int32, (route.shape[0], LOCAL_ROWS), 1).astype(F32) == local,
                     1.0, 0.0).astype(BF16)
    moe = sum(jnp.dot(pick, piece, preferred_element_type=F32) for piece in _split3(buf[slot])[:2])
    y = x1_ref[...] + g2_ref[0, 0] * moe
    o_ref[...] = _rms(y) * fg_ref[...]


def _combine(ys, runs, x1, route, tokens, mod4, mod_row_of_tile, final_g, tile_base):
    tiles = tokens // OUTPROJ_TILE
    tok = lambda w: pl.BlockSpec((OUTPROJ_TILE, w), lambda i, *_: (i + tile_base, 0))
    grid_spec = pltpu.PrefetchScalarGridSpec(
        num_scalar_prefetch=3,
        grid=(tiles,),
        in_specs=[
            tok(D_MODEL), tok(ROUTER_COLS),
            pl.BlockSpec((1, 1, 1, D_MODEL), lambda i, *_: (mod_row_of_tile(i + tile_base), 5, 0, 0)),
            pl.BlockSpec((1, D_MODEL), lambda i, *_: (0, 0)),
            pl.BlockSpec(memory_space=pl.ANY),
        ],
        out_specs=pl.BlockSpec((OUTPROJ_TILE, D_MODEL), lambda i, *_: (i, 0)),
        scratch_shapes=[pltpu.VMEM((2, LOCAL_ROWS, D_MODEL), F32), pltpu.SemaphoreType.DMA((2,))],
    )
    return pl.pallas_call(
        functools.partial(_combine_kernel, tiles, tile_base),
        grid_spec=grid_spec,
        out_shape=jax.ShapeDtypeStruct((tokens, D_MODEL), F32),
        compiler_params=pltpu.CompilerParams(
            dimension_semantics=("arbitrary",), vmem_limit_bytes=VMEM_LIMIT),
        name="combine",
    )(*runs, x1, route, mod4, final_g, ys)


def _routing_tables(counts):
    counts = counts.astype(jnp.int32)
    tiles = counts.shape[0]
    run_len = ((counts + SUBLANES - 1) // SUBLANES) * SUBLANES
    run_src = jnp.cumsum(run_len, axis=1) - run_len
    group_rows = jnp.sum(run_len, axis=0)
    padded = ((group_rows + GROUP_TILE - 1) // GROUP_TILE) * GROUP_TILE
    ends = jnp.cumsum(padded)
    offs = ends - padded
    run_dst = offs[None, :] + jnp.cumsum(run_len, axis=0) - run_len

    n_used = ends[-1] // GROUP_TILE
    max_rows = tiles * (OUTPROJ_TILE + N_GROUPS * (SUBLANES - 1))
    max_tiles = -(-max_rows // GROUP_TILE) + N_GROUPS
    tile_ids = jnp.minimum(jnp.arange(max_tiles, dtype=jnp.int32), n_used - 1)
    tile_group = jnp.sum(tile_ids[:, None] * GROUP_TILE >= ends[None, :], axis=1).astype(jnp.int32)
    of_group = (tile_group[:, None] == jnp.arange(N_GROUPS, dtype=jnp.int32))[:, None, :]
    start = jnp.sum(jnp.where(of_group, run_dst[None], 0), axis=-1)
    stop = start + jnp.sum(jnp.where(of_group, run_len[None], 0), axis=-1)
    row0 = (tile_ids * GROUP_TILE)[:, None]
    first = jnp.sum(stop <= row0, axis=1).astype(jnp.int32)
    last = jnp.sum(start < row0 + GROUP_TILE, axis=1).astype(jnp.int32) - 1
    group_end = jnp.sum(jnp.where(of_group[:, 0, :], (offs + group_rows)[None, :], 0), axis=-1)
    fill = jnp.clip(group_end - row0[:, 0], 0, GROUP_TILE).astype(jnp.int32)
    flat = lambda a: a.reshape(-1)
    return ((tile_group, n_used.reshape(1), first, last, fill), (flat(run_len), flat(run_src), flat(run_dst)),
            max_tiles * GROUP_TILE)


def _mixer(x, mod4, mod_row, is_grid, s_f0, s_b0, p):
    norm_mix_g, w_in_bf, conv_w, decay_rows = p
    y_conv, *qkvg = _inproj(x, mod4, mod_row, norm_mix_g, w_in_bf, conv_w, is_grid)
    per_seq = lambda a: a.reshape(x.shape[0], x.shape[1], a.shape[-1])
    ret = _retention(*map(per_seq, qkvg), decay_rows, s_f0, s_b0, emit_state=not is_grid)
    return y_conv, ret[0].reshape(-1, RET_W), ret[1:]


def kernel(x_prompt, x_sample, state_ret_fwd, state_ret_bwd, c, c_ctx, norm_mix_g, norm_ffn_g, w_ada, b_ada, w_in, conv_w, ret_decay_fwd, ret_decay_bwd, w_out, w_router_group, b_router_group, w_router_expert, b_router_expert, w_gate_e, w_up_e, w_down_e, final_norm_g):
    assert norm_mix_g.shape[0] == 1, "single-layer backbone"
    n_lat = c.shape[0]
    ctx_row = n_lat
    mod_rows = 8
    cvec = jnp.concatenate([c, c_ctx[None, :], jnp.zeros((mod_rows - n_lat - 1, D_MODEL), F32)], axis=0)
    mod = _modulation(cvec, w_ada[0], b_ada[0][None, :])
    mod4 = mod.reshape(mod_rows, 6, 1, D_MODEL)

    pad = ROUTER_COLS - N_GROUPS - N_EXPERTS
    w_router = jnp.concatenate(
        [w_router_group[0], w_router_expert[0], jnp.zeros((D_MODEL, pad), F32)], axis=1).astype(BF16)
    b_router = jnp.concatenate([b_router_group[0], b_router_expert[0], jnp.zeros((pad,), F32)])[None, :]
    decay_rows = jnp.broadcast_to(
        jnp.concatenate([ret_decay_fwd[0], ret_decay_bwd[0]])[:, None], (2 * RET_HEADS, LANES))
    p_mix = (norm_mix_g, w_in[0].astype(BF16), conv_w[0], decay_rows)
    w_out_bf = w_out[0].astype(BF16)
    final_g = final_norm_g[None, :]

    ctx_tokens = x_prompt.shape[0] * x_prompt.shape[1]
    lat_tokens = x_sample.shape[0] * x_sample.shape[1]
    ctx_tiles = ctx_tokens // OUTPROJ_TILE
    lat_tiles_per_seq = x_sample.shape[1] // OUTPROJ_TILE
    tile_mod = lambda i: jnp.where(i < ctx_tiles, ctx_row, (i - ctx_tiles) // lat_tiles_per_seq)
    flat = lambda a: a.reshape(-1, a.shape[-1])

    yc_c, yr_c, (s_f, s_b) = _mixer(x_prompt, mod4, lambda b: ctx_row, False, None, None, p_mix)
    yc_l, yr_l, _ = _mixer(x_sample, mod4, lambda b: b, True, state_ret_fwd, state_ret_bwd, p_mix)

    x1, xloc, route, cnt = _outproj((yc_c, yr_c, flat(x_prompt)), (yc_l, yr_l, flat(x_sample)), mod4, tile_mod,
                                    norm_ffn_g, w_out_bf, w_router, b_router)
    tile_tables, runs, slots = _routing_tables(cnt[:, 0, :N_GROUPS])
    ys = _experts(xloc, tile_tables, runs, slots, w_gate_e[0], w_up_e[0], w_down_e[0])
    y_prompt = _combine(ys, runs, x1, route, ctx_tokens, mod4, tile_mod, final_g, 0)
    y_sample = _combine(ys, runs, x1, route, lat_tokens, mod4, tile_mod, final_g, ctx_tiles)
    return (y_prompt.reshape(x_prompt.shape), y_sample.reshape(x_sample.shape),
            s_f.astype(x_prompt.dtype), s_b.astype(x_prompt.dtype))
```

```python
import functools

import jax
import jax.numpy as jnp
from jax import lax
from jax.experimental import pallas as pl
from jax.experimental.pallas import tpu as pltpu

F32 = jnp.float32
BF16 = jnp.bfloat16

D_MODEL = 1024
GRID_W = 64
CONV_W = 512
RET_HEADS = 4
RET_DK = 128
RET_DV = 128
RET_W = RET_HEADS * RET_DV
QK_W = RET_HEADS * RET_DK
CHUNK = 128
N_GROUPS = 4
EXPERTS_PER_GROUP = 8
N_EXPERTS = N_GROUPS * EXPERTS_PER_GROUP
D_EXPERT = 256
ROPE_BASE = 10000.0
EPS = 1e-6

LANES = 128
TOKEN_TILE = 512
OUTPROJ_TILE = 512
GROUP_TILE = 1024
EXPERTS_PER_STEP = 2
RET_UNROLL = 8
SUBLANES = 8
XLOC_ROWS = OUTPROJ_TILE + N_GROUPS * SUBLANES
LOCAL_ROWS = OUTPROJ_TILE + LANES
ROW_W = D_MODEL + LANES
ROUTE_GROUP, ROUTE_LOCAL, ROUTE_E1, ROUTE_E2, ROUTE_W1, ROUTE_W2 = range(6)
MOD_COLS = 1536
ROUTER_COLS = LANES
VMEM_LIMIT = 48 * 1024 * 1024
EXPERT_VMEM_LIMIT = 56 * 1024 * 1024


def _silu(x):
    return x * jax.nn.sigmoid(x)


def _rms(x):
    return x * lax.rsqrt(jnp.mean(x * x, axis=-1, keepdims=True) + EPS)


def _bdot(a, b):
    return jnp.dot(a.astype(BF16), b.astype(BF16), preferred_element_type=F32)


def _split3(x):
    hi = x.astype(BF16)
    rest = x - hi.astype(F32)
    mid = rest.astype(BF16)
    return hi, mid, (rest - mid.astype(F32)).astype(BF16)


def _mod_kernel(c_ref, w_ref, b_ref, o_ref):
    o_ref[...] = _bdot(_silu(c_ref[...]), w_ref[...]) + b_ref[...]


def _modulation(cvec, w_ada, b_ada):
    rows = cvec.shape[0]
    n = w_ada.shape[1]
    return pl.pallas_call(
        _mod_kernel,
        grid=(n // MOD_COLS,),
        in_specs=[
            pl.BlockSpec((rows, D_MODEL), lambda j: (0, 0)),
            pl.BlockSpec((D_MODEL, MOD_COLS), lambda j: (0, j)),
            pl.BlockSpec((1, MOD_COLS), lambda j: (0, j)),
        ],
        out_specs=pl.BlockSpec((rows, MOD_COLS), lambda j: (0, j)),
        out_shape=jax.ShapeDtypeStruct((rows, n), F32),
        compiler_params=pltpu.CompilerParams(vmem_limit_bytes=VMEM_LIMIT),
        name="modulation",
    )(cvec, w_ada, b_ada)


def _inproj_kernel(seg, is_grid, x_ref, sh_ref, sc_ref, ng_ref, w_ref, cw_ref, *rest):
    if is_grid:
        cos_ref, sa_ref, sb_ref, yc_ref, q_ref, k_ref, v_ref, g_ref = rest
    else:
        yc_ref, q_ref, k_ref, v_ref, g_ref = rest
    x = x_ref[...]
    xn = (_rms(x) * ng_ref[...]) * (1.0 + sc_ref[0, 0]) + sh_ref[0, 0]
    xb = xn.astype(BF16)

    def proj(c0, n):
        return jnp.dot(xb, w_ref[:, c0:c0 + n], preferred_element_type=F32)

    gate_b = proj(0, CONV_W)
    u = proj(CONV_W, CONV_W) * proj(2 * CONV_W, CONV_W)
    rows = u.shape[0]
    pos = lax.broadcasted_iota(jnp.int32, u.shape, 0) & (seg - 1)
    u_prev = jnp.where(pos != 0, pltpu.roll(u, 1, 0), 0.0)
    u_next = jnp.where(pos != seg - 1, pltpu.roll(u, rows - 1, 0), 0.0)
    conv = cw_ref[0:1, :] * u_prev + cw_ref[1:2, :] * u + cw_ref[2:3, :] * u_next
    yc_ref[...] = (gate_b * conv).astype(yc_ref.dtype)

    q0 = 3 * CONV_W
    q = proj(q0, QK_W)
    k = proj(q0 + QK_W, QK_W)
    if is_grid:
        cos, sa, sb = cos_ref[...], sa_ref[...], sb_ref[...]

        def rope(t):
            out = []
            for h in range(RET_HEADS):
                th = t[:, h * RET_DK:(h + 1) * RET_DK]
                out.append(th * cos + pltpu.roll(th, RET_DK - 1, 1) * sa + pltpu.roll(th, 1, 1) * sb)
            return jnp.concatenate(out, axis=1)

        q, k = rope(q), rope(k)
    q_ref[...] = q
    k_ref[...] = k
    v_ref[...] = proj(q0 + 2 * QK_W, RET_W)
    g_ref[...] = proj(q0 + 2 * QK_W + RET_W, RET_W)


def _rope_tables(length):
    pos = jnp.arange(length)
    row = (pos // GRID_W).astype(F32)
    col = (pos % GRID_W).astype(F32)
    n_pairs = RET_DK // 4
    freqs = ROPE_BASE ** (-(jnp.arange(n_pairs, dtype=F32) * 2.0 / (RET_DK // 2)))
    ang = jnp.concatenate([row[:, None] * freqs, col[:, None] * freqs], axis=-1)
    cos = jnp.repeat(jnp.cos(ang), 2, axis=-1)
    sin = jnp.repeat(jnp.sin(ang), 2, axis=-1)
    even = (jnp.arange(RET_DK) % 2) == 0
    return cos, jnp.where(even, -sin, 0.0), jnp.where(even, 0.0, sin)


def _inproj(x, mod4, mod_row, norm_g, w_in_bf, conv_w, is_grid):
    bsz, length, _ = x.shape
    seg = GRID_W if is_grid else length
    assert TOKEN_TILE % seg == 0 and (length % TOKEN_TILE == 0 or TOKEN_TILE % length == 0)
    tokens = bsz * length
    tiles_per_seq = max(length // TOKEN_TILE, 1)
    seqs_per_tile = max(TOKEN_TILE // length, 1)
    batch_of = lambda i: (i // tiles_per_seq) * seqs_per_tile

    def mod_spec(which):
        return pl.BlockSpec((1, 1, 1, D_MODEL), lambda i: (mod_row(batch_of(i)), which, 0, 0))

    def tok_spec(width):
        return pl.BlockSpec((TOKEN_TILE, width), lambda i: (i, 0))

    in_specs = [
        tok_spec(D_MODEL), mod_spec(0), mod_spec(1),
        pl.BlockSpec((1, D_MODEL), lambda i: (0, 0)),
        pl.BlockSpec(w_in_bf.shape, lambda i: (0, 0)),
        pl.BlockSpec(conv_w.shape, lambda i: (0, 0)),
    ]
    args = [x.reshape(tokens, D_MODEL), mod4, mod4, norm_g, w_in_bf, conv_w]
    if is_grid:
        assert length % TOKEN_TILE == 0
        in_specs += [pl.BlockSpec((TOKEN_TILE, RET_DK), lambda i: (i % tiles_per_seq, 0))] * 3
        args += list(_rope_tables(length))
    shp = lambda w, dt: jax.ShapeDtypeStruct((tokens, w), dt)
    return pl.pallas_call(
        functools.partial(_inproj_kernel, seg, is_grid),
        grid=(tokens // TOKEN_TILE,),
        in_specs=in_specs,
        out_specs=[tok_spec(CONV_W), tok_spec(QK_W), tok_spec(QK_W), tok_spec(RET_W), tok_spec(RET_W)],
        out_shape=[shp(CONV_W, BF16), shp(QK_W, F32), shp(QK_W, F32), shp(RET_W, F32), shp(RET_W, F32)],
        compiler_params=pltpu.CompilerParams(
            dimension_semantics=("parallel",), vmem_limit_bytes=VMEM_LIMIT),
        name="inproj_grid" if is_grid else "inproj_seq",
    )(*args)


def _ret_kernel(n_chunks, heads, has_init, emit_state, a_ref, q_ref, k_ref, v_ref, g_ref, *rest):
    rest = list(rest)
    if has_init:
        sf0_ref, sb0_ref = rest[:2]
        rest = rest[2:]
    y_ref = rest.pop(0)
    if emit_state:
        sf_out, sb_out = rest[:2]
        rest = rest[2:]
    st_f, st_b, dec = rest
    c = CHUNK
    sq = (c, c)
    head0 = pl.program_id(0) * heads

    def log_decays(hh):
        lg_f = jnp.log1p(-jnp.exp(a_ref[pl.ds(head0 + hh, 1), :]))
        lg_b = jnp.log1p(-jnp.exp(a_ref[pl.ds(head0 + hh + RET_HEADS, 1), :]))
        return lg_f, lg_b

    @pl.when(pl.program_id(1) == 0)
    def _():
        row = lax.broadcasted_iota(jnp.int32, sq, 0).astype(F32)
        col = lax.broadcasted_iota(jnp.int32, sq, 1).astype(F32)
        scale = RET_DK ** -0.5
        for hh in range(heads):
            lg_f, lg_b = log_decays(hh)
            dec[hh, 0] = scale * (
                jnp.where(row >= col, jnp.exp(jnp.where(row >= col, row - col, 0.0) * lg_f), 0.0)
                + jnp.where(col >= row, jnp.exp(jnp.where(col >= row, col - row, 0.0) * lg_b), 0.0))
            dec[hh, 1] = jnp.exp((row + 1.0) * lg_f)
            dec[hh, 2] = jnp.exp((c - row) * lg_b)
            dec[hh, 3] = scale * jnp.exp((c - 1.0 - col) * lg_f)
            dec[hh, 4] = scale * jnp.exp(col * lg_b)

    def rows(n):
        return pl.ds(pl.multiple_of(n * c, c), c) if not isinstance(n, int) else pl.ds(n * c, c)

    def cols(hh):
        return slice(hh * RET_DK, (hh + 1) * RET_DK)

    def kv_step(hh, n):
        kt = jnp.transpose(k_ref[0, rows(n), cols(hh)])
        lhs = jnp.concatenate([kt * dec[hh, 3], kt * dec[hh, 4]], axis=0)
        kv = _bdot(lhs, v_ref[0, rows(n), cols(hh)])
        st_f[hh, n] = kv[:RET_DK]
        st_b[hh, n] = kv[RET_DK:]

    def scan(hh, st, decay, order, s):
        def step(i, s):
            n = order(i)
            kv = st[hh, n]
            st[hh, n] = s
            return s * decay + kv
        if n_chunks <= RET_UNROLL:
            for i in range(n_chunks):
                s = step(i, s)
            return s
        return lax.fori_loop(0, n_chunks, step, s, unroll=RET_UNROLL)

    def out_step(hh, n):
        q = q_ref[0, rows(n), cols(hh)]
        scores = lax.dot_general(q.astype(BF16), k_ref[0, rows(n), cols(hh)].astype(BF16),
                                 (((1,), (1,)), ((), ())), preferred_element_type=F32)
        o = _bdot(scores * dec[hh, 0], v_ref[0, rows(n), cols(hh)])
        q_dec = jnp.concatenate([q * dec[hh, 1], q * dec[hh, 2]], axis=1)
        o = o + _bdot(q_dec, jnp.concatenate([st_f[hh, n], st_b[hh, n]], axis=0))
        y = _silu(g_ref[0, rows(n), cols(hh)]) * _rms(o)
        y_ref[0, rows(n), cols(hh)] = y.astype(y_ref.dtype)

    def over_chunks(step):
        if n_chunks * heads <= RET_UNROLL:
            for hh in range(heads):
                for n in range(n_chunks):
                    step(hh, n)
        else:
            for hh in range(heads):
                lax.fori_loop(0, n_chunks, lambda n, carry: (step(hh, n), carry)[1], 0, unroll=RET_UNROLL)

    over_chunks(kv_step)
    finals = []
    for hh in range(heads):
        lg_f, lg_b = log_decays(hh)
        s_f = sf0_ref[0, 0, hh] if has_init else jnp.zeros(sq, F32)
        s_b = sb0_ref[0, 0, hh] if has_init else jnp.zeros(sq, F32)
        s_f = scan(hh, st_f, jnp.exp(c * lg_f), lambda i: i, s_f)
        s_b = scan(hh, st_b, jnp.exp(c * lg_b), lambda i: n_chunks - 1 - i, s_b)
        finals.append((s_f, s_b))
    over_chunks(out_step)
    if emit_state:
        for hh, (s_f, s_b) in enumerate(finals):
            sf_out[0, 0, hh] = s_f
            sb_out[0, 0, hh] = s_b


def _retention(q, k, v, g, decay_rows, s_f0, s_b0, emit_state):
    bsz, length, _ = q.shape
    n_chunks = length // CHUNK
    has_init = s_f0 is not None
    heads = RET_HEADS if n_chunks * RET_HEADS <= RET_UNROLL else RET_HEADS // 2
    head_spec = pl.BlockSpec((1, length, heads * RET_DK), lambda h, b: (b, 0, h))
    st_spec = pl.BlockSpec((1, 1, heads, RET_DK, RET_DV), lambda h, b: (b, 0, h, 0, 0))
    in_specs = [pl.BlockSpec(decay_rows.shape, lambda h, b: (0, 0))] + [head_spec] * 4
    args = [decay_rows, q, k, v, g]
    if has_init:
        in_specs += [st_spec, st_spec]
        args += [s_f0, s_b0]
    out_specs = [head_spec]
    out_shape = [jax.ShapeDtypeStruct((bsz, length, RET_W), BF16)]
    if emit_state:
        st_shape = jax.ShapeDtypeStruct((bsz, 1, RET_HEADS, RET_DK, RET_DV), F32)
        out_specs += [st_spec, st_spec]
        out_shape += [st_shape, st_shape]
    return pl.pallas_call(
        functools.partial(_ret_kernel, n_chunks, heads, has_init, emit_state),
        grid=(RET_HEADS // heads, bsz),
        in_specs=in_specs,
        out_specs=out_specs,
        out_shape=out_shape,
        scratch_shapes=[
            pltpu.VMEM((heads, n_chunks, RET_DK, RET_DV), F32),
            pltpu.VMEM((heads, n_chunks, RET_DK, RET_DV), F32),
            pltpu.VMEM((heads, 5, CHUNK, CHUNK), F32),
        ],
        compiler_params=pltpu.CompilerParams(
            dimension_semantics=("arbitrary", "arbitrary"), vmem_limit_bytes=VMEM_LIMIT),
        name="retention_init" if has_init else "retention_zero",
    )(*args)


def _route(logits):
    lane = lax.broadcasted_iota(jnp.int32, logits.shape, 1)
    lane_f = lane.astype(F32)
    neg = -jnp.inf
    far = float(LANES)
    is_g = lane < N_GROUPS
    lg = jnp.where(is_g, logits, neg)
    g_max = jnp.max(lg, axis=1, keepdims=True)
    g_idx = jnp.min(jnp.where(lg == g_max, lane_f, far), axis=1, keepdims=True)
    p_sel = 1.0 / jnp.sum(jnp.where(is_g, jnp.exp(lg - g_max), 0.0), axis=1, keepdims=True)
    lane_group = ((lane - N_GROUPS) >> (EXPERTS_PER_GROUP.bit_length() - 1)).astype(F32)
    sel = (lane >= N_GROUPS) & (lane < N_GROUPS + N_EXPERTS) & (lane_group == g_idx)
    le = jnp.where(sel, logits, neg)
    v1 = jnp.max(le, axis=1, keepdims=True)
    i1 = jnp.min(jnp.where(le == v1, lane_f, far), axis=1, keepdims=True)
    le2 = jnp.where(lane_f == i1, neg, le)
    v2 = jnp.max(le2, axis=1, keepdims=True)
    i2 = jnp.min(jnp.where(le2 == v2, lane_f, far), axis=1, keepdims=True)
    e2 = jnp.exp(v2 - v1)
    w1 = p_sel * (1.0 / (1.0 + e2))
    w2 = p_sel * (e2 / (1.0 + e2))
    return lane, lane_f, g_idx, i1, i2, w1, w2


def _outproj_kernel(ctx_tiles, yc_c, yr_c, x_c, yc_l, yr_l, x_l, *rest):
    @pl.when(pl.program_id(0) < ctx_tiles)
    def _():
        _outproj_tile(yc_c, yr_c, x_c, *rest)

    @pl.when(pl.program_id(0) >= ctx_tiles)
    def _():
        _outproj_tile(yc_l, yr_l, x_l, *rest)


def _outproj_tile(yc_ref, yr_ref, x_ref, g1_ref, sh_ref, sc_ref, ng_ref, wo_ref, wr_ref, br_ref,
                  x1_ref, xloc_ref, route_ref, cnt_ref):
    m = (jnp.dot(yc_ref[...], wo_ref[0:CONV_W, :], preferred_element_type=F32)
         + jnp.dot(yr_ref[...], wo_ref[CONV_W:, :], preferred_element_type=F32))
    x1 = x_ref[...] + g1_ref[0, 0] * m
    x1_ref[...] = x1
    xn = (_rms(x1) * ng_ref[...]) * (1.0 + sc_ref[0, 0]) + sh_ref[0, 0]
    xb = xn.astype(BF16)
    logits = jnp.dot(xb, wr_ref[...], preferred_element_type=F32) + br_ref[...]
    lane, lane_f, g_idx, i1, i2, w1, w2 = _route(logits)

    picks = jnp.where(lane_f == g_idx, 1.0, 0.0)
    rows = picks.shape[0]
    tri = (lax.broadcasted_iota(jnp.int32, (rows, rows), 0)
           > lax.broadcasted_iota(jnp.int32, (rows, rows), 1))
    before = jnp.dot(jnp.where(tri, 1.0, 0.0).astype(BF16), picks.astype(BF16),
                     preferred_element_type=F32)
    count = jnp.sum(picks, axis=0, keepdims=True)
    cnt_ref[0] = count
    count8 = jnp.broadcast_to(jnp.floor((count + (SUBLANES - 1)) * (1.0 / SUBLANES)) * SUBLANES,
                              (SUBLANES, LANES))
    lane8 = lane[:SUBLANES]
    start = sum(jnp.where(lane8 >= k, pltpu.roll(count8, k, 1), 0.0) for k in range(1, N_GROUPS))
    local = jnp.sum(jnp.where(lane_f == g_idx, before + start[0:1], 0.0), axis=1, keepdims=True)
    route = jnp.where(lane == ROUTE_GROUP, g_idx, jnp.where(lane == ROUTE_LOCAL, local, jnp.where(
        lane == ROUTE_E1, i1 - N_GROUPS, jnp.where(lane == ROUTE_E2, i2 - N_GROUPS, jnp.where(
            lane == ROUTE_W1, w1, jnp.where(lane == ROUTE_W2, w2, 0.0))))))
    route_ref[...] = route

    local_row = jnp.transpose(jnp.broadcast_to(local, (rows, LANES)))[0:1, :]
    place = jnp.where(lax.broadcasted_iota(jnp.int32, (XLOC_ROWS, rows), 0).astype(F32) == local_row,
                      1.0, 0.0).astype(BF16)
    xloc_ref[0, :, :D_MODEL] = jnp.dot(place, xb, preferred_element_type=F32)
    xloc_ref[0, :, D_MODEL:] = sum(
        jnp.dot(place, piece, preferred_element_type=F32) for piece in _split3(route))


def _outproj(ctx, lat, mod4, mod_row_of_tile, norm_g, w_out_bf, w_router_bf, b_router):
    ctx_tiles = ctx[2].shape[0] // OUTPROJ_TILE
    tiles = ctx_tiles + lat[2].shape[0] // OUTPROJ_TILE
    tokens = tiles * OUTPROJ_TILE

    def mod_spec(which):
        return pl.BlockSpec((1, 1, 1, D_MODEL), lambda i: (mod_row_of_tile(i), which, 0, 0))

    ctx_tok = lambda w: pl.BlockSpec((OUTPROJ_TILE, w), lambda i: (jnp.minimum(i, ctx_tiles - 1), 0))
    lat_tok = lambda w: pl.BlockSpec((OUTPROJ_TILE, w), lambda i: (jnp.maximum(i - ctx_tiles, 0), 0))
    tok = lambda w: pl.BlockSpec((OUTPROJ_TILE, w), lambda i: (i, 0))
    full = lambda a: pl.BlockSpec(a.shape, lambda i: (0,) * a.ndim)
    widths = (CONV_W, RET_W, D_MODEL)
    return pl.pallas_call(
        functools.partial(_outproj_kernel, ctx_tiles),
        grid=(tiles,),
        in_specs=[ctx_tok(w) for w in widths] + [lat_tok(w) for w in widths] + [
            mod_spec(2), mod_spec(3), mod_spec(4),
            full(norm_g), full(w_out_bf), full(w_router_bf), full(b_router)],
        out_specs=[tok(D_MODEL),
                   pl.BlockSpec((1, XLOC_ROWS, ROW_W), lambda i: (i, 0, 0)),
                   tok(ROUTER_COLS),
                   pl.BlockSpec((1, 1, ROUTER_COLS), lambda i: (i, 0, 0))],
        out_shape=[jax.ShapeDtypeStruct((tokens, D_MODEL), F32),
                   jax.ShapeDtypeStruct((tiles, XLOC_ROWS, ROW_W), F32),
                   jax.ShapeDtypeStruct((tokens, ROUTER_COLS), F32),
                   jax.ShapeDtypeStruct((tiles, 1, ROUTER_COLS), F32)],
        compiler_params=pltpu.CompilerParams(
            dimension_semantics=("parallel",), vmem_limit_bytes=VMEM_LIMIT),
        name="outproj",
    )(*ctx, *lat, mod4, mod4, mod4, norm_g, w_out_bf, w_router_bf, b_router)


RUN_PIECES = tuple(SUBLANES << b for b in reversed(range((OUTPROJ_TILE // SUBLANES).bit_length())))


def _expert_kernel(tile_group_ref, n_used_ref, first_ref, last_ref, fill_ref,
                   run_len_ref, run_src_ref, run_dst_ref,
                   xloc_hbm, w1_ref, w3_ref, w2_ref, ys_ref, xbuf, xb, gate_tabs, sem):
    j = pl.program_id(0)
    step = pl.program_id(1)
    n_used = n_used_ref[0]

    def tile_copies(tile, act):
        slot = tile % 2
        group = tile_group_ref[tile]
        row0 = tile * GROUP_TILE

        def from_token_tile(b, carry):
            run = b * N_GROUPS + group
            lo = jnp.maximum(run_dst_ref[run], row0)
            hi = jnp.minimum(run_dst_ref[run] + run_len_ref[run], row0 + GROUP_TILE)
            n = jnp.maximum(hi - lo, 0)
            src = run_src_ref[run] + lo - run_dst_ref[run]
            dst = lo - row0
            for size in RUN_PIECES:
                done = n & (-2 * size)

                @pl.when((n & size) != 0)
                def _():
                    act(pltpu.make_async_copy(
                        xloc_hbm.at[b, pl.ds(pl.multiple_of(src + done, SUBLANES), size)],
                        xbuf.at[slot, pl.ds(pl.multiple_of(dst + done, SUBLANES), size)], sem.at[slot]))
            return carry

        lax.fori_loop(first_ref[tile], last_ref[tile] + 1, from_token_tile, 0)

    start = lambda cp: cp.start()
    wait = lambda cp: cp.wait()

    @pl.when(j < n_used)
    def _():
        @pl.when(step == 0)
        def _():
            @pl.when(j == 0)
            def _():
                xbuf[...] = jnp.zeros_like(xbuf)
                tile_copies(j, start)

            tile_copies(j, wait)
            rows_in = xbuf[j % 2]
            xb[...] = rows_in[:, :D_MODEL].astype(BF16)
            route = rows_in[:, D_MODEL:]
            lane = lax.broadcasted_iota(jnp.int32, route.shape, 1)
            for n, which in enumerate((ROUTE_E1, ROUTE_E2, ROUTE_W1, ROUTE_W2)):
                col = jnp.sum(jnp.where(lane == which, route, 0.0), axis=1, keepdims=True)
                gate_tabs[n] = jnp.broadcast_to(col, route.shape)

            @pl.when(j + 1 < n_used)
            def _():
                tile_copies(j + 1, start)

        def evaluate(rows):
            x = xb[:rows]
            total = None
            for s in range(EXPERTS_PER_STEP):
                expert = (tile_group_ref[j] * EXPERTS_PER_GROUP + step * EXPERTS_PER_STEP + s).astype(F32)
                gate = (jnp.where(gate_tabs[0, :rows] == expert, gate_tabs[2, :rows], 0.0)
                        + jnp.where(gate_tabs[1, :rows] == expert, gate_tabs[3, :rows], 0.0))
                hid = _silu(jnp.dot(x, w1_ref[0, s].astype(BF16), preferred_element_type=F32)) * jnp.dot(
                    x, w3_ref[0, s].astype(BF16), preferred_element_type=F32)
                y = jnp.dot(hid.astype(BF16), w2_ref[0, s].astype(BF16), preferred_element_type=F32)
                gated = jnp.concatenate(
                    [gate * y[:, c * LANES:(c + 1) * LANES] for c in range(D_MODEL // LANES)], axis=1)
                total = gated if total is None else total + gated

            @pl.when(step == 0)
            def _():
                ys_ref[:rows] = total

            @pl.when(step > 0)
            def _():
                ys_ref[:rows] += total

        half = GROUP_TILE // 2

        @pl.when(fill_ref[j] > half)
        def _():
            evaluate(GROUP_TILE)

        @pl.when(fill_ref[j] <= half)
        def _():
            evaluate(half)


def _experts(xloc, tile_tables, runs, slots, w1, w3, w2):
    steps = EXPERTS_PER_GROUP // EXPERTS_PER_STEP
    paired = lambda w: w.reshape((N_EXPERTS // EXPERTS_PER_STEP, EXPERTS_PER_STEP) + w.shape[1:])
    w_spec = lambda shape: pl.BlockSpec((1, EXPERTS_PER_STEP) + shape, lambda j, s, tg, nu, *_: (
        tg[jnp.minimum(j, nu[0] - 1)] * steps + jnp.where(j < nu[0], s, steps - 1), 0, 0, 0))
    grid_spec = pltpu.PrefetchScalarGridSpec(
        num_scalar_prefetch=8,
        grid=(slots // GROUP_TILE, steps),
        in_specs=[
            pl.BlockSpec(memory_space=pl.ANY),
            w_spec((D_MODEL, D_EXPERT)), w_spec((D_MODEL, D_EXPERT)), w_spec((D_EXPERT, D_MODEL)),
        ],
        out_specs=pl.BlockSpec((GROUP_TILE, D_MODEL), lambda j, s, tg, nu, *_: (jnp.minimum(j, nu[0] - 1), 0)),
        scratch_shapes=[pltpu.VMEM((2, GROUP_TILE, ROW_W), F32), pltpu.VMEM((GROUP_TILE, D_MODEL), BF16),
                        pltpu.VMEM((4, GROUP_TILE, LANES), F32), pltpu.SemaphoreType.DMA((2,))],
    )
    return pl.pallas_call(
        _expert_kernel,
        grid_spec=grid_spec,
        out_shape=jax.ShapeDtypeStruct((slots, D_MODEL), F32),
        compiler_params=pltpu.CompilerParams(
            dimension_semantics=("arbitrary", "arbitrary"), vmem_limit_bytes=EXPERT_VMEM_LIMIT),
        name="experts",
    )(*tile_tables, *runs, xloc, paired(w1), paired(w3), paired(w2))


def _combine_kernel(n_tiles, tile_base, run_len_ref, run_src_ref, run_dst_ref,
                    x1_ref, route_ref, g2_ref, fg_ref, ys_hbm, o_ref, buf, sem):
    i = pl.program_id(0)
    slot = i % 2

    def run_copies(local_tile, act):
        s = local_tile % 2
        tile = local_tile + tile_base
        for g in range(N_GROUPS):
            n = run_len_ref[tile * N_GROUPS + g]
            src = run_src_ref[tile * N_GROUPS + g]
            dst = run_dst_ref[tile * N_GROUPS + g]
            for size in RUN_PIECES:
                done = n & (-2 * size)

                @pl.when((n & size) != 0)
                def _():
                    act(pltpu.make_async_copy(
                        ys_hbm.at[pl.ds(pl.multiple_of(dst + done, SUBLANES), size)],
                        buf.at[s, pl.ds(pl.multiple_of(src + done, SUBLANES), size)], sem.at[s]))

    @pl.when(i == 0)
    def _():
        buf[...] = jnp.zeros_like(buf)
        run_copies(i, lambda cp: cp.start())

    @pl.when(i + 1 < n_tiles)
    def _():
        run_copies(i + 1, lambda cp: cp.start())

    run_copies(i, lambda cp: cp.wait())
    route = route_ref[...]
    lane = lax.broadcasted_iota(jnp.int32, route.shape, 1)
    local = jnp.sum(jnp.where(lane == ROUTE_LOCAL, route, 0.0), axis=1, keepdims=True)
    pick = jnp.where(lax.broadcasted_iota(jnp.int32, (route.shape[0], LOCAL_ROWS), 1).astype(F32) == local,
                     1.0, 0.0).astype(BF16)
    moe = sum(jnp.dot(pick, piece, preferred_element_type=F32) for piece in _split3(buf[slot])[:2])
    y = x1_ref[...] + g2_ref[0, 0] * moe
    o_ref[...] = _rms(y) * fg_ref[...]


def _combine(ys, runs, x1, route, tokens, mod4, mod_row_of_tile, final_g, tile_base):
    tiles = tokens // OUTPROJ_TILE
    tok = lambda w: pl.BlockSpec((OUTPROJ_TILE, w), lambda i, *_: (i + tile_base, 0))
    grid_spec = pltpu.PrefetchScalarGridSpec(
        num_scalar_prefetch=3,
        grid=(tiles,),
        in_specs=[
            tok(D_MODEL), tok(ROUTER_COLS),
            pl.BlockSpec((1, 1, 1, D_MODEL), lambda i, *_: (mod_row_of_tile(i + tile_base), 5, 0, 0)),
            pl.BlockSpec((1, D_MODEL), lambda i, *_: (0, 0)),
            pl.BlockSpec(memory_space=pl.ANY),
        ],
        out_specs=pl.BlockSpec((OUTPROJ_TILE, D_MODEL), lambda i, *_: (i, 0)),
        scratch_shapes=[pltpu.VMEM((2, LOCAL_ROWS, D_MODEL), F32), pltpu.SemaphoreType.DMA((2,))],
    )
    return pl.pallas_call(
        functools.partial(_combine_kernel, tiles, tile_base),
        grid_spec=grid_spec,
        out_shape=jax.ShapeDtypeStruct((tokens, D_MODEL), F32),
        compiler_params=pltpu.CompilerParams(
            dimension_semantics=("arbitrary",), vmem_limit_bytes=VMEM_LIMIT),
        name="combine",
    )(*runs, x1, route, mod4, final_g, ys)


def _routing_tables(counts):
    counts = counts.astype(jnp.int32)
    tiles = counts.shape[0]
    run_len = ((counts + SUBLANES - 1) // SUBLANES) * SUBLANES
    run_src = jnp.cumsum(run_len, axis=1) - run_len
    group_rows = jnp.sum(run_len, axis=0)
    padded = ((group_rows + GROUP_TILE - 1) // GROUP_TILE) * GROUP_TILE
    ends = jnp.cumsum(padded)
    offs = ends - padded
    run_dst = offs[None, :] + jnp.cumsum(run_len, axis=0) - run_len

    n_used = ends[-1] // GROUP_TILE
    max_rows = tiles * (OUTPROJ_TILE + N_GROUPS * (SUBLANES - 1))
    max_tiles = -(-max_rows // GROUP_TILE) + N_GROUPS
    tile_ids = jnp.minimum(jnp.arange(max_tiles, dtype=jnp.int32), n_used - 1)
    tile_group = jnp.sum(tile_ids[:, None] * GROUP_TILE >= ends[None, :], axis=1).astype(jnp.int32)
    of_group = (tile_group[:, None] == jnp.arange(N_GROUPS, dtype=jnp.int32))[:, None, :]
    start = jnp.sum(jnp.where(of_group, run_dst[None], 0), axis=-1)
    stop = start + jnp.sum(jnp.where(of_group, run_len[None], 0), axis=-1)
    row0 = (tile_ids * GROUP_TILE)[:, None]
    first = jnp.sum(stop <= row0, axis=1).astype(jnp.int32)
    last = jnp.sum(start < row0 + GROUP_TILE, axis=1).astype(jnp.int32) - 1
    group_end = jnp.sum(jnp.where(of_group[:, 0, :], (offs + group_rows)[None, :], 0), axis=-1)
    fill = jnp.clip(group_end - row0[:, 0], 0, GROUP_TILE).astype(jnp.int32)
    flat = lambda a: a.reshape(-1)
    return ((tile_group, n_used.reshape(1), first, last, fill), (flat(run_len), flat(run_src), flat(run_dst)),
            max_tiles * GROUP_TILE)


def _mixer(x, mod4, mod_row, is_grid, s_f0, s_b0, p):
    norm_mix_g, w_in_bf, conv_w, decay_rows = p
    y_conv, *qkvg = _inproj(x, mod4, mod_row, norm_mix_g, w_in_bf, conv_w, is_grid)
    per_seq = lambda a: a.reshape(x.shape[0], x.shape[1], a.shape[-1])
    ret = _retention(*map(per_seq, qkvg), decay_rows, s_f0, s_b0, emit_state=not is_grid)
    return y_conv, ret[0].reshape(-1, RET_W), ret[1:]


def kernel(x_prompt, x_sample, state_ret_fwd, state_ret_bwd, c, c_ctx, norm_mix_g, norm_ffn_g, w_ada, b_ada, w_in, conv_w, ret_decay_fwd, ret_decay_bwd, w_out, w_router_group, b_router_group, w_router_expert, b_router_expert, w_gate_e, w_up_e, w_down_e, final_norm_g):
    assert norm_mix_g.shape[0] == 1, "single-layer backbone"
    n_lat = c.shape[0]
    ctx_row = n_lat
    mod_rows = 8
    cvec = jnp.concatenate([c, c_ctx[None, :], jnp.zeros((mod_rows - n_lat - 1, D_MODEL), F32)], axis=0)
    mod = _modulation(cvec, w_ada[0], b_ada[0][None, :])
    mod4 = mod.reshape(mod_rows, 6, 1, D_MODEL)

    pad = ROUTER_COLS - N_GROUPS - N_EXPERTS
    w_router = jnp.concatenate(
        [w_router_group[0], w_router_expert[0], jnp.zeros((D_MODEL, pad), F32)], axis=1).astype(BF16)
    b_router = jnp.concatenate([b_router_group[0], b_router_expert[0], jnp.zeros((pad,), F32)])[None, :]
    decay_rows = jnp.broadcast_to(
        jnp.concatenate([ret_decay_fwd[0], ret_decay_bwd[0]])[:, None], (2 * RET_HEADS, LANES))
    p_mix = (norm_mix_g, w_in[0].astype(BF16), conv_w[0], decay_rows)
    w_out_bf = w_out[0].astype(BF16)
    final_g = final_norm_g[None, :]

    ctx_tokens = x_prompt.shape[0] * x_prompt.shape[1]
    lat_tokens = x_sample.shape[0] * x_sample.shape[1]
    ctx_tiles = ctx_tokens // OUTPROJ_TILE
    lat_tiles_per_seq = x_sample.shape[1] // OUTPROJ_TILE
    tile_mod = lambda i: jnp.where(i < ctx_tiles, ctx_row, (i - ctx_tiles) // lat_tiles_per_seq)
    flat = lambda a: a.reshape(-1, a.shape[-1])

    yc_c, yr_c, (s_f, s_b) = _mixer(x_prompt, mod4, lambda b: ctx_row, False, None, None, p_mix)
    yc_l, yr_l, _ = _mixer(x_sample, mod4, lambda b: b, True, state_ret_fwd, state_ret_bwd, p_mix)

    x1, xloc, route, cnt = _outproj((yc_c, yr_c, flat(x_prompt)), (yc_l, yr_l, flat(x_sample)), mod4, tile_mod,
                                    norm_ffn_g, w_out_bf, w_router, b_router)
    tile_tables, runs, slots = _routing_tables(cnt[:, 0, :N_GROUPS])
    ys = _experts(xloc, tile_tables, runs, slots, w_gate_e[0], w_up_e[0], w_down_e[0])
    y_prompt = _combine(ys, runs, x1, route, ctx_tokens, mod4, tile_mod, final_g, 0)
    y_sample = _combine(ys, runs, x1, route, lat_tokens, mod4, tile_mod, final_g, ctx_tiles)
    return (y_prompt.reshape(x_prompt.shape), y_sample.reshape(x_sample.shape),
            s_f.astype(x_prompt.dtype), s_b.astype(x_prompt.dtype))
```

```python
import functools

import jax
import jax.numpy as jnp
from jax import lax
from jax.experimental import pallas as pl
from jax.experimental.pallas import tpu as pltpu

F32 = jnp.float32
BF16 = jnp.bfloat16

D_MODEL = 1024
GRID_W = 64
CONV_W = 512
RET_HEADS = 4
RET_DK = 128
RET_DV = 128
RET_W = RET_HEADS * RET_DV
QK_W = RET_HEADS * RET_DK
CHUNK = 128
N_GROUPS = 4
EXPERTS_PER_GROUP = 8
N_EXPERTS = N_GROUPS * EXPERTS_PER_GROUP
D_EXPERT = 256
ROPE_BASE = 10000.0
EPS = 1e-6

LANES = 128
TOKEN_TILE = 512
OUTPROJ_TILE = 512
GROUP_TILE = 1024
EXPERTS_PER_STEP = 2
RET_UNROLL = 8
SUBLANES = 8
XLOC_ROWS = OUTPROJ_TILE + N_GROUPS * SUBLANES
LOCAL_ROWS = OUTPROJ_TILE + LANES
ROW_W = D_MODEL + LANES
ROUTE_GROUP, ROUTE_LOCAL, ROUTE_E1, ROUTE_E2, ROUTE_W1, ROUTE_W2 = range(6)
MOD_COLS = 1536
ROUTER_COLS = LANES
VMEM_LIMIT = 48 * 1024 * 1024
EXPERT_VMEM_LIMIT = 56 * 1024 * 1024


def _silu(x):
    return x * jax.nn.sigmoid(x)


def _rms(x):
    return x * lax.rsqrt(jnp.mean(x * x, axis=-1, keepdims=True) + EPS)


def _bdot(a, b):
    return jnp.dot(a.astype(BF16), b.astype(BF16), preferred_element_type=F32)


def _split3(x):
    hi = x.astype(BF16)
    rest = x - hi.astype(F32)
    mid = rest.astype(BF16)
    return hi, mid, (rest - mid.astype(F32)).astype(BF16)


def _mod_kernel(c_ref, w_ref, b_ref, o_ref):
    o_ref[...] = _bdot(_silu(c_ref[...]), w_ref[...]) + b_ref[...]


def _modulation(cvec, w_ada, b_ada):
    rows = cvec.shape[0]
    n = w_ada.shape[1]
    return pl.pallas_call(
        _mod_kernel,
        grid=(n // MOD_COLS,),
        in_specs=[
            pl.BlockSpec((rows, D_MODEL), lambda j: (0, 0)),
            pl.BlockSpec((D_MODEL, MOD_COLS), lambda j: (0, j)),
            pl.BlockSpec((1, MOD_COLS), lambda j: (0, j)),
        ],
        out_specs=pl.BlockSpec((rows, MOD_COLS), lambda j: (0, j)),
        out_shape=jax.ShapeDtypeStruct((rows, n), F32),
        compiler_params=pltpu.CompilerParams(vmem_limit_bytes=VMEM_LIMIT),
        name="modulation",
    )(cvec, w_ada, b_ada)


def _inproj_kernel(seg, is_grid, x_ref, sh_ref, sc_ref, ng_ref, w_ref, cw_ref, *rest):
    if is_grid:
        cos_ref, sa_ref, sb_ref, yc_ref, q_ref, k_ref, v_ref, g_ref = rest
    else:
        yc_ref, q_ref, k_ref, v_ref, g_ref = rest
    x = x_ref[...]
    xn = (_rms(x) * ng_ref[...]) * (1.0 + sc_ref[0, 0]) + sh_ref[0, 0]
    xb = xn.astype(BF16)

    def proj(c0, n):
        return jnp.dot(xb, w_ref[:, c0:c0 + n], preferred_element_type=F32)

    gate_b = proj(0, CONV_W)
    u = proj(CONV_W, CONV_W) * proj(2 * CONV_W, CONV_W)
    rows = u.shape[0]
    pos = lax.broadcasted_iota(jnp.int32, u.shape, 0) & (seg - 1)
    u_prev = jnp.where(pos != 0, pltpu.roll(u, 1, 0), 0.0)
    u_next = jnp.where(pos != seg - 1, pltpu.roll(u, rows - 1, 0), 0.0)
    conv = cw_ref[0:1, :] * u_prev + cw_ref[1:2, :] * u + cw_ref[2:3, :] * u_next
    yc_ref[...] = (gate_b * conv).astype(yc_ref.dtype)

    q0 = 3 * CONV_W
    q = proj(q0, QK_W)
    k = proj(q0 + QK_W, QK_W)
    if is_grid:
        cos, sa, sb = cos_ref[...], sa_ref[...], sb_ref[...]

        def rope(t):
            out = []
            for h in range(RET_HEADS):
                th = t[:, h * RET_DK:(h + 1) * RET_DK]
                out.append(th * cos + pltpu.roll(th, RET_DK - 1, 1) * sa + pltpu.roll(th, 1, 1) * sb)
            return jnp.concatenate(out, axis=1)

        q, k = rope(q), rope(k)
    q_ref[...] = q
    k_ref[...] = k
    v_ref[...] = proj(q0 + 2 * QK_W, RET_W)
    g_ref[...] = proj(q0 + 2 * QK_W + RET_W, RET_W)


def _rope_tables(length):
    pos = jnp.arange(length)
    row = (pos // GRID_W).astype(F32)
    col = (pos % GRID_W).astype(F32)
    n_pairs = RET_DK // 4
    freqs = ROPE_BASE ** (-(jnp.arange(n_pairs, dtype=F32) * 2.0 / (RET_DK // 2)))
    ang = jnp.concatenate([row[:, None] * freqs, col[:, None] * freqs], axis=-1)
    cos = jnp.repeat(jnp.cos(ang), 2, axis=-1)
    sin = jnp.repeat(jnp.sin(ang), 2, axis=-1)
    even = (jnp.arange(RET_DK) % 2) == 0
    return cos, jnp.where(even, -sin, 0.0), jnp.where(even, 0.0, sin)


def _inproj(x, mod4, mod_row, norm_g, w_in_bf, conv_w, is_grid):
    bsz, length, _ = x.shape
    seg = GRID_W if is_grid else length
    assert TOKEN_TILE % seg == 0 and (length % TOKEN_TILE == 0 or TOKEN_TILE % length == 0)
    tokens = bsz * length
    tiles_per_seq = max(length // TOKEN_TILE, 1)
    seqs_per_tile = max(TOKEN_TILE // length, 1)
    batch_of = lambda i: (i // tiles_per_seq) * seqs_per_tile

    def mod_spec(which):
        return pl.BlockSpec((1, 1, 1, D_MODEL), lambda i: (mod_row(batch_of(i)), which, 0, 0))

    def tok_spec(width):
        return pl.BlockSpec((TOKEN_TILE, width), lambda i: (i, 0))

    in_specs = [
        tok_spec(D_MODEL), mod_spec(0), mod_spec(1),
        pl.BlockSpec((1, D_MODEL), lambda i: (0, 0)),
        pl.BlockSpec(w_in_bf.shape, lambda i: (0, 0)),
        pl.BlockSpec(conv_w.shape, lambda i: (0, 0)),
    ]
    args = [x.reshape(tokens, D_MODEL), mod4, mod4, norm_g, w_in_bf, conv_w]
    if is_grid:
        assert length % TOKEN_TILE == 0
        in_specs += [pl.BlockSpec((TOKEN_TILE, RET_DK), lambda i: (i % tiles_per_seq, 0))] * 3
        args += list(_rope_tables(length))
    shp = lambda w, dt: jax.ShapeDtypeStruct((tokens, w), dt)
    return pl.pallas_call(
        functools.partial(_inproj_kernel, seg, is_grid),
        grid=(tokens // TOKEN_TILE,),
        in_specs=in_specs,
        out_specs=[tok_spec(CONV_W), tok_spec(QK_W), tok_spec(QK_W), tok_spec(RET_W), tok_spec(RET_W)],
        out_shape=[shp(CONV_W, BF16), shp(QK_W, F32), shp(QK_W, F32), shp(RET_W, F32), shp(RET_W, F32)],
        compiler_params=pltpu.CompilerParams(
            dimension_semantics=("parallel",), vmem_limit_bytes=VMEM_LIMIT),
        name="inproj_grid" if is_grid else "inproj_seq",
    )(*args)


def _ret_kernel(n_chunks, heads, has_init, emit_state, a_ref, q_ref, k_ref, v_ref, g_ref, *rest):
    rest = list(rest)
    if has_init:
        sf0_ref, sb0_ref = rest[:2]
        rest = rest[2:]
    y_ref = rest.pop(0)
    if emit_state:
        sf_out, sb_out = rest[:2]
        rest = rest[2:]
    st_f, st_b, dec = rest
    c = CHUNK
    sq = (c, c)
    head0 = pl.program_id(0) * heads

    def log_decays(hh):
        lg_f = jnp.log1p(-jnp.exp(a_ref[pl.ds(head0 + hh, 1), :]))
        lg_b = jnp.log1p(-jnp.exp(a_ref[pl.ds(head0 + hh + RET_HEADS, 1), :]))
        return lg_f, lg_b

    @pl.when(pl.program_id(1) == 0)
    def _():
        row = lax.broadcasted_iota(jnp.int32, sq, 0).astype(F32)
        col = lax.broadcasted_iota(jnp.int32, sq, 1).astype(F32)
        scale = RET_DK ** -0.5
        for hh in range(heads):
            lg_f, lg_b = log_decays(hh)
            dec[hh, 0] = scale * (
                jnp.where(row >= col, jnp.exp(jnp.where(row >= col, row - col, 0.0) * lg_f), 0.0)
                + jnp.where(col >= row, jnp.exp(jnp.where(col >= row, col - row, 0.0) * lg_b), 0.0))
            dec[hh, 1] = jnp.exp((row + 1.0) * lg_f)
            dec[hh, 2] = jnp.exp((c - row) * lg_b)
            dec[hh, 3] = scale * jnp.exp((c - 1.0 - col) * lg_f)
            dec[hh, 4] = scale * jnp.exp(col * lg_b)

    def rows(n):
        return pl.ds(pl.multiple_of(n * c, c), c) if not isinstance(n, int) else pl.ds(n * c, c)

    def cols(hh):
        return slice(hh * RET_DK, (hh + 1) * RET_DK)

    def kv_step(hh, n):
        kt = jnp.transpose(k_ref[0, rows(n), cols(hh)])
        lhs = jnp.concatenate([kt * dec[hh, 3], kt * dec[hh, 4]], axis=0)
        kv = _bdot(lhs, v_ref[0, rows(n), cols(hh)])
        st_f[hh, n] = kv[:RET_DK]
        st_b[hh, n] = kv[RET_DK:]

    def scan(hh, st, decay, order, s):
        def step(i, s):
            n = order(i)
            kv = st[hh, n]
            st[hh, n] = s
            return s * decay + kv
        if n_chunks <= RET_UNROLL:
            for i in range(n_chunks):
                s = step(i, s)
            return s
        return lax.fori_loop(0, n_chunks, step, s, unroll=RET_UNROLL)

    def out_step(hh, n):
        q = q_ref[0, rows(n), cols(hh)]
        scores = lax.dot_general(q.astype(BF16), k_ref[0, rows(n), cols(hh)].astype(BF16),
                                 (((1,), (1,)), ((), ())), preferred_element_type=F32)
        o = _bdot(scores * dec[hh, 0], v_ref[0, rows(n), cols(hh)])
        q_dec = jnp.concatenate([q * dec[hh, 1], q * dec[hh, 2]], axis=1)
        o = o + _bdot(q_dec, jnp.concatenate([st_f[hh, n], st_b[hh, n]], axis=0))
        y = _silu(g_ref[0, rows(n), cols(hh)]) * _rms(o)
        y_ref[0, rows(n), cols(hh)] = y.astype(y_ref.dtype)

    def over_chunks(step):
        if n_chunks * heads <= RET_UNROLL:
            for hh in range(heads):
                for n in range(n_chunks):
                    step(hh, n)
        else:
            for hh in range(heads):
                lax.fori_loop(0, n_chunks, lambda n, carry: (step(hh, n), carry)[1], 0, unroll=RET_UNROLL)

    over_chunks(kv_step)
    finals = []
    for hh in range(heads):
        lg_f, lg_b = log_decays(hh)
        s_f = sf0_ref[0, 0, hh] if has_init else jnp.zeros(sq, F32)
        s_b = sb0_ref[0, 0, hh] if has_init else jnp.zeros(sq, F32)
        s_f = scan(hh, st_f, jnp.exp(c * lg_f), lambda i: i, s_f)
        s_b = scan(hh, st_b, jnp.exp(c * lg_b), lambda i: n_chunks - 1 - i, s_b)
        finals.append((s_f, s_b))
    over_chunks(out_step)
    if emit_state:
        for hh, (s_f, s_b) in enumerate(finals):
            sf_out[0, 0, hh] = s_f
            sb_out[0, 0, hh] = s_b


def _retention(q, k, v, g, decay_rows, s_f0, s_b0, emit_state):
    bsz, length, _ = q.shape
    n_chunks = length // CHUNK
    has_init = s_f0 is not None
    heads = RET_HEADS if n_chunks * RET_HEADS <= RET_UNROLL else 1
    head_spec = pl.BlockSpec((1, length, heads * RET_DK), lambda h, b: (b, 0, h))
    st_spec = pl.BlockSpec((1, 1, heads, RET_DK, RET_DV), lambda h, b: (b, 0, h, 0, 0))
    in_specs = [pl.BlockSpec(decay_rows.shape, lambda h, b: (0, 0))] + [head_spec] * 4
    args = [decay_rows, q, k, v, g]
    if has_init:
        in_specs += [st_spec, st_spec]
        args += [s_f0, s_b0]
    out_specs = [head_spec]
    out_shape = [jax.ShapeDtypeStruct((bsz, length, RET_W), BF16)]
    if emit_state:
        st_shape = jax.ShapeDtypeStruct((bsz, 1, RET_HEADS, RET_DK, RET_DV), F32)
        out_specs += [st_spec, st_spec]
        out_shape += [st_shape, st_shape]
    return pl.pallas_call(
        functools.partial(_ret_kernel, n_chunks, heads, has_init, emit_state),
        grid=(RET_HEADS // heads, bsz),
        in_specs=in_specs,
        out_specs=out_specs,
        out_shape=out_shape,
        scratch_shapes=[
            pltpu.VMEM((heads, n_chunks, RET_DK, RET_DV), F32),
            pltpu.VMEM((heads, n_chunks, RET_DK, RET_DV), F32),
            pltpu.VMEM((heads, 5, CHUNK, CHUNK), F32),
        ],
        compiler_params=pltpu.CompilerParams(
            dimension_semantics=("arbitrary", "arbitrary"), vmem_limit_bytes=VMEM_LIMIT),
        name="retention_init" if has_init else "retention_zero",
    )(*args)


def _route(logits):
    lane = lax.broadcasted_iota(jnp.int32, logits.shape, 1)
    lane_f = lane.astype(F32)
    neg = -jnp.inf
    far = float(LANES)
    is_g = lane < N_GROUPS
    lg = jnp.where(is_g, logits, neg)
    g_max = jnp.max(lg, axis=1, keepdims=True)
    g_idx = jnp.min(jnp.where(lg == g_max, lane_f, far), axis=1, keepdims=True)
    p_sel = 1.0 / jnp.sum(jnp.where(is_g, jnp.exp(lg - g_max), 0.0), axis=1, keepdims=True)
    lane_group = ((lane - N_GROUPS) >> (EXPERTS_PER_GROUP.bit_length() - 1)).astype(F32)
    sel = (lane >= N_GROUPS) & (lane < N_GROUPS + N_EXPERTS) & (lane_group == g_idx)
    le = jnp.where(sel, logits, neg)
    v1 = jnp.max(le, axis=1, keepdims=True)
    i1 = jnp.min(jnp.where(le == v1, lane_f, far), axis=1, keepdims=True)
    le2 = jnp.where(lane_f == i1, neg, le)
    v2 = jnp.max(le2, axis=1, keepdims=True)
    i2 = jnp.min(jnp.where(le2 == v2, lane_f, far), axis=1, keepdims=True)
    e2 = jnp.exp(v2 - v1)
    w1 = p_sel * (1.0 / (1.0 + e2))
    w2 = p_sel * (e2 / (1.0 + e2))
    return lane, lane_f, g_idx, i1, i2, w1, w2


def _outproj_kernel(ctx_tiles, yc_c, yr_c, x_c, yc_l, yr_l, x_l, *rest):
    @pl.when(pl.program_id(0) < ctx_tiles)
    def _():
        _outproj_tile(yc_c, yr_c, x_c, *rest)

    @pl.when(pl.program_id(0) >= ctx_tiles)
    def _():
        _outproj_tile(yc_l, yr_l, x_l, *rest)


def _outproj_tile(yc_ref, yr_ref, x_ref, g1_ref, sh_ref, sc_ref, ng_ref, wo_ref, wr_ref, br_ref,
                  x1_ref, xloc_ref, route_ref, cnt_ref):
    m = (jnp.dot(yc_ref[...], wo_ref[0:CONV_W, :], preferred_element_type=F32)
         + jnp.dot(yr_ref[...], wo_ref[CONV_W:, :], preferred_element_type=F32))
    x1 = x_ref[...] + g1_ref[0, 0] * m
    x1_ref[...] = x1
    xn = (_rms(x1) * ng_ref[...]) * (1.0 + sc_ref[0, 0]) + sh_ref[0, 0]
    xb = xn.astype(BF16)
    logits = jnp.dot(xb, wr_ref[...], preferred_element_type=F32) + br_ref[...]
    lane, lane_f, g_idx, i1, i2, w1, w2 = _route(logits)

    picks = jnp.where(lane_f == g_idx, 1.0, 0.0)
    rows = picks.shape[0]
    tri = (lax.broadcasted_iota(jnp.int32, (rows, rows), 0)
           > lax.broadcasted_iota(jnp.int32, (rows, rows), 1))
    before = jnp.dot(jnp.where(tri, 1.0, 0.0).astype(BF16), picks.astype(BF16),
                     preferred_element_type=F32)
    count = jnp.sum(picks, axis=0, keepdims=True)
    cnt_ref[0] = count
    count8 = jnp.broadcast_to(jnp.floor((count + (SUBLANES - 1)) * (1.0 / SUBLANES)) * SUBLANES,
                              (SUBLANES, LANES))
    lane8 = lane[:SUBLANES]
    start = sum(jnp.where(lane8 >= k, pltpu.roll(count8, k, 1), 0.0) for k in range(1, N_GROUPS))
    local = jnp.sum(jnp.where(lane_f == g_idx, before + start[0:1], 0.0), axis=1, keepdims=True)
    route = jnp.where(lane == ROUTE_GROUP, g_idx, jnp.where(lane == ROUTE_LOCAL, local, jnp.where(
        lane == ROUTE_E1, i1 - N_GROUPS, jnp.where(lane == ROUTE_E2, i2 - N_GROUPS, jnp.where(
            lane == ROUTE_W1, w1, jnp.where(lane == ROUTE_W2, w2, 0.0))))))
    route_ref[...] = route

    local_row = jnp.transpose(jnp.broadcast_to(local, (rows, LANES)))[0:1, :]
    place = jnp.where(lax.broadcasted_iota(jnp.int32, (XLOC_ROWS, rows), 0).astype(F32) == local_row,
                      1.0, 0.0).astype(BF16)
    xloc_ref[0, :, :D_MODEL] = jnp.dot(place, xb, preferred_element_type=F32)
    xloc_ref[0, :, D_MODEL:] = sum(
        jnp.dot(place, piece, preferred_element_type=F32) for piece in _split3(route))


def _outproj(ctx, lat, mod4, mod_row_of_tile, norm_g, w_out_bf, w_router_bf, b_router):
    ctx_tiles = ctx[2].shape[0] // OUTPROJ_TILE
    tiles = ctx_tiles + lat[2].shape[0] // OUTPROJ_TILE
    tokens = tiles * OUTPROJ_TILE

    def mod_spec(which):
        return pl.BlockSpec((1, 1, 1, D_MODEL), lambda i: (mod_row_of_tile(i), which, 0, 0))

    ctx_tok = lambda w: pl.BlockSpec((OUTPROJ_TILE, w), lambda i: (jnp.minimum(i, ctx_tiles - 1), 0))
    lat_tok = lambda w: pl.BlockSpec((OUTPROJ_TILE, w), lambda i: (jnp.maximum(i - ctx_tiles, 0), 0))
    tok = lambda w: pl.BlockSpec((OUTPROJ_TILE, w), lambda i: (i, 0))
    full = lambda a: pl.BlockSpec(a.shape, lambda i: (0,) * a.ndim)
    widths = (CONV_W, RET_W, D_MODEL)
    return pl.pallas_call(
        functools.partial(_outproj_kernel, ctx_tiles),
        grid=(tiles,),
        in_specs=[ctx_tok(w) for w in widths] + [lat_tok(w) for w in widths] + [
            mod_spec(2), mod_spec(3), mod_spec(4),
            full(norm_g), full(w_out_bf), full(w_router_bf), full(b_router)],
        out_specs=[tok(D_MODEL),
                   pl.BlockSpec((1, XLOC_ROWS, ROW_W), lambda i: (i, 0, 0)),
                   tok(ROUTER_COLS),
                   pl.BlockSpec((1, 1, ROUTER_COLS), lambda i: (i, 0, 0))],
        out_shape=[jax.ShapeDtypeStruct((tokens, D_MODEL), F32),
                   jax.ShapeDtypeStruct((tiles, XLOC_ROWS, ROW_W), F32),
                   jax.ShapeDtypeStruct((tokens, ROUTER_COLS), F32),
                   jax.ShapeDtypeStruct((tiles, 1, ROUTER_COLS), F32)],
        compiler_params=pltpu.CompilerParams(
            dimension_semantics=("parallel",), vmem_limit_bytes=VMEM_LIMIT),
        name="outproj",
    )(*ctx, *lat, mod4, mod4, mod4, norm_g, w_out_bf, w_router_bf, b_router)


RUN_PIECES = tuple(SUBLANES << b for b in reversed(range((OUTPROJ_TILE // SUBLANES).bit_length())))


def _expert_kernel(tile_group_ref, n_used_ref, first_ref, last_ref, fill_ref,
                   run_len_ref, run_src_ref, run_dst_ref,
                   xloc_hbm, w1_ref, w3_ref, w2_ref, ys_ref, xbuf, xb, gate_tabs, sem):
    j = pl.program_id(0)
    step = pl.program_id(1)
    n_used = n_used_ref[0]

    def tile_copies(tile, act):
        slot = tile % 2
        group = tile_group_ref[tile]
        row0 = tile * GROUP_TILE

        def from_token_tile(b, carry):
            run = b * N_GROUPS + group
            lo = jnp.maximum(run_dst_ref[run], row0)
            hi = jnp.minimum(run_dst_ref[run] + run_len_ref[run], row0 + GROUP_TILE)
            n = jnp.maximum(hi - lo, 0)
            src = run_src_ref[run] + lo - run_dst_ref[run]
            dst = lo - row0
            for size in RUN_PIECES:
                done = n & (-2 * size)

                @pl.when((n & size) != 0)
                def _():
                    act(pltpu.make_async_copy(
                        xloc_hbm.at[b, pl.ds(pl.multiple_of(src + done, SUBLANES), size)],
                        xbuf.at[slot, pl.ds(pl.multiple_of(dst + done, SUBLANES), size)], sem.at[slot]))
            return carry

        lax.fori_loop(first_ref[tile], last_ref[tile] + 1, from_token_tile, 0)

    start = lambda cp: cp.start()
    wait = lambda cp: cp.wait()

    @pl.when(j < n_used)
    def _():
        @pl.when(step == 0)
        def _():
            @pl.when(j == 0)
            def _():
                xbuf[...] = jnp.zeros_like(xbuf)
                tile_copies(j, start)

            tile_copies(j, wait)
            rows_in = xbuf[j % 2]
            xb[...] = rows_in[:, :D_MODEL].astype(BF16)
            route = rows_in[:, D_MODEL:]
            lane = lax.broadcasted_iota(jnp.int32, route.shape, 1)
            for n, which in enumerate((ROUTE_E1, ROUTE_E2, ROUTE_W1, ROUTE_W2)):
                col = jnp.sum(jnp.where(lane == which, route, 0.0), axis=1, keepdims=True)
                gate_tabs[n] = jnp.broadcast_to(col, route.shape)

            @pl.when(j + 1 < n_used)
            def _():
                tile_copies(j + 1, start)

        def evaluate(rows):
            x = xb[:rows]
            total = None
            for s in range(EXPERTS_PER_STEP):
                expert = (tile_group_ref[j] * EXPERTS_PER_GROUP + step * EXPERTS_PER_STEP + s).astype(F32)
                gate = (jnp.where(gate_tabs[0, :rows] == expert, gate_tabs[2, :rows], 0.0)
                        + jnp.where(gate_tabs[1, :rows] == expert, gate_tabs[3, :rows], 0.0))
                hid = _silu(jnp.dot(x, w1_ref[0, s].astype(BF16), preferred_element_type=F32)) * jnp.dot(
                    x, w3_ref[0, s].astype(BF16), preferred_element_type=F32)
                y = jnp.dot(hid.astype(BF16), w2_ref[0, s].astype(BF16), preferred_element_type=F32)
                gated = jnp.concatenate(
                    [gate * y[:, c * LANES:(c + 1) * LANES] for c in range(D_MODEL // LANES)], axis=1)
                total = gated if total is None else total + gated

            @pl.when(step == 0)
            def _():
                ys_ref[:rows] = total

            @pl.when(step > 0)
            def _():
                ys_ref[:rows] += total

        half = GROUP_TILE // 2

        @pl.when(fill_ref[j] > half)
        def _():
            evaluate(GROUP_TILE)

        @pl.when(fill_ref[j] <= half)
        def _():
            evaluate(half)


def _experts(xloc, tile_tables, runs, slots, w1, w3, w2):
    steps = EXPERTS_PER_GROUP // EXPERTS_PER_STEP
    paired = lambda w: w.reshape((N_EXPERTS // EXPERTS_PER_STEP, EXPERTS_PER_STEP) + w.shape[1:])
    w_spec = lambda shape: pl.BlockSpec((1, EXPERTS_PER_STEP) + shape, lambda j, s, tg, nu, *_: (
        tg[jnp.minimum(j, nu[0] - 1)] * steps + jnp.where(j < nu[0], s, steps - 1), 0, 0, 0))
    grid_spec = pltpu.PrefetchScalarGridSpec(
        num_scalar_prefetch=8,
        grid=(slots // GROUP_TILE, steps),
        in_specs=[
            pl.BlockSpec(memory_space=pl.ANY),
            w_spec((D_MODEL, D_EXPERT)), w_spec((D_MODEL, D_EXPERT)), w_spec((D_EXPERT, D_MODEL)),
        ],
        out_specs=pl.BlockSpec((GROUP_TILE, D_MODEL), lambda j, s, tg, nu, *_: (jnp.minimum(j, nu[0] - 1), 0)),
        scratch_shapes=[pltpu.VMEM((2, GROUP_TILE, ROW_W), F32), pltpu.VMEM((GROUP_TILE, D_MODEL), BF16),
                        pltpu.VMEM((4, GROUP_TILE, LANES), F32), pltpu.SemaphoreType.DMA((2,))],
    )
    return pl.pallas_call(
        _expert_kernel,
        grid_spec=grid_spec,
        out_shape=jax.ShapeDtypeStruct((slots, D_MODEL), F32),
        compiler_params=pltpu.CompilerParams(
            dimension_semantics=("arbitrary", "arbitrary"), vmem_limit_bytes=EXPERT_VMEM_LIMIT),
        name="experts",
    )(*tile_tables, *runs, xloc, paired(w1), paired(w3), paired(w2))


def _combine_kernel(n_tiles, tile_base, run_len_ref, run_src_ref, run_dst_ref,
                    x1_ref, route_ref, g2_ref, fg_ref, ys_hbm, o_ref, buf, sem):
    i = pl.program_id(0)
    slot = i % 2

    def run_copies(local_tile, act):
        s = local_tile % 2
        tile = local_tile + tile_base
        for g in range(N_GROUPS):
            n = run_len_ref[tile * N_GROUPS + g]
            src = run_src_ref[tile * N_GROUPS + g]
            dst = run_dst_ref[tile * N_GROUPS + g]
            for size in RUN_PIECES:
                done = n & (-2 * size)

                @pl.when((n & size) != 0)
                def _():
                    act(pltpu.make_async_copy(
                        ys_hbm.at[pl.ds(pl.multiple_of(dst + done, SUBLANES), size)],
                        buf.at[s, pl.ds(pl.multiple_of(src + done, SUBLANES), size)], sem.at[s]))

    @pl.when(i == 0)
    def _():
        buf[...] = jnp.zeros_like(buf)
        run_copies(i, lambda cp: cp.start())

    @pl.when(i + 1 < n_tiles)
    def _():
        run_copies(i + 1, lambda cp: cp.start())

    run_copies(i, lambda cp: cp.wait())
    route = route_ref[...]
    lane = lax.broadcasted_iota(jnp.int32, route.shape, 1)
    local = jnp.sum(jnp.where(lane == ROUTE_LOCAL, route, 0.0), axis=1, keepdims=True)
    pick = jnp.where(lax.broadcasted_iota(jnp.int32, (route.shape[0], LOCAL_ROWS), 1).astype(F32) == local,
                     1.0, 0.0).astype(BF16)
    moe = sum(jnp.dot(pick, piece, preferred_element_type=F32) for piece in _split3(buf[slot])[:2])
    y = x1_ref[...] + g2_ref[0, 0] * moe
    o_ref[...] = _rms(y) * fg_ref[...]


def _combine(ys, runs, x1, route, tokens, mod4, mod_row_of_tile, final_g, tile_base):
    tiles = tokens // OUTPROJ_TILE
    tok = lambda w: pl.BlockSpec((OUTPROJ_TILE, w), lambda i, *_: (i + tile_base, 0))
    grid_spec = pltpu.PrefetchScalarGridSpec(
        num_scalar_prefetch=3,
        grid=(tiles,),
        in_specs=[
            tok(D_MODEL), tok(ROUTER_COLS),
            pl.BlockSpec((1, 1, 1, D_MODEL), lambda i, *_: (mod_row_of_tile(i + tile_base), 5, 0, 0)),
            pl.BlockSpec((1, D_MODEL), lambda i, *_: (0, 0)),
            pl.BlockSpec(memory_space=pl.ANY),
        ],
        out_specs=pl.BlockSpec((OUTPROJ_TILE, D_MODEL), lambda i, *_: (i, 0)),
        scratch_shapes=[pltpu.VMEM((2, LOCAL_ROWS, D_MODEL), F32), pltpu.SemaphoreType.DMA((2,))],
    )
    return pl.pallas_call(
        functools.partial(_combine_kernel, tiles, tile_base),
        grid_spec=grid_spec,
        out_shape=jax.ShapeDtypeStruct((tokens, D_MODEL), F32),
        compiler_params=pltpu.CompilerParams(
            dimension_semantics=("arbitrary",), vmem_limit_bytes=VMEM_LIMIT),
        name="combine",
    )(*runs, x1, route, mod4, final_g, ys)


def _routing_tables(counts):
    counts = counts.astype(jnp.int32)
    tiles = counts.shape[0]
    run_len = ((counts + SUBLANES - 1) // SUBLANES) * SUBLANES
    run_src = jnp.cumsum(run_len, axis=1) - run_len
    group_rows = jnp.sum(run_len, axis=0)
    padded = ((group_rows + GROUP_TILE - 1) // GROUP_TILE) * GROUP_TILE
    ends = jnp.cumsum(padded)
    offs = ends - padded
    run_dst = offs[None, :] + jnp.cumsum(run_len, axis=0) - run_len

    n_used = ends[-1] // GROUP_TILE
    max_rows = tiles * (OUTPROJ_TILE + N_GROUPS * (SUBLANES - 1))
    max_tiles = -(-max_rows // GROUP_TILE) + N_GROUPS
    tile_ids = jnp.minimum(jnp.arange(max_tiles, dtype=jnp.int32), n_used - 1)
    tile_group = jnp.sum(tile_ids[:, None] * GROUP_TILE >= ends[None, :], axis=1).astype(jnp.int32)
    of_group = (tile_group[:, None] == jnp.arange(N_GROUPS, dtype=jnp.int32))[:, None, :]
    start = jnp.sum(jnp.where(of_group, run_dst[None], 0), axis=-1)
    stop = start + jnp.sum(jnp.where(of_group, run_len[None], 0), axis=-1)
    row0 = (tile_ids * GROUP_TILE)[:, None]
    first = jnp.sum(stop <= row0, axis=1).astype(jnp.int32)
    last = jnp.sum(start < row0 + GROUP_TILE, axis=1).astype(jnp.int32) - 1
    group_end = jnp.sum(jnp.where(of_group[:, 0, :], (offs + group_rows)[None, :], 0), axis=-1)
    fill = jnp.clip(group_end - row0[:, 0], 0, GROUP_TILE).astype(jnp.int32)
    flat = lambda a: a.reshape(-1)
    return ((tile_group, n_used.reshape(1), first, last, fill), (flat(run_len), flat(run_src), flat(run_dst)),
            max_tiles * GROUP_TILE)


def _mixer(x, mod4, mod_row, is_grid, s_f0, s_b0, p):
    norm_mix_g, w_in_bf, conv_w, decay_rows = p
    y_conv, *qkvg = _inproj(x, mod4, mod_row, norm_mix_g, w_in_bf, conv_w, is_grid)
    per_seq = lambda a: a.reshape(x.shape[0], x.shape[1], a.shape[-1])
    ret = _retention(*map(per_seq, qkvg), decay_rows, s_f0, s_b0, emit_state=not is_grid)
    return y_conv, ret[0].reshape(-1, RET_W), ret[1:]


def kernel(x_prompt, x_sample, state_ret_fwd, state_ret_bwd, c, c_ctx, norm_mix_g, norm_ffn_g, w_ada, b_ada, w_in, conv_w, ret_decay_fwd, ret_decay_bwd, w_out, w_router_group, b_router_group, w_router_expert, b_router_expert, w_gate_e, w_up_e, w_down_e, final_norm_g):
    assert norm_mix_g.shape[0] == 1, "single-layer backbone"
    n_lat = c.shape[0]
    ctx_row = n_lat
    mod_rows = 8
    cvec = jnp.concatenate([c, c_ctx[None, :], jnp.zeros((mod_rows - n_lat - 1, D_MODEL), F32)], axis=0)
    mod = _modulation(cvec, w_ada[0], b_ada[0][None, :])
    mod4 = mod.reshape(mod_rows, 6, 1, D_MODEL)

    pad = ROUTER_COLS - N_GROUPS - N_EXPERTS
    w_router = jnp.concatenate(
        [w_router_group[0], w_router_expert[0], jnp.zeros((D_MODEL, pad), F32)], axis=1).astype(BF16)
    b_router = jnp.concatenate([b_router_group[0], b_router_expert[0], jnp.zeros((pad,), F32)])[None, :]
    decay_rows = jnp.broadcast_to(
        jnp.concatenate([ret_decay_fwd[0], ret_decay_bwd[0]])[:, None], (2 * RET_HEADS, LANES))
    p_mix = (norm_mix_g, w_in[0].astype(BF16), conv_w[0], decay_rows)
    w_out_bf = w_out[0].astype(BF16)
    final_g = final_norm_g[None, :]

    ctx_tokens = x_prompt.shape[0] * x_prompt.shape[1]
    lat_tokens = x_sample.shape[0] * x_sample.shape[1]
    ctx_tiles = ctx_tokens // OUTPROJ_TILE
    lat_tiles_per_seq = x_sample.shape[1] // OUTPROJ_TILE
    tile_mod = lambda i: jnp.where(i < ctx_tiles, ctx_row, (i - ctx_tiles) // lat_tiles_per_seq)
    flat = lambda a: a.reshape(-1, a.shape[-1])

    yc_c, yr_c, (s_f, s_b) = _mixer(x_prompt, mod4, lambda b: ctx_row, False, None, None, p_mix)
    yc_l, yr_l, _ = _mixer(x_sample, mod4, lambda b: b, True, state_ret_fwd, state_ret_bwd, p_mix)

    x1, xloc, route, cnt = _outproj((yc_c, yr_c, flat(x_prompt)), (yc_l, yr_l, flat(x_sample)), mod4, tile_mod,
                                    norm_ffn_g, w_out_bf, w_router, b_router)
    tile_tables, runs, slots = _routing_tables(cnt[:, 0, :N_GROUPS])
    ys = _experts(xloc, tile_tables, runs, slots, w_gate_e[0], w_up_e[0], w_down_e[0])
    y_prompt = _combine(ys, runs, x1, route, ctx_tokens, mod4, tile_mod, final_g, 0)
    y_sample = _combine(ys, runs, x1, route, lat_tokens, mod4, tile_mod, final_g, ctx_tiles)
    return (y_prompt.reshape(x_prompt.shape), y_sample.reshape(x_sample.shape),
            s_f.astype(x_prompt.dtype), s_b.astype(x_prompt.dtype))
```

```python
import functools

import jax
import jax.numpy as jnp
from jax import lax
from jax.experimental import pallas as pl
from jax.experimental.pallas import tpu as pltpu

F32 = jnp.float32
BF16 = jnp.bfloat16

D_MODEL = 1024
GRID_W = 64
CONV_W = 512
RET_HEADS = 4
RET_DK = 128
RET_DV = 128
RET_W = RET_HEADS * RET_DV
QK_W = RET_HEADS * RET_DK
CHUNK = 128
N_GROUPS = 4
EXPERTS_PER_GROUP = 8
N_EXPERTS = N_GROUPS * EXPERTS_PER_GROUP
D_EXPERT = 256
ROPE_BASE = 10000.0
EPS = 1e-6

LANES = 128
TOKEN_TILE = 512
OUTPROJ_TILE = 512
GROUP_TILE = 1024
EXPERTS_PER_STEP = 4
RET_UNROLL = 8
SUBLANES = 8
XLOC_ROWS = OUTPROJ_TILE + N_GROUPS * SUBLANES
LOCAL_ROWS = OUTPROJ_TILE + LANES
ROW_W = D_MODEL + LANES
ROUTE_GROUP, ROUTE_LOCAL, ROUTE_E1, ROUTE_E2, ROUTE_W1, ROUTE_W2 = range(6)
MOD_COLS = 1536
ROUTER_COLS = LANES
VMEM_LIMIT = 48 * 1024 * 1024
EXPERT_VMEM_LIMIT = 56 * 1024 * 1024


def _silu(x):
    return x * jax.nn.sigmoid(x)


def _rms(x):
    return x * lax.rsqrt(jnp.mean(x * x, axis=-1, keepdims=True) + EPS)


def _bdot(a, b):
    return jnp.dot(a.astype(BF16), b.astype(BF16), preferred_element_type=F32)


def _split3(x):
    hi = x.astype(BF16)
    rest = x - hi.astype(F32)
    mid = rest.astype(BF16)
    return hi, mid, (rest - mid.astype(F32)).astype(BF16)


def _mod_kernel(c_ref, w_ref, b_ref, o_ref):
    o_ref[...] = _bdot(_silu(c_ref[...]), w_ref[...]) + b_ref[...]


def _modulation(cvec, w_ada, b_ada):
    rows = cvec.shape[0]
    n = w_ada.shape[1]
    return pl.pallas_call(
        _mod_kernel,
        grid=(n // MOD_COLS,),
        in_specs=[
            pl.BlockSpec((rows, D_MODEL), lambda j: (0, 0)),
            pl.BlockSpec((D_MODEL, MOD_COLS), lambda j: (0, j)),
            pl.BlockSpec((1, MOD_COLS), lambda j: (0, j)),
        ],
        out_specs=pl.BlockSpec((rows, MOD_COLS), lambda j: (0, j)),
        out_shape=jax.ShapeDtypeStruct((rows, n), F32),
        compiler_params=pltpu.CompilerParams(vmem_limit_bytes=VMEM_LIMIT),
        name="modulation",
    )(cvec, w_ada, b_ada)


def _inproj_kernel(seg, is_grid, x_ref, sh_ref, sc_ref, ng_ref, w_ref, cw_ref, *rest):
    if is_grid:
        cos_ref, sa_ref, sb_ref, yc_ref, q_ref, k_ref, v_ref, g_ref = rest
    else:
        yc_ref, q_ref, k_ref, v_ref, g_ref = rest
    x = x_ref[...]
    xn = (_rms(x) * ng_ref[...]) * (1.0 + sc_ref[0, 0]) + sh_ref[0, 0]
    xb = xn.astype(BF16)

    def proj(c0, n):
        return jnp.dot(xb, w_ref[:, c0:c0 + n], preferred_element_type=F32)

    gate_b = proj(0, CONV_W)
    u = proj(CONV_W, CONV_W) * proj(2 * CONV_W, CONV_W)
    rows = u.shape[0]
    pos = lax.broadcasted_iota(jnp.int32, u.shape, 0) & (seg - 1)
    u_prev = jnp.where(pos != 0, pltpu.roll(u, 1, 0), 0.0)
    u_next = jnp.where(pos != seg - 1, pltpu.roll(u, rows - 1, 0), 0.0)
    conv = cw_ref[0:1, :] * u_prev + cw_ref[1:2, :] * u + cw_ref[2:3, :] * u_next
    yc_ref[...] = (gate_b * conv).astype(yc_ref.dtype)

    q0 = 3 * CONV_W
    q = proj(q0, QK_W)
    k = proj(q0 + QK_W, QK_W)
    if is_grid:
        cos, sa, sb = cos_ref[...], sa_ref[...], sb_ref[...]

        def rope(t):
            out = []
            for h in range(RET_HEADS):
                th = t[:, h * RET_DK:(h + 1) * RET_DK]
                out.append(th * cos + pltpu.roll(th, RET_DK - 1, 1) * sa + pltpu.roll(th, 1, 1) * sb)
            return jnp.concatenate(out, axis=1)

        q, k = rope(q), rope(k)
    q_ref[...] = q
    k_ref[...] = k
    v_ref[...] = proj(q0 + 2 * QK_W, RET_W)
    g_ref[...] = proj(q0 + 2 * QK_W + RET_W, RET_W)


def _rope_tables(length):
    pos = jnp.arange(length)
    row = (pos // GRID_W).astype(F32)
    col = (pos % GRID_W).astype(F32)
    n_pairs = RET_DK // 4
    freqs = ROPE_BASE ** (-(jnp.arange(n_pairs, dtype=F32) * 2.0 / (RET_DK // 2)))
    ang = jnp.concatenate([row[:, None] * freqs, col[:, None] * freqs], axis=-1)
    cos = jnp.repeat(jnp.cos(ang), 2, axis=-1)
    sin = jnp.repeat(jnp.sin(ang), 2, axis=-1)
    even = (jnp.arange(RET_DK) % 2) == 0
    return cos, jnp.where(even, -sin, 0.0), jnp.where(even, 0.0, sin)


def _inproj(x, mod4, mod_row, norm_g, w_in_bf, conv_w, is_grid):
    bsz, length, _ = x.shape
    seg = GRID_W if is_grid else length
    assert TOKEN_TILE % seg == 0 and (length % TOKEN_TILE == 0 or TOKEN_TILE % length == 0)
    tokens = bsz * length
    tiles_per_seq = max(length // TOKEN_TILE, 1)
    seqs_per_tile = max(TOKEN_TILE // length, 1)
    batch_of = lambda i: (i // tiles_per_seq) * seqs_per_tile

    def mod_spec(which):
        return pl.BlockSpec((1, 1, 1, D_MODEL), lambda i: (mod_row(batch_of(i)), which, 0, 0))

    def tok_spec(width):
        return pl.BlockSpec((TOKEN_TILE, width), lambda i: (i, 0))

    in_specs = [
        tok_spec(D_MODEL), mod_spec(0), mod_spec(1),
        pl.BlockSpec((1, D_MODEL), lambda i: (0, 0)),
        pl.BlockSpec(w_in_bf.shape, lambda i: (0, 0)),
        pl.BlockSpec(conv_w.shape, lambda i: (0, 0)),
    ]
    args = [x.reshape(tokens, D_MODEL), mod4, mod4, norm_g, w_in_bf, conv_w]
    if is_grid:
        assert length % TOKEN_TILE == 0
        in_specs += [pl.BlockSpec((TOKEN_TILE, RET_DK), lambda i: (i % tiles_per_seq, 0))] * 3
        args += list(_rope_tables(length))
    shp = lambda w, dt: jax.ShapeDtypeStruct((tokens, w), dt)
    return pl.pallas_call(
        functools.partial(_inproj_kernel, seg, is_grid),
        grid=(tokens // TOKEN_TILE,),
        in_specs=in_specs,
        out_specs=[tok_spec(CONV_W), tok_spec(QK_W), tok_spec(QK_W), tok_spec(RET_W), tok_spec(RET_W)],
        out_shape=[shp(CONV_W, BF16), shp(QK_W, F32), shp(QK_W, F32), shp(RET_W, F32), shp(RET_W, F32)],
        compiler_params=pltpu.CompilerParams(
            dimension_semantics=("parallel",), vmem_limit_bytes=VMEM_LIMIT),
        name="inproj_grid" if is_grid else "inproj_seq",
    )(*args)


def _ret_kernel(n_chunks, heads, has_init, emit_state, a_ref, q_ref, k_ref, v_ref, g_ref, *rest):
    rest = list(rest)
    if has_init:
        sf0_ref, sb0_ref = rest[:2]
        rest = rest[2:]
    y_ref = rest.pop(0)
    if emit_state:
        sf_out, sb_out = rest[:2]
        rest = rest[2:]
    st_f, st_b, dec = rest
    c = CHUNK
    sq = (c, c)
    head0 = pl.program_id(0) * heads

    def log_decays(hh):
        lg_f = jnp.log1p(-jnp.exp(a_ref[pl.ds(head0 + hh, 1), :]))
        lg_b = jnp.log1p(-jnp.exp(a_ref[pl.ds(head0 + hh + RET_HEADS, 1), :]))
        return lg_f, lg_b

    @pl.when(pl.program_id(1) == 0)
    def _():
        row = lax.broadcasted_iota(jnp.int32, sq, 0).astype(F32)
        col = lax.broadcasted_iota(jnp.int32, sq, 1).astype(F32)
        scale = RET_DK ** -0.5
        for hh in range(heads):
            lg_f, lg_b = log_decays(hh)
            dec[hh, 0] = scale * (
                jnp.where(row >= col, jnp.exp(jnp.where(row >= col, row - col, 0.0) * lg_f), 0.0)
                + jnp.where(col >= row, jnp.exp(jnp.where(col >= row, col - row, 0.0) * lg_b), 0.0))
            dec[hh, 1] = jnp.exp((row + 1.0) * lg_f)
            dec[hh, 2] = jnp.exp((c - row) * lg_b)
            dec[hh, 3] = scale * jnp.exp((c - 1.0 - col) * lg_f)
            dec[hh, 4] = scale * jnp.exp(col * lg_b)

    def rows(n):
        return pl.ds(pl.multiple_of(n * c, c), c) if not isinstance(n, int) else pl.ds(n * c, c)

    def cols(hh):
        return slice(hh * RET_DK, (hh + 1) * RET_DK)

    def kv_step(hh, n):
        kt = jnp.transpose(k_ref[0, rows(n), cols(hh)])
        lhs = jnp.concatenate([kt * dec[hh, 3], kt * dec[hh, 4]], axis=0)
        kv = _bdot(lhs, v_ref[0, rows(n), cols(hh)])
        st_f[hh, n] = kv[:RET_DK]
        st_b[hh, n] = kv[RET_DK:]

    def scan(hh, st, decay, order, s):
        def step(i, s):
            n = order(i)
            kv = st[hh, n]
            st[hh, n] = s
            return s * decay + kv
        if n_chunks <= RET_UNROLL:
            for i in range(n_chunks):
                s = step(i, s)
            return s
        return lax.fori_loop(0, n_chunks, step, s, unroll=RET_UNROLL)

    def out_step(hh, n):
        q = q_ref[0, rows(n), cols(hh)]
        scores = lax.dot_general(q.astype(BF16), k_ref[0, rows(n), cols(hh)].astype(BF16),
                                 (((1,), (1,)), ((), ())), preferred_element_type=F32)
        o = _bdot(scores * dec[hh, 0], v_ref[0, rows(n), cols(hh)])
        q_dec = jnp.concatenate([q * dec[hh, 1], q * dec[hh, 2]], axis=1)
        o = o + _bdot(q_dec, jnp.concatenate([st_f[hh, n], st_b[hh, n]], axis=0))
        y = _silu(g_ref[0, rows(n), cols(hh)]) * _rms(o)
        y_ref[0, rows(n), cols(hh)] = y.astype(y_ref.dtype)

    def over_chunks(step):
        if n_chunks * heads <= RET_UNROLL:
            for hh in range(heads):
                for n in range(n_chunks):
                    step(hh, n)
        else:
            for hh in range(heads):
                lax.fori_loop(0, n_chunks, lambda n, carry: (step(hh, n), carry)[1], 0, unroll=RET_UNROLL)

    over_chunks(kv_step)
    finals = []
    for hh in range(heads):
        lg_f, lg_b = log_decays(hh)
        s_f = sf0_ref[0, 0, hh] if has_init else jnp.zeros(sq, F32)
        s_b = sb0_ref[0, 0, hh] if has_init else jnp.zeros(sq, F32)
        s_f = scan(hh, st_f, jnp.exp(c * lg_f), lambda i: i, s_f)
        s_b = scan(hh, st_b, jnp.exp(c * lg_b), lambda i: n_chunks - 1 - i, s_b)
        finals.append((s_f, s_b))
    over_chunks(out_step)
    if emit_state:
        for hh, (s_f, s_b) in enumerate(finals):
            sf_out[0, 0, hh] = s_f
            sb_out[0, 0, hh] = s_b


def _retention(q, k, v, g, decay_rows, s_f0, s_b0, emit_state):
    bsz, length, _ = q.shape
    n_chunks = length // CHUNK
    has_init = s_f0 is not None
    heads = RET_HEADS if n_chunks * RET_HEADS <= RET_UNROLL else 1
    head_spec = pl.BlockSpec((1, length, heads * RET_DK), lambda h, b: (b, 0, h))
    st_spec = pl.BlockSpec((1, 1, heads, RET_DK, RET_DV), lambda h, b: (b, 0, h, 0, 0))
    in_specs = [pl.BlockSpec(decay_rows.shape, lambda h, b: (0, 0))] + [head_spec] * 4
    args = [decay_rows, q, k, v, g]
    if has_init:
        in_specs += [st_spec, st_spec]
        args += [s_f0, s_b0]
    out_specs = [head_spec]
    out_shape = [jax.ShapeDtypeStruct((bsz, length, RET_W), BF16)]
    if emit_state:
        st_shape = jax.ShapeDtypeStruct((bsz, 1, RET_HEADS, RET_DK, RET_DV), F32)
        out_specs += [st_spec, st_spec]
        out_shape += [st_shape, st_shape]
    return pl.pallas_call(
        functools.partial(_ret_kernel, n_chunks, heads, has_init, emit_state),
        grid=(RET_HEADS // heads, bsz),
        in_specs=in_specs,
        out_specs=out_specs,
        out_shape=out_shape,
        scratch_shapes=[
            pltpu.VMEM((heads, n_chunks, RET_DK, RET_DV), F32),
            pltpu.VMEM((heads, n_chunks, RET_DK, RET_DV), F32),
            pltpu.VMEM((heads, 5, CHUNK, CHUNK), F32),
        ],
        compiler_params=pltpu.CompilerParams(
            dimension_semantics=("arbitrary", "arbitrary"), vmem_limit_bytes=VMEM_LIMIT),
        name="retention_init" if has_init else "retention_zero",
    )(*args)


def _route(logits):
    lane = lax.broadcasted_iota(jnp.int32, logits.shape, 1)
    lane_f = lane.astype(F32)
    neg = -jnp.inf
    far = float(LANES)
    is_g = lane < N_GROUPS
    lg = jnp.where(is_g, logits, neg)
    g_max = jnp.max(lg, axis=1, keepdims=True)
    g_idx = jnp.min(jnp.where(lg == g_max, lane_f, far), axis=1, keepdims=True)
    p_sel = 1.0 / jnp.sum(jnp.where(is_g, jnp.exp(lg - g_max), 0.0), axis=1, keepdims=True)
    lane_group = ((lane - N_GROUPS) >> (EXPERTS_PER_GROUP.bit_length() - 1)).astype(F32)
    sel = (lane >= N_GROUPS) & (lane < N_GROUPS + N_EXPERTS) & (lane_group == g_idx)
    le = jnp.where(sel, logits, neg)
    v1 = jnp.max(le, axis=1, keepdims=True)
    i1 = jnp.min(jnp.where(le == v1, lane_f, far), axis=1, keepdims=True)
    le2 = jnp.where(lane_f == i1, neg, le)
    v2 = jnp.max(le2, axis=1, keepdims=True)
    i2 = jnp.min(jnp.where(le2 == v2, lane_f, far), axis=1, keepdims=True)
    e2 = jnp.exp(v2 - v1)
    w1 = p_sel * (1.0 / (1.0 + e2))
    w2 = p_sel * (e2 / (1.0 + e2))
    return lane, lane_f, g_idx, i1, i2, w1, w2


def _outproj_kernel(ctx_tiles, yc_c, yr_c, x_c, yc_l, yr_l, x_l, *rest):
    @pl.when(pl.program_id(0) < ctx_tiles)
    def _():
        _outproj_tile(yc_c, yr_c, x_c, *rest)

    @pl.when(pl.program_id(0) >= ctx_tiles)
    def _():
        _outproj_tile(yc_l, yr_l, x_l, *rest)


def _outproj_tile(yc_ref, yr_ref, x_ref, g1_ref, sh_ref, sc_ref, ng_ref, wo_ref, wr_ref, br_ref,
                  x1_ref, xloc_ref, route_ref, cnt_ref):
    m = (jnp.dot(yc_ref[...], wo_ref[0:CONV_W, :], preferred_element_type=F32)
         + jnp.dot(yr_ref[...], wo_ref[CONV_W:, :], preferred_element_type=F32))
    x1 = x_ref[...] + g1_ref[0, 0] * m
    x1_ref[...] = x1
    xn = (_rms(x1) * ng_ref[...]) * (1.0 + sc_ref[0, 0]) + sh_ref[0, 0]
    xb = xn.astype(BF16)
    logits = jnp.dot(xb, wr_ref[...], preferred_element_type=F32) + br_ref[...]
    lane, lane_f, g_idx, i1, i2, w1, w2 = _route(logits)

    picks = jnp.where(lane_f == g_idx, 1.0, 0.0)
    rows = picks.shape[0]
    tri = (lax.broadcasted_iota(jnp.int32, (rows, rows), 0)
           > lax.broadcasted_iota(jnp.int32, (rows, rows), 1))
    before = jnp.dot(jnp.where(tri, 1.0, 0.0).astype(BF16), picks.astype(BF16),
                     preferred_element_type=F32)
    count = jnp.sum(picks, axis=0, keepdims=True)
    cnt_ref[0] = count
    count8 = jnp.broadcast_to(jnp.floor((count + (SUBLANES - 1)) * (1.0 / SUBLANES)) * SUBLANES,
                              (SUBLANES, LANES))
    lane8 = lane[:SUBLANES]
    start = sum(jnp.where(lane8 >= k, pltpu.roll(count8, k, 1), 0.0) for k in range(1, N_GROUPS))
    local = jnp.sum(jnp.where(lane_f == g_idx, before + start[0:1], 0.0), axis=1, keepdims=True)
    route = jnp.where(lane == ROUTE_GROUP, g_idx, jnp.where(lane == ROUTE_LOCAL, local, jnp.where(
        lane == ROUTE_E1, i1 - N_GROUPS, jnp.where(lane == ROUTE_E2, i2 - N_GROUPS, jnp.where(
            lane == ROUTE_W1, w1, jnp.where(lane == ROUTE_W2, w2, 0.0))))))
    route_ref[...] = route

    local_row = jnp.transpose(jnp.broadcast_to(local, (rows, LANES)))[0:1, :]
    place = jnp.where(lax.broadcasted_iota(jnp.int32, (XLOC_ROWS, rows), 0).astype(F32) == local_row,
                      1.0, 0.0).astype(BF16)
    xloc_ref[0, :, :D_MODEL] = jnp.dot(place, xb, preferred_element_type=F32)
    xloc_ref[0, :, D_MODEL:] = sum(
        jnp.dot(place, piece, preferred_element_type=F32) for piece in _split3(route))


def _outproj(ctx, lat, mod4, mod_row_of_tile, norm_g, w_out_bf, w_router_bf, b_router):
    ctx_tiles = ctx[2].shape[0] // OUTPROJ_TILE
    tiles = ctx_tiles + lat[2].shape[0] // OUTPROJ_TILE
    tokens = tiles * OUTPROJ_TILE

    def mod_spec(which):
        return pl.BlockSpec((1, 1, 1, D_MODEL), lambda i: (mod_row_of_tile(i), which, 0, 0))

    ctx_tok = lambda w: pl.BlockSpec((OUTPROJ_TILE, w), lambda i: (jnp.minimum(i, ctx_tiles - 1), 0))
    lat_tok = lambda w: pl.BlockSpec((OUTPROJ_TILE, w), lambda i: (jnp.maximum(i - ctx_tiles, 0), 0))
    tok = lambda w: pl.BlockSpec((OUTPROJ_TILE, w), lambda i: (i, 0))
    full = lambda a: pl.BlockSpec(a.shape, lambda i: (0,) * a.ndim)
    widths = (CONV_W, RET_W, D_MODEL)
    return pl.pallas_call(
        functools.partial(_outproj_kernel, ctx_tiles),
        grid=(tiles,),
        in_specs=[ctx_tok(w) for w in widths] + [lat_tok(w) for w in widths] + [
            mod_spec(2), mod_spec(3), mod_spec(4),
            full(norm_g), full(w_out_bf), full(w_router_bf), full(b_router)],
        out_specs=[tok(D_MODEL),
                   pl.BlockSpec((1, XLOC_ROWS, ROW_W), lambda i: (i, 0, 0)),
                   tok(ROUTER_COLS),
                   pl.BlockSpec((1, 1, ROUTER_COLS), lambda i: (i, 0, 0))],
        out_shape=[jax.ShapeDtypeStruct((tokens, D_MODEL), F32),
                   jax.ShapeDtypeStruct((tiles, XLOC_ROWS, ROW_W), F32),
                   jax.ShapeDtypeStruct((tokens, ROUTER_COLS), F32),
                   jax.ShapeDtypeStruct((tiles, 1, ROUTER_COLS), F32)],
        compiler_params=pltpu.CompilerParams(
            dimension_semantics=("parallel",), vmem_limit_bytes=VMEM_LIMIT),
        name="outproj",
    )(*ctx, *lat, mod4, mod4, mod4, norm_g, w_out_bf, w_router_bf, b_router)


RUN_PIECES = tuple(SUBLANES << b for b in reversed(range((OUTPROJ_TILE // SUBLANES).bit_length())))


def _expert_kernel(tile_group_ref, n_used_ref, first_ref, last_ref, fill_ref,
                   run_len_ref, run_src_ref, run_dst_ref,
                   xloc_hbm, w1_ref, w3_ref, w2_ref, ys_ref, xbuf, xb, gate_tabs, sem):
    j = pl.program_id(0)
    step = pl.program_id(1)
    n_used = n_used_ref[0]

    def tile_copies(tile, act):
        slot = tile % 2
        group = tile_group_ref[tile]
        row0 = tile * GROUP_TILE

        def from_token_tile(b, carry):
            run = b * N_GROUPS + group
            lo = jnp.maximum(run_dst_ref[run], row0)
            hi = jnp.minimum(run_dst_ref[run] + run_len_ref[run], row0 + GROUP_TILE)
            n = jnp.maximum(hi - lo, 0)
            src = run_src_ref[run] + lo - run_dst_ref[run]
            dst = lo - row0
            for size in RUN_PIECES:
                done = n & (-2 * size)

                @pl.when((n & size) != 0)
                def _():
                    act(pltpu.make_async_copy(
                        xloc_hbm.at[b, pl.ds(pl.multiple_of(src + done, SUBLANES), size)],
                        xbuf.at[slot, pl.ds(pl.multiple_of(dst + done, SUBLANES), size)], sem.at[slot]))
            return carry

        lax.fori_loop(first_ref[tile], last_ref[tile] + 1, from_token_tile, 0)

    start = lambda cp: cp.start()
    wait = lambda cp: cp.wait()

    @pl.when(j < n_used)
    def _():
        @pl.when(step == 0)
        def _():
            @pl.when(j == 0)
            def _():
                xbuf[...] = jnp.zeros_like(xbuf)
                tile_copies(j, start)

            tile_copies(j, wait)
            rows_in = xbuf[j % 2]
            xb[...] = rows_in[:, :D_MODEL].astype(BF16)
            route = rows_in[:, D_MODEL:]
            lane = lax.broadcasted_iota(jnp.int32, route.shape, 1)
            for n, which in enumerate((ROUTE_E1, ROUTE_E2, ROUTE_W1, ROUTE_W2)):
                col = jnp.sum(jnp.where(lane == which, route, 0.0), axis=1, keepdims=True)
                gate_tabs[n] = jnp.broadcast_to(col, route.shape)

            @pl.when(j + 1 < n_used)
            def _():
                tile_copies(j + 1, start)

        def evaluate(rows):
            x = xb[:rows]
            total = None
            for s in range(EXPERTS_PER_STEP):
                expert = (tile_group_ref[j] * EXPERTS_PER_GROUP + step * EXPERTS_PER_STEP + s).astype(F32)
                gate = (jnp.where(gate_tabs[0, :rows] == expert, gate_tabs[2, :rows], 0.0)
                        + jnp.where(gate_tabs[1, :rows] == expert, gate_tabs[3, :rows], 0.0))
                hid = _silu(jnp.dot(x, w1_ref[0, s].astype(BF16), preferred_element_type=F32)) * jnp.dot(
                    x, w3_ref[0, s].astype(BF16), preferred_element_type=F32)
                y = jnp.dot(hid.astype(BF16), w2_ref[0, s].astype(BF16), preferred_element_type=F32)
                gated = jnp.concatenate(
                    [gate * y[:, c * LANES:(c + 1) * LANES] for c in range(D_MODEL // LANES)], axis=1)
                total = gated if total is None else total + gated

            @pl.when(step == 0)
            def _():
                ys_ref[:rows] = total

            @pl.when(step > 0)
            def _():
                ys_ref[:rows] += total

        half = GROUP_TILE // 2

        @pl.when(fill_ref[j] > half)
        def _():
            evaluate(GROUP_TILE)

        @pl.when(fill_ref[j] <= half)
        def _():
            evaluate(half)


def _experts(xloc, tile_tables, runs, slots, w1, w3, w2):
    steps = EXPERTS_PER_GROUP // EXPERTS_PER_STEP
    paired = lambda w: w.reshape((N_EXPERTS // EXPERTS_PER_STEP, EXPERTS_PER_STEP) + w.shape[1:])
    w_spec = lambda shape: pl.BlockSpec((1, EXPERTS_PER_STEP) + shape, lambda j, s, tg, nu, *_: (
        tg[jnp.minimum(j, nu[0] - 1)] * steps + jnp.where(j < nu[0], s, steps - 1), 0, 0, 0))
    grid_spec = pltpu.PrefetchScalarGridSpec(
        num_scalar_prefetch=8,
        grid=(slots // GROUP_TILE, steps),
        in_specs=[
            pl.BlockSpec(memory_space=pl.ANY),
            w_spec((D_MODEL, D_EXPERT)), w_spec((D_MODEL, D_EXPERT)), w_spec((D_EXPERT, D_MODEL)),
        ],
        out_specs=pl.BlockSpec((GROUP_TILE, D_MODEL), lambda j, s, tg, nu, *_: (jnp.minimum(j, nu[0] - 1), 0)),
        scratch_shapes=[pltpu.VMEM((2, GROUP_TILE, ROW_W), F32), pltpu.VMEM((GROUP_TILE, D_MODEL), BF16),
                        pltpu.VMEM((4, GROUP_TILE, LANES), F32), pltpu.SemaphoreType.DMA((2,))],
    )
    return pl.pallas_call(
        _expert_kernel,
        grid_spec=grid_spec,
        out_shape=jax.ShapeDtypeStruct((slots, D_MODEL), F32),
        compiler_params=pltpu.CompilerParams(
            dimension_semantics=("arbitrary", "arbitrary"), vmem_limit_bytes=EXPERT_VMEM_LIMIT),
        name="experts",
    )(*tile_tables, *runs, xloc, paired(w1), paired(w3), paired(w2))


def _combine_kernel(n_tiles, tile_base, run_len_ref, run_src_ref, run_dst_ref,
                    x1_ref, route_ref, g2_ref, fg_ref, ys_hbm, o_ref, buf, sem):
    i = pl.program_id(0)
    slot = i % 2

    def run_copies(local_tile, act):
        s = local_tile % 2
        tile = local_tile + tile_base
        for g in range(N_GROUPS):
            n = run_len_ref[tile * N_GROUPS + g]
            src = run_src_ref[tile * N_GROUPS + g]
            dst = run_dst_ref[tile * N_GROUPS + g]
            for size in RUN_PIECES:
                done = n & (-2 * size)

                @pl.when((n & size) != 0)
                def _():
                    act(pltpu.make_async_copy(
                        ys_hbm.at[pl.ds(pl.multiple_of(dst + done, SUBLANES), size)],
                        buf.at[s, pl.ds(pl.multiple_of(src + done, SUBLANES), size)], sem.at[s]))

    @pl.when(i == 0)
    def _():
        buf[...] = jnp.zeros_like(buf)
        run_copies(i, lambda cp: cp.start())

    @pl.when(i + 1 < n_tiles)
    def _():
        run_copies(i + 1, lambda cp: cp.start())

    run_copies(i, lambda cp: cp.wait())
    route = route_ref[...]
    lane = lax.broadcasted_iota(jnp.int32, route.shape, 1)
    local = jnp.sum(jnp.where(lane == ROUTE_LOCAL, route, 0.0), axis=1, keepdims=True)
    pick = jnp.where(lax.broadcasted_iota(jnp.int32, (route.shape[0], LOCAL_ROWS), 1).astype(F32) == local,
                     1.0, 0.0).astype(BF16)
    moe = sum(jnp.dot(pick, piece, preferred_element_type=F32) for piece in _split3(buf[slot])[:2])
    y = x1_ref[...] + g2_ref[0, 0] * moe
    o_ref[...] = _rms(y) * fg_ref[...]


def _combine(ys, runs, x1, route, tokens, mod4, mod_row_of_tile, final_g, tile_base):
    tiles = tokens // OUTPROJ_TILE
    tok = lambda w: pl.BlockSpec((OUTPROJ_TILE, w), lambda i, *_: (i + tile_base, 0))
    grid_spec = pltpu.PrefetchScalarGridSpec(
        num_scalar_prefetch=3,
        grid=(tiles,),
        in_specs=[
            tok(D_MODEL), tok(ROUTER_COLS),
            pl.BlockSpec((1, 1, 1, D_MODEL), lambda i, *_: (mod_row_of_tile(i + tile_base), 5, 0, 0)),
            pl.BlockSpec((1, D_MODEL), lambda i, *_: (0, 0)),
            pl.BlockSpec(memory_space=pl.ANY),
        ],
        out_specs=pl.BlockSpec((OUTPROJ_TILE, D_MODEL), lambda i, *_: (i, 0)),
        scratch_shapes=[pltpu.VMEM((2, LOCAL_ROWS, D_MODEL), F32), pltpu.SemaphoreType.DMA((2,))],
    )
    return pl.pallas_call(
        functools.partial(_combine_kernel, tiles, tile_base),
        grid_spec=grid_spec,
        out_shape=jax.ShapeDtypeStruct((tokens, D_MODEL), F32),
        compiler_params=pltpu.CompilerParams(
            dimension_semantics=("arbitrary",), vmem_limit_bytes=VMEM_LIMIT),
        name="combine",
    )(*runs, x1, route, mod4, final_g, ys)


def _routing_tables(counts):
    counts = counts.astype(jnp.int32)
    tiles = counts.shape[0]
    run_len = ((counts + SUBLANES - 1) // SUBLANES) * SUBLANES
    run_src = jnp.cumsum(run_len, axis=1) - run_len
    group_rows = jnp.sum(run_len, axis=0)
    padded = ((group_rows + GROUP_TILE - 1) // GROUP_TILE) * GROUP_TILE
    ends = jnp.cumsum(padded)
    offs = ends - padded
    run_dst = offs[None, :] + jnp.cumsum(run_len, axis=0) - run_len

    n_used = ends[-1] // GROUP_TILE
    max_rows = tiles * (OUTPROJ_TILE + N_GROUPS * (SUBLANES - 1))
    max_tiles = -(-max_rows // GROUP_TILE) + N_GROUPS
    tile_ids = jnp.minimum(jnp.arange(max_tiles, dtype=jnp.int32), n_used - 1)
    tile_group = jnp.sum(tile_ids[:, None] * GROUP_TILE >= ends[None, :], axis=1).astype(jnp.int32)
    of_group = (tile_group[:, None] == jnp.arange(N_GROUPS, dtype=jnp.int32))[:, None, :]
    start = jnp.sum(jnp.where(of_group, run_dst[None], 0), axis=-1)
    stop = start + jnp.sum(jnp.where(of_group, run_len[None], 0), axis=-1)
    row0 = (tile_ids * GROUP_TILE)[:, None]
    first = jnp.sum(stop <= row0, axis=1).astype(jnp.int32)
    last = jnp.sum(start < row0 + GROUP_TILE, axis=1).astype(jnp.int32) - 1
    group_end = jnp.sum(jnp.where(of_group[:, 0, :], (offs + group_rows)[None, :], 0), axis=-1)
    fill = jnp.clip(group_end - row0[:, 0], 0, GROUP_TILE).astype(jnp.int32)
    flat = lambda a: a.reshape(-1)
    return ((tile_group, n_used.reshape(1), first, last, fill), (flat(run_len), flat(run_src), flat(run_dst)),
            max_tiles * GROUP_TILE)


def _mixer(x, mod4, mod_row, is_grid, s_f0, s_b0, p):
    norm_mix_g, w_in_bf, conv_w, decay_rows = p
    y_conv, *qkvg = _inproj(x, mod4, mod_row, norm_mix_g, w_in_bf, conv_w, is_grid)
    per_seq = lambda a: a.reshape(x.shape[0], x.shape[1], a.shape[-1])
    ret = _retention(*map(per_seq, qkvg), decay_rows, s_f0, s_b0, emit_state=not is_grid)
    return y_conv, ret[0].reshape(-1, RET_W), ret[1:]


def kernel(x_prompt, x_sample, state_ret_fwd, state_ret_bwd, c, c_ctx, norm_mix_g, norm_ffn_g, w_ada, b_ada, w_in, conv_w, ret_decay_fwd, ret_decay_bwd, w_out, w_router_group, b_router_group, w_router_expert, b_router_expert, w_gate_e, w_up_e, w_down_e, final_norm_g):
    assert norm_mix_g.shape[0] == 1, "single-layer backbone"
    n_lat = c.shape[0]
    ctx_row = n_lat
    mod_rows = 8
    cvec = jnp.concatenate([c, c_ctx[None, :], jnp.zeros((mod_rows - n_lat - 1, D_MODEL), F32)], axis=0)
    mod = _modulation(cvec, w_ada[0], b_ada[0][None, :])
    mod4 = mod.reshape(mod_rows, 6, 1, D_MODEL)

    pad = ROUTER_COLS - N_GROUPS - N_EXPERTS
    w_router = jnp.concatenate(
        [w_router_group[0], w_router_expert[0], jnp.zeros((D_MODEL, pad), F32)], axis=1).astype(BF16)
    b_router = jnp.concatenate([b_router_group[0], b_router_expert[0], jnp.zeros((pad,), F32)])[None, :]
    decay_rows = jnp.broadcast_to(
        jnp.concatenate([ret_decay_fwd[0], ret_decay_bwd[0]])[:, None], (2 * RET_HEADS, LANES))
    p_mix = (norm_mix_g, w_in[0].astype(BF16), conv_w[0], decay_rows)
    w_out_bf = w_out[0].astype(BF16)
    final_g = final_norm_g[None, :]

    ctx_tokens = x_prompt.shape[0] * x_prompt.shape[1]
    lat_tokens = x_sample.shape[0] * x_sample.shape[1]
    ctx_tiles = ctx_tokens // OUTPROJ_TILE
    lat_tiles_per_seq = x_sample.shape[1] // OUTPROJ_TILE
    tile_mod = lambda i: jnp.where(i < ctx_tiles, ctx_row, (i - ctx_tiles) // lat_tiles_per_seq)
    flat = lambda a: a.reshape(-1, a.shape[-1])

    yc_c, yr_c, (s_f, s_b) = _mixer(x_prompt, mod4, lambda b: ctx_row, False, None, None, p_mix)
    yc_l, yr_l, _ = _mixer(x_sample, mod4, lambda b: b, True, state_ret_fwd, state_ret_bwd, p_mix)

    x1, xloc, route, cnt = _outproj((yc_c, yr_c, flat(x_prompt)), (yc_l, yr_l, flat(x_sample)), mod4, tile_mod,
                                    norm_ffn_g, w_out_bf, w_router, b_router)
    tile_tables, runs, slots = _routing_tables(cnt[:, 0, :N_GROUPS])
    ys = _experts(xloc, tile_tables, runs, slots, w_gate_e[0], w_up_e[0], w_down_e[0])
    y_prompt = _combine(ys, runs, x1, route, ctx_tokens, mod4, tile_mod, final_g, 0)
    y_sample = _combine(ys, runs, x1, route, lat_tokens, mod4, tile_mod, final_g, ctx_tiles)
    return (y_prompt.reshape(x_prompt.shape), y_sample.reshape(x_sample.shape),
            s_f.astype(x_prompt.dtype), s_b.astype(x_prompt.dtype))
```

```python
import functools

import jax
import jax.numpy as jnp
from jax import lax
from jax.experimental import pallas as pl
from jax.experimental.pallas import tpu as pltpu

F32 = jnp.float32
BF16 = jnp.bfloat16

D_MODEL = 1024
GRID_W = 64
CONV_W = 512
RET_HEADS = 4
RET_DK = 128
RET_DV = 128
RET_W = RET_HEADS * RET_DV
QK_W = RET_HEADS * RET_DK
CHUNK = 128
N_GROUPS = 4
EXPERTS_PER_GROUP = 8
N_EXPERTS = N_GROUPS * EXPERTS_PER_GROUP
D_EXPERT = 256
ROPE_BASE = 10000.0
EPS = 1e-6

LANES = 128
TOKEN_TILE = 1024
OUTPROJ_TILE = 512
GROUP_TILE = 1024
EXPERTS_PER_STEP = 4
RET_UNROLL = 8
SUBLANES = 8
XLOC_ROWS = OUTPROJ_TILE + N_GROUPS * SUBLANES
LOCAL_ROWS = OUTPROJ_TILE + LANES
ROW_W = D_MODEL + LANES
ROUTE_GROUP, ROUTE_LOCAL, ROUTE_E1, ROUTE_E2, ROUTE_W1, ROUTE_W2 = range(6)
MOD_COLS = 1536
ROUTER_COLS = LANES
VMEM_LIMIT = 48 * 1024 * 1024
EXPERT_VMEM_LIMIT = 56 * 1024 * 1024


def _silu(x):
    return x * jax.nn.sigmoid(x)


def _rms(x):
    return x * lax.rsqrt(jnp.mean(x * x, axis=-1, keepdims=True) + EPS)


def _bdot(a, b):
    return jnp.dot(a.astype(BF16), b.astype(BF16), preferred_element_type=F32)


def _split3(x):
    hi = x.astype(BF16)
    rest = x - hi.astype(F32)
    mid = rest.astype(BF16)
    return hi, mid, (rest - mid.astype(F32)).astype(BF16)


def _mod_kernel(c_ref, w_ref, b_ref, o_ref):
    o_ref[...] = _bdot(_silu(c_ref[...]), w_ref[...]) + b_ref[...]


def _modulation(cvec, w_ada, b_ada):
    rows = cvec.shape[0]
    n = w_ada.shape[1]
    return pl.pallas_call(
        _mod_kernel,
        grid=(n // MOD_COLS,),
        in_specs=[
            pl.BlockSpec((rows, D_MODEL), lambda j: (0, 0)),
            pl.BlockSpec((D_MODEL, MOD_COLS), lambda j: (0, j)),
            pl.BlockSpec((1, MOD_COLS), lambda j: (0, j)),
        ],
        out_specs=pl.BlockSpec((rows, MOD_COLS), lambda j: (0, j)),
        out_shape=jax.ShapeDtypeStruct((rows, n), F32),
        compiler_params=pltpu.CompilerParams(vmem_limit_bytes=VMEM_LIMIT),
        name="modulation",
    )(cvec, w_ada, b_ada)


def _inproj_kernel(seg, is_grid, x_ref, sh_ref, sc_ref, ng_ref, w_ref, cw_ref, *rest):
    if is_grid:
        cos_ref, sa_ref, sb_ref, yc_ref, q_ref, k_ref, v_ref, g_ref = rest
    else:
        yc_ref, q_ref, k_ref, v_ref, g_ref = rest
    x = x_ref[...]
    xn = (_rms(x) * ng_ref[...]) * (1.0 + sc_ref[0, 0]) + sh_ref[0, 0]
    xb = xn.astype(BF16)

    def proj(c0, n):
        return jnp.dot(xb, w_ref[:, c0:c0 + n], preferred_element_type=F32)

    gate_b = proj(0, CONV_W)
    u = proj(CONV_W, CONV_W) * proj(2 * CONV_W, CONV_W)
    rows = u.shape[0]
    pos = lax.broadcasted_iota(jnp.int32, u.shape, 0) & (seg - 1)
    u_prev = jnp.where(pos != 0, pltpu.roll(u, 1, 0), 0.0)
    u_next = jnp.where(pos != seg - 1, pltpu.roll(u, rows - 1, 0), 0.0)
    conv = cw_ref[0:1, :] * u_prev + cw_ref[1:2, :] * u + cw_ref[2:3, :] * u_next
    yc_ref[...] = (gate_b * conv).astype(yc_ref.dtype)

    q0 = 3 * CONV_W
    q = proj(q0, QK_W)
    k = proj(q0 + QK_W, QK_W)
    if is_grid:
        cos, sa, sb = cos_ref[...], sa_ref[...], sb_ref[...]

        def rope(t):
            out = []
            for h in range(RET_HEADS):
                th = t[:, h * RET_DK:(h + 1) * RET_DK]
                out.append(th * cos + pltpu.roll(th, RET_DK - 1, 1) * sa + pltpu.roll(th, 1, 1) * sb)
            return jnp.concatenate(out, axis=1)

        q, k = rope(q), rope(k)
    q_ref[...] = q
    k_ref[...] = k
    v_ref[...] = proj(q0 + 2 * QK_W, RET_W)
    g_ref[...] = proj(q0 + 2 * QK_W + RET_W, RET_W)


def _rope_tables(length):
    pos = jnp.arange(length)
    row = (pos // GRID_W).astype(F32)
    col = (pos % GRID_W).astype(F32)
    n_pairs = RET_DK // 4
    freqs = ROPE_BASE ** (-(jnp.arange(n_pairs, dtype=F32) * 2.0 / (RET_DK // 2)))
    ang = jnp.concatenate([row[:, None] * freqs, col[:, None] * freqs], axis=-1)
    cos = jnp.repeat(jnp.cos(ang), 2, axis=-1)
    sin = jnp.repeat(jnp.sin(ang), 2, axis=-1)
    even = (jnp.arange(RET_DK) % 2) == 0
    return cos, jnp.where(even, -sin, 0.0), jnp.where(even, 0.0, sin)


def _inproj(x, mod4, mod_row, norm_g, w_in_bf, conv_w, is_grid):
    bsz, length, _ = x.shape
    seg = GRID_W if is_grid else length
    assert TOKEN_TILE % seg == 0 and (length % TOKEN_TILE == 0 or TOKEN_TILE % length == 0)
    tokens = bsz * length
    tiles_per_seq = max(length // TOKEN_TILE, 1)
    seqs_per_tile = max(TOKEN_TILE // length, 1)
    batch_of = lambda i: (i // tiles_per_seq) * seqs_per_tile

    def mod_spec(which):
        return pl.BlockSpec((1, 1, 1, D_MODEL), lambda i: (mod_row(batch_of(i)), which, 0, 0))

    def tok_spec(width):
        return pl.BlockSpec((TOKEN_TILE, width), lambda i: (i, 0))

    in_specs = [
        tok_spec(D_MODEL), mod_spec(0), mod_spec(1),
        pl.BlockSpec((1, D_MODEL), lambda i: (0, 0)),
        pl.BlockSpec(w_in_bf.shape, lambda i: (0, 0)),
        pl.BlockSpec(conv_w.shape, lambda i: (0, 0)),
    ]
    args = [x.reshape(tokens, D_MODEL), mod4, mod4, norm_g, w_in_bf, conv_w]
    if is_grid:
        assert length % TOKEN_TILE == 0
        in_specs += [pl.BlockSpec((TOKEN_TILE, RET_DK), lambda i: (i % tiles_per_seq, 0))] * 3
        args += list(_rope_tables(length))
    shp = lambda w, dt: jax.ShapeDtypeStruct((tokens, w), dt)
    return pl.pallas_call(
        functools.partial(_inproj_kernel, seg, is_grid),
        grid=(tokens // TOKEN_TILE,),
        in_specs=in_specs,
        out_specs=[tok_spec(CONV_W), tok_spec(QK_W), tok_spec(QK_W), tok_spec(RET_W), tok_spec(RET_W)],
        out_shape=[shp(CONV_W, BF16), shp(QK_W, F32), shp(QK_W, F32), shp(RET_W, F32), shp(RET_W, F32)],
        compiler_params=pltpu.CompilerParams(
            dimension_semantics=("parallel",), vmem_limit_bytes=VMEM_LIMIT),
        name="inproj_grid" if is_grid else "inproj_seq",
    )(*args)


def _ret_kernel(n_chunks, heads, has_init, emit_state, a_ref, q_ref, k_ref, v_ref, g_ref, *rest):
    rest = list(rest)
    if has_init:
        sf0_ref, sb0_ref = rest[:2]
        rest = rest[2:]
    y_ref = rest.pop(0)
    if emit_state:
        sf_out, sb_out = rest[:2]
        rest = rest[2:]
    st_f, st_b, dec = rest
    c = CHUNK
    sq = (c, c)
    head0 = pl.program_id(0) * heads

    def log_decays(hh):
        lg_f = jnp.log1p(-jnp.exp(a_ref[pl.ds(head0 + hh, 1), :]))
        lg_b = jnp.log1p(-jnp.exp(a_ref[pl.ds(head0 + hh + RET_HEADS, 1), :]))
        return lg_f, lg_b

    @pl.when(pl.program_id(1) == 0)
    def _():
        row = lax.broadcasted_iota(jnp.int32, sq, 0).astype(F32)
        col = lax.broadcasted_iota(jnp.int32, sq, 1).astype(F32)
        scale = RET_DK ** -0.5
        for hh in range(heads):
            lg_f, lg_b = log_decays(hh)
            dec[hh, 0] = scale * (
                jnp.where(row >= col, jnp.exp(jnp.where(row >= col, row - col, 0.0) * lg_f), 0.0)
                + jnp.where(col >= row, jnp.exp(jnp.where(col >= row, col - row, 0.0) * lg_b), 0.0))
            dec[hh, 1] = jnp.exp((row + 1.0) * lg_f)
            dec[hh, 2] = jnp.exp((c - row) * lg_b)
            dec[hh, 3] = scale * jnp.exp((c - 1.0 - col) * lg_f)
            dec[hh, 4] = scale * jnp.exp(col * lg_b)

    def rows(n):
        return pl.ds(pl.multiple_of(n * c, c), c) if not isinstance(n, int) else pl.ds(n * c, c)

    def cols(hh):
        return slice(hh * RET_DK, (hh + 1) * RET_DK)

    def kv_step(hh, n):
        kt = jnp.transpose(k_ref[0, rows(n), cols(hh)])
        lhs = jnp.concatenate([kt * dec[hh, 3], kt * dec[hh, 4]], axis=0)
        kv = _bdot(lhs, v_ref[0, rows(n), cols(hh)])
        st_f[hh, n] = kv[:RET_DK]
        st_b[hh, n] = kv[RET_DK:]

    def scan(hh, st, decay, order, s):
        def step(i, s):
            n = order(i)
            kv = st[hh, n]
            st[hh, n] = s
            return s * decay + kv
        if n_chunks <= RET_UNROLL:
            for i in range(n_chunks):
                s = step(i, s)
            return s
        return lax.fori_loop(0, n_chunks, step, s, unroll=RET_UNROLL)

    def out_step(hh, n):
        q = q_ref[0, rows(n), cols(hh)]
        scores = lax.dot_general(q.astype(BF16), k_ref[0, rows(n), cols(hh)].astype(BF16),
                                 (((1,), (1,)), ((), ())), preferred_element_type=F32)
        o = _bdot(scores * dec[hh, 0], v_ref[0, rows(n), cols(hh)])
        q_dec = jnp.concatenate([q * dec[hh, 1], q * dec[hh, 2]], axis=1)
        o = o + _bdot(q_dec, jnp.concatenate([st_f[hh, n], st_b[hh, n]], axis=0))
        y = _silu(g_ref[0, rows(n), cols(hh)]) * _rms(o)
        y_ref[0, rows(n), cols(hh)] = y.astype(y_ref.dtype)

    def over_chunks(step):
        if n_chunks * heads <= RET_UNROLL:
            for hh in range(heads):
                for n in range(n_chunks):
                    step(hh, n)
        else:
            for hh in range(heads):
                lax.fori_loop(0, n_chunks, lambda n, carry: (step(hh, n), carry)[1], 0, unroll=RET_UNROLL)

    over_chunks(kv_step)
    finals = []
    for hh in range(heads):
        lg_f, lg_b = log_decays(hh)
        s_f = sf0_ref[0, 0, hh] if has_init else jnp.zeros(sq, F32)
        s_b = sb0_ref[0, 0, hh] if has_init else jnp.zeros(sq, F32)
        s_f = scan(hh, st_f, jnp.exp(c * lg_f), lambda i: i, s_f)
        s_b = scan(hh, st_b, jnp.exp(c * lg_b), lambda i: n_chunks - 1 - i, s_b)
        finals.append((s_f, s_b))
    over_chunks(out_step)
    if emit_state:
        for hh, (s_f, s_b) in enumerate(finals):
            sf_out[0, 0, hh] = s_f
            sb_out[0, 0, hh] = s_b


def _retention(q, k, v, g, decay_rows, s_f0, s_b0, emit_state):
    bsz, length, _ = q.shape
    n_chunks = length // CHUNK
    has_init = s_f0 is not None
    heads = RET_HEADS if n_chunks * RET_HEADS <= RET_UNROLL else 1
    head_spec = pl.BlockSpec((1, length, heads * RET_DK), lambda h, b: (b, 0, h))
    st_spec = pl.BlockSpec((1, 1, heads, RET_DK, RET_DV), lambda h, b: (b, 0, h, 0, 0))
    in_specs = [pl.BlockSpec(decay_rows.shape, lambda h, b: (0, 0))] + [head_spec] * 4
    args = [decay_rows, q, k, v, g]
    if has_init:
        in_specs += [st_spec, st_spec]
        args += [s_f0, s_b0]
    out_specs = [head_spec]
    out_shape = [jax.ShapeDtypeStruct((bsz, length, RET_W), BF16)]
    if emit_state:
        st_shape = jax.ShapeDtypeStruct((bsz, 1, RET_HEADS, RET_DK, RET_DV), F32)
        out_specs += [st_spec, st_spec]
        out_shape += [st_shape, st_shape]
    return pl.pallas_call(
        functools.partial(_ret_kernel, n_chunks, heads, has_init, emit_state),
        grid=(RET_HEADS // heads, bsz),
        in_specs=in_specs,
        out_specs=out_specs,
        out_shape=out_shape,
        scratch_shapes=[
            pltpu.VMEM((heads, n_chunks, RET_DK, RET_DV), F32),
            pltpu.VMEM((heads, n_chunks, RET_DK, RET_DV), F32),
            pltpu.VMEM((heads, 5, CHUNK, CHUNK), F32),
        ],
        compiler_params=pltpu.CompilerParams(
            dimension_semantics=("arbitrary", "arbitrary"), vmem_limit_bytes=VMEM_LIMIT),
        name="retention_init" if has_init else "retention_zero",
    )(*args)


def _route(logits):
    lane = lax.broadcasted_iota(jnp.int32, logits.shape, 1)
    lane_f = lane.astype(F32)
    neg = -jnp.inf
    far = float(LANES)
    is_g = lane < N_GROUPS
    lg = jnp.where(is_g, logits, neg)
    g_max = jnp.max(lg, axis=1, keepdims=True)
    g_idx = jnp.min(jnp.where(lg == g_max, lane_f, far), axis=1, keepdims=True)
    p_sel = 1.0 / jnp.sum(jnp.where(is_g, jnp.exp(lg - g_max), 0.0), axis=1, keepdims=True)
    lane_group = ((lane - N_GROUPS) >> (EXPERTS_PER_GROUP.bit_length() - 1)).astype(F32)
    sel = (lane >= N_GROUPS) & (lane < N_GROUPS + N_EXPERTS) & (lane_group == g_idx)
    le = jnp.where(sel, logits, neg)
    v1 = jnp.max(le, axis=1, keepdims=True)
    i1 = jnp.min(jnp.where(le == v1, lane_f, far), axis=1, keepdims=True)
    le2 = jnp.where(lane_f == i1, neg, le)
    v2 = jnp.max(le2, axis=1, keepdims=True)
    i2 = jnp.min(jnp.where(le2 == v2, lane_f, far), axis=1, keepdims=True)
    e2 = jnp.exp(v2 - v1)
    w1 = p_sel * (1.0 / (1.0 + e2))
    w2 = p_sel * (e2 / (1.0 + e2))
    return lane, lane_f, g_idx, i1, i2, w1, w2


def _outproj_kernel(ctx_tiles, yc_c, yr_c, x_c, yc_l, yr_l, x_l, *rest):
    @pl.when(pl.program_id(0) < ctx_tiles)
    def _():
        _outproj_tile(yc_c, yr_c, x_c, *rest)

    @pl.when(pl.program_id(0) >= ctx_tiles)
    def _():
        _outproj_tile(yc_l, yr_l, x_l, *rest)


def _outproj_tile(yc_ref, yr_ref, x_ref, g1_ref, sh_ref, sc_ref, ng_ref, wo_ref, wr_ref, br_ref,
                  x1_ref, xloc_ref, route_ref, cnt_ref):
    m = (jnp.dot(yc_ref[...], wo_ref[0:CONV_W, :], preferred_element_type=F32)
         + jnp.dot(yr_ref[...], wo_ref[CONV_W:, :], preferred_element_type=F32))
    x1 = x_ref[...] + g1_ref[0, 0] * m
    x1_ref[...] = x1
    xn = (_rms(x1) * ng_ref[...]) * (1.0 + sc_ref[0, 0]) + sh_ref[0, 0]
    xb = xn.astype(BF16)
    logits = jnp.dot(xb, wr_ref[...], preferred_element_type=F32) + br_ref[...]
    lane, lane_f, g_idx, i1, i2, w1, w2 = _route(logits)

    picks = jnp.where(lane_f == g_idx, 1.0, 0.0)
    rows = picks.shape[0]
    tri = (lax.broadcasted_iota(jnp.int32, (rows, rows), 0)
           > lax.broadcasted_iota(jnp.int32, (rows, rows), 1))
    before = jnp.dot(jnp.where(tri, 1.0, 0.0).astype(BF16), picks.astype(BF16),
                     preferred_element_type=F32)
    count = jnp.sum(picks, axis=0, keepdims=True)
    cnt_ref[0] = count
    count8 = jnp.broadcast_to(jnp.floor((count + (SUBLANES - 1)) * (1.0 / SUBLANES)) * SUBLANES,
                              (SUBLANES, LANES))
    lane8 = lane[:SUBLANES]
    start = sum(jnp.where(lane8 >= k, pltpu.roll(count8, k, 1), 0.0) for k in range(1, N_GROUPS))
    local = jnp.sum(jnp.where(lane_f == g_idx, before + start[0:1], 0.0), axis=1, keepdims=True)
    route = jnp.where(lane == ROUTE_GROUP, g_idx, jnp.where(lane == ROUTE_LOCAL, local, jnp.where(
        lane == ROUTE_E1, i1 - N_GROUPS, jnp.where(lane == ROUTE_E2, i2 - N_GROUPS, jnp.where(
            lane == ROUTE_W1, w1, jnp.where(lane == ROUTE_W2, w2, 0.0))))))
    route_ref[...] = route

    local_row = jnp.transpose(jnp.broadcast_to(local, (rows, LANES)))[0:1, :]
    place = jnp.where(lax.broadcasted_iota(jnp.int32, (XLOC_ROWS, rows), 0).astype(F32) == local_row,
                      1.0, 0.0).astype(BF16)
    xloc_ref[0, :, :D_MODEL] = jnp.dot(place, xb, preferred_element_type=F32)
    xloc_ref[0, :, D_MODEL:] = sum(
        jnp.dot(place, piece, preferred_element_type=F32) for piece in _split3(route))


def _outproj(ctx, lat, mod4, mod_row_of_tile, norm_g, w_out_bf, w_router_bf, b_router):
    ctx_tiles = ctx[2].shape[0] // OUTPROJ_TILE
    tiles = ctx_tiles + lat[2].shape[0] // OUTPROJ_TILE
    tokens = tiles * OUTPROJ_TILE

    def mod_spec(which):
        return pl.BlockSpec((1, 1, 1, D_MODEL), lambda i: (mod_row_of_tile(i), which, 0, 0))

    ctx_tok = lambda w: pl.BlockSpec((OUTPROJ_TILE, w), lambda i: (jnp.minimum(i, ctx_tiles - 1), 0))
    lat_tok = lambda w: pl.BlockSpec((OUTPROJ_TILE, w), lambda i: (jnp.maximum(i - ctx_tiles, 0), 0))
    tok = lambda w: pl.BlockSpec((OUTPROJ_TILE, w), lambda i: (i, 0))
    full = lambda a: pl.BlockSpec(a.shape, lambda i: (0,) * a.ndim)
    widths = (CONV_W, RET_W, D_MODEL)
    return pl.pallas_call(
        functools.partial(_outproj_kernel, ctx_tiles),
        grid=(tiles,),
        in_specs=[ctx_tok(w) for w in widths] + [lat_tok(w) for w in widths] + [
            mod_spec(2), mod_spec(3), mod_spec(4),
            full(norm_g), full(w_out_bf), full(w_router_bf), full(b_router)],
        out_specs=[tok(D_MODEL),
                   pl.BlockSpec((1, XLOC_ROWS, ROW_W), lambda i: (i, 0, 0)),
                   tok(ROUTER_COLS),
                   pl.BlockSpec((1, 1, ROUTER_COLS), lambda i: (i, 0, 0))],
        out_shape=[jax.ShapeDtypeStruct((tokens, D_MODEL), F32),
                   jax.ShapeDtypeStruct((tiles, XLOC_ROWS, ROW_W), F32),
                   jax.ShapeDtypeStruct((tokens, ROUTER_COLS), F32),
                   jax.ShapeDtypeStruct((tiles, 1, ROUTER_COLS), F32)],
        compiler_params=pltpu.CompilerParams(
            dimension_semantics=("parallel",), vmem_limit_bytes=VMEM_LIMIT),
        name="outproj",
    )(*ctx, *lat, mod4, mod4, mod4, norm_g, w_out_bf, w_router_bf, b_router)


RUN_PIECES = tuple(SUBLANES << b for b in reversed(range((OUTPROJ_TILE // SUBLANES).bit_length())))


def _expert_kernel(tile_group_ref, n_used_ref, first_ref, last_ref, fill_ref,
                   run_len_ref, run_src_ref, run_dst_ref,
                   xloc_hbm, w1_ref, w3_ref, w2_ref, ys_ref, xbuf, xb, gate_tabs, sem):
    j = pl.program_id(0)
    step = pl.program_id(1)
    n_used = n_used_ref[0]

    def tile_copies(tile, act):
        slot = tile % 2
        group = tile_group_ref[tile]
        row0 = tile * GROUP_TILE

        def from_token_tile(b, carry):
            run = b * N_GROUPS + group
            lo = jnp.maximum(run_dst_ref[run], row0)
            hi = jnp.minimum(run_dst_ref[run] + run_len_ref[run], row0 + GROUP_TILE)
            n = jnp.maximum(hi - lo, 0)
            src = run_src_ref[run] + lo - run_dst_ref[run]
            dst = lo - row0
            for size in RUN_PIECES:
                done = n & (-2 * size)

                @pl.when((n & size) != 0)
                def _():
                    act(pltpu.make_async_copy(
                        xloc_hbm.at[b, pl.ds(pl.multiple_of(src + done, SUBLANES), size)],
                        xbuf.at[slot, pl.ds(pl.multiple_of(dst + done, SUBLANES), size)], sem.at[slot]))
            return carry

        lax.fori_loop(first_ref[tile], last_ref[tile] + 1, from_token_tile, 0)

    start = lambda cp: cp.start()
    wait = lambda cp: cp.wait()

    @pl.when(j < n_used)
    def _():
        @pl.when(step == 0)
        def _():
            @pl.when(j == 0)
            def _():
                xbuf[...] = jnp.zeros_like(xbuf)
                tile_copies(j, start)

            tile_copies(j, wait)
            rows_in = xbuf[j % 2]
            xb[...] = rows_in[:, :D_MODEL].astype(BF16)
            route = rows_in[:, D_MODEL:]
            lane = lax.broadcasted_iota(jnp.int32, route.shape, 1)
            for n, which in enumerate((ROUTE_E1, ROUTE_E2, ROUTE_W1, ROUTE_W2)):
                col = jnp.sum(jnp.where(lane == which, route, 0.0), axis=1, keepdims=True)
                gate_tabs[n] = jnp.broadcast_to(col, route.shape)

            @pl.when(j + 1 < n_used)
            def _():
                tile_copies(j + 1, start)

        def evaluate(rows):
            x = xb[:rows]
            total = None
            for s in range(EXPERTS_PER_STEP):
                expert = (tile_group_ref[j] * EXPERTS_PER_GROUP + step * EXPERTS_PER_STEP + s).astype(F32)
                gate = (jnp.where(gate_tabs[0, :rows] == expert, gate_tabs[2, :rows], 0.0)
                        + jnp.where(gate_tabs[1, :rows] == expert, gate_tabs[3, :rows], 0.0))
                hid = _silu(jnp.dot(x, w1_ref[0, s].astype(BF16), preferred_element_type=F32)) * jnp.dot(
                    x, w3_ref[0, s].astype(BF16), preferred_element_type=F32)
                y = jnp.dot(hid.astype(BF16), w2_ref[0, s].astype(BF16), preferred_element_type=F32)
                gated = jnp.concatenate(
                    [gate * y[:, c * LANES:(c + 1) * LANES] for c in range(D_MODEL // LANES)], axis=1)
                total = gated if total is None else total + gated

            @pl.when(step == 0)
            def _():
                ys_ref[:rows] = total

            @pl.when(step > 0)
            def _():
                ys_ref[:rows] += total

        half = GROUP_TILE // 2

        @pl.when(fill_ref[j] > half)
        def _():
            evaluate(GROUP_TILE)

        @pl.when(fill_ref[j] <= half)
        def _():
            evaluate(half)


def _experts(xloc, tile_tables, runs, slots, w1, w3, w2):
    steps = EXPERTS_PER_GROUP // EXPERTS_PER_STEP
    paired = lambda w: w.reshape((N_EXPERTS // EXPERTS_PER_STEP, EXPERTS_PER_STEP) + w.shape[1:])
    w_spec = lambda shape: pl.BlockSpec((1, EXPERTS_PER_STEP) + shape, lambda j, s, tg, nu, *_: (
        tg[jnp.minimum(j, nu[0] - 1)] * steps + jnp.where(j < nu[0], s, steps - 1), 0, 0, 0))
    grid_spec = pltpu.PrefetchScalarGridSpec(
        num_scalar_prefetch=8,
        grid=(slots // GROUP_TILE, steps),
        in_specs=[
            pl.BlockSpec(memory_space=pl.ANY),
            w_spec((D_MODEL, D_EXPERT)), w_spec((D_MODEL, D_EXPERT)), w_spec((D_EXPERT, D_MODEL)),
        ],
        out_specs=pl.BlockSpec((GROUP_TILE, D_MODEL), lambda j, s, tg, nu, *_: (jnp.minimum(j, nu[0] - 1), 0)),
        scratch_shapes=[pltpu.VMEM((2, GROUP_TILE, ROW_W), F32), pltpu.VMEM((GROUP_TILE, D_MODEL), BF16),
                        pltpu.VMEM((4, GROUP_TILE, LANES), F32), pltpu.SemaphoreType.DMA((2,))],
    )
    return pl.pallas_call(
        _expert_kernel,
        grid_spec=grid_spec,
        out_shape=jax.ShapeDtypeStruct((slots, D_MODEL), F32),
        compiler_params=pltpu.CompilerParams(
            dimension_semantics=("arbitrary", "arbitrary"), vmem_limit_bytes=EXPERT_VMEM_LIMIT),
        name="experts",
    )(*tile_tables, *runs, xloc, paired(w1), paired(w3), paired(w2))


def _combine_kernel(n_tiles, tile_base, run_len_ref, run_src_ref, run_dst_ref,
                    x1_ref, route_ref, g2_ref, fg_ref, ys_hbm, o_ref, buf, sem):
    i = pl.program_id(0)
    slot = i % 2

    def run_copies(local_tile, act):
        s = local_tile % 2
        tile = local_tile + tile_base
        for g in range(N_GROUPS):
            n = run_len_ref[tile * N_GROUPS + g]
            src = run_src_ref[tile * N_GROUPS + g]
            dst = run_dst_ref[tile * N_GROUPS + g]
            for size in RUN_PIECES:
                done = n & (-2 * size)

                @pl.when((n & size) != 0)
                def _():
                    act(pltpu.make_async_copy(
                        ys_hbm.at[pl.ds(pl.multiple_of(dst + done, SUBLANES), size)],
                        buf.at[s, pl.ds(pl.multiple_of(src + done, SUBLANES), size)], sem.at[s]))

    @pl.when(i == 0)
    def _():
        buf[...] = jnp.zeros_like(buf)
        run_copies(i, lambda cp: cp.start())

    @pl.when(i + 1 < n_tiles)
    def _():
        run_copies(i + 1, lambda cp: cp.start())

    run_copies(i, lambda cp: cp.wait())
    route = route_ref[...]
    lane = lax.broadcasted_iota(jnp.int32, route.shape, 1)
    local = jnp.sum(jnp.where(lane == ROUTE_LOCAL, route, 0.0), axis=1, keepdims=True)
    pick = jnp.where(lax.broadcasted_iota(jnp.int32, (route.shape[0], LOCAL_ROWS), 1).astype(F32) == local,
                     1.0, 0.0).astype(BF16)
    moe = sum(jnp.dot(pick, piece, preferred_element_type=F32) for piece in _split3(buf[slot])[:2])
    y = x1_ref[...] + g2_ref[0, 0] * moe
    o_ref[...] = _rms(y) * fg_ref[...]


def _combine(ys, runs, x1, route, tokens, mod4, mod_row_of_tile, final_g, tile_base):
    tiles = tokens // OUTPROJ_TILE
    tok = lambda w: pl.BlockSpec((OUTPROJ_TILE, w), lambda i, *_: (i + tile_base, 0))
    grid_spec = pltpu.PrefetchScalarGridSpec(
        num_scalar_prefetch=3,
        grid=(tiles,),
        in_specs=[
            tok(D_MODEL), tok(ROUTER_COLS),
            pl.BlockSpec((1, 1, 1, D_MODEL), lambda i, *_: (mod_row_of_tile(i + tile_base), 5, 0, 0)),
            pl.BlockSpec((1, D_MODEL), lambda i, *_: (0, 0)),
            pl.BlockSpec(memory_space=pl.ANY),
        ],
        out_specs=pl.BlockSpec((OUTPROJ_TILE, D_MODEL), lambda i, *_: (i, 0)),
        scratch_shapes=[pltpu.VMEM((2, LOCAL_ROWS, D_MODEL), F32), pltpu.SemaphoreType.DMA((2,))],
    )
    return pl.pallas_call(
        functools.partial(_combine_kernel, tiles, tile_base),
        grid_spec=grid_spec,
        out_shape=jax.ShapeDtypeStruct((tokens, D_MODEL), F32),
        compiler_params=pltpu.CompilerParams(
            dimension_semantics=("arbitrary",), vmem_limit_bytes=VMEM_LIMIT),
        name="combine",
    )(*runs, x1, route, mod4, final_g, ys)


def _routing_tables(counts):
    counts = counts.astype(jnp.int32)
    tiles = counts.shape[0]
    run_len = ((counts + SUBLANES - 1) // SUBLANES) * SUBLANES
    run_src = jnp.cumsum(run_len, axis=1) - run_len
    group_rows = jnp.sum(run_len, axis=0)
    padded = ((group_rows + GROUP_TILE - 1) // GROUP_TILE) * GROUP_TILE
    ends = jnp.cumsum(padded)
    offs = ends - padded
    run_dst = offs[None, :] + jnp.cumsum(run_len, axis=0) - run_len

    n_used = ends[-1] // GROUP_TILE
    max_rows = tiles * (OUTPROJ_TILE + N_GROUPS * (SUBLANES - 1))
    max_tiles = -(-max_rows // GROUP_TILE) + N_GROUPS
    tile_ids = jnp.minimum(jnp.arange(max_tiles, dtype=jnp.int32), n_used - 1)
    tile_group = jnp.sum(tile_ids[:, None] * GROUP_TILE >= ends[None, :], axis=1).astype(jnp.int32)
    of_group = (tile_group[:, None] == jnp.arange(N_GROUPS, dtype=jnp.int32))[:, None, :]
    start = jnp.sum(jnp.where(of_group, run_dst[None], 0), axis=-1)
    stop = start + jnp.sum(jnp.where(of_group, run_len[None], 0), axis=-1)
    row0 = (tile_ids * GROUP_TILE)[:, None]
    first = jnp.sum(stop <= row0, axis=1).astype(jnp.int32)
    last = jnp.sum(start < row0 + GROUP_TILE, axis=1).astype(jnp.int32) - 1
    group_end = jnp.sum(jnp.where(of_group[:, 0, :], (offs + group_rows)[None, :], 0), axis=-1)
    fill = jnp.clip(group_end - row0[:, 0], 0, GROUP_TILE).astype(jnp.int32)
    flat = lambda a: a.reshape(-1)
    return ((tile_group, n_used.reshape(1), first, last, fill), (flat(run_len), flat(run_src), flat(run_dst)),
            max_tiles * GROUP_TILE)


def _mixer(x, mod4, mod_row, is_grid, s_f0, s_b0, p):
    norm_mix_g, w_in_bf, conv_w, decay_rows = p
    y_conv, *qkvg = _inproj(x, mod4, mod_row, norm_mix_g, w_in_bf, conv_w, is_grid)
    per_seq = lambda a: a.reshape(x.shape[0], x.shape[1], a.shape[-1])
    ret = _retention(*map(per_seq, qkvg), decay_rows, s_f0, s_b0, emit_state=not is_grid)
    return y_conv, ret[0].reshape(-1, RET_W), ret[1:]


def kernel(x_prompt, x_sample, state_ret_fwd, state_ret_bwd, c, c_ctx, norm_mix_g, norm_ffn_g, w_ada, b_ada, w_in, conv_w, ret_decay_fwd, ret_decay_bwd, w_out, w_router_group, b_router_group, w_router_expert, b_router_expert, w_gate_e, w_up_e, w_down_e, final_norm_g):
    assert norm_mix_g.shape[0] == 1, "single-layer backbone"
    n_lat = c.shape[0]
    ctx_row = n_lat
    mod_rows = 8
    cvec = jnp.concatenate([c, c_ctx[None, :], jnp.zeros((mod_rows - n_lat - 1, D_MODEL), F32)], axis=0)
    mod = _modulation(cvec, w_ada[0], b_ada[0][None, :])
    mod4 = mod.reshape(mod_rows, 6, 1, D_MODEL)

    pad = ROUTER_COLS - N_GROUPS - N_EXPERTS
    w_router = jnp.concatenate(
        [w_router_group[0], w_router_expert[0], jnp.zeros((D_MODEL, pad), F32)], axis=1).astype(BF16)
    b_router = jnp.concatenate([b_router_group[0], b_router_expert[0], jnp.zeros((pad,), F32)])[None, :]
    decay_rows = jnp.broadcast_to(
        jnp.concatenate([ret_decay_fwd[0], ret_decay_bwd[0]])[:, None], (2 * RET_HEADS, LANES))
    p_mix = (norm_mix_g, w_in[0].astype(BF16), conv_w[0], decay_rows)
    w_out_bf = w_out[0].astype(BF16)
    final_g = final_norm_g[None, :]

    ctx_tokens = x_prompt.shape[0] * x_prompt.shape[1]
    lat_tokens = x_sample.shape[0] * x_sample.shape[1]
    ctx_tiles = ctx_tokens // OUTPROJ_TILE
    lat_tiles_per_seq = x_sample.shape[1] // OUTPROJ_TILE
    tile_mod = lambda i: jnp.where(i < ctx_tiles, ctx_row, (i - ctx_tiles) // lat_tiles_per_seq)
    flat = lambda a: a.reshape(-1, a.shape[-1])

    yc_c, yr_c, (s_f, s_b) = _mixer(x_prompt, mod4, lambda b: ctx_row, False, None, None, p_mix)
    yc_l, yr_l, _ = _mixer(x_sample, mod4, lambda b: b, True, state_ret_fwd, state_ret_bwd, p_mix)

    x1, xloc, route, cnt = _outproj((yc_c, yr_c, flat(x_prompt)), (yc_l, yr_l, flat(x_sample)), mod4, tile_mod,
                                    norm_ffn_g, w_out_bf, w_router, b_router)
    tile_tables, runs, slots = _routing_tables(cnt[:, 0, :N_GROUPS])
    ys = _experts(xloc, tile_tables, runs, slots, w_gate_e[0], w_up_e[0], w_down_e[0])
    y_prompt = _combine(ys, runs, x1, route, ctx_tokens, mod4, tile_mod, final_g, 0)
    y_sample = _combine(ys, runs, x1, route, lat_tokens, mod4, tile_mod, final_g, ctx_tiles)
    return (y_prompt.reshape(x_prompt.shape), y_sample.reshape(x_sample.shape),
            s_f.astype(x_prompt.dtype), s_b.astype(x_prompt.dtype))
```

```python
import functools

import jax
import jax.numpy as jnp
from jax import lax
from jax.experimental import pallas as pl
from jax.experimental.pallas import tpu as pltpu

F32 = jnp.float32
BF16 = jnp.bfloat16

D_MODEL = 1024
GRID_W = 64
CONV_W = 512
RET_HEADS = 4
RET_DK = 128
RET_DV = 128
RET_W = RET_HEADS * RET_DV
QK_W = RET_HEADS * RET_DK
CHUNK = 128
N_GROUPS = 4
EXPERTS_PER_GROUP = 8
N_EXPERTS = N_GROUPS * EXPERTS_PER_GROUP
D_EXPERT = 256
ROPE_BASE = 10000.0
EPS = 1e-6

LANES = 128
TOKEN_TILE = 1024
OUTPROJ_TILE = 512
GROUP_TILE = 1024
EXPERTS_PER_STEP = 4
RET_UNROLL = 8
SUBLANES = 8
XLOC_ROWS = OUTPROJ_TILE + N_GROUPS * SUBLANES
LOCAL_ROWS = OUTPROJ_TILE + LANES
ROW_W = D_MODEL + LANES
ROUTE_GROUP, ROUTE_LOCAL, ROUTE_E1, ROUTE_E2, ROUTE_W1, ROUTE_W2 = range(6)
MOD_COLS = 1536
ROUTER_COLS = LANES
VMEM_LIMIT = 48 * 1024 * 1024
EXPERT_VMEM_LIMIT = 56 * 1024 * 1024


def _silu(x):
    return x * jax.nn.sigmoid(x)


def _rms(x):
    return x * lax.rsqrt(jnp.mean(x * x, axis=-1, keepdims=True) + EPS)


def _bdot(a, b):
    return jnp.dot(a.astype(BF16), b.astype(BF16), preferred_element_type=F32)


def _split3(x):
    hi = x.astype(BF16)
    rest = x - hi.astype(F32)
    mid = rest.astype(BF16)
    return hi, mid, (rest - mid.astype(F32)).astype(BF16)


def _mod_kernel(c_ref, w_ref, b_ref, o_ref):
    o_ref[...] = _bdot(_silu(c_ref[...]), w_ref[...]) + b_ref[...]


def _modulation(cvec, w_ada, b_ada):
    rows = cvec.shape[0]
    n = w_ada.shape[1]
    return pl.pallas_call(
        _mod_kernel,
        grid=(n // MOD_COLS,),
        in_specs=[
            pl.BlockSpec((rows, D_MODEL), lambda j: (0, 0)),
            pl.BlockSpec((D_MODEL, MOD_COLS), lambda j: (0, j)),
            pl.BlockSpec((1, MOD_COLS), lambda j: (0, j)),
        ],
        out_specs=pl.BlockSpec((rows, MOD_COLS), lambda j: (0, j)),
        out_shape=jax.ShapeDtypeStruct((rows, n), F32),
        compiler_params=pltpu.CompilerParams(vmem_limit_bytes=VMEM_LIMIT),
        name="modulation",
    )(cvec, w_ada, b_ada)


def _inproj_kernel(seg, is_grid, x_ref, sh_ref, sc_ref, ng_ref, w_ref, cw_ref, *rest):
    if is_grid:
        cos_ref, sa_ref, sb_ref, yc_ref, q_ref, k_ref, v_ref, g_ref = rest
    else:
        yc_ref, q_ref, k_ref, v_ref, g_ref = rest
    x = x_ref[...]
    xn = (_rms(x) * ng_ref[...]) * (1.0 + sc_ref[0, 0]) + sh_ref[0, 0]
    xb = xn.astype(BF16)

    def proj(c0, n):
        return jnp.dot(xb, w_ref[:, c0:c0 + n], preferred_element_type=F32)

    gate_b = proj(0, CONV_W)
    u = proj(CONV_W, CONV_W) * proj(2 * CONV_W, CONV_W)
    rows = u.shape[0]
    pos = lax.broadcasted_iota(jnp.int32, u.shape, 0) & (seg - 1)
    u_prev = jnp.where(pos != 0, pltpu.roll(u, 1, 0), 0.0)
    u_next = jnp.where(pos != seg - 1, pltpu.roll(u, rows - 1, 0), 0.0)
    conv = cw_ref[0:1, :] * u_prev + cw_ref[1:2, :] * u + cw_ref[2:3, :] * u_next
    yc_ref[...] = (gate_b * conv).astype(yc_ref.dtype)

    q0 = 3 * CONV_W
    q = proj(q0, QK_W)
    k = proj(q0 + QK_W, QK_W)
    if is_grid:
        cos, sa, sb = cos_ref[...], sa_ref[...], sb_ref[...]

        def rope(t):
            out = []
            for h in range(RET_HEADS):
                th = t[:, h * RET_DK:(h + 1) * RET_DK]
                out.append(th * cos + pltpu.roll(th, RET_DK - 1, 1) * sa + pltpu.roll(th, 1, 1) * sb)
            return jnp.concatenate(out, axis=1)

        q, k = rope(q), rope(k)
    q_ref[...] = q
    k_ref[...] = k
    v_ref[...] = proj(q0 + 2 * QK_W, RET_W)
    g_ref[...] = proj(q0 + 2 * QK_W + RET_W, RET_W)


def _rope_tables(length):
    pos = jnp.arange(length)
    row = (pos // GRID_W).astype(F32)
    col = (pos % GRID_W).astype(F32)
    n_pairs = RET_DK // 4
    freqs = ROPE_BASE ** (-(jnp.arange(n_pairs, dtype=F32) * 2.0 / (RET_DK // 2)))
    ang = jnp.concatenate([row[:, None] * freqs, col[:, None] * freqs], axis=-1)
    cos = jnp.repeat(jnp.cos(ang), 2, axis=-1)
    sin = jnp.repeat(jnp.sin(ang), 2, axis=-1)
    even = (jnp.arange(RET_DK) % 2) == 0
    return cos, jnp.where(even, -sin, 0.0), jnp.where(even, 0.0, sin)


def _inproj(x, mod4, mod_row, norm_g, w_in_bf, conv_w, is_grid):
    bsz, length, _ = x.shape
    seg = GRID_W if is_grid else length
    assert TOKEN_TILE % seg == 0 and (length % TOKEN_TILE == 0 or TOKEN_TILE % length == 0)
    tokens = bsz * length
    tiles_per_seq = max(length // TOKEN_TILE, 1)
    seqs_per_tile = max(TOKEN_TILE // length, 1)
    batch_of = lambda i: (i // tiles_per_seq) * seqs_per_tile

    def mod_spec(which):
        return pl.BlockSpec((1, 1, 1, D_MODEL), lambda i: (mod_row(batch_of(i)), which, 0, 0))

    def tok_spec(width):
        return pl.BlockSpec((TOKEN_TILE, width), lambda i: (i, 0))

    in_specs = [
        tok_spec(D_MODEL), mod_spec(0), mod_spec(1),
        pl.BlockSpec((1, D_MODEL), lambda i: (0, 0)),
        pl.BlockSpec(w_in_bf.shape, lambda i: (0, 0)),
        pl.BlockSpec(conv_w.shape, lambda i: (0, 0)),
    ]
    args = [x.reshape(tokens, D_MODEL), mod4, mod4, norm_g, w_in_bf, conv_w]
    if is_grid:
        assert length % TOKEN_TILE == 0
        in_specs += [pl.BlockSpec((TOKEN_TILE, RET_DK), lambda i: (i % tiles_per_seq, 0))] * 3
        args += list(_rope_tables(length))
    shp = lambda w, dt: jax.ShapeDtypeStruct((tokens, w), dt)
    return pl.pallas_call(
        functools.partial(_inproj_kernel, seg, is_grid),
        grid=(tokens // TOKEN_TILE,),
        in_specs=in_specs,
        out_specs=[tok_spec(CONV_W), tok_spec(QK_W), tok_spec(QK_W), tok_spec(RET_W), tok_spec(RET_W)],
        out_shape=[shp(CONV_W, BF16), shp(QK_W, F32), shp(QK_W, F32), shp(RET_W, F32), shp(RET_W, F32)],
        compiler_params=pltpu.CompilerParams(
            dimension_semantics=("parallel",), vmem_limit_bytes=VMEM_LIMIT),
        name="inproj_grid" if is_grid else "inproj_seq",
    )(*args)


def _ret_kernel(n_chunks, heads, has_init, emit_state, a_ref, q_ref, k_ref, v_ref, g_ref, *rest):
    rest = list(rest)
    if has_init:
        sf0_ref, sb0_ref = rest[:2]
        rest = rest[2:]
    y_ref = rest.pop(0)
    if emit_state:
        sf_out, sb_out = rest[:2]
        rest = rest[2:]
    st_f, st_b, dec = rest
    c = CHUNK
    sq = (c, c)
    head0 = pl.program_id(0) * heads

    def log_decays(hh):
        lg_f = jnp.log1p(-jnp.exp(a_ref[pl.ds(head0 + hh, 1), :]))
        lg_b = jnp.log1p(-jnp.exp(a_ref[pl.ds(head0 + hh + RET_HEADS, 1), :]))
        return lg_f, lg_b

    @pl.when(pl.program_id(1) == 0)
    def _():
        row = lax.broadcasted_iota(jnp.int32, sq, 0).astype(F32)
        col = lax.broadcasted_iota(jnp.int32, sq, 1).astype(F32)
        scale = RET_DK ** -0.5
        for hh in range(heads):
            lg_f, lg_b = log_decays(hh)
            dec[hh, 0] = scale * (
                jnp.where(row >= col, jnp.exp(jnp.where(row >= col, row - col, 0.0) * lg_f), 0.0)
                + jnp.where(col >= row, jnp.exp(jnp.where(col >= row, col - row, 0.0) * lg_b), 0.0))
            dec[hh, 1] = jnp.exp((row + 1.0) * lg_f)
            dec[hh, 2] = jnp.exp((c - row) * lg_b)
            dec[hh, 3] = scale * jnp.exp((c - 1.0 - col) * lg_f)
            dec[hh, 4] = scale * jnp.exp(col * lg_b)

    def rows(n):
        return pl.ds(pl.multiple_of(n * c, c), c) if not isinstance(n, int) else pl.ds(n * c, c)

    def cols(hh):
        return slice(hh * RET_DK, (hh + 1) * RET_DK)

    def kv_step(hh, n):
        kt = jnp.transpose(k_ref[0, rows(n), cols(hh)])
        lhs = jnp.concatenate([kt * dec[hh, 3], kt * dec[hh, 4]], axis=0)
        kv = _bdot(lhs, v_ref[0, rows(n), cols(hh)])
        st_f[hh, n] = kv[:RET_DK]
        st_b[hh, n] = kv[RET_DK:]

    def scan(hh, st, decay, order, s):
        def step(i, s):
            n = order(i)
            kv = st[hh, n]
            st[hh, n] = s
            return s * decay + kv
        if n_chunks <= RET_UNROLL:
            for i in range(n_chunks):
                s = step(i, s)
            return s
        return lax.fori_loop(0, n_chunks, step, s, unroll=RET_UNROLL)

    def out_step(hh, n):
        q = q_ref[0, rows(n), cols(hh)]
        scores = lax.dot_general(q.astype(BF16), k_ref[0, rows(n), cols(hh)].astype(BF16),
                                 (((1,), (1,)), ((), ())), preferred_element_type=F32)
        o = _bdot(scores * dec[hh, 0], v_ref[0, rows(n), cols(hh)])
        q_dec = jnp.concatenate([q * dec[hh, 1], q * dec[hh, 2]], axis=1)
        o = o + _bdot(q_dec, jnp.concatenate([st_f[hh, n], st_b[hh, n]], axis=0))
        y = _silu(g_ref[0, rows(n), cols(hh)]) * _rms(o)
        y_ref[0, rows(n), cols(hh)] = y.astype(y_ref.dtype)

    def over_chunks(step):
        if n_chunks * heads <= RET_UNROLL:
            for hh in range(heads):
                for n in range(n_chunks):
                    step(hh, n)
        else:
            for hh in range(heads):
                lax.fori_loop(0, n_chunks, lambda n, carry: (step(hh, n), carry)[1], 0, unroll=RET_UNROLL)

    over_chunks(kv_step)
    finals = []
    for hh in range(heads):
        lg_f, lg_b = log_decays(hh)
        s_f = sf0_ref[0, 0, hh] if has_init else jnp.zeros(sq, F32)
        s_b = sb0_ref[0, 0, hh] if has_init else jnp.zeros(sq, F32)
        s_f = scan(hh, st_f, jnp.exp(c * lg_f), lambda i: i, s_f)
        s_b = scan(hh, st_b, jnp.exp(c * lg_b), lambda i: n_chunks - 1 - i, s_b)
        finals.append((s_f, s_b))
    over_chunks(out_step)
    if emit_state:
        for hh, (s_f, s_b) in enumerate(finals):
            sf_out[0, 0, hh] = s_f
            sb_out[0, 0, hh] = s_b


def _retention(q, k, v, g, decay_rows, s_f0, s_b0, emit_state):
    bsz, length, _ = q.shape
    n_chunks = length // CHUNK
    has_init = s_f0 is not None
    heads = RET_HEADS if n_chunks * RET_HEADS <= RET_UNROLL else 1
    head_spec = pl.BlockSpec((1, length, heads * RET_DK), lambda h, b: (b, 0, h))
    st_spec = pl.BlockSpec((1, 1, heads, RET_DK, RET_DV), lambda h, b: (b, 0, h, 0, 0))
    in_specs = [pl.BlockSpec(decay_rows.shape, lambda h, b: (0, 0))] + [head_spec] * 4
    args = [decay_rows, q, k, v, g]
    if has_init:
        in_specs += [st_spec, st_spec]
        args += [s_f0, s_b0]
    out_specs = [head_spec]
    out_shape = [jax.ShapeDtypeStruct((bsz, length, RET_W), BF16)]
    if emit_state:
        st_shape = jax.ShapeDtypeStruct((bsz, 1, RET_HEADS, RET_DK, RET_DV), F32)
        out_specs += [st_spec, st_spec]
        out_shape += [st_shape, st_shape]
    return pl.pallas_call(
        functools.partial(_ret_kernel, n_chunks, heads, has_init, emit_state),
        grid=(RET_HEADS // heads, bsz),
        in_specs=in_specs,
        out_specs=out_specs,
        out_shape=out_shape,
        scratch_shapes=[
            pltpu.VMEM((heads, n_chunks, RET_DK, RET_DV), F32),
            pltpu.VMEM((heads, n_chunks, RET_DK, RET_DV), F32),
            pltpu.VMEM((heads, 5, CHUNK, CHUNK), F32),
        ],
        compiler_params=pltpu.CompilerParams(
            dimension_semantics=("arbitrary", "arbitrary"), vmem_limit_bytes=VMEM_LIMIT),
        name="retention_init" if has_init else "retention_zero",
    )(*args)


def _route(logits):
    lane = lax.broadcasted_iota(jnp.int32, logits.shape, 1)
    lane_f = lane.astype(F32)
    neg = -jnp.inf
    far = float(LANES)
    is_g = lane < N_GROUPS
    lg = jnp.where(is_g, logits, neg)
    g_max = jnp.max(lg, axis=1, keepdims=True)
    g_idx = jnp.min(jnp.where(lg == g_max, lane_f, far), axis=1, keepdims=True)
    p_sel = 1.0 / jnp.sum(jnp.where(is_g, jnp.exp(lg - g_max), 0.0), axis=1, keepdims=True)
    lane_group = ((lane - N_GROUPS) >> (EXPERTS_PER_GROUP.bit_length() - 1)).astype(F32)
    sel = (lane >= N_GROUPS) & (lane < N_GROUPS + N_EXPERTS) & (lane_group == g_idx)
    le = jnp.where(sel, logits, neg)
    v1 = jnp.max(le, axis=1, keepdims=True)
    i1 = jnp.min(jnp.where(le == v1, lane_f, far), axis=1, keepdims=True)
    le2 = jnp.where(lane_f == i1, neg, le)
    v2 = jnp.max(le2, axis=1, keepdims=True)
    i2 = jnp.min(jnp.where(le2 == v2, lane_f, far), axis=1, keepdims=True)
    e2 = jnp.exp(v2 - v1)
    w1 = p_sel * (1.0 / (1.0 + e2))
    w2 = p_sel * (e2 / (1.0 + e2))
    return lane, lane_f, g_idx, i1, i2, w1, w2


def _outproj_kernel(ctx_tiles, yc_c, yr_c, x_c, yc_l, yr_l, x_l, *rest):
    @pl.when(pl.program_id(0) < ctx_tiles)
    def _():
        _outproj_tile(yc_c, yr_c, x_c, *rest)

    @pl.when(pl.program_id(0) >= ctx_tiles)
    def _():
        _outproj_tile(yc_l, yr_l, x_l, *rest)


def _outproj_tile(yc_ref, yr_ref, x_ref, g1_ref, sh_ref, sc_ref, ng_ref, wo_ref, wr_ref, br_ref,
                  x1_ref, xloc_ref, route_ref, cnt_ref):
    m = (jnp.dot(yc_ref[...], wo_ref[0:CONV_W, :], preferred_element_type=F32)
         + jnp.dot(yr_ref[...], wo_ref[CONV_W:, :], preferred_element_type=F32))
    x1 = x_ref[...] + g1_ref[0, 0] * m
    x1_ref[...] = x1
    xn = (_rms(x1) * ng_ref[...]) * (1.0 + sc_ref[0, 0]) + sh_ref[0, 0]
    xb = xn.astype(BF16)
    logits = jnp.dot(xb, wr_ref[...], preferred_element_type=F32) + br_ref[...]
    lane, lane_f, g_idx, i1, i2, w1, w2 = _route(logits)

    picks = jnp.where(lane_f == g_idx, 1.0, 0.0)
    rows = picks.shape[0]
    tri = (lax.broadcasted_iota(jnp.int32, (rows, rows), 0)
           > lax.broadcasted_iota(jnp.int32, (rows, rows), 1))
    before = jnp.dot(jnp.where(tri, 1.0, 0.0).astype(BF16), picks.astype(BF16),
                     preferred_element_type=F32)
    count = jnp.sum(picks, axis=0, keepdims=True)
    cnt_ref[0] = count
    count8 = jnp.broadcast_to(jnp.floor((count + (SUBLANES - 1)) * (1.0 / SUBLANES)) * SUBLANES,
                              (SUBLANES, LANES))
    lane8 = lane[:SUBLANES]
    start = sum(jnp.where(lane8 >= k, pltpu.roll(count8, k, 1), 0.0) for k in range(1, N_GROUPS))
    local = jnp.sum(jnp.where(lane_f == g_idx, before + start[0:1], 0.0), axis=1, keepdims=True)
    route = jnp.where(lane == ROUTE_GROUP, g_idx, jnp.where(lane == ROUTE_LOCAL, local, jnp.where(
        lane == ROUTE_E1, i1 - N_GROUPS, jnp.where(lane == ROUTE_E2, i2 - N_GROUPS, jnp.where(
            lane == ROUTE_W1, w1, jnp.where(lane == ROUTE_W2, w2, 0.0))))))
    route_ref[...] = route

    local_row = jnp.transpose(jnp.broadcast_to(local, (rows, LANES)))[0:1, :]
    place = jnp.where(lax.broadcasted_iota(jnp.int32, (XLOC_ROWS, rows), 0).astype(F32) == local_row,
                      1.0, 0.0).astype(BF16)
    xloc_ref[0, :, :D_MODEL] = jnp.dot(place, xb, preferred_element_type=F32)
    xloc_ref[0, :, D_MODEL:] = sum(
        jnp.dot(place, piece, preferred_element_type=F32) for piece in _split3(route))


def _outproj(ctx, lat, mod4, mod_row_of_tile, norm_g, w_out_bf, w_router_bf, b_router):
    ctx_tiles = ctx[2].shape[0] // OUTPROJ_TILE
    tiles = ctx_tiles + lat[2].shape[0] // OUTPROJ_TILE
    tokens = tiles * OUTPROJ_TILE

    def mod_spec(which):
        return pl.BlockSpec((1, 1, 1, D_MODEL), lambda i: (mod_row_of_tile(i), which, 0, 0))

    ctx_tok = lambda w: pl.BlockSpec((OUTPROJ_TILE, w), lambda i: (jnp.minimum(i, ctx_tiles - 1), 0))
    lat_tok = lambda w: pl.BlockSpec((OUTPROJ_TILE, w), lambda i: (jnp.maximum(i - ctx_tiles, 0), 0))
    tok = lambda w: pl.BlockSpec((OUTPROJ_TILE, w), lambda i: (i, 0))
    full = lambda a: pl.BlockSpec(a.shape, lambda i: (0,) * a.ndim)
    widths = (CONV_W, RET_W, D_MODEL)
    return pl.pallas_call(
        functools.partial(_outproj_kernel, ctx_tiles),
        grid=(tiles,),
        in_specs=[ctx_tok(w) for w in widths] + [lat_tok(w) for w in widths] + [
            mod_spec(2), mod_spec(3), mod_spec(4),
            full(norm_g), full(w_out_bf), full(w_router_bf), full(b_router)],
        out_specs=[tok(D_MODEL),
                   pl.BlockSpec((1, XLOC_ROWS, ROW_W), lambda i: (i, 0, 0)),
                   tok(ROUTER_COLS),
                   pl.BlockSpec((1, 1, ROUTER_COLS), lambda i: (i, 0, 0))],
        out_shape=[jax.ShapeDtypeStruct((tokens, D_MODEL), F32),
                   jax.ShapeDtypeStruct((tiles, XLOC_ROWS, ROW_W), F32),
                   jax.ShapeDtypeStruct((tokens, ROUTER_COLS), F32),
                   jax.ShapeDtypeStruct((tiles, 1, ROUTER_COLS), F32)],
        compiler_params=pltpu.CompilerParams(
            dimension_semantics=("parallel",), vmem_limit_bytes=VMEM_LIMIT),
        name="outproj",
    )(*ctx, *lat, mod4, mod4, mod4, norm_g, w_out_bf, w_router_bf, b_router)


RUN_PIECES = tuple(SUBLANES << b for b in reversed(range((OUTPROJ_TILE // SUBLANES).bit_length())))


def _expert_kernel(tile_group_ref, n_used_ref, first_ref, last_ref, fill_ref,
                   run_len_ref, run_src_ref, run_dst_ref,
                   xloc_hbm, w1_ref, w3_ref, w2_ref, ys_ref, xbuf, xb, gate_tabs, sem):
    j = pl.program_id(0)
    step = pl.program_id(1)
    n_used = n_used_ref[0]

    def tile_fetch(tile):
        slot = tile % 2
        group = tile_group_ref[tile]
        row0 = tile * GROUP_TILE

        def from_token_tile(b, carry):
            run = b * N_GROUPS + group
            lo = jnp.maximum(run_dst_ref[run], row0)
            hi = jnp.minimum(run_dst_ref[run] + run_len_ref[run], row0 + GROUP_TILE)
            n = jnp.maximum(hi - lo, 0)
            src = run_src_ref[run] + lo - run_dst_ref[run]
            dst = lo - row0
            for size in RUN_PIECES:
                done = n & (-2 * size)

                @pl.when((n & size) != 0)
                def _():
                    pltpu.make_async_copy(
                        xloc_hbm.at[b, pl.ds(pl.multiple_of(src + done, SUBLANES), size)],
                        xbuf.at[slot, pl.ds(pl.multiple_of(dst + done, SUBLANES), size)], sem.at[slot]).start()
            return carry

        lax.fori_loop(first_ref[tile], last_ref[tile] + 1, from_token_tile, 0)

    def tile_wait(tile):
        slot = tile % 2
        for size in (GROUP_TILE,) + tuple(GROUP_TILE >> k for k in range(1, (GROUP_TILE // SUBLANES).bit_length())):
            @pl.when((fill_ref[tile] & size) != 0)
            def _():
                pltpu.make_async_copy(xbuf.at[1 - slot, pl.ds(0, size)], xbuf.at[slot, pl.ds(0, size)],
                                      sem.at[slot]).wait()

    @pl.when(j < n_used)
    def _():
        @pl.when(step == 0)
        def _():
            @pl.when(j == 0)
            def _():
                xbuf[...] = jnp.zeros_like(xbuf)
                tile_fetch(j)

            tile_wait(j)
            rows_in = xbuf[j % 2]
            xb[...] = rows_in[:, :D_MODEL].astype(BF16)
            route = rows_in[:, D_MODEL:]
            lane = lax.broadcasted_iota(jnp.int32, route.shape, 1)
            for n, which in enumerate((ROUTE_E1, ROUTE_E2, ROUTE_W1, ROUTE_W2)):
                col = jnp.sum(jnp.where(lane == which, route, 0.0), axis=1, keepdims=True)
                gate_tabs[n] = jnp.broadcast_to(col, route.shape)

            @pl.when(j + 1 < n_used)
            def _():
                tile_fetch(j + 1)

        def evaluate(rows):
            x = xb[:rows]
            total = None
            for s in range(EXPERTS_PER_STEP):
                expert = (tile_group_ref[j] * EXPERTS_PER_GROUP + step * EXPERTS_PER_STEP + s).astype(F32)
                gate = (jnp.where(gate_tabs[0, :rows] == expert, gate_tabs[2, :rows], 0.0)
                        + jnp.where(gate_tabs[1, :rows] == expert, gate_tabs[3, :rows], 0.0))
                hid = _silu(jnp.dot(x, w1_ref[0, s].astype(BF16), preferred_element_type=F32)) * jnp.dot(
                    x, w3_ref[0, s].astype(BF16), preferred_element_type=F32)
                y = jnp.dot(hid.astype(BF16), w2_ref[0, s].astype(BF16), preferred_element_type=F32)
                gated = jnp.concatenate(
                    [gate * y[:, c * LANES:(c + 1) * LANES] for c in range(D_MODEL // LANES)], axis=1)
                total = gated if total is None else total + gated

            @pl.when(step == 0)
            def _():
                ys_ref[:rows] = total

            @pl.when(step > 0)
            def _():
                ys_ref[:rows] += total

        half = GROUP_TILE // 2

        @pl.when(fill_ref[j] > half)
        def _():
            evaluate(GROUP_TILE)

        @pl.when(fill_ref[j] <= half)
        def _():
            evaluate(half)


def _experts(xloc, tile_tables, runs, slots, w1, w3, w2):
    steps = EXPERTS_PER_GROUP // EXPERTS_PER_STEP
    paired = lambda w: w.reshape((N_EXPERTS // EXPERTS_PER_STEP, EXPERTS_PER_STEP) + w.shape[1:])
    w_spec = lambda shape: pl.BlockSpec((1, EXPERTS_PER_STEP) + shape, lambda j, s, tg, nu, *_: (
        tg[jnp.minimum(j, nu[0] - 1)] * steps + jnp.where(j < nu[0], s, steps - 1), 0, 0, 0))
    grid_spec = pltpu.PrefetchScalarGridSpec(
        num_scalar_prefetch=8,
        grid=(slots // GROUP_TILE, steps),
        in_specs=[
            pl.BlockSpec(memory_space=pl.ANY),
            w_spec((D_MODEL, D_EXPERT)), w_spec((D_MODEL, D_EXPERT)), w_spec((D_EXPERT, D_MODEL)),
        ],
        out_specs=pl.BlockSpec((GROUP_TILE, D_MODEL), lambda j, s, tg, nu, *_: (jnp.minimum(j, nu[0] - 1), 0)),
        scratch_shapes=[pltpu.VMEM((2, GROUP_TILE, ROW_W), F32), pltpu.VMEM((GROUP_TILE, D_MODEL), BF16),
                        pltpu.VMEM((4, GROUP_TILE, LANES), F32), pltpu.SemaphoreType.DMA((2,))],
    )
    return pl.pallas_call(
        _expert_kernel,
        grid_spec=grid_spec,
        out_shape=jax.ShapeDtypeStruct((slots, D_MODEL), F32),
        compiler_params=pltpu.CompilerParams(
            dimension_semantics=("arbitrary", "arbitrary"), vmem_limit_bytes=EXPERT_VMEM_LIMIT),
        name="experts",
    )(*tile_tables, *runs, xloc, paired(w1), paired(w3), paired(w2))


def _combine_kernel(n_tiles, tile_base, run_len_ref, run_src_ref, run_dst_ref,
                    x1_ref, route_ref, g2_ref, fg_ref, ys_hbm, o_ref, buf, sem):
    i = pl.program_id(0)
    slot = i % 2

    def run_copies(local_tile, act):
        s = local_tile % 2
        tile = local_tile + tile_base
        for g in range(N_GROUPS):
            n = run_len_ref[tile * N_GROUPS + g]
            src = run_src_ref[tile * N_GROUPS + g]
            dst = run_dst_ref[tile * N_GROUPS + g]
            for size in RUN_PIECES:
                done = n & (-2 * size)

                @pl.when((n & size) != 0)
                def _():
                    act(pltpu.make_async_copy(
                        ys_hbm.at[pl.ds(pl.multiple_of(dst + done, SUBLANES), size)],
                        buf.at[s, pl.ds(pl.multiple_of(src + done, SUBLANES), size)], sem.at[s]))

    @pl.when(i == 0)
    def _():
        buf[...] = jnp.zeros_like(buf)
        run_copies(i, lambda cp: cp.start())

    @pl.when(i + 1 < n_tiles)
    def _():
        run_copies(i + 1, lambda cp: cp.start())

    run_copies(i, lambda cp: cp.wait())
    route = route_ref[...]
    lane = lax.broadcasted_iota(jnp.int32, route.shape, 1)
    local = jnp.sum(jnp.where(lane == ROUTE_LOCAL, route, 0.0), axis=1, keepdims=True)
    pick = jnp.where(lax.broadcasted_iota(jnp.int32, (route.shape[0], LOCAL_ROWS), 1).astype(F32) == local,
                     1.0, 0.0).astype(BF16)
    moe = sum(jnp.dot(pick, piece, preferred_element_type=F32) for piece in _split3(buf[slot])[:2])
    y = x1_ref[...] + g2_ref[0, 0] * moe
    o_ref[...] = _rms(y) * fg_ref[...]


def _combine(ys, runs, x1, route, tokens, mod4, mod_row_of_tile, final_g, tile_base):
    tiles = tokens // OUTPROJ_TILE
    tok = lambda w: pl.BlockSpec((OUTPROJ_TILE, w), lambda i, *_: (i + tile_base, 0))
    grid_spec = pltpu.PrefetchScalarGridSpec(
        num_scalar_prefetch=3,
        grid=(tiles,),
        in_specs=[
            tok(D_MODEL), tok(ROUTER_COLS),
            pl.BlockSpec((1, 1, 1, D_MODEL), lambda i, *_: (mod_row_of_tile(i + tile_base), 5, 0, 0)),
            pl.BlockSpec((1, D_MODEL), lambda i, *_: (0, 0)),
            pl.BlockSpec(memory_space=pl.ANY),
        ],
        out_specs=pl.BlockSpec((OUTPROJ_TILE, D_MODEL), lambda i, *_: (i, 0)),
        scratch_shapes=[pltpu.VMEM((2, LOCAL_ROWS, D_MODEL), F32), pltpu.SemaphoreType.DMA((2,))],
    )
    return pl.pallas_call(
        functools.partial(_combine_kernel, tiles, tile_base),
        grid_spec=grid_spec,
        out_shape=jax.ShapeDtypeStruct((tokens, D_MODEL), F32),
        compiler_params=pltpu.CompilerParams(
            dimension_semantics=("arbitrary",), vmem_limit_bytes=VMEM_LIMIT),
        name="combine",
    )(*runs, x1, route, mod4, final_g, ys)


def _routing_tables(counts):
    counts = counts.astype(jnp.int32)
    tiles = counts.shape[0]
    run_len = ((counts + SUBLANES - 1) // SUBLANES) * SUBLANES
    run_src = jnp.cumsum(run_len, axis=1) - run_len
    group_rows = jnp.sum(run_len, axis=0)
    padded = ((group_rows + GROUP_TILE - 1) // GROUP_TILE) * GROUP_TILE
    ends = jnp.cumsum(padded)
    offs = ends - padded
    run_dst = offs[None, :] + jnp.cumsum(run_len, axis=0) - run_len

    n_used = ends[-1] // GROUP_TILE
    max_rows = tiles * (OUTPROJ_TILE + N_GROUPS * (SUBLANES - 1))
    max_tiles = -(-max_rows // GROUP_TILE) + N_GROUPS
    tile_ids = jnp.minimum(jnp.arange(max_tiles, dtype=jnp.int32), n_used - 1)
    tile_group = jnp.sum(tile_ids[:, None] * GROUP_TILE >= ends[None, :], axis=1).astype(jnp.int32)
    of_group = (tile_group[:, None] == jnp.arange(N_GROUPS, dtype=jnp.int32))[:, None, :]
    start = jnp.sum(jnp.where(of_group, run_dst[None], 0), axis=-1)
    stop = start + jnp.sum(jnp.where(of_group, run_len[None], 0), axis=-1)
    row0 = (tile_ids * GROUP_TILE)[:, None]
    first = jnp.sum(stop <= row0, axis=1).astype(jnp.int32)
    last = jnp.sum(start < row0 + GROUP_TILE, axis=1).astype(jnp.int32) - 1
    group_end = jnp.sum(jnp.where(of_group[:, 0, :], (offs + group_rows)[None, :], 0), axis=-1)
    fill = jnp.clip(group_end - row0[:, 0], 0, GROUP_TILE).astype(jnp.int32)
    flat = lambda a: a.reshape(-1)
    return ((tile_group, n_used.reshape(1), first, last, fill), (flat(run_len), flat(run_src), flat(run_dst)),
            max_tiles * GROUP_TILE)


def _mixer(x, mod4, mod_row, is_grid, s_f0, s_b0, p):
    norm_mix_g, w_in_bf, conv_w, decay_rows = p
    y_conv, *qkvg = _inproj(x, mod4, mod_row, norm_mix_g, w_in_bf, conv_w, is_grid)
    per_seq = lambda a: a.reshape(x.shape[0], x.shape[1], a.shape[-1])
    ret = _retention(*map(per_seq, qkvg), decay_rows, s_f0, s_b0, emit_state=not is_grid)
    return y_conv, ret[0].reshape(-1, RET_W), ret[1:]


def kernel(x_prompt, x_sample, state_ret_fwd, state_ret_bwd, c, c_ctx, norm_mix_g, norm_ffn_g, w_ada, b_ada, w_in, conv_w, ret_decay_fwd, ret_decay_bwd, w_out, w_router_group, b_router_group, w_router_expert, b_router_expert, w_gate_e, w_up_e, w_down_e, final_norm_g):
    assert norm_mix_g.shape[0] == 1, "single-layer backbone"
    n_lat = c.shape[0]
    ctx_row = n_lat
    mod_rows = 8
    cvec = jnp.concatenate([c, c_ctx[None, :], jnp.zeros((mod_rows - n_lat - 1, D_MODEL), F32)], axis=0)
    mod = _modulation(cvec, w_ada[0], b_ada[0][None, :])
    mod4 = mod.reshape(mod_rows, 6, 1, D_MODEL)

    pad = ROUTER_COLS - N_GROUPS - N_EXPERTS
    w_router = jnp.concatenate(
        [w_router_group[0], w_router_expert[0], jnp.zeros((D_MODEL, pad), F32)], axis=1).astype(BF16)
    b_router = jnp.concatenate([b_router_group[0], b_router_expert[0], jnp.zeros((pad,), F32)])[None, :]
    decay_rows = jnp.broadcast_to(
        jnp.concatenate([ret_decay_fwd[0], ret_decay_bwd[0]])[:, None], (2 * RET_HEADS, LANES))
    p_mix = (norm_mix_g, w_in[0].astype(BF16), conv_w[0], decay_rows)
    w_out_bf = w_out[0].astype(BF16)
    final_g = final_norm_g[None, :]

    ctx_tokens = x_prompt.shape[0] * x_prompt.shape[1]
    lat_tokens = x_sample.shape[0] * x_sample.shape[1]
    ctx_tiles = ctx_tokens // OUTPROJ_TILE
    lat_tiles_per_seq = x_sample.shape[1] // OUTPROJ_TILE
    tile_mod = lambda i: jnp.where(i < ctx_tiles, ctx_row, (i - ctx_tiles) // lat_tiles_per_seq)
    flat = lambda a: a.reshape(-1, a.shape[-1])

    yc_c, yr_c, (s_f, s_b) = _mixer(x_prompt, mod4, lambda b: ctx_row, False, None, None, p_mix)
    yc_l, yr_l, _ = _mixer(x_sample, mod4, lambda b: b, True, state_ret_fwd, state_ret_bwd, p_mix)

    x1, xloc, route, cnt = _outproj((yc_c, yr_c, flat(x_prompt)), (yc_l, yr_l, flat(x_sample)), mod4, tile_mod,
                                    norm_ffn_g, w_out_bf, w_router, b_router)
    tile_tables, runs, slots = _routing_tables(cnt[:, 0, :N_GROUPS])
    ys = _experts(xloc, tile_tables, runs, slots, w_gate_e[0], w_up_e[0], w_down_e[0])
    y_prompt = _combine(ys, runs, x1, route, ctx_tokens, mod4, tile_mod, final_g, 0)
    y_sample = _combine(ys, runs, x1, route, lat_tokens, mod4, tile_mod, final_g, ctx_tiles)
    return (y_prompt.reshape(x_prompt.shape), y_sample.reshape(x_sample.shape),
            s_f.astype(x_prompt.dtype), s_b.astype(x_prompt.dtype))
```

```python
import functools

import jax
import jax.numpy as jnp
import numpy as np
from jax import lax
from jax.experimental import pallas as pl
from jax.experimental.pallas import tpu as pltpu

F32 = jnp.float32
BF16 = jnp.bfloat16

D_MODEL = 1024
GRID_W = 64
CONV_W = 512
RET_HEADS = 4
RET_DK = 128
RET_DV = 128
RET_W = RET_HEADS * RET_DV
QK_W = RET_HEADS * RET_DK
CHUNK = 128
N_GROUPS = 4
EXPERTS_PER_GROUP = 8
N_EXPERTS = N_GROUPS * EXPERTS_PER_GROUP
D_EXPERT = 256
ROPE_BASE = 10000.0
EPS = 1e-6

LANES = 128
TOKEN_TILE = 1024
OUTPROJ_TILE = 512
GROUP_TILE = 1024
EXPERTS_PER_STEP = 4
RET_UNROLL = 8
SUBLANES = 8
XLOC_ROWS = OUTPROJ_TILE + N_GROUPS * SUBLANES
LOCAL_ROWS = OUTPROJ_TILE + LANES
ROW_W = D_MODEL + LANES
ROUTE_GROUP, ROUTE_LOCAL, ROUTE_E1, ROUTE_E2, ROUTE_W1, ROUTE_W2 = range(6)
MOD_COLS = 1536
ROUTER_COLS = LANES
VMEM_LIMIT = 48 * 1024 * 1024
EXPERT_VMEM_LIMIT = 56 * 1024 * 1024


def _silu(x):
    return x * jax.nn.sigmoid(x)


def _rms(x):
    return x * lax.rsqrt(jnp.mean(x * x, axis=-1, keepdims=True) + EPS)


def _bdot(a, b):
    return jnp.dot(a.astype(BF16), b.astype(BF16), preferred_element_type=F32)


def _split3(x):
    hi = x.astype(BF16)
    rest = x - hi.astype(F32)
    mid = rest.astype(BF16)
    return hi, mid, (rest - mid.astype(F32)).astype(BF16)


def _mod_kernel(c_ref, w_ref, b_ref, o_ref):
    o_ref[...] = _bdot(_silu(c_ref[...]), w_ref[...]) + b_ref[...]


def _modulation(cvec, w_ada, b_ada):
    rows = cvec.shape[0]
    n = w_ada.shape[1]
    return pl.pallas_call(
        _mod_kernel,
        grid=(n // MOD_COLS,),
        in_specs=[
            pl.BlockSpec((rows, D_MODEL), lambda j: (0, 0)),
            pl.BlockSpec((D_MODEL, MOD_COLS), lambda j: (0, j)),
            pl.BlockSpec((1, MOD_COLS), lambda j: (0, j)),
        ],
        out_specs=pl.BlockSpec((rows, MOD_COLS), lambda j: (0, j)),
        out_shape=jax.ShapeDtypeStruct((rows, n), F32),
        compiler_params=pltpu.CompilerParams(vmem_limit_bytes=VMEM_LIMIT),
        name="modulation",
    )(cvec, w_ada, b_ada)


def _inproj_kernel(seg, is_grid, x_ref, sh_ref, sc_ref, ng_ref, w_ref, cw_ref, *rest):
    if is_grid:
        cos_ref, sa_ref, sb_ref, yc_ref, q_ref, k_ref, v_ref, g_ref = rest
    else:
        yc_ref, q_ref, k_ref, v_ref, g_ref = rest
    x = x_ref[...]
    xn = (_rms(x) * ng_ref[...]) * (1.0 + sc_ref[0, 0]) + sh_ref[0, 0]
    xb = xn.astype(BF16)

    def proj(c0, n):
        return jnp.dot(xb, w_ref[:, c0:c0 + n], preferred_element_type=F32)

    gate_b = proj(0, CONV_W)
    u = proj(CONV_W, CONV_W) * proj(2 * CONV_W, CONV_W)
    rows = u.shape[0]
    pos = lax.broadcasted_iota(jnp.int32, u.shape, 0) & (seg - 1)
    u_prev = jnp.where(pos != 0, pltpu.roll(u, 1, 0), 0.0)
    u_next = jnp.where(pos != seg - 1, pltpu.roll(u, rows - 1, 0), 0.0)
    conv = cw_ref[0:1, :] * u_prev + cw_ref[1:2, :] * u + cw_ref[2:3, :] * u_next
    yc_ref[...] = (gate_b * conv).astype(yc_ref.dtype)

    q0 = 3 * CONV_W
    q = proj(q0, QK_W)
    k = proj(q0 + QK_W, QK_W)
    if is_grid:
        cos, sa, sb = cos_ref[...], sa_ref[...], sb_ref[...]

        def rope(t):
            out = []
            for h in range(RET_HEADS):
                th = t[:, h * RET_DK:(h + 1) * RET_DK]
                out.append(th * cos + pltpu.roll(th, RET_DK - 1, 1) * sa + pltpu.roll(th, 1, 1) * sb)
            return jnp.concatenate(out, axis=1)

        q, k = rope(q), rope(k)
    q_ref[...] = q
    k_ref[...] = k
    v_ref[...] = proj(q0 + 2 * QK_W, RET_W)
    g_ref[...] = proj(q0 + 2 * QK_W + RET_W, RET_W)


def _rope_tables(length):
    pos = np.arange(length)
    row = (pos // GRID_W).astype(np.float64)
    col = (pos % GRID_W).astype(np.float64)
    n_pairs = RET_DK // 4
    freqs = ROPE_BASE ** (-(np.arange(n_pairs, dtype=np.float64) * 2.0 / (RET_DK // 2)))
    ang = np.concatenate([row[:, None] * freqs, col[:, None] * freqs], axis=-1)
    cos = np.repeat(np.cos(ang), 2, axis=-1)
    sin = np.repeat(np.sin(ang), 2, axis=-1)
    even = (np.arange(RET_DK) % 2) == 0
    return tuple(jnp.asarray(t, F32) for t in (cos, np.where(even, -sin, 0.0), np.where(even, 0.0, sin)))


def _inproj(x, mod4, mod_row, norm_g, w_in_bf, conv_w, is_grid):
    bsz, length, _ = x.shape
    seg = GRID_W if is_grid else length
    assert TOKEN_TILE % seg == 0 and (length % TOKEN_TILE == 0 or TOKEN_TILE % length == 0)
    tokens = bsz * length
    tiles_per_seq = max(length // TOKEN_TILE, 1)
    seqs_per_tile = max(TOKEN_TILE // length, 1)
    batch_of = lambda i: (i // tiles_per_seq) * seqs_per_tile

    def mod_spec(which):
        return pl.BlockSpec((1, 1, 1, D_MODEL), lambda i: (mod_row(batch_of(i)), which, 0, 0))

    def tok_spec(width):
        return pl.BlockSpec((TOKEN_TILE, width), lambda i: (i, 0))

    in_specs = [
        tok_spec(D_MODEL), mod_spec(0), mod_spec(1),
        pl.BlockSpec((1, D_MODEL), lambda i: (0, 0)),
        pl.BlockSpec(w_in_bf.shape, lambda i: (0, 0)),
        pl.BlockSpec(conv_w.shape, lambda i: (0, 0)),
    ]
    args = [x.reshape(tokens, D_MODEL), mod4, mod4, norm_g, w_in_bf, conv_w]
    if is_grid:
        assert length % TOKEN_TILE == 0
        in_specs += [pl.BlockSpec((TOKEN_TILE, RET_DK), lambda i: (i % tiles_per_seq, 0))] * 3
        args += list(_rope_tables(length))
    shp = lambda w, dt: jax.ShapeDtypeStruct((tokens, w), dt)
    return pl.pallas_call(
        functools.partial(_inproj_kernel, seg, is_grid),
        grid=(tokens // TOKEN_TILE,),
        in_specs=in_specs,
        out_specs=[tok_spec(CONV_W), tok_spec(QK_W), tok_spec(QK_W), tok_spec(RET_W), tok_spec(RET_W)],
        out_shape=[shp(CONV_W, BF16), shp(QK_W, F32), shp(QK_W, F32), shp(RET_W, F32), shp(RET_W, F32)],
        compiler_params=pltpu.CompilerParams(
            dimension_semantics=("parallel",), vmem_limit_bytes=VMEM_LIMIT),
        name="inproj_grid" if is_grid else "inproj_seq",
    )(*args)


def _ret_kernel(n_chunks, heads, has_init, emit_state, a_ref, q_ref, k_ref, v_ref, g_ref, *rest):
    rest = list(rest)
    if has_init:
        sf0_ref, sb0_ref = rest[:2]
        rest = rest[2:]
    y_ref = rest.pop(0)
    if emit_state:
        sf_out, sb_out = rest[:2]
        rest = rest[2:]
    st_f, st_b, dec = rest
    c = CHUNK
    sq = (c, c)
    head0 = pl.program_id(0) * heads

    def log_decays(hh):
        lg_f = jnp.log1p(-jnp.exp(a_ref[pl.ds(head0 + hh, 1), :]))
        lg_b = jnp.log1p(-jnp.exp(a_ref[pl.ds(head0 + hh + RET_HEADS, 1), :]))
        return lg_f, lg_b

    @pl.when(pl.program_id(1) == 0)
    def _():
        row = lax.broadcasted_iota(jnp.int32, sq, 0).astype(F32)
        col = lax.broadcasted_iota(jnp.int32, sq, 1).astype(F32)
        scale = RET_DK ** -0.5
        for hh in range(heads):
            lg_f, lg_b = log_decays(hh)
            dec[hh, 0] = scale * (
                jnp.where(row >= col, jnp.exp(jnp.where(row >= col, row - col, 0.0) * lg_f), 0.0)
                + jnp.where(col >= row, jnp.exp(jnp.where(col >= row, col - row, 0.0) * lg_b), 0.0))
            dec[hh, 1] = jnp.exp((row + 1.0) * lg_f)
            dec[hh, 2] = jnp.exp((c - row) * lg_b)
            dec[hh, 3] = scale * jnp.exp((c - 1.0 - col) * lg_f)
            dec[hh, 4] = scale * jnp.exp(col * lg_b)

    def rows(n):
        return pl.ds(pl.multiple_of(n * c, c), c) if not isinstance(n, int) else pl.ds(n * c, c)

    def cols(hh):
        return slice(hh * RET_DK, (hh + 1) * RET_DK)

    def kv_step(hh, n):
        kt = jnp.transpose(k_ref[0, rows(n), cols(hh)])
        lhs = jnp.concatenate([kt * dec[hh, 3], kt * dec[hh, 4]], axis=0)
        kv = _bdot(lhs, v_ref[0, rows(n), cols(hh)])
        st_f[hh, n] = kv[:RET_DK]
        st_b[hh, n] = kv[RET_DK:]

    def scan(hh, st, decay, order, s):
        def step(i, s):
            n = order(i)
            kv = st[hh, n]
            st[hh, n] = s
            return s * decay + kv
        if n_chunks <= RET_UNROLL:
            for i in range(n_chunks):
                s = step(i, s)
            return s
        return lax.fori_loop(0, n_chunks, step, s, unroll=RET_UNROLL)

    def out_step(hh, n):
        q = q_ref[0, rows(n), cols(hh)]
        scores = lax.dot_general(q.astype(BF16), k_ref[0, rows(n), cols(hh)].astype(BF16),
                                 (((1,), (1,)), ((), ())), preferred_element_type=F32)
        o = _bdot(scores * dec[hh, 0], v_ref[0, rows(n), cols(hh)])
        q_dec = jnp.concatenate([q * dec[hh, 1], q * dec[hh, 2]], axis=1)
        o = o + _bdot(q_dec, jnp.concatenate([st_f[hh, n], st_b[hh, n]], axis=0))
        y = _silu(g_ref[0, rows(n), cols(hh)]) * _rms(o)
        y_ref[0, rows(n), cols(hh)] = y.astype(y_ref.dtype)

    def over_chunks(step):
        if n_chunks * heads <= RET_UNROLL:
            for hh in range(heads):
                for n in range(n_chunks):
                    step(hh, n)
        else:
            for hh in range(heads):
                lax.fori_loop(0, n_chunks, lambda n, carry: (step(hh, n), carry)[1], 0, unroll=RET_UNROLL)

    over_chunks(kv_step)
    finals = []
    for hh in range(heads):
        lg_f, lg_b = log_decays(hh)
        s_f = sf0_ref[0, 0, hh] if has_init else jnp.zeros(sq, F32)
        s_b = sb0_ref[0, 0, hh] if has_init else jnp.zeros(sq, F32)
        s_f = scan(hh, st_f, jnp.exp(c * lg_f), lambda i: i, s_f)
        s_b = scan(hh, st_b, jnp.exp(c * lg_b), lambda i: n_chunks - 1 - i, s_b)
        finals.append((s_f, s_b))
    over_chunks(out_step)
    if emit_state:
        for hh, (s_f, s_b) in enumerate(finals):
            sf_out[0, 0, hh] = s_f
            sb_out[0, 0, hh] = s_b


def _retention(q, k, v, g, decay_rows, s_f0, s_b0, emit_state):
    bsz, length, _ = q.shape
    n_chunks = length // CHUNK
    has_init = s_f0 is not None
    heads = RET_HEADS if n_chunks * RET_HEADS <= RET_UNROLL else 1
    head_spec = pl.BlockSpec((1, length, heads * RET_DK), lambda h, b: (b, 0, h))
    st_spec = pl.BlockSpec((1, 1, heads, RET_DK, RET_DV), lambda h, b: (b, 0, h, 0, 0))
    in_specs = [pl.BlockSpec(decay_rows.shape, lambda h, b: (0, 0))] + [head_spec] * 4
    args = [decay_rows, q, k, v, g]
    if has_init:
        in_specs += [st_spec, st_spec]
        args += [s_f0, s_b0]
    out_specs = [head_spec]
    out_shape = [jax.ShapeDtypeStruct((bsz, length, RET_W), BF16)]
    if emit_state:
        st_shape = jax.ShapeDtypeStruct((bsz, 1, RET_HEADS, RET_DK, RET_DV), F32)
        out_specs += [st_spec, st_spec]
        out_shape += [st_shape, st_shape]
    return pl.pallas_call(
        functools.partial(_ret_kernel, n_chunks, heads, has_init, emit_state),
        grid=(RET_HEADS // heads, bsz),
        in_specs=in_specs,
        out_specs=out_specs,
        out_shape=out_shape,
        scratch_shapes=[
            pltpu.VMEM((heads, n_chunks, RET_DK, RET_DV), F32),
            pltpu.VMEM((heads, n_chunks, RET_DK, RET_DV), F32),
            pltpu.VMEM((heads, 5, CHUNK, CHUNK), F32),
        ],
        compiler_params=pltpu.CompilerParams(
            dimension_semantics=("arbitrary", "arbitrary"), vmem_limit_bytes=VMEM_LIMIT),
        name="retention_init" if has_init else "retention_zero",
    )(*args)


def _route(logits):
    lane = lax.broadcasted_iota(jnp.int32, logits.shape, 1)
    lane_f = lane.astype(F32)
    neg = -jnp.inf
    far = float(LANES)
    is_g = lane < N_GROUPS
    lg = jnp.where(is_g, logits, neg)
    g_max = jnp.max(lg, axis=1, keepdims=True)
    g_idx = jnp.min(jnp.where(lg == g_max, lane_f, far), axis=1, keepdims=True)
    p_sel = 1.0 / jnp.sum(jnp.where(is_g, jnp.exp(lg - g_max), 0.0), axis=1, keepdims=True)
    lane_group = ((lane - N_GROUPS) >> (EXPERTS_PER_GROUP.bit_length() - 1)).astype(F32)
    sel = (lane >= N_GROUPS) & (lane < N_GROUPS + N_EXPERTS) & (lane_group == g_idx)
    le = jnp.where(sel, logits, neg)
    v1 = jnp.max(le, axis=1, keepdims=True)
    i1 = jnp.min(jnp.where(le == v1, lane_f, far), axis=1, keepdims=True)
    le2 = jnp.where(lane_f == i1, neg, le)
    v2 = jnp.max(le2, axis=1, keepdims=True)
    i2 = jnp.min(jnp.where(le2 == v2, lane_f, far), axis=1, keepdims=True)
    e2 = jnp.exp(v2 - v1)
    w1 = p_sel * (1.0 / (1.0 + e2))
    w2 = p_sel * (e2 / (1.0 + e2))
    return lane, lane_f, g_idx, i1, i2, w1, w2


def _outproj_kernel(ctx_tiles, yc_c, yr_c, x_c, yc_l, yr_l, x_l, *rest):
    @pl.when(pl.program_id(0) < ctx_tiles)
    def _():
        _outproj_tile(yc_c, yr_c, x_c, *rest)

    @pl.when(pl.program_id(0) >= ctx_tiles)
    def _():
        _outproj_tile(yc_l, yr_l, x_l, *rest)


def _outproj_tile(yc_ref, yr_ref, x_ref, g1_ref, sh_ref, sc_ref, ng_ref, wo_ref, wr_ref, br_ref,
                  x1_ref, xloc_ref, route_ref, cnt_ref):
    m = (jnp.dot(yc_ref[...], wo_ref[0:CONV_W, :], preferred_element_type=F32)
         + jnp.dot(yr_ref[...], wo_ref[CONV_W:, :], preferred_element_type=F32))
    x1 = x_ref[...] + g1_ref[0, 0] * m
    x1_ref[...] = x1
    xn = (_rms(x1) * ng_ref[...]) * (1.0 + sc_ref[0, 0]) + sh_ref[0, 0]
    xb = xn.astype(BF16)
    logits = jnp.dot(xb, wr_ref[...], preferred_element_type=F32) + br_ref[...]
    lane, lane_f, g_idx, i1, i2, w1, w2 = _route(logits)

    picks = jnp.where(lane_f == g_idx, 1.0, 0.0)
    rows = picks.shape[0]
    tri = (lax.broadcasted_iota(jnp.int32, (rows, rows), 0)
           > lax.broadcasted_iota(jnp.int32, (rows, rows), 1))
    before = jnp.dot(jnp.where(tri, 1.0, 0.0).astype(BF16), picks.astype(BF16),
                     preferred_element_type=F32)
    count = jnp.sum(picks, axis=0, keepdims=True)
    cnt_ref[0] = count
    count8 = jnp.broadcast_to(jnp.floor((count + (SUBLANES - 1)) * (1.0 / SUBLANES)) * SUBLANES,
                              (SUBLANES, LANES))
    lane8 = lane[:SUBLANES]
    start = sum(jnp.where(lane8 >= k, pltpu.roll(count8, k, 1), 0.0) for k in range(1, N_GROUPS))
    local = jnp.sum(jnp.where(lane_f == g_idx, before + start[0:1], 0.0), axis=1, keepdims=True)
    route = jnp.where(lane == ROUTE_GROUP, g_idx, jnp.where(lane == ROUTE_LOCAL, local, jnp.where(
        lane == ROUTE_E1, i1 - N_GROUPS, jnp.where(lane == ROUTE_E2, i2 - N_GROUPS, jnp.where(
            lane == ROUTE_W1, w1, jnp.where(lane == ROUTE_W2, w2, 0.0))))))
    route_ref[...] = route

    local_row = jnp.transpose(jnp.broadcast_to(local, (rows, LANES)))[0:1, :]
    place = jnp.where(lax.broadcasted_iota(jnp.int32, (XLOC_ROWS, rows), 0).astype(F32) == local_row,
                      1.0, 0.0).astype(BF16)
    xloc_ref[0, :, :D_MODEL] = jnp.dot(place, xb, preferred_element_type=F32)
    xloc_ref[0, :, D_MODEL:] = sum(
        jnp.dot(place, piece, preferred_element_type=F32) for piece in _split3(route))


def _outproj(ctx, lat, mod4, mod_row_of_tile, norm_g, w_out_bf, w_router_bf, b_router):
    ctx_tiles = ctx[2].shape[0] // OUTPROJ_TILE
    tiles = ctx_tiles + lat[2].shape[0] // OUTPROJ_TILE
    tokens = tiles * OUTPROJ_TILE

    def mod_spec(which):
        return pl.BlockSpec((1, 1, 1, D_MODEL), lambda i: (mod_row_of_tile(i), which, 0, 0))

    ctx_tok = lambda w: pl.BlockSpec((OUTPROJ_TILE, w), lambda i: (jnp.minimum(i, ctx_tiles - 1), 0))
    lat_tok = lambda w: pl.BlockSpec((OUTPROJ_TILE, w), lambda i: (jnp.maximum(i - ctx_tiles, 0), 0))
    tok = lambda w: pl.BlockSpec((OUTPROJ_TILE, w), lambda i: (i, 0))
    full = lambda a: pl.BlockSpec(a.shape, lambda i: (0,) * a.ndim)
    widths = (CONV_W, RET_W, D_MODEL)
    return pl.pallas_call(
        functools.partial(_outproj_kernel, ctx_tiles),
        grid=(tiles,),
        in_specs=[ctx_tok(w) for w in widths] + [lat_tok(w) for w in widths] + [
            mod_spec(2), mod_spec(3), mod_spec(4),
            full(norm_g), full(w_out_bf), full(w_router_bf), full(b_router)],
        out_specs=[tok(D_MODEL),
                   pl.BlockSpec((1, XLOC_ROWS, ROW_W), lambda i: (i, 0, 0)),
                   tok(ROUTER_COLS),
                   pl.BlockSpec((1, 1, ROUTER_COLS), lambda i: (i, 0, 0))],
        out_shape=[jax.ShapeDtypeStruct((tokens, D_MODEL), F32),
                   jax.ShapeDtypeStruct((tiles, XLOC_ROWS, ROW_W), F32),
                   jax.ShapeDtypeStruct((tokens, ROUTER_COLS), F32),
                   jax.ShapeDtypeStruct((tiles, 1, ROUTER_COLS), F32)],
        compiler_params=pltpu.CompilerParams(
            dimension_semantics=("parallel",), vmem_limit_bytes=VMEM_LIMIT),
        name="outproj",
    )(*ctx, *lat, mod4, mod4, mod4, norm_g, w_out_bf, w_router_bf, b_router)


RUN_PIECES = tuple(SUBLANES << b for b in reversed(range((OUTPROJ_TILE // SUBLANES).bit_length())))


def _expert_kernel(tile_group_ref, n_used_ref, first_ref, last_ref, fill_ref,
                   run_len_ref, run_src_ref, run_dst_ref,
                   xloc_hbm, w1_ref, w3_ref, w2_ref, ys_ref, xbuf, xb, gate_tabs, sem):
    j = pl.program_id(0)
    step = pl.program_id(1)
    n_used = n_used_ref[0]

    def tile_fetch(tile):
        slot = tile % 2
        group = tile_group_ref[tile]
        row0 = tile * GROUP_TILE

        def from_token_tile(b, carry):
            run = b * N_GROUPS + group
            lo = jnp.maximum(run_dst_ref[run], row0)
            hi = jnp.minimum(run_dst_ref[run] + run_len_ref[run], row0 + GROUP_TILE)
            n = jnp.maximum(hi - lo, 0)
            src = run_src_ref[run] + lo - run_dst_ref[run]
            dst = lo - row0
            for size in RUN_PIECES:
                done = n & (-2 * size)

                @pl.when((n & size) != 0)
                def _():
                    pltpu.make_async_copy(
                        xloc_hbm.at[b, pl.ds(pl.multiple_of(src + done, SUBLANES), size)],
                        xbuf.at[slot, pl.ds(pl.multiple_of(dst + done, SUBLANES), size)], sem.at[slot]).start()
            return carry

        lax.fori_loop(first_ref[tile], last_ref[tile] + 1, from_token_tile, 0)

    def tile_wait(tile):
        slot = tile % 2
        for size in (GROUP_TILE,) + tuple(GROUP_TILE >> k for k in range(1, (GROUP_TILE // SUBLANES).bit_length())):
            @pl.when((fill_ref[tile] & size) != 0)
            def _():
                pltpu.make_async_copy(xbuf.at[1 - slot, pl.ds(0, size)], xbuf.at[slot, pl.ds(0, size)],
                                      sem.at[slot]).wait()

    @pl.when(j < n_used)
    def _():
        @pl.when(step == 0)
        def _():
            @pl.when(j == 0)
            def _():
                xbuf[...] = jnp.zeros_like(xbuf)
                tile_fetch(j)

            tile_wait(j)
            rows_in = xbuf[j % 2]
            xb[...] = rows_in[:, :D_MODEL].astype(BF16)
            route = rows_in[:, D_MODEL:]
            lane = lax.broadcasted_iota(jnp.int32, route.shape, 1)
            for n, which in enumerate((ROUTE_E1, ROUTE_E2, ROUTE_W1, ROUTE_W2)):
                col = jnp.sum(jnp.where(lane == which, route, 0.0), axis=1, keepdims=True)
                gate_tabs[n] = jnp.broadcast_to(col, route.shape)

            @pl.when(j + 1 < n_used)
            def _():
                tile_fetch(j + 1)

        def evaluate(rows):
            x = xb[:rows]
            total = None
            for s in range(EXPERTS_PER_STEP):
                expert = (tile_group_ref[j] * EXPERTS_PER_GROUP + step * EXPERTS_PER_STEP + s).astype(F32)
                gate = (jnp.where(gate_tabs[0, :rows] == expert, gate_tabs[2, :rows], 0.0)
                        + jnp.where(gate_tabs[1, :rows] == expert, gate_tabs[3, :rows], 0.0))
                hid = _silu(jnp.dot(x, w1_ref[0, s].astype(BF16), preferred_element_type=F32)) * jnp.dot(
                    x, w3_ref[0, s].astype(BF16), preferred_element_type=F32)
                y = jnp.dot(hid.astype(BF16), w2_ref[0, s].astype(BF16), preferred_element_type=F32)
                gated = jnp.concatenate(
                    [gate * y[:, c * LANES:(c + 1) * LANES] for c in range(D_MODEL // LANES)], axis=1)
                total = gated if total is None else total + gated

            @pl.when(step == 0)
            def _():
                ys_ref[:rows] = total

            @pl.when(step > 0)
            def _():
                ys_ref[:rows] += total

        half = GROUP_TILE // 2

        @pl.when(fill_ref[j] > half)
        def _():
            evaluate(GROUP_TILE)

        @pl.when(fill_ref[j] <= half)
        def _():
            evaluate(half)


def _experts(xloc, tile_tables, runs, slots, w1, w3, w2):
    steps = EXPERTS_PER_GROUP // EXPERTS_PER_STEP
    paired = lambda w: w.reshape((N_EXPERTS // EXPERTS_PER_STEP, EXPERTS_PER_STEP) + w.shape[1:])
    w_spec = lambda shape: pl.BlockSpec((1, EXPERTS_PER_STEP) + shape, lambda j, s, tg, nu, *_: (
        tg[jnp.minimum(j, nu[0] - 1)] * steps + jnp.where(j < nu[0], s, steps - 1), 0, 0, 0))
    grid_spec = pltpu.PrefetchScalarGridSpec(
        num_scalar_prefetch=8,
        grid=(slots // GROUP_TILE, steps),
        in_specs=[
            pl.BlockSpec(memory_space=pl.ANY),
            w_spec((D_MODEL, D_EXPERT)), w_spec((D_MODEL, D_EXPERT)), w_spec((D_EXPERT, D_MODEL)),
        ],
        out_specs=pl.BlockSpec((GROUP_TILE, D_MODEL), lambda j, s, tg, nu, *_: (jnp.minimum(j, nu[0] - 1), 0)),
        scratch_shapes=[pltpu.VMEM((2, GROUP_TILE, ROW_W), F32), pltpu.VMEM((GROUP_TILE, D_MODEL), BF16),
                        pltpu.VMEM((4, GROUP_TILE, LANES), F32), pltpu.SemaphoreType.DMA((2,))],
    )
    return pl.pallas_call(
        _expert_kernel,
        grid_spec=grid_spec,
        out_shape=jax.ShapeDtypeStruct((slots, D_MODEL), F32),
        compiler_params=pltpu.CompilerParams(
            dimension_semantics=("arbitrary", "arbitrary"), vmem_limit_bytes=EXPERT_VMEM_LIMIT),
        name="experts",
    )(*tile_tables, *runs, xloc, paired(w1), paired(w3), paired(w2))


def _combine_kernel(n_tiles, tile_base, run_len_ref, run_src_ref, run_dst_ref,
                    x1_ref, route_ref, g2_ref, fg_ref, ys_hbm, o_ref, buf, sem):
    i = pl.program_id(0)
    slot = i % 2

    def run_copies(local_tile, act):
        s = local_tile % 2
        tile = local_tile + tile_base
        for g in range(N_GROUPS):
            n = run_len_ref[tile * N_GROUPS + g]
            src = run_src_ref[tile * N_GROUPS + g]
            dst = run_dst_ref[tile * N_GROUPS + g]
            for size in RUN_PIECES:
                done = n & (-2 * size)

                @pl.when((n & size) != 0)
                def _():
                    act(pltpu.make_async_copy(
                        ys_hbm.at[pl.ds(pl.multiple_of(dst + done, SUBLANES), size)],
                        buf.at[s, pl.ds(pl.multiple_of(src + done, SUBLANES), size)], sem.at[s]))

    @pl.when(i == 0)
    def _():
        buf[...] = jnp.zeros_like(buf)
        run_copies(i, lambda cp: cp.start())

    @pl.when(i + 1 < n_tiles)
    def _():
        run_copies(i + 1, lambda cp: cp.start())

    run_copies(i, lambda cp: cp.wait())
    route = route_ref[...]
    lane = lax.broadcasted_iota(jnp.int32, route.shape, 1)
    local = jnp.sum(jnp.where(lane == ROUTE_LOCAL, route, 0.0), axis=1, keepdims=True)
    pick = jnp.where(lax.broadcasted_iota(jnp.int32, (route.shape[0], LOCAL_ROWS), 1).astype(F32) == local,
                     1.0, 0.0).astype(BF16)
    moe = sum(jnp.dot(pick, piece, preferred_element_type=F32) for piece in _split3(buf[slot])[:2])
    y = x1_ref[...] + g2_ref[0, 0] * moe
    o_ref[...] = _rms(y) * fg_ref[...]


def _combine(ys, runs, x1, route, tokens, mod4, mod_row_of_tile, final_g, tile_base):
    tiles = tokens // OUTPROJ_TILE
    tok = lambda w: pl.BlockSpec((OUTPROJ_TILE, w), lambda i, *_: (i + tile_base, 0))
    grid_spec = pltpu.PrefetchScalarGridSpec(
        num_scalar_prefetch=3,
        grid=(tiles,),
        in_specs=[
            tok(D_MODEL), tok(ROUTER_COLS),
            pl.BlockSpec((1, 1, 1, D_MODEL), lambda i, *_: (mod_row_of_tile(i + tile_base), 5, 0, 0)),
            pl.BlockSpec((1, D_MODEL), lambda i, *_: (0, 0)),
            pl.BlockSpec(memory_space=pl.ANY),
        ],
        out_specs=pl.BlockSpec((OUTPROJ_TILE, D_MODEL), lambda i, *_: (i, 0)),
        scratch_shapes=[pltpu.VMEM((2, LOCAL_ROWS, D_MODEL), F32), pltpu.SemaphoreType.DMA((2,))],
    )
    return pl.pallas_call(
        functools.partial(_combine_kernel, tiles, tile_base),
        grid_spec=grid_spec,
        out_shape=jax.ShapeDtypeStruct((tokens, D_MODEL), F32),
        compiler_params=pltpu.CompilerParams(
            dimension_semantics=("arbitrary",), vmem_limit_bytes=VMEM_LIMIT),
        name="combine",
    )(*runs, x1, route, mod4, final_g, ys)


def _routing_tables(counts):
    counts = counts.astype(jnp.int32)
    tiles = counts.shape[0]
    run_len = ((counts + SUBLANES - 1) // SUBLANES) * SUBLANES
    run_src = jnp.cumsum(run_len, axis=1) - run_len
    group_rows = jnp.sum(run_len, axis=0)
    padded = ((group_rows + GROUP_TILE - 1) // GROUP_TILE) * GROUP_TILE
    ends = jnp.cumsum(padded)
    offs = ends - padded
    run_dst = offs[None, :] + jnp.cumsum(run_len, axis=0) - run_len

    n_used = ends[-1] // GROUP_TILE
    max_rows = tiles * (OUTPROJ_TILE + N_GROUPS * (SUBLANES - 1))
    max_tiles = -(-max_rows // GROUP_TILE) + N_GROUPS
    tile_ids = jnp.minimum(jnp.arange(max_tiles, dtype=jnp.int32), n_used - 1)
    tile_group = jnp.sum(tile_ids[:, None] * GROUP_TILE >= ends[None, :], axis=1).astype(jnp.int32)
    of_group = (tile_group[:, None] == jnp.arange(N_GROUPS, dtype=jnp.int32))[:, None, :]
    start = jnp.sum(jnp.where(of_group, run_dst[None], 0), axis=-1)
    stop = start + jnp.sum(jnp.where(of_group, run_len[None], 0), axis=-1)
    row0 = (tile_ids * GROUP_TILE)[:, None]
    first = jnp.sum(stop <= row0, axis=1).astype(jnp.int32)
    last = jnp.sum(start < row0 + GROUP_TILE, axis=1).astype(jnp.int32) - 1
    group_end = jnp.sum(jnp.where(of_group[:, 0, :], (offs + group_rows)[None, :], 0), axis=-1)
    fill = jnp.clip(group_end - row0[:, 0], 0, GROUP_TILE).astype(jnp.int32)
    flat = lambda a: a.reshape(-1)
    return ((tile_group, n_used.reshape(1), first, last, fill), (flat(run_len), flat(run_src), flat(run_dst)),
            max_tiles * GROUP_TILE)


def _mixer(x, mod4, mod_row, is_grid, s_f0, s_b0, p):
    norm_mix_g, w_in_bf, conv_w, decay_rows = p
    y_conv, *qkvg = _inproj(x, mod4, mod_row, norm_mix_g, w_in_bf, conv_w, is_grid)
    per_seq = lambda a: a.reshape(x.shape[0], x.shape[1], a.shape[-1])
    ret = _retention(*map(per_seq, qkvg), decay_rows, s_f0, s_b0, emit_state=not is_grid)
    return y_conv, ret[0].reshape(-1, RET_W), ret[1:]


def kernel(x_prompt, x_sample, state_ret_fwd, state_ret_bwd, c, c_ctx, norm_mix_g, norm_ffn_g, w_ada, b_ada, w_in, conv_w, ret_decay_fwd, ret_decay_bwd, w_out, w_router_group, b_router_group, w_router_expert, b_router_expert, w_gate_e, w_up_e, w_down_e, final_norm_g):
    assert norm_mix_g.shape[0] == 1, "single-layer backbone"
    n_lat = c.shape[0]
    ctx_row = n_lat
    mod_rows = 8
    cvec = jnp.concatenate([c, c_ctx[None, :], jnp.zeros((mod_rows - n_lat - 1, D_MODEL), F32)], axis=0)
    mod = _modulation(cvec, w_ada[0], b_ada[0][None, :])
    mod4 = mod.reshape(mod_rows, 6, 1, D_MODEL)

    pad = ROUTER_COLS - N_GROUPS - N_EXPERTS
    w_router = jnp.concatenate(
        [w_router_group[0], w_router_expert[0], jnp.zeros((D_MODEL, pad), F32)], axis=1).astype(BF16)
    b_router = jnp.concatenate([b_router_group[0], b_router_expert[0], jnp.zeros((pad,), F32)])[None, :]
    decay_rows = jnp.broadcast_to(
        jnp.concatenate([ret_decay_fwd[0], ret_decay_bwd[0]])[:, None], (2 * RET_HEADS, LANES))
    p_mix = (norm_mix_g, w_in[0].astype(BF16), conv_w[0], decay_rows)
    w_out_bf = w_out[0].astype(BF16)
    final_g = final_norm_g[None, :]

    ctx_tokens = x_prompt.shape[0] * x_prompt.shape[1]
    lat_tokens = x_sample.shape[0] * x_sample.shape[1]
    ctx_tiles = ctx_tokens // OUTPROJ_TILE
    lat_tiles_per_seq = x_sample.shape[1] // OUTPROJ_TILE
    tile_mod = lambda i: jnp.where(i < ctx_tiles, ctx_row, (i - ctx_tiles) // lat_tiles_per_seq)
    flat = lambda a: a.reshape(-1, a.shape[-1])

    yc_c, yr_c, (s_f, s_b) = _mixer(x_prompt, mod4, lambda b: ctx_row, False, None, None, p_mix)
    yc_l, yr_l, _ = _mixer(x_sample, mod4, lambda b: b, True, state_ret_fwd, state_ret_bwd, p_mix)

    x1, xloc, route, cnt = _outproj((yc_c, yr_c, flat(x_prompt)), (yc_l, yr_l, flat(x_sample)), mod4, tile_mod,
                                    norm_ffn_g, w_out_bf, w_router, b_router)
    tile_tables, runs, slots = _routing_tables(cnt[:, 0, :N_GROUPS])
    ys = _experts(xloc, tile_tables, runs, slots, w_gate_e[0], w_up_e[0], w_down_e[0])
    y_prompt = _combine(ys, runs, x1, route, ctx_tokens, mod4, tile_mod, final_g, 0)
    y_sample = _combine(ys, runs, x1, route, lat_tokens, mod4, tile_mod, final_g, ctx_tiles)
    return (y_prompt.reshape(x_prompt.shape), y_sample.reshape(x_sample.shape),
            s_f.astype(x_prompt.dtype), s_b.astype(x_prompt.dtype))
```

```python
import functools

import jax
import jax.numpy as jnp
import numpy as np
from jax import lax
from jax.experimental import pallas as pl
from jax.experimental.pallas import tpu as pltpu

F32 = jnp.float32
BF16 = jnp.bfloat16

D_MODEL = 1024
GRID_W = 64
CONV_W = 512
RET_HEADS = 4
RET_DK = 128
RET_DV = 128
RET_W = RET_HEADS * RET_DV
QK_W = RET_HEADS * RET_DK
CHUNK = 128
N_GROUPS = 4
EXPERTS_PER_GROUP = 8
N_EXPERTS = N_GROUPS * EXPERTS_PER_GROUP
D_EXPERT = 256
ROPE_BASE = 10000.0
EPS = 1e-6

LANES = 128
TOKEN_TILE = 1024
OUTPROJ_TILE = 512
GROUP_TILE = 1024
EXPERTS_PER_STEP = 4
RET_UNROLL = 8
SUBLANES = 8
XLOC_ROWS = OUTPROJ_TILE + N_GROUPS * SUBLANES
LOCAL_ROWS = OUTPROJ_TILE + LANES
ROW_W = D_MODEL + LANES
ROUTE_GROUP, ROUTE_LOCAL, ROUTE_E1, ROUTE_E2, ROUTE_W1, ROUTE_W2 = range(6)
MOD_COLS = 1536
ROUTER_COLS = LANES
VMEM_LIMIT = 48 * 1024 * 1024
EXPERT_VMEM_LIMIT = 56 * 1024 * 1024


def _silu(x):
    return x * jax.nn.sigmoid(x)


def _rms(x):
    return x * lax.rsqrt(jnp.mean(x * x, axis=-1, keepdims=True) + EPS)


def _bdot(a, b):
    return jnp.dot(a.astype(BF16), b.astype(BF16), preferred_element_type=F32)


def _split3(x):
    hi = x.astype(BF16)
    rest = x - hi.astype(F32)
    mid = rest.astype(BF16)
    return hi, mid, (rest - mid.astype(F32)).astype(BF16)


def _mod_kernel(c_ref, w_ref, b_ref, o_ref):
    o_ref[...] = _bdot(_silu(c_ref[...]), w_ref[...]) + b_ref[...]


def _modulation(cvec, w_ada, b_ada):
    rows = cvec.shape[0]
    n = w_ada.shape[1]
    return pl.pallas_call(
        _mod_kernel,
        grid=(n // MOD_COLS,),
        in_specs=[
            pl.BlockSpec((rows, D_MODEL), lambda j: (0, 0)),
            pl.BlockSpec((D_MODEL, MOD_COLS), lambda j: (0, j)),
            pl.BlockSpec((1, MOD_COLS), lambda j: (0, j)),
        ],
        out_specs=pl.BlockSpec((rows, MOD_COLS), lambda j: (0, j)),
        out_shape=jax.ShapeDtypeStruct((rows, n), F32),
        compiler_params=pltpu.CompilerParams(vmem_limit_bytes=VMEM_LIMIT),
        name="modulation",
    )(cvec, w_ada, b_ada)


def _inproj_kernel(seg, is_grid, x_ref, sh_ref, sc_ref, ng_ref, w_ref, cw_ref, *rest):
    if is_grid:
        cos_ref, sa_ref, sb_ref, yc_ref, q_ref, k_ref, v_ref, g_ref = rest
    else:
        yc_ref, q_ref, k_ref, v_ref, g_ref = rest
    x = x_ref[...]
    xn = (_rms(x) * ng_ref[...]) * (1.0 + sc_ref[0, 0]) + sh_ref[0, 0]
    xb = xn.astype(BF16)

    def proj(c0, n):
        return jnp.dot(xb, w_ref[:, c0:c0 + n], preferred_element_type=F32)

    gate_b = proj(0, CONV_W)
    u = proj(CONV_W, CONV_W) * proj(2 * CONV_W, CONV_W)
    rows = u.shape[0]
    pos = lax.broadcasted_iota(jnp.int32, u.shape, 0) & (seg - 1)
    u_prev = jnp.where(pos != 0, pltpu.roll(u, 1, 0), 0.0)
    u_next = jnp.where(pos != seg - 1, pltpu.roll(u, rows - 1, 0), 0.0)
    conv = cw_ref[0:1, :] * u_prev + cw_ref[1:2, :] * u + cw_ref[2:3, :] * u_next
    yc_ref[...] = (gate_b * conv).astype(yc_ref.dtype)

    q0 = 3 * CONV_W
    q = proj(q0, QK_W)
    k = proj(q0 + QK_W, QK_W)
    if is_grid:
        cos, sa, sb = cos_ref[...], sa_ref[...], sb_ref[...]

        def rope(t):
            out = []
            for h in range(RET_HEADS):
                th = t[:, h * RET_DK:(h + 1) * RET_DK]
                out.append(th * cos + pltpu.roll(th, RET_DK - 1, 1) * sa + pltpu.roll(th, 1, 1) * sb)
            return jnp.concatenate(out, axis=1)

        q, k = rope(q), rope(k)
    q_ref[...] = q
    k_ref[...] = k
    v_ref[...] = proj(q0 + 2 * QK_W, RET_W)
    g_ref[...] = proj(q0 + 2 * QK_W + RET_W, RET_W)


def _rope_tables(length):
    pos = np.arange(length)
    row = (pos // GRID_W).astype(np.float64)
    col = (pos % GRID_W).astype(np.float64)
    n_pairs = RET_DK // 4
    freqs = ROPE_BASE ** (-(np.arange(n_pairs, dtype=np.float64) * 2.0 / (RET_DK // 2)))
    ang = np.concatenate([row[:, None] * freqs, col[:, None] * freqs], axis=-1)
    cos = np.repeat(np.cos(ang), 2, axis=-1)
    sin = np.repeat(np.sin(ang), 2, axis=-1)
    even = (np.arange(RET_DK) % 2) == 0
    return tuple(jnp.asarray(t, F32) for t in (cos, np.where(even, -sin, 0.0), np.where(even, 0.0, sin)))


def _inproj(x, mod4, mod_row, norm_g, w_in_bf, conv_w, is_grid):
    bsz, length, _ = x.shape
    seg = GRID_W if is_grid else length
    assert TOKEN_TILE % seg == 0 and (length % TOKEN_TILE == 0 or TOKEN_TILE % length == 0)
    tokens = bsz * length
    tiles_per_seq = max(length // TOKEN_TILE, 1)
    seqs_per_tile = max(TOKEN_TILE // length, 1)
    batch_of = lambda i: (i // tiles_per_seq) * seqs_per_tile

    def mod_spec(which):
        return pl.BlockSpec((1, 1, 1, D_MODEL), lambda i: (mod_row(batch_of(i)), which, 0, 0))

    def tok_spec(width):
        return pl.BlockSpec((TOKEN_TILE, width), lambda i: (i, 0))

    in_specs = [
        tok_spec(D_MODEL), mod_spec(0), mod_spec(1),
        pl.BlockSpec((1, D_MODEL), lambda i: (0, 0)),
        pl.BlockSpec(w_in_bf.shape, lambda i: (0, 0)),
        pl.BlockSpec(conv_w.shape, lambda i: (0, 0)),
    ]
    args = [x.reshape(tokens, D_MODEL), mod4, mod4, norm_g, w_in_bf, conv_w]
    if is_grid:
        assert length % TOKEN_TILE == 0
        in_specs += [pl.BlockSpec((TOKEN_TILE, RET_DK), lambda i: (i % tiles_per_seq, 0))] * 3
        args += list(_rope_tables(length))
    shp = lambda w, dt: jax.ShapeDtypeStruct((tokens, w), dt)
    return pl.pallas_call(
        functools.partial(_inproj_kernel, seg, is_grid),
        grid=(tokens // TOKEN_TILE,),
        in_specs=in_specs,
        out_specs=[tok_spec(CONV_W), tok_spec(QK_W), tok_spec(QK_W), tok_spec(RET_W), tok_spec(RET_W)],
        out_shape=[shp(CONV_W, BF16), shp(QK_W, F32), shp(QK_W, F32), shp(RET_W, F32), shp(RET_W, F32)],
        compiler_params=pltpu.CompilerParams(
            dimension_semantics=("parallel",), vmem_limit_bytes=VMEM_LIMIT),
        name="inproj_grid" if is_grid else "inproj_seq",
    )(*args)


def _ret_kernel(n_chunks, heads, has_init, emit_state, a_ref, q_ref, k_ref, v_ref, g_ref, *rest):
    rest = list(rest)
    if has_init:
        sf0_ref, sb0_ref = rest[:2]
        rest = rest[2:]
    y_ref = rest.pop(0)
    if emit_state:
        sf_out, sb_out = rest[:2]
        rest = rest[2:]
    st_f, st_b, dec = rest
    c = CHUNK
    sq = (c, c)
    head0 = pl.program_id(0) * heads

    def log_decays(hh):
        lg_f = jnp.log1p(-jnp.exp(a_ref[pl.ds(head0 + hh, 1), :]))
        lg_b = jnp.log1p(-jnp.exp(a_ref[pl.ds(head0 + hh + RET_HEADS, 1), :]))
        return lg_f, lg_b

    @pl.when(pl.program_id(1) == 0)
    def _():
        row = lax.broadcasted_iota(jnp.int32, sq, 0).astype(F32)
        col = lax.broadcasted_iota(jnp.int32, sq, 1).astype(F32)
        scale = RET_DK ** -0.5
        for hh in range(heads):
            lg_f, lg_b = log_decays(hh)
            dec[hh, 0] = scale * (
                jnp.where(row >= col, jnp.exp(jnp.where(row >= col, row - col, 0.0) * lg_f), 0.0)
                + jnp.where(col >= row, jnp.exp(jnp.where(col >= row, col - row, 0.0) * lg_b), 0.0))
            dec[hh, 1] = jnp.exp((row + 1.0) * lg_f)
            dec[hh, 2] = jnp.exp((c - row) * lg_b)
            dec[hh, 3] = scale * jnp.exp((c - 1.0 - col) * lg_f)
            dec[hh, 4] = scale * jnp.exp(col * lg_b)

    def rows(n):
        return pl.ds(pl.multiple_of(n * c, c), c) if not isinstance(n, int) else pl.ds(n * c, c)

    def cols(hh):
        return slice(hh * RET_DK, (hh + 1) * RET_DK)

    def kv_step(hh, n):
        kt = jnp.transpose(k_ref[0, rows(n), cols(hh)])
        lhs = jnp.concatenate([kt * dec[hh, 3], kt * dec[hh, 4]], axis=0)
        kv = _bdot(lhs, v_ref[0, rows(n), cols(hh)])
        st_f[hh, n] = kv[:RET_DK]
        st_b[hh, n] = kv[RET_DK:]

    def scan(hh, st, decay, order, s):
        def step(i, s):
            n = order(i)
            kv = st[hh, n]
            st[hh, n] = s
            return s * decay + kv
        if n_chunks <= RET_UNROLL:
            for i in range(n_chunks):
                s = step(i, s)
            return s
        return lax.fori_loop(0, n_chunks, step, s, unroll=RET_UNROLL)

    def out_step(hh, n):
        q = q_ref[0, rows(n), cols(hh)]
        scores = lax.dot_general(q.astype(BF16), k_ref[0, rows(n), cols(hh)].astype(BF16),
                                 (((1,), (1,)), ((), ())), preferred_element_type=F32)
        o = _bdot(scores * dec[hh, 0], v_ref[0, rows(n), cols(hh)])
        q_dec = jnp.concatenate([q * dec[hh, 1], q * dec[hh, 2]], axis=1)
        o = o + _bdot(q_dec, jnp.concatenate([st_f[hh, n], st_b[hh, n]], axis=0))
        y = _silu(g_ref[0, rows(n), cols(hh)]) * _rms(o)
        y_ref[0, rows(n), cols(hh)] = y.astype(y_ref.dtype)

    def over_chunks(step):
        if n_chunks * heads <= RET_UNROLL:
            for hh in range(heads):
                for n in range(n_chunks):
                    step(hh, n)
        else:
            for hh in range(heads):
                lax.fori_loop(0, n_chunks, lambda n, carry: (step(hh, n), carry)[1], 0, unroll=RET_UNROLL)

    over_chunks(kv_step)
    finals = []
    for hh in range(heads):
        lg_f, lg_b = log_decays(hh)
        s_f = sf0_ref[0, 0, hh] if has_init else jnp.zeros(sq, F32)
        s_b = sb0_ref[0, 0, hh] if has_init else jnp.zeros(sq, F32)
        s_f = scan(hh, st_f, jnp.exp(c * lg_f), lambda i: i, s_f)
        s_b = scan(hh, st_b, jnp.exp(c * lg_b), lambda i: n_chunks - 1 - i, s_b)
        finals.append((s_f, s_b))
    over_chunks(out_step)
    if emit_state:
        for hh, (s_f, s_b) in enumerate(finals):
            sf_out[0, 0, hh] = s_f
            sb_out[0, 0, hh] = s_b


def _retention(q, k, v, g, decay_rows, s_f0, s_b0, emit_state):
    bsz, length, _ = q.shape
    n_chunks = length // CHUNK
    has_init = s_f0 is not None
    heads = RET_HEADS if n_chunks * RET_HEADS <= RET_UNROLL else 1
    head_spec = pl.BlockSpec((1, length, heads * RET_DK), lambda h, b: (b, 0, h))
    st_spec = pl.BlockSpec((1, 1, heads, RET_DK, RET_DV), lambda h, b: (b, 0, h, 0, 0))
    in_specs = [pl.BlockSpec(decay_rows.shape, lambda h, b: (0, 0))] + [head_spec] * 4
    args = [decay_rows, q, k, v, g]
    if has_init:
        in_specs += [st_spec, st_spec]
        args += [s_f0, s_b0]
    out_specs = [head_spec]
    out_shape = [jax.ShapeDtypeStruct((bsz, length, RET_W), BF16)]
    if emit_state:
        st_shape = jax.ShapeDtypeStruct((bsz, 1, RET_HEADS, RET_DK, RET_DV), F32)
        out_specs += [st_spec, st_spec]
        out_shape += [st_shape, st_shape]
    return pl.pallas_call(
        functools.partial(_ret_kernel, n_chunks, heads, has_init, emit_state),
        grid=(RET_HEADS // heads, bsz),
        in_specs=in_specs,
        out_specs=out_specs,
        out_shape=out_shape,
        scratch_shapes=[
            pltpu.VMEM((heads, n_chunks, RET_DK, RET_DV), F32),
            pltpu.VMEM((heads, n_chunks, RET_DK, RET_DV), F32),
            pltpu.VMEM((heads, 5, CHUNK, CHUNK), F32),
        ],
        compiler_params=pltpu.CompilerParams(
            dimension_semantics=("arbitrary", "arbitrary"), vmem_limit_bytes=VMEM_LIMIT),
        name="retention_init" if has_init else "retention_zero",
    )(*args)


def _route(logits):
    lane = lax.broadcasted_iota(jnp.int32, logits.shape, 1)
    lane_f = lane.astype(F32)
    neg = -jnp.inf
    far = float(LANES)
    is_g = lane < N_GROUPS
    lg = jnp.where(is_g, logits, neg)
    g_max = jnp.max(lg, axis=1, keepdims=True)
    g_idx = jnp.min(jnp.where(lg == g_max, lane_f, far), axis=1, keepdims=True)
    p_sel = 1.0 / jnp.sum(jnp.where(is_g, jnp.exp(lg - g_max), 0.0), axis=1, keepdims=True)
    lane_group = ((lane - N_GROUPS) >> (EXPERTS_PER_GROUP.bit_length() - 1)).astype(F32)
    sel = (lane >= N_GROUPS) & (lane < N_GROUPS + N_EXPERTS) & (lane_group == g_idx)
    le = jnp.where(sel, logits, neg)
    v1 = jnp.max(le, axis=1, keepdims=True)
    i1 = jnp.min(jnp.where(le == v1, lane_f, far), axis=1, keepdims=True)
    le2 = jnp.where(lane_f == i1, neg, le)
    v2 = jnp.max(le2, axis=1, keepdims=True)
    i2 = jnp.min(jnp.where(le2 == v2, lane_f, far), axis=1, keepdims=True)
    e2 = jnp.exp(v2 - v1)
    w1 = p_sel * (1.0 / (1.0 + e2))
    w2 = p_sel * (e2 / (1.0 + e2))
    return lane, lane_f, g_idx, i1, i2, w1, w2


def _outproj_kernel(ctx_tiles, yc_c, yr_c, x_c, yc_l, yr_l, x_l, *rest):
    @pl.when(pl.program_id(0) < ctx_tiles)
    def _():
        _outproj_tile(yc_c, yr_c, x_c, *rest)

    @pl.when(pl.program_id(0) >= ctx_tiles)
    def _():
        _outproj_tile(yc_l, yr_l, x_l, *rest)


def _outproj_tile(yc_ref, yr_ref, x_ref, g1_ref, sh_ref, sc_ref, ng_ref, wo_ref, wr_ref, br_ref,
                  x1_ref, xloc_ref, route_ref, cnt_ref):
    m = (jnp.dot(yc_ref[...], wo_ref[0:CONV_W, :], preferred_element_type=F32)
         + jnp.dot(yr_ref[...], wo_ref[CONV_W:, :], preferred_element_type=F32))
    x1 = x_ref[...] + g1_ref[0, 0] * m
    x1_ref[...] = x1
    xn = (_rms(x1) * ng_ref[...]) * (1.0 + sc_ref[0, 0]) + sh_ref[0, 0]
    xb = xn.astype(BF16)
    logits = jnp.dot(xb, wr_ref[...], preferred_element_type=F32) + br_ref[...]
    lane, lane_f, g_idx, i1, i2, w1, w2 = _route(logits)

    picks = jnp.where(lane_f == g_idx, 1.0, 0.0)
    rows = picks.shape[0]
    tri = (lax.broadcasted_iota(jnp.int32, (rows, rows), 0)
           > lax.broadcasted_iota(jnp.int32, (rows, rows), 1))
    before = jnp.dot(jnp.where(tri, 1.0, 0.0).astype(BF16), picks.astype(BF16),
                     preferred_element_type=F32)
    count = jnp.sum(picks, axis=0, keepdims=True)
    cnt_ref[0] = count
    count8 = jnp.broadcast_to(jnp.floor((count + (SUBLANES - 1)) * (1.0 / SUBLANES)) * SUBLANES,
                              (SUBLANES, LANES))
    lane8 = lane[:SUBLANES]
    start = sum(jnp.where(lane8 >= k, pltpu.roll(count8, k, 1), 0.0) for k in range(1, N_GROUPS))
    local = jnp.sum(jnp.where(lane_f == g_idx, before + start[0:1], 0.0), axis=1, keepdims=True)
    route = jnp.where(lane == ROUTE_GROUP, g_idx, jnp.where(lane == ROUTE_LOCAL, local, jnp.where(
        lane == ROUTE_E1, i1 - N_GROUPS, jnp.where(lane == ROUTE_E2, i2 - N_GROUPS, jnp.where(
            lane == ROUTE_W1, w1, jnp.where(lane == ROUTE_W2, w2, 0.0))))))
    route_ref[...] = route

    local_row = jnp.transpose(jnp.broadcast_to(local, (rows, LANES)))[0:1, :]
    place = jnp.where(lax.broadcasted_iota(jnp.int32, (XLOC_ROWS, rows), 0).astype(F32) == local_row,
                      1.0, 0.0).astype(BF16)
    xloc_ref[0, :, :D_MODEL] = jnp.dot(place, xb, preferred_element_type=F32)
    xloc_ref[0, :, D_MODEL:] = sum(
        jnp.dot(place, piece, preferred_element_type=F32) for piece in _split3(route))


def _outproj(ctx, lat, mod4, mod_row_of_tile, norm_g, w_out_bf, w_router_bf, b_router):
    ctx_tiles = ctx[2].shape[0] // OUTPROJ_TILE
    tiles = ctx_tiles + lat[2].shape[0] // OUTPROJ_TILE
    tokens = tiles * OUTPROJ_TILE

    def mod_spec(which):
        return pl.BlockSpec((1, 1, 1, D_MODEL), lambda i: (mod_row_of_tile(i), which, 0, 0))

    ctx_tok = lambda w: pl.BlockSpec((OUTPROJ_TILE, w), lambda i: (jnp.minimum(i, ctx_tiles - 1), 0))
    lat_tok = lambda w: pl.BlockSpec((OUTPROJ_TILE, w), lambda i: (jnp.maximum(i - ctx_tiles, 0), 0))
    tok = lambda w: pl.BlockSpec((OUTPROJ_TILE, w), lambda i: (i, 0))
    full = lambda a: pl.BlockSpec(a.shape, lambda i: (0,) * a.ndim)
    widths = (CONV_W, RET_W, D_MODEL)
    return pl.pallas_call(
        functools.partial(_outproj_kernel, ctx_tiles),
        grid=(tiles,),
        in_specs=[ctx_tok(w) for w in widths] + [lat_tok(w) for w in widths] + [
            mod_spec(2), mod_spec(3), mod_spec(4),
            full(norm_g), full(w_out_bf), full(w_router_bf), full(b_router)],
        out_specs=[tok(D_MODEL),
                   pl.BlockSpec((1, XLOC_ROWS, ROW_W), lambda i: (i, 0, 0)),
                   tok(ROUTER_COLS),
                   pl.BlockSpec((1, 1, ROUTER_COLS), lambda i: (i, 0, 0))],
        out_shape=[jax.ShapeDtypeStruct((tokens, D_MODEL), F32),
                   jax.ShapeDtypeStruct((tiles, XLOC_ROWS, ROW_W), F32),
                   jax.ShapeDtypeStruct((tokens, ROUTER_COLS), F32),
                   jax.ShapeDtypeStruct((tiles, 1, ROUTER_COLS), F32)],
        compiler_params=pltpu.CompilerParams(
            dimension_semantics=("parallel",), vmem_limit_bytes=VMEM_LIMIT),
        name="outproj",
    )(*ctx, *lat, mod4, mod4, mod4, norm_g, w_out_bf, w_router_bf, b_router)


RUN_PIECES = tuple(SUBLANES << b for b in reversed(range((OUTPROJ_TILE // SUBLANES).bit_length())))


def _expert_kernel(tile_group_ref, n_used_ref, first_ref, last_ref, fill_ref,
                   run_len_ref, run_src_ref, run_dst_ref,
                   xloc_hbm, w1_ref, w3_ref, w2_ref, ys_ref, xbuf, xb, gate_tabs, sem):
    j = pl.program_id(0)
    step = pl.program_id(1)
    n_used = n_used_ref[0]

    def tile_fetch(tile):
        slot = tile % 2
        group = tile_group_ref[tile]
        row0 = tile * GROUP_TILE

        def from_token_tile(b, carry):
            run = b * N_GROUPS + group
            lo = jnp.maximum(run_dst_ref[run], row0)
            hi = jnp.minimum(run_dst_ref[run] + run_len_ref[run], row0 + GROUP_TILE)
            n = jnp.maximum(hi - lo, 0)
            src = run_src_ref[run] + lo - run_dst_ref[run]
            dst = lo - row0
            for size in RUN_PIECES:
                done = n & (-2 * size)

                @pl.when((n & size) != 0)
                def _():
                    pltpu.make_async_copy(
                        xloc_hbm.at[b, pl.ds(pl.multiple_of(src + done, SUBLANES), size)],
                        xbuf.at[slot, pl.ds(pl.multiple_of(dst + done, SUBLANES), size)], sem.at[slot]).start()
            return carry

        lax.fori_loop(first_ref[tile], last_ref[tile] + 1, from_token_tile, 0)

    def tile_wait(tile):
        slot = tile % 2
        for size in (GROUP_TILE,) + tuple(GROUP_TILE >> k for k in range(1, (GROUP_TILE // SUBLANES).bit_length())):
            @pl.when((fill_ref[tile] & size) != 0)
            def _():
                pltpu.make_async_copy(xbuf.at[1 - slot, pl.ds(0, size)], xbuf.at[slot, pl.ds(0, size)],
                                      sem.at[slot]).wait()

    @pl.when(j < n_used)
    def _():
        @pl.when(step == 0)
        def _():
            @pl.when(j == 0)
            def _():
                xbuf[...] = jnp.zeros_like(xbuf)
                tile_fetch(j)

            tile_wait(j)
            rows_in = xbuf[j % 2]
            xb[...] = rows_in[:, :D_MODEL].astype(BF16)
            route = rows_in[:, D_MODEL:]
            lane = lax.broadcasted_iota(jnp.int32, route.shape, 1)
            for n, which in enumerate((ROUTE_E1, ROUTE_E2, ROUTE_W1, ROUTE_W2)):
                col = jnp.sum(jnp.where(lane == which, route, 0.0), axis=1, keepdims=True)
                gate_tabs[n] = jnp.broadcast_to(col, route.shape)

            @pl.when(j + 1 < n_used)
            def _():
                tile_fetch(j + 1)

        def evaluate(rows):
            x = xb[:rows]
            total = None
            for s in range(EXPERTS_PER_STEP):
                expert = (tile_group_ref[j] * EXPERTS_PER_GROUP + step * EXPERTS_PER_STEP + s).astype(F32)
                gate = (jnp.where(gate_tabs[0, :rows] == expert, gate_tabs[2, :rows], 0.0)
                        + jnp.where(gate_tabs[1, :rows] == expert, gate_tabs[3, :rows], 0.0))
                hid = _silu(jnp.dot(x, w1_ref[0, s].astype(BF16), preferred_element_type=F32)) * jnp.dot(
                    x, w3_ref[0, s].astype(BF16), preferred_element_type=F32)
                y = jnp.dot(hid.astype(BF16), w2_ref[0, s].astype(BF16), preferred_element_type=F32)
                gated = jnp.concatenate(
                    [gate * y[:, c * LANES:(c + 1) * LANES] for c in range(D_MODEL // LANES)], axis=1)
                total = gated if total is None else total + gated

            @pl.when(step == 0)
            def _():
                ys_ref[:rows] = total
                if rows < GROUP_TILE:
                    ys_ref[rows:] = jnp.zeros((GROUP_TILE - rows, D_MODEL), F32)

            @pl.when(step > 0)
            def _():
                ys_ref[:rows] += total

        half = GROUP_TILE // 2

        @pl.when(fill_ref[j] > half)
        def _():
            evaluate(GROUP_TILE)

        @pl.when(fill_ref[j] <= half)
        def _():
            evaluate(half)

    @pl.when((j >= n_used) & (step == 0))
    def _():
        ys_ref[...] = jnp.zeros_like(ys_ref)


def _experts(xloc, tile_tables, runs, slots, w1, w3, w2):
    steps = EXPERTS_PER_GROUP // EXPERTS_PER_STEP
    paired = lambda w: w.reshape((N_EXPERTS // EXPERTS_PER_STEP, EXPERTS_PER_STEP) + w.shape[1:])
    w_spec = lambda shape: pl.BlockSpec((1, EXPERTS_PER_STEP) + shape, lambda j, s, tg, nu, *_: (
        tg[jnp.minimum(j, nu[0] - 1)] * steps + jnp.where(j < nu[0], s, steps - 1), 0, 0, 0))
    grid_spec = pltpu.PrefetchScalarGridSpec(
        num_scalar_prefetch=8,
        grid=(slots // GROUP_TILE, steps),
        in_specs=[
            pl.BlockSpec(memory_space=pl.ANY),
            w_spec((D_MODEL, D_EXPERT)), w_spec((D_MODEL, D_EXPERT)), w_spec((D_EXPERT, D_MODEL)),
        ],
        out_specs=pl.BlockSpec((GROUP_TILE, D_MODEL), lambda j, s, *_: (j, 0)),
        scratch_shapes=[pltpu.VMEM((2, GROUP_TILE, ROW_W), F32), pltpu.VMEM((GROUP_TILE, D_MODEL), BF16),
                        pltpu.VMEM((4, GROUP_TILE, LANES), F32), pltpu.SemaphoreType.DMA((2,))],
    )
    return pl.pallas_call(
        _expert_kernel,
        grid_spec=grid_spec,
        out_shape=jax.ShapeDtypeStruct((slots, D_MODEL), F32),
        compiler_params=pltpu.CompilerParams(
            dimension_semantics=("arbitrary", "arbitrary"), vmem_limit_bytes=EXPERT_VMEM_LIMIT),
        name="experts",
    )(*tile_tables, *runs, xloc, paired(w1), paired(w3), paired(w2))


def _combine_kernel(n_tiles, tile_base, run_len_ref, run_src_ref, run_dst_ref,
                    x1_ref, route_ref, g2_ref, fg_ref, ys_hbm, o_ref, buf, sem):
    i = pl.program_id(0)
    slot = i % 2

    def run_copies(local_tile, act):
        s = local_tile % 2
        tile = local_tile + tile_base
        for g in range(N_GROUPS):
            n = run_len_ref[tile * N_GROUPS + g]
            src = run_src_ref[tile * N_GROUPS + g]
            dst = run_dst_ref[tile * N_GROUPS + g]
            for size in RUN_PIECES:
                done = n & (-2 * size)

                @pl.when((n & size) != 0)
                def _():
                    act(pltpu.make_async_copy(
                        ys_hbm.at[pl.ds(pl.multiple_of(dst + done, SUBLANES), size)],
                        buf.at[s, pl.ds(pl.multiple_of(src + done, SUBLANES), size)], sem.at[s]))

    @pl.when(i == 0)
    def _():
        buf[...] = jnp.zeros_like(buf)
        run_copies(i, lambda cp: cp.start())

    @pl.when(i + 1 < n_tiles)
    def _():
        run_copies(i + 1, lambda cp: cp.start())

    run_copies(i, lambda cp: cp.wait())
    route = route_ref[...]
    lane = lax.broadcasted_iota(jnp.int32, route.shape, 1)
    local = jnp.sum(jnp.where(lane == ROUTE_LOCAL, route, 0.0), axis=1, keepdims=True)
    pick = jnp.where(lax.broadcasted_iota(jnp.int32, (route.shape[0], LOCAL_ROWS), 1).astype(F32) == local,
                     1.0, 0.0).astype(BF16)
    moe = sum(jnp.dot(pick, piece, preferred_element_type=F32) for piece in _split3(buf[slot])[:2])
    y = x1_ref[...] + g2_ref[0, 0] * moe
    o_ref[...] = _rms(y) * fg_ref[...]


def _combine(ys, runs, x1, route, tokens, mod4, mod_row_of_tile, final_g, tile_base):
    tiles = tokens // OUTPROJ_TILE
    tok = lambda w: pl.BlockSpec((OUTPROJ_TILE, w), lambda i, *_: (i + tile_base, 0))
    grid_spec = pltpu.PrefetchScalarGridSpec(
        num_scalar_prefetch=3,
        grid=(tiles,),
        in_specs=[
            tok(D_MODEL), tok(ROUTER_COLS),
            pl.BlockSpec((1, 1, 1, D_MODEL), lambda i, *_: (mod_row_of_tile(i + tile_base), 5, 0, 0)),
            pl.BlockSpec((1, D_MODEL), lambda i, *_: (0, 0)),
            pl.BlockSpec(memory_space=pl.ANY),
        ],
        out_specs=pl.BlockSpec((OUTPROJ_TILE, D_MODEL), lambda i, *_: (i, 0)),
        scratch_shapes=[pltpu.VMEM((2, LOCAL_ROWS, D_MODEL), F32), pltpu.SemaphoreType.DMA((2,))],
    )
    return pl.pallas_call(
        functools.partial(_combine_kernel, tiles, tile_base),
        grid_spec=grid_spec,
        out_shape=jax.ShapeDtypeStruct((tokens, D_MODEL), F32),
        compiler_params=pltpu.CompilerParams(
            dimension_semantics=("arbitrary",), vmem_limit_bytes=VMEM_LIMIT),
        name="combine",
    )(*runs, x1, route, mod4, final_g, ys)


def _routing_tables(counts):
    counts = counts.astype(jnp.int32)
    tiles = counts.shape[0]
    run_len = ((counts + SUBLANES - 1) // SUBLANES) * SUBLANES
    run_src = jnp.cumsum(run_len, axis=1) - run_len
    group_rows = jnp.sum(run_len, axis=0)
    padded = ((group_rows + GROUP_TILE - 1) // GROUP_TILE) * GROUP_TILE
    ends = jnp.cumsum(padded)
    offs = ends - padded
    run_dst = offs[None, :] + jnp.cumsum(run_len, axis=0) - run_len

    n_used = ends[-1] // GROUP_TILE
    max_rows = tiles * (OUTPROJ_TILE + N_GROUPS * (SUBLANES - 1))
    max_tiles = -(-max_rows // GROUP_TILE) + N_GROUPS
    tile_ids = jnp.minimum(jnp.arange(max_tiles, dtype=jnp.int32), n_used - 1)
    tile_group = jnp.sum(tile_ids[:, None] * GROUP_TILE >= ends[None, :], axis=1).astype(jnp.int32)
    of_group = (tile_group[:, None] == jnp.arange(N_GROUPS, dtype=jnp.int32))[:, None, :]
    start = jnp.sum(jnp.where(of_group, run_dst[None], 0), axis=-1)
    stop = start + jnp.sum(jnp.where(of_group, run_len[None], 0), axis=-1)
    row0 = (tile_ids * GROUP_TILE)[:, None]
    first = jnp.sum(stop <= row0, axis=1).astype(jnp.int32)
    last = jnp.sum(start < row0 + GROUP_TILE, axis=1).astype(jnp.int32) - 1
    group_end = jnp.sum(jnp.where(of_group[:, 0, :], (offs + group_rows)[None, :], 0), axis=-1)
    fill = jnp.clip(group_end - row0[:, 0], 0, GROUP_TILE).astype(jnp.int32)
    flat = lambda a: a.reshape(-1)
    return ((tile_group, n_used.reshape(1), first, last, fill), (flat(run_len), flat(run_src), flat(run_dst)),
            max_tiles * GROUP_TILE)


def _mixer(x, mod4, mod_row, is_grid, s_f0, s_b0, p):
    norm_mix_g, w_in_bf, conv_w, decay_rows = p
    y_conv, *qkvg = _inproj(x, mod4, mod_row, norm_mix_g, w_in_bf, conv_w, is_grid)
    per_seq = lambda a: a.reshape(x.shape[0], x.shape[1], a.shape[-1])
    ret = _retention(*map(per_seq, qkvg), decay_rows, s_f0, s_b0, emit_state=not is_grid)
    return y_conv, ret[0].reshape(-1, RET_W), ret[1:]


def kernel(x_prompt, x_sample, state_ret_fwd, state_ret_bwd, c, c_ctx, norm_mix_g, norm_ffn_g, w_ada, b_ada, w_in, conv_w, ret_decay_fwd, ret_decay_bwd, w_out, w_router_group, b_router_group, w_router_expert, b_router_expert, w_gate_e, w_up_e, w_down_e, final_norm_g):
    assert norm_mix_g.shape[0] == 1, "single-layer backbone"
    n_lat = c.shape[0]
    ctx_row = n_lat
    mod_rows = 8
    cvec = jnp.concatenate([c, c_ctx[None, :], jnp.zeros((mod_rows - n_lat - 1, D_MODEL), F32)], axis=0)
    mod = _modulation(cvec, w_ada[0], b_ada[0][None, :])
    mod4 = mod.reshape(mod_rows, 6, 1, D_MODEL)

    pad = ROUTER_COLS - N_GROUPS - N_EXPERTS
    w_router = jnp.concatenate(
        [w_router_group[0], w_router_expert[0], jnp.zeros((D_MODEL, pad), F32)], axis=1).astype(BF16)
    b_router = jnp.concatenate([b_router_group[0], b_router_expert[0], jnp.zeros((pad,), F32)])[None, :]
    decay_rows = jnp.broadcast_to(
        jnp.concatenate([ret_decay_fwd[0], ret_decay_bwd[0]])[:, None], (2 * RET_HEADS, LANES))
    p_mix = (norm_mix_g, w_in[0].astype(BF16), conv_w[0], decay_rows)
    w_out_bf = w_out[0].astype(BF16)
    final_g = final_norm_g[None, :]

    ctx_tokens = x_prompt.shape[0] * x_prompt.shape[1]
    lat_tokens = x_sample.shape[0] * x_sample.shape[1]
    ctx_tiles = ctx_tokens // OUTPROJ_TILE
    lat_tiles_per_seq = x_sample.shape[1] // OUTPROJ_TILE
    tile_mod = lambda i: jnp.where(i < ctx_tiles, ctx_row, (i - ctx_tiles) // lat_tiles_per_seq)
    flat = lambda a: a.reshape(-1, a.shape[-1])

    yc_c, yr_c, (s_f, s_b) = _mixer(x_prompt, mod4, lambda b: ctx_row, False, None, None, p_mix)
    yc_l, yr_l, _ = _mixer(x_sample, mod4, lambda b: b, True, state_ret_fwd, state_ret_bwd, p_mix)

    x1, xloc, route, cnt = _outproj((yc_c, yr_c, flat(x_prompt)), (yc_l, yr_l, flat(x_sample)), mod4, tile_mod,
                                    norm_ffn_g, w_out_bf, w_router, b_router)
    tile_tables, runs, slots = _routing_tables(cnt[:, 0, :N_GROUPS])
    ys = _experts(xloc, tile_tables, runs, slots, w_gate_e[0], w_up_e[0], w_down_e[0])
    y_prompt = _combine(ys, runs, x1, route, ctx_tokens, mod4, tile_mod, final_g, 0)
    y_sample = _combine(ys, runs, x1, route, lat_tokens, mod4, tile_mod, final_g, ctx_tiles)
    return (y_prompt.reshape(x_prompt.shape), y_sample.reshape(x_sample.shape),
            s_f.astype(x_prompt.dtype), s_b.astype(x_prompt.dtype))
```

```python
import functools

import jax
import jax.numpy as jnp
import numpy as np
from jax import lax
from jax.experimental import pallas as pl
from jax.experimental.pallas import tpu as pltpu

F32 = jnp.float32
BF16 = jnp.bfloat16

D_MODEL = 1024
GRID_W = 64
CONV_W = 512
RET_HEADS = 4
RET_DK = 128
RET_DV = 128
RET_W = RET_HEADS * RET_DV
QK_W = RET_HEADS * RET_DK
CHUNK = 128
N_GROUPS = 4
EXPERTS_PER_GROUP = 8
N_EXPERTS = N_GROUPS * EXPERTS_PER_GROUP
D_EXPERT = 256
ROPE_BASE = 10000.0
EPS = 1e-6

LANES = 128
TOKEN_TILE = 1024
OUTPROJ_TILE = 512
GROUP_TILE = 1024
EXPERTS_PER_STEP = 4
RET_UNROLL = 8
SUBLANES = 8
XLOC_ROWS = OUTPROJ_TILE + N_GROUPS * SUBLANES
LOCAL_ROWS = OUTPROJ_TILE + LANES
ROW_W = D_MODEL + LANES
ROUTE_GROUP, ROUTE_LOCAL, ROUTE_E1, ROUTE_E2, ROUTE_W1, ROUTE_W2 = range(6)
MOD_COLS = 1536
ROUTER_COLS = LANES
VMEM_LIMIT = 48 * 1024 * 1024
EXPERT_VMEM_LIMIT = 56 * 1024 * 1024


def _silu(x):
    return x * jax.nn.sigmoid(x)


def _rms(x):
    return x * lax.rsqrt(jnp.mean(x * x, axis=-1, keepdims=True) + EPS)


def _bdot(a, b):
    return jnp.dot(a.astype(BF16), b.astype(BF16), preferred_element_type=F32)


def _split3(x):
    hi = x.astype(BF16)
    rest = x - hi.astype(F32)
    mid = rest.astype(BF16)
    return hi, mid, (rest - mid.astype(F32)).astype(BF16)


def _mod_kernel(c_ref, w_ref, b_ref, o_ref):
    o_ref[...] = _bdot(_silu(c_ref[...]), w_ref[...]) + b_ref[...]


def _modulation(cvec, w_ada, b_ada):
    rows = cvec.shape[0]
    n = w_ada.shape[1]
    return pl.pallas_call(
        _mod_kernel,
        grid=(n // MOD_COLS,),
        in_specs=[
            pl.BlockSpec((rows, D_MODEL), lambda j: (0, 0)),
            pl.BlockSpec((D_MODEL, MOD_COLS), lambda j: (0, j)),
            pl.BlockSpec((1, MOD_COLS), lambda j: (0, j)),
        ],
        out_specs=pl.BlockSpec((rows, MOD_COLS), lambda j: (0, j)),
        out_shape=jax.ShapeDtypeStruct((rows, n), F32),
        compiler_params=pltpu.CompilerParams(vmem_limit_bytes=VMEM_LIMIT),
        name="modulation",
    )(cvec, w_ada, b_ada)


def _inproj_kernel(seg, is_grid, x_ref, sh_ref, sc_ref, ng_ref, w_ref, cw_ref, *rest):
    if is_grid:
        cos_ref, sa_ref, sb_ref, yc_ref, q_ref, k_ref, v_ref, g_ref = rest
    else:
        yc_ref, q_ref, k_ref, v_ref, g_ref = rest
    x = x_ref[...]
    xn = (_rms(x) * ng_ref[...]) * (1.0 + sc_ref[0, 0]) + sh_ref[0, 0]
    xb = xn.astype(BF16)

    def proj(c0, n):
        return jnp.dot(xb, w_ref[:, c0:c0 + n], preferred_element_type=F32)

    gate_b = proj(0, CONV_W)
    u = proj(CONV_W, CONV_W) * proj(2 * CONV_W, CONV_W)
    rows = u.shape[0]
    pos = lax.broadcasted_iota(jnp.int32, u.shape, 0) & (seg - 1)
    u_prev = jnp.where(pos != 0, pltpu.roll(u, 1, 0), 0.0)
    u_next = jnp.where(pos != seg - 1, pltpu.roll(u, rows - 1, 0), 0.0)
    conv = cw_ref[0:1, :] * u_prev + cw_ref[1:2, :] * u + cw_ref[2:3, :] * u_next
    yc_ref[...] = (gate_b * conv).astype(yc_ref.dtype)

    q0 = 3 * CONV_W
    q = proj(q0, QK_W)
    k = proj(q0 + QK_W, QK_W)
    if is_grid:
        cos, sa, sb = cos_ref[...], sa_ref[...], sb_ref[...]

        def rope(t):
            out = []
            for h in range(RET_HEADS):
                th = t[:, h * RET_DK:(h + 1) * RET_DK]
                out.append(th * cos + pltpu.roll(th, RET_DK - 1, 1) * sa + pltpu.roll(th, 1, 1) * sb)
            return jnp.concatenate(out, axis=1)

        q, k = rope(q), rope(k)
    q_ref[...] = q
    k_ref[...] = k
    v_ref[...] = proj(q0 + 2 * QK_W, RET_W)
    g_ref[...] = proj(q0 + 2 * QK_W + RET_W, RET_W)


def _rope_tables(length):
    pos = np.arange(length)
    row = (pos // GRID_W).astype(np.float64)
    col = (pos % GRID_W).astype(np.float64)
    n_pairs = RET_DK // 4
    freqs = ROPE_BASE ** (-(np.arange(n_pairs, dtype=np.float64) * 2.0 / (RET_DK // 2)))
    ang = np.concatenate([row[:, None] * freqs, col[:, None] * freqs], axis=-1)
    cos = np.repeat(np.cos(ang), 2, axis=-1)
    sin = np.repeat(np.sin(ang), 2, axis=-1)
    even = (np.arange(RET_DK) % 2) == 0
    return tuple(jnp.asarray(t, F32) for t in (cos, np.where(even, -sin, 0.0), np.where(even, 0.0, sin)))


def _inproj(x, mod4, mod_row, norm_g, w_in_bf, conv_w, is_grid):
    bsz, length, _ = x.shape
    seg = GRID_W if is_grid else length
    assert TOKEN_TILE % seg == 0 and (length % TOKEN_TILE == 0 or TOKEN_TILE % length == 0)
    tokens = bsz * length
    tiles_per_seq = max(length // TOKEN_TILE, 1)
    seqs_per_tile = max(TOKEN_TILE // length, 1)
    batch_of = lambda i: (i // tiles_per_seq) * seqs_per_tile

    def mod_spec(which):
        return pl.BlockSpec((1, 1, 1, D_MODEL), lambda i: (mod_row(batch_of(i)), which, 0, 0))

    def tok_spec(width):
        return pl.BlockSpec((TOKEN_TILE, width), lambda i: (i, 0))

    in_specs = [
        tok_spec(D_MODEL), mod_spec(0), mod_spec(1),
        pl.BlockSpec((1, D_MODEL), lambda i: (0, 0)),
        pl.BlockSpec(w_in_bf.shape, lambda i: (0, 0)),
        pl.BlockSpec(conv_w.shape, lambda i: (0, 0)),
    ]
    args = [x.reshape(tokens, D_MODEL), mod4, mod4, norm_g, w_in_bf, conv_w]
    if is_grid:
        assert length % TOKEN_TILE == 0
        in_specs += [pl.BlockSpec((TOKEN_TILE, RET_DK), lambda i: (i % tiles_per_seq, 0))] * 3
        args += list(_rope_tables(length))
    shp = lambda w, dt: jax.ShapeDtypeStruct((tokens, w), dt)
    return pl.pallas_call(
        functools.partial(_inproj_kernel, seg, is_grid),
        grid=(tokens // TOKEN_TILE,),
        in_specs=in_specs,
        out_specs=[tok_spec(CONV_W), tok_spec(QK_W), tok_spec(QK_W), tok_spec(RET_W), tok_spec(RET_W)],
        out_shape=[shp(CONV_W, BF16), shp(QK_W, F32), shp(QK_W, F32), shp(RET_W, F32), shp(RET_W, F32)],
        compiler_params=pltpu.CompilerParams(
            dimension_semantics=("parallel",), vmem_limit_bytes=VMEM_LIMIT),
        name="inproj_grid" if is_grid else "inproj_seq",
    )(*args)


def _ret_kernel(n_chunks, heads, has_init, emit_state, a_ref, q_ref, k_ref, v_ref, g_ref, *rest):
    rest = list(rest)
    if has_init:
        sf0_ref, sb0_ref = rest[:2]
        rest = rest[2:]
    y_ref = rest.pop(0)
    if emit_state:
        sf_out, sb_out = rest[:2]
        rest = rest[2:]
    st_f, st_b, dec = rest
    c = CHUNK
    sq = (c, c)
    head0 = pl.program_id(0) * heads

    def log_decays(hh):
        lg_f = jnp.log1p(-jnp.exp(a_ref[pl.ds(head0 + hh, 1), :]))
        lg_b = jnp.log1p(-jnp.exp(a_ref[pl.ds(head0 + hh + RET_HEADS, 1), :]))
        return lg_f, lg_b

    @pl.when(pl.program_id(1) == 0)
    def _():
        row = lax.broadcasted_iota(jnp.int32, sq, 0).astype(F32)
        col = lax.broadcasted_iota(jnp.int32, sq, 1).astype(F32)
        scale = RET_DK ** -0.5
        for hh in range(heads):
            lg_f, lg_b = log_decays(hh)
            dec[hh, 0] = scale * (
                jnp.where(row >= col, jnp.exp(jnp.where(row >= col, row - col, 0.0) * lg_f), 0.0)
                + jnp.where(col >= row, jnp.exp(jnp.where(col >= row, col - row, 0.0) * lg_b), 0.0))
            dec[hh, 1] = jnp.exp((row + 1.0) * lg_f)
            dec[hh, 2] = jnp.exp((c - row) * lg_b)
            dec[hh, 3] = scale * jnp.exp((c - 1.0 - col) * lg_f)
            dec[hh, 4] = scale * jnp.exp(col * lg_b)

    def rows(n):
        return pl.ds(pl.multiple_of(n * c, c), c) if not isinstance(n, int) else pl.ds(n * c, c)

    def cols(hh):
        return slice(hh * RET_DK, (hh + 1) * RET_DK)

    def kv_step(hh, n):
        kt = jnp.transpose(k_ref[0, rows(n), cols(hh)])
        lhs = jnp.concatenate([kt * dec[hh, 3], kt * dec[hh, 4]], axis=0)
        kv = _bdot(lhs, v_ref[0, rows(n), cols(hh)])
        st_f[hh, n] = kv[:RET_DK]
        st_b[hh, n] = kv[RET_DK:]

    def scan(hh, st, decay, order, s):
        def step(i, s):
            n = order(i)
            kv = st[hh, n]
            st[hh, n] = s
            return s * decay + kv
        if n_chunks <= RET_UNROLL:
            for i in range(n_chunks):
                s = step(i, s)
            return s
        return lax.fori_loop(0, n_chunks, step, s, unroll=RET_UNROLL)

    def out_step(hh, n):
        q = q_ref[0, rows(n), cols(hh)]
        scores = lax.dot_general(q.astype(BF16), k_ref[0, rows(n), cols(hh)].astype(BF16),
                                 (((1,), (1,)), ((), ())), preferred_element_type=F32)
        o = _bdot(scores * dec[hh, 0], v_ref[0, rows(n), cols(hh)])
        q_dec = jnp.concatenate([q * dec[hh, 1], q * dec[hh, 2]], axis=1)
        o = o + _bdot(q_dec, jnp.concatenate([st_f[hh, n], st_b[hh, n]], axis=0))
        y = _silu(g_ref[0, rows(n), cols(hh)]) * _rms(o)
        y_ref[0, rows(n), cols(hh)] = y.astype(y_ref.dtype)

    def over_chunks(step):
        if n_chunks * heads <= RET_UNROLL:
            for hh in range(heads):
                for n in range(n_chunks):
                    step(hh, n)
        else:
            for hh in range(heads):
                lax.fori_loop(0, n_chunks, lambda n, carry: (step(hh, n), carry)[1], 0, unroll=RET_UNROLL)

    over_chunks(kv_step)
    finals = []
    for hh in range(heads):
        lg_f, lg_b = log_decays(hh)
        s_f = sf0_ref[0, 0, hh] if has_init else jnp.zeros(sq, F32)
        s_b = sb0_ref[0, 0, hh] if has_init else jnp.zeros(sq, F32)
        s_f = scan(hh, st_f, jnp.exp(c * lg_f), lambda i: i, s_f)
        s_b = scan(hh, st_b, jnp.exp(c * lg_b), lambda i: n_chunks - 1 - i, s_b)
        finals.append((s_f, s_b))
    over_chunks(out_step)
    if emit_state:
        for hh, (s_f, s_b) in enumerate(finals):
            sf_out[0, 0, hh] = s_f
            sb_out[0, 0, hh] = s_b


def _retention(q, k, v, g, decay_rows, s_f0, s_b0, emit_state):
    bsz, length, _ = q.shape
    n_chunks = length // CHUNK
    has_init = s_f0 is not None
    heads = RET_HEADS if n_chunks * RET_HEADS <= RET_UNROLL else 1
    head_spec = pl.BlockSpec((1, length, heads * RET_DK), lambda h, b: (b, 0, h))
    st_spec = pl.BlockSpec((1, 1, heads, RET_DK, RET_DV), lambda h, b: (b, 0, h, 0, 0))
    in_specs = [pl.BlockSpec(decay_rows.shape, lambda h, b: (0, 0))] + [head_spec] * 4
    args = [decay_rows, q, k, v, g]
    if has_init:
        in_specs += [st_spec, st_spec]
        args += [s_f0, s_b0]
    out_specs = [head_spec]
    out_shape = [jax.ShapeDtypeStruct((bsz, length, RET_W), BF16)]
    if emit_state:
        st_shape = jax.ShapeDtypeStruct((bsz, 1, RET_HEADS, RET_DK, RET_DV), F32)
        out_specs += [st_spec, st_spec]
        out_shape += [st_shape, st_shape]
    return pl.pallas_call(
        functools.partial(_ret_kernel, n_chunks, heads, has_init, emit_state),
        grid=(RET_HEADS // heads, bsz),
        in_specs=in_specs,
        out_specs=out_specs,
        out_shape=out_shape,
        scratch_shapes=[
            pltpu.VMEM((heads, n_chunks, RET_DK, RET_DV), F32),
            pltpu.VMEM((heads, n_chunks, RET_DK, RET_DV), F32),
            pltpu.VMEM((heads, 5, CHUNK, CHUNK), F32),
        ],
        compiler_params=pltpu.CompilerParams(
            dimension_semantics=("arbitrary", "arbitrary"), vmem_limit_bytes=VMEM_LIMIT),
        name="retention_init" if has_init else "retention_zero",
    )(*args)


def _route(logits):
    lane = lax.broadcasted_iota(jnp.int32, logits.shape, 1)
    lane_f = lane.astype(F32)
    neg = -jnp.inf
    far = float(LANES)
    is_g = lane < N_GROUPS
    lg = jnp.where(is_g, logits, neg)
    g_max = jnp.max(lg, axis=1, keepdims=True)
    g_idx = jnp.min(jnp.where(lg == g_max, lane_f, far), axis=1, keepdims=True)
    p_sel = 1.0 / jnp.sum(jnp.where(is_g, jnp.exp(lg - g_max), 0.0), axis=1, keepdims=True)
    lane_group = ((lane - N_GROUPS) >> (EXPERTS_PER_GROUP.bit_length() - 1)).astype(F32)
    sel = (lane >= N_GROUPS) & (lane < N_GROUPS + N_EXPERTS) & (lane_group == g_idx)
    le = jnp.where(sel, logits, neg)
    v1 = jnp.max(le, axis=1, keepdims=True)
    i1 = jnp.min(jnp.where(le == v1, lane_f, far), axis=1, keepdims=True)
    le2 = jnp.where(lane_f == i1, neg, le)
    v2 = jnp.max(le2, axis=1, keepdims=True)
    i2 = jnp.min(jnp.where(le2 == v2, lane_f, far), axis=1, keepdims=True)
    e2 = jnp.exp(v2 - v1)
    w1 = p_sel * (1.0 / (1.0 + e2))
    w2 = p_sel * (e2 / (1.0 + e2))
    return lane, lane_f, g_idx, i1, i2, w1, w2


def _outproj_kernel(ctx_tiles, yc_c, yr_c, x_c, yc_l, yr_l, x_l, *rest):
    @pl.when(pl.program_id(0) < ctx_tiles)
    def _():
        _outproj_tile(yc_c, yr_c, x_c, *rest)

    @pl.when(pl.program_id(0) >= ctx_tiles)
    def _():
        _outproj_tile(yc_l, yr_l, x_l, *rest)


def _outproj_tile(yc_ref, yr_ref, x_ref, g1_ref, sh_ref, sc_ref, ng_ref, wo_ref, wr_ref, br_ref,
                  x1_ref, xloc_ref, route_ref, cnt_ref):
    m = (jnp.dot(yc_ref[...], wo_ref[0:CONV_W, :], preferred_element_type=F32)
         + jnp.dot(yr_ref[...], wo_ref[CONV_W:, :], preferred_element_type=F32))
    x1 = x_ref[...] + g1_ref[0, 0] * m
    x1_ref[...] = x1
    xn = (_rms(x1) * ng_ref[...]) * (1.0 + sc_ref[0, 0]) + sh_ref[0, 0]
    xb = xn.astype(BF16)
    logits = jnp.dot(xb, wr_ref[...], preferred_element_type=F32) + br_ref[...]
    lane, lane_f, g_idx, i1, i2, w1, w2 = _route(logits)

    picks = jnp.where(lane_f == g_idx, 1.0, 0.0)
    rows = picks.shape[0]
    tri = (lax.broadcasted_iota(jnp.int32, (rows, rows), 0)
           > lax.broadcasted_iota(jnp.int32, (rows, rows), 1))
    before = jnp.dot(jnp.where(tri, 1.0, 0.0).astype(BF16), picks.astype(BF16),
                     preferred_element_type=F32)
    count = jnp.sum(picks, axis=0, keepdims=True)
    cnt_ref[0] = count
    count8 = jnp.broadcast_to(jnp.floor((count + (SUBLANES - 1)) * (1.0 / SUBLANES)) * SUBLANES,
                              (SUBLANES, LANES))
    lane8 = lane[:SUBLANES]
    start = sum(jnp.where(lane8 >= k, pltpu.roll(count8, k, 1), 0.0) for k in range(1, N_GROUPS))
    local = jnp.sum(jnp.where(lane_f == g_idx, before + start[0:1], 0.0), axis=1, keepdims=True)
    route = jnp.where(lane == ROUTE_GROUP, g_idx, jnp.where(lane == ROUTE_LOCAL, local, jnp.where(
        lane == ROUTE_E1, i1 - N_GROUPS, jnp.where(lane == ROUTE_E2, i2 - N_GROUPS, jnp.where(
            lane == ROUTE_W1, w1, jnp.where(lane == ROUTE_W2, w2, 0.0))))))
    route_ref[...] = route

    local_row = jnp.transpose(jnp.broadcast_to(local, (rows, LANES)))[0:1, :]
    place = jnp.where(lax.broadcasted_iota(jnp.int32, (XLOC_ROWS, rows), 0).astype(F32) == local_row,
                      1.0, 0.0).astype(BF16)
    xloc_ref[0, :, :D_MODEL] = jnp.dot(place, xb, preferred_element_type=F32)
    xloc_ref[0, :, D_MODEL:] = sum(
        jnp.dot(place, piece, preferred_element_type=F32) for piece in _split3(route))


def _outproj(ctx, lat, mod4, mod_row_of_tile, norm_g, w_out_bf, w_router_bf, b_router):
    ctx_tiles = ctx[2].shape[0] // OUTPROJ_TILE
    tiles = ctx_tiles + lat[2].shape[0] // OUTPROJ_TILE
    tokens = tiles * OUTPROJ_TILE

    def mod_spec(which):
        return pl.BlockSpec((1, 1, 1, D_MODEL), lambda i: (mod_row_of_tile(i), which, 0, 0))

    ctx_tok = lambda w: pl.BlockSpec((OUTPROJ_TILE, w), lambda i: (jnp.minimum(i, ctx_tiles - 1), 0))
    lat_tok = lambda w: pl.BlockSpec((OUTPROJ_TILE, w), lambda i: (jnp.maximum(i - ctx_tiles, 0), 0))
    tok = lambda w: pl.BlockSpec((OUTPROJ_TILE, w), lambda i: (i, 0))
    full = lambda a: pl.BlockSpec(a.shape, lambda i: (0,) * a.ndim)
    widths = (CONV_W, RET_W, D_MODEL)
    return pl.pallas_call(
        functools.partial(_outproj_kernel, ctx_tiles),
        grid=(tiles,),
        in_specs=[ctx_tok(w) for w in widths] + [lat_tok(w) for w in widths] + [
            mod_spec(2), mod_spec(3), mod_spec(4),
            full(norm_g), full(w_out_bf), full(w_router_bf), full(b_router)],
        out_specs=[tok(D_MODEL),
                   pl.BlockSpec((1, XLOC_ROWS, ROW_W), lambda i: (i, 0, 0)),
                   tok(ROUTER_COLS),
                   pl.BlockSpec((1, 1, ROUTER_COLS), lambda i: (i, 0, 0))],
        out_shape=[jax.ShapeDtypeStruct((tokens, D_MODEL), F32),
                   jax.ShapeDtypeStruct((tiles, XLOC_ROWS, ROW_W), F32),
                   jax.ShapeDtypeStruct((tokens, ROUTER_COLS), F32),
                   jax.ShapeDtypeStruct((tiles, 1, ROUTER_COLS), F32)],
        compiler_params=pltpu.CompilerParams(
            dimension_semantics=("parallel",), vmem_limit_bytes=VMEM_LIMIT),
        name="outproj",
    )(*ctx, *lat, mod4, mod4, mod4, norm_g, w_out_bf, w_router_bf, b_router)


RUN_PIECES = tuple(SUBLANES << b for b in reversed(range((OUTPROJ_TILE // SUBLANES).bit_length())))


def _expert_kernel(tile_group_ref, n_used_ref, first_ref, last_ref, fill_ref,
                   run_len_ref, run_src_ref, run_dst_ref,
                   xloc_hbm, w1_ref, w3_ref, w2_ref, ys_ref, xbuf, xb, gate_tabs, sem):
    j = pl.program_id(0)
    step = pl.program_id(1)
    n_used = n_used_ref[0]

    def tile_fetch(tile):
        slot = tile % 2
        group = tile_group_ref[tile]
        row0 = tile * GROUP_TILE

        def from_token_tile(b, carry):
            run = b * N_GROUPS + group
            lo = jnp.maximum(run_dst_ref[run], row0)
            hi = jnp.minimum(run_dst_ref[run] + run_len_ref[run], row0 + GROUP_TILE)
            n = jnp.maximum(hi - lo, 0)
            src = run_src_ref[run] + lo - run_dst_ref[run]
            dst = lo - row0
            for size in RUN_PIECES:
                done = n & (-2 * size)

                @pl.when((n & size) != 0)
                def _():
                    pltpu.make_async_copy(
                        xloc_hbm.at[b, pl.ds(pl.multiple_of(src + done, SUBLANES), size)],
                        xbuf.at[slot, pl.ds(pl.multiple_of(dst + done, SUBLANES), size)], sem.at[slot]).start()
            return carry

        lax.fori_loop(first_ref[tile], last_ref[tile] + 1, from_token_tile, 0)

    def tile_wait(tile):
        slot = tile % 2
        for size in (GROUP_TILE,) + tuple(GROUP_TILE >> k for k in range(1, (GROUP_TILE // SUBLANES).bit_length())):
            @pl.when((fill_ref[tile] & size) != 0)
            def _():
                pltpu.make_async_copy(xbuf.at[1 - slot, pl.ds(0, size)], xbuf.at[slot, pl.ds(0, size)],
                                      sem.at[slot]).wait()

    @pl.when(j < n_used)
    def _():
        @pl.when(step == 0)
        def _():
            @pl.when(j == 0)
            def _():
                xbuf[...] = jnp.zeros_like(xbuf)
                tile_fetch(j)

            tile_wait(j)
            rows_in = xbuf[j % 2]
            xb[...] = rows_in[:, :D_MODEL].astype(BF16)
            route = rows_in[:, D_MODEL:]
            lane = lax.broadcasted_iota(jnp.int32, route.shape, 1)
            for n, which in enumerate((ROUTE_E1, ROUTE_E2, ROUTE_W1, ROUTE_W2)):
                col = jnp.sum(jnp.where(lane == which, route, 0.0), axis=1, keepdims=True)
                gate_tabs[n] = jnp.broadcast_to(col, route.shape)

            @pl.when(j + 1 < n_used)
            def _():
                tile_fetch(j + 1)

        def evaluate(rows):
            x = xb[:rows]
            total = None
            for s in range(EXPERTS_PER_STEP):
                expert = (tile_group_ref[j] * EXPERTS_PER_GROUP + step * EXPERTS_PER_STEP + s).astype(F32)
                gate = (jnp.where(gate_tabs[0, :rows] == expert, gate_tabs[2, :rows], 0.0)
                        + jnp.where(gate_tabs[1, :rows] == expert, gate_tabs[3, :rows], 0.0))
                hid = _silu(jnp.dot(x, w1_ref[0, s].astype(BF16), preferred_element_type=F32)) * jnp.dot(
                    x, w3_ref[0, s].astype(BF16), preferred_element_type=F32)
                y = jnp.dot(hid.astype(BF16), w2_ref[0, s].astype(BF16), preferred_element_type=F32)
                gated = jnp.concatenate(
                    [gate * y[:, c * LANES:(c + 1) * LANES] for c in range(D_MODEL // LANES)], axis=1)
                total = gated if total is None else total + gated

            @pl.when(step == 0)
            def _():
                ys_ref[:rows] = total
                if rows < GROUP_TILE:
                    ys_ref[rows:] = jnp.zeros((GROUP_TILE - rows, D_MODEL), F32)

            @pl.when(step > 0)
            def _():
                ys_ref[:rows] += total

        half = GROUP_TILE // 2

        @pl.when(fill_ref[j] > half)
        def _():
            evaluate(GROUP_TILE)

        @pl.when(fill_ref[j] <= half)
        def _():
            evaluate(half)

    @pl.when((j >= n_used) & (step == 0))
    def _():
        ys_ref[...] = jnp.zeros_like(ys_ref)


def _experts(xloc, tile_tables, runs, slots, w1, w3, w2):
    steps = EXPERTS_PER_GROUP // EXPERTS_PER_STEP
    paired = lambda w: w.reshape((N_EXPERTS // EXPERTS_PER_STEP, EXPERTS_PER_STEP) + w.shape[1:])
    w_spec = lambda shape: pl.BlockSpec((1, EXPERTS_PER_STEP) + shape, lambda j, s, tg, nu, *_: (
        tg[jnp.minimum(j, nu[0] - 1)] * steps + jnp.where(j < nu[0], s, steps - 1), 0, 0, 0))
    grid_spec = pltpu.PrefetchScalarGridSpec(
        num_scalar_prefetch=8,
        grid=(slots // GROUP_TILE, steps),
        in_specs=[
            pl.BlockSpec(memory_space=pl.ANY),
            w_spec((D_MODEL, D_EXPERT)), w_spec((D_MODEL, D_EXPERT)), w_spec((D_EXPERT, D_MODEL)),
        ],
        out_specs=pl.BlockSpec((GROUP_TILE, D_MODEL), lambda j, s, *_: (j, 0)),
        scratch_shapes=[pltpu.VMEM((2, GROUP_TILE, ROW_W), F32), pltpu.VMEM((GROUP_TILE, D_MODEL), BF16),
                        pltpu.VMEM((4, GROUP_TILE, LANES), F32), pltpu.SemaphoreType.DMA((2,))],
    )
    return pl.pallas_call(
        _expert_kernel,
        grid_spec=grid_spec,
        out_shape=jax.ShapeDtypeStruct((slots, D_MODEL), F32),
        compiler_params=pltpu.CompilerParams(
            dimension_semantics=("arbitrary", "arbitrary"), vmem_limit_bytes=EXPERT_VMEM_LIMIT),
        name="experts",
    )(*tile_tables, *runs, xloc, paired(w1), paired(w3), paired(w2))


def _combine_kernel(n_tiles, tile_base, run_len_ref, run_src_ref, run_dst_ref,
                    x1_ref, route_ref, g2_ref, fg_ref, ys_hbm, o_ref, buf, sem):
    i = pl.program_id(0)
    slot = i % 2

    def run_copies(local_tile, act):
        s = local_tile % 2
        tile = local_tile + tile_base
        for g in range(N_GROUPS):
            n = run_len_ref[tile * N_GROUPS + g]
            src = run_src_ref[tile * N_GROUPS + g]
            dst = run_dst_ref[tile * N_GROUPS + g]
            for size in RUN_PIECES:
                done = n & (-2 * size)

                @pl.when((n & size) != 0)
                def _():
                    act(pltpu.make_async_copy(
                        ys_hbm.at[pl.ds(pl.multiple_of(dst + done, SUBLANES), size)],
                        buf.at[s, pl.ds(pl.multiple_of(src + done, SUBLANES), size)], sem.at[s]))

    @pl.when(i == 0)
    def _():
        buf[...] = jnp.zeros_like(buf)
        run_copies(i, lambda cp: cp.start())

    @pl.when(i + 1 < n_tiles)
    def _():
        run_copies(i + 1, lambda cp: cp.start())

    run_copies(i, lambda cp: cp.wait())
    route = route_ref[...]
    lane = lax.broadcasted_iota(jnp.int32, route.shape, 1)
    local = jnp.sum(jnp.where(lane == ROUTE_LOCAL, route, 0.0), axis=1, keepdims=True)
    pick = jnp.where(lax.broadcasted_iota(jnp.int32, (route.shape[0], LOCAL_ROWS), 1).astype(F32) == local,
                     1.0, 0.0).astype(BF16)
    moe = sum(jnp.dot(pick, piece, preferred_element_type=F32) for piece in _split3(buf[slot])[:2])
    y = x1_ref[...] + g2_ref[0, 0] * moe
    o_ref[...] = _rms(y) * fg_ref[...]


def _combine(ys, runs, x1, route, tokens, mod4, mod_row_of_tile, final_g, tile_base):
    tiles = tokens // OUTPROJ_TILE
    tok = lambda w: pl.BlockSpec((OUTPROJ_TILE, w), lambda i, *_: (i + tile_base, 0))
    grid_spec = pltpu.PrefetchScalarGridSpec(
        num_scalar_prefetch=3,
        grid=(tiles,),
        in_specs=[
            tok(D_MODEL), tok(ROUTER_COLS),
            pl.BlockSpec((1, 1, 1, D_MODEL), lambda i, *_: (mod_row_of_tile(i + tile_base), 5, 0, 0)),
            pl.BlockSpec((1, D_MODEL), lambda i, *_: (0, 0)),
            pl.BlockSpec(memory_space=pl.ANY),
        ],
        out_specs=pl.BlockSpec((OUTPROJ_TILE, D_MODEL), lambda i, *_: (i, 0)),
        scratch_shapes=[pltpu.VMEM((2, LOCAL_ROWS, D_MODEL), F32), pltpu.SemaphoreType.DMA((2,))],
    )
    return pl.pallas_call(
        functools.partial(_combine_kernel, tiles, tile_base),
        grid_spec=grid_spec,
        out_shape=jax.ShapeDtypeStruct((tokens, D_MODEL), F32),
        compiler_params=pltpu.CompilerParams(
            dimension_semantics=("arbitrary",), vmem_limit_bytes=VMEM_LIMIT),
        name="combine",
    )(*runs, x1, route, mod4, final_g, ys)


def _routing_tables(counts):
    counts = counts.astype(jnp.int32)
    tiles = counts.shape[0]
    run_len = ((counts + SUBLANES - 1) // SUBLANES) * SUBLANES
    run_src = jnp.cumsum(run_len, axis=1) - run_len
    group_rows = jnp.sum(run_len, axis=0)
    padded = ((group_rows + GROUP_TILE - 1) // GROUP_TILE) * GROUP_TILE
    ends = jnp.cumsum(padded)
    offs = ends - padded
    run_dst = offs[None, :] + jnp.cumsum(run_len, axis=0) - run_len

    n_used = ends[-1] // GROUP_TILE
    max_rows = tiles * (OUTPROJ_TILE + N_GROUPS * (SUBLANES - 1))
    max_tiles = max_rows // GROUP_TILE + N_GROUPS
    tile_ids = jnp.minimum(jnp.arange(max_tiles, dtype=jnp.int32), n_used - 1)
    tile_group = jnp.sum(tile_ids[:, None] * GROUP_TILE >= ends[None, :], axis=1).astype(jnp.int32)
    of_group = (tile_group[:, None] == jnp.arange(N_GROUPS, dtype=jnp.int32))[:, None, :]
    start = jnp.sum(jnp.where(of_group, run_dst[None], 0), axis=-1)
    stop = start + jnp.sum(jnp.where(of_group, run_len[None], 0), axis=-1)
    row0 = (tile_ids * GROUP_TILE)[:, None]
    first = jnp.sum(stop <= row0, axis=1).astype(jnp.int32)
    last = jnp.sum(start < row0 + GROUP_TILE, axis=1).astype(jnp.int32) - 1
    group_end = jnp.sum(jnp.where(of_group[:, 0, :], (offs + group_rows)[None, :], 0), axis=-1)
    fill = jnp.clip(group_end - row0[:, 0], 0, GROUP_TILE).astype(jnp.int32)
    flat = lambda a: a.reshape(-1)
    return ((tile_group, n_used.reshape(1), first, last, fill), (flat(run_len), flat(run_src), flat(run_dst)),
            max_tiles * GROUP_TILE)


def _mixer(x, mod4, mod_row, is_grid, s_f0, s_b0, p):
    norm_mix_g, w_in_bf, conv_w, decay_rows = p
    y_conv, *qkvg = _inproj(x, mod4, mod_row, norm_mix_g, w_in_bf, conv_w, is_grid)
    per_seq = lambda a: a.reshape(x.shape[0], x.shape[1], a.shape[-1])
    ret = _retention(*map(per_seq, qkvg), decay_rows, s_f0, s_b0, emit_state=not is_grid)
    return y_conv, ret[0].reshape(-1, RET_W), ret[1:]


def kernel(x_prompt, x_sample, state_ret_fwd, state_ret_bwd, c, c_ctx, norm_mix_g, norm_ffn_g, w_ada, b_ada, w_in, conv_w, ret_decay_fwd, ret_decay_bwd, w_out, w_router_group, b_router_group, w_router_expert, b_router_expert, w_gate_e, w_up_e, w_down_e, final_norm_g):
    assert norm_mix_g.shape[0] == 1, "single-layer backbone"
    n_lat = c.shape[0]
    ctx_row = n_lat
    mod_rows = 8
    cvec = jnp.concatenate([c, c_ctx[None, :], jnp.zeros((mod_rows - n_lat - 1, D_MODEL), F32)], axis=0)
    mod = _modulation(cvec, w_ada[0], b_ada[0][None, :])
    mod4 = mod.reshape(mod_rows, 6, 1, D_MODEL)

    pad = ROUTER_COLS - N_GROUPS - N_EXPERTS
    w_router = jnp.concatenate(
        [w_router_group[0], w_router_expert[0], jnp.zeros((D_MODEL, pad), F32)], axis=1).astype(BF16)
    b_router = jnp.concatenate([b_router_group[0], b_router_expert[0], jnp.zeros((pad,), F32)])[None, :]
    decay_rows = jnp.broadcast_to(
        jnp.concatenate([ret_decay_fwd[0], ret_decay_bwd[0]])[:, None], (2 * RET_HEADS, LANES))
    p_mix = (norm_mix_g, w_in[0].astype(BF16), conv_w[0], decay_rows)
    w_out_bf = w_out[0].astype(BF16)
    final_g = final_norm_g[None, :]

    ctx_tokens = x_prompt.shape[0] * x_prompt.shape[1]
    lat_tokens = x_sample.shape[0] * x_sample.shape[1]
    ctx_tiles = ctx_tokens // OUTPROJ_TILE
    lat_tiles_per_seq = x_sample.shape[1] // OUTPROJ_TILE
    tile_mod = lambda i: jnp.where(i < ctx_tiles, ctx_row, (i - ctx_tiles) // lat_tiles_per_seq)
    flat = lambda a: a.reshape(-1, a.shape[-1])

    yc_c, yr_c, (s_f, s_b) = _mixer(x_prompt, mod4, lambda b: ctx_row, False, None, None, p_mix)
    yc_l, yr_l, _ = _mixer(x_sample, mod4, lambda b: b, True, state_ret_fwd, state_ret_bwd, p_mix)

    x1, xloc, route, cnt = _outproj((yc_c, yr_c, flat(x_prompt)), (yc_l, yr_l, flat(x_sample)), mod4, tile_mod,
                                    norm_ffn_g, w_out_bf, w_router, b_router)
    tile_tables, runs, slots = _routing_tables(cnt[:, 0, :N_GROUPS])
    ys = _experts(xloc, tile_tables, runs, slots, w_gate_e[0], w_up_e[0], w_down_e[0])
    y_prompt = _combine(ys, runs, x1, route, ctx_tokens, mod4, tile_mod, final_g, 0)
    y_sample = _combine(ys, runs, x1, route, lat_tokens, mod4, tile_mod, final_g, ctx_tiles)
    return (y_prompt.reshape(x_prompt.shape), y_sample.reshape(x_sample.shape),
            s_f.astype(x_prompt.dtype), s_b.astype(x_prompt.dtype))
```

```python
import functools

import jax
import jax.numpy as jnp
import numpy as np
from jax import lax
from jax.experimental import pallas as pl
from jax.experimental.pallas import tpu as pltpu

F32 = jnp.float32
BF16 = jnp.bfloat16

D_MODEL = 1024
GRID_W = 64
CONV_W = 512
RET_HEADS = 4
RET_DK = 128
RET_DV = 128
RET_W = RET_HEADS * RET_DV
QK_W = RET_HEADS * RET_DK
CHUNK = 128
N_GROUPS = 4
EXPERTS_PER_GROUP = 8
N_EXPERTS = N_GROUPS * EXPERTS_PER_GROUP
D_EXPERT = 256
ROPE_BASE = 10000.0
EPS = 1e-6

LANES = 128
TOKEN_TILE = 1024
OUTPROJ_TILE = 512
GROUP_TILE = 1024
EXPERTS_PER_STEP = 4
RET_UNROLL = 8
SUBLANES = 8
XLOC_ROWS = OUTPROJ_TILE + N_GROUPS * SUBLANES
LOCAL_ROWS = OUTPROJ_TILE + LANES
ROW_W = D_MODEL + LANES
ROUTE_GROUP, ROUTE_LOCAL, ROUTE_E1, ROUTE_E2, ROUTE_W1, ROUTE_W2 = range(6)
MOD_COLS = 1536
ROUTER_COLS = LANES
VMEM_LIMIT = 48 * 1024 * 1024
EXPERT_VMEM_LIMIT = 56 * 1024 * 1024


def _silu(x):
    return x * jax.nn.sigmoid(x)


def _rms(x):
    return x * lax.rsqrt(jnp.mean(x * x, axis=-1, keepdims=True) + EPS)


def _bdot(a, b):
    return jnp.dot(a.astype(BF16), b.astype(BF16), preferred_element_type=F32)


def _split3(x):
    hi = x.astype(BF16)
    rest = x - hi.astype(F32)
    mid = rest.astype(BF16)
    return hi, mid, (rest - mid.astype(F32)).astype(BF16)


def _mod_kernel(c_ref, w_ref, b_ref, o_ref):
    o_ref[...] = _bdot(_silu(c_ref[...]), w_ref[...]) + b_ref[...]


def _modulation(cvec, w_ada, b_ada):
    rows = cvec.shape[0]
    n = w_ada.shape[1]
    return pl.pallas_call(
        _mod_kernel,
        grid=(n // MOD_COLS,),
        in_specs=[
            pl.BlockSpec((rows, D_MODEL), lambda j: (0, 0)),
            pl.BlockSpec((D_MODEL, MOD_COLS), lambda j: (0, j)),
            pl.BlockSpec((1, MOD_COLS), lambda j: (0, j)),
        ],
        out_specs=pl.BlockSpec((rows, MOD_COLS), lambda j: (0, j)),
        out_shape=jax.ShapeDtypeStruct((rows, n), F32),
        compiler_params=pltpu.CompilerParams(vmem_limit_bytes=VMEM_LIMIT),
        name="modulation",
    )(cvec, w_ada, b_ada)


def _inproj_kernel(seg, is_grid, x_ref, sh_ref, sc_ref, ng_ref, w_ref, cw_ref, *rest):
    if is_grid:
        cos_ref, sa_ref, sb_ref, yc_ref, q_ref, k_ref, v_ref, g_ref = rest
    else:
        yc_ref, q_ref, k_ref, v_ref, g_ref = rest
    x = x_ref[...]
    xn = (_rms(x) * ng_ref[...]) * (1.0 + sc_ref[0, 0]) + sh_ref[0, 0]
    xb = xn.astype(BF16)

    def proj(c0, n):
        return jnp.dot(xb, w_ref[:, c0:c0 + n], preferred_element_type=F32)

    gate_b = proj(0, CONV_W)
    u = proj(CONV_W, CONV_W) * proj(2 * CONV_W, CONV_W)
    rows = u.shape[0]
    pos = lax.broadcasted_iota(jnp.int32, u.shape, 0) & (seg - 1)
    u_prev = jnp.where(pos != 0, pltpu.roll(u, 1, 0), 0.0)
    u_next = jnp.where(pos != seg - 1, pltpu.roll(u, rows - 1, 0), 0.0)
    conv = cw_ref[0:1, :] * u_prev + cw_ref[1:2, :] * u + cw_ref[2:3, :] * u_next
    yc_ref[...] = (gate_b * conv).astype(yc_ref.dtype)

    q0 = 3 * CONV_W
    q = proj(q0, QK_W)
    k = proj(q0 + QK_W, QK_W)
    if is_grid:
        cos, sa, sb = cos_ref[...], sa_ref[...], sb_ref[...]

        def rope(t):
            out = []
            for h in range(RET_HEADS):
                th = t[:, h * RET_DK:(h + 1) * RET_DK]
                out.append(th * cos + pltpu.roll(th, RET_DK - 1, 1) * sa + pltpu.roll(th, 1, 1) * sb)
            return jnp.concatenate(out, axis=1)

        q, k = rope(q), rope(k)
    q_ref[...] = q
    k_ref[...] = k
    v_ref[...] = proj(q0 + 2 * QK_W, RET_W)
    g_ref[...] = proj(q0 + 2 * QK_W + RET_W, RET_W)


def _rope_tables(length):
    pos = np.arange(length)
    row = (pos // GRID_W).astype(np.float64)
    col = (pos % GRID_W).astype(np.float64)
    n_pairs = RET_DK // 4
    freqs = ROPE_BASE ** (-(np.arange(n_pairs, dtype=np.float64) * 2.0 / (RET_DK // 2)))
    ang = np.concatenate([row[:, None] * freqs, col[:, None] * freqs], axis=-1)
    cos = np.repeat(np.cos(ang), 2, axis=-1)
    sin = np.repeat(np.sin(ang), 2, axis=-1)
    even = (np.arange(RET_DK) % 2) == 0
    return tuple(jnp.asarray(t, F32) for t in (cos, np.where(even, -sin, 0.0), np.where(even, 0.0, sin)))


def _inproj(x, mod4, mod_row, norm_g, w_in_bf, conv_w, is_grid):
    bsz, length, _ = x.shape
    seg = GRID_W if is_grid else length
    assert TOKEN_TILE % seg == 0 and (length % TOKEN_TILE == 0 or TOKEN_TILE % length == 0)
    tokens = bsz * length
    tiles_per_seq = max(length // TOKEN_TILE, 1)
    seqs_per_tile = max(TOKEN_TILE // length, 1)
    batch_of = lambda i: (i // tiles_per_seq) * seqs_per_tile

    def mod_spec(which):
        return pl.BlockSpec((1, 1, 1, D_MODEL), lambda i: (mod_row(batch_of(i)), which, 0, 0))

    def tok_spec(width):
        return pl.BlockSpec((TOKEN_TILE, width), lambda i: (i, 0))

    in_specs = [
        tok_spec(D_MODEL), mod_spec(0), mod_spec(1),
        pl.BlockSpec((1, D_MODEL), lambda i: (0, 0)),
        pl.BlockSpec(w_in_bf.shape, lambda i: (0, 0)),
        pl.BlockSpec(conv_w.shape, lambda i: (0, 0)),
    ]
    args = [x.reshape(tokens, D_MODEL), mod4, mod4, norm_g, w_in_bf, conv_w]
    if is_grid:
        assert length % TOKEN_TILE == 0
        in_specs += [pl.BlockSpec((TOKEN_TILE, RET_DK), lambda i: (i % tiles_per_seq, 0))] * 3
        args += list(_rope_tables(length))
    shp = lambda w, dt: jax.ShapeDtypeStruct((tokens, w), dt)
    return pl.pallas_call(
        functools.partial(_inproj_kernel, seg, is_grid),
        grid=(tokens // TOKEN_TILE,),
        in_specs=in_specs,
        out_specs=[tok_spec(CONV_W), tok_spec(QK_W), tok_spec(QK_W), tok_spec(RET_W), tok_spec(RET_W)],
        out_shape=[shp(CONV_W, BF16), shp(QK_W, F32), shp(QK_W, F32), shp(RET_W, F32), shp(RET_W, F32)],
        compiler_params=pltpu.CompilerParams(
            dimension_semantics=("parallel",), vmem_limit_bytes=VMEM_LIMIT),
        name="inproj_grid" if is_grid else "inproj_seq",
    )(*args)


def _ret_kernel(n_chunks, heads, has_init, emit_state, a_ref, q_ref, k_ref, v_ref, g_ref, *rest):
    rest = list(rest)
    if has_init:
        sf0_ref, sb0_ref = rest[:2]
        rest = rest[2:]
    y_ref = rest.pop(0)
    if emit_state:
        sf_out, sb_out = rest[:2]
        rest = rest[2:]
    st_f, st_b, dec = rest
    c = CHUNK
    sq = (c, c)
    head0 = pl.program_id(0) * heads

    def log_decays(hh):
        lg_f = jnp.log1p(-jnp.exp(a_ref[pl.ds(head0 + hh, 1), :]))
        lg_b = jnp.log1p(-jnp.exp(a_ref[pl.ds(head0 + hh + RET_HEADS, 1), :]))
        return lg_f, lg_b

    @pl.when(pl.program_id(1) == 0)
    def _():
        row = lax.broadcasted_iota(jnp.int32, sq, 0).astype(F32)
        col = lax.broadcasted_iota(jnp.int32, sq, 1).astype(F32)
        scale = RET_DK ** -0.5
        for hh in range(heads):
            lg_f, lg_b = log_decays(hh)
            dec[hh, 0] = scale * (
                jnp.where(row >= col, jnp.exp(jnp.where(row >= col, row - col, 0.0) * lg_f), 0.0)
                + jnp.where(col >= row, jnp.exp(jnp.where(col >= row, col - row, 0.0) * lg_b), 0.0))
            dec[hh, 1] = jnp.exp((row + 1.0) * lg_f)
            dec[hh, 2] = jnp.exp((c - row) * lg_b)
            dec[hh, 3] = scale * jnp.exp((c - 1.0 - col) * lg_f)
            dec[hh, 4] = scale * jnp.exp(col * lg_b)

    def rows(n):
        return pl.ds(pl.multiple_of(n * c, c), c) if not isinstance(n, int) else pl.ds(n * c, c)

    def cols(hh):
        return slice(hh * RET_DK, (hh + 1) * RET_DK)

    def kv_step(hh, n):
        kt = jnp.transpose(k_ref[0, rows(n), cols(hh)])
        lhs = jnp.concatenate([kt * dec[hh, 3], kt * dec[hh, 4]], axis=0)
        kv = _bdot(lhs, v_ref[0, rows(n), cols(hh)])
        st_f[hh, n] = kv[:RET_DK]
        st_b[hh, n] = kv[RET_DK:]

    def scan(hh, st, decay, order, s):
        def step(i, s):
            n = order(i)
            kv = st[hh, n]
            st[hh, n] = s
            return s * decay + kv
        if n_chunks <= RET_UNROLL:
            for i in range(n_chunks):
                s = step(i, s)
            return s
        return lax.fori_loop(0, n_chunks, step, s, unroll=RET_UNROLL)

    def out_step(hh, n):
        q = q_ref[0, rows(n), cols(hh)]
        scores = lax.dot_general(q.astype(BF16), k_ref[0, rows(n), cols(hh)].astype(BF16),
                                 (((1,), (1,)), ((), ())), preferred_element_type=F32)
        o = _bdot(scores * dec[hh, 0], v_ref[0, rows(n), cols(hh)])
        q_dec = jnp.concatenate([q * dec[hh, 1], q * dec[hh, 2]], axis=1)
        o = o + _bdot(q_dec, jnp.concatenate([st_f[hh, n], st_b[hh, n]], axis=0))
        y = _silu(g_ref[0, rows(n), cols(hh)]) * _rms(o)
        y_ref[0, rows(n), cols(hh)] = y.astype(y_ref.dtype)

    def over_chunks(step):
        if n_chunks * heads <= RET_UNROLL:
            for hh in range(heads):
                for n in range(n_chunks):
                    step(hh, n)
        else:
            for hh in range(heads):
                lax.fori_loop(0, n_chunks, lambda n, carry: (step(hh, n), carry)[1], 0, unroll=RET_UNROLL)

    over_chunks(kv_step)
    finals = []
    for hh in range(heads):
        lg_f, lg_b = log_decays(hh)
        s_f = sf0_ref[0, 0, hh] if has_init else jnp.zeros(sq, F32)
        s_b = sb0_ref[0, 0, hh] if has_init else jnp.zeros(sq, F32)
        s_f = scan(hh, st_f, jnp.exp(c * lg_f), lambda i: i, s_f)
        s_b = scan(hh, st_b, jnp.exp(c * lg_b), lambda i: n_chunks - 1 - i, s_b)
        finals.append((s_f, s_b))
    over_chunks(out_step)
    if emit_state:
        for hh, (s_f, s_b) in enumerate(finals):
            sf_out[0, 0, hh] = s_f
            sb_out[0, 0, hh] = s_b


def _retention(q, k, v, g, decay_rows, s_f0, s_b0, emit_state):
    bsz, length, _ = q.shape
    n_chunks = length // CHUNK
    has_init = s_f0 is not None
    heads = RET_HEADS if n_chunks * RET_HEADS <= RET_UNROLL else 1
    head_spec = pl.BlockSpec((1, length, heads * RET_DK), lambda h, b: (b, 0, h))
    st_spec = pl.BlockSpec((1, 1, heads, RET_DK, RET_DV), lambda h, b: (b, 0, h, 0, 0))
    in_specs = [pl.BlockSpec(decay_rows.shape, lambda h, b: (0, 0))] + [head_spec] * 4
    args = [decay_rows, q, k, v, g]
    if has_init:
        in_specs += [st_spec, st_spec]
        args += [s_f0, s_b0]
    out_specs = [head_spec]
    out_shape = [jax.ShapeDtypeStruct((bsz, length, RET_W), BF16)]
    if emit_state:
        st_shape = jax.ShapeDtypeStruct((bsz, 1, RET_HEADS, RET_DK, RET_DV), F32)
        out_specs += [st_spec, st_spec]
        out_shape += [st_shape, st_shape]
    return pl.pallas_call(
        functools.partial(_ret_kernel, n_chunks, heads, has_init, emit_state),
        grid=(RET_HEADS // heads, bsz),
        in_specs=in_specs,
        out_specs=out_specs,
        out_shape=out_shape,
        scratch_shapes=[
            pltpu.VMEM((heads, n_chunks, RET_DK, RET_DV), F32),
            pltpu.VMEM((heads, n_chunks, RET_DK, RET_DV), F32),
            pltpu.VMEM((heads, 5, CHUNK, CHUNK), F32),
        ],
        compiler_params=pltpu.CompilerParams(
            dimension_semantics=("arbitrary", "arbitrary"), vmem_limit_bytes=VMEM_LIMIT),
        name="retention_init" if has_init else "retention_zero",
    )(*args)


def _route(logits):
    lane = lax.broadcasted_iota(jnp.int32, logits.shape, 1)
    lane_f = lane.astype(F32)
    neg = -jnp.inf
    far = float(LANES)
    is_g = lane < N_GROUPS
    lg = jnp.where(is_g, logits, neg)
    g_max = jnp.max(lg, axis=1, keepdims=True)
    g_idx = jnp.min(jnp.where(lg == g_max, lane_f, far), axis=1, keepdims=True)
    p_sel = 1.0 / jnp.sum(jnp.where(is_g, jnp.exp(lg - g_max), 0.0), axis=1, keepdims=True)
    lane_group = ((lane - N_GROUPS) >> (EXPERTS_PER_GROUP.bit_length() - 1)).astype(F32)
    sel = (lane >= N_GROUPS) & (lane < N_GROUPS + N_EXPERTS) & (lane_group == g_idx)
    le = jnp.where(sel, logits, neg)
    v1 = jnp.max(le, axis=1, keepdims=True)
    i1 = jnp.min(jnp.where(le == v1, lane_f, far), axis=1, keepdims=True)
    le2 = jnp.where(lane_f == i1, neg, le)
    v2 = jnp.max(le2, axis=1, keepdims=True)
    i2 = jnp.min(jnp.where(le2 == v2, lane_f, far), axis=1, keepdims=True)
    e2 = jnp.exp(v2 - v1)
    w1 = p_sel * (1.0 / (1.0 + e2))
    w2 = p_sel * (e2 / (1.0 + e2))
    return lane, lane_f, g_idx, i1, i2, w1, w2


def _outproj_kernel(ctx_tiles, yc_c, yr_c, x_c, yc_l, yr_l, x_l, *rest):
    @pl.when(pl.program_id(0) < ctx_tiles)
    def _():
        _outproj_tile(yc_c, yr_c, x_c, *rest)

    @pl.when(pl.program_id(0) >= ctx_tiles)
    def _():
        _outproj_tile(yc_l, yr_l, x_l, *rest)


def _outproj_tile(yc_ref, yr_ref, x_ref, g1_ref, sh_ref, sc_ref, ng_ref, wo_ref, wr_ref, br_ref,
                  x1_ref, xloc_ref, route_ref, cnt_ref):
    m = (jnp.dot(yc_ref[...], wo_ref[0:CONV_W, :], preferred_element_type=F32)
         + jnp.dot(yr_ref[...], wo_ref[CONV_W:, :], preferred_element_type=F32))
    x1 = x_ref[...] + g1_ref[0, 0] * m
    x1_ref[...] = x1
    xn = (_rms(x1) * ng_ref[...]) * (1.0 + sc_ref[0, 0]) + sh_ref[0, 0]
    xb = xn.astype(BF16)
    logits = jnp.dot(xb, wr_ref[...], preferred_element_type=F32) + br_ref[...]
    lane, lane_f, g_idx, i1, i2, w1, w2 = _route(logits)

    picks = jnp.where(lane_f == g_idx, 1.0, 0.0)
    rows = picks.shape[0]
    tri = (lax.broadcasted_iota(jnp.int32, (rows, rows), 0)
           > lax.broadcasted_iota(jnp.int32, (rows, rows), 1))
    before = jnp.dot(jnp.where(tri, 1.0, 0.0).astype(BF16), picks.astype(BF16),
                     preferred_element_type=F32)
    count = jnp.sum(picks, axis=0, keepdims=True)
    cnt_ref[0] = count
    count8 = jnp.broadcast_to(jnp.floor((count + (SUBLANES - 1)) * (1.0 / SUBLANES)) * SUBLANES,
                              (SUBLANES, LANES))
    lane8 = lane[:SUBLANES]
    start = sum(jnp.where(lane8 >= k, pltpu.roll(count8, k, 1), 0.0) for k in range(1, N_GROUPS))
    local = jnp.sum(jnp.where(lane_f == g_idx, before + start[0:1], 0.0), axis=1, keepdims=True)
    route = jnp.where(lane == ROUTE_GROUP, g_idx, jnp.where(lane == ROUTE_LOCAL, local, jnp.where(
        lane == ROUTE_E1, i1 - N_GROUPS, jnp.where(lane == ROUTE_E2, i2 - N_GROUPS, jnp.where(
            lane == ROUTE_W1, w1, jnp.where(lane == ROUTE_W2, w2, 0.0))))))
    route_ref[...] = route

    local_row = jnp.transpose(jnp.broadcast_to(local, (rows, LANES)))[0:1, :]
    place = jnp.where(lax.broadcasted_iota(jnp.int32, (XLOC_ROWS, rows), 0).astype(F32) == local_row,
                      1.0, 0.0).astype(BF16)
    xloc_ref[0, :, :D_MODEL] = jnp.dot(place, xb, preferred_element_type=F32)
    xloc_ref[0, :, D_MODEL:] = sum(
        jnp.dot(place, piece, preferred_element_type=F32) for piece in _split3(route))


def _outproj(ctx, lat, mod4, mod_row_of_tile, norm_g, w_out_bf, w_router_bf, b_router):
    ctx_tiles = ctx[2].shape[0] // OUTPROJ_TILE
    tiles = ctx_tiles + lat[2].shape[0] // OUTPROJ_TILE
    tokens = tiles * OUTPROJ_TILE

    def mod_spec(which):
        return pl.BlockSpec((1, 1, 1, D_MODEL), lambda i: (mod_row_of_tile(i), which, 0, 0))

    ctx_tok = lambda w: pl.BlockSpec((OUTPROJ_TILE, w), lambda i: (jnp.minimum(i, ctx_tiles - 1), 0))
    lat_tok = lambda w: pl.BlockSpec((OUTPROJ_TILE, w), lambda i: (jnp.maximum(i - ctx_tiles, 0), 0))
    tok = lambda w: pl.BlockSpec((OUTPROJ_TILE, w), lambda i: (i, 0))
    full = lambda a: pl.BlockSpec(a.shape, lambda i: (0,) * a.ndim)
    widths = (CONV_W, RET_W, D_MODEL)
    return pl.pallas_call(
        functools.partial(_outproj_kernel, ctx_tiles),
        grid=(tiles,),
        in_specs=[ctx_tok(w) for w in widths] + [lat_tok(w) for w in widths] + [
            mod_spec(2), mod_spec(3), mod_spec(4),
            full(norm_g), full(w_out_bf), full(w_router_bf), full(b_router)],
        out_specs=[tok(D_MODEL),
                   pl.BlockSpec((1, XLOC_ROWS, ROW_W), lambda i: (i, 0, 0)),
                   tok(ROUTER_COLS),
                   pl.BlockSpec((1, 1, ROUTER_COLS), lambda i: (i, 0, 0))],
        out_shape=[jax.ShapeDtypeStruct((tokens, D_MODEL), F32),
                   jax.ShapeDtypeStruct((tiles, XLOC_ROWS, ROW_W), F32),
                   jax.ShapeDtypeStruct((tokens, ROUTER_COLS), F32),
                   jax.ShapeDtypeStruct((tiles, 1, ROUTER_COLS), F32)],
        compiler_params=pltpu.CompilerParams(
            dimension_semantics=("parallel",), vmem_limit_bytes=VMEM_LIMIT),
        name="outproj",
    )(*ctx, *lat, mod4, mod4, mod4, norm_g, w_out_bf, w_router_bf, b_router)


RUN_PIECES = tuple(SUBLANES << b for b in reversed(range((OUTPROJ_TILE // SUBLANES).bit_length())))


def _expert_kernel(tile_group_ref, n_used_ref, first_ref, last_ref, fill_ref,
                   run_len_ref, run_src_ref, run_dst_ref,
                   xloc_hbm, w1_ref, w3_ref, w2_ref, ys_ref, xbuf, xb, gate_tabs, sem):
    j = pl.program_id(0)
    step = pl.program_id(1)
    n_used = n_used_ref[0]

    def tile_fetch(tile):
        slot = tile % 2
        group = tile_group_ref[tile]
        row0 = tile * GROUP_TILE

        def from_token_tile(b, carry):
            run = b * N_GROUPS + group
            lo = jnp.maximum(run_dst_ref[run], row0)
            hi = jnp.minimum(run_dst_ref[run] + run_len_ref[run], row0 + GROUP_TILE)
            n = jnp.maximum(hi - lo, 0)
            src = run_src_ref[run] + lo - run_dst_ref[run]
            dst = lo - row0
            for size in RUN_PIECES:
                done = n & (-2 * size)

                @pl.when((n & size) != 0)
                def _():
                    pltpu.make_async_copy(
                        xloc_hbm.at[b, pl.ds(pl.multiple_of(src + done, SUBLANES), size)],
                        xbuf.at[slot, pl.ds(pl.multiple_of(dst + done, SUBLANES), size)], sem.at[slot]).start()
            return carry

        lax.fori_loop(first_ref[tile], last_ref[tile] + 1, from_token_tile, 0)

    def tile_wait(tile):
        slot = tile % 2
        for size in (GROUP_TILE,) + tuple(GROUP_TILE >> k for k in range(1, (GROUP_TILE // SUBLANES).bit_length())):
            @pl.when((fill_ref[tile] & size) != 0)
            def _():
                pltpu.make_async_copy(xbuf.at[1 - slot, pl.ds(0, size)], xbuf.at[slot, pl.ds(0, size)],
                                      sem.at[slot]).wait()

    @pl.when(j < n_used)
    def _():
        @pl.when(step == 0)
        def _():
            @pl.when(j == 0)
            def _():
                xbuf[...] = jnp.zeros_like(xbuf)
                tile_fetch(j)

            tile_wait(j)
            rows_in = xbuf[j % 2]
            xb[...] = rows_in[:, :D_MODEL].astype(BF16)
            route = rows_in[:, D_MODEL:]
            lane = lax.broadcasted_iota(jnp.int32, route.shape, 1)
            for n, which in enumerate((ROUTE_E1, ROUTE_E2, ROUTE_W1, ROUTE_W2)):
                col = jnp.sum(jnp.where(lane == which, route, 0.0), axis=1, keepdims=True)
                gate_tabs[n] = jnp.broadcast_to(col, route.shape)

            @pl.when(j + 1 < n_used)
            def _():
                tile_fetch(j + 1)

        def evaluate(rows):
            x = xb[:rows]
            total = None
            for s in range(EXPERTS_PER_STEP):
                expert = (tile_group_ref[j] * EXPERTS_PER_GROUP + step * EXPERTS_PER_STEP + s).astype(F32)
                gate = (jnp.where(gate_tabs[0, :rows] == expert, gate_tabs[2, :rows], 0.0)
                        + jnp.where(gate_tabs[1, :rows] == expert, gate_tabs[3, :rows], 0.0))
                hid = _silu(jnp.dot(x, w1_ref[0, s].astype(BF16), preferred_element_type=F32)) * jnp.dot(
                    x, w3_ref[0, s].astype(BF16), preferred_element_type=F32)
                y = jnp.dot(hid.astype(BF16), w2_ref[0, s].astype(BF16), preferred_element_type=F32)
                gated = jnp.concatenate(
                    [gate * y[:, c * LANES:(c + 1) * LANES] for c in range(D_MODEL // LANES)], axis=1)
                total = gated if total is None else total + gated

            @pl.when(step == 0)
            def _():
                ys_ref[:rows] = total
                if rows < GROUP_TILE:
                    ys_ref[rows:] = jnp.zeros((GROUP_TILE - rows, D_MODEL), F32)

            @pl.when(step > 0)
            def _():
                ys_ref[:rows] += total

        half = GROUP_TILE // 2

        @pl.when(fill_ref[j] > half)
        def _():
            evaluate(GROUP_TILE)

        @pl.when(fill_ref[j] <= half)
        def _():
            evaluate(half)

    @pl.when((j >= n_used) & (step == 0))
    def _():
        ys_ref[...] = jnp.zeros_like(ys_ref)


def _experts(xloc, tile_tables, runs, slots, w1, w3, w2):
    steps = EXPERTS_PER_GROUP // EXPERTS_PER_STEP
    per_step = lambda w: w.reshape((N_EXPERTS // EXPERTS_PER_STEP, EXPERTS_PER_STEP) + w.shape[1:])
    w_spec = lambda shape: pl.BlockSpec((1, EXPERTS_PER_STEP) + shape, lambda j, s, tg, nu, *_: (
        tg[jnp.minimum(j, nu[0] - 1)] * steps + jnp.where(j < nu[0], s, steps - 1), 0, 0, 0))
    grid_spec = pltpu.PrefetchScalarGridSpec(
        num_scalar_prefetch=8,
        grid=(slots // GROUP_TILE, steps),
        in_specs=[
            pl.BlockSpec(memory_space=pl.ANY),
            w_spec((D_MODEL, D_EXPERT)), w_spec((D_MODEL, D_EXPERT)), w_spec((D_EXPERT, D_MODEL)),
        ],
        out_specs=pl.BlockSpec((GROUP_TILE, D_MODEL), lambda j, s, *_: (j, 0)),
        scratch_shapes=[pltpu.VMEM((2, GROUP_TILE, ROW_W), F32), pltpu.VMEM((GROUP_TILE, D_MODEL), BF16),
                        pltpu.VMEM((4, GROUP_TILE, LANES), F32), pltpu.SemaphoreType.DMA((2,))],
    )
    return pl.pallas_call(
        _expert_kernel,
        grid_spec=grid_spec,
        out_shape=jax.ShapeDtypeStruct((slots, D_MODEL), F32),
        compiler_params=pltpu.CompilerParams(
            dimension_semantics=("arbitrary", "arbitrary"), vmem_limit_bytes=EXPERT_VMEM_LIMIT),
        name="experts",
    )(*tile_tables, *runs, xloc, per_step(w1), per_step(w3), per_step(w2))


def _combine_kernel(n_tiles, tile_base, run_len_ref, run_src_ref, run_dst_ref,
                    x1_ref, route_ref, g2_ref, fg_ref, ys_hbm, o_ref, buf, sem):
    i = pl.program_id(0)
    slot = i % 2

    def run_copies(local_tile, act):
        s = local_tile % 2
        tile = local_tile + tile_base
        for g in range(N_GROUPS):
            n = run_len_ref[tile * N_GROUPS + g]
            src = run_src_ref[tile * N_GROUPS + g]
            dst = run_dst_ref[tile * N_GROUPS + g]
            for size in RUN_PIECES:
                done = n & (-2 * size)

                @pl.when((n & size) != 0)
                def _():
                    act(pltpu.make_async_copy(
                        ys_hbm.at[pl.ds(pl.multiple_of(dst + done, SUBLANES), size)],
                        buf.at[s, pl.ds(pl.multiple_of(src + done, SUBLANES), size)], sem.at[s]))

    @pl.when(i == 0)
    def _():
        buf[...] = jnp.zeros_like(buf)
        run_copies(i, lambda cp: cp.start())

    @pl.when(i + 1 < n_tiles)
    def _():
        run_copies(i + 1, lambda cp: cp.start())

    run_copies(i, lambda cp: cp.wait())
    route = route_ref[...]
    lane = lax.broadcasted_iota(jnp.int32, route.shape, 1)
    local = jnp.sum(jnp.where(lane == ROUTE_LOCAL, route, 0.0), axis=1, keepdims=True)
    pick = jnp.where(lax.broadcasted_iota(jnp.int32, (route.shape[0], LOCAL_ROWS), 1).astype(F32) == local,
                     1.0, 0.0).astype(BF16)
    moe = sum(jnp.dot(pick, piece, preferred_element_type=F32) for piece in _split3(buf[slot])[:2])
    y = x1_ref[...] + g2_ref[0, 0] * moe
    o_ref[...] = _rms(y) * fg_ref[...]


def _combine(ys, runs, x1, route, tokens, mod4, mod_row_of_tile, final_g, tile_base):
    tiles = tokens // OUTPROJ_TILE
    tok = lambda w: pl.BlockSpec((OUTPROJ_TILE, w), lambda i, *_: (i + tile_base, 0))
    grid_spec = pltpu.PrefetchScalarGridSpec(
        num_scalar_prefetch=3,
        grid=(tiles,),
        in_specs=[
            tok(D_MODEL), tok(ROUTER_COLS),
            pl.BlockSpec((1, 1, 1, D_MODEL), lambda i, *_: (mod_row_of_tile(i + tile_base), 5, 0, 0)),
            pl.BlockSpec((1, D_MODEL), lambda i, *_: (0, 0)),
            pl.BlockSpec(memory_space=pl.ANY),
        ],
        out_specs=pl.BlockSpec((OUTPROJ_TILE, D_MODEL), lambda i, *_: (i, 0)),
        scratch_shapes=[pltpu.VMEM((2, LOCAL_ROWS, D_MODEL), F32), pltpu.SemaphoreType.DMA((2,))],
    )
    return pl.pallas_call(
        functools.partial(_combine_kernel, tiles, tile_base),
        grid_spec=grid_spec,
        out_shape=jax.ShapeDtypeStruct((tokens, D_MODEL), F32),
        compiler_params=pltpu.CompilerParams(
            dimension_semantics=("arbitrary",), vmem_limit_bytes=VMEM_LIMIT),
        name="combine",
    )(*runs, x1, route, mod4, final_g, ys)


def _routing_tables(counts):
    counts = counts.astype(jnp.int32)
    tiles = counts.shape[0]
    run_len = ((counts + SUBLANES - 1) // SUBLANES) * SUBLANES
    run_src = jnp.cumsum(run_len, axis=1) - run_len
    group_rows = jnp.sum(run_len, axis=0)
    padded = ((group_rows + GROUP_TILE - 1) // GROUP_TILE) * GROUP_TILE
    ends = jnp.cumsum(padded)
    offs = ends - padded
    run_dst = offs[None, :] + jnp.cumsum(run_len, axis=0) - run_len

    n_used = ends[-1] // GROUP_TILE
    max_rows = tiles * (OUTPROJ_TILE + N_GROUPS * (SUBLANES - 1))
    max_tiles = max_rows // GROUP_TILE + N_GROUPS
    tile_ids = jnp.minimum(jnp.arange(max_tiles, dtype=jnp.int32), n_used - 1)
    tile_group = jnp.sum(tile_ids[:, None] * GROUP_TILE >= ends[None, :], axis=1).astype(jnp.int32)
    of_group = (tile_group[:, None] == jnp.arange(N_GROUPS, dtype=jnp.int32))[:, None, :]
    start = jnp.sum(jnp.where(of_group, run_dst[None], 0), axis=-1)
    stop = start + jnp.sum(jnp.where(of_group, run_len[None], 0), axis=-1)
    row0 = (tile_ids * GROUP_TILE)[:, None]
    first = jnp.sum(stop <= row0, axis=1).astype(jnp.int32)
    last = jnp.sum(start < row0 + GROUP_TILE, axis=1).astype(jnp.int32) - 1
    group_end = jnp.sum(jnp.where(of_group[:, 0, :], (offs + group_rows)[None, :], 0), axis=-1)
    fill = jnp.clip(group_end - row0[:, 0], 0, GROUP_TILE).astype(jnp.int32)
    flat = lambda a: a.reshape(-1)
    return ((tile_group, n_used.reshape(1), first, last, fill), (flat(run_len), flat(run_src), flat(run_dst)),
            max_tiles * GROUP_TILE)


def _mixer(x, mod4, mod_row, is_grid, s_f0, s_b0, p):
    norm_mix_g, w_in_bf, conv_w, decay_rows = p
    y_conv, *qkvg = _inproj(x, mod4, mod_row, norm_mix_g, w_in_bf, conv_w, is_grid)
    per_seq = lambda a: a.reshape(x.shape[0], x.shape[1], a.shape[-1])
    ret = _retention(*map(per_seq, qkvg), decay_rows, s_f0, s_b0, emit_state=not is_grid)
    return y_conv, ret[0].reshape(-1, RET_W), ret[1:]


def kernel(x_prompt, x_sample, state_ret_fwd, state_ret_bwd, c, c_ctx, norm_mix_g, norm_ffn_g, w_ada, b_ada, w_in, conv_w, ret_decay_fwd, ret_decay_bwd, w_out, w_router_group, b_router_group, w_router_expert, b_router_expert, w_gate_e, w_up_e, w_down_e, final_norm_g):
    assert norm_mix_g.shape[0] == 1, "single-layer backbone"
    n_lat = c.shape[0]
    ctx_row = n_lat
    mod_rows = 8
    cvec = jnp.concatenate([c, c_ctx[None, :], jnp.zeros((mod_rows - n_lat - 1, D_MODEL), F32)], axis=0)
    mod = _modulation(cvec, w_ada[0], b_ada[0][None, :])
    mod4 = mod.reshape(mod_rows, 6, 1, D_MODEL)

    pad = ROUTER_COLS - N_GROUPS - N_EXPERTS
    w_router = jnp.concatenate(
        [w_router_group[0], w_router_expert[0], jnp.zeros((D_MODEL, pad), F32)], axis=1).astype(BF16)
    b_router = jnp.concatenate([b_router_group[0], b_router_expert[0], jnp.zeros((pad,), F32)])[None, :]
    decay_rows = jnp.broadcast_to(
        jnp.concatenate([ret_decay_fwd[0], ret_decay_bwd[0]])[:, None], (2 * RET_HEADS, LANES))
    p_mix = (norm_mix_g, w_in[0].astype(BF16), conv_w[0], decay_rows)
    w_out_bf = w_out[0].astype(BF16)
    final_g = final_norm_g[None, :]

    ctx_tokens = x_prompt.shape[0] * x_prompt.shape[1]
    lat_tokens = x_sample.shape[0] * x_sample.shape[1]
    ctx_tiles = ctx_tokens // OUTPROJ_TILE
    lat_tiles_per_seq = x_sample.shape[1] // OUTPROJ_TILE
    tile_mod = lambda i: jnp.where(i < ctx_tiles, ctx_row, (i - ctx_tiles) // lat_tiles_per_seq)
    flat = lambda a: a.reshape(-1, a.shape[-1])

    yc_c, yr_c, (s_f, s_b) = _mixer(x_prompt, mod4, lambda b: ctx_row, False, None, None, p_mix)
    yc_l, yr_l, _ = _mixer(x_sample, mod4, lambda b: b, True, state_ret_fwd, state_ret_bwd, p_mix)

    x1, xloc, route, cnt = _outproj((yc_c, yr_c, flat(x_prompt)), (yc_l, yr_l, flat(x_sample)), mod4, tile_mod,
                                    norm_ffn_g, w_out_bf, w_router, b_router)
    tile_tables, runs, slots = _routing_tables(cnt[:, 0, :N_GROUPS])
    ys = _experts(xloc, tile_tables, runs, slots, w_gate_e[0], w_up_e[0], w_down_e[0])
    y_prompt = _combine(ys, runs, x1, route, ctx_tokens, mod4, tile_mod, final_g, 0)
    y_sample = _combine(ys, runs, x1, route, lat_tokens, mod4, tile_mod, final_g, ctx_tiles)
    return (y_prompt.reshape(x_prompt.shape), y_sample.reshape(x_sample.shape),
            s_f.astype(x_prompt.dtype), s_b.astype(x_prompt.dtype))
```

```python
import functools

import jax
import jax.numpy as jnp
import numpy as np
from jax import lax
from jax.experimental import pallas as pl
from jax.experimental.pallas import tpu as pltpu

F32 = jnp.float32
BF16 = jnp.bfloat16

D_MODEL = 1024
GRID_W = 64
CONV_W = 512
RET_HEADS = 4
RET_DK = 128
RET_DV = 128
RET_W = RET_HEADS * RET_DV
QK_W = RET_HEADS * RET_DK
CHUNK = 128
N_GROUPS = 4
EXPERTS_PER_GROUP = 8
N_EXPERTS = N_GROUPS * EXPERTS_PER_GROUP
D_EXPERT = 256
ROPE_BASE = 10000.0
EPS = 1e-6

LANES = 128
TOKEN_TILE = 1024
OUTPROJ_TILE = 512
GROUP_TILE = 1024
EXPERTS_PER_STEP = 4
RET_UNROLL = 8
COMBINE_RING = 3
SUBLANES = 8
XLOC_ROWS = OUTPROJ_TILE + N_GROUPS * SUBLANES
LOCAL_ROWS = OUTPROJ_TILE + LANES
ROW_W = D_MODEL + LANES
ROUTE_GROUP, ROUTE_LOCAL, ROUTE_E1, ROUTE_E2, ROUTE_W1, ROUTE_W2 = range(6)
MOD_COLS = 1536
ROUTER_COLS = LANES
VMEM_LIMIT = 48 * 1024 * 1024
EXPERT_VMEM_LIMIT = 56 * 1024 * 1024


def _silu(x):
    return x * jax.nn.sigmoid(x)


def _rms(x):
    return x * lax.rsqrt(jnp.mean(x * x, axis=-1, keepdims=True) + EPS)


def _bdot(a, b):
    return jnp.dot(a.astype(BF16), b.astype(BF16), preferred_element_type=F32)


def _split3(x):
    hi = x.astype(BF16)
    rest = x - hi.astype(F32)
    mid = rest.astype(BF16)
    return hi, mid, (rest - mid.astype(F32)).astype(BF16)


def _mod_kernel(c_ref, w_ref, b_ref, o_ref):
    o_ref[...] = _bdot(_silu(c_ref[...]), w_ref[...]) + b_ref[...]


def _modulation(cvec, w_ada, b_ada):
    rows = cvec.shape[0]
    n = w_ada.shape[1]
    return pl.pallas_call(
        _mod_kernel,
        grid=(n // MOD_COLS,),
        in_specs=[
            pl.BlockSpec((rows, D_MODEL), lambda j: (0, 0)),
            pl.BlockSpec((D_MODEL, MOD_COLS), lambda j: (0, j)),
            pl.BlockSpec((1, MOD_COLS), lambda j: (0, j)),
        ],
        out_specs=pl.BlockSpec((rows, MOD_COLS), lambda j: (0, j)),
        out_shape=jax.ShapeDtypeStruct((rows, n), F32),
        compiler_params=pltpu.CompilerParams(vmem_limit_bytes=VMEM_LIMIT),
        name="modulation",
    )(cvec, w_ada, b_ada)


def _inproj_kernel(seg, is_grid, x_ref, sh_ref, sc_ref, ng_ref, w_ref, cw_ref, *rest):
    if is_grid:
        cos_ref, sa_ref, sb_ref, yc_ref, q_ref, k_ref, v_ref, g_ref = rest
    else:
        yc_ref, q_ref, k_ref, v_ref, g_ref = rest
    x = x_ref[...]
    xn = (_rms(x) * ng_ref[...]) * (1.0 + sc_ref[0, 0]) + sh_ref[0, 0]
    xb = xn.astype(BF16)

    def proj(c0, n):
        return jnp.dot(xb, w_ref[:, c0:c0 + n], preferred_element_type=F32)

    gate_b = proj(0, CONV_W)
    u = proj(CONV_W, CONV_W) * proj(2 * CONV_W, CONV_W)
    rows = u.shape[0]
    pos = lax.broadcasted_iota(jnp.int32, u.shape, 0) & (seg - 1)
    u_prev = jnp.where(pos != 0, pltpu.roll(u, 1, 0), 0.0)
    u_next = jnp.where(pos != seg - 1, pltpu.roll(u, rows - 1, 0), 0.0)
    conv = cw_ref[0:1, :] * u_prev + cw_ref[1:2, :] * u + cw_ref[2:3, :] * u_next
    yc_ref[...] = (gate_b * conv).astype(yc_ref.dtype)

    q0 = 3 * CONV_W
    q = proj(q0, QK_W)
    k = proj(q0 + QK_W, QK_W)
    if is_grid:
        cos, sa, sb = cos_ref[...], sa_ref[...], sb_ref[...]

        def rope(t):
            out = []
            for h in range(RET_HEADS):
                th = t[:, h * RET_DK:(h + 1) * RET_DK]
                out.append(th * cos + pltpu.roll(th, RET_DK - 1, 1) * sa + pltpu.roll(th, 1, 1) * sb)
            return jnp.concatenate(out, axis=1)

        q, k = rope(q), rope(k)
    q_ref[...] = q
    k_ref[...] = k
    v_ref[...] = proj(q0 + 2 * QK_W, RET_W)
    g_ref[...] = proj(q0 + 2 * QK_W + RET_W, RET_W)


def _rope_tables(length):
    pos = np.arange(length)
    row = (pos // GRID_W).astype(np.float64)
    col = (pos % GRID_W).astype(np.float64)
    n_pairs = RET_DK // 4
    freqs = ROPE_BASE ** (-(np.arange(n_pairs, dtype=np.float64) * 2.0 / (RET_DK // 2)))
    ang = np.concatenate([row[:, None] * freqs, col[:, None] * freqs], axis=-1)
    cos = np.repeat(np.cos(ang), 2, axis=-1)
    sin = np.repeat(np.sin(ang), 2, axis=-1)
    even = (np.arange(RET_DK) % 2) == 0
    return tuple(jnp.asarray(t, F32) for t in (cos, np.where(even, -sin, 0.0), np.where(even, 0.0, sin)))


def _inproj(x, mod4, mod_row, norm_g, w_in_bf, conv_w, is_grid):
    bsz, length, _ = x.shape
    seg = GRID_W if is_grid else length
    assert TOKEN_TILE % seg == 0 and (length % TOKEN_TILE == 0 or TOKEN_TILE % length == 0)
    tokens = bsz * length
    tiles_per_seq = max(length // TOKEN_TILE, 1)
    seqs_per_tile = max(TOKEN_TILE // length, 1)
    batch_of = lambda i: (i // tiles_per_seq) * seqs_per_tile

    def mod_spec(which):
        return pl.BlockSpec((1, 1, 1, D_MODEL), lambda i: (mod_row(batch_of(i)), which, 0, 0))

    def tok_spec(width):
        return pl.BlockSpec((TOKEN_TILE, width), lambda i: (i, 0))

    in_specs = [
        tok_spec(D_MODEL), mod_spec(0), mod_spec(1),
        pl.BlockSpec((1, D_MODEL), lambda i: (0, 0)),
        pl.BlockSpec(w_in_bf.shape, lambda i: (0, 0)),
        pl.BlockSpec(conv_w.shape, lambda i: (0, 0)),
    ]
    args = [x.reshape(tokens, D_MODEL), mod4, mod4, norm_g, w_in_bf, conv_w]
    if is_grid:
        assert length % TOKEN_TILE == 0
        in_specs += [pl.BlockSpec((TOKEN_TILE, RET_DK), lambda i: (i % tiles_per_seq, 0))] * 3
        args += list(_rope_tables(length))
    shp = lambda w, dt: jax.ShapeDtypeStruct((tokens, w), dt)
    return pl.pallas_call(
        functools.partial(_inproj_kernel, seg, is_grid),
        grid=(tokens // TOKEN_TILE,),
        in_specs=in_specs,
        out_specs=[tok_spec(CONV_W), tok_spec(QK_W), tok_spec(QK_W), tok_spec(RET_W), tok_spec(RET_W)],
        out_shape=[shp(CONV_W, BF16), shp(QK_W, F32), shp(QK_W, F32), shp(RET_W, F32), shp(RET_W, F32)],
        compiler_params=pltpu.CompilerParams(
            dimension_semantics=("parallel",), vmem_limit_bytes=VMEM_LIMIT),
        name="inproj_grid" if is_grid else "inproj_seq",
    )(*args)


def _ret_kernel(n_chunks, heads, has_init, emit_state, a_ref, q_ref, k_ref, v_ref, g_ref, *rest):
    rest = list(rest)
    if has_init:
        sf0_ref, sb0_ref = rest[:2]
        rest = rest[2:]
    y_ref = rest.pop(0)
    if emit_state:
        sf_out, sb_out = rest[:2]
        rest = rest[2:]
    st_f, st_b, dec = rest
    c = CHUNK
    sq = (c, c)
    head0 = pl.program_id(0) * heads

    def log_decays(hh):
        lg_f = jnp.log1p(-jnp.exp(a_ref[pl.ds(head0 + hh, 1), :]))
        lg_b = jnp.log1p(-jnp.exp(a_ref[pl.ds(head0 + hh + RET_HEADS, 1), :]))
        return lg_f, lg_b

    @pl.when(pl.program_id(1) == 0)
    def _():
        row = lax.broadcasted_iota(jnp.int32, sq, 0).astype(F32)
        col = lax.broadcasted_iota(jnp.int32, sq, 1).astype(F32)
        scale = RET_DK ** -0.5
        for hh in range(heads):
            lg_f, lg_b = log_decays(hh)
            dec[hh, 0] = scale * (
                jnp.where(row >= col, jnp.exp(jnp.where(row >= col, row - col, 0.0) * lg_f), 0.0)
                + jnp.where(col >= row, jnp.exp(jnp.where(col >= row, col - row, 0.0) * lg_b), 0.0))
            dec[hh, 1] = jnp.exp((row + 1.0) * lg_f)
            dec[hh, 2] = jnp.exp((c - row) * lg_b)
            dec[hh, 3] = scale * jnp.exp((c - 1.0 - col) * lg_f)
            dec[hh, 4] = scale * jnp.exp(col * lg_b)

    def rows(n):
        return pl.ds(pl.multiple_of(n * c, c), c) if not isinstance(n, int) else pl.ds(n * c, c)

    def cols(hh):
        return slice(hh * RET_DK, (hh + 1) * RET_DK)

    def kv_step(hh, n):
        kt = jnp.transpose(k_ref[0, rows(n), cols(hh)])
        lhs = jnp.concatenate([kt * dec[hh, 3], kt * dec[hh, 4]], axis=0)
        kv = _bdot(lhs, v_ref[0, rows(n), cols(hh)])
        st_f[hh, n] = kv[:RET_DK]
        st_b[hh, n] = kv[RET_DK:]

    def scan(hh, st, decay, order, s):
        def step(i, s):
            n = order(i)
            kv = st[hh, n]
            st[hh, n] = s
            return s * decay + kv
        if n_chunks <= RET_UNROLL:
            for i in range(n_chunks):
                s = step(i, s)
            return s
        return lax.fori_loop(0, n_chunks, step, s, unroll=RET_UNROLL)

    def out_step(hh, n):
        q = q_ref[0, rows(n), cols(hh)]
        scores = lax.dot_general(q.astype(BF16), k_ref[0, rows(n), cols(hh)].astype(BF16),
                                 (((1,), (1,)), ((), ())), preferred_element_type=F32)
        o = _bdot(scores * dec[hh, 0], v_ref[0, rows(n), cols(hh)])
        q_dec = jnp.concatenate([q * dec[hh, 1], q * dec[hh, 2]], axis=1)
        o = o + _bdot(q_dec, jnp.concatenate([st_f[hh, n], st_b[hh, n]], axis=0))
        y = _silu(g_ref[0, rows(n), cols(hh)]) * _rms(o)
        y_ref[0, rows(n), cols(hh)] = y.astype(y_ref.dtype)

    def over_chunks(step):
        if n_chunks * heads <= RET_UNROLL:
            for hh in range(heads):
                for n in range(n_chunks):
                    step(hh, n)
        else:
            for hh in range(heads):
                lax.fori_loop(0, n_chunks, lambda n, carry: (step(hh, n), carry)[1], 0, unroll=RET_UNROLL)

    over_chunks(kv_step)
    finals = []
    for hh in range(heads):
        lg_f, lg_b = log_decays(hh)
        s_f = sf0_ref[0, 0, hh] if has_init else jnp.zeros(sq, F32)
        s_b = sb0_ref[0, 0, hh] if has_init else jnp.zeros(sq, F32)
        s_f = scan(hh, st_f, jnp.exp(c * lg_f), lambda i: i, s_f)
        s_b = scan(hh, st_b, jnp.exp(c * lg_b), lambda i: n_chunks - 1 - i, s_b)
        finals.append((s_f, s_b))
    over_chunks(out_step)
    if emit_state:
        for hh, (s_f, s_b) in enumerate(finals):
            sf_out[0, 0, hh] = s_f
            sb_out[0, 0, hh] = s_b


def _retention(q, k, v, g, decay_rows, s_f0, s_b0, emit_state):
    bsz, length, _ = q.shape
    n_chunks = length // CHUNK
    has_init = s_f0 is not None
    heads = RET_HEADS if n_chunks * RET_HEADS <= RET_UNROLL else 1
    head_spec = pl.BlockSpec((1, length, heads * RET_DK), lambda h, b: (b, 0, h))
    st_spec = pl.BlockSpec((1, 1, heads, RET_DK, RET_DV), lambda h, b: (b, 0, h, 0, 0))
    in_specs = [pl.BlockSpec(decay_rows.shape, lambda h, b: (0, 0))] + [head_spec] * 4
    args = [decay_rows, q, k, v, g]
    if has_init:
        in_specs += [st_spec, st_spec]
        args += [s_f0, s_b0]
    out_specs = [head_spec]
    out_shape = [jax.ShapeDtypeStruct((bsz, length, RET_W), BF16)]
    if emit_state:
        st_shape = jax.ShapeDtypeStruct((bsz, 1, RET_HEADS, RET_DK, RET_DV), F32)
        out_specs += [st_spec, st_spec]
        out_shape += [st_shape, st_shape]
    return pl.pallas_call(
        functools.partial(_ret_kernel, n_chunks, heads, has_init, emit_state),
        grid=(RET_HEADS // heads, bsz),
        in_specs=in_specs,
        out_specs=out_specs,
        out_shape=out_shape,
        scratch_shapes=[
            pltpu.VMEM((heads, n_chunks, RET_DK, RET_DV), F32),
            pltpu.VMEM((heads, n_chunks, RET_DK, RET_DV), F32),
            pltpu.VMEM((heads, 5, CHUNK, CHUNK), F32),
        ],
        compiler_params=pltpu.CompilerParams(
            dimension_semantics=("arbitrary", "arbitrary"), vmem_limit_bytes=VMEM_LIMIT),
        name="retention_init" if has_init else "retention_zero",
    )(*args)


def _route(logits):
    lane = lax.broadcasted_iota(jnp.int32, logits.shape, 1)
    lane_f = lane.astype(F32)
    neg = -jnp.inf
    far = float(LANES)
    is_g = lane < N_GROUPS
    lg = jnp.where(is_g, logits, neg)
    g_max = jnp.max(lg, axis=1, keepdims=True)
    g_idx = jnp.min(jnp.where(lg == g_max, lane_f, far), axis=1, keepdims=True)
    p_sel = 1.0 / jnp.sum(jnp.where(is_g, jnp.exp(lg - g_max), 0.0), axis=1, keepdims=True)
    lane_group = ((lane - N_GROUPS) >> (EXPERTS_PER_GROUP.bit_length() - 1)).astype(F32)
    sel = (lane >= N_GROUPS) & (lane < N_GROUPS + N_EXPERTS) & (lane_group == g_idx)
    le = jnp.where(sel, logits, neg)
    v1 = jnp.max(le, axis=1, keepdims=True)
    i1 = jnp.min(jnp.where(le == v1, lane_f, far), axis=1, keepdims=True)
    le2 = jnp.where(lane_f == i1, neg, le)
    v2 = jnp.max(le2, axis=1, keepdims=True)
    i2 = jnp.min(jnp.where(le2 == v2, lane_f, far), axis=1, keepdims=True)
    e2 = jnp.exp(v2 - v1)
    w1 = p_sel * (1.0 / (1.0 + e2))
    w2 = p_sel * (e2 / (1.0 + e2))
    return lane, lane_f, g_idx, i1, i2, w1, w2


def _outproj_kernel(ctx_tiles, yc_c, yr_c, x_c, yc_l, yr_l, x_l, *rest):
    @pl.when(pl.program_id(0) < ctx_tiles)
    def _():
        _outproj_tile(yc_c, yr_c, x_c, *rest)

    @pl.when(pl.program_id(0) >= ctx_tiles)
    def _():
        _outproj_tile(yc_l, yr_l, x_l, *rest)


def _outproj_tile(yc_ref, yr_ref, x_ref, g1_ref, sh_ref, sc_ref, ng_ref, wo_ref, wr_ref, br_ref,
                  x1_ref, xloc_ref, route_ref, cnt_ref):
    m = (jnp.dot(yc_ref[...], wo_ref[0:CONV_W, :], preferred_element_type=F32)
         + jnp.dot(yr_ref[...], wo_ref[CONV_W:, :], preferred_element_type=F32))
    x1 = x_ref[...] + g1_ref[0, 0] * m
    x1_ref[...] = x1
    xn = (_rms(x1) * ng_ref[...]) * (1.0 + sc_ref[0, 0]) + sh_ref[0, 0]
    xb = xn.astype(BF16)
    logits = jnp.dot(xb, wr_ref[...], preferred_element_type=F32) + br_ref[...]
    lane, lane_f, g_idx, i1, i2, w1, w2 = _route(logits)

    picks = jnp.where(lane_f == g_idx, 1.0, 0.0)
    rows = picks.shape[0]
    tri = (lax.broadcasted_iota(jnp.int32, (rows, rows), 0)
           > lax.broadcasted_iota(jnp.int32, (rows, rows), 1))
    before = jnp.dot(jnp.where(tri, 1.0, 0.0).astype(BF16), picks.astype(BF16),
                     preferred_element_type=F32)
    count = jnp.sum(picks, axis=0, keepdims=True)
    cnt_ref[0] = count
    count8 = jnp.broadcast_to(jnp.floor((count + (SUBLANES - 1)) * (1.0 / SUBLANES)) * SUBLANES,
                              (SUBLANES, LANES))
    lane8 = lane[:SUBLANES]
    start = sum(jnp.where(lane8 >= k, pltpu.roll(count8, k, 1), 0.0) for k in range(1, N_GROUPS))
    local = jnp.sum(jnp.where(lane_f == g_idx, before + start[0:1], 0.0), axis=1, keepdims=True)
    route = jnp.where(lane == ROUTE_GROUP, g_idx, jnp.where(lane == ROUTE_LOCAL, local, jnp.where(
        lane == ROUTE_E1, i1 - N_GROUPS, jnp.where(lane == ROUTE_E2, i2 - N_GROUPS, jnp.where(
            lane == ROUTE_W1, w1, jnp.where(lane == ROUTE_W2, w2, 0.0))))))
    route_ref[...] = route

    local_row = jnp.transpose(jnp.broadcast_to(local, (rows, LANES)))[0:1, :]
    place = jnp.where(lax.broadcasted_iota(jnp.int32, (XLOC_ROWS, rows), 0).astype(F32) == local_row,
                      1.0, 0.0).astype(BF16)
    xloc_ref[0, :, :D_MODEL] = jnp.dot(place, xb, preferred_element_type=F32)
    xloc_ref[0, :, D_MODEL:] = sum(
        jnp.dot(place, piece, preferred_element_type=F32) for piece in _split3(route))


def _outproj(ctx, lat, mod4, mod_row_of_tile, norm_g, w_out_bf, w_router_bf, b_router):
    ctx_tiles = ctx[2].shape[0] // OUTPROJ_TILE
    tiles = ctx_tiles + lat[2].shape[0] // OUTPROJ_TILE
    tokens = tiles * OUTPROJ_TILE

    def mod_spec(which):
        return pl.BlockSpec((1, 1, 1, D_MODEL), lambda i: (mod_row_of_tile(i), which, 0, 0))

    ctx_tok = lambda w: pl.BlockSpec((OUTPROJ_TILE, w), lambda i: (jnp.minimum(i, ctx_tiles - 1), 0))
    lat_tok = lambda w: pl.BlockSpec((OUTPROJ_TILE, w), lambda i: (jnp.maximum(i - ctx_tiles, 0), 0))
    tok = lambda w: pl.BlockSpec((OUTPROJ_TILE, w), lambda i: (i, 0))
    full = lambda a: pl.BlockSpec(a.shape, lambda i: (0,) * a.ndim)
    widths = (CONV_W, RET_W, D_MODEL)
    return pl.pallas_call(
        functools.partial(_outproj_kernel, ctx_tiles),
        grid=(tiles,),
        in_specs=[ctx_tok(w) for w in widths] + [lat_tok(w) for w in widths] + [
            mod_spec(2), mod_spec(3), mod_spec(4),
            full(norm_g), full(w_out_bf), full(w_router_bf), full(b_router)],
        out_specs=[tok(D_MODEL),
                   pl.BlockSpec((1, XLOC_ROWS, ROW_W), lambda i: (i, 0, 0)),
                   tok(ROUTER_COLS),
                   pl.BlockSpec((1, 1, ROUTER_COLS), lambda i: (i, 0, 0))],
        out_shape=[jax.ShapeDtypeStruct((tokens, D_MODEL), F32),
                   jax.ShapeDtypeStruct((tiles, XLOC_ROWS, ROW_W), F32),
                   jax.ShapeDtypeStruct((tokens, ROUTER_COLS), F32),
                   jax.ShapeDtypeStruct((tiles, 1, ROUTER_COLS), F32)],
        compiler_params=pltpu.CompilerParams(
            dimension_semantics=("parallel",), vmem_limit_bytes=VMEM_LIMIT),
        name="outproj",
    )(*ctx, *lat, mod4, mod4, mod4, norm_g, w_out_bf, w_router_bf, b_router)


RUN_PIECES = tuple(SUBLANES << b for b in reversed(range((OUTPROJ_TILE // SUBLANES).bit_length())))


def _expert_kernel(tile_group_ref, n_used_ref, first_ref, last_ref, fill_ref,
                   run_len_ref, run_src_ref, run_dst_ref,
                   xloc_hbm, w1_ref, w3_ref, w2_ref, ys_ref, xbuf, xb, gate_tabs, sem):
    j = pl.program_id(0)
    step = pl.program_id(1)
    n_used = n_used_ref[0]

    def tile_fetch(tile):
        slot = tile % 2
        group = tile_group_ref[tile]
        row0 = tile * GROUP_TILE

        def from_token_tile(b, carry):
            run = b * N_GROUPS + group
            lo = jnp.maximum(run_dst_ref[run], row0)
            hi = jnp.minimum(run_dst_ref[run] + run_len_ref[run], row0 + GROUP_TILE)
            n = jnp.maximum(hi - lo, 0)
            src = run_src_ref[run] + lo - run_dst_ref[run]
            dst = lo - row0
            for size in RUN_PIECES:
                done = n & (-2 * size)

                @pl.when((n & size) != 0)
                def _():
                    pltpu.make_async_copy(
                        xloc_hbm.at[b, pl.ds(pl.multiple_of(src + done, SUBLANES), size)],
                        xbuf.at[slot, pl.ds(pl.multiple_of(dst + done, SUBLANES), size)], sem.at[slot]).start()
            return carry

        lax.fori_loop(first_ref[tile], last_ref[tile] + 1, from_token_tile, 0)

    def tile_wait(tile):
        slot = tile % 2
        for size in (GROUP_TILE,) + tuple(GROUP_TILE >> k for k in range(1, (GROUP_TILE // SUBLANES).bit_length())):
            @pl.when((fill_ref[tile] & size) != 0)
            def _():
                pltpu.make_async_copy(xbuf.at[1 - slot, pl.ds(0, size)], xbuf.at[slot, pl.ds(0, size)],
                                      sem.at[slot]).wait()

    @pl.when(j < n_used)
    def _():
        @pl.when(step == 0)
        def _():
            @pl.when(j == 0)
            def _():
                xbuf[...] = jnp.zeros_like(xbuf)
                tile_fetch(j)

            tile_wait(j)
            rows_in = xbuf[j % 2]
            xb[...] = rows_in[:, :D_MODEL].astype(BF16)
            route = rows_in[:, D_MODEL:]
            lane = lax.broadcasted_iota(jnp.int32, route.shape, 1)
            for n, which in enumerate((ROUTE_E1, ROUTE_E2, ROUTE_W1, ROUTE_W2)):
                col = jnp.sum(jnp.where(lane == which, route, 0.0), axis=1, keepdims=True)
                gate_tabs[n] = jnp.broadcast_to(col, route.shape)

            @pl.when(j + 1 < n_used)
            def _():
                tile_fetch(j + 1)

        def evaluate(rows):
            x = xb[:rows]
            total = None
            for s in range(EXPERTS_PER_STEP):
                expert = (tile_group_ref[j] * EXPERTS_PER_GROUP + step * EXPERTS_PER_STEP + s).astype(F32)
                gate = (jnp.where(gate_tabs[0, :rows] == expert, gate_tabs[2, :rows], 0.0)
                        + jnp.where(gate_tabs[1, :rows] == expert, gate_tabs[3, :rows], 0.0))
                hid = _silu(jnp.dot(x, w1_ref[0, s].astype(BF16), preferred_element_type=F32)) * jnp.dot(
                    x, w3_ref[0, s].astype(BF16), preferred_element_type=F32)
                y = jnp.dot(hid.astype(BF16), w2_ref[0, s].astype(BF16), preferred_element_type=F32)
                gated = jnp.concatenate(
                    [gate * y[:, c * LANES:(c + 1) * LANES] for c in range(D_MODEL // LANES)], axis=1)
                total = gated if total is None else total + gated

            @pl.when(step == 0)
            def _():
                ys_ref[:rows] = total
                if rows < GROUP_TILE:
                    ys_ref[rows:] = jnp.zeros((GROUP_TILE - rows, D_MODEL), F32)

            @pl.when(step > 0)
            def _():
                ys_ref[:rows] += total

        half = GROUP_TILE // 2

        @pl.when(fill_ref[j] > half)
        def _():
            evaluate(GROUP_TILE)

        @pl.when(fill_ref[j] <= half)
        def _():
            evaluate(half)

    @pl.when((j >= n_used) & (step == 0))
    def _():
        ys_ref[...] = jnp.zeros_like(ys_ref)


def _experts(xloc, tile_tables, runs, slots, w1, w3, w2):
    steps = EXPERTS_PER_GROUP // EXPERTS_PER_STEP
    per_step = lambda w: w.reshape((N_EXPERTS // EXPERTS_PER_STEP, EXPERTS_PER_STEP) + w.shape[1:])
    w_spec = lambda shape: pl.BlockSpec((1, EXPERTS_PER_STEP) + shape, lambda j, s, tg, nu, *_: (
        tg[jnp.minimum(j, nu[0] - 1)] * steps + jnp.where(j < nu[0], s, steps - 1), 0, 0, 0))
    grid_spec = pltpu.PrefetchScalarGridSpec(
        num_scalar_prefetch=8,
        grid=(slots // GROUP_TILE, steps),
        in_specs=[
            pl.BlockSpec(memory_space=pl.ANY),
            w_spec((D_MODEL, D_EXPERT)), w_spec((D_MODEL, D_EXPERT)), w_spec((D_EXPERT, D_MODEL)),
        ],
        out_specs=pl.BlockSpec((GROUP_TILE, D_MODEL), lambda j, s, *_: (j, 0)),
        scratch_shapes=[pltpu.VMEM((2, GROUP_TILE, ROW_W), F32), pltpu.VMEM((GROUP_TILE, D_MODEL), BF16),
                        pltpu.VMEM((4, GROUP_TILE, LANES), F32), pltpu.SemaphoreType.DMA((2,))],
    )
    return pl.pallas_call(
        _expert_kernel,
        grid_spec=grid_spec,
        out_shape=jax.ShapeDtypeStruct((slots, D_MODEL), F32),
        compiler_params=pltpu.CompilerParams(
            dimension_semantics=("arbitrary", "arbitrary"), vmem_limit_bytes=EXPERT_VMEM_LIMIT),
        name="experts",
    )(*tile_tables, *runs, xloc, per_step(w1), per_step(w3), per_step(w2))


def _combine_kernel(n_tiles, tile_base, run_len_ref, run_src_ref, run_dst_ref,
                    route_ref, g2_ref, fg_ref, x1_hbm, ys_hbm, o_ref, buf, x1_buf, sem, x1_sem):
    i = pl.program_id(0)
    slot = i % COMBINE_RING

    def x1_copy(local_tile):
        s = local_tile % COMBINE_RING
        rows = pl.ds(pl.multiple_of((local_tile + tile_base) * OUTPROJ_TILE, OUTPROJ_TILE), OUTPROJ_TILE)
        return pltpu.make_async_copy(x1_hbm.at[rows], x1_buf.at[s], x1_sem.at[s])

    def run_copies(local_tile, act):
        s = local_tile % COMBINE_RING
        tile = local_tile + tile_base
        for g in range(N_GROUPS):
            n = run_len_ref[tile * N_GROUPS + g]
            src = run_src_ref[tile * N_GROUPS + g]
            dst = run_dst_ref[tile * N_GROUPS + g]
            for size in RUN_PIECES:
                done = n & (-2 * size)

                @pl.when((n & size) != 0)
                def _():
                    act(pltpu.make_async_copy(
                        ys_hbm.at[pl.ds(pl.multiple_of(dst + done, SUBLANES), size)],
                        buf.at[s, pl.ds(pl.multiple_of(src + done, SUBLANES), size)], sem.at[s]))

    def fetch(local_tile):
        run_copies(local_tile, lambda cp: cp.start())
        x1_copy(local_tile).start()

    @pl.when(i == 0)
    def _():
        buf[...] = jnp.zeros_like(buf)
        for ahead in range(min(COMBINE_RING - 1, n_tiles)):
            fetch(i + ahead)

    @pl.when(i + COMBINE_RING - 1 < n_tiles)
    def _():
        fetch(i + COMBINE_RING - 1)

    run_copies(i, lambda cp: cp.wait())
    x1_copy(i).wait()
    route = route_ref[...]
    lane = lax.broadcasted_iota(jnp.int32, route.shape, 1)
    local = jnp.sum(jnp.where(lane == ROUTE_LOCAL, route, 0.0), axis=1, keepdims=True)
    pick = jnp.where(lax.broadcasted_iota(jnp.int32, (route.shape[0], LOCAL_ROWS), 1).astype(F32) == local,
                     1.0, 0.0).astype(BF16)
    moe = sum(jnp.dot(pick, piece, preferred_element_type=F32) for piece in _split3(buf[slot])[:2])
    y = x1_buf[slot] + g2_ref[0, 0] * moe
    o_ref[...] = _rms(y) * fg_ref[...]


def _combine(ys, runs, x1, route, tokens, mod4, mod_row_of_tile, final_g, tile_base):
    tiles = tokens // OUTPROJ_TILE
    tok = lambda w: pl.BlockSpec((OUTPROJ_TILE, w), lambda i, *_: (i + tile_base, 0))
    grid_spec = pltpu.PrefetchScalarGridSpec(
        num_scalar_prefetch=3,
        grid=(tiles,),
        in_specs=[
            tok(ROUTER_COLS),
            pl.BlockSpec((1, 1, 1, D_MODEL), lambda i, *_: (mod_row_of_tile(i + tile_base), 5, 0, 0)),
            pl.BlockSpec((1, D_MODEL), lambda i, *_: (0, 0)),
            pl.BlockSpec(memory_space=pl.ANY),
            pl.BlockSpec(memory_space=pl.ANY),
        ],
        out_specs=pl.BlockSpec((OUTPROJ_TILE, D_MODEL), lambda i, *_: (i, 0)),
        scratch_shapes=[pltpu.VMEM((COMBINE_RING, LOCAL_ROWS, D_MODEL), F32),
                        pltpu.VMEM((COMBINE_RING, OUTPROJ_TILE, D_MODEL), F32),
                        pltpu.SemaphoreType.DMA((COMBINE_RING,)), pltpu.SemaphoreType.DMA((COMBINE_RING,))],
    )
    return pl.pallas_call(
        functools.partial(_combine_kernel, tiles, tile_base),
        grid_spec=grid_spec,
        out_shape=jax.ShapeDtypeStruct((tokens, D_MODEL), F32),
        compiler_params=pltpu.CompilerParams(
            dimension_semantics=("arbitrary",), vmem_limit_bytes=VMEM_LIMIT),
        name="combine",
    )(*runs, route, mod4, final_g, x1, ys)


def _routing_tables(counts):
    counts = counts.astype(jnp.int32)
    tiles = counts.shape[0]
    run_len = ((counts + SUBLANES - 1) // SUBLANES) * SUBLANES
    run_src = jnp.cumsum(run_len, axis=1) - run_len
    group_rows = jnp.sum(run_len, axis=0)
    padded = ((group_rows + GROUP_TILE - 1) // GROUP_TILE) * GROUP_TILE
    ends = jnp.cumsum(padded)
    offs = ends - padded
    run_dst = offs[None, :] + jnp.cumsum(run_len, axis=0) - run_len

    n_used = ends[-1] // GROUP_TILE
    max_rows = tiles * (OUTPROJ_TILE + N_GROUPS * (SUBLANES - 1))
    max_tiles = max_rows // GROUP_TILE + N_GROUPS
    tile_ids = jnp.minimum(jnp.arange(max_tiles, dtype=jnp.int32), n_used - 1)
    tile_group = jnp.sum(tile_ids[:, None] * GROUP_TILE >= ends[None, :], axis=1).astype(jnp.int32)
    of_group = (tile_group[:, None] == jnp.arange(N_GROUPS, dtype=jnp.int32))[:, None, :]
    start = jnp.sum(jnp.where(of_group, run_dst[None], 0), axis=-1)
    stop = start + jnp.sum(jnp.where(of_group, run_len[None], 0), axis=-1)
    row0 = (tile_ids * GROUP_TILE)[:, None]
    first = jnp.sum(stop <= row0, axis=1).astype(jnp.int32)
    last = jnp.sum(start < row0 + GROUP_TILE, axis=1).astype(jnp.int32) - 1
    group_end = jnp.sum(jnp.where(of_group[:, 0, :], (offs + group_rows)[None, :], 0), axis=-1)
    fill = jnp.clip(group_end - row0[:, 0], 0, GROUP_TILE).astype(jnp.int32)
    flat = lambda a: a.reshape(-1)
    return ((tile_group, n_used.reshape(1), first, last, fill), (flat(run_len), flat(run_src), flat(run_dst)),
            max_tiles * GROUP_TILE)


def _mixer(x, mod4, mod_row, is_grid, s_f0, s_b0, p):
    norm_mix_g, w_in_bf, conv_w, decay_rows = p
    y_conv, *qkvg = _inproj(x, mod4, mod_row, norm_mix_g, w_in_bf, conv_w, is_grid)
    per_seq = lambda a: a.reshape(x.shape[0], x.shape[1], a.shape[-1])
    ret = _retention(*map(per_seq, qkvg), decay_rows, s_f0, s_b0, emit_state=not is_grid)
    return y_conv, ret[0].reshape(-1, RET_W), ret[1:]


def kernel(x_prompt, x_sample, state_ret_fwd, state_ret_bwd, c, c_ctx, norm_mix_g, norm_ffn_g, w_ada, b_ada, w_in, conv_w, ret_decay_fwd, ret_decay_bwd, w_out, w_router_group, b_router_group, w_router_expert, b_router_expert, w_gate_e, w_up_e, w_down_e, final_norm_g):
    assert norm_mix_g.shape[0] == 1, "single-layer backbone"
    n_lat = c.shape[0]
    ctx_row = n_lat
    mod_rows = 8
    cvec = jnp.concatenate([c, c_ctx[None, :], jnp.zeros((mod_rows - n_lat - 1, D_MODEL), F32)], axis=0)
    mod = _modulation(cvec, w_ada[0], b_ada[0][None, :])
    mod4 = mod.reshape(mod_rows, 6, 1, D_MODEL)

    pad = ROUTER_COLS - N_GROUPS - N_EXPERTS
    w_router = jnp.concatenate(
        [w_router_group[0], w_router_expert[0], jnp.zeros((D_MODEL, pad), F32)], axis=1).astype(BF16)
    b_router = jnp.concatenate([b_router_group[0], b_router_expert[0], jnp.zeros((pad,), F32)])[None, :]
    decay_rows = jnp.broadcast_to(
        jnp.concatenate([ret_decay_fwd[0], ret_decay_bwd[0]])[:, None], (2 * RET_HEADS, LANES))
    p_mix = (norm_mix_g, w_in[0].astype(BF16), conv_w[0], decay_rows)
    w_out_bf = w_out[0].astype(BF16)
    final_g = final_norm_g[None, :]

    ctx_tokens = x_prompt.shape[0] * x_prompt.shape[1]
    lat_tokens = x_sample.shape[0] * x_sample.shape[1]
    ctx_tiles = ctx_tokens // OUTPROJ_TILE
    lat_tiles_per_seq = x_sample.shape[1] // OUTPROJ_TILE
    tile_mod = lambda i: jnp.where(i < ctx_tiles, ctx_row, (i - ctx_tiles) // lat_tiles_per_seq)
    flat = lambda a: a.reshape(-1, a.shape[-1])

    yc_c, yr_c, (s_f, s_b) = _mixer(x_prompt, mod4, lambda b: ctx_row, False, None, None, p_mix)
    yc_l, yr_l, _ = _mixer(x_sample, mod4, lambda b: b, True, state_ret_fwd, state_ret_bwd, p_mix)

    x1, xloc, route, cnt = _outproj((yc_c, yr_c, flat(x_prompt)), (yc_l, yr_l, flat(x_sample)), mod4, tile_mod,
                                    norm_ffn_g, w_out_bf, w_router, b_router)
    tile_tables, runs, slots = _routing_tables(cnt[:, 0, :N_GROUPS])
    ys = _experts(xloc, tile_tables, runs, slots, w_gate_e[0], w_up_e[0], w_down_e[0])
    y_prompt = _combine(ys, runs, x1, route, ctx_tokens, mod4, tile_mod, final_g, 0)
    y_sample = _combine(ys, runs, x1, route, lat_tokens, mod4, tile_mod, final_g, ctx_tiles)
    return (y_prompt.reshape(x_prompt.shape), y_sample.reshape(x_sample.shape),
            s_f.astype(x_prompt.dtype), s_b.astype(x_prompt.dtype))
```

```python
import functools

import jax
import jax.numpy as jnp
import numpy as np
from jax import lax
from jax.experimental import pallas as pl
from jax.experimental.pallas import tpu as pltpu

F32 = jnp.float32
BF16 = jnp.bfloat16

D_MODEL = 1024
GRID_W = 64
CONV_W = 512
RET_HEADS = 4
RET_DK = 128
RET_DV = 128
RET_W = RET_HEADS * RET_DV
QK_W = RET_HEADS * RET_DK
CHUNK = 128
N_GROUPS = 4
EXPERTS_PER_GROUP = 8
N_EXPERTS = N_GROUPS * EXPERTS_PER_GROUP
D_EXPERT = 256
ROPE_BASE = 10000.0
EPS = 1e-6

LANES = 128
TOKEN_TILE = 1024
OUTPROJ_TILE = 512
GROUP_TILE = 1024
EXPERTS_PER_STEP = 4
RET_UNROLL = 8
COMBINE_RING = 3
INPUT_RING = 3
SUBLANES = 8
XLOC_ROWS = OUTPROJ_TILE + N_GROUPS * SUBLANES
LOCAL_ROWS = OUTPROJ_TILE + LANES
ROW_W = D_MODEL + LANES
ROUTE_GROUP, ROUTE_LOCAL, ROUTE_E1, ROUTE_E2, ROUTE_W1, ROUTE_W2 = range(6)
MOD_COLS = 1536
ROUTER_COLS = LANES
VMEM_LIMIT = 48 * 1024 * 1024
EXPERT_VMEM_LIMIT = 56 * 1024 * 1024


def _silu(x):
    return x * jax.nn.sigmoid(x)


def _rms(x):
    return x * lax.rsqrt(jnp.mean(x * x, axis=-1, keepdims=True) + EPS)


def _bdot(a, b):
    return jnp.dot(a.astype(BF16), b.astype(BF16), preferred_element_type=F32)


def _split3(x):
    hi = x.astype(BF16)
    rest = x - hi.astype(F32)
    mid = rest.astype(BF16)
    return hi, mid, (rest - mid.astype(F32)).astype(BF16)


def _mod_kernel(c_ref, w_ref, b_ref, o_ref):
    o_ref[...] = _bdot(_silu(c_ref[...]), w_ref[...]) + b_ref[...]


def _modulation(cvec, w_ada, b_ada):
    rows = cvec.shape[0]
    n = w_ada.shape[1]
    return pl.pallas_call(
        _mod_kernel,
        grid=(n // MOD_COLS,),
        in_specs=[
            pl.BlockSpec((rows, D_MODEL), lambda j: (0, 0)),
            pl.BlockSpec((D_MODEL, MOD_COLS), lambda j: (0, j)),
            pl.BlockSpec((1, MOD_COLS), lambda j: (0, j)),
        ],
        out_specs=pl.BlockSpec((rows, MOD_COLS), lambda j: (0, j)),
        out_shape=jax.ShapeDtypeStruct((rows, n), F32),
        compiler_params=pltpu.CompilerParams(vmem_limit_bytes=VMEM_LIMIT),
        name="modulation",
    )(cvec, w_ada, b_ada)


def _inproj_kernel(seg, is_grid, x_hbm, sh_ref, sc_ref, ng_ref, w_ref, cw_ref, *rest):
    *rest, x_ring, x_sem = rest
    if is_grid:
        cos_ref, sa_ref, sb_ref, yc_ref, q_ref, k_ref, v_ref, g_ref = rest
    else:
        yc_ref, q_ref, k_ref, v_ref, g_ref = rest

    i = pl.program_id(0)
    n_tiles = pl.num_programs(0)

    def x_copy(tile):
        s = tile % INPUT_RING
        rows = pl.ds(pl.multiple_of(tile * TOKEN_TILE, TOKEN_TILE), TOKEN_TILE)
        return pltpu.make_async_copy(x_hbm.at[rows], x_ring.at[s], x_sem.at[s])

    @pl.when(i == 0)
    def _():
        for ahead in range(INPUT_RING - 1):
            @pl.when(ahead < n_tiles)
            def _():
                x_copy(i + ahead).start()

    @pl.when(i + INPUT_RING - 1 < n_tiles)
    def _():
        x_copy(i + INPUT_RING - 1).start()

    x_copy(i).wait()
    x = x_ring[i % INPUT_RING]
    xn = (_rms(x) * ng_ref[...]) * (1.0 + sc_ref[0, 0]) + sh_ref[0, 0]
    xb = xn.astype(BF16)

    def proj(c0, n):
        return jnp.dot(xb, w_ref[:, c0:c0 + n], preferred_element_type=F32)

    gate_b = proj(0, CONV_W)
    u = proj(CONV_W, CONV_W) * proj(2 * CONV_W, CONV_W)
    rows = u.shape[0]
    pos = lax.broadcasted_iota(jnp.int32, u.shape, 0) & (seg - 1)
    u_prev = jnp.where(pos != 0, pltpu.roll(u, 1, 0), 0.0)
    u_next = jnp.where(pos != seg - 1, pltpu.roll(u, rows - 1, 0), 0.0)
    conv = cw_ref[0:1, :] * u_prev + cw_ref[1:2, :] * u + cw_ref[2:3, :] * u_next
    yc_ref[...] = (gate_b * conv).astype(yc_ref.dtype)

    q0 = 3 * CONV_W
    q = proj(q0, QK_W)
    k = proj(q0 + QK_W, QK_W)
    if is_grid:
        cos, sa, sb = cos_ref[...], sa_ref[...], sb_ref[...]

        def rope(t):
            out = []
            for h in range(RET_HEADS):
                th = t[:, h * RET_DK:(h + 1) * RET_DK]
                out.append(th * cos + pltpu.roll(th, RET_DK - 1, 1) * sa + pltpu.roll(th, 1, 1) * sb)
            return jnp.concatenate(out, axis=1)

        q, k = rope(q), rope(k)
    q_ref[...] = q
    k_ref[...] = k
    v_ref[...] = proj(q0 + 2 * QK_W, RET_W)
    g_ref[...] = proj(q0 + 2 * QK_W + RET_W, RET_W)


def _rope_tables(length):
    pos = np.arange(length)
    row = (pos // GRID_W).astype(np.float64)
    col = (pos % GRID_W).astype(np.float64)
    n_pairs = RET_DK // 4
    freqs = ROPE_BASE ** (-(np.arange(n_pairs, dtype=np.float64) * 2.0 / (RET_DK // 2)))
    ang = np.concatenate([row[:, None] * freqs, col[:, None] * freqs], axis=-1)
    cos = np.repeat(np.cos(ang), 2, axis=-1)
    sin = np.repeat(np.sin(ang), 2, axis=-1)
    even = (np.arange(RET_DK) % 2) == 0
    return tuple(jnp.asarray(t, F32) for t in (cos, np.where(even, -sin, 0.0), np.where(even, 0.0, sin)))


def _inproj(x, mod4, mod_row, norm_g, w_in_bf, conv_w, is_grid):
    bsz, length, _ = x.shape
    seg = GRID_W if is_grid else length
    assert TOKEN_TILE % seg == 0 and (length % TOKEN_TILE == 0 or TOKEN_TILE % length == 0)
    tokens = bsz * length
    tiles_per_seq = max(length // TOKEN_TILE, 1)
    seqs_per_tile = max(TOKEN_TILE // length, 1)
    batch_of = lambda i: (i // tiles_per_seq) * seqs_per_tile

    def mod_spec(which):
        return pl.BlockSpec((1, 1, 1, D_MODEL), lambda i: (mod_row(batch_of(i)), which, 0, 0))

    def tok_spec(width):
        return pl.BlockSpec((TOKEN_TILE, width), lambda i: (i, 0))

    in_specs = [
        pl.BlockSpec(memory_space=pl.ANY), mod_spec(0), mod_spec(1),
        pl.BlockSpec((1, D_MODEL), lambda i: (0, 0)),
        pl.BlockSpec(w_in_bf.shape, lambda i: (0, 0)),
        pl.BlockSpec(conv_w.shape, lambda i: (0, 0)),
    ]
    args = [x.reshape(tokens, D_MODEL), mod4, mod4, norm_g, w_in_bf, conv_w]
    if is_grid:
        assert length % TOKEN_TILE == 0
        in_specs += [pl.BlockSpec((TOKEN_TILE, RET_DK), lambda i: (i % tiles_per_seq, 0))] * 3
        args += list(_rope_tables(length))
    shp = lambda w, dt: jax.ShapeDtypeStruct((tokens, w), dt)
    return pl.pallas_call(
        functools.partial(_inproj_kernel, seg, is_grid),
        grid=(tokens // TOKEN_TILE,),
        in_specs=in_specs,
        out_specs=[tok_spec(CONV_W), tok_spec(QK_W), tok_spec(QK_W), tok_spec(RET_W), tok_spec(RET_W)],
        out_shape=[shp(CONV_W, BF16), shp(QK_W, F32), shp(QK_W, F32), shp(RET_W, F32), shp(RET_W, F32)],
        scratch_shapes=[pltpu.VMEM((INPUT_RING, TOKEN_TILE, D_MODEL), F32), pltpu.SemaphoreType.DMA((INPUT_RING,))],
        compiler_params=pltpu.CompilerParams(
            dimension_semantics=("arbitrary",), vmem_limit_bytes=VMEM_LIMIT),
        name="inproj_grid" if is_grid else "inproj_seq",
    )(*args)


def _ret_kernel(n_chunks, heads, has_init, emit_state, a_ref, q_ref, k_ref, v_ref, g_ref, *rest):
    rest = list(rest)
    if has_init:
        sf0_ref, sb0_ref = rest[:2]
        rest = rest[2:]
    y_ref = rest.pop(0)
    if emit_state:
        sf_out, sb_out = rest[:2]
        rest = rest[2:]
    st_f, st_b, dec = rest
    c = CHUNK
    sq = (c, c)
    head0 = pl.program_id(0) * heads

    def log_decays(hh):
        lg_f = jnp.log1p(-jnp.exp(a_ref[pl.ds(head0 + hh, 1), :]))
        lg_b = jnp.log1p(-jnp.exp(a_ref[pl.ds(head0 + hh + RET_HEADS, 1), :]))
        return lg_f, lg_b

    @pl.when(pl.program_id(1) == 0)
    def _():
        row = lax.broadcasted_iota(jnp.int32, sq, 0).astype(F32)
        col = lax.broadcasted_iota(jnp.int32, sq, 1).astype(F32)
        scale = RET_DK ** -0.5
        for hh in range(heads):
            lg_f, lg_b = log_decays(hh)
            dec[hh, 0] = scale * (
                jnp.where(row >= col, jnp.exp(jnp.where(row >= col, row - col, 0.0) * lg_f), 0.0)
                + jnp.where(col >= row, jnp.exp(jnp.where(col >= row, col - row, 0.0) * lg_b), 0.0))
            dec[hh, 1] = jnp.exp((row + 1.0) * lg_f)
            dec[hh, 2] = jnp.exp((c - row) * lg_b)
            dec[hh, 3] = scale * jnp.exp((c - 1.0 - col) * lg_f)
            dec[hh, 4] = scale * jnp.exp(col * lg_b)

    def rows(n):
        return pl.ds(pl.multiple_of(n * c, c), c) if not isinstance(n, int) else pl.ds(n * c, c)

    def cols(hh):
        return slice(hh * RET_DK, (hh + 1) * RET_DK)

    def kv_step(hh, n):
        kt = jnp.transpose(k_ref[0, rows(n), cols(hh)])
        lhs = jnp.concatenate([kt * dec[hh, 3], kt * dec[hh, 4]], axis=0)
        kv = _bdot(lhs, v_ref[0, rows(n), cols(hh)])
        st_f[hh, n] = kv[:RET_DK]
        st_b[hh, n] = kv[RET_DK:]

    def scan(hh, st, decay, order, s):
        def step(i, s):
            n = order(i)
            kv = st[hh, n]
            st[hh, n] = s
            return s * decay + kv
        if n_chunks <= RET_UNROLL:
            for i in range(n_chunks):
                s = step(i, s)
            return s
        return lax.fori_loop(0, n_chunks, step, s, unroll=RET_UNROLL)

    def out_step(hh, n):
        q = q_ref[0, rows(n), cols(hh)]
        scores = lax.dot_general(q.astype(BF16), k_ref[0, rows(n), cols(hh)].astype(BF16),
                                 (((1,), (1,)), ((), ())), preferred_element_type=F32)
        o = _bdot(scores * dec[hh, 0], v_ref[0, rows(n), cols(hh)])
        q_dec = jnp.concatenate([q * dec[hh, 1], q * dec[hh, 2]], axis=1)
        o = o + _bdot(q_dec, jnp.concatenate([st_f[hh, n], st_b[hh, n]], axis=0))
        y = _silu(g_ref[0, rows(n), cols(hh)]) * _rms(o)
        y_ref[0, rows(n), cols(hh)] = y.astype(y_ref.dtype)

    def over_chunks(step):
        if n_chunks * heads <= RET_UNROLL:
            for hh in range(heads):
                for n in range(n_chunks):
                    step(hh, n)
        else:
            for hh in range(heads):
                lax.fori_loop(0, n_chunks, lambda n, carry: (step(hh, n), carry)[1], 0, unroll=RET_UNROLL)

    over_chunks(kv_step)
    finals = []
    for hh in range(heads):
        lg_f, lg_b = log_decays(hh)
        s_f = sf0_ref[0, 0, hh] if has_init else jnp.zeros(sq, F32)
        s_b = sb0_ref[0, 0, hh] if has_init else jnp.zeros(sq, F32)
        s_f = scan(hh, st_f, jnp.exp(c * lg_f), lambda i: i, s_f)
        s_b = scan(hh, st_b, jnp.exp(c * lg_b), lambda i: n_chunks - 1 - i, s_b)
        finals.append((s_f, s_b))
    over_chunks(out_step)
    if emit_state:
        for hh, (s_f, s_b) in enumerate(finals):
            sf_out[0, 0, hh] = s_f
            sb_out[0, 0, hh] = s_b


def _retention(q, k, v, g, decay_rows, s_f0, s_b0, emit_state):
    bsz, length, _ = q.shape
    n_chunks = length // CHUNK
    has_init = s_f0 is not None
    heads = RET_HEADS if n_chunks * RET_HEADS <= RET_UNROLL else 1
    head_spec = pl.BlockSpec((1, length, heads * RET_DK), lambda h, b: (b, 0, h))
    st_spec = pl.BlockSpec((1, 1, heads, RET_DK, RET_DV), lambda h, b: (b, 0, h, 0, 0))
    in_specs = [pl.BlockSpec(decay_rows.shape, lambda h, b: (0, 0))] + [head_spec] * 4
    args = [decay_rows, q, k, v, g]
    if has_init:
        in_specs += [st_spec, st_spec]
        args += [s_f0, s_b0]
    out_specs = [head_spec]
    out_shape = [jax.ShapeDtypeStruct((bsz, length, RET_W), BF16)]
    if emit_state:
        st_shape = jax.ShapeDtypeStruct((bsz, 1, RET_HEADS, RET_DK, RET_DV), F32)
        out_specs += [st_spec, st_spec]
        out_shape += [st_shape, st_shape]
    return pl.pallas_call(
        functools.partial(_ret_kernel, n_chunks, heads, has_init, emit_state),
        grid=(RET_HEADS // heads, bsz),
        in_specs=in_specs,
        out_specs=out_specs,
        out_shape=out_shape,
        scratch_shapes=[
            pltpu.VMEM((heads, n_chunks, RET_DK, RET_DV), F32),
            pltpu.VMEM((heads, n_chunks, RET_DK, RET_DV), F32),
            pltpu.VMEM((heads, 5, CHUNK, CHUNK), F32),
        ],
        compiler_params=pltpu.CompilerParams(
            dimension_semantics=("arbitrary", "arbitrary"), vmem_limit_bytes=VMEM_LIMIT),
        name="retention_init" if has_init else "retention_zero",
    )(*args)


def _route(logits):
    lane = lax.broadcasted_iota(jnp.int32, logits.shape, 1)
    lane_f = lane.astype(F32)
    neg = -jnp.inf
    far = float(LANES)
    is_g = lane < N_GROUPS
    lg = jnp.where(is_g, logits, neg)
    g_max = jnp.max(lg, axis=1, keepdims=True)
    g_idx = jnp.min(jnp.where(lg == g_max, lane_f, far), axis=1, keepdims=True)
    p_sel = 1.0 / jnp.sum(jnp.where(is_g, jnp.exp(lg - g_max), 0.0), axis=1, keepdims=True)
    lane_group = ((lane - N_GROUPS) >> (EXPERTS_PER_GROUP.bit_length() - 1)).astype(F32)
    sel = (lane >= N_GROUPS) & (lane < N_GROUPS + N_EXPERTS) & (lane_group == g_idx)
    le = jnp.where(sel, logits, neg)
    v1 = jnp.max(le, axis=1, keepdims=True)
    i1 = jnp.min(jnp.where(le == v1, lane_f, far), axis=1, keepdims=True)
    le2 = jnp.where(lane_f == i1, neg, le)
    v2 = jnp.max(le2, axis=1, keepdims=True)
    i2 = jnp.min(jnp.where(le2 == v2, lane_f, far), axis=1, keepdims=True)
    e2 = jnp.exp(v2 - v1)
    w1 = p_sel * (1.0 / (1.0 + e2))
    w2 = p_sel * (e2 / (1.0 + e2))
    return lane, lane_f, g_idx, i1, i2, w1, w2


def _outproj_kernel(ctx_tiles, yc_c, yr_c, x_c, yc_l, yr_l, x_l, *rest):
    @pl.when(pl.program_id(0) < ctx_tiles)
    def _():
        _outproj_tile(yc_c, yr_c, x_c, *rest)

    @pl.when(pl.program_id(0) >= ctx_tiles)
    def _():
        _outproj_tile(yc_l, yr_l, x_l, *rest)


def _outproj_tile(yc_ref, yr_ref, x_ref, g1_ref, sh_ref, sc_ref, ng_ref, wo_ref, wr_ref, br_ref,
                  x1_ref, xloc_ref, route_ref, cnt_ref):
    m = (jnp.dot(yc_ref[...], wo_ref[0:CONV_W, :], preferred_element_type=F32)
         + jnp.dot(yr_ref[...], wo_ref[CONV_W:, :], preferred_element_type=F32))
    x1 = x_ref[...] + g1_ref[0, 0] * m
    x1_ref[...] = x1
    xn = (_rms(x1) * ng_ref[...]) * (1.0 + sc_ref[0, 0]) + sh_ref[0, 0]
    xb = xn.astype(BF16)
    logits = jnp.dot(xb, wr_ref[...], preferred_element_type=F32) + br_ref[...]
    lane, lane_f, g_idx, i1, i2, w1, w2 = _route(logits)

    picks = jnp.where(lane_f == g_idx, 1.0, 0.0)
    rows = picks.shape[0]
    tri = (lax.broadcasted_iota(jnp.int32, (rows, rows), 0)
           > lax.broadcasted_iota(jnp.int32, (rows, rows), 1))
    before = jnp.dot(jnp.where(tri, 1.0, 0.0).astype(BF16), picks.astype(BF16),
                     preferred_element_type=F32)
    count = jnp.sum(picks, axis=0, keepdims=True)
    cnt_ref[0] = count
    count8 = jnp.broadcast_to(jnp.floor((count + (SUBLANES - 1)) * (1.0 / SUBLANES)) * SUBLANES,
                              (SUBLANES, LANES))
    lane8 = lane[:SUBLANES]
    start = sum(jnp.where(lane8 >= k, pltpu.roll(count8, k, 1), 0.0) for k in range(1, N_GROUPS))
    local = jnp.sum(jnp.where(lane_f == g_idx, before + start[0:1], 0.0), axis=1, keepdims=True)
    route = jnp.where(lane == ROUTE_GROUP, g_idx, jnp.where(lane == ROUTE_LOCAL, local, jnp.where(
        lane == ROUTE_E1, i1 - N_GROUPS, jnp.where(lane == ROUTE_E2, i2 - N_GROUPS, jnp.where(
            lane == ROUTE_W1, w1, jnp.where(lane == ROUTE_W2, w2, 0.0))))))
    route_ref[...] = route

    local_row = jnp.transpose(jnp.broadcast_to(local, (rows, LANES)))[0:1, :]
    place = jnp.where(lax.broadcasted_iota(jnp.int32, (XLOC_ROWS, rows), 0).astype(F32) == local_row,
                      1.0, 0.0).astype(BF16)
    xloc_ref[0, :, :D_MODEL] = jnp.dot(place, xb, preferred_element_type=F32)
    xloc_ref[0, :, D_MODEL:] = sum(
        jnp.dot(place, piece, preferred_element_type=F32) for piece in _split3(route))


def _outproj(ctx, lat, mod4, mod_row_of_tile, norm_g, w_out_bf, w_router_bf, b_router):
    ctx_tiles = ctx[2].shape[0] // OUTPROJ_TILE
    tiles = ctx_tiles + lat[2].shape[0] // OUTPROJ_TILE
    tokens = tiles * OUTPROJ_TILE

    def mod_spec(which):
        return pl.BlockSpec((1, 1, 1, D_MODEL), lambda i: (mod_row_of_tile(i), which, 0, 0))

    ctx_tok = lambda w: pl.BlockSpec((OUTPROJ_TILE, w), lambda i: (jnp.minimum(i, ctx_tiles - 1), 0))
    lat_tok = lambda w: pl.BlockSpec((OUTPROJ_TILE, w), lambda i: (jnp.maximum(i - ctx_tiles, 0), 0))
    tok = lambda w: pl.BlockSpec((OUTPROJ_TILE, w), lambda i: (i, 0))
    full = lambda a: pl.BlockSpec(a.shape, lambda i: (0,) * a.ndim)
    widths = (CONV_W, RET_W, D_MODEL)
    return pl.pallas_call(
        functools.partial(_outproj_kernel, ctx_tiles),
        grid=(tiles,),
        in_specs=[ctx_tok(w) for w in widths] + [lat_tok(w) for w in widths] + [
            mod_spec(2), mod_spec(3), mod_spec(4),
            full(norm_g), full(w_out_bf), full(w_router_bf), full(b_router)],
        out_specs=[tok(D_MODEL),
                   pl.BlockSpec((1, XLOC_ROWS, ROW_W), lambda i: (i, 0, 0)),
                   tok(ROUTER_COLS),
                   pl.BlockSpec((1, 1, ROUTER_COLS), lambda i: (i, 0, 0))],
        out_shape=[jax.ShapeDtypeStruct((tokens, D_MODEL), F32),
                   jax.ShapeDtypeStruct((tiles, XLOC_ROWS, ROW_W), F32),
                   jax.ShapeDtypeStruct((tokens, ROUTER_COLS), F32),
                   jax.ShapeDtypeStruct((tiles, 1, ROUTER_COLS), F32)],
        compiler_params=pltpu.CompilerParams(
            dimension_semantics=("parallel",), vmem_limit_bytes=VMEM_LIMIT),
        name="outproj",
    )(*ctx, *lat, mod4, mod4, mod4, norm_g, w_out_bf, w_router_bf, b_router)


RUN_PIECES = tuple(SUBLANES << b for b in reversed(range((OUTPROJ_TILE // SUBLANES).bit_length())))


def _expert_kernel(tile_group_ref, n_used_ref, first_ref, last_ref, fill_ref,
                   run_len_ref, run_src_ref, run_dst_ref,
                   xloc_hbm, w1_ref, w3_ref, w2_ref, ys_ref, xbuf, xb, gate_tabs, sem):
    j = pl.program_id(0)
    step = pl.program_id(1)
    n_used = n_used_ref[0]

    def tile_fetch(tile):
        slot = tile % 2
        group = tile_group_ref[tile]
        row0 = tile * GROUP_TILE

        def from_token_tile(b, carry):
            run = b * N_GROUPS + group
            lo = jnp.maximum(run_dst_ref[run], row0)
            hi = jnp.minimum(run_dst_ref[run] + run_len_ref[run], row0 + GROUP_TILE)
            n = jnp.maximum(hi - lo, 0)
            src = run_src_ref[run] + lo - run_dst_ref[run]
            dst = lo - row0
            for size in RUN_PIECES:
                done = n & (-2 * size)

                @pl.when((n & size) != 0)
                def _():
                    pltpu.make_async_copy(
                        xloc_hbm.at[b, pl.ds(pl.multiple_of(src + done, SUBLANES), size)],
                        xbuf.at[slot, pl.ds(pl.multiple_of(dst + done, SUBLANES), size)], sem.at[slot]).start()
            return carry

        lax.fori_loop(first_ref[tile], last_ref[tile] + 1, from_token_tile, 0)

    def tile_wait(tile):
        slot = tile % 2
        for size in (GROUP_TILE,) + tuple(GROUP_TILE >> k for k in range(1, (GROUP_TILE // SUBLANES).bit_length())):
            @pl.when((fill_ref[tile] & size) != 0)
            def _():
                pltpu.make_async_copy(xbuf.at[1 - slot, pl.ds(0, size)], xbuf.at[slot, pl.ds(0, size)],
                                      sem.at[slot]).wait()

    @pl.when(j < n_used)
    def _():
        @pl.when(step == 0)
        def _():
            @pl.when(j == 0)
            def _():
                xbuf[...] = jnp.zeros_like(xbuf)
                tile_fetch(j)

            tile_wait(j)
            rows_in = xbuf[j % 2]
            xb[...] = rows_in[:, :D_MODEL].astype(BF16)
            route = rows_in[:, D_MODEL:]
            lane = lax.broadcasted_iota(jnp.int32, route.shape, 1)
            for n, which in enumerate((ROUTE_E1, ROUTE_E2, ROUTE_W1, ROUTE_W2)):
                col = jnp.sum(jnp.where(lane == which, route, 0.0), axis=1, keepdims=True)
                gate_tabs[n] = jnp.broadcast_to(col, route.shape)

            @pl.when(j + 1 < n_used)
            def _():
                tile_fetch(j + 1)

        def evaluate(rows):
            x = xb[:rows]
            total = None
            for s in range(EXPERTS_PER_STEP):
                expert = (tile_group_ref[j] * EXPERTS_PER_GROUP + step * EXPERTS_PER_STEP + s).astype(F32)
                gate = (jnp.where(gate_tabs[0, :rows] == expert, gate_tabs[2, :rows], 0.0)
                        + jnp.where(gate_tabs[1, :rows] == expert, gate_tabs[3, :rows], 0.0))
                hid = _silu(jnp.dot(x, w1_ref[0, s].astype(BF16), preferred_element_type=F32)) * jnp.dot(
                    x, w3_ref[0, s].astype(BF16), preferred_element_type=F32)
                y = jnp.dot(hid.astype(BF16), w2_ref[0, s].astype(BF16), preferred_element_type=F32)
                gated = jnp.concatenate(
                    [gate * y[:, c * LANES:(c + 1) * LANES] for c in range(D_MODEL // LANES)], axis=1)
                total = gated if total is None else total + gated

            @pl.when(step == 0)
            def _():
                ys_ref[:rows] = total
                if rows < GROUP_TILE:
                    ys_ref[rows:] = jnp.zeros((GROUP_TILE - rows, D_MODEL), F32)

            @pl.when(step > 0)
            def _():
                ys_ref[:rows] += total

        half = GROUP_TILE // 2

        @pl.when(fill_ref[j] > half)
        def _():
            evaluate(GROUP_TILE)

        @pl.when(fill_ref[j] <= half)
        def _():
            evaluate(half)

    @pl.when((j >= n_used) & (step == 0))
    def _():
        ys_ref[...] = jnp.zeros_like(ys_ref)


def _experts(xloc, tile_tables, runs, slots, w1, w3, w2):
    steps = EXPERTS_PER_GROUP // EXPERTS_PER_STEP
    per_step = lambda w: w.reshape((N_EXPERTS // EXPERTS_PER_STEP, EXPERTS_PER_STEP) + w.shape[1:])
    w_spec = lambda shape: pl.BlockSpec((1, EXPERTS_PER_STEP) + shape, lambda j, s, tg, nu, *_: (
        tg[jnp.minimum(j, nu[0] - 1)] * steps + jnp.where(j < nu[0], s, steps - 1), 0, 0, 0))
    grid_spec = pltpu.PrefetchScalarGridSpec(
        num_scalar_prefetch=8,
        grid=(slots // GROUP_TILE, steps),
        in_specs=[
            pl.BlockSpec(memory_space=pl.ANY),
            w_spec((D_MODEL, D_EXPERT)), w_spec((D_MODEL, D_EXPERT)), w_spec((D_EXPERT, D_MODEL)),
        ],
        out_specs=pl.BlockSpec((GROUP_TILE, D_MODEL), lambda j, s, *_: (j, 0)),
        scratch_shapes=[pltpu.VMEM((2, GROUP_TILE, ROW_W), F32), pltpu.VMEM((GROUP_TILE, D_MODEL), BF16),
                        pltpu.VMEM((4, GROUP_TILE, LANES), F32), pltpu.SemaphoreType.DMA((2,))],
    )
    return pl.pallas_call(
        _expert_kernel,
        grid_spec=grid_spec,
        out_shape=jax.ShapeDtypeStruct((slots, D_MODEL), F32),
        compiler_params=pltpu.CompilerParams(
            dimension_semantics=("arbitrary", "arbitrary"), vmem_limit_bytes=EXPERT_VMEM_LIMIT),
        name="experts",
    )(*tile_tables, *runs, xloc, per_step(w1), per_step(w3), per_step(w2))


def _combine_kernel(n_tiles, tile_base, run_len_ref, run_src_ref, run_dst_ref,
                    route_ref, g2_ref, fg_ref, x1_hbm, ys_hbm, o_ref, buf, x1_buf, sem, x1_sem):
    i = pl.program_id(0)
    slot = i % COMBINE_RING

    def x1_copy(local_tile):
        s = local_tile % COMBINE_RING
        rows = pl.ds(pl.multiple_of((local_tile + tile_base) * OUTPROJ_TILE, OUTPROJ_TILE), OUTPROJ_TILE)
        return pltpu.make_async_copy(x1_hbm.at[rows], x1_buf.at[s], x1_sem.at[s])

    def run_copies(local_tile, act):
        s = local_tile % COMBINE_RING
        tile = local_tile + tile_base
        for g in range(N_GROUPS):
            n = run_len_ref[tile * N_GROUPS + g]
            src = run_src_ref[tile * N_GROUPS + g]
            dst = run_dst_ref[tile * N_GROUPS + g]
            for size in RUN_PIECES:
                done = n & (-2 * size)

                @pl.when((n & size) != 0)
                def _():
                    act(pltpu.make_async_copy(
                        ys_hbm.at[pl.ds(pl.multiple_of(dst + done, SUBLANES), size)],
                        buf.at[s, pl.ds(pl.multiple_of(src + done, SUBLANES), size)], sem.at[s]))

    def fetch(local_tile):
        run_copies(local_tile, lambda cp: cp.start())
        x1_copy(local_tile).start()

    @pl.when(i == 0)
    def _():
        buf[...] = jnp.zeros_like(buf)
        for ahead in range(min(COMBINE_RING - 1, n_tiles)):
            fetch(i + ahead)

    @pl.when(i + COMBINE_RING - 1 < n_tiles)
    def _():
        fetch(i + COMBINE_RING - 1)

    run_copies(i, lambda cp: cp.wait())
    x1_copy(i).wait()
    route = route_ref[...]
    lane = lax.broadcasted_iota(jnp.int32, route.shape, 1)
    local = jnp.sum(jnp.where(lane == ROUTE_LOCAL, route, 0.0), axis=1, keepdims=True)
    pick = jnp.where(lax.broadcasted_iota(jnp.int32, (route.shape[0], LOCAL_ROWS), 1).astype(F32) == local,
                     1.0, 0.0).astype(BF16)
    moe = sum(jnp.dot(pick, piece, preferred_element_type=F32) for piece in _split3(buf[slot])[:2])
    y = x1_buf[slot] + g2_ref[0, 0] * moe
    o_ref[...] = _rms(y) * fg_ref[...]


def _combine(ys, runs, x1, route, tokens, mod4, mod_row_of_tile, final_g, tile_base):
    tiles = tokens // OUTPROJ_TILE
    tok = lambda w: pl.BlockSpec((OUTPROJ_TILE, w), lambda i, *_: (i + tile_base, 0))
    grid_spec = pltpu.PrefetchScalarGridSpec(
        num_scalar_prefetch=3,
        grid=(tiles,),
        in_specs=[
            tok(ROUTER_COLS),
            pl.BlockSpec((1, 1, 1, D_MODEL), lambda i, *_: (mod_row_of_tile(i + tile_base), 5, 0, 0)),
            pl.BlockSpec((1, D_MODEL), lambda i, *_: (0, 0)),
            pl.BlockSpec(memory_space=pl.ANY),
            pl.BlockSpec(memory_space=pl.ANY),
        ],
        out_specs=pl.BlockSpec((OUTPROJ_TILE, D_MODEL), lambda i, *_: (i, 0)),
        scratch_shapes=[pltpu.VMEM((COMBINE_RING, LOCAL_ROWS, D_MODEL), F32),
                        pltpu.VMEM((COMBINE_RING, OUTPROJ_TILE, D_MODEL), F32),
                        pltpu.SemaphoreType.DMA((COMBINE_RING,)), pltpu.SemaphoreType.DMA((COMBINE_RING,))],
    )
    return pl.pallas_call(
        functools.partial(_combine_kernel, tiles, tile_base),
        grid_spec=grid_spec,
        out_shape=jax.ShapeDtypeStruct((tokens, D_MODEL), F32),
        compiler_params=pltpu.CompilerParams(
            dimension_semantics=("arbitrary",), vmem_limit_bytes=VMEM_LIMIT),
        name="combine",
    )(*runs, route, mod4, final_g, x1, ys)


def _routing_tables(counts):
    counts = counts.astype(jnp.int32)
    tiles = counts.shape[0]
    run_len = ((counts + SUBLANES - 1) // SUBLANES) * SUBLANES
    run_src = jnp.cumsum(run_len, axis=1) - run_len
    group_rows = jnp.sum(run_len, axis=0)
    padded = ((group_rows + GROUP_TILE - 1) // GROUP_TILE) * GROUP_TILE
    ends = jnp.cumsum(padded)
    offs = ends - padded
    run_dst = offs[None, :] + jnp.cumsum(run_len, axis=0) - run_len

    n_used = ends[-1] // GROUP_TILE
    max_rows = tiles * (OUTPROJ_TILE + N_GROUPS * (SUBLANES - 1))
    max_tiles = max_rows // GROUP_TILE + N_GROUPS
    tile_ids = jnp.minimum(jnp.arange(max_tiles, dtype=jnp.int32), n_used - 1)
    tile_group = jnp.sum(tile_ids[:, None] * GROUP_TILE >= ends[None, :], axis=1).astype(jnp.int32)
    of_group = (tile_group[:, None] == jnp.arange(N_GROUPS, dtype=jnp.int32))[:, None, :]
    start = jnp.sum(jnp.where(of_group, run_dst[None], 0), axis=-1)
    stop = start + jnp.sum(jnp.where(of_group, run_len[None], 0), axis=-1)
    row0 = (tile_ids * GROUP_TILE)[:, None]
    first = jnp.sum(stop <= row0, axis=1).astype(jnp.int32)
    last = jnp.sum(start < row0 + GROUP_TILE, axis=1).astype(jnp.int32) - 1
    group_end = jnp.sum(jnp.where(of_group[:, 0, :], (offs + group_rows)[None, :], 0), axis=-1)
    fill = jnp.clip(group_end - row0[:, 0], 0, GROUP_TILE).astype(jnp.int32)
    flat = lambda a: a.reshape(-1)
    return ((tile_group, n_used.reshape(1), first, last, fill), (flat(run_len), flat(run_src), flat(run_dst)),
            max_tiles * GROUP_TILE)


def _mixer(x, mod4, mod_row, is_grid, s_f0, s_b0, p):
    norm_mix_g, w_in_bf, conv_w, decay_rows = p
    y_conv, *qkvg = _inproj(x, mod4, mod_row, norm_mix_g, w_in_bf, conv_w, is_grid)
    per_seq = lambda a: a.reshape(x.shape[0], x.shape[1], a.shape[-1])
    ret = _retention(*map(per_seq, qkvg), decay_rows, s_f0, s_b0, emit_state=not is_grid)
    return y_conv, ret[0].reshape(-1, RET_W), ret[1:]


def kernel(x_prompt, x_sample, state_ret_fwd, state_ret_bwd, c, c_ctx, norm_mix_g, norm_ffn_g, w_ada, b_ada, w_in, conv_w, ret_decay_fwd, ret_decay_bwd, w_out, w_router_group, b_router_group, w_router_expert, b_router_expert, w_gate_e, w_up_e, w_down_e, final_norm_g):
    assert norm_mix_g.shape[0] == 1, "single-layer backbone"
    n_lat = c.shape[0]
    ctx_row = n_lat
    mod_rows = 8
    cvec = jnp.concatenate([c, c_ctx[None, :], jnp.zeros((mod_rows - n_lat - 1, D_MODEL), F32)], axis=0)
    mod = _modulation(cvec, w_ada[0], b_ada[0][None, :])
    mod4 = mod.reshape(mod_rows, 6, 1, D_MODEL)

    pad = ROUTER_COLS - N_GROUPS - N_EXPERTS
    w_router = jnp.concatenate(
        [w_router_group[0], w_router_expert[0], jnp.zeros((D_MODEL, pad), F32)], axis=1).astype(BF16)
    b_router = jnp.concatenate([b_router_group[0], b_router_expert[0], jnp.zeros((pad,), F32)])[None, :]
    decay_rows = jnp.broadcast_to(
        jnp.concatenate([ret_decay_fwd[0], ret_decay_bwd[0]])[:, None], (2 * RET_HEADS, LANES))
    p_mix = (norm_mix_g, w_in[0].astype(BF16), conv_w[0], decay_rows)
    w_out_bf = w_out[0].astype(BF16)
    final_g = final_norm_g[None, :]

    ctx_tokens = x_prompt.shape[0] * x_prompt.shape[1]
    lat_tokens = x_sample.shape[0] * x_sample.shape[1]
    ctx_tiles = ctx_tokens // OUTPROJ_TILE
    lat_tiles_per_seq = x_sample.shape[1] // OUTPROJ_TILE
    tile_mod = lambda i: jnp.where(i < ctx_tiles, ctx_row, (i - ctx_tiles) // lat_tiles_per_seq)
    flat = lambda a: a.reshape(-1, a.shape[-1])

    yc_c, yr_c, (s_f, s_b) = _mixer(x_prompt, mod4, lambda b: ctx_row, False, None, None, p_mix)
    yc_l, yr_l, _ = _mixer(x_sample, mod4, lambda b: b, True, state_ret_fwd, state_ret_bwd, p_mix)

    x1, xloc, route, cnt = _outproj((yc_c, yr_c, flat(x_prompt)), (yc_l, yr_l, flat(x_sample)), mod4, tile_mod,
                                    norm_ffn_g, w_out_bf, w_router, b_router)
    tile_tables, runs, slots = _routing_tables(cnt[:, 0, :N_GROUPS])
    ys = _experts(xloc, tile_tables, runs, slots, w_gate_e[0], w_up_e[0], w_down_e[0])
    y_prompt = _combine(ys, runs, x1, route, ctx_tokens, mod4, tile_mod, final_g, 0)
    y_sample = _combine(ys, runs, x1, route, lat_tokens, mod4, tile_mod, final_g, ctx_tiles)
    return (y_prompt.reshape(x_prompt.shape), y_sample.reshape(x_sample.shape),
            s_f.astype(x_prompt.dtype), s_b.astype(x_prompt.dtype))
```

```python
import functools

import jax
import jax.numpy as jnp
import numpy as np
from jax import lax
from jax.experimental import pallas as pl
from jax.experimental.pallas import tpu as pltpu

F32 = jnp.float32
BF16 = jnp.bfloat16

D_MODEL = 1024
GRID_W = 64
CONV_W = 512
RET_HEADS = 4
RET_DK = 128
RET_DV = 128
RET_W = RET_HEADS * RET_DV
QK_W = RET_HEADS * RET_DK
CHUNK = 128
N_GROUPS = 4
EXPERTS_PER_GROUP = 8
N_EXPERTS = N_GROUPS * EXPERTS_PER_GROUP
D_EXPERT = 256
ROPE_BASE = 10000.0
EPS = 1e-6

LANES = 128
TOKEN_TILE = 1024
OUTPROJ_TILE = 512
GROUP_TILE = 1024
EXPERTS_PER_STEP = 4
RET_UNROLL = 8
COMBINE_RING = 3
SUBLANES = 8
XLOC_ROWS = OUTPROJ_TILE + N_GROUPS * SUBLANES
LOCAL_ROWS = OUTPROJ_TILE + LANES
ROW_W = D_MODEL + LANES
ROUTE_GROUP, ROUTE_LOCAL, ROUTE_E1, ROUTE_E2, ROUTE_W1, ROUTE_W2 = range(6)
MOD_COLS = 1536
ROUTER_COLS = LANES
MIB = 1024 * 1024
SMALL_VMEM_LIMIT = 16 * MIB
MID_VMEM_LIMIT = 32 * MIB
VMEM_LIMIT = 48 * MIB
EXPERT_VMEM_LIMIT = 56 * MIB


def _silu(x):
    return x * jax.nn.sigmoid(x)


def _rms(x):
    return x * lax.rsqrt(jnp.mean(x * x, axis=-1, keepdims=True) + EPS)


def _bdot(a, b):
    return jnp.dot(a.astype(BF16), b.astype(BF16), preferred_element_type=F32)


def _split3(x):
    hi = x.astype(BF16)
    rest = x - hi.astype(F32)
    mid = rest.astype(BF16)
    return hi, mid, (rest - mid.astype(F32)).astype(BF16)


def _mod_kernel(c_ref, w_ref, b_ref, o_ref):
    o_ref[...] = _bdot(_silu(c_ref[...]), w_ref[...]) + b_ref[...]


def _modulation(cvec, w_ada, b_ada):
    rows = cvec.shape[0]
    n = w_ada.shape[1]
    return pl.pallas_call(
        _mod_kernel,
        grid=(n // MOD_COLS,),
        in_specs=[
            pl.BlockSpec((rows, D_MODEL), lambda j: (0, 0)),
            pl.BlockSpec((D_MODEL, MOD_COLS), lambda j: (0, j)),
            pl.BlockSpec((1, MOD_COLS), lambda j: (0, j)),
        ],
        out_specs=pl.BlockSpec((rows, MOD_COLS), lambda j: (0, j)),
        out_shape=jax.ShapeDtypeStruct((rows, n), F32),
        compiler_params=pltpu.CompilerParams(vmem_limit_bytes=SMALL_VMEM_LIMIT),
        name="modulation",
    )(cvec, w_ada, b_ada)


def _inproj_kernel(seg, is_grid, x_ref, sh_ref, sc_ref, ng_ref, w_ref, cw_ref, *rest):
    if is_grid:
        cos_ref, sa_ref, sb_ref, yc_ref, q_ref, k_ref, v_ref, g_ref = rest
    else:
        yc_ref, q_ref, k_ref, v_ref, g_ref = rest
    x = x_ref[...]
    xn = (_rms(x) * ng_ref[...]) * (1.0 + sc_ref[0, 0]) + sh_ref[0, 0]
    xb = xn.astype(BF16)

    def proj(c0, n):
        return jnp.dot(xb, w_ref[:, c0:c0 + n], preferred_element_type=F32)

    gate_b = proj(0, CONV_W)
    u = proj(CONV_W, CONV_W) * proj(2 * CONV_W, CONV_W)
    rows = u.shape[0]
    pos = lax.broadcasted_iota(jnp.int32, u.shape, 0) & (seg - 1)
    u_prev = jnp.where(pos != 0, pltpu.roll(u, 1, 0), 0.0)
    u_next = jnp.where(pos != seg - 1, pltpu.roll(u, rows - 1, 0), 0.0)
    conv = cw_ref[0:1, :] * u_prev + cw_ref[1:2, :] * u + cw_ref[2:3, :] * u_next
    yc_ref[...] = (gate_b * conv).astype(yc_ref.dtype)

    q0 = 3 * CONV_W
    q = proj(q0, QK_W)
    k = proj(q0 + QK_W, QK_W)
    if is_grid:
        cos, sa, sb = cos_ref[...], sa_ref[...], sb_ref[...]

        def rope(t):
            out = []
            for h in range(RET_HEADS):
                th = t[:, h * RET_DK:(h + 1) * RET_DK]
                out.append(th * cos + pltpu.roll(th, RET_DK - 1, 1) * sa + pltpu.roll(th, 1, 1) * sb)
            return jnp.concatenate(out, axis=1)

        q, k = rope(q), rope(k)
    q_ref[...] = q
    k_ref[...] = k
    v_ref[...] = proj(q0 + 2 * QK_W, RET_W)
    g_ref[...] = proj(q0 + 2 * QK_W + RET_W, RET_W)


def _rope_tables(length):
    pos = np.arange(length)
    row = (pos // GRID_W).astype(np.float64)
    col = (pos % GRID_W).astype(np.float64)
    n_pairs = RET_DK // 4
    freqs = ROPE_BASE ** (-(np.arange(n_pairs, dtype=np.float64) * 2.0 / (RET_DK // 2)))
    ang = np.concatenate([row[:, None] * freqs, col[:, None] * freqs], axis=-1)
    cos = np.repeat(np.cos(ang), 2, axis=-1)
    sin = np.repeat(np.sin(ang), 2, axis=-1)
    even = (np.arange(RET_DK) % 2) == 0
    return tuple(jnp.asarray(t, F32) for t in (cos, np.where(even, -sin, 0.0), np.where(even, 0.0, sin)))


def _inproj(x, mod4, mod_row, norm_g, w_in_bf, conv_w, is_grid):
    bsz, length, _ = x.shape
    seg = GRID_W if is_grid else length
    assert TOKEN_TILE % seg == 0 and (length % TOKEN_TILE == 0 or TOKEN_TILE % length == 0)
    tokens = bsz * length
    tiles_per_seq = max(length // TOKEN_TILE, 1)
    seqs_per_tile = max(TOKEN_TILE // length, 1)
    batch_of = lambda i: (i // tiles_per_seq) * seqs_per_tile

    def mod_spec(which):
        return pl.BlockSpec((1, 1, 1, D_MODEL), lambda i: (mod_row(batch_of(i)), which, 0, 0))

    def tok_spec(width):
        return pl.BlockSpec((TOKEN_TILE, width), lambda i: (i, 0))

    in_specs = [
        tok_spec(D_MODEL), mod_spec(0), mod_spec(1),
        pl.BlockSpec((1, D_MODEL), lambda i: (0, 0)),
        pl.BlockSpec(w_in_bf.shape, lambda i: (0, 0)),
        pl.BlockSpec(conv_w.shape, lambda i: (0, 0)),
    ]
    args = [x.reshape(tokens, D_MODEL), mod4, mod4, norm_g, w_in_bf, conv_w]
    if is_grid:
        assert length % TOKEN_TILE == 0
        in_specs += [pl.BlockSpec((TOKEN_TILE, RET_DK), lambda i: (i % tiles_per_seq, 0))] * 3
        args += list(_rope_tables(length))
    shp = lambda w, dt: jax.ShapeDtypeStruct((tokens, w), dt)
    return pl.pallas_call(
        functools.partial(_inproj_kernel, seg, is_grid),
        grid=(tokens // TOKEN_TILE,),
        in_specs=in_specs,
        out_specs=[tok_spec(CONV_W), tok_spec(QK_W), tok_spec(QK_W), tok_spec(RET_W), tok_spec(RET_W)],
        out_shape=[shp(CONV_W, BF16), shp(QK_W, F32), shp(QK_W, F32), shp(RET_W, F32), shp(RET_W, F32)],
        compiler_params=pltpu.CompilerParams(
            dimension_semantics=("parallel",), vmem_limit_bytes=VMEM_LIMIT),
        name="inproj_grid" if is_grid else "inproj_seq",
    )(*args)


def _ret_kernel(n_chunks, heads, has_init, emit_state, a_ref, q_ref, k_ref, v_ref, g_ref, *rest):
    rest = list(rest)
    if has_init:
        sf0_ref, sb0_ref = rest[:2]
        rest = rest[2:]
    y_ref = rest.pop(0)
    if emit_state:
        sf_out, sb_out = rest[:2]
        rest = rest[2:]
    st_f, st_b, dec = rest
    c = CHUNK
    sq = (c, c)
    head0 = pl.program_id(0) * heads

    def log_decays(hh):
        lg_f = jnp.log1p(-jnp.exp(a_ref[pl.ds(head0 + hh, 1), :]))
        lg_b = jnp.log1p(-jnp.exp(a_ref[pl.ds(head0 + hh + RET_HEADS, 1), :]))
        return lg_f, lg_b

    @pl.when(pl.program_id(1) == 0)
    def _():
        row = lax.broadcasted_iota(jnp.int32, sq, 0).astype(F32)
        col = lax.broadcasted_iota(jnp.int32, sq, 1).astype(F32)
        scale = RET_DK ** -0.5
        for hh in range(heads):
            lg_f, lg_b = log_decays(hh)
            dec[hh, 0] = scale * (
                jnp.where(row >= col, jnp.exp(jnp.where(row >= col, row - col, 0.0) * lg_f), 0.0)
                + jnp.where(col >= row, jnp.exp(jnp.where(col >= row, col - row, 0.0) * lg_b), 0.0))
            dec[hh, 1] = jnp.exp((row + 1.0) * lg_f)
            dec[hh, 2] = jnp.exp((c - row) * lg_b)
            dec[hh, 3] = scale * jnp.exp((c - 1.0 - col) * lg_f)
            dec[hh, 4] = scale * jnp.exp(col * lg_b)

    def rows(n):
        return pl.ds(pl.multiple_of(n * c, c), c) if not isinstance(n, int) else pl.ds(n * c, c)

    def cols(hh):
        return slice(hh * RET_DK, (hh + 1) * RET_DK)

    def kv_step(hh, n):
        kt = jnp.transpose(k_ref[0, rows(n), cols(hh)])
        lhs = jnp.concatenate([kt * dec[hh, 3], kt * dec[hh, 4]], axis=0)
        kv = _bdot(lhs, v_ref[0, rows(n), cols(hh)])
        st_f[hh, n] = kv[:RET_DK]
        st_b[hh, n] = kv[RET_DK:]

    def scan(hh, st, decay, order, s):
        def step(i, s):
            n = order(i)
            kv = st[hh, n]
            st[hh, n] = s
            return s * decay + kv
        if n_chunks <= RET_UNROLL:
            for i in range(n_chunks):
                s = step(i, s)
            return s
        return lax.fori_loop(0, n_chunks, step, s, unroll=RET_UNROLL)

    def out_step(hh, n):
        q = q_ref[0, rows(n), cols(hh)]
        scores = lax.dot_general(q.astype(BF16), k_ref[0, rows(n), cols(hh)].astype(BF16),
                                 (((1,), (1,)), ((), ())), preferred_element_type=F32)
        o = _bdot(scores * dec[hh, 0], v_ref[0, rows(n), cols(hh)])
        q_dec = jnp.concatenate([q * dec[hh, 1], q * dec[hh, 2]], axis=1)
        o = o + _bdot(q_dec, jnp.concatenate([st_f[hh, n], st_b[hh, n]], axis=0))
        y = _silu(g_ref[0, rows(n), cols(hh)]) * _rms(o)
        y_ref[0, rows(n), cols(hh)] = y.astype(y_ref.dtype)

    def over_chunks(step):
        if n_chunks * heads <= RET_UNROLL:
            for hh in range(heads):
                for n in range(n_chunks):
                    step(hh, n)
        else:
            for hh in range(heads):
                lax.fori_loop(0, n_chunks, lambda n, carry: (step(hh, n), carry)[1], 0, unroll=RET_UNROLL)

    over_chunks(kv_step)
    finals = []
    for hh in range(heads):
        lg_f, lg_b = log_decays(hh)
        s_f = sf0_ref[0, 0, hh] if has_init else jnp.zeros(sq, F32)
        s_b = sb0_ref[0, 0, hh] if has_init else jnp.zeros(sq, F32)
        s_f = scan(hh, st_f, jnp.exp(c * lg_f), lambda i: i, s_f)
        s_b = scan(hh, st_b, jnp.exp(c * lg_b), lambda i: n_chunks - 1 - i, s_b)
        finals.append((s_f, s_b))
    over_chunks(out_step)
    if emit_state:
        for hh, (s_f, s_b) in enumerate(finals):
            sf_out[0, 0, hh] = s_f
            sb_out[0, 0, hh] = s_b


def _retention(q, k, v, g, decay_rows, s_f0, s_b0, emit_state):
    bsz, length, _ = q.shape
    n_chunks = length // CHUNK
    has_init = s_f0 is not None
    heads = RET_HEADS if n_chunks * RET_HEADS <= RET_UNROLL else 1
    head_spec = pl.BlockSpec((1, length, heads * RET_DK), lambda h, b: (b, 0, h))
    st_spec = pl.BlockSpec((1, 1, heads, RET_DK, RET_DV), lambda h, b: (b, 0, h, 0, 0))
    in_specs = [pl.BlockSpec(decay_rows.shape, lambda h, b: (0, 0))] + [head_spec] * 4
    args = [decay_rows, q, k, v, g]
    if has_init:
        in_specs += [st_spec, st_spec]
        args += [s_f0, s_b0]
    out_specs = [head_spec]
    out_shape = [jax.ShapeDtypeStruct((bsz, length, RET_W), BF16)]
    if emit_state:
        st_shape = jax.ShapeDtypeStruct((bsz, 1, RET_HEADS, RET_DK, RET_DV), F32)
        out_specs += [st_spec, st_spec]
        out_shape += [st_shape, st_shape]
    return pl.pallas_call(
        functools.partial(_ret_kernel, n_chunks, heads, has_init, emit_state),
        grid=(RET_HEADS // heads, bsz),
        in_specs=in_specs,
        out_specs=out_specs,
        out_shape=out_shape,
        scratch_shapes=[
            pltpu.VMEM((heads, n_chunks, RET_DK, RET_DV), F32),
            pltpu.VMEM((heads, n_chunks, RET_DK, RET_DV), F32),
            pltpu.VMEM((heads, 5, CHUNK, CHUNK), F32),
        ],
        compiler_params=pltpu.CompilerParams(
            dimension_semantics=("arbitrary", "arbitrary"), vmem_limit_bytes=SMALL_VMEM_LIMIT),
        name="retention_init" if has_init else "retention_zero",
    )(*args)


def _route(logits):
    lane = lax.broadcasted_iota(jnp.int32, logits.shape, 1)
    lane_f = lane.astype(F32)
    neg = -jnp.inf
    far = float(LANES)
    is_g = lane < N_GROUPS
    lg = jnp.where(is_g, logits, neg)
    g_max = jnp.max(lg, axis=1, keepdims=True)
    g_idx = jnp.min(jnp.where(lg == g_max, lane_f, far), axis=1, keepdims=True)
    p_sel = 1.0 / jnp.sum(jnp.where(is_g, jnp.exp(lg - g_max), 0.0), axis=1, keepdims=True)
    lane_group = ((lane - N_GROUPS) >> (EXPERTS_PER_GROUP.bit_length() - 1)).astype(F32)
    sel = (lane >= N_GROUPS) & (lane < N_GROUPS + N_EXPERTS) & (lane_group == g_idx)
    le = jnp.where(sel, logits, neg)
    v1 = jnp.max(le, axis=1, keepdims=True)
    i1 = jnp.min(jnp.where(le == v1, lane_f, far), axis=1, keepdims=True)
    le2 = jnp.where(lane_f == i1, neg, le)
    v2 = jnp.max(le2, axis=1, keepdims=True)
    i2 = jnp.min(jnp.where(le2 == v2, lane_f, far), axis=1, keepdims=True)
    e2 = jnp.exp(v2 - v1)
    w1 = p_sel * (1.0 / (1.0 + e2))
    w2 = p_sel * (e2 / (1.0 + e2))
    return lane, lane_f, g_idx, i1, i2, w1, w2


def _outproj_kernel(ctx_tiles, yc_c, yr_c, x_c, yc_l, yr_l, x_l, *rest):
    @pl.when(pl.program_id(0) < ctx_tiles)
    def _():
        _outproj_tile(yc_c, yr_c, x_c, *rest)

    @pl.when(pl.program_id(0) >= ctx_tiles)
    def _():
        _outproj_tile(yc_l, yr_l, x_l, *rest)


def _outproj_tile(yc_ref, yr_ref, x_ref, g1_ref, sh_ref, sc_ref, ng_ref, wo_ref, wr_ref, br_ref,
                  x1_ref, xloc_ref, route_ref, cnt_ref):
    m = (jnp.dot(yc_ref[...], wo_ref[0:CONV_W, :], preferred_element_type=F32)
         + jnp.dot(yr_ref[...], wo_ref[CONV_W:, :], preferred_element_type=F32))
    x1 = x_ref[...] + g1_ref[0, 0] * m
    x1_ref[...] = x1
    xn = (_rms(x1) * ng_ref[...]) * (1.0 + sc_ref[0, 0]) + sh_ref[0, 0]
    xb = xn.astype(BF16)
    logits = jnp.dot(xb, wr_ref[...], preferred_element_type=F32) + br_ref[...]
    lane, lane_f, g_idx, i1, i2, w1, w2 = _route(logits)

    picks = jnp.where(lane_f == g_idx, 1.0, 0.0)
    rows = picks.shape[0]
    tri = (lax.broadcasted_iota(jnp.int32, (rows, rows), 0)
           > lax.broadcasted_iota(jnp.int32, (rows, rows), 1))
    before = jnp.dot(jnp.where(tri, 1.0, 0.0).astype(BF16), picks.astype(BF16),
                     preferred_element_type=F32)
    count = jnp.sum(picks, axis=0, keepdims=True)
    cnt_ref[0] = count
    count8 = jnp.broadcast_to(jnp.floor((count + (SUBLANES - 1)) * (1.0 / SUBLANES)) * SUBLANES,
                              (SUBLANES, LANES))
    lane8 = lane[:SUBLANES]
    start = sum(jnp.where(lane8 >= k, pltpu.roll(count8, k, 1), 0.0) for k in range(1, N_GROUPS))
    local = jnp.sum(jnp.where(lane_f == g_idx, before + start[0:1], 0.0), axis=1, keepdims=True)
    route = jnp.where(lane == ROUTE_GROUP, g_idx, jnp.where(lane == ROUTE_LOCAL, local, jnp.where(
        lane == ROUTE_E1, i1 - N_GROUPS, jnp.where(lane == ROUTE_E2, i2 - N_GROUPS, jnp.where(
            lane == ROUTE_W1, w1, jnp.where(lane == ROUTE_W2, w2, 0.0))))))
    route_ref[...] = route

    local_row = jnp.transpose(jnp.broadcast_to(local, (rows, LANES)))[0:1, :]
    place = jnp.where(lax.broadcasted_iota(jnp.int32, (XLOC_ROWS, rows), 0).astype(F32) == local_row,
                      1.0, 0.0).astype(BF16)
    xloc_ref[0, :, :D_MODEL] = jnp.dot(place, xb, preferred_element_type=F32)
    xloc_ref[0, :, D_MODEL:] = sum(
        jnp.dot(place, piece, preferred_element_type=F32) for piece in _split3(route))


def _outproj(ctx, lat, mod4, mod_row_of_tile, norm_g, w_out_bf, w_router_bf, b_router):
    ctx_tiles = ctx[2].shape[0] // OUTPROJ_TILE
    tiles = ctx_tiles + lat[2].shape[0] // OUTPROJ_TILE
    tokens = tiles * OUTPROJ_TILE

    def mod_spec(which):
        return pl.BlockSpec((1, 1, 1, D_MODEL), lambda i: (mod_row_of_tile(i), which, 0, 0))

    ctx_tok = lambda w: pl.BlockSpec((OUTPROJ_TILE, w), lambda i: (jnp.minimum(i, ctx_tiles - 1), 0))
    lat_tok = lambda w: pl.BlockSpec((OUTPROJ_TILE, w), lambda i: (jnp.maximum(i - ctx_tiles, 0), 0))
    tok = lambda w: pl.BlockSpec((OUTPROJ_TILE, w), lambda i: (i, 0))
    full = lambda a: pl.BlockSpec(a.shape, lambda i: (0,) * a.ndim)
    widths = (CONV_W, RET_W, D_MODEL)
    return pl.pallas_call(
        functools.partial(_outproj_kernel, ctx_tiles),
        grid=(tiles,),
        in_specs=[ctx_tok(w) for w in widths] + [lat_tok(w) for w in widths] + [
            mod_spec(2), mod_spec(3), mod_spec(4),
            full(norm_g), full(w_out_bf), full(w_router_bf), full(b_router)],
        out_specs=[tok(D_MODEL),
                   pl.BlockSpec((1, XLOC_ROWS, ROW_W), lambda i: (i, 0, 0)),
                   tok(ROUTER_COLS),
                   pl.BlockSpec((1, 1, ROUTER_COLS), lambda i: (i, 0, 0))],
        out_shape=[jax.ShapeDtypeStruct((tokens, D_MODEL), F32),
                   jax.ShapeDtypeStruct((tiles, XLOC_ROWS, ROW_W), F32),
                   jax.ShapeDtypeStruct((tokens, ROUTER_COLS), F32),
                   jax.ShapeDtypeStruct((tiles, 1, ROUTER_COLS), F32)],
        compiler_params=pltpu.CompilerParams(
            dimension_semantics=("parallel",), vmem_limit_bytes=MID_VMEM_LIMIT),
        name="outproj",
    )(*ctx, *lat, mod4, mod4, mod4, norm_g, w_out_bf, w_router_bf, b_router)


RUN_PIECES = tuple(SUBLANES << b for b in reversed(range((OUTPROJ_TILE // SUBLANES).bit_length())))


def _expert_kernel(tile_group_ref, n_used_ref, first_ref, last_ref, fill_ref,
                   run_len_ref, run_src_ref, run_dst_ref,
                   xloc_hbm, w1_ref, w3_ref, w2_ref, ys_ref, xbuf, xb, gate_tabs, sem):
    j = pl.program_id(0)
    step = pl.program_id(1)
    n_used = n_used_ref[0]

    def tile_fetch(tile):
        slot = tile % 2
        group = tile_group_ref[tile]
        row0 = tile * GROUP_TILE

        def from_token_tile(b, carry):
            run = b * N_GROUPS + group
            lo = jnp.maximum(run_dst_ref[run], row0)
            hi = jnp.minimum(run_dst_ref[run] + run_len_ref[run], row0 + GROUP_TILE)
            n = jnp.maximum(hi - lo, 0)
            src = run_src_ref[run] + lo - run_dst_ref[run]
            dst = lo - row0
            for size in RUN_PIECES:
                done = n & (-2 * size)

                @pl.when((n & size) != 0)
                def _():
                    pltpu.make_async_copy(
                        xloc_hbm.at[b, pl.ds(pl.multiple_of(src + done, SUBLANES), size)],
                        xbuf.at[slot, pl.ds(pl.multiple_of(dst + done, SUBLANES), size)], sem.at[slot]).start()
            return carry

        lax.fori_loop(first_ref[tile], last_ref[tile] + 1, from_token_tile, 0)

    def tile_wait(tile):
        slot = tile % 2
        for size in (GROUP_TILE,) + tuple(GROUP_TILE >> k for k in range(1, (GROUP_TILE // SUBLANES).bit_length())):
            @pl.when((fill_ref[tile] & size) != 0)
            def _():
                pltpu.make_async_copy(xbuf.at[1 - slot, pl.ds(0, size)], xbuf.at[slot, pl.ds(0, size)],
                                      sem.at[slot]).wait()

    @pl.when(j < n_used)
    def _():
        @pl.when(step == 0)
        def _():
            @pl.when(j == 0)
            def _():
                xbuf[...] = jnp.zeros_like(xbuf)
                tile_fetch(j)

            tile_wait(j)
            rows_in = xbuf[j % 2]
            xb[...] = rows_in[:, :D_MODEL].astype(BF16)
            route = rows_in[:, D_MODEL:]
            lane = lax.broadcasted_iota(jnp.int32, route.shape, 1)
            for n, which in enumerate((ROUTE_E1, ROUTE_E2, ROUTE_W1, ROUTE_W2)):
                col = jnp.sum(jnp.where(lane == which, route, 0.0), axis=1, keepdims=True)
                gate_tabs[n] = jnp.broadcast_to(col, route.shape)

            @pl.when(j + 1 < n_used)
            def _():
                tile_fetch(j + 1)

        def evaluate(rows):
            x = xb[:rows]
            total = None
            for s in range(EXPERTS_PER_STEP):
                expert = (tile_group_ref[j] * EXPERTS_PER_GROUP + step * EXPERTS_PER_STEP + s).astype(F32)
                gate = (jnp.where(gate_tabs[0, :rows] == expert, gate_tabs[2, :rows], 0.0)
                        + jnp.where(gate_tabs[1, :rows] == expert, gate_tabs[3, :rows], 0.0))
                hid = _silu(jnp.dot(x, w1_ref[0, s].astype(BF16), preferred_element_type=F32)) * jnp.dot(
                    x, w3_ref[0, s].astype(BF16), preferred_element_type=F32)
                y = jnp.dot(hid.astype(BF16), w2_ref[0, s].astype(BF16), preferred_element_type=F32)
                gated = jnp.concatenate(
                    [gate * y[:, c * LANES:(c + 1) * LANES] for c in range(D_MODEL // LANES)], axis=1)
                total = gated if total is None else total + gated

            @pl.when(step == 0)
            def _():
                ys_ref[:rows] = total
                if rows < GROUP_TILE:
                    ys_ref[rows:] = jnp.zeros((GROUP_TILE - rows, D_MODEL), F32)

            @pl.when(step > 0)
            def _():
                ys_ref[:rows] += total

        half = GROUP_TILE // 2

        @pl.when(fill_ref[j] > half)
        def _():
            evaluate(GROUP_TILE)

        @pl.when(fill_ref[j] <= half)
        def _():
            evaluate(half)

    @pl.when((j >= n_used) & (step == 0))
    def _():
        ys_ref[...] = jnp.zeros_like(ys_ref)


def _experts(xloc, tile_tables, runs, slots, w1, w3, w2):
    steps = EXPERTS_PER_GROUP // EXPERTS_PER_STEP
    per_step = lambda w: w.reshape((N_EXPERTS // EXPERTS_PER_STEP, EXPERTS_PER_STEP) + w.shape[1:])
    w_spec = lambda shape: pl.BlockSpec((1, EXPERTS_PER_STEP) + shape, lambda j, s, tg, nu, *_: (
        tg[jnp.minimum(j, nu[0] - 1)] * steps + jnp.where(j < nu[0], s, steps - 1), 0, 0, 0))
    grid_spec = pltpu.PrefetchScalarGridSpec(
        num_scalar_prefetch=8,
        grid=(slots // GROUP_TILE, steps),
        in_specs=[
            pl.BlockSpec(memory_space=pl.ANY),
            w_spec((D_MODEL, D_EXPERT)), w_spec((D_MODEL, D_EXPERT)), w_spec((D_EXPERT, D_MODEL)),
        ],
        out_specs=pl.BlockSpec((GROUP_TILE, D_MODEL), lambda j, s, *_: (j, 0)),
        scratch_shapes=[pltpu.VMEM((2, GROUP_TILE, ROW_W), F32), pltpu.VMEM((GROUP_TILE, D_MODEL), BF16),
                        pltpu.VMEM((4, GROUP_TILE, LANES), F32), pltpu.SemaphoreType.DMA((2,))],
    )
    return pl.pallas_call(
        _expert_kernel,
        grid_spec=grid_spec,
        out_shape=jax.ShapeDtypeStruct((slots, D_MODEL), F32),
        compiler_params=pltpu.CompilerParams(
            dimension_semantics=("arbitrary", "arbitrary"), vmem_limit_bytes=EXPERT_VMEM_LIMIT),
        name="experts",
    )(*tile_tables, *runs, xloc, per_step(w1), per_step(w3), per_step(w2))


def _combine_kernel(n_tiles, tile_base, run_len_ref, run_src_ref, run_dst_ref,
                    route_ref, g2_ref, fg_ref, x1_hbm, ys_hbm, o_ref, buf, x1_buf, sem, x1_sem):
    i = pl.program_id(0)
    slot = i % COMBINE_RING

    def x1_copy(local_tile):
        s = local_tile % COMBINE_RING
        rows = pl.ds(pl.multiple_of((local_tile + tile_base) * OUTPROJ_TILE, OUTPROJ_TILE), OUTPROJ_TILE)
        return pltpu.make_async_copy(x1_hbm.at[rows], x1_buf.at[s], x1_sem.at[s])

    def run_copies(local_tile, act):
        s = local_tile % COMBINE_RING
        tile = local_tile + tile_base
        for g in range(N_GROUPS):
            n = run_len_ref[tile * N_GROUPS + g]
            src = run_src_ref[tile * N_GROUPS + g]
            dst = run_dst_ref[tile * N_GROUPS + g]
            for size in RUN_PIECES:
                done = n & (-2 * size)

                @pl.when((n & size) != 0)
                def _():
                    act(pltpu.make_async_copy(
                        ys_hbm.at[pl.ds(pl.multiple_of(dst + done, SUBLANES), size)],
                        buf.at[s, pl.ds(pl.multiple_of(src + done, SUBLANES), size)], sem.at[s]))

    def fetch(local_tile):
        run_copies(local_tile, lambda cp: cp.start())
        x1_copy(local_tile).start()

    @pl.when(i == 0)
    def _():
        buf[...] = jnp.zeros_like(buf)
        for ahead in range(min(COMBINE_RING - 1, n_tiles)):
            fetch(i + ahead)

    @pl.when(i + COMBINE_RING - 1 < n_tiles)
    def _():
        fetch(i + COMBINE_RING - 1)

    run_copies(i, lambda cp: cp.wait())
    x1_copy(i).wait()
    route = route_ref[...]
    lane = lax.broadcasted_iota(jnp.int32, route.shape, 1)
    local = jnp.sum(jnp.where(lane == ROUTE_LOCAL, route, 0.0), axis=1, keepdims=True)
    pick = jnp.where(lax.broadcasted_iota(jnp.int32, (route.shape[0], LOCAL_ROWS), 1).astype(F32) == local,
                     1.0, 0.0).astype(BF16)
    moe = sum(jnp.dot(pick, piece, preferred_element_type=F32) for piece in _split3(buf[slot])[:2])
    y = x1_buf[slot] + g2_ref[0, 0] * moe
    o_ref[...] = _rms(y) * fg_ref[...]


def _combine(ys, runs, x1, route, tokens, mod4, mod_row_of_tile, final_g, tile_base):
    tiles = tokens // OUTPROJ_TILE
    tok = lambda w: pl.BlockSpec((OUTPROJ_TILE, w), lambda i, *_: (i + tile_base, 0))
    grid_spec = pltpu.PrefetchScalarGridSpec(
        num_scalar_prefetch=3,
        grid=(tiles,),
        in_specs=[
            tok(ROUTER_COLS),
            pl.BlockSpec((1, 1, 1, D_MODEL), lambda i, *_: (mod_row_of_tile(i + tile_base), 5, 0, 0)),
            pl.BlockSpec((1, D_MODEL), lambda i, *_: (0, 0)),
            pl.BlockSpec(memory_space=pl.ANY),
            pl.BlockSpec(memory_space=pl.ANY),
        ],
        out_specs=pl.BlockSpec((OUTPROJ_TILE, D_MODEL), lambda i, *_: (i, 0)),
        scratch_shapes=[pltpu.VMEM((COMBINE_RING, LOCAL_ROWS, D_MODEL), F32),
                        pltpu.VMEM((COMBINE_RING, OUTPROJ_TILE, D_MODEL), F32),
                        pltpu.SemaphoreType.DMA((COMBINE_RING,)), pltpu.SemaphoreType.DMA((COMBINE_RING,))],
    )
    return pl.pallas_call(
        functools.partial(_combine_kernel, tiles, tile_base),
        grid_spec=grid_spec,
        out_shape=jax.ShapeDtypeStruct((tokens, D_MODEL), F32),
        compiler_params=pltpu.CompilerParams(
            dimension_semantics=("arbitrary",), vmem_limit_bytes=MID_VMEM_LIMIT),
        name="combine",
    )(*runs, route, mod4, final_g, x1, ys)


def _routing_tables(counts):
    counts = counts.astype(jnp.int32)
    tiles = counts.shape[0]
    run_len = ((counts + SUBLANES - 1) // SUBLANES) * SUBLANES
    run_src = jnp.cumsum(run_len, axis=1) - run_len
    group_rows = jnp.sum(run_len, axis=0)
    padded = ((group_rows + GROUP_TILE - 1) // GROUP_TILE) * GROUP_TILE
    ends = jnp.cumsum(padded)
    offs = ends - padded
    run_dst = offs[None, :] + jnp.cumsum(run_len, axis=0) - run_len

    n_used = ends[-1] // GROUP_TILE
    max_rows = tiles * (OUTPROJ_TILE + N_GROUPS * (SUBLANES - 1))
    max_tiles = max_rows // GROUP_TILE + N_GROUPS
    tile_ids = jnp.minimum(jnp.arange(max_tiles, dtype=jnp.int32), n_used - 1)
    tile_group = jnp.sum(tile_ids[:, None] * GROUP_TILE >= ends[None, :], axis=1).astype(jnp.int32)
    of_group = (tile_group[:, None] == jnp.arange(N_GROUPS, dtype=jnp.int32))[:, None, :]
    start = jnp.sum(jnp.where(of_group, run_dst[None], 0), axis=-1)
    stop = start + jnp.sum(jnp.where(of_group, run_len[None], 0), axis=-1)
    row0 = (tile_ids * GROUP_TILE)[:, None]
    first = jnp.sum(stop <= row0, axis=1).astype(jnp.int32)
    last = jnp.sum(start < row0 + GROUP_TILE, axis=1).astype(jnp.int32) - 1
    group_end = jnp.sum(jnp.where(of_group[:, 0, :], (offs + group_rows)[None, :], 0), axis=-1)
    fill = jnp.clip(group_end - row0[:, 0], 0, GROUP_TILE).astype(jnp.int32)
    flat = lambda a: a.reshape(-1)
    return ((tile_group, n_used.reshape(1), first, last, fill), (flat(run_len), flat(run_src), flat(run_dst)),
            max_tiles * GROUP_TILE)


def _mixer(x, mod4, mod_row, is_grid, s_f0, s_b0, p):
    norm_mix_g, w_in_bf, conv_w, decay_rows = p
    y_conv, *qkvg = _inproj(x, mod4, mod_row, norm_mix_g, w_in_bf, conv_w, is_grid)
    per_seq = lambda a: a.reshape(x.shape[0], x.shape[1], a.shape[-1])
    ret = _retention(*map(per_seq, qkvg), decay_rows, s_f0, s_b0, emit_state=not is_grid)
    return y_conv, ret[0].reshape(-1, RET_W), ret[1:]


def kernel(x_prompt, x_sample, state_ret_fwd, state_ret_bwd, c, c_ctx, norm_mix_g, norm_ffn_g, w_ada, b_ada, w_in, conv_w, ret_decay_fwd, ret_decay_bwd, w_out, w_router_group, b_router_group, w_router_expert, b_router_expert, w_gate_e, w_up_e, w_down_e, final_norm_g):
    assert norm_mix_g.shape[0] == 1, "single-layer backbone"
    n_lat = c.shape[0]
    ctx_row = n_lat
    mod_rows = 8
    cvec = jnp.concatenate([c, c_ctx[None, :], jnp.zeros((mod_rows - n_lat - 1, D_MODEL), F32)], axis=0)
    mod = _modulation(cvec, w_ada[0], b_ada[0][None, :])
    mod4 = mod.reshape(mod_rows, 6, 1, D_MODEL)

    pad = ROUTER_COLS - N_GROUPS - N_EXPERTS
    w_router = jnp.concatenate(
        [w_router_group[0], w_router_expert[0], jnp.zeros((D_MODEL, pad), F32)], axis=1).astype(BF16)
    b_router = jnp.concatenate([b_router_group[0], b_router_expert[0], jnp.zeros((pad,), F32)])[None, :]
    decay_rows = jnp.broadcast_to(
        jnp.concatenate([ret_decay_fwd[0], ret_decay_bwd[0]])[:, None], (2 * RET_HEADS, LANES))
    p_mix = (norm_mix_g, w_in[0].astype(BF16), conv_w[0], decay_rows)
    w_out_bf = w_out[0].astype(BF16)
    final_g = final_norm_g[None, :]

    ctx_tokens = x_prompt.shape[0] * x_prompt.shape[1]
    lat_tokens = x_sample.shape[0] * x_sample.shape[1]
    ctx_tiles = ctx_tokens // OUTPROJ_TILE
    lat_tiles_per_seq = x_sample.shape[1] // OUTPROJ_TILE
    tile_mod = lambda i: jnp.where(i < ctx_tiles, ctx_row, (i - ctx_tiles) // lat_tiles_per_seq)
    flat = lambda a: a.reshape(-1, a.shape[-1])

    yc_c, yr_c, (s_f, s_b) = _mixer(x_prompt, mod4, lambda b: ctx_row, False, None, None, p_mix)
    yc_l, yr_l, _ = _mixer(x_sample, mod4, lambda b: b, True, state_ret_fwd, state_ret_bwd, p_mix)

    x1, xloc, route, cnt = _outproj((yc_c, yr_c, flat(x_prompt)), (yc_l, yr_l, flat(x_sample)), mod4, tile_mod,
                                    norm_ffn_g, w_out_bf, w_router, b_router)
    tile_tables, runs, slots = _routing_tables(cnt[:, 0, :N_GROUPS])
    ys = _experts(xloc, tile_tables, runs, slots, w_gate_e[0], w_up_e[0], w_down_e[0])
    y_prompt = _combine(ys, runs, x1, route, ctx_tokens, mod4, tile_mod, final_g, 0)
    y_sample = _combine(ys, runs, x1, route, lat_tokens, mod4, tile_mod, final_g, ctx_tiles)
    return (y_prompt.reshape(x_prompt.shape), y_sample.reshape(x_sample.shape),
            s_f.astype(x_prompt.dtype), s_b.astype(x_prompt.dtype))
```

```python
import functools

import jax
import jax.numpy as jnp
import numpy as np
from jax import lax
from jax.experimental import pallas as pl
from jax.experimental.pallas import tpu as pltpu

F32 = jnp.float32
BF16 = jnp.bfloat16

D_MODEL = 1024
GRID_W = 64
CONV_W = 512
RET_HEADS = 4
RET_DK = 128
RET_DV = 128
RET_W = RET_HEADS * RET_DV
QK_W = RET_HEADS * RET_DK
CHUNK = 128
N_GROUPS = 4
EXPERTS_PER_GROUP = 8
N_EXPERTS = N_GROUPS * EXPERTS_PER_GROUP
D_EXPERT = 256
ROPE_BASE = 10000.0
EPS = 1e-6

LANES = 128
TOKEN_TILE = 1024
OUTPROJ_TILE = 512
GROUP_TILE = 1024
EXPERTS_PER_STEP = 4
RET_UNROLL = 8
COMBINE_RING = 3
SUBLANES = 8
XLOC_ROWS = OUTPROJ_TILE + N_GROUPS * SUBLANES
LOCAL_ROWS = OUTPROJ_TILE + LANES
ROW_W = D_MODEL + LANES
ROUTE_GROUP, ROUTE_LOCAL, ROUTE_E1, ROUTE_E2, ROUTE_W1, ROUTE_W2 = range(6)
MOD_COLS = 1536
ROUTER_COLS = LANES
VMEM_LIMIT = 56 * 1024 * 1024
EXPERT_VMEM_LIMIT = VMEM_LIMIT


def _silu(x):
    return x * jax.nn.sigmoid(x)


def _rms(x):
    return x * lax.rsqrt(jnp.mean(x * x, axis=-1, keepdims=True) + EPS)


def _bdot(a, b):
    return jnp.dot(a.astype(BF16), b.astype(BF16), preferred_element_type=F32)


def _split3(x):
    hi = x.astype(BF16)
    rest = x - hi.astype(F32)
    mid = rest.astype(BF16)
    return hi, mid, (rest - mid.astype(F32)).astype(BF16)


def _mod_kernel(c_ref, w_ref, b_ref, o_ref):
    o_ref[...] = _bdot(_silu(c_ref[...]), w_ref[...]) + b_ref[...]


def _modulation(cvec, w_ada, b_ada):
    rows = cvec.shape[0]
    n = w_ada.shape[1]
    return pl.pallas_call(
        _mod_kernel,
        grid=(n // MOD_COLS,),
        in_specs=[
            pl.BlockSpec((rows, D_MODEL), lambda j: (0, 0)),
            pl.BlockSpec((D_MODEL, MOD_COLS), lambda j: (0, j)),
            pl.BlockSpec((1, MOD_COLS), lambda j: (0, j)),
        ],
        out_specs=pl.BlockSpec((rows, MOD_COLS), lambda j: (0, j)),
        out_shape=jax.ShapeDtypeStruct((rows, n), F32),
        compiler_params=pltpu.CompilerParams(vmem_limit_bytes=VMEM_LIMIT),
        name="modulation",
    )(cvec, w_ada, b_ada)


def _inproj_kernel(seg, is_grid, x_ref, sh_ref, sc_ref, ng_ref, w_ref, cw_ref, *rest):
    if is_grid:
        cos_ref, sa_ref, sb_ref, yc_ref, q_ref, k_ref, v_ref, g_ref = rest
    else:
        yc_ref, q_ref, k_ref, v_ref, g_ref = rest
    x = x_ref[...]
    xn = (_rms(x) * ng_ref[...]) * (1.0 + sc_ref[0, 0]) + sh_ref[0, 0]
    xb = xn.astype(BF16)

    def proj(c0, n):
        return jnp.dot(xb, w_ref[:, c0:c0 + n], preferred_element_type=F32)

    gate_b = proj(0, CONV_W)
    u = proj(CONV_W, CONV_W) * proj(2 * CONV_W, CONV_W)
    rows = u.shape[0]
    pos = lax.broadcasted_iota(jnp.int32, u.shape, 0) & (seg - 1)
    u_prev = jnp.where(pos != 0, pltpu.roll(u, 1, 0), 0.0)
    u_next = jnp.where(pos != seg - 1, pltpu.roll(u, rows - 1, 0), 0.0)
    conv = cw_ref[0:1, :] * u_prev + cw_ref[1:2, :] * u + cw_ref[2:3, :] * u_next
    yc_ref[...] = (gate_b * conv).astype(yc_ref.dtype)

    q0 = 3 * CONV_W
    q = proj(q0, QK_W)
    k = proj(q0 + QK_W, QK_W)
    if is_grid:
        cos, sa, sb = cos_ref[...], sa_ref[...], sb_ref[...]

        def rope(t):
            out = []
            for h in range(RET_HEADS):
                th = t[:, h * RET_DK:(h + 1) * RET_DK]
                out.append(th * cos + pltpu.roll(th, RET_DK - 1, 1) * sa + pltpu.roll(th, 1, 1) * sb)
            return jnp.concatenate(out, axis=1)

        q, k = rope(q), rope(k)
    q_ref[...] = q
    k_ref[...] = k
    v_ref[...] = proj(q0 + 2 * QK_W, RET_W)
    g_ref[...] = proj(q0 + 2 * QK_W + RET_W, RET_W)


def _rope_tables(length):
    pos = np.arange(length)
    row = (pos // GRID_W).astype(np.float64)
    col = (pos % GRID_W).astype(np.float64)
    n_pairs = RET_DK // 4
    freqs = ROPE_BASE ** (-(np.arange(n_pairs, dtype=np.float64) * 2.0 / (RET_DK // 2)))
    ang = np.concatenate([row[:, None] * freqs, col[:, None] * freqs], axis=-1)
    cos = np.repeat(np.cos(ang), 2, axis=-1)
    sin = np.repeat(np.sin(ang), 2, axis=-1)
    even = (np.arange(RET_DK) % 2) == 0
    return tuple(jnp.asarray(t, F32) for t in (cos, np.where(even, -sin, 0.0), np.where(even, 0.0, sin)))


def _inproj(x, mod4, mod_row, norm_g, w_in_bf, conv_w, is_grid):
    bsz, length, _ = x.shape
    seg = GRID_W if is_grid else length
    assert TOKEN_TILE % seg == 0 and (length % TOKEN_TILE == 0 or TOKEN_TILE % length == 0)
    tokens = bsz * length
    tiles_per_seq = max(length // TOKEN_TILE, 1)
    seqs_per_tile = max(TOKEN_TILE // length, 1)
    batch_of = lambda i: (i // tiles_per_seq) * seqs_per_tile

    def mod_spec(which):
        return pl.BlockSpec((1, 1, 1, D_MODEL), lambda i: (mod_row(batch_of(i)), which, 0, 0))

    def tok_spec(width):
        return pl.BlockSpec((TOKEN_TILE, width), lambda i: (i, 0))

    in_specs = [
        tok_spec(D_MODEL), mod_spec(0), mod_spec(1),
        pl.BlockSpec((1, D_MODEL), lambda i: (0, 0)),
        pl.BlockSpec(w_in_bf.shape, lambda i: (0, 0)),
        pl.BlockSpec(conv_w.shape, lambda i: (0, 0)),
    ]
    args = [x.reshape(tokens, D_MODEL), mod4, mod4, norm_g, w_in_bf, conv_w]
    if is_grid:
        assert length % TOKEN_TILE == 0
        in_specs += [pl.BlockSpec((TOKEN_TILE, RET_DK), lambda i: (i % tiles_per_seq, 0))] * 3
        args += list(_rope_tables(length))
    shp = lambda w, dt: jax.ShapeDtypeStruct((tokens, w), dt)
    return pl.pallas_call(
        functools.partial(_inproj_kernel, seg, is_grid),
        grid=(tokens // TOKEN_TILE,),
        in_specs=in_specs,
        out_specs=[tok_spec(CONV_W), tok_spec(QK_W), tok_spec(QK_W), tok_spec(RET_W), tok_spec(RET_W)],
        out_shape=[shp(CONV_W, BF16), shp(QK_W, F32), shp(QK_W, F32), shp(RET_W, F32), shp(RET_W, F32)],
        compiler_params=pltpu.CompilerParams(
            dimension_semantics=("parallel",), vmem_limit_bytes=VMEM_LIMIT),
        name="inproj_grid" if is_grid else "inproj_seq",
    )(*args)


def _ret_kernel(n_chunks, heads, has_init, emit_state, a_ref, q_ref, k_ref, v_ref, g_ref, *rest):
    rest = list(rest)
    if has_init:
        sf0_ref, sb0_ref = rest[:2]
        rest = rest[2:]
    y_ref = rest.pop(0)
    if emit_state:
        sf_out, sb_out = rest[:2]
        rest = rest[2:]
    st_f, st_b, dec = rest
    c = CHUNK
    sq = (c, c)
    head0 = pl.program_id(0) * heads

    def log_decays(hh):
        lg_f = jnp.log1p(-jnp.exp(a_ref[pl.ds(head0 + hh, 1), :]))
        lg_b = jnp.log1p(-jnp.exp(a_ref[pl.ds(head0 + hh + RET_HEADS, 1), :]))
        return lg_f, lg_b

    @pl.when(pl.program_id(1) == 0)
    def _():
        row = lax.broadcasted_iota(jnp.int32, sq, 0).astype(F32)
        col = lax.broadcasted_iota(jnp.int32, sq, 1).astype(F32)
        scale = RET_DK ** -0.5
        for hh in range(heads):
            lg_f, lg_b = log_decays(hh)
            dec[hh, 0] = scale * (
                jnp.where(row >= col, jnp.exp(jnp.where(row >= col, row - col, 0.0) * lg_f), 0.0)
                + jnp.where(col >= row, jnp.exp(jnp.where(col >= row, col - row, 0.0) * lg_b), 0.0))
            dec[hh, 1] = jnp.exp((row + 1.0) * lg_f)
            dec[hh, 2] = jnp.exp((c - row) * lg_b)
            dec[hh, 3] = scale * jnp.exp((c - 1.0 - col) * lg_f)
            dec[hh, 4] = scale * jnp.exp(col * lg_b)

    def rows(n):
        return pl.ds(pl.multiple_of(n * c, c), c) if not isinstance(n, int) else pl.ds(n * c, c)

    def cols(hh):
        return slice(hh * RET_DK, (hh + 1) * RET_DK)

    def kv_step(hh, n):
        kt = jnp.transpose(k_ref[0, rows(n), cols(hh)])
        lhs = jnp.concatenate([kt * dec[hh, 3], kt * dec[hh, 4]], axis=0)
        kv = _bdot(lhs, v_ref[0, rows(n), cols(hh)])
        st_f[hh, n] = kv[:RET_DK]
        st_b[hh, n] = kv[RET_DK:]

    def scan(hh, st, decay, order, s):
        def step(i, s):
            n = order(i)
            kv = st[hh, n]
            st[hh, n] = s
            return s * decay + kv
        if n_chunks <= RET_UNROLL:
            for i in range(n_chunks):
                s = step(i, s)
            return s
        return lax.fori_loop(0, n_chunks, step, s, unroll=RET_UNROLL)

    def out_step(hh, n):
        q = q_ref[0, rows(n), cols(hh)]
        scores = lax.dot_general(q.astype(BF16), k_ref[0, rows(n), cols(hh)].astype(BF16),
                                 (((1,), (1,)), ((), ())), preferred_element_type=F32)
        o = _bdot(scores * dec[hh, 0], v_ref[0, rows(n), cols(hh)])
        q_dec = jnp.concatenate([q * dec[hh, 1], q * dec[hh, 2]], axis=1)
        o = o + _bdot(q_dec, jnp.concatenate([st_f[hh, n], st_b[hh, n]], axis=0))
        y = _silu(g_ref[0, rows(n), cols(hh)]) * _rms(o)
        y_ref[0, rows(n), cols(hh)] = y.astype(y_ref.dtype)

    def over_chunks(step):
        if n_chunks * heads <= RET_UNROLL:
            for hh in range(heads):
                for n in range(n_chunks):
                    step(hh, n)
        else:
            for hh in range(heads):
                lax.fori_loop(0, n_chunks, lambda n, carry: (step(hh, n), carry)[1], 0, unroll=RET_UNROLL)

    over_chunks(kv_step)
    finals = []
    for hh in range(heads):
        lg_f, lg_b = log_decays(hh)
        s_f = sf0_ref[0, 0, hh] if has_init else jnp.zeros(sq, F32)
        s_b = sb0_ref[0, 0, hh] if has_init else jnp.zeros(sq, F32)
        s_f = scan(hh, st_f, jnp.exp(c * lg_f), lambda i: i, s_f)
        s_b = scan(hh, st_b, jnp.exp(c * lg_b), lambda i: n_chunks - 1 - i, s_b)
        finals.append((s_f, s_b))
    over_chunks(out_step)
    if emit_state:
        for hh, (s_f, s_b) in enumerate(finals):
            sf_out[0, 0, hh] = s_f
            sb_out[0, 0, hh] = s_b


def _retention(q, k, v, g, decay_rows, s_f0, s_b0, emit_state):
    bsz, length, _ = q.shape
    n_chunks = length // CHUNK
    has_init = s_f0 is not None
    heads = RET_HEADS if n_chunks * RET_HEADS <= RET_UNROLL else 1
    head_spec = pl.BlockSpec((1, length, heads * RET_DK), lambda h, b: (b, 0, h))
    st_spec = pl.BlockSpec((1, 1, heads, RET_DK, RET_DV), lambda h, b: (b, 0, h, 0, 0))
    in_specs = [pl.BlockSpec(decay_rows.shape, lambda h, b: (0, 0))] + [head_spec] * 4
    args = [decay_rows, q, k, v, g]
    if has_init:
        in_specs += [st_spec, st_spec]
        args += [s_f0, s_b0]
    out_specs = [head_spec]
    out_shape = [jax.ShapeDtypeStruct((bsz, length, RET_W), BF16)]
    if emit_state:
        st_shape = jax.ShapeDtypeStruct((bsz, 1, RET_HEADS, RET_DK, RET_DV), F32)
        out_specs += [st_spec, st_spec]
        out_shape += [st_shape, st_shape]
    return pl.pallas_call(
        functools.partial(_ret_kernel, n_chunks, heads, has_init, emit_state),
        grid=(RET_HEADS // heads, bsz),
        in_specs=in_specs,
        out_specs=out_specs,
        out_shape=out_shape,
        scratch_shapes=[
            pltpu.VMEM((heads, n_chunks, RET_DK, RET_DV), F32),
            pltpu.VMEM((heads, n_chunks, RET_DK, RET_DV), F32),
            pltpu.VMEM((heads, 5, CHUNK, CHUNK), F32),
        ],
        compiler_params=pltpu.CompilerParams(
            dimension_semantics=("arbitrary", "arbitrary"), vmem_limit_bytes=VMEM_LIMIT),
        name="retention_init" if has_init else "retention_zero",
    )(*args)


def _route(logits):
    lane = lax.broadcasted_iota(jnp.int32, logits.shape, 1)
    lane_f = lane.astype(F32)
    neg = -jnp.inf
    far = float(LANES)
    is_g = lane < N_GROUPS
    lg = jnp.where(is_g, logits, neg)
    g_max = jnp.max(lg, axis=1, keepdims=True)
    g_idx = jnp.min(jnp.where(lg == g_max, lane_f, far), axis=1, keepdims=True)
    p_sel = 1.0 / jnp.sum(jnp.where(is_g, jnp.exp(lg - g_max), 0.0), axis=1, keepdims=True)
    lane_group = ((lane - N_GROUPS) >> (EXPERTS_PER_GROUP.bit_length() - 1)).astype(F32)
    sel = (lane >= N_GROUPS) & (lane < N_GROUPS + N_EXPERTS) & (lane_group == g_idx)
    le = jnp.where(sel, logits, neg)
    v1 = jnp.max(le, axis=1, keepdims=True)
    i1 = jnp.min(jnp.where(le == v1, lane_f, far), axis=1, keepdims=True)
    le2 = jnp.where(lane_f == i1, neg, le)
    v2 = jnp.max(le2, axis=1, keepdims=True)
    i2 = jnp.min(jnp.where(le2 == v2, lane_f, far), axis=1, keepdims=True)
    e2 = jnp.exp(v2 - v1)
    w1 = p_sel * (1.0 / (1.0 + e2))
    w2 = p_sel * (e2 / (1.0 + e2))
    return lane, lane_f, g_idx, i1, i2, w1, w2


def _outproj_kernel(ctx_tiles, yc_c, yr_c, x_c, yc_l, yr_l, x_l, *rest):
    @pl.when(pl.program_id(0) < ctx_tiles)
    def _():
        _outproj_tile(yc_c, yr_c, x_c, *rest)

    @pl.when(pl.program_id(0) >= ctx_tiles)
    def _():
        _outproj_tile(yc_l, yr_l, x_l, *rest)


def _outproj_tile(yc_ref, yr_ref, x_ref, g1_ref, sh_ref, sc_ref, ng_ref, wo_ref, wr_ref, br_ref,
                  x1_ref, xloc_ref, route_ref, cnt_ref):
    m = (jnp.dot(yc_ref[...], wo_ref[0:CONV_W, :], preferred_element_type=F32)
         + jnp.dot(yr_ref[...], wo_ref[CONV_W:, :], preferred_element_type=F32))
    x1 = x_ref[...] + g1_ref[0, 0] * m
    x1_ref[...] = x1
    xn = (_rms(x1) * ng_ref[...]) * (1.0 + sc_ref[0, 0]) + sh_ref[0, 0]
    xb = xn.astype(BF16)
    logits = jnp.dot(xb, wr_ref[...], preferred_element_type=F32) + br_ref[...]
    lane, lane_f, g_idx, i1, i2, w1, w2 = _route(logits)

    picks = jnp.where(lane_f == g_idx, 1.0, 0.0)
    rows = picks.shape[0]
    tri = (lax.broadcasted_iota(jnp.int32, (rows, rows), 0)
           > lax.broadcasted_iota(jnp.int32, (rows, rows), 1))
    before = jnp.dot(jnp.where(tri, 1.0, 0.0).astype(BF16), picks.astype(BF16),
                     preferred_element_type=F32)
    count = jnp.sum(picks, axis=0, keepdims=True)
    cnt_ref[0] = count
    count8 = jnp.broadcast_to(jnp.floor((count + (SUBLANES - 1)) * (1.0 / SUBLANES)) * SUBLANES,
                              (SUBLANES, LANES))
    lane8 = lane[:SUBLANES]
    start = sum(jnp.where(lane8 >= k, pltpu.roll(count8, k, 1), 0.0) for k in range(1, N_GROUPS))
    local = jnp.sum(jnp.where(lane_f == g_idx, before + start[0:1], 0.0), axis=1, keepdims=True)
    route = jnp.where(lane == ROUTE_GROUP, g_idx, jnp.where(lane == ROUTE_LOCAL, local, jnp.where(
        lane == ROUTE_E1, i1 - N_GROUPS, jnp.where(lane == ROUTE_E2, i2 - N_GROUPS, jnp.where(
            lane == ROUTE_W1, w1, jnp.where(lane == ROUTE_W2, w2, 0.0))))))
    route_ref[...] = route

    local_row = jnp.transpose(jnp.broadcast_to(local, (rows, LANES)))[0:1, :]
    place = jnp.where(lax.broadcasted_iota(jnp.int32, (XLOC_ROWS, rows), 0).astype(F32) == local_row,
                      1.0, 0.0).astype(BF16)
    xloc_ref[0, :, :D_MODEL] = jnp.dot(place, xb, preferred_element_type=F32)
    xloc_ref[0, :, D_MODEL:] = sum(
        jnp.dot(place, piece, preferred_element_type=F32) for piece in _split3(route))


def _outproj(ctx, lat, mod4, mod_row_of_tile, norm_g, w_out_bf, w_router_bf, b_router):
    ctx_tiles = ctx[2].shape[0] // OUTPROJ_TILE
    tiles = ctx_tiles + lat[2].shape[0] // OUTPROJ_TILE
    tokens = tiles * OUTPROJ_TILE

    def mod_spec(which):
        return pl.BlockSpec((1, 1, 1, D_MODEL), lambda i: (mod_row_of_tile(i), which, 0, 0))

    ctx_tok = lambda w: pl.BlockSpec((OUTPROJ_TILE, w), lambda i: (jnp.minimum(i, ctx_tiles - 1), 0))
    lat_tok = lambda w: pl.BlockSpec((OUTPROJ_TILE, w), lambda i: (jnp.maximum(i - ctx_tiles, 0), 0))
    tok = lambda w: pl.BlockSpec((OUTPROJ_TILE, w), lambda i: (i, 0))
    full = lambda a: pl.BlockSpec(a.shape, lambda i: (0,) * a.ndim)
    widths = (CONV_W, RET_W, D_MODEL)
    return pl.pallas_call(
        functools.partial(_outproj_kernel, ctx_tiles),
        grid=(tiles,),
        in_specs=[ctx_tok(w) for w in widths] + [lat_tok(w) for w in widths] + [
            mod_spec(2), mod_spec(3), mod_spec(4),
            full(norm_g), full(w_out_bf), full(w_router_bf), full(b_router)],
        out_specs=[tok(D_MODEL),
                   pl.BlockSpec((1, XLOC_ROWS, ROW_W), lambda i: (i, 0, 0)),
                   tok(ROUTER_COLS),
                   pl.BlockSpec((1, 1, ROUTER_COLS), lambda i: (i, 0, 0))],
        out_shape=[jax.ShapeDtypeStruct((tokens, D_MODEL), F32),
                   jax.ShapeDtypeStruct((tiles, XLOC_ROWS, ROW_W), F32),
                   jax.ShapeDtypeStruct((tokens, ROUTER_COLS), F32),
                   jax.ShapeDtypeStruct((tiles, 1, ROUTER_COLS), F32)],
        compiler_params=pltpu.CompilerParams(
            dimension_semantics=("parallel",), vmem_limit_bytes=VMEM_LIMIT),
        name="outproj",
    )(*ctx, *lat, mod4, mod4, mod4, norm_g, w_out_bf, w_router_bf, b_router)


RUN_PIECES = tuple(SUBLANES << b for b in reversed(range((OUTPROJ_TILE // SUBLANES).bit_length())))


def _expert_kernel(tile_group_ref, n_used_ref, first_ref, last_ref, fill_ref,
                   run_len_ref, run_src_ref, run_dst_ref,
                   xloc_hbm, w1_ref, w3_ref, w2_ref, ys_ref, xbuf, xb, gate_tabs, sem):
    j = pl.program_id(0)
    step = pl.program_id(1)
    n_used = n_used_ref[0]

    def tile_fetch(tile):
        slot = tile % 2
        group = tile_group_ref[tile]
        row0 = tile * GROUP_TILE

        def from_token_tile(b, carry):
            run = b * N_GROUPS + group
            lo = jnp.maximum(run_dst_ref[run], row0)
            hi = jnp.minimum(run_dst_ref[run] + run_len_ref[run], row0 + GROUP_TILE)
            n = jnp.maximum(hi - lo, 0)
            src = run_src_ref[run] + lo - run_dst_ref[run]
            dst = lo - row0
            for size in RUN_PIECES:
                done = n & (-2 * size)

                @pl.when((n & size) != 0)
                def _():
                    pltpu.make_async_copy(
                        xloc_hbm.at[b, pl.ds(pl.multiple_of(src + done, SUBLANES), size)],
                        xbuf.at[slot, pl.ds(pl.multiple_of(dst + done, SUBLANES), size)], sem.at[slot]).start()
            return carry

        lax.fori_loop(first_ref[tile], last_ref[tile] + 1, from_token_tile, 0)

    def tile_wait(tile):
        slot = tile % 2
        for size in (GROUP_TILE,) + tuple(GROUP_TILE >> k for k in range(1, (GROUP_TILE // SUBLANES).bit_length())):
            @pl.when((fill_ref[tile] & size) != 0)
            def _():
                pltpu.make_async_copy(xbuf.at[1 - slot, pl.ds(0, size)], xbuf.at[slot, pl.ds(0, size)],
                                      sem.at[slot]).wait()

    @pl.when(j < n_used)
    def _():
        @pl.when(step == 0)
        def _():
            @pl.when(j == 0)
            def _():
                xbuf[...] = jnp.zeros_like(xbuf)
                tile_fetch(j)

            tile_wait(j)
            rows_in = xbuf[j % 2]
            xb[...] = rows_in[:, :D_MODEL].astype(BF16)
            route = rows_in[:, D_MODEL:]
            lane = lax.broadcasted_iota(jnp.int32, route.shape, 1)
            for n, which in enumerate((ROUTE_E1, ROUTE_E2, ROUTE_W1, ROUTE_W2)):
                col = jnp.sum(jnp.where(lane == which, route, 0.0), axis=1, keepdims=True)
                gate_tabs[n] = jnp.broadcast_to(col, route.shape)

            @pl.when(j + 1 < n_used)
            def _():
                tile_fetch(j + 1)

        def evaluate(rows):
            x = xb[:rows]
            total = None
            for s in range(EXPERTS_PER_STEP):
                expert = (tile_group_ref[j] * EXPERTS_PER_GROUP + step * EXPERTS_PER_STEP + s).astype(F32)
                gate = (jnp.where(gate_tabs[0, :rows] == expert, gate_tabs[2, :rows], 0.0)
                        + jnp.where(gate_tabs[1, :rows] == expert, gate_tabs[3, :rows], 0.0))
                hid = _silu(jnp.dot(x, w1_ref[0, s].astype(BF16), preferred_element_type=F32)) * jnp.dot(
                    x, w3_ref[0, s].astype(BF16), preferred_element_type=F32)
                y = jnp.dot(hid.astype(BF16), w2_ref[0, s].astype(BF16), preferred_element_type=F32)
                gated = jnp.concatenate(
                    [gate * y[:, c * LANES:(c + 1) * LANES] for c in range(D_MODEL // LANES)], axis=1)
                total = gated if total is None else total + gated

            @pl.when(step == 0)
            def _():
                ys_ref[:rows] = total
                if rows < GROUP_TILE:
                    ys_ref[rows:] = jnp.zeros((GROUP_TILE - rows, D_MODEL), F32)

            @pl.when(step > 0)
            def _():
                ys_ref[:rows] += total

        half = GROUP_TILE // 2

        @pl.when(fill_ref[j] > half)
        def _():
            evaluate(GROUP_TILE)

        @pl.when(fill_ref[j] <= half)
        def _():
            evaluate(half)

    @pl.when((j >= n_used) & (step == 0))
    def _():
        ys_ref[...] = jnp.zeros_like(ys_ref)


def _experts(xloc, tile_tables, runs, slots, w1, w3, w2):
    steps = EXPERTS_PER_GROUP // EXPERTS_PER_STEP
    per_step = lambda w: w.reshape((N_EXPERTS // EXPERTS_PER_STEP, EXPERTS_PER_STEP) + w.shape[1:])
    w_spec = lambda shape: pl.BlockSpec((1, EXPERTS_PER_STEP) + shape, lambda j, s, tg, nu, *_: (
        tg[jnp.minimum(j, nu[0] - 1)] * steps + jnp.where(j < nu[0], s, steps - 1), 0, 0, 0))
    grid_spec = pltpu.PrefetchScalarGridSpec(
        num_scalar_prefetch=8,
        grid=(slots // GROUP_TILE, steps),
        in_specs=[
            pl.BlockSpec(memory_space=pl.ANY),
            w_spec((D_MODEL, D_EXPERT)), w_spec((D_MODEL, D_EXPERT)), w_spec((D_EXPERT, D_MODEL)),
        ],
        out_specs=pl.BlockSpec((GROUP_TILE, D_MODEL), lambda j, s, *_: (j, 0)),
        scratch_shapes=[pltpu.VMEM((2, GROUP_TILE, ROW_W), F32), pltpu.VMEM((GROUP_TILE, D_MODEL), BF16),
                        pltpu.VMEM((4, GROUP_TILE, LANES), F32), pltpu.SemaphoreType.DMA((2,))],
    )
    return pl.pallas_call(
        _expert_kernel,
        grid_spec=grid_spec,
        out_shape=jax.ShapeDtypeStruct((slots, D_MODEL), F32),
        compiler_params=pltpu.CompilerParams(
            dimension_semantics=("arbitrary", "arbitrary"), vmem_limit_bytes=EXPERT_VMEM_LIMIT),
        name="experts",
    )(*tile_tables, *runs, xloc, per_step(w1), per_step(w3), per_step(w2))


def _combine_kernel(n_tiles, tile_base, run_len_ref, run_src_ref, run_dst_ref,
                    route_ref, g2_ref, fg_ref, x1_hbm, ys_hbm, o_ref, buf, x1_buf, sem, x1_sem):
    i = pl.program_id(0)
    slot = i % COMBINE_RING

    def x1_copy(local_tile):
        s = local_tile % COMBINE_RING
        rows = pl.ds(pl.multiple_of((local_tile + tile_base) * OUTPROJ_TILE, OUTPROJ_TILE), OUTPROJ_TILE)
        return pltpu.make_async_copy(x1_hbm.at[rows], x1_buf.at[s], x1_sem.at[s])

    def run_copies(local_tile, act):
        s = local_tile % COMBINE_RING
        tile = local_tile + tile_base
        for g in range(N_GROUPS):
            n = run_len_ref[tile * N_GROUPS + g]
            src = run_src_ref[tile * N_GROUPS + g]
            dst = run_dst_ref[tile * N_GROUPS + g]
            for size in RUN_PIECES:
                done = n & (-2 * size)

                @pl.when((n & size) != 0)
                def _():
                    act(pltpu.make_async_copy(
                        ys_hbm.at[pl.ds(pl.multiple_of(dst + done, SUBLANES), size)],
                        buf.at[s, pl.ds(pl.multiple_of(src + done, SUBLANES), size)], sem.at[s]))

    def fetch(local_tile):
        run_copies(local_tile, lambda cp: cp.start())
        x1_copy(local_tile).start()

    @pl.when(i == 0)
    def _():
        buf[...] = jnp.zeros_like(buf)
        for ahead in range(min(COMBINE_RING - 1, n_tiles)):
            fetch(i + ahead)

    @pl.when(i + COMBINE_RING - 1 < n_tiles)
    def _():
        fetch(i + COMBINE_RING - 1)

    run_copies(i, lambda cp: cp.wait())
    x1_copy(i).wait()
    route = route_ref[...]
    lane = lax.broadcasted_iota(jnp.int32, route.shape, 1)
    local = jnp.sum(jnp.where(lane == ROUTE_LOCAL, route, 0.0), axis=1, keepdims=True)
    pick = jnp.where(lax.broadcasted_iota(jnp.int32, (route.shape[0], LOCAL_ROWS), 1).astype(F32) == local,
                     1.0, 0.0).astype(BF16)
    moe = sum(jnp.dot(pick, piece, preferred_element_type=F32) for piece in _split3(buf[slot])[:2])
    y = x1_buf[slot] + g2_ref[0, 0] * moe
    o_ref[...] = _rms(y) * fg_ref[...]


def _combine(ys, runs, x1, route, tokens, mod4, mod_row_of_tile, final_g, tile_base):
    tiles = tokens // OUTPROJ_TILE
    tok = lambda w: pl.BlockSpec((OUTPROJ_TILE, w), lambda i, *_: (i + tile_base, 0))
    grid_spec = pltpu.PrefetchScalarGridSpec(
        num_scalar_prefetch=3,
        grid=(tiles,),
        in_specs=[
            tok(ROUTER_COLS),
            pl.BlockSpec((1, 1, 1, D_MODEL), lambda i, *_: (mod_row_of_tile(i + tile_base), 5, 0, 0)),
            pl.BlockSpec((1, D_MODEL), lambda i, *_: (0, 0)),
            pl.BlockSpec(memory_space=pl.ANY),
            pl.BlockSpec(memory_space=pl.ANY),
        ],
        out_specs=pl.BlockSpec((OUTPROJ_TILE, D_MODEL), lambda i, *_: (i, 0)),
        scratch_shapes=[pltpu.VMEM((COMBINE_RING, LOCAL_ROWS, D_MODEL), F32),
                        pltpu.VMEM((COMBINE_RING, OUTPROJ_TILE, D_MODEL), F32),
                        pltpu.SemaphoreType.DMA((COMBINE_RING,)), pltpu.SemaphoreType.DMA((COMBINE_RING,))],
    )
    return pl.pallas_call(
        functools.partial(_combine_kernel, tiles, tile_base),
        grid_spec=grid_spec,
        out_shape=jax.ShapeDtypeStruct((tokens, D_MODEL), F32),
        compiler_params=pltpu.CompilerParams(
            dimension_semantics=("arbitrary",), vmem_limit_bytes=VMEM_LIMIT),
        name="combine",
    )(*runs, route, mod4, final_g, x1, ys)


def _routing_tables(counts):
    counts = counts.astype(jnp.int32)
    tiles = counts.shape[0]
    run_len = ((counts + SUBLANES - 1) // SUBLANES) * SUBLANES
    run_src = jnp.cumsum(run_len, axis=1) - run_len
    group_rows = jnp.sum(run_len, axis=0)
    padded = ((group_rows + GROUP_TILE - 1) // GROUP_TILE) * GROUP_TILE
    ends = jnp.cumsum(padded)
    offs = ends - padded
    run_dst = offs[None, :] + jnp.cumsum(run_len, axis=0) - run_len

    n_used = ends[-1] // GROUP_TILE
    max_rows = tiles * (OUTPROJ_TILE + N_GROUPS * (SUBLANES - 1))
    max_tiles = max_rows // GROUP_TILE + N_GROUPS
    tile_ids = jnp.minimum(jnp.arange(max_tiles, dtype=jnp.int32), n_used - 1)
    tile_group = jnp.sum(tile_ids[:, None] * GROUP_TILE >= ends[None, :], axis=1).astype(jnp.int32)
    of_group = (tile_group[:, None] == jnp.arange(N_GROUPS, dtype=jnp.int32))[:, None, :]
    start = jnp.sum(jnp.where(of_group, run_dst[None], 0), axis=-1)
    stop = start + jnp.sum(jnp.where(of_group, run_len[None], 0), axis=-1)
    row0 = (tile_ids * GROUP_TILE)[:, None]
    first = jnp.sum(stop <= row0, axis=1).astype(jnp.int32)
    last = jnp.sum(start < row0 + GROUP_TILE, axis=1).astype(jnp.int32) - 1
    group_end = jnp.sum(jnp.where(of_group[:, 0, :], (offs + group_rows)[None, :], 0), axis=-1)
    fill = jnp.clip(group_end - row0[:, 0], 0, GROUP_TILE).astype(jnp.int32)
    flat = lambda a: a.reshape(-1)
    return ((tile_group, n_used.reshape(1), first, last, fill), (flat(run_len), flat(run_src), flat(run_dst)),
            max_tiles * GROUP_TILE)


def _mixer(x, mod4, mod_row, is_grid, s_f0, s_b0, p):
    norm_mix_g, w_in_bf, conv_w, decay_rows = p
    y_conv, *qkvg = _inproj(x, mod4, mod_row, norm_mix_g, w_in_bf, conv_w, is_grid)
    per_seq = lambda a: a.reshape(x.shape[0], x.shape[1], a.shape[-1])
    ret = _retention(*map(per_seq, qkvg), decay_rows, s_f0, s_b0, emit_state=not is_grid)
    return y_conv, ret[0].reshape(-1, RET_W), ret[1:]


def kernel(x_prompt, x_sample, state_ret_fwd, state_ret_bwd, c, c_ctx, norm_mix_g, norm_ffn_g, w_ada, b_ada, w_in, conv_w, ret_decay_fwd, ret_decay_bwd, w_out, w_router_group, b_router_group, w_router_expert, b_router_expert, w_gate_e, w_up_e, w_down_e, final_norm_g):
    assert norm_mix_g.shape[0] == 1, "single-layer backbone"
    n_lat = c.shape[0]
    ctx_row = n_lat
    mod_rows = 8
    cvec = jnp.concatenate([c, c_ctx[None, :], jnp.zeros((mod_rows - n_lat - 1, D_MODEL), F32)], axis=0)
    mod = _modulation(cvec, w_ada[0], b_ada[0][None, :])
    mod4 = mod.reshape(mod_rows, 6, 1, D_MODEL)

    pad = ROUTER_COLS - N_GROUPS - N_EXPERTS
    w_router = jnp.concatenate(
        [w_router_group[0], w_router_expert[0], jnp.zeros((D_MODEL, pad), F32)], axis=1).astype(BF16)
    b_router = jnp.concatenate([b_router_group[0], b_router_expert[0], jnp.zeros((pad,), F32)])[None, :]
    decay_rows = jnp.broadcast_to(
        jnp.concatenate([ret_decay_fwd[0], ret_decay_bwd[0]])[:, None], (2 * RET_HEADS, LANES))
    p_mix = (norm_mix_g, w_in[0].astype(BF16), conv_w[0], decay_rows)
    w_out_bf = w_out[0].astype(BF16)
    final_g = final_norm_g[None, :]

    ctx_tokens = x_prompt.shape[0] * x_prompt.shape[1]
    lat_tokens = x_sample.shape[0] * x_sample.shape[1]
    ctx_tiles = ctx_tokens // OUTPROJ_TILE
    lat_tiles_per_seq = x_sample.shape[1] // OUTPROJ_TILE
    tile_mod = lambda i: jnp.where(i < ctx_tiles, ctx_row, (i - ctx_tiles) // lat_tiles_per_seq)
    flat = lambda a: a.reshape(-1, a.shape[-1])

    yc_c, yr_c, (s_f, s_b) = _mixer(x_prompt, mod4, lambda b: ctx_row, False, None, None, p_mix)
    yc_l, yr_l, _ = _mixer(x_sample, mod4, lambda b: b, True, state_ret_fwd, state_ret_bwd, p_mix)

    x1, xloc, route, cnt = _outproj((yc_c, yr_c, flat(x_prompt)), (yc_l, yr_l, flat(x_sample)), mod4, tile_mod,
                                    norm_ffn_g, w_out_bf, w_router, b_router)
    tile_tables, runs, slots = _routing_tables(cnt[:, 0, :N_GROUPS])
    ys = _experts(xloc, tile_tables, runs, slots, w_gate_e[0], w_up_e[0], w_down_e[0])
    y_prompt = _combine(ys, runs, x1, route, ctx_tokens, mod4, tile_mod, final_g, 0)
    y_sample = _combine(ys, runs, x1, route, lat_tokens, mod4, tile_mod, final_g, ctx_tiles)
    return (y_prompt.reshape(x_prompt.shape), y_sample.reshape(x_sample.shape),
            s_f.astype(x_prompt.dtype), s_b.astype(x_prompt.dtype))
```

```python
import functools

import jax
import jax.numpy as jnp
import numpy as np
from jax import lax
from jax.experimental import pallas as pl
from jax.experimental.pallas import tpu as pltpu

F32 = jnp.float32
BF16 = jnp.bfloat16

D_MODEL = 1024
GRID_W = 64
CONV_W = 512
RET_HEADS = 4
RET_DK = 128
RET_DV = 128
RET_W = RET_HEADS * RET_DV
QK_W = RET_HEADS * RET_DK
CHUNK = 128
N_GROUPS = 4
EXPERTS_PER_GROUP = 8
N_EXPERTS = N_GROUPS * EXPERTS_PER_GROUP
D_EXPERT = 256
ROPE_BASE = 10000.0
EPS = 1e-6

LANES = 128
TOKEN_TILE = 1024
OUTPROJ_TILE = 512
GROUP_TILE = 1024
EXPERTS_PER_STEP = 4
RET_UNROLL = 8
COMBINE_RING = 3
SUBLANES = 8
XLOC_ROWS = OUTPROJ_TILE + N_GROUPS * SUBLANES
LOCAL_ROWS = OUTPROJ_TILE + LANES
ROW_W = D_MODEL + LANES
ROUTE_GROUP, ROUTE_LOCAL, ROUTE_E1, ROUTE_E2, ROUTE_W1, ROUTE_W2 = range(6)
MOD_COLS = 1536
ROUTER_COLS = LANES
VMEM_LIMIT = 56 * 1024 * 1024


def _silu(x):
    return x * jax.nn.sigmoid(x)


def _rms(x):
    return x * lax.rsqrt(jnp.mean(x * x, axis=-1, keepdims=True) + EPS)


def _bdot(a, b):
    return jnp.dot(a.astype(BF16), b.astype(BF16), preferred_element_type=F32)


def _split3(x):
    hi = x.astype(BF16)
    rest = x - hi.astype(F32)
    mid = rest.astype(BF16)
    return hi, mid, (rest - mid.astype(F32)).astype(BF16)


def _mod_kernel(c_ref, w_ref, b_ref, o_ref):
    o_ref[...] = _bdot(_silu(c_ref[...]), w_ref[...]) + b_ref[...]


def _modulation(cvec, w_ada, b_ada):
    rows = cvec.shape[0]
    n = w_ada.shape[1]
    return pl.pallas_call(
        _mod_kernel,
        grid=(n // MOD_COLS,),
        in_specs=[
            pl.BlockSpec((rows, D_MODEL), lambda j: (0, 0)),
            pl.BlockSpec((D_MODEL, MOD_COLS), lambda j: (0, j)),
            pl.BlockSpec((1, MOD_COLS), lambda j: (0, j)),
        ],
        out_specs=pl.BlockSpec((rows, MOD_COLS), lambda j: (0, j)),
        out_shape=jax.ShapeDtypeStruct((rows, n), F32),
        compiler_params=pltpu.CompilerParams(vmem_limit_bytes=VMEM_LIMIT),
        name="modulation",
    )(cvec, w_ada, b_ada)


def _inproj_kernel(seg, is_grid, x_ref, sh_ref, sc_ref, ng_ref, w_ref, cw_ref, *rest):
    if is_grid:
        cos_ref, sa_ref, sb_ref, yc_ref, q_ref, k_ref, v_ref, g_ref = rest
    else:
        yc_ref, q_ref, k_ref, v_ref, g_ref = rest
    x = x_ref[...]
    xn = (_rms(x) * ng_ref[...]) * (1.0 + sc_ref[0, 0]) + sh_ref[0, 0]
    xb = xn.astype(BF16)

    def proj(c0, n):
        return jnp.dot(xb, w_ref[:, c0:c0 + n], preferred_element_type=F32)

    gate_b = proj(0, CONV_W)
    u = proj(CONV_W, CONV_W) * proj(2 * CONV_W, CONV_W)
    rows = u.shape[0]
    pos = lax.broadcasted_iota(jnp.int32, u.shape, 0) & (seg - 1)
    u_prev = jnp.where(pos != 0, pltpu.roll(u, 1, 0), 0.0)
    u_next = jnp.where(pos != seg - 1, pltpu.roll(u, rows - 1, 0), 0.0)
    conv = cw_ref[0:1, :] * u_prev + cw_ref[1:2, :] * u + cw_ref[2:3, :] * u_next
    yc_ref[...] = (gate_b * conv).astype(yc_ref.dtype)

    q0 = 3 * CONV_W
    q = proj(q0, QK_W)
    k = proj(q0 + QK_W, QK_W)
    if is_grid:
        cos, sa, sb = cos_ref[...], sa_ref[...], sb_ref[...]

        def rope(t):
            out = []
            for h in range(RET_HEADS):
                th = t[:, h * RET_DK:(h + 1) * RET_DK]
                out.append(th * cos + pltpu.roll(th, RET_DK - 1, 1) * sa + pltpu.roll(th, 1, 1) * sb)
            return jnp.concatenate(out, axis=1)

        q, k = rope(q), rope(k)
    q_ref[...] = q
    k_ref[...] = k
    v_ref[...] = proj(q0 + 2 * QK_W, RET_W)
    g_ref[...] = proj(q0 + 2 * QK_W + RET_W, RET_W)


def _rope_tables(length):
    pos = np.arange(length)
    row = (pos // GRID_W).astype(np.float64)
    col = (pos % GRID_W).astype(np.float64)
    n_pairs = RET_DK // 4
    freqs = ROPE_BASE ** (-(np.arange(n_pairs, dtype=np.float64) * 2.0 / (RET_DK // 2)))
    ang = np.concatenate([row[:, None] * freqs, col[:, None] * freqs], axis=-1)
    cos = np.repeat(np.cos(ang), 2, axis=-1)
    sin = np.repeat(np.sin(ang), 2, axis=-1)
    even = (np.arange(RET_DK) % 2) == 0
    return tuple(jnp.asarray(t, F32) for t in (cos, np.where(even, -sin, 0.0), np.where(even, 0.0, sin)))


def _inproj(x, mod4, mod_row, norm_g, w_in_bf, conv_w, is_grid):
    bsz, length, _ = x.shape
    seg = GRID_W if is_grid else length
    assert TOKEN_TILE % seg == 0 and (length % TOKEN_TILE == 0 or TOKEN_TILE % length == 0)
    tokens = bsz * length
    tiles_per_seq = max(length // TOKEN_TILE, 1)
    seqs_per_tile = max(TOKEN_TILE // length, 1)
    batch_of = lambda i: (i // tiles_per_seq) * seqs_per_tile

    def mod_spec(which):
        return pl.BlockSpec((1, 1, 1, D_MODEL), lambda i: (mod_row(batch_of(i)), which, 0, 0))

    def tok_spec(width):
        return pl.BlockSpec((TOKEN_TILE, width), lambda i: (i, 0))

    in_specs = [
        tok_spec(D_MODEL), mod_spec(0), mod_spec(1),
        pl.BlockSpec((1, D_MODEL), lambda i: (0, 0)),
        pl.BlockSpec(w_in_bf.shape, lambda i: (0, 0)),
        pl.BlockSpec(conv_w.shape, lambda i: (0, 0)),
    ]
    args = [x.reshape(tokens, D_MODEL), mod4, mod4, norm_g, w_in_bf, conv_w]
    if is_grid:
        assert length % TOKEN_TILE == 0
        in_specs += [pl.BlockSpec((TOKEN_TILE, RET_DK), lambda i: (i % tiles_per_seq, 0))] * 3
        args += list(_rope_tables(length))
    shp = lambda w, dt: jax.ShapeDtypeStruct((tokens, w), dt)
    return pl.pallas_call(
        functools.partial(_inproj_kernel, seg, is_grid),
        grid=(tokens // TOKEN_TILE,),
        in_specs=in_specs,
        out_specs=[tok_spec(CONV_W), tok_spec(QK_W), tok_spec(QK_W), tok_spec(RET_W), tok_spec(RET_W)],
        out_shape=[shp(CONV_W, BF16), shp(QK_W, F32), shp(QK_W, F32), shp(RET_W, F32), shp(RET_W, F32)],
        compiler_params=pltpu.CompilerParams(
            dimension_semantics=("parallel",), vmem_limit_bytes=VMEM_LIMIT),
        name="inproj_grid" if is_grid else "inproj_seq",
    )(*args)


def _ret_kernel(n_chunks, heads, has_init, emit_state, a_ref, q_ref, k_ref, v_ref, g_ref, *rest):
    rest = list(rest)
    if has_init:
        sf0_ref, sb0_ref = rest[:2]
        rest = rest[2:]
    y_ref = rest.pop(0)
    if emit_state:
        sf_out, sb_out = rest[:2]
        rest = rest[2:]
    st_f, st_b, dec = rest
    c = CHUNK
    sq = (c, c)
    head0 = pl.program_id(0) * heads

    def log_decays(hh):
        lg_f = jnp.log1p(-jnp.exp(a_ref[pl.ds(head0 + hh, 1), :]))
        lg_b = jnp.log1p(-jnp.exp(a_ref[pl.ds(head0 + hh + RET_HEADS, 1), :]))
        return lg_f, lg_b

    @pl.when(pl.program_id(1) == 0)
    def _():
        row = lax.broadcasted_iota(jnp.int32, sq, 0).astype(F32)
        col = lax.broadcasted_iota(jnp.int32, sq, 1).astype(F32)
        scale = RET_DK ** -0.5
        for hh in range(heads):
            lg_f, lg_b = log_decays(hh)
            dec[hh, 0] = scale * (
                jnp.where(row >= col, jnp.exp(jnp.where(row >= col, row - col, 0.0) * lg_f), 0.0)
                + jnp.where(col >= row, jnp.exp(jnp.where(col >= row, col - row, 0.0) * lg_b), 0.0))
            dec[hh, 1] = jnp.exp((row + 1.0) * lg_f)
            dec[hh, 2] = jnp.exp((c - row) * lg_b)
            dec[hh, 3] = scale * jnp.exp((c - 1.0 - col) * lg_f)
            dec[hh, 4] = scale * jnp.exp(col * lg_b)

    def rows(n):
        return pl.ds(pl.multiple_of(n * c, c), c) if not isinstance(n, int) else pl.ds(n * c, c)

    def cols(hh):
        return slice(hh * RET_DK, (hh + 1) * RET_DK)

    def kv_step(hh, n):
        kt = jnp.transpose(k_ref[0, rows(n), cols(hh)])
        lhs = jnp.concatenate([kt * dec[hh, 3], kt * dec[hh, 4]], axis=0)
        kv = _bdot(lhs, v_ref[0, rows(n), cols(hh)])
        st_f[hh, n] = kv[:RET_DK]
        st_b[hh, n] = kv[RET_DK:]

    def scan(hh, st, decay, order, s):
        def step(i, s):
            n = order(i)
            kv = st[hh, n]
            st[hh, n] = s
            return s * decay + kv
        if n_chunks <= RET_UNROLL:
            for i in range(n_chunks):
                s = step(i, s)
            return s
        return lax.fori_loop(0, n_chunks, step, s, unroll=RET_UNROLL)

    def out_step(hh, n):
        q = q_ref[0, rows(n), cols(hh)]
        scores = lax.dot_general(q.astype(BF16), k_ref[0, rows(n), cols(hh)].astype(BF16),
                                 (((1,), (1,)), ((), ())), preferred_element_type=F32)
        o = _bdot(scores * dec[hh, 0], v_ref[0, rows(n), cols(hh)])
        q_dec = jnp.concatenate([q * dec[hh, 1], q * dec[hh, 2]], axis=1)
        o = o + _bdot(q_dec, jnp.concatenate([st_f[hh, n], st_b[hh, n]], axis=0))
        y = _silu(g_ref[0, rows(n), cols(hh)]) * _rms(o)
        y_ref[0, rows(n), cols(hh)] = y.astype(y_ref.dtype)

    def over_chunks(step):
        if n_chunks * heads <= RET_UNROLL:
            for hh in range(heads):
                for n in range(n_chunks):
                    step(hh, n)
        else:
            for hh in range(heads):
                lax.fori_loop(0, n_chunks, lambda n, carry: (step(hh, n), carry)[1], 0, unroll=RET_UNROLL)

    over_chunks(kv_step)
    finals = []
    for hh in range(heads):
        lg_f, lg_b = log_decays(hh)
        s_f = sf0_ref[0, 0, hh] if has_init else jnp.zeros(sq, F32)
        s_b = sb0_ref[0, 0, hh] if has_init else jnp.zeros(sq, F32)
        s_f = scan(hh, st_f, jnp.exp(c * lg_f), lambda i: i, s_f)
        s_b = scan(hh, st_b, jnp.exp(c * lg_b), lambda i: n_chunks - 1 - i, s_b)
        finals.append((s_f, s_b))
    over_chunks(out_step)
    if emit_state:
        for hh, (s_f, s_b) in enumerate(finals):
            sf_out[0, 0, hh] = s_f
            sb_out[0, 0, hh] = s_b


def _retention(q, k, v, g, decay_rows, s_f0, s_b0, emit_state):
    bsz, length, _ = q.shape
    n_chunks = length // CHUNK
    has_init = s_f0 is not None
    heads = RET_HEADS if n_chunks * RET_HEADS <= RET_UNROLL else 1
    head_spec = pl.BlockSpec((1, length, heads * RET_DK), lambda h, b: (b, 0, h))
    st_spec = pl.BlockSpec((1, 1, heads, RET_DK, RET_DV), lambda h, b: (b, 0, h, 0, 0))
    in_specs = [pl.BlockSpec(decay_rows.shape, lambda h, b: (0, 0))] + [head_spec] * 4
    args = [decay_rows, q, k, v, g]
    if has_init:
        in_specs += [st_spec, st_spec]
        args += [s_f0, s_b0]
    out_specs = [head_spec]
    out_shape = [jax.ShapeDtypeStruct((bsz, length, RET_W), BF16)]
    if emit_state:
        st_shape = jax.ShapeDtypeStruct((bsz, 1, RET_HEADS, RET_DK, RET_DV), F32)
        out_specs += [st_spec, st_spec]
        out_shape += [st_shape, st_shape]
    return pl.pallas_call(
        functools.partial(_ret_kernel, n_chunks, heads, has_init, emit_state),
        grid=(RET_HEADS // heads, bsz),
        in_specs=in_specs,
        out_specs=out_specs,
        out_shape=out_shape,
        scratch_shapes=[
            pltpu.VMEM((heads, n_chunks, RET_DK, RET_DV), F32),
            pltpu.VMEM((heads, n_chunks, RET_DK, RET_DV), F32),
            pltpu.VMEM((heads, 5, CHUNK, CHUNK), F32),
        ],
        compiler_params=pltpu.CompilerParams(
            dimension_semantics=("arbitrary", "arbitrary"), vmem_limit_bytes=VMEM_LIMIT),
        name="retention_init" if has_init else "retention_zero",
    )(*args)


def _route(logits):
    lane = lax.broadcasted_iota(jnp.int32, logits.shape, 1)
    lane_f = lane.astype(F32)
    neg = -jnp.inf
    far = float(LANES)
    is_g = lane < N_GROUPS
    lg = jnp.where(is_g, logits, neg)
    g_max = jnp.max(lg, axis=1, keepdims=True)
    g_idx = jnp.min(jnp.where(lg == g_max, lane_f, far), axis=1, keepdims=True)
    p_sel = 1.0 / jnp.sum(jnp.where(is_g, jnp.exp(lg - g_max), 0.0), axis=1, keepdims=True)
    lane_group = ((lane - N_GROUPS) >> (EXPERTS_PER_GROUP.bit_length() - 1)).astype(F32)
    sel = (lane >= N_GROUPS) & (lane < N_GROUPS + N_EXPERTS) & (lane_group == g_idx)
    le = jnp.where(sel, logits, neg)
    v1 = jnp.max(le, axis=1, keepdims=True)
    i1 = jnp.min(jnp.where(le == v1, lane_f, far), axis=1, keepdims=True)
    le2 = jnp.where(lane_f == i1, neg, le)
    v2 = jnp.max(le2, axis=1, keepdims=True)
    i2 = jnp.min(jnp.where(le2 == v2, lane_f, far), axis=1, keepdims=True)
    e2 = jnp.exp(v2 - v1)
    w1 = p_sel * (1.0 / (1.0 + e2))
    w2 = p_sel * (e2 / (1.0 + e2))
    return lane, lane_f, g_idx, i1, i2, w1, w2


def _outproj_kernel(ctx_tiles, yc_c, yr_c, x_c, yc_l, yr_l, x_l, *rest):
    @pl.when(pl.program_id(0) < ctx_tiles)
    def _():
        _outproj_tile(yc_c, yr_c, x_c, *rest)

    @pl.when(pl.program_id(0) >= ctx_tiles)
    def _():
        _outproj_tile(yc_l, yr_l, x_l, *rest)


def _outproj_tile(yc_ref, yr_ref, x_ref, g1_ref, sh_ref, sc_ref, ng_ref, wo_ref, wr_ref, br_ref,
                  x1_ref, xloc_ref, route_ref, cnt_ref):
    m = (jnp.dot(yc_ref[...], wo_ref[0:CONV_W, :], preferred_element_type=F32)
         + jnp.dot(yr_ref[...], wo_ref[CONV_W:, :], preferred_element_type=F32))
    x1 = x_ref[...] + g1_ref[0, 0] * m
    x1_ref[...] = x1
    xn = (_rms(x1) * ng_ref[...]) * (1.0 + sc_ref[0, 0]) + sh_ref[0, 0]
    xb = xn.astype(BF16)
    logits = jnp.dot(xb, wr_ref[...], preferred_element_type=F32) + br_ref[...]
    lane, lane_f, g_idx, i1, i2, w1, w2 = _route(logits)

    picks = jnp.where(lane_f == g_idx, 1.0, 0.0)
    rows = picks.shape[0]
    tri = (lax.broadcasted_iota(jnp.int32, (rows, rows), 0)
           > lax.broadcasted_iota(jnp.int32, (rows, rows), 1))
    before = jnp.dot(jnp.where(tri, 1.0, 0.0).astype(BF16), picks.astype(BF16),
                     preferred_element_type=F32)
    count = jnp.sum(picks, axis=0, keepdims=True)
    cnt_ref[0] = count
    count8 = jnp.broadcast_to(jnp.floor((count + (SUBLANES - 1)) * (1.0 / SUBLANES)) * SUBLANES,
                              (SUBLANES, LANES))
    lane8 = lane[:SUBLANES]
    start = sum(jnp.where(lane8 >= k, pltpu.roll(count8, k, 1), 0.0) for k in range(1, N_GROUPS))
    local = jnp.sum(jnp.where(lane_f == g_idx, before + start[0:1], 0.0), axis=1, keepdims=True)
    route = jnp.where(lane == ROUTE_GROUP, g_idx, jnp.where(lane == ROUTE_LOCAL, local, jnp.where(
        lane == ROUTE_E1, i1 - N_GROUPS, jnp.where(lane == ROUTE_E2, i2 - N_GROUPS, jnp.where(
            lane == ROUTE_W1, w1, jnp.where(lane == ROUTE_W2, w2, 0.0))))))
    route_ref[...] = route

    local_row = jnp.transpose(jnp.broadcast_to(local, (rows, LANES)))[0:1, :]
    place = jnp.where(lax.broadcasted_iota(jnp.int32, (XLOC_ROWS, rows), 0).astype(F32) == local_row,
                      1.0, 0.0).astype(BF16)
    xloc_ref[0, :, :D_MODEL] = jnp.dot(place, xb, preferred_element_type=F32)
    xloc_ref[0, :, D_MODEL:] = sum(
        jnp.dot(place, piece, preferred_element_type=F32) for piece in _split3(route))


def _outproj(ctx, lat, mod4, mod_row_of_tile, norm_g, w_out_bf, w_router_bf, b_router):
    ctx_tiles = ctx[2].shape[0] // OUTPROJ_TILE
    tiles = ctx_tiles + lat[2].shape[0] // OUTPROJ_TILE
    tokens = tiles * OUTPROJ_TILE

    def mod_spec(which):
        return pl.BlockSpec((1, 1, 1, D_MODEL), lambda i: (mod_row_of_tile(i), which, 0, 0))

    ctx_tok = lambda w: pl.BlockSpec((OUTPROJ_TILE, w), lambda i: (jnp.minimum(i, ctx_tiles - 1), 0))
    lat_tok = lambda w: pl.BlockSpec((OUTPROJ_TILE, w), lambda i: (jnp.maximum(i - ctx_tiles, 0), 0))
    tok = lambda w: pl.BlockSpec((OUTPROJ_TILE, w), lambda i: (i, 0))
    full = lambda a: pl.BlockSpec(a.shape, lambda i: (0,) * a.ndim)
    widths = (CONV_W, RET_W, D_MODEL)
    return pl.pallas_call(
        functools.partial(_outproj_kernel, ctx_tiles),
        grid=(tiles,),
        in_specs=[ctx_tok(w) for w in widths] + [lat_tok(w) for w in widths] + [
            mod_spec(2), mod_spec(3), mod_spec(4),
            full(norm_g), full(w_out_bf), full(w_router_bf), full(b_router)],
        out_specs=[tok(D_MODEL),
                   pl.BlockSpec((1, XLOC_ROWS, ROW_W), lambda i: (i, 0, 0)),
                   tok(ROUTER_COLS),
                   pl.BlockSpec((1, 1, ROUTER_COLS), lambda i: (i, 0, 0))],
        out_shape=[jax.ShapeDtypeStruct((tokens, D_MODEL), F32),
                   jax.ShapeDtypeStruct((tiles, XLOC_ROWS, ROW_W), F32),
                   jax.ShapeDtypeStruct((tokens, ROUTER_COLS), F32),
                   jax.ShapeDtypeStruct((tiles, 1, ROUTER_COLS), F32)],
        compiler_params=pltpu.CompilerParams(
            dimension_semantics=("parallel",), vmem_limit_bytes=VMEM_LIMIT),
        name="outproj",
    )(*ctx, *lat, mod4, mod4, mod4, norm_g, w_out_bf, w_router_bf, b_router)


RUN_PIECES = tuple(SUBLANES << b for b in reversed(range((OUTPROJ_TILE // SUBLANES).bit_length())))


def _expert_kernel(tile_group_ref, n_used_ref, first_ref, last_ref, fill_ref,
                   run_len_ref, run_src_ref, run_dst_ref,
                   xloc_hbm, w1_ref, w3_ref, w2_ref, ys_ref, xbuf, xb, gate_tabs, sem):
    j = pl.program_id(0)
    step = pl.program_id(1)
    n_used = n_used_ref[0]

    def tile_fetch(tile):
        slot = tile % 2
        group = tile_group_ref[tile]
        row0 = tile * GROUP_TILE

        def from_token_tile(b, carry):
            run = b * N_GROUPS + group
            lo = jnp.maximum(run_dst_ref[run], row0)
            hi = jnp.minimum(run_dst_ref[run] + run_len_ref[run], row0 + GROUP_TILE)
            n = jnp.maximum(hi - lo, 0)
            src = run_src_ref[run] + lo - run_dst_ref[run]
            dst = lo - row0
            for size in RUN_PIECES:
                done = n & (-2 * size)

                @pl.when((n & size) != 0)
                def _():
                    pltpu.make_async_copy(
                        xloc_hbm.at[b, pl.ds(pl.multiple_of(src + done, SUBLANES), size)],
                        xbuf.at[slot, pl.ds(pl.multiple_of(dst + done, SUBLANES), size)], sem.at[slot]).start()
            return carry

        lax.fori_loop(first_ref[tile], last_ref[tile] + 1, from_token_tile, 0)

    def tile_wait(tile):
        slot = tile % 2
        for size in (GROUP_TILE,) + tuple(GROUP_TILE >> k for k in range(1, (GROUP_TILE // SUBLANES).bit_length())):
            @pl.when((fill_ref[tile] & size) != 0)
            def _():
                pltpu.make_async_copy(xbuf.at[1 - slot, pl.ds(0, size)], xbuf.at[slot, pl.ds(0, size)],
                                      sem.at[slot]).wait()

    @pl.when(j < n_used)
    def _():
        @pl.when(step == 0)
        def _():
            @pl.when(j == 0)
            def _():
                xbuf[...] = jnp.zeros_like(xbuf)
                tile_fetch(j)

            tile_wait(j)
            rows_in = xbuf[j % 2]
            xb[...] = rows_in[:, :D_MODEL].astype(BF16)
            route = rows_in[:, D_MODEL:]
            lane = lax.broadcasted_iota(jnp.int32, route.shape, 1)
            for n, which in enumerate((ROUTE_E1, ROUTE_E2, ROUTE_W1, ROUTE_W2)):
                col = jnp.sum(jnp.where(lane == which, route, 0.0), axis=1, keepdims=True)
                gate_tabs[n] = jnp.broadcast_to(col, route.shape)

            @pl.when(j + 1 < n_used)
            def _():
                tile_fetch(j + 1)

        def evaluate(rows):
            x = xb[:rows]
            total = None
            for s in range(EXPERTS_PER_STEP):
                expert = (tile_group_ref[j] * EXPERTS_PER_GROUP + step * EXPERTS_PER_STEP + s).astype(F32)
                gate = (jnp.where(gate_tabs[0, :rows] == expert, gate_tabs[2, :rows], 0.0)
                        + jnp.where(gate_tabs[1, :rows] == expert, gate_tabs[3, :rows], 0.0))
                hid = _silu(jnp.dot(x, w1_ref[0, s].astype(BF16), preferred_element_type=F32)) * jnp.dot(
                    x, w3_ref[0, s].astype(BF16), preferred_element_type=F32)
                y = jnp.dot(hid.astype(BF16), w2_ref[0, s].astype(BF16), preferred_element_type=F32)
                gated = jnp.concatenate(
                    [gate * y[:, c * LANES:(c + 1) * LANES] for c in range(D_MODEL // LANES)], axis=1)
                total = gated if total is None else total + gated

            @pl.when(step == 0)
            def _():
                ys_ref[:rows] = total
                if rows < GROUP_TILE:
                    ys_ref[rows:] = jnp.zeros((GROUP_TILE - rows, D_MODEL), F32)

            @pl.when(step > 0)
            def _():
                ys_ref[:rows] += total

        half = GROUP_TILE // 2

        @pl.when(fill_ref[j] > half)
        def _():
            evaluate(GROUP_TILE)

        @pl.when(fill_ref[j] <= half)
        def _():
            evaluate(half)

    @pl.when((j >= n_used) & (step == 0))
    def _():
        ys_ref[...] = jnp.zeros_like(ys_ref)


def _experts(xloc, tile_tables, runs, slots, w1, w3, w2):
    steps = EXPERTS_PER_GROUP // EXPERTS_PER_STEP
    per_step = lambda w: w.reshape((N_EXPERTS // EXPERTS_PER_STEP, EXPERTS_PER_STEP) + w.shape[1:])
    w_spec = lambda shape: pl.BlockSpec((1, EXPERTS_PER_STEP) + shape, lambda j, s, tg, nu, *_: (
        tg[jnp.minimum(j, nu[0] - 1)] * steps + jnp.where(j < nu[0], s, steps - 1), 0, 0, 0))
    grid_spec = pltpu.PrefetchScalarGridSpec(
        num_scalar_prefetch=8,
        grid=(slots // GROUP_TILE, steps),
        in_specs=[
            pl.BlockSpec(memory_space=pl.ANY),
            w_spec((D_MODEL, D_EXPERT)), w_spec((D_MODEL, D_EXPERT)), w_spec((D_EXPERT, D_MODEL)),
        ],
        out_specs=pl.BlockSpec((GROUP_TILE, D_MODEL), lambda j, s, *_: (j, 0)),
        scratch_shapes=[pltpu.VMEM((2, GROUP_TILE, ROW_W), F32), pltpu.VMEM((GROUP_TILE, D_MODEL), BF16),
                        pltpu.VMEM((4, GROUP_TILE, LANES), F32), pltpu.SemaphoreType.DMA((2,))],
    )
    return pl.pallas_call(
        _expert_kernel,
        grid_spec=grid_spec,
        out_shape=jax.ShapeDtypeStruct((slots, D_MODEL), F32),
        compiler_params=pltpu.CompilerParams(
            dimension_semantics=("arbitrary", "arbitrary"), vmem_limit_bytes=VMEM_LIMIT),
        name="experts",
    )(*tile_tables, *runs, xloc, per_step(w1), per_step(w3), per_step(w2))


def _combine_kernel(n_tiles, tile_base, run_len_ref, run_src_ref, run_dst_ref,
                    route_ref, g2_ref, fg_ref, x1_hbm, ys_hbm, o_ref, buf, x1_buf, sem, x1_sem):
    i = pl.program_id(0)
    slot = i % COMBINE_RING

    def x1_copy(local_tile):
        s = local_tile % COMBINE_RING
        rows = pl.ds(pl.multiple_of((local_tile + tile_base) * OUTPROJ_TILE, OUTPROJ_TILE), OUTPROJ_TILE)
        return pltpu.make_async_copy(x1_hbm.at[rows], x1_buf.at[s], x1_sem.at[s])

    def run_copies(local_tile, act):
        s = local_tile % COMBINE_RING
        tile = local_tile + tile_base
        for g in range(N_GROUPS):
            n = run_len_ref[tile * N_GROUPS + g]
            src = run_src_ref[tile * N_GROUPS + g]
            dst = run_dst_ref[tile * N_GROUPS + g]
            for size in RUN_PIECES:
                done = n & (-2 * size)

                @pl.when((n & size) != 0)
                def _():
                    act(pltpu.make_async_copy(
                        ys_hbm.at[pl.ds(pl.multiple_of(dst + done, SUBLANES), size)],
                        buf.at[s, pl.ds(pl.multiple_of(src + done, SUBLANES), size)], sem.at[s]))

    def fetch(local_tile):
        run_copies(local_tile, lambda cp: cp.start())
        x1_copy(local_tile).start()

    @pl.when(i == 0)
    def _():
        buf[...] = jnp.zeros_like(buf)
        for ahead in range(min(COMBINE_RING - 1, n_tiles)):
            fetch(i + ahead)

    @pl.when(i + COMBINE_RING - 1 < n_tiles)
    def _():
        fetch(i + COMBINE_RING - 1)

    run_copies(i, lambda cp: cp.wait())
    x1_copy(i).wait()
    route = route_ref[...]
    lane = lax.broadcasted_iota(jnp.int32, route.shape, 1)
    local = jnp.sum(jnp.where(lane == ROUTE_LOCAL, route, 0.0), axis=1, keepdims=True)
    pick = jnp.where(lax.broadcasted_iota(jnp.int32, (route.shape[0], LOCAL_ROWS), 1).astype(F32) == local,
                     1.0, 0.0).astype(BF16)
    moe = sum(jnp.dot(pick, piece, preferred_element_type=F32) for piece in _split3(buf[slot])[:2])
    y = x1_buf[slot] + g2_ref[0, 0] * moe
    o_ref[...] = _rms(y) * fg_ref[...]


def _combine(ys, runs, x1, route, tokens, mod4, mod_row_of_tile, final_g, tile_base):
    tiles = tokens // OUTPROJ_TILE
    tok = lambda w: pl.BlockSpec((OUTPROJ_TILE, w), lambda i, *_: (i + tile_base, 0))
    grid_spec = pltpu.PrefetchScalarGridSpec(
        num_scalar_prefetch=3,
        grid=(tiles,),
        in_specs=[
            tok(ROUTER_COLS),
            pl.BlockSpec((1, 1, 1, D_MODEL), lambda i, *_: (mod_row_of_tile(i + tile_base), 5, 0, 0)),
            pl.BlockSpec((1, D_MODEL), lambda i, *_: (0, 0)),
            pl.BlockSpec(memory_space=pl.ANY),
            pl.BlockSpec(memory_space=pl.ANY),
        ],
        out_specs=pl.BlockSpec((OUTPROJ_TILE, D_MODEL), lambda i, *_: (i, 0)),
        scratch_shapes=[pltpu.VMEM((COMBINE_RING, LOCAL_ROWS, D_MODEL), F32),
                        pltpu.VMEM((COMBINE_RING, OUTPROJ_TILE, D_MODEL), F32),
                        pltpu.SemaphoreType.DMA((COMBINE_RING,)), pltpu.SemaphoreType.DMA((COMBINE_RING,))],
    )
    return pl.pallas_call(
        functools.partial(_combine_kernel, tiles, tile_base),
        grid_spec=grid_spec,
        out_shape=jax.ShapeDtypeStruct((tokens, D_MODEL), F32),
        compiler_params=pltpu.CompilerParams(
            dimension_semantics=("arbitrary",), vmem_limit_bytes=VMEM_LIMIT),
        name="combine",
    )(*runs, route, mod4, final_g, x1, ys)


def _routing_tables(counts):
    counts = counts.astype(jnp.int32)
    tiles = counts.shape[0]
    run_len = ((counts + SUBLANES - 1) // SUBLANES) * SUBLANES
    run_src = jnp.cumsum(run_len, axis=1) - run_len
    group_rows = jnp.sum(run_len, axis=0)
    padded = ((group_rows + GROUP_TILE - 1) // GROUP_TILE) * GROUP_TILE
    ends = jnp.cumsum(padded)
    offs = ends - padded
    run_dst = offs[None, :] + jnp.cumsum(run_len, axis=0) - run_len

    n_used = ends[-1] // GROUP_TILE
    max_rows = tiles * (OUTPROJ_TILE + N_GROUPS * (SUBLANES - 1))
    max_tiles = max_rows // GROUP_TILE + N_GROUPS
    tile_ids = jnp.minimum(jnp.arange(max_tiles, dtype=jnp.int32), n_used - 1)
    tile_group = jnp.sum(tile_ids[:, None] * GROUP_TILE >= ends[None, :], axis=1).astype(jnp.int32)
    of_group = (tile_group[:, None] == jnp.arange(N_GROUPS, dtype=jnp.int32))[:, None, :]
    start = jnp.sum(jnp.where(of_group, run_dst[None], 0), axis=-1)
    stop = start + jnp.sum(jnp.where(of_group, run_len[None], 0), axis=-1)
    row0 = (tile_ids * GROUP_TILE)[:, None]
    first = jnp.sum(stop <= row0, axis=1).astype(jnp.int32)
    last = jnp.sum(start < row0 + GROUP_TILE, axis=1).astype(jnp.int32) - 1
    group_end = jnp.sum(jnp.where(of_group[:, 0, :], (offs + group_rows)[None, :], 0), axis=-1)
    fill = jnp.clip(group_end - row0[:, 0], 0, GROUP_TILE).astype(jnp.int32)
    flat = lambda a: a.reshape(-1)
    return ((tile_group, n_used.reshape(1), first, last, fill), (flat(run_len), flat(run_src), flat(run_dst)),
            max_tiles * GROUP_TILE)


def _mixer(x, mod4, mod_row, is_grid, s_f0, s_b0, p):
    norm_mix_g, w_in_bf, conv_w, decay_rows = p
    y_conv, *qkvg = _inproj(x, mod4, mod_row, norm_mix_g, w_in_bf, conv_w, is_grid)
    per_seq = lambda a: a.reshape(x.shape[0], x.shape[1], a.shape[-1])
    ret = _retention(*map(per_seq, qkvg), decay_rows, s_f0, s_b0, emit_state=not is_grid)
    return y_conv, ret[0].reshape(-1, RET_W), ret[1:]


def kernel(x_prompt, x_sample, state_ret_fwd, state_ret_bwd, c, c_ctx, norm_mix_g, norm_ffn_g, w_ada, b_ada, w_in, conv_w, ret_decay_fwd, ret_decay_bwd, w_out, w_router_group, b_router_group, w_router_expert, b_router_expert, w_gate_e, w_up_e, w_down_e, final_norm_g):
    assert norm_mix_g.shape[0] == 1, "single-layer backbone"
    n_lat = c.shape[0]
    ctx_row = n_lat
    mod_rows = 8
    cvec = jnp.concatenate([c, c_ctx[None, :], jnp.zeros((mod_rows - n_lat - 1, D_MODEL), F32)], axis=0)
    mod = _modulation(cvec, w_ada[0], b_ada[0][None, :])
    mod4 = mod.reshape(mod_rows, 6, 1, D_MODEL)

    pad = ROUTER_COLS - N_GROUPS - N_EXPERTS
    w_router = jnp.concatenate(
        [w_router_group[0], w_router_expert[0], jnp.zeros((D_MODEL, pad), F32)], axis=1).astype(BF16)
    b_router = jnp.concatenate([b_router_group[0], b_router_expert[0], jnp.zeros((pad,), F32)])[None, :]
    decay_rows = jnp.broadcast_to(
        jnp.concatenate([ret_decay_fwd[0], ret_decay_bwd[0]])[:, None], (2 * RET_HEADS, LANES))
    p_mix = (norm_mix_g, w_in[0].astype(BF16), conv_w[0], decay_rows)
    w_out_bf = w_out[0].astype(BF16)
    final_g = final_norm_g[None, :]

    ctx_tokens = x_prompt.shape[0] * x_prompt.shape[1]
    lat_tokens = x_sample.shape[0] * x_sample.shape[1]
    ctx_tiles = ctx_tokens // OUTPROJ_TILE
    lat_tiles_per_seq = x_sample.shape[1] // OUTPROJ_TILE
    tile_mod = lambda i: jnp.where(i < ctx_tiles, ctx_row, (i - ctx_tiles) // lat_tiles_per_seq)
    flat = lambda a: a.reshape(-1, a.shape[-1])

    yc_c, yr_c, (s_f, s_b) = _mixer(x_prompt, mod4, lambda b: ctx_row, False, None, None, p_mix)
    yc_l, yr_l, _ = _mixer(x_sample, mod4, lambda b: b, True, state_ret_fwd, state_ret_bwd, p_mix)

    x1, xloc, route, cnt = _outproj((yc_c, yr_c, flat(x_prompt)), (yc_l, yr_l, flat(x_sample)), mod4, tile_mod,
                                    norm_ffn_g, w_out_bf, w_router, b_router)
    tile_tables, runs, slots = _routing_tables(cnt[:, 0, :N_GROUPS])
    ys = _experts(xloc, tile_tables, runs, slots, w_gate_e[0], w_up_e[0], w_down_e[0])
    y_prompt = _combine(ys, runs, x1, route, ctx_tokens, mod4, tile_mod, final_g, 0)
    y_sample = _combine(ys, runs, x1, route, lat_tokens, mod4, tile_mod, final_g, ctx_tiles)
    return (y_prompt.reshape(x_prompt.shape), y_sample.reshape(x_sample.shape),
            s_f.astype(x_prompt.dtype), s_b.astype(x_prompt.dtype))
```

```python
import functools

import jax
import jax.numpy as jnp
import numpy as np
from jax import lax
from jax.experimental import pallas as pl
from jax.experimental.pallas import tpu as pltpu

F32 = jnp.float32
BF16 = jnp.bfloat16

D_MODEL = 1024
GRID_W = 64
CONV_W = 512
RET_HEADS = 4
RET_DK = 128
RET_DV = 128
RET_W = RET_HEADS * RET_DV
QK_W = RET_HEADS * RET_DK
CHUNK = 128
N_GROUPS = 4
EXPERTS_PER_GROUP = 8
N_EXPERTS = N_GROUPS * EXPERTS_PER_GROUP
D_EXPERT = 256
ROPE_BASE = 10000.0
EPS = 1e-6

LANES = 128
TOKEN_TILE = 1024
OUTPROJ_TILE = 512
GROUP_TILE = 1024
EXPERTS_PER_STEP = 4
RET_UNROLL = 8
COMBINE_RING = 3
SUBLANES = 8
XLOC_ROWS = OUTPROJ_TILE + N_GROUPS * SUBLANES
LOCAL_ROWS = OUTPROJ_TILE + LANES
ROW_W = D_MODEL + LANES
ROUTE_GROUP, ROUTE_LOCAL, ROUTE_E1, ROUTE_E2, ROUTE_W1, ROUTE_W2 = range(6)
MOD_COLS = 1536
ROUTER_COLS = LANES
VMEM_LIMIT = 56 * 1024 * 1024


def _silu(x):
    return x * jax.nn.sigmoid(x)


def _rms(x):
    return x * lax.rsqrt(jnp.mean(x * x, axis=-1, keepdims=True) + EPS)


def _bdot(a, b):
    return jnp.dot(a.astype(BF16), b.astype(BF16), preferred_element_type=F32)


def _split3(x):
    hi = x.astype(BF16)
    rest = x - hi.astype(F32)
    mid = rest.astype(BF16)
    return hi, mid, (rest - mid.astype(F32)).astype(BF16)


def _mod_kernel(c_ref, w_ref, b_ref, o_ref):
    o_ref[...] = _bdot(_silu(c_ref[...]), w_ref[...]) + b_ref[...]


def _modulation(cvec, w_ada, b_ada):
    rows = cvec.shape[0]
    n = w_ada.shape[1]
    return pl.pallas_call(
        _mod_kernel,
        grid=(n // MOD_COLS,),
        in_specs=[
            pl.BlockSpec((rows, D_MODEL), lambda j: (0, 0)),
            pl.BlockSpec((D_MODEL, MOD_COLS), lambda j: (0, j)),
            pl.BlockSpec((1, MOD_COLS), lambda j: (0, j)),
        ],
        out_specs=pl.BlockSpec((rows, MOD_COLS), lambda j: (0, j)),
        out_shape=jax.ShapeDtypeStruct((rows, n), F32),
        compiler_params=pltpu.CompilerParams(vmem_limit_bytes=VMEM_LIMIT),
        name="modulation",
    )(cvec, w_ada, b_ada)


def _inproj_kernel(seg, is_grid, x_ref, sh_ref, sc_ref, ng_ref, w_ref, cw_ref, *rest):
    if is_grid:
        cos_ref, sa_ref, sb_ref, yc_ref, q_ref, k_ref, v_ref, g_ref = rest
    else:
        yc_ref, q_ref, k_ref, v_ref, g_ref = rest
    x = x_ref[...]
    xn = (_rms(x) * ng_ref[...]) * (1.0 + sc_ref[0, 0]) + sh_ref[0, 0]
    xb = xn.astype(BF16)

    def proj(c0, n):
        return jnp.dot(xb, w_ref[:, c0:c0 + n], preferred_element_type=F32)

    gate_b = proj(0, CONV_W)
    u = proj(CONV_W, CONV_W) * proj(2 * CONV_W, CONV_W)
    rows = u.shape[0]
    pos = lax.broadcasted_iota(jnp.int32, u.shape, 0) & (seg - 1)
    u_prev = jnp.where(pos != 0, pltpu.roll(u, 1, 0), 0.0)
    u_next = jnp.where(pos != seg - 1, pltpu.roll(u, rows - 1, 0), 0.0)
    conv = cw_ref[0:1, :] * u_prev + cw_ref[1:2, :] * u + cw_ref[2:3, :] * u_next
    yc_ref[...] = (gate_b * conv).astype(yc_ref.dtype)

    q0 = 3 * CONV_W
    q = proj(q0, QK_W)
    k = proj(q0 + QK_W, QK_W)
    if is_grid:
        cos, sa, sb = cos_ref[...], sa_ref[...], sb_ref[...]

        def rope(t):
            out = []
            for h in range(RET_HEADS):
                th = t[:, h * RET_DK:(h + 1) * RET_DK]
                out.append(th * cos + pltpu.roll(th, RET_DK - 1, 1) * sa + pltpu.roll(th, 1, 1) * sb)
            return jnp.concatenate(out, axis=1)

        q, k = rope(q), rope(k)
    q_ref[...] = q
    k_ref[...] = k
    v_ref[...] = proj(q0 + 2 * QK_W, RET_W)
    g_ref[...] = proj(q0 + 2 * QK_W + RET_W, RET_W)


def _rope_tables(length):
    pos = np.arange(length)
    row = (pos // GRID_W).astype(np.float64)
    col = (pos % GRID_W).astype(np.float64)
    n_pairs = RET_DK // 4
    freqs = ROPE_BASE ** (-(np.arange(n_pairs, dtype=np.float64) * 2.0 / (RET_DK // 2)))
    ang = np.concatenate([row[:, None] * freqs, col[:, None] * freqs], axis=-1)
    cos = np.repeat(np.cos(ang), 2, axis=-1)
    sin = np.repeat(np.sin(ang), 2, axis=-1)
    even = (np.arange(RET_DK) % 2) == 0
    return tuple(jnp.asarray(t, F32) for t in (cos, np.where(even, -sin, 0.0), np.where(even, 0.0, sin)))


def _inproj(x, mod4, mod_row, norm_g, w_in_bf, conv_w, is_grid):
    bsz, length, _ = x.shape
    seg = GRID_W if is_grid else length
    assert TOKEN_TILE % seg == 0 and (length % TOKEN_TILE == 0 or TOKEN_TILE % length == 0)
    tokens = bsz * length
    tiles_per_seq = max(length // TOKEN_TILE, 1)
    seqs_per_tile = max(TOKEN_TILE // length, 1)
    batch_of = lambda i: (i // tiles_per_seq) * seqs_per_tile

    def mod_spec(which):
        return pl.BlockSpec((1, 1, 1, D_MODEL), lambda i: (mod_row(batch_of(i)), which, 0, 0))

    def tok_spec(width):
        return pl.BlockSpec((TOKEN_TILE, width), lambda i: (i, 0))

    in_specs = [
        tok_spec(D_MODEL), mod_spec(0), mod_spec(1),
        pl.BlockSpec((1, D_MODEL), lambda i: (0, 0)),
        pl.BlockSpec(w_in_bf.shape, lambda i: (0, 0)),
        pl.BlockSpec(conv_w.shape, lambda i: (0, 0)),
    ]
    args = [x.reshape(tokens, D_MODEL), mod4, mod4, norm_g, w_in_bf, conv_w]
    if is_grid:
        assert length % TOKEN_TILE == 0
        in_specs += [pl.BlockSpec((TOKEN_TILE, RET_DK), lambda i: (i % tiles_per_seq, 0))] * 3
        args += list(_rope_tables(length))
    shp = lambda w, dt: jax.ShapeDtypeStruct((tokens, w), dt)
    return pl.pallas_call(
        functools.partial(_inproj_kernel, seg, is_grid),
        grid=(tokens // TOKEN_TILE,),
        in_specs=in_specs,
        out_specs=[tok_spec(CONV_W), tok_spec(QK_W), tok_spec(QK_W), tok_spec(RET_W), tok_spec(RET_W)],
        out_shape=[shp(CONV_W, BF16), shp(QK_W, F32), shp(QK_W, F32), shp(RET_W, F32), shp(RET_W, F32)],
        compiler_params=pltpu.CompilerParams(
            dimension_semantics=("parallel",), vmem_limit_bytes=VMEM_LIMIT,
            allow_input_fusion=[a is w_in_bf for a in args]),
        name="inproj_grid" if is_grid else "inproj_seq",
    )(*args)


def _ret_kernel(n_chunks, heads, has_init, emit_state, a_ref, q_ref, k_ref, v_ref, g_ref, *rest):
    rest = list(rest)
    if has_init:
        sf0_ref, sb0_ref = rest[:2]
        rest = rest[2:]
    y_ref = rest.pop(0)
    if emit_state:
        sf_out, sb_out = rest[:2]
        rest = rest[2:]
    st_f, st_b, dec = rest
    c = CHUNK
    sq = (c, c)
    head0 = pl.program_id(0) * heads

    def log_decays(hh):
        lg_f = jnp.log1p(-jnp.exp(a_ref[pl.ds(head0 + hh, 1), :]))
        lg_b = jnp.log1p(-jnp.exp(a_ref[pl.ds(head0 + hh + RET_HEADS, 1), :]))
        return lg_f, lg_b

    @pl.when(pl.program_id(1) == 0)
    def _():
        row = lax.broadcasted_iota(jnp.int32, sq, 0).astype(F32)
        col = lax.broadcasted_iota(jnp.int32, sq, 1).astype(F32)
        scale = RET_DK ** -0.5
        for hh in range(heads):
            lg_f, lg_b = log_decays(hh)
            dec[hh, 0] = scale * (
                jnp.where(row >= col, jnp.exp(jnp.where(row >= col, row - col, 0.0) * lg_f), 0.0)
                + jnp.where(col >= row, jnp.exp(jnp.where(col >= row, col - row, 0.0) * lg_b), 0.0))
            dec[hh, 1] = jnp.exp((row + 1.0) * lg_f)
            dec[hh, 2] = jnp.exp((c - row) * lg_b)
            dec[hh, 3] = scale * jnp.exp((c - 1.0 - col) * lg_f)
            dec[hh, 4] = scale * jnp.exp(col * lg_b)

    def rows(n):
        return pl.ds(pl.multiple_of(n * c, c), c) if not isinstance(n, int) else pl.ds(n * c, c)

    def cols(hh):
        return slice(hh * RET_DK, (hh + 1) * RET_DK)

    def kv_step(hh, n):
        kt = jnp.transpose(k_ref[0, rows(n), cols(hh)])
        lhs = jnp.concatenate([kt * dec[hh, 3], kt * dec[hh, 4]], axis=0)
        kv = _bdot(lhs, v_ref[0, rows(n), cols(hh)])
        st_f[hh, n] = kv[:RET_DK]
        st_b[hh, n] = kv[RET_DK:]

    def scan(hh, st, decay, order, s):
        def step(i, s):
            n = order(i)
            kv = st[hh, n]
            st[hh, n] = s
            return s * decay + kv
        if n_chunks <= RET_UNROLL:
            for i in range(n_chunks):
                s = step(i, s)
            return s
        return lax.fori_loop(0, n_chunks, step, s, unroll=RET_UNROLL)

    def out_step(hh, n):
        q = q_ref[0, rows(n), cols(hh)]
        scores = lax.dot_general(q.astype(BF16), k_ref[0, rows(n), cols(hh)].astype(BF16),
                                 (((1,), (1,)), ((), ())), preferred_element_type=F32)
        o = _bdot(scores * dec[hh, 0], v_ref[0, rows(n), cols(hh)])
        q_dec = jnp.concatenate([q * dec[hh, 1], q * dec[hh, 2]], axis=1)
        o = o + _bdot(q_dec, jnp.concatenate([st_f[hh, n], st_b[hh, n]], axis=0))
        y = _silu(g_ref[0, rows(n), cols(hh)]) * _rms(o)
        y_ref[0, rows(n), cols(hh)] = y.astype(y_ref.dtype)

    def over_chunks(step):
        if n_chunks * heads <= RET_UNROLL:
            for hh in range(heads):
                for n in range(n_chunks):
                    step(hh, n)
        else:
            for hh in range(heads):
                lax.fori_loop(0, n_chunks, lambda n, carry: (step(hh, n), carry)[1], 0, unroll=RET_UNROLL)

    over_chunks(kv_step)
    finals = []
    for hh in range(heads):
        lg_f, lg_b = log_decays(hh)
        s_f = sf0_ref[0, 0, hh] if has_init else jnp.zeros(sq, F32)
        s_b = sb0_ref[0, 0, hh] if has_init else jnp.zeros(sq, F32)
        s_f = scan(hh, st_f, jnp.exp(c * lg_f), lambda i: i, s_f)
        s_b = scan(hh, st_b, jnp.exp(c * lg_b), lambda i: n_chunks - 1 - i, s_b)
        finals.append((s_f, s_b))
    over_chunks(out_step)
    if emit_state:
        for hh, (s_f, s_b) in enumerate(finals):
            sf_out[0, 0, hh] = s_f
            sb_out[0, 0, hh] = s_b


def _retention(q, k, v, g, decay_rows, s_f0, s_b0, emit_state):
    bsz, length, _ = q.shape
    n_chunks = length // CHUNK
    has_init = s_f0 is not None
    heads = RET_HEADS if n_chunks * RET_HEADS <= RET_UNROLL else 1
    head_spec = pl.BlockSpec((1, length, heads * RET_DK), lambda h, b: (b, 0, h))
    st_spec = pl.BlockSpec((1, 1, heads, RET_DK, RET_DV), lambda h, b: (b, 0, h, 0, 0))
    in_specs = [pl.BlockSpec(decay_rows.shape, lambda h, b: (0, 0))] + [head_spec] * 4
    args = [decay_rows, q, k, v, g]
    if has_init:
        in_specs += [st_spec, st_spec]
        args += [s_f0, s_b0]
    out_specs = [head_spec]
    out_shape = [jax.ShapeDtypeStruct((bsz, length, RET_W), BF16)]
    if emit_state:
        st_shape = jax.ShapeDtypeStruct((bsz, 1, RET_HEADS, RET_DK, RET_DV), F32)
        out_specs += [st_spec, st_spec]
        out_shape += [st_shape, st_shape]
    return pl.pallas_call(
        functools.partial(_ret_kernel, n_chunks, heads, has_init, emit_state),
        grid=(RET_HEADS // heads, bsz),
        in_specs=in_specs,
        out_specs=out_specs,
        out_shape=out_shape,
        scratch_shapes=[
            pltpu.VMEM((heads, n_chunks, RET_DK, RET_DV), F32),
            pltpu.VMEM((heads, n_chunks, RET_DK, RET_DV), F32),
            pltpu.VMEM((heads, 5, CHUNK, CHUNK), F32),
        ],
        compiler_params=pltpu.CompilerParams(
            dimension_semantics=("arbitrary", "arbitrary"), vmem_limit_bytes=VMEM_LIMIT),
        name="retention_init" if has_init else "retention_zero",
    )(*args)


def _route(logits):
    lane = lax.broadcasted_iota(jnp.int32, logits.shape, 1)
    lane_f = lane.astype(F32)
    neg = -jnp.inf
    far = float(LANES)
    is_g = lane < N_GROUPS
    lg = jnp.where(is_g, logits, neg)
    g_max = jnp.max(lg, axis=1, keepdims=True)
    g_idx = jnp.min(jnp.where(lg == g_max, lane_f, far), axis=1, keepdims=True)
    p_sel = 1.0 / jnp.sum(jnp.where(is_g, jnp.exp(lg - g_max), 0.0), axis=1, keepdims=True)
    lane_group = ((lane - N_GROUPS) >> (EXPERTS_PER_GROUP.bit_length() - 1)).astype(F32)
    sel = (lane >= N_GROUPS) & (lane < N_GROUPS + N_EXPERTS) & (lane_group == g_idx)
    le = jnp.where(sel, logits, neg)
    v1 = jnp.max(le, axis=1, keepdims=True)
    i1 = jnp.min(jnp.where(le == v1, lane_f, far), axis=1, keepdims=True)
    le2 = jnp.where(lane_f == i1, neg, le)
    v2 = jnp.max(le2, axis=1, keepdims=True)
    i2 = jnp.min(jnp.where(le2 == v2, lane_f, far), axis=1, keepdims=True)
    e2 = jnp.exp(v2 - v1)
    w1 = p_sel * (1.0 / (1.0 + e2))
    w2 = p_sel * (e2 / (1.0 + e2))
    return lane, lane_f, g_idx, i1, i2, w1, w2


def _outproj_kernel(ctx_tiles, yc_c, yr_c, x_c, yc_l, yr_l, x_l, *rest):
    @pl.when(pl.program_id(0) < ctx_tiles)
    def _():
        _outproj_tile(yc_c, yr_c, x_c, *rest)

    @pl.when(pl.program_id(0) >= ctx_tiles)
    def _():
        _outproj_tile(yc_l, yr_l, x_l, *rest)


def _outproj_tile(yc_ref, yr_ref, x_ref, g1_ref, sh_ref, sc_ref, ng_ref, wo_ref, wr_ref, br_ref,
                  x1_ref, xloc_ref, route_ref, cnt_ref):
    m = (jnp.dot(yc_ref[...], wo_ref[0:CONV_W, :], preferred_element_type=F32)
         + jnp.dot(yr_ref[...], wo_ref[CONV_W:, :], preferred_element_type=F32))
    x1 = x_ref[...] + g1_ref[0, 0] * m
    x1_ref[...] = x1
    xn = (_rms(x1) * ng_ref[...]) * (1.0 + sc_ref[0, 0]) + sh_ref[0, 0]
    xb = xn.astype(BF16)
    logits = jnp.dot(xb, wr_ref[...], preferred_element_type=F32) + br_ref[...]
    lane, lane_f, g_idx, i1, i2, w1, w2 = _route(logits)

    picks = jnp.where(lane_f == g_idx, 1.0, 0.0)
    rows = picks.shape[0]
    tri = (lax.broadcasted_iota(jnp.int32, (rows, rows), 0)
           > lax.broadcasted_iota(jnp.int32, (rows, rows), 1))
    before = jnp.dot(jnp.where(tri, 1.0, 0.0).astype(BF16), picks.astype(BF16),
                     preferred_element_type=F32)
    count = jnp.sum(picks, axis=0, keepdims=True)
    cnt_ref[0] = count
    count8 = jnp.broadcast_to(jnp.floor((count + (SUBLANES - 1)) * (1.0 / SUBLANES)) * SUBLANES,
                              (SUBLANES, LANES))
    lane8 = lane[:SUBLANES]
    start = sum(jnp.where(lane8 >= k, pltpu.roll(count8, k, 1), 0.0) for k in range(1, N_GROUPS))
    local = jnp.sum(jnp.where(lane_f == g_idx, before + start[0:1], 0.0), axis=1, keepdims=True)
    route = jnp.where(lane == ROUTE_GROUP, g_idx, jnp.where(lane == ROUTE_LOCAL, local, jnp.where(
        lane == ROUTE_E1, i1 - N_GROUPS, jnp.where(lane == ROUTE_E2, i2 - N_GROUPS, jnp.where(
            lane == ROUTE_W1, w1, jnp.where(lane == ROUTE_W2, w2, 0.0))))))
    route_ref[...] = route

    local_row = jnp.transpose(jnp.broadcast_to(local, (rows, LANES)))[0:1, :]
    place = jnp.where(lax.broadcasted_iota(jnp.int32, (XLOC_ROWS, rows), 0).astype(F32) == local_row,
                      1.0, 0.0).astype(BF16)
    xloc_ref[0, :, :D_MODEL] = jnp.dot(place, xb, preferred_element_type=F32)
    xloc_ref[0, :, D_MODEL:] = sum(
        jnp.dot(place, piece, preferred_element_type=F32) for piece in _split3(route))


def _outproj(ctx, lat, mod4, mod_row_of_tile, norm_g, w_out_bf, w_router_bf, b_router):
    ctx_tiles = ctx[2].shape[0] // OUTPROJ_TILE
    tiles = ctx_tiles + lat[2].shape[0] // OUTPROJ_TILE
    tokens = tiles * OUTPROJ_TILE

    def mod_spec(which):
        return pl.BlockSpec((1, 1, 1, D_MODEL), lambda i: (mod_row_of_tile(i), which, 0, 0))

    ctx_tok = lambda w: pl.BlockSpec((OUTPROJ_TILE, w), lambda i: (jnp.minimum(i, ctx_tiles - 1), 0))
    lat_tok = lambda w: pl.BlockSpec((OUTPROJ_TILE, w), lambda i: (jnp.maximum(i - ctx_tiles, 0), 0))
    tok = lambda w: pl.BlockSpec((OUTPROJ_TILE, w), lambda i: (i, 0))
    full = lambda a: pl.BlockSpec(a.shape, lambda i: (0,) * a.ndim)
    widths = (CONV_W, RET_W, D_MODEL)
    return pl.pallas_call(
        functools.partial(_outproj_kernel, ctx_tiles),
        grid=(tiles,),
        in_specs=[ctx_tok(w) for w in widths] + [lat_tok(w) for w in widths] + [
            mod_spec(2), mod_spec(3), mod_spec(4),
            full(norm_g), full(w_out_bf), full(w_router_bf), full(b_router)],
        out_specs=[tok(D_MODEL),
                   pl.BlockSpec((1, XLOC_ROWS, ROW_W), lambda i: (i, 0, 0)),
                   tok(ROUTER_COLS),
                   pl.BlockSpec((1, 1, ROUTER_COLS), lambda i: (i, 0, 0))],
        out_shape=[jax.ShapeDtypeStruct((tokens, D_MODEL), F32),
                   jax.ShapeDtypeStruct((tiles, XLOC_ROWS, ROW_W), F32),
                   jax.ShapeDtypeStruct((tokens, ROUTER_COLS), F32),
                   jax.ShapeDtypeStruct((tiles, 1, ROUTER_COLS), F32)],
        compiler_params=pltpu.CompilerParams(
            dimension_semantics=("parallel",), vmem_limit_bytes=VMEM_LIMIT,
            allow_input_fusion=[False] * 10 + [True, True, False]),
        name="outproj",
    )(*ctx, *lat, mod4, mod4, mod4, norm_g, w_out_bf, w_router_bf, b_router)


RUN_PIECES = tuple(SUBLANES << b for b in reversed(range((OUTPROJ_TILE // SUBLANES).bit_length())))


def _expert_kernel(tile_group_ref, n_used_ref, first_ref, last_ref, fill_ref,
                   run_len_ref, run_src_ref, run_dst_ref,
                   xloc_hbm, w1_ref, w3_ref, w2_ref, ys_ref, xbuf, xb, gate_tabs, sem):
    j = pl.program_id(0)
    step = pl.program_id(1)
    n_used = n_used_ref[0]

    def tile_fetch(tile):
        slot = tile % 2
        group = tile_group_ref[tile]
        row0 = tile * GROUP_TILE

        def from_token_tile(b, carry):
            run = b * N_GROUPS + group
            lo = jnp.maximum(run_dst_ref[run], row0)
            hi = jnp.minimum(run_dst_ref[run] + run_len_ref[run], row0 + GROUP_TILE)
            n = jnp.maximum(hi - lo, 0)
            src = run_src_ref[run] + lo - run_dst_ref[run]
            dst = lo - row0
            for size in RUN_PIECES:
                done = n & (-2 * size)

                @pl.when((n & size) != 0)
                def _():
                    pltpu.make_async_copy(
                        xloc_hbm.at[b, pl.ds(pl.multiple_of(src + done, SUBLANES), size)],
                        xbuf.at[slot, pl.ds(pl.multiple_of(dst + done, SUBLANES), size)], sem.at[slot]).start()
            return carry

        lax.fori_loop(first_ref[tile], last_ref[tile] + 1, from_token_tile, 0)

    def tile_wait(tile):
        slot = tile % 2
        for size in (GROUP_TILE,) + tuple(GROUP_TILE >> k for k in range(1, (GROUP_TILE // SUBLANES).bit_length())):
            @pl.when((fill_ref[tile] & size) != 0)
            def _():
                pltpu.make_async_copy(xbuf.at[1 - slot, pl.ds(0, size)], xbuf.at[slot, pl.ds(0, size)],
                                      sem.at[slot]).wait()

    @pl.when(j < n_used)
    def _():
        @pl.when(step == 0)
        def _():
            @pl.when(j == 0)
            def _():
                xbuf[...] = jnp.zeros_like(xbuf)
                tile_fetch(j)

            tile_wait(j)
            rows_in = xbuf[j % 2]
            xb[...] = rows_in[:, :D_MODEL].astype(BF16)
            route = rows_in[:, D_MODEL:]
            lane = lax.broadcasted_iota(jnp.int32, route.shape, 1)
            for n, which in enumerate((ROUTE_E1, ROUTE_E2, ROUTE_W1, ROUTE_W2)):
                col = jnp.sum(jnp.where(lane == which, route, 0.0), axis=1, keepdims=True)
                gate_tabs[n] = jnp.broadcast_to(col, route.shape)

            @pl.when(j + 1 < n_used)
            def _():
                tile_fetch(j + 1)

        def evaluate(rows):
            x = xb[:rows]
            total = None
            for s in range(EXPERTS_PER_STEP):
                expert = (tile_group_ref[j] * EXPERTS_PER_GROUP + step * EXPERTS_PER_STEP + s).astype(F32)
                gate = (jnp.where(gate_tabs[0, :rows] == expert, gate_tabs[2, :rows], 0.0)
                        + jnp.where(gate_tabs[1, :rows] == expert, gate_tabs[3, :rows], 0.0))
                hid = _silu(jnp.dot(x, w1_ref[0, s].astype(BF16), preferred_element_type=F32)) * jnp.dot(
                    x, w3_ref[0, s].astype(BF16), preferred_element_type=F32)
                y = jnp.dot(hid.astype(BF16), w2_ref[0, s].astype(BF16), preferred_element_type=F32)
                gated = jnp.concatenate(
                    [gate * y[:, c * LANES:(c + 1) * LANES] for c in range(D_MODEL // LANES)], axis=1)
                total = gated if total is None else total + gated

            @pl.when(step == 0)
            def _():
                ys_ref[:rows] = total
                if rows < GROUP_TILE:
                    ys_ref[rows:] = jnp.zeros((GROUP_TILE - rows, D_MODEL), F32)

            @pl.when(step > 0)
            def _():
                ys_ref[:rows] += total

        half = GROUP_TILE // 2

        @pl.when(fill_ref[j] > half)
        def _():
            evaluate(GROUP_TILE)

        @pl.when(fill_ref[j] <= half)
        def _():
            evaluate(half)

    @pl.when((j >= n_used) & (step == 0))
    def _():
        ys_ref[...] = jnp.zeros_like(ys_ref)


def _experts(xloc, tile_tables, runs, slots, w1, w3, w2):
    steps = EXPERTS_PER_GROUP // EXPERTS_PER_STEP
    per_step = lambda w: w.reshape((N_EXPERTS // EXPERTS_PER_STEP, EXPERTS_PER_STEP) + w.shape[1:])
    w_spec = lambda shape: pl.BlockSpec((1, EXPERTS_PER_STEP) + shape, lambda j, s, tg, nu, *_: (
        tg[jnp.minimum(j, nu[0] - 1)] * steps + jnp.where(j < nu[0], s, steps - 1), 0, 0, 0))
    grid_spec = pltpu.PrefetchScalarGridSpec(
        num_scalar_prefetch=8,
        grid=(slots // GROUP_TILE, steps),
        in_specs=[
            pl.BlockSpec(memory_space=pl.ANY),
            w_spec((D_MODEL, D_EXPERT)), w_spec((D_MODEL, D_EXPERT)), w_spec((D_EXPERT, D_MODEL)),
        ],
        out_specs=pl.BlockSpec((GROUP_TILE, D_MODEL), lambda j, s, *_: (j, 0)),
        scratch_shapes=[pltpu.VMEM((2, GROUP_TILE, ROW_W), F32), pltpu.VMEM((GROUP_TILE, D_MODEL), BF16),
                        pltpu.VMEM((4, GROUP_TILE, LANES), F32), pltpu.SemaphoreType.DMA((2,))],
    )
    return pl.pallas_call(
        _expert_kernel,
        grid_spec=grid_spec,
        out_shape=jax.ShapeDtypeStruct((slots, D_MODEL), F32),
        compiler_params=pltpu.CompilerParams(
            dimension_semantics=("arbitrary", "arbitrary"), vmem_limit_bytes=VMEM_LIMIT),
        name="experts",
    )(*tile_tables, *runs, xloc, per_step(w1), per_step(w3), per_step(w2))


def _combine_kernel(n_tiles, tile_base, run_len_ref, run_src_ref, run_dst_ref,
                    route_ref, g2_ref, fg_ref, x1_hbm, ys_hbm, o_ref, buf, x1_buf, sem, x1_sem):
    i = pl.program_id(0)
    slot = i % COMBINE_RING

    def x1_copy(local_tile):
        s = local_tile % COMBINE_RING
        rows = pl.ds(pl.multiple_of((local_tile + tile_base) * OUTPROJ_TILE, OUTPROJ_TILE), OUTPROJ_TILE)
        return pltpu.make_async_copy(x1_hbm.at[rows], x1_buf.at[s], x1_sem.at[s])

    def run_copies(local_tile, act):
        s = local_tile % COMBINE_RING
        tile = local_tile + tile_base
        for g in range(N_GROUPS):
            n = run_len_ref[tile * N_GROUPS + g]
            src = run_src_ref[tile * N_GROUPS + g]
            dst = run_dst_ref[tile * N_GROUPS + g]
            for size in RUN_PIECES:
                done = n & (-2 * size)

                @pl.when((n & size) != 0)
                def _():
                    act(pltpu.make_async_copy(
                        ys_hbm.at[pl.ds(pl.multiple_of(dst + done, SUBLANES), size)],
                        buf.at[s, pl.ds(pl.multiple_of(src + done, SUBLANES), size)], sem.at[s]))

    def fetch(local_tile):
        run_copies(local_tile, lambda cp: cp.start())
        x1_copy(local_tile).start()

    @pl.when(i == 0)
    def _():
        buf[...] = jnp.zeros_like(buf)
        for ahead in range(min(COMBINE_RING - 1, n_tiles)):
            fetch(i + ahead)

    @pl.when(i + COMBINE_RING - 1 < n_tiles)
    def _():
        fetch(i + COMBINE_RING - 1)

    run_copies(i, lambda cp: cp.wait())
    x1_copy(i).wait()
    route = route_ref[...]
    lane = lax.broadcasted_iota(jnp.int32, route.shape, 1)
    local = jnp.sum(jnp.where(lane == ROUTE_LOCAL, route, 0.0), axis=1, keepdims=True)
    pick = jnp.where(lax.broadcasted_iota(jnp.int32, (route.shape[0], LOCAL_ROWS), 1).astype(F32) == local,
                     1.0, 0.0).astype(BF16)
    moe = sum(jnp.dot(pick, piece, preferred_element_type=F32) for piece in _split3(buf[slot])[:2])
    y = x1_buf[slot] + g2_ref[0, 0] * moe
    o_ref[...] = _rms(y) * fg_ref[...]


def _combine(ys, runs, x1, route, tokens, mod4, mod_row_of_tile, final_g, tile_base):
    tiles = tokens // OUTPROJ_TILE
    tok = lambda w: pl.BlockSpec((OUTPROJ_TILE, w), lambda i, *_: (i + tile_base, 0))
    grid_spec = pltpu.PrefetchScalarGridSpec(
        num_scalar_prefetch=3,
        grid=(tiles,),
        in_specs=[
            tok(ROUTER_COLS),
            pl.BlockSpec((1, 1, 1, D_MODEL), lambda i, *_: (mod_row_of_tile(i + tile_base), 5, 0, 0)),
            pl.BlockSpec((1, D_MODEL), lambda i, *_: (0, 0)),
            pl.BlockSpec(memory_space=pl.ANY),
            pl.BlockSpec(memory_space=pl.ANY),
        ],
        out_specs=pl.BlockSpec((OUTPROJ_TILE, D_MODEL), lambda i, *_: (i, 0)),
        scratch_shapes=[pltpu.VMEM((COMBINE_RING, LOCAL_ROWS, D_MODEL), F32),
                        pltpu.VMEM((COMBINE_RING, OUTPROJ_TILE, D_MODEL), F32),
                        pltpu.SemaphoreType.DMA((COMBINE_RING,)), pltpu.SemaphoreType.DMA((COMBINE_RING,))],
    )
    return pl.pallas_call(
        functools.partial(_combine_kernel, tiles, tile_base),
        grid_spec=grid_spec,
        out_shape=jax.ShapeDtypeStruct((tokens, D_MODEL), F32),
        compiler_params=pltpu.CompilerParams(
            dimension_semantics=("arbitrary",), vmem_limit_bytes=VMEM_LIMIT),
        name="combine",
    )(*runs, route, mod4, final_g, x1, ys)


def _routing_tables(counts):
    counts = counts.astype(jnp.int32)
    tiles = counts.shape[0]
    run_len = ((counts + SUBLANES - 1) // SUBLANES) * SUBLANES
    run_src = jnp.cumsum(run_len, axis=1) - run_len
    group_rows = jnp.sum(run_len, axis=0)
    padded = ((group_rows + GROUP_TILE - 1) // GROUP_TILE) * GROUP_TILE
    ends = jnp.cumsum(padded)
    offs = ends - padded
    run_dst = offs[None, :] + jnp.cumsum(run_len, axis=0) - run_len

    n_used = ends[-1] // GROUP_TILE
    max_rows = tiles * (OUTPROJ_TILE + N_GROUPS * (SUBLANES - 1))
    max_tiles = max_rows // GROUP_TILE + N_GROUPS
    tile_ids = jnp.minimum(jnp.arange(max_tiles, dtype=jnp.int32), n_used - 1)
    tile_group = jnp.sum(tile_ids[:, None] * GROUP_TILE >= ends[None, :], axis=1).astype(jnp.int32)
    of_group = (tile_group[:, None] == jnp.arange(N_GROUPS, dtype=jnp.int32))[:, None, :]
    start = jnp.sum(jnp.where(of_group, run_dst[None], 0), axis=-1)
    stop = start + jnp.sum(jnp.where(of_group, run_len[None], 0), axis=-1)
    row0 = (tile_ids * GROUP_TILE)[:, None]
    first = jnp.sum(stop <= row0, axis=1).astype(jnp.int32)
    last = jnp.sum(start < row0 + GROUP_TILE, axis=1).astype(jnp.int32) - 1
    group_end = jnp.sum(jnp.where(of_group[:, 0, :], (offs + group_rows)[None, :], 0), axis=-1)
    fill = jnp.clip(group_end - row0[:, 0], 0, GROUP_TILE).astype(jnp.int32)
    flat = lambda a: a.reshape(-1)
    return ((tile_group, n_used.reshape(1), first, last, fill), (flat(run_len), flat(run_src), flat(run_dst)),
            max_tiles * GROUP_TILE)


def _mixer(x, mod4, mod_row, is_grid, s_f0, s_b0, p):
    norm_mix_g, w_in_bf, conv_w, decay_rows = p
    y_conv, *qkvg = _inproj(x, mod4, mod_row, norm_mix_g, w_in_bf, conv_w, is_grid)
    per_seq = lambda a: a.reshape(x.shape[0], x.shape[1], a.shape[-1])
    ret = _retention(*map(per_seq, qkvg), decay_rows, s_f0, s_b0, emit_state=not is_grid)
    return y_conv, ret[0].reshape(-1, RET_W), ret[1:]


def kernel(x_prompt, x_sample, state_ret_fwd, state_ret_bwd, c, c_ctx, norm_mix_g, norm_ffn_g, w_ada, b_ada, w_in, conv_w, ret_decay_fwd, ret_decay_bwd, w_out, w_router_group, b_router_group, w_router_expert, b_router_expert, w_gate_e, w_up_e, w_down_e, final_norm_g):
    assert norm_mix_g.shape[0] == 1, "single-layer backbone"
    n_lat = c.shape[0]
    ctx_row = n_lat
    mod_rows = 8
    cvec = jnp.concatenate([c, c_ctx[None, :], jnp.zeros((mod_rows - n_lat - 1, D_MODEL), F32)], axis=0)
    mod = _modulation(cvec, w_ada[0], b_ada[0][None, :])
    mod4 = mod.reshape(mod_rows, 6, 1, D_MODEL)

    pad = ROUTER_COLS - N_GROUPS - N_EXPERTS
    w_router = jnp.concatenate(
        [w_router_group[0], w_router_expert[0], jnp.zeros((D_MODEL, pad), F32)], axis=1).astype(BF16)
    b_router = jnp.concatenate([b_router_group[0], b_router_expert[0], jnp.zeros((pad,), F32)])[None, :]
    decay_rows = jnp.broadcast_to(
        jnp.concatenate([ret_decay_fwd[0], ret_decay_bwd[0]])[:, None], (2 * RET_HEADS, LANES))
    p_mix = (norm_mix_g, w_in[0].astype(BF16), conv_w[0], decay_rows)
    w_out_bf = w_out[0].astype(BF16)
    final_g = final_norm_g[None, :]

    ctx_tokens = x_prompt.shape[0] * x_prompt.shape[1]
    lat_tokens = x_sample.shape[0] * x_sample.shape[1]
    ctx_tiles = ctx_tokens // OUTPROJ_TILE
    lat_tiles_per_seq = x_sample.shape[1] // OUTPROJ_TILE
    tile_mod = lambda i: jnp.where(i < ctx_tiles, ctx_row, (i - ctx_tiles) // lat_tiles_per_seq)
    flat = lambda a: a.reshape(-1, a.shape[-1])

    yc_c, yr_c, (s_f, s_b) = _mixer(x_prompt, mod4, lambda b: ctx_row, False, None, None, p_mix)
    yc_l, yr_l, _ = _mixer(x_sample, mod4, lambda b: b, True, state_ret_fwd, state_ret_bwd, p_mix)

    x1, xloc, route, cnt = _outproj((yc_c, yr_c, flat(x_prompt)), (yc_l, yr_l, flat(x_sample)), mod4, tile_mod,
                                    norm_ffn_g, w_out_bf, w_router, b_router)
    tile_tables, runs, slots = _routing_tables(cnt[:, 0, :N_GROUPS])
    ys = _experts(xloc, tile_tables, runs, slots, w_gate_e[0], w_up_e[0], w_down_e[0])
    y_prompt = _combine(ys, runs, x1, route, ctx_tokens, mod4, tile_mod, final_g, 0)
    y_sample = _combine(ys, runs, x1, route, lat_tokens, mod4, tile_mod, final_g, ctx_tiles)
    return (y_prompt.reshape(x_prompt.shape), y_sample.reshape(x_sample.shape),
            s_f.astype(x_prompt.dtype), s_b.astype(x_prompt.dtype))
```

```python
import functools

import jax
import jax.numpy as jnp
import numpy as np
from jax import lax
from jax.experimental import pallas as pl
from jax.experimental.pallas import tpu as pltpu

F32 = jnp.float32
BF16 = jnp.bfloat16

D_MODEL = 1024
GRID_W = 64
CONV_W = 512
RET_HEADS = 4
RET_DK = 128
RET_DV = 128
RET_W = RET_HEADS * RET_DV
QK_W = RET_HEADS * RET_DK
CHUNK = 128
N_GROUPS = 4
EXPERTS_PER_GROUP = 8
N_EXPERTS = N_GROUPS * EXPERTS_PER_GROUP
D_EXPERT = 256
ROPE_BASE = 10000.0
EPS = 1e-6

LANES = 128
TOKEN_TILE = 1024
OUTPROJ_TILE = 512
GROUP_TILE = 1024
EXPERTS_PER_STEP = 4
RET_UNROLL = 8
COMBINE_RING = 3
SUBLANES = 8
XLOC_ROWS = OUTPROJ_TILE + N_GROUPS * SUBLANES
LOCAL_ROWS = OUTPROJ_TILE + LANES
ROW_W = D_MODEL + LANES
ROUTE_GROUP, ROUTE_LOCAL, ROUTE_E1, ROUTE_E2, ROUTE_W1, ROUTE_W2 = range(6)
MOD_COLS = D_MODEL
ROUTER_COLS = LANES
VMEM_LIMIT = 56 * 1024 * 1024


def _silu(x):
    return x * jax.nn.sigmoid(x)


def _rms(x):
    return x * lax.rsqrt(jnp.mean(x * x, axis=-1, keepdims=True) + EPS)


def _bdot(a, b):
    return jnp.dot(a.astype(BF16), b.astype(BF16), preferred_element_type=F32)


def _split3(x):
    hi = x.astype(BF16)
    rest = x - hi.astype(F32)
    mid = rest.astype(BF16)
    return hi, mid, (rest - mid.astype(F32)).astype(BF16)


def _mod_kernel(c_ref, w_ref, b_ref, o_ref):
    res = _bdot(_silu(c_ref[...]), w_ref[...]) + b_ref[...]
    for r in range(res.shape[0]):
        o_ref[r, 0] = res[r:r + 1]


def _modulation(cvec, w_ada, b_ada):
    rows = cvec.shape[0]
    n = w_ada.shape[1]
    return pl.pallas_call(
        _mod_kernel,
        grid=(n // MOD_COLS,),
        in_specs=[
            pl.BlockSpec((rows, D_MODEL), lambda j: (0, 0)),
            pl.BlockSpec((D_MODEL, MOD_COLS), lambda j: (0, j)),
            pl.BlockSpec((1, MOD_COLS), lambda j: (0, j)),
        ],
        out_specs=pl.BlockSpec((rows, 1, 1, MOD_COLS), lambda j: (0, j, 0, 0)),
        out_shape=jax.ShapeDtypeStruct((rows, n // MOD_COLS, 1, MOD_COLS), F32),
        compiler_params=pltpu.CompilerParams(vmem_limit_bytes=VMEM_LIMIT),
        name="modulation",
    )(cvec, w_ada, b_ada)


def _inproj_kernel(seg, is_grid, x_ref, sh_ref, sc_ref, ng_ref, w_ref, cw_ref, *rest):
    if is_grid:
        cos_ref, sa_ref, sb_ref, yc_ref, q_ref, k_ref, v_ref, g_ref = rest
    else:
        yc_ref, q_ref, k_ref, v_ref, g_ref = rest
    x = x_ref[...]
    xn = (_rms(x) * ng_ref[...]) * (1.0 + sc_ref[0, 0]) + sh_ref[0, 0]
    xb = xn.astype(BF16)

    def proj(c0, n):
        return jnp.dot(xb, w_ref[:, c0:c0 + n], preferred_element_type=F32)

    gate_b = proj(0, CONV_W)
    u = proj(CONV_W, CONV_W) * proj(2 * CONV_W, CONV_W)
    rows = u.shape[0]
    pos = lax.broadcasted_iota(jnp.int32, u.shape, 0) & (seg - 1)
    u_prev = jnp.where(pos != 0, pltpu.roll(u, 1, 0), 0.0)
    u_next = jnp.where(pos != seg - 1, pltpu.roll(u, rows - 1, 0), 0.0)
    conv = cw_ref[0:1, :] * u_prev + cw_ref[1:2, :] * u + cw_ref[2:3, :] * u_next
    yc_ref[...] = (gate_b * conv).astype(yc_ref.dtype)

    q0 = 3 * CONV_W
    q = proj(q0, QK_W)
    k = proj(q0 + QK_W, QK_W)
    if is_grid:
        cos, sa, sb = cos_ref[...], sa_ref[...], sb_ref[...]

        def rope(t):
            out = []
            for h in range(RET_HEADS):
                th = t[:, h * RET_DK:(h + 1) * RET_DK]
                out.append(th * cos + pltpu.roll(th, RET_DK - 1, 1) * sa + pltpu.roll(th, 1, 1) * sb)
            return jnp.concatenate(out, axis=1)

        q, k = rope(q), rope(k)
    q_ref[...] = q
    k_ref[...] = k
    v_ref[...] = proj(q0 + 2 * QK_W, RET_W)
    g_ref[...] = proj(q0 + 2 * QK_W + RET_W, RET_W)


def _rope_tables(length):
    pos = np.arange(length)
    row = (pos // GRID_W).astype(np.float64)
    col = (pos % GRID_W).astype(np.float64)
    n_pairs = RET_DK // 4
    freqs = ROPE_BASE ** (-(np.arange(n_pairs, dtype=np.float64) * 2.0 / (RET_DK // 2)))
    ang = np.concatenate([row[:, None] * freqs, col[:, None] * freqs], axis=-1)
    cos = np.repeat(np.cos(ang), 2, axis=-1)
    sin = np.repeat(np.sin(ang), 2, axis=-1)
    even = (np.arange(RET_DK) % 2) == 0
    return tuple(jnp.asarray(t, F32) for t in (cos, np.where(even, -sin, 0.0), np.where(even, 0.0, sin)))


def _inproj(x, mod4, mod_row, norm_g, w_in_bf, conv_w, is_grid):
    bsz, length, _ = x.shape
    seg = GRID_W if is_grid else length
    assert TOKEN_TILE % seg == 0 and (length % TOKEN_TILE == 0 or TOKEN_TILE % length == 0)
    tokens = bsz * length
    tiles_per_seq = max(length // TOKEN_TILE, 1)
    seqs_per_tile = max(TOKEN_TILE // length, 1)
    batch_of = lambda i: (i // tiles_per_seq) * seqs_per_tile

    def mod_spec(which):
        return pl.BlockSpec((1, 1, 1, D_MODEL), lambda i: (mod_row(batch_of(i)), which, 0, 0))

    def tok_spec(width):
        return pl.BlockSpec((TOKEN_TILE, width), lambda i: (i, 0))

    in_specs = [
        tok_spec(D_MODEL), mod_spec(0), mod_spec(1),
        pl.BlockSpec((1, D_MODEL), lambda i: (0, 0)),
        pl.BlockSpec(w_in_bf.shape, lambda i: (0, 0)),
        pl.BlockSpec(conv_w.shape, lambda i: (0, 0)),
    ]
    args = [x.reshape(tokens, D_MODEL), mod4, mod4, norm_g, w_in_bf, conv_w]
    if is_grid:
        assert length % TOKEN_TILE == 0
        in_specs += [pl.BlockSpec((TOKEN_TILE, RET_DK), lambda i: (i % tiles_per_seq, 0))] * 3
        args += list(_rope_tables(length))
    shp = lambda w, dt: jax.ShapeDtypeStruct((tokens, w), dt)
    return pl.pallas_call(
        functools.partial(_inproj_kernel, seg, is_grid),
        grid=(tokens // TOKEN_TILE,),
        in_specs=in_specs,
        out_specs=[tok_spec(CONV_W), tok_spec(QK_W), tok_spec(QK_W), tok_spec(RET_W), tok_spec(RET_W)],
        out_shape=[shp(CONV_W, BF16), shp(QK_W, F32), shp(QK_W, F32), shp(RET_W, F32), shp(RET_W, F32)],
        compiler_params=pltpu.CompilerParams(
            dimension_semantics=("parallel",), vmem_limit_bytes=VMEM_LIMIT,
            allow_input_fusion=[a is w_in_bf for a in args]),
        name="inproj_grid" if is_grid else "inproj_seq",
    )(*args)


def _ret_kernel(n_chunks, heads, has_init, emit_state, a_ref, q_ref, k_ref, v_ref, g_ref, *rest):
    rest = list(rest)
    if has_init:
        sf0_ref, sb0_ref = rest[:2]
        rest = rest[2:]
    y_ref = rest.pop(0)
    if emit_state:
        sf_out, sb_out = rest[:2]
        rest = rest[2:]
    st_f, st_b, dec = rest
    c = CHUNK
    sq = (c, c)
    head0 = pl.program_id(0) * heads

    def log_decays(hh):
        lg_f = jnp.log1p(-jnp.exp(a_ref[pl.ds(head0 + hh, 1), :]))
        lg_b = jnp.log1p(-jnp.exp(a_ref[pl.ds(head0 + hh + RET_HEADS, 1), :]))
        return lg_f, lg_b

    @pl.when(pl.program_id(1) == 0)
    def _():
        row = lax.broadcasted_iota(jnp.int32, sq, 0).astype(F32)
        col = lax.broadcasted_iota(jnp.int32, sq, 1).astype(F32)
        scale = RET_DK ** -0.5
        for hh in range(heads):
            lg_f, lg_b = log_decays(hh)
            dec[hh, 0] = scale * (
                jnp.where(row >= col, jnp.exp(jnp.where(row >= col, row - col, 0.0) * lg_f), 0.0)
                + jnp.where(col >= row, jnp.exp(jnp.where(col >= row, col - row, 0.0) * lg_b), 0.0))
            dec[hh, 1] = jnp.exp((row + 1.0) * lg_f)
            dec[hh, 2] = jnp.exp((c - row) * lg_b)
            dec[hh, 3] = scale * jnp.exp((c - 1.0 - col) * lg_f)
            dec[hh, 4] = scale * jnp.exp(col * lg_b)

    def rows(n):
        return pl.ds(pl.multiple_of(n * c, c), c) if not isinstance(n, int) else pl.ds(n * c, c)

    def cols(hh):
        return slice(hh * RET_DK, (hh + 1) * RET_DK)

    def kv_step(hh, n):
        kt = jnp.transpose(k_ref[0, rows(n), cols(hh)])
        lhs = jnp.concatenate([kt * dec[hh, 3], kt * dec[hh, 4]], axis=0)
        kv = _bdot(lhs, v_ref[0, rows(n), cols(hh)])
        st_f[hh, n] = kv[:RET_DK]
        st_b[hh, n] = kv[RET_DK:]

    def scan(hh, st, decay, order, s):
        def step(i, s):
            n = order(i)
            kv = st[hh, n]
            st[hh, n] = s
            return s * decay + kv
        if n_chunks <= RET_UNROLL:
            for i in range(n_chunks):
                s = step(i, s)
            return s
        return lax.fori_loop(0, n_chunks, step, s, unroll=RET_UNROLL)

    def out_step(hh, n):
        q = q_ref[0, rows(n), cols(hh)]
        scores = lax.dot_general(q.astype(BF16), k_ref[0, rows(n), cols(hh)].astype(BF16),
                                 (((1,), (1,)), ((), ())), preferred_element_type=F32)
        o = _bdot(scores * dec[hh, 0], v_ref[0, rows(n), cols(hh)])
        q_dec = jnp.concatenate([q * dec[hh, 1], q * dec[hh, 2]], axis=1)
        o = o + _bdot(q_dec, jnp.concatenate([st_f[hh, n], st_b[hh, n]], axis=0))
        y = _silu(g_ref[0, rows(n), cols(hh)]) * _rms(o)
        y_ref[0, rows(n), cols(hh)] = y.astype(y_ref.dtype)

    def over_chunks(step):
        if n_chunks * heads <= RET_UNROLL:
            for hh in range(heads):
                for n in range(n_chunks):
                    step(hh, n)
        else:
            for hh in range(heads):
                lax.fori_loop(0, n_chunks, lambda n, carry: (step(hh, n), carry)[1], 0, unroll=RET_UNROLL)

    over_chunks(kv_step)
    finals = []
    for hh in range(heads):
        lg_f, lg_b = log_decays(hh)
        s_f = sf0_ref[0, 0, hh] if has_init else jnp.zeros(sq, F32)
        s_b = sb0_ref[0, 0, hh] if has_init else jnp.zeros(sq, F32)
        s_f = scan(hh, st_f, jnp.exp(c * lg_f), lambda i: i, s_f)
        s_b = scan(hh, st_b, jnp.exp(c * lg_b), lambda i: n_chunks - 1 - i, s_b)
        finals.append((s_f, s_b))
    over_chunks(out_step)
    if emit_state:
        for hh, (s_f, s_b) in enumerate(finals):
            sf_out[0, 0, hh] = s_f
            sb_out[0, 0, hh] = s_b


def _retention(q, k, v, g, decay_rows, s_f0, s_b0, emit_state):
    bsz, length, _ = q.shape
    n_chunks = length // CHUNK
    has_init = s_f0 is not None
    heads = RET_HEADS if n_chunks * RET_HEADS <= RET_UNROLL else 1
    head_spec = pl.BlockSpec((1, length, heads * RET_DK), lambda h, b: (b, 0, h))
    st_spec = pl.BlockSpec((1, 1, heads, RET_DK, RET_DV), lambda h, b: (b, 0, h, 0, 0))
    in_specs = [pl.BlockSpec(decay_rows.shape, lambda h, b: (0, 0))] + [head_spec] * 4
    args = [decay_rows, q, k, v, g]
    if has_init:
        in_specs += [st_spec, st_spec]
        args += [s_f0, s_b0]
    out_specs = [head_spec]
    out_shape = [jax.ShapeDtypeStruct((bsz, length, RET_W), BF16)]
    if emit_state:
        st_shape = jax.ShapeDtypeStruct((bsz, 1, RET_HEADS, RET_DK, RET_DV), F32)
        out_specs += [st_spec, st_spec]
        out_shape += [st_shape, st_shape]
    return pl.pallas_call(
        functools.partial(_ret_kernel, n_chunks, heads, has_init, emit_state),
        grid=(RET_HEADS // heads, bsz),
        in_specs=in_specs,
        out_specs=out_specs,
        out_shape=out_shape,
        scratch_shapes=[
            pltpu.VMEM((heads, n_chunks, RET_DK, RET_DV), F32),
            pltpu.VMEM((heads, n_chunks, RET_DK, RET_DV), F32),
            pltpu.VMEM((heads, 5, CHUNK, CHUNK), F32),
        ],
        compiler_params=pltpu.CompilerParams(
            dimension_semantics=("arbitrary", "arbitrary"), vmem_limit_bytes=VMEM_LIMIT),
        name="retention_init" if has_init else "retention_zero",
    )(*args)


def _route(logits):
    lane = lax.broadcasted_iota(jnp.int32, logits.shape, 1)
    lane_f = lane.astype(F32)
    neg = -jnp.inf
    far = float(LANES)
    is_g = lane < N_GROUPS
    lg = jnp.where(is_g, logits, neg)
    g_max = jnp.max(lg, axis=1, keepdims=True)
    g_idx = jnp.min(jnp.where(lg == g_max, lane_f, far), axis=1, keepdims=True)
    p_sel = 1.0 / jnp.sum(jnp.where(is_g, jnp.exp(lg - g_max), 0.0), axis=1, keepdims=True)
    lane_group = ((lane - N_GROUPS) >> (EXPERTS_PER_GROUP.bit_length() - 1)).astype(F32)
    sel = (lane >= N_GROUPS) & (lane < N_GROUPS + N_EXPERTS) & (lane_group == g_idx)
    le = jnp.where(sel, logits, neg)
    v1 = jnp.max(le, axis=1, keepdims=True)
    i1 = jnp.min(jnp.where(le == v1, lane_f, far), axis=1, keepdims=True)
    le2 = jnp.where(lane_f == i1, neg, le)
    v2 = jnp.max(le2, axis=1, keepdims=True)
    i2 = jnp.min(jnp.where(le2 == v2, lane_f, far), axis=1, keepdims=True)
    e2 = jnp.exp(v2 - v1)
    w1 = p_sel * (1.0 / (1.0 + e2))
    w2 = p_sel * (e2 / (1.0 + e2))
    return lane, lane_f, g_idx, i1, i2, w1, w2


def _outproj_kernel(ctx_tiles, yc_c, yr_c, x_c, yc_l, yr_l, x_l, *rest):
    @pl.when(pl.program_id(0) < ctx_tiles)
    def _():
        _outproj_tile(yc_c, yr_c, x_c, *rest)

    @pl.when(pl.program_id(0) >= ctx_tiles)
    def _():
        _outproj_tile(yc_l, yr_l, x_l, *rest)


def _outproj_tile(yc_ref, yr_ref, x_ref, g1_ref, sh_ref, sc_ref, ng_ref, wo_ref, wr_ref, br_ref,
                  x1_ref, xloc_ref, route_ref, cnt_ref):
    m = (jnp.dot(yc_ref[...], wo_ref[0:CONV_W, :], preferred_element_type=F32)
         + jnp.dot(yr_ref[...], wo_ref[CONV_W:, :], preferred_element_type=F32))
    x1 = x_ref[...] + g1_ref[0, 0] * m
    x1_ref[...] = x1
    xn = (_rms(x1) * ng_ref[...]) * (1.0 + sc_ref[0, 0]) + sh_ref[0, 0]
    xb = xn.astype(BF16)
    logits = jnp.dot(xb, wr_ref[...], preferred_element_type=F32) + br_ref[...]
    lane, lane_f, g_idx, i1, i2, w1, w2 = _route(logits)

    picks = jnp.where(lane_f == g_idx, 1.0, 0.0)
    rows = picks.shape[0]
    tri = (lax.broadcasted_iota(jnp.int32, (rows, rows), 0)
           > lax.broadcasted_iota(jnp.int32, (rows, rows), 1))
    before = jnp.dot(jnp.where(tri, 1.0, 0.0).astype(BF16), picks.astype(BF16),
                     preferred_element_type=F32)
    count = jnp.sum(picks, axis=0, keepdims=True)
    cnt_ref[0] = count
    count8 = jnp.broadcast_to(jnp.floor((count + (SUBLANES - 1)) * (1.0 / SUBLANES)) * SUBLANES,
                              (SUBLANES, LANES))
    lane8 = lane[:SUBLANES]
    start = sum(jnp.where(lane8 >= k, pltpu.roll(count8, k, 1), 0.0) for k in range(1, N_GROUPS))
    local = jnp.sum(jnp.where(lane_f == g_idx, before + start[0:1], 0.0), axis=1, keepdims=True)
    route = jnp.where(lane == ROUTE_GROUP, g_idx, jnp.where(lane == ROUTE_LOCAL, local, jnp.where(
        lane == ROUTE_E1, i1 - N_GROUPS, jnp.where(lane == ROUTE_E2, i2 - N_GROUPS, jnp.where(
            lane == ROUTE_W1, w1, jnp.where(lane == ROUTE_W2, w2, 0.0))))))
    route_ref[...] = route

    local_row = jnp.transpose(jnp.broadcast_to(local, (rows, LANES)))[0:1, :]
    place = jnp.where(lax.broadcasted_iota(jnp.int32, (XLOC_ROWS, rows), 0).astype(F32) == local_row,
                      1.0, 0.0).astype(BF16)
    xloc_ref[0, :, :D_MODEL] = jnp.dot(place, xb, preferred_element_type=F32)
    xloc_ref[0, :, D_MODEL:] = sum(
        jnp.dot(place, piece, preferred_element_type=F32) for piece in _split3(route))


def _outproj(ctx, lat, mod4, mod_row_of_tile, norm_g, w_out_bf, w_router_bf, b_router):
    ctx_tiles = ctx[2].shape[0] // OUTPROJ_TILE
    tiles = ctx_tiles + lat[2].shape[0] // OUTPROJ_TILE
    tokens = tiles * OUTPROJ_TILE

    def mod_spec(which):
        return pl.BlockSpec((1, 1, 1, D_MODEL), lambda i: (mod_row_of_tile(i), which, 0, 0))

    ctx_tok = lambda w: pl.BlockSpec((OUTPROJ_TILE, w), lambda i: (jnp.minimum(i, ctx_tiles - 1), 0))
    lat_tok = lambda w: pl.BlockSpec((OUTPROJ_TILE, w), lambda i: (jnp.maximum(i - ctx_tiles, 0), 0))
    tok = lambda w: pl.BlockSpec((OUTPROJ_TILE, w), lambda i: (i, 0))
    full = lambda a: pl.BlockSpec(a.shape, lambda i: (0,) * a.ndim)
    widths = (CONV_W, RET_W, D_MODEL)
    return pl.pallas_call(
        functools.partial(_outproj_kernel, ctx_tiles),
        grid=(tiles,),
        in_specs=[ctx_tok(w) for w in widths] + [lat_tok(w) for w in widths] + [
            mod_spec(2), mod_spec(3), mod_spec(4),
            full(norm_g), full(w_out_bf), full(w_router_bf), full(b_router)],
        out_specs=[tok(D_MODEL),
                   pl.BlockSpec((1, XLOC_ROWS, ROW_W), lambda i: (i, 0, 0)),
                   tok(ROUTER_COLS),
                   pl.BlockSpec((1, 1, ROUTER_COLS), lambda i: (i, 0, 0))],
        out_shape=[jax.ShapeDtypeStruct((tokens, D_MODEL), F32),
                   jax.ShapeDtypeStruct((tiles, XLOC_ROWS, ROW_W), F32),
                   jax.ShapeDtypeStruct((tokens, ROUTER_COLS), F32),
                   jax.ShapeDtypeStruct((tiles, 1, ROUTER_COLS), F32)],
        compiler_params=pltpu.CompilerParams(
            dimension_semantics=("parallel",), vmem_limit_bytes=VMEM_LIMIT,
            allow_input_fusion=[False] * 10 + [True, True, False]),
        name="outproj",
    )(*ctx, *lat, mod4, mod4, mod4, norm_g, w_out_bf, w_router_bf, b_router)


RUN_PIECES = tuple(SUBLANES << b for b in reversed(range((OUTPROJ_TILE // SUBLANES).bit_length())))


def _expert_kernel(tile_group_ref, n_used_ref, first_ref, last_ref, fill_ref,
                   run_len_ref, run_src_ref, run_dst_ref,
                   xloc_hbm, w1_ref, w3_ref, w2_ref, ys_ref, xbuf, xb, gate_tabs, sem):
    j = pl.program_id(0)
    step = pl.program_id(1)
    n_used = n_used_ref[0]

    def tile_fetch(tile):
        slot = tile % 2
        group = tile_group_ref[tile]
        row0 = tile * GROUP_TILE

        def from_token_tile(b, carry):
            run = b * N_GROUPS + group
            lo = jnp.maximum(run_dst_ref[run], row0)
            hi = jnp.minimum(run_dst_ref[run] + run_len_ref[run], row0 + GROUP_TILE)
            n = jnp.maximum(hi - lo, 0)
            src = run_src_ref[run] + lo - run_dst_ref[run]
            dst = lo - row0
            for size in RUN_PIECES:
                done = n & (-2 * size)

                @pl.when((n & size) != 0)
                def _():
                    pltpu.make_async_copy(
                        xloc_hbm.at[b, pl.ds(pl.multiple_of(src + done, SUBLANES), size)],
                        xbuf.at[slot, pl.ds(pl.multiple_of(dst + done, SUBLANES), size)], sem.at[slot]).start()
            return carry

        lax.fori_loop(first_ref[tile], last_ref[tile] + 1, from_token_tile, 0)

    def tile_wait(tile):
        slot = tile % 2
        for size in (GROUP_TILE,) + tuple(GROUP_TILE >> k for k in range(1, (GROUP_TILE // SUBLANES).bit_length())):
            @pl.when((fill_ref[tile] & size) != 0)
            def _():
                pltpu.make_async_copy(xbuf.at[1 - slot, pl.ds(0, size)], xbuf.at[slot, pl.ds(0, size)],
                                      sem.at[slot]).wait()

    @pl.when(j < n_used)
    def _():
        @pl.when(step == 0)
        def _():
            @pl.when(j == 0)
            def _():
                xbuf[...] = jnp.zeros_like(xbuf)
                tile_fetch(j)

            tile_wait(j)
            rows_in = xbuf[j % 2]
            xb[...] = rows_in[:, :D_MODEL].astype(BF16)
            route = rows_in[:, D_MODEL:]
            lane = lax.broadcasted_iota(jnp.int32, route.shape, 1)
            for n, which in enumerate((ROUTE_E1, ROUTE_E2, ROUTE_W1, ROUTE_W2)):
                col = jnp.sum(jnp.where(lane == which, route, 0.0), axis=1, keepdims=True)
                gate_tabs[n] = jnp.broadcast_to(col, route.shape)

            @pl.when(j + 1 < n_used)
            def _():
                tile_fetch(j + 1)

        def evaluate(rows):
            x = xb[:rows]
            total = None
            for s in range(EXPERTS_PER_STEP):
                expert = (tile_group_ref[j] * EXPERTS_PER_GROUP + step * EXPERTS_PER_STEP + s).astype(F32)
                gate = (jnp.where(gate_tabs[0, :rows] == expert, gate_tabs[2, :rows], 0.0)
                        + jnp.where(gate_tabs[1, :rows] == expert, gate_tabs[3, :rows], 0.0))
                hid = _silu(jnp.dot(x, w1_ref[0, s].astype(BF16), preferred_element_type=F32)) * jnp.dot(
                    x, w3_ref[0, s].astype(BF16), preferred_element_type=F32)
                y = jnp.dot(hid.astype(BF16), w2_ref[0, s].astype(BF16), preferred_element_type=F32)
                gated = jnp.concatenate(
                    [gate * y[:, c * LANES:(c + 1) * LANES] for c in range(D_MODEL // LANES)], axis=1)
                total = gated if total is None else total + gated

            @pl.when(step == 0)
            def _():
                ys_ref[:rows] = total
                if rows < GROUP_TILE:
                    ys_ref[rows:] = jnp.zeros((GROUP_TILE - rows, D_MODEL), F32)

            @pl.when(step > 0)
            def _():
                ys_ref[:rows] += total

        half = GROUP_TILE // 2

        @pl.when(fill_ref[j] > half)
        def _():
            evaluate(GROUP_TILE)

        @pl.when(fill_ref[j] <= half)
        def _():
            evaluate(half)

    @pl.when((j >= n_used) & (step == 0))
    def _():
        ys_ref[...] = jnp.zeros_like(ys_ref)


def _experts(xloc, tile_tables, runs, slots, w1, w3, w2):
    steps = EXPERTS_PER_GROUP // EXPERTS_PER_STEP
    per_step = lambda w: w.reshape((N_EXPERTS // EXPERTS_PER_STEP, EXPERTS_PER_STEP) + w.shape[1:])
    w_spec = lambda shape: pl.BlockSpec((1, EXPERTS_PER_STEP) + shape, lambda j, s, tg, nu, *_: (
        tg[jnp.minimum(j, nu[0] - 1)] * steps + jnp.where(j < nu[0], s, steps - 1), 0, 0, 0))
    grid_spec = pltpu.PrefetchScalarGridSpec(
        num_scalar_prefetch=8,
        grid=(slots // GROUP_TILE, steps),
        in_specs=[
            pl.BlockSpec(memory_space=pl.ANY),
            w_spec((D_MODEL, D_EXPERT)), w_spec((D_MODEL, D_EXPERT)), w_spec((D_EXPERT, D_MODEL)),
        ],
        out_specs=pl.BlockSpec((GROUP_TILE, D_MODEL), lambda j, s, *_: (j, 0)),
        scratch_shapes=[pltpu.VMEM((2, GROUP_TILE, ROW_W), F32), pltpu.VMEM((GROUP_TILE, D_MODEL), BF16),
                        pltpu.VMEM((4, GROUP_TILE, LANES), F32), pltpu.SemaphoreType.DMA((2,))],
    )
    return pl.pallas_call(
        _expert_kernel,
        grid_spec=grid_spec,
        out_shape=jax.ShapeDtypeStruct((slots, D_MODEL), F32),
        compiler_params=pltpu.CompilerParams(
            dimension_semantics=("arbitrary", "arbitrary"), vmem_limit_bytes=VMEM_LIMIT),
        name="experts",
    )(*tile_tables, *runs, xloc, per_step(w1), per_step(w3), per_step(w2))


def _combine_kernel(n_tiles, tile_base, run_len_ref, run_src_ref, run_dst_ref,
                    route_ref, g2_ref, fg_ref, x1_hbm, ys_hbm, o_ref, buf, x1_buf, sem, x1_sem):
    i = pl.program_id(0)
    slot = i % COMBINE_RING

    def x1_copy(local_tile):
        s = local_tile % COMBINE_RING
        rows = pl.ds(pl.multiple_of((local_tile + tile_base) * OUTPROJ_TILE, OUTPROJ_TILE), OUTPROJ_TILE)
        return pltpu.make_async_copy(x1_hbm.at[rows], x1_buf.at[s], x1_sem.at[s])

    def run_copies(local_tile, act):
        s = local_tile % COMBINE_RING
        tile = local_tile + tile_base
        for g in range(N_GROUPS):
            n = run_len_ref[tile * N_GROUPS + g]
            src = run_src_ref[tile * N_GROUPS + g]
            dst = run_dst_ref[tile * N_GROUPS + g]
            for size in RUN_PIECES:
                done = n & (-2 * size)

                @pl.when((n & size) != 0)
                def _():
                    act(pltpu.make_async_copy(
                        ys_hbm.at[pl.ds(pl.multiple_of(dst + done, SUBLANES), size)],
                        buf.at[s, pl.ds(pl.multiple_of(src + done, SUBLANES), size)], sem.at[s]))

    def fetch(local_tile):
        run_copies(local_tile, lambda cp: cp.start())
        x1_copy(local_tile).start()

    @pl.when(i == 0)
    def _():
        buf[...] = jnp.zeros_like(buf)
        for ahead in range(min(COMBINE_RING - 1, n_tiles)):
            fetch(i + ahead)

    @pl.when(i + COMBINE_RING - 1 < n_tiles)
    def _():
        fetch(i + COMBINE_RING - 1)

    run_copies(i, lambda cp: cp.wait())
    x1_copy(i).wait()
    route = route_ref[...]
    lane = lax.broadcasted_iota(jnp.int32, route.shape, 1)
    local = jnp.sum(jnp.where(lane == ROUTE_LOCAL, route, 0.0), axis=1, keepdims=True)
    pick = jnp.where(lax.broadcasted_iota(jnp.int32, (route.shape[0], LOCAL_ROWS), 1).astype(F32) == local,
                     1.0, 0.0).astype(BF16)
    moe = sum(jnp.dot(pick, piece, preferred_element_type=F32) for piece in _split3(buf[slot])[:2])
    y = x1_buf[slot] + g2_ref[0, 0] * moe
    o_ref[...] = _rms(y) * fg_ref[...]


def _combine(ys, runs, x1, route, tokens, mod4, mod_row_of_tile, final_g, tile_base):
    tiles = tokens // OUTPROJ_TILE
    tok = lambda w: pl.BlockSpec((OUTPROJ_TILE, w), lambda i, *_: (i + tile_base, 0))
    grid_spec = pltpu.PrefetchScalarGridSpec(
        num_scalar_prefetch=3,
        grid=(tiles,),
        in_specs=[
            tok(ROUTER_COLS),
            pl.BlockSpec((1, 1, 1, D_MODEL), lambda i, *_: (mod_row_of_tile(i + tile_base), 5, 0, 0)),
            pl.BlockSpec((1, D_MODEL), lambda i, *_: (0, 0)),
            pl.BlockSpec(memory_space=pl.ANY),
            pl.BlockSpec(memory_space=pl.ANY),
        ],
        out_specs=pl.BlockSpec((OUTPROJ_TILE, D_MODEL), lambda i, *_: (i, 0)),
        scratch_shapes=[pltpu.VMEM((COMBINE_RING, LOCAL_ROWS, D_MODEL), F32),
                        pltpu.VMEM((COMBINE_RING, OUTPROJ_TILE, D_MODEL), F32),
                        pltpu.SemaphoreType.DMA((COMBINE_RING,)), pltpu.SemaphoreType.DMA((COMBINE_RING,))],
    )
    return pl.pallas_call(
        functools.partial(_combine_kernel, tiles, tile_base),
        grid_spec=grid_spec,
        out_shape=jax.ShapeDtypeStruct((tokens, D_MODEL), F32),
        compiler_params=pltpu.CompilerParams(
            dimension_semantics=("arbitrary",), vmem_limit_bytes=VMEM_LIMIT),
        name="combine",
    )(*runs, route, mod4, final_g, x1, ys)


def _routing_tables(counts):
    counts = counts.astype(jnp.int32)
    tiles = counts.shape[0]
    run_len = ((counts + SUBLANES - 1) // SUBLANES) * SUBLANES
    run_src = jnp.cumsum(run_len, axis=1) - run_len
    group_rows = jnp.sum(run_len, axis=0)
    padded = ((group_rows + GROUP_TILE - 1) // GROUP_TILE) * GROUP_TILE
    ends = jnp.cumsum(padded)
    offs = ends - padded
    run_dst = offs[None, :] + jnp.cumsum(run_len, axis=0) - run_len

    n_used = ends[-1] // GROUP_TILE
    max_rows = tiles * (OUTPROJ_TILE + N_GROUPS * (SUBLANES - 1))
    max_tiles = max_rows // GROUP_TILE + N_GROUPS
    tile_ids = jnp.minimum(jnp.arange(max_tiles, dtype=jnp.int32), n_used - 1)
    tile_group = jnp.sum(tile_ids[:, None] * GROUP_TILE >= ends[None, :], axis=1).astype(jnp.int32)
    of_group = (tile_group[:, None] == jnp.arange(N_GROUPS, dtype=jnp.int32))[:, None, :]
    start = jnp.sum(jnp.where(of_group, run_dst[None], 0), axis=-1)
    stop = start + jnp.sum(jnp.where(of_group, run_len[None], 0), axis=-1)
    row0 = (tile_ids * GROUP_TILE)[:, None]
    first = jnp.sum(stop <= row0, axis=1).astype(jnp.int32)
    last = jnp.sum(start < row0 + GROUP_TILE, axis=1).astype(jnp.int32) - 1
    group_end = jnp.sum(jnp.where(of_group[:, 0, :], (offs + group_rows)[None, :], 0), axis=-1)
    fill = jnp.clip(group_end - row0[:, 0], 0, GROUP_TILE).astype(jnp.int32)
    flat = lambda a: a.reshape(-1)
    return ((tile_group, n_used.reshape(1), first, last, fill), (flat(run_len), flat(run_src), flat(run_dst)),
            max_tiles * GROUP_TILE)


def _mixer(x, mod4, mod_row, is_grid, s_f0, s_b0, p):
    norm_mix_g, w_in_bf, conv_w, decay_rows = p
    y_conv, *qkvg = _inproj(x, mod4, mod_row, norm_mix_g, w_in_bf, conv_w, is_grid)
    per_seq = lambda a: a.reshape(x.shape[0], x.shape[1], a.shape[-1])
    ret = _retention(*map(per_seq, qkvg), decay_rows, s_f0, s_b0, emit_state=not is_grid)
    return y_conv, ret[0].reshape(-1, RET_W), ret[1:]


def kernel(x_prompt, x_sample, state_ret_fwd, state_ret_bwd, c, c_ctx, norm_mix_g, norm_ffn_g, w_ada, b_ada, w_in, conv_w, ret_decay_fwd, ret_decay_bwd, w_out, w_router_group, b_router_group, w_router_expert, b_router_expert, w_gate_e, w_up_e, w_down_e, final_norm_g):
    assert norm_mix_g.shape[0] == 1, "single-layer backbone"
    n_lat = c.shape[0]
    ctx_row = n_lat
    mod_rows = 8
    cvec = jnp.concatenate([c, c_ctx[None, :], jnp.zeros((mod_rows - n_lat - 1, D_MODEL), F32)], axis=0)
    mod4 = _modulation(cvec, w_ada[0], b_ada[0][None, :])

    pad = ROUTER_COLS - N_GROUPS - N_EXPERTS
    w_router = jnp.concatenate(
        [w_router_group[0], w_router_expert[0], jnp.zeros((D_MODEL, pad), F32)], axis=1).astype(BF16)
    b_router = jnp.concatenate([b_router_group[0], b_router_expert[0], jnp.zeros((pad,), F32)])[None, :]
    decay_rows = jnp.broadcast_to(
        jnp.concatenate([ret_decay_fwd[0], ret_decay_bwd[0]])[:, None], (2 * RET_HEADS, LANES))
    p_mix = (norm_mix_g, w_in[0].astype(BF16), conv_w[0], decay_rows)
    w_out_bf = w_out[0].astype(BF16)
    final_g = final_norm_g[None, :]

    ctx_tokens = x_prompt.shape[0] * x_prompt.shape[1]
    lat_tokens = x_sample.shape[0] * x_sample.shape[1]
    ctx_tiles = ctx_tokens // OUTPROJ_TILE
    lat_tiles_per_seq = x_sample.shape[1] // OUTPROJ_TILE
    tile_mod = lambda i: jnp.where(i < ctx_tiles, ctx_row, (i - ctx_tiles) // lat_tiles_per_seq)
    flat = lambda a: a.reshape(-1, a.shape[-1])

    yc_c, yr_c, (s_f, s_b) = _mixer(x_prompt, mod4, lambda b: ctx_row, False, None, None, p_mix)
    yc_l, yr_l, _ = _mixer(x_sample, mod4, lambda b: b, True, state_ret_fwd, state_ret_bwd, p_mix)

    x1, xloc, route, cnt = _outproj((yc_c, yr_c, flat(x_prompt)), (yc_l, yr_l, flat(x_sample)), mod4, tile_mod,
                                    norm_ffn_g, w_out_bf, w_router, b_router)
    tile_tables, runs, slots = _routing_tables(cnt[:, 0, :N_GROUPS])
    ys = _experts(xloc, tile_tables, runs, slots, w_gate_e[0], w_up_e[0], w_down_e[0])
    y_prompt = _combine(ys, runs, x1, route, ctx_tokens, mod4, tile_mod, final_g, 0)
    y_sample = _combine(ys, runs, x1, route, lat_tokens, mod4, tile_mod, final_g, ctx_tiles)
    return (y_prompt.reshape(x_prompt.shape), y_sample.reshape(x_sample.shape),
            s_f.astype(x_prompt.dtype), s_b.astype(x_prompt.dtype))
```

```python
import functools

import jax
import jax.numpy as jnp
import numpy as np
from jax import lax
from jax.experimental import pallas as pl
from jax.experimental.pallas import tpu as pltpu

F32 = jnp.float32
BF16 = jnp.bfloat16

D_MODEL = 1024
GRID_W = 64
CONV_W = 512
RET_HEADS = 4
RET_DK = 128
RET_DV = 128
RET_W = RET_HEADS * RET_DV
QK_W = RET_HEADS * RET_DK
CHUNK = 128
N_GROUPS = 4
EXPERTS_PER_GROUP = 8
N_EXPERTS = N_GROUPS * EXPERTS_PER_GROUP
D_EXPERT = 256
ROPE_BASE = 10000.0
EPS = 1e-6

LANES = 128
TOKEN_TILE = 1024
OUTPROJ_TILE = 512
GROUP_TILE = 1024
EXPERTS_PER_STEP = 4
RET_UNROLL = 8
COMBINE_RING = 3
SUBLANES = 8
XLOC_ROWS = OUTPROJ_TILE + N_GROUPS * SUBLANES
LOCAL_ROWS = OUTPROJ_TILE + LANES
ROW_W = D_MODEL + LANES
ROUTE_GROUP, ROUTE_LOCAL, ROUTE_E1, ROUTE_E2, ROUTE_W1, ROUTE_W2 = range(6)
MOD_COLS = D_MODEL
CAST_ROWS = 128
ROUTER_COLS = LANES
VMEM_LIMIT = 56 * 1024 * 1024


def _silu(x):
    return x * jax.nn.sigmoid(x)


def _rms(x):
    return x * lax.rsqrt(jnp.mean(x * x, axis=-1, keepdims=True) + EPS)


def _bdot(a, b):
    return jnp.dot(a.astype(BF16), b.astype(BF16), preferred_element_type=F32)


def _split3(x):
    hi = x.astype(BF16)
    rest = x - hi.astype(F32)
    mid = rest.astype(BF16)
    return hi, mid, (rest - mid.astype(F32)).astype(BF16)


def _mod_kernel(c_ref, w_ref, b_ref, o_ref):
    res = _bdot(_silu(c_ref[...]), w_ref[...]) + b_ref[...]
    for r in range(res.shape[0]):
        o_ref[r, 0] = res[r:r + 1]


def _modulation(cvec, w_ada, b_ada):
    rows = cvec.shape[0]
    n = w_ada.shape[1]
    return pl.pallas_call(
        _mod_kernel,
        grid=(n // MOD_COLS,),
        in_specs=[
            pl.BlockSpec((rows, D_MODEL), lambda j: (0, 0)),
            pl.BlockSpec((D_MODEL, MOD_COLS), lambda j: (0, j)),
            pl.BlockSpec((1, MOD_COLS), lambda j: (0, j)),
        ],
        out_specs=pl.BlockSpec((rows, 1, 1, MOD_COLS), lambda j: (0, j, 0, 0)),
        out_shape=jax.ShapeDtypeStruct((rows, n // MOD_COLS, 1, MOD_COLS), F32),
        compiler_params=pltpu.CompilerParams(vmem_limit_bytes=VMEM_LIMIT),
        name="modulation",
    )(cvec, w_ada, b_ada)


def _inproj_kernel(seg, is_grid, cast_w, x_ref, sh_ref, sc_ref, ng_ref, w_ref, cw_ref, *rest):
    if cast_w:
        *rest, wout_ref, wbf_ref, stage_ref, sem = rest
        w_hbm = w_ref
        n_chunks = w_hbm.shape[0] // CAST_ROWS
        export = pltpu.make_async_copy(wbf_ref, wout_ref, sem.at[2])

        def chunk(c):
            return pltpu.make_async_copy(w_hbm.at[pl.ds(c * CAST_ROWS, CAST_ROWS)], stage_ref.at[c % 2],
                                         sem.at[c % 2])

        @pl.when(pl.program_id(0) == 0)
        def _():
            chunk(0).start()
            chunk(1).start()
            for c in range(n_chunks):
                chunk(c).wait()
                wbf_ref[pl.ds(c * CAST_ROWS, CAST_ROWS), :] = stage_ref[c % 2].astype(BF16)
                if c + 2 < n_chunks:
                    chunk(c + 2).start()
            export.start()

        @pl.when(pl.program_id(0) == pl.num_programs(0) - 1)
        def _():
            export.wait()

        w_ref = wbf_ref
    if is_grid:
        cos_ref, sa_ref, sb_ref, yc_ref, q_ref, k_ref, v_ref, g_ref = rest
    else:
        yc_ref, q_ref, k_ref, v_ref, g_ref = rest
    x = x_ref[...]
    xn = (_rms(x) * ng_ref[...]) * (1.0 + sc_ref[0, 0]) + sh_ref[0, 0]
    xb = xn.astype(BF16)

    def proj(c0, n):
        return jnp.dot(xb, w_ref[:, c0:c0 + n], preferred_element_type=F32)

    gate_b = proj(0, CONV_W)
    u = proj(CONV_W, CONV_W) * proj(2 * CONV_W, CONV_W)
    rows = u.shape[0]
    pos = lax.broadcasted_iota(jnp.int32, u.shape, 0) & (seg - 1)
    u_prev = jnp.where(pos != 0, pltpu.roll(u, 1, 0), 0.0)
    u_next = jnp.where(pos != seg - 1, pltpu.roll(u, rows - 1, 0), 0.0)
    conv = cw_ref[0:1, :] * u_prev + cw_ref[1:2, :] * u + cw_ref[2:3, :] * u_next
    yc_ref[...] = (gate_b * conv).astype(yc_ref.dtype)

    q0 = 3 * CONV_W
    q = proj(q0, QK_W)
    k = proj(q0 + QK_W, QK_W)
    if is_grid:
        cos, sa, sb = cos_ref[...], sa_ref[...], sb_ref[...]

        def rope(t):
            out = []
            for h in range(RET_HEADS):
                th = t[:, h * RET_DK:(h + 1) * RET_DK]
                out.append(th * cos + pltpu.roll(th, RET_DK - 1, 1) * sa + pltpu.roll(th, 1, 1) * sb)
            return jnp.concatenate(out, axis=1)

        q, k = rope(q), rope(k)
    q_ref[...] = q
    k_ref[...] = k
    v_ref[...] = proj(q0 + 2 * QK_W, RET_W)
    g_ref[...] = proj(q0 + 2 * QK_W + RET_W, RET_W)


def _rope_tables(length):
    pos = np.arange(length)
    row = (pos // GRID_W).astype(np.float64)
    col = (pos % GRID_W).astype(np.float64)
    n_pairs = RET_DK // 4
    freqs = ROPE_BASE ** (-(np.arange(n_pairs, dtype=np.float64) * 2.0 / (RET_DK // 2)))
    ang = np.concatenate([row[:, None] * freqs, col[:, None] * freqs], axis=-1)
    cos = np.repeat(np.cos(ang), 2, axis=-1)
    sin = np.repeat(np.sin(ang), 2, axis=-1)
    even = (np.arange(RET_DK) % 2) == 0
    return tuple(jnp.asarray(t, F32) for t in (cos, np.where(even, -sin, 0.0), np.where(even, 0.0, sin)))


def _inproj(x, mod4, mod_row, norm_g, w_in, conv_w, is_grid):
    bsz, length, _ = x.shape
    seg = GRID_W if is_grid else length
    assert TOKEN_TILE % seg == 0 and (length % TOKEN_TILE == 0 or TOKEN_TILE % length == 0)
    tokens = bsz * length
    tiles_per_seq = max(length // TOKEN_TILE, 1)
    seqs_per_tile = max(TOKEN_TILE // length, 1)
    batch_of = lambda i: (i // tiles_per_seq) * seqs_per_tile

    def mod_spec(which):
        return pl.BlockSpec((1, 1, 1, D_MODEL), lambda i: (mod_row(batch_of(i)), which, 0, 0))

    def tok_spec(width):
        return pl.BlockSpec((TOKEN_TILE, width), lambda i: (i, 0))

    cast_w = w_in.dtype != BF16
    in_specs = [
        tok_spec(D_MODEL), mod_spec(0), mod_spec(1),
        pl.BlockSpec((1, D_MODEL), lambda i: (0, 0)),
        pl.BlockSpec(memory_space=pl.ANY) if cast_w else pl.BlockSpec(w_in.shape, lambda i: (0, 0)),
        pl.BlockSpec(conv_w.shape, lambda i: (0, 0)),
    ]
    args = [x.reshape(tokens, D_MODEL), mod4, mod4, norm_g, w_in, conv_w]
    if is_grid:
        assert length % TOKEN_TILE == 0
        in_specs += [pl.BlockSpec((TOKEN_TILE, RET_DK), lambda i: (i % tiles_per_seq, 0))] * 3
        args += list(_rope_tables(length))
    shp = lambda w, dt: jax.ShapeDtypeStruct((tokens, w), dt)
    out_specs = [tok_spec(CONV_W), tok_spec(QK_W), tok_spec(QK_W), tok_spec(RET_W), tok_spec(RET_W)]
    out_shape = [shp(CONV_W, BF16), shp(QK_W, F32), shp(QK_W, F32), shp(RET_W, F32), shp(RET_W, F32)]
    scratch = []
    if cast_w:
        out_specs.append(pl.BlockSpec(memory_space=pl.ANY))
        out_shape.append(jax.ShapeDtypeStruct(w_in.shape, BF16))
        scratch = [pltpu.VMEM(w_in.shape, BF16), pltpu.VMEM((2, CAST_ROWS, w_in.shape[1]), F32),
                   pltpu.SemaphoreType.DMA((3,))]
    outs = pl.pallas_call(
        functools.partial(_inproj_kernel, seg, is_grid, cast_w),
        grid=(tokens // TOKEN_TILE,),
        in_specs=in_specs,
        out_specs=out_specs,
        out_shape=out_shape,
        scratch_shapes=scratch,
        compiler_params=pltpu.CompilerParams(
            dimension_semantics=("arbitrary" if cast_w else "parallel",), vmem_limit_bytes=VMEM_LIMIT),
        name="inproj_grid" if is_grid else "inproj_seq",
    )(*args)
    return outs if cast_w else [*outs, w_in]


def _ret_kernel(n_chunks, heads, has_init, emit_state, a_ref, q_ref, k_ref, v_ref, g_ref, *rest):
    rest = list(rest)
    if has_init:
        sf0_ref, sb0_ref = rest[:2]
        rest = rest[2:]
    y_ref = rest.pop(0)
    if emit_state:
        sf_out, sb_out = rest[:2]
        rest = rest[2:]
    st_f, st_b, dec = rest
    c = CHUNK
    sq = (c, c)
    head0 = pl.program_id(0) * heads

    def log_decays(hh):
        lg_f = jnp.log1p(-jnp.exp(a_ref[pl.ds(head0 + hh, 1), :]))
        lg_b = jnp.log1p(-jnp.exp(a_ref[pl.ds(head0 + hh + RET_HEADS, 1), :]))
        return lg_f, lg_b

    @pl.when(pl.program_id(1) == 0)
    def _():
        row = lax.broadcasted_iota(jnp.int32, sq, 0).astype(F32)
        col = lax.broadcasted_iota(jnp.int32, sq, 1).astype(F32)
        scale = RET_DK ** -0.5
        for hh in range(heads):
            lg_f, lg_b = log_decays(hh)
            dec[hh, 0] = scale * (
                jnp.where(row >= col, jnp.exp(jnp.where(row >= col, row - col, 0.0) * lg_f), 0.0)
                + jnp.where(col >= row, jnp.exp(jnp.where(col >= row, col - row, 0.0) * lg_b), 0.0))
            dec[hh, 1] = jnp.exp((row + 1.0) * lg_f)
            dec[hh, 2] = jnp.exp((c - row) * lg_b)
            dec[hh, 3] = scale * jnp.exp((c - 1.0 - col) * lg_f)
            dec[hh, 4] = scale * jnp.exp(col * lg_b)

    def rows(n):
        return pl.ds(pl.multiple_of(n * c, c), c) if not isinstance(n, int) else pl.ds(n * c, c)

    def cols(hh):
        return slice(hh * RET_DK, (hh + 1) * RET_DK)

    def kv_step(hh, n):
        kt = jnp.transpose(k_ref[0, rows(n), cols(hh)])
        lhs = jnp.concatenate([kt * dec[hh, 3], kt * dec[hh, 4]], axis=0)
        kv = _bdot(lhs, v_ref[0, rows(n), cols(hh)])
        st_f[hh, n] = kv[:RET_DK]
        st_b[hh, n] = kv[RET_DK:]

    def scan(hh, st, decay, order, s):
        def step(i, s):
            n = order(i)
            kv = st[hh, n]
            st[hh, n] = s
            return s * decay + kv
        if n_chunks <= RET_UNROLL:
            for i in range(n_chunks):
                s = step(i, s)
            return s
        return lax.fori_loop(0, n_chunks, step, s, unroll=RET_UNROLL)

    def out_step(hh, n):
        q = q_ref[0, rows(n), cols(hh)]
        scores = lax.dot_general(q.astype(BF16), k_ref[0, rows(n), cols(hh)].astype(BF16),
                                 (((1,), (1,)), ((), ())), preferred_element_type=F32)
        o = _bdot(scores * dec[hh, 0], v_ref[0, rows(n), cols(hh)])
        q_dec = jnp.concatenate([q * dec[hh, 1], q * dec[hh, 2]], axis=1)
        o = o + _bdot(q_dec, jnp.concatenate([st_f[hh, n], st_b[hh, n]], axis=0))
        y = _silu(g_ref[0, rows(n), cols(hh)]) * _rms(o)
        y_ref[0, rows(n), cols(hh)] = y.astype(y_ref.dtype)

    def over_chunks(step):
        if n_chunks * heads <= RET_UNROLL:
            for hh in range(heads):
                for n in range(n_chunks):
                    step(hh, n)
        else:
            for hh in range(heads):
                lax.fori_loop(0, n_chunks, lambda n, carry: (step(hh, n), carry)[1], 0, unroll=RET_UNROLL)

    over_chunks(kv_step)
    finals = []
    for hh in range(heads):
        lg_f, lg_b = log_decays(hh)
        s_f = sf0_ref[0, 0, hh] if has_init else jnp.zeros(sq, F32)
        s_b = sb0_ref[0, 0, hh] if has_init else jnp.zeros(sq, F32)
        s_f = scan(hh, st_f, jnp.exp(c * lg_f), lambda i: i, s_f)
        s_b = scan(hh, st_b, jnp.exp(c * lg_b), lambda i: n_chunks - 1 - i, s_b)
        finals.append((s_f, s_b))
    over_chunks(out_step)
    if emit_state:
        for hh, (s_f, s_b) in enumerate(finals):
            sf_out[0, 0, hh] = s_f
            sb_out[0, 0, hh] = s_b


def _retention(q, k, v, g, decay_rows, s_f0, s_b0, emit_state):
    bsz, length, _ = q.shape
    n_chunks = length // CHUNK
    has_init = s_f0 is not None
    heads = RET_HEADS if n_chunks * RET_HEADS <= RET_UNROLL else 1
    head_spec = pl.BlockSpec((1, length, heads * RET_DK), lambda h, b: (b, 0, h))
    st_spec = pl.BlockSpec((1, 1, heads, RET_DK, RET_DV), lambda h, b: (b, 0, h, 0, 0))
    in_specs = [pl.BlockSpec(decay_rows.shape, lambda h, b: (0, 0))] + [head_spec] * 4
    args = [decay_rows, q, k, v, g]
    if has_init:
        in_specs += [st_spec, st_spec]
        args += [s_f0, s_b0]
    out_specs = [head_spec]
    out_shape = [jax.ShapeDtypeStruct((bsz, length, RET_W), BF16)]
    if emit_state:
        st_shape = jax.ShapeDtypeStruct((bsz, 1, RET_HEADS, RET_DK, RET_DV), F32)
        out_specs += [st_spec, st_spec]
        out_shape += [st_shape, st_shape]
    return pl.pallas_call(
        functools.partial(_ret_kernel, n_chunks, heads, has_init, emit_state),
        grid=(RET_HEADS // heads, bsz),
        in_specs=in_specs,
        out_specs=out_specs,
        out_shape=out_shape,
        scratch_shapes=[
            pltpu.VMEM((heads, n_chunks, RET_DK, RET_DV), F32),
            pltpu.VMEM((heads, n_chunks, RET_DK, RET_DV), F32),
            pltpu.VMEM((heads, 5, CHUNK, CHUNK), F32),
        ],
        compiler_params=pltpu.CompilerParams(
            dimension_semantics=("arbitrary", "arbitrary"), vmem_limit_bytes=VMEM_LIMIT),
        name="retention_init" if has_init else "retention_zero",
    )(*args)


def _route(logits):
    lane = lax.broadcasted_iota(jnp.int32, logits.shape, 1)
    lane_f = lane.astype(F32)
    neg = -jnp.inf
    far = float(LANES)
    is_g = lane < N_GROUPS
    lg = jnp.where(is_g, logits, neg)
    g_max = jnp.max(lg, axis=1, keepdims=True)
    g_idx = jnp.min(jnp.where(lg == g_max, lane_f, far), axis=1, keepdims=True)
    p_sel = 1.0 / jnp.sum(jnp.where(is_g, jnp.exp(lg - g_max), 0.0), axis=1, keepdims=True)
    lane_group = ((lane - N_GROUPS) >> (EXPERTS_PER_GROUP.bit_length() - 1)).astype(F32)
    sel = (lane >= N_GROUPS) & (lane < N_GROUPS + N_EXPERTS) & (lane_group == g_idx)
    le = jnp.where(sel, logits, neg)
    v1 = jnp.max(le, axis=1, keepdims=True)
    i1 = jnp.min(jnp.where(le == v1, lane_f, far), axis=1, keepdims=True)
    le2 = jnp.where(lane_f == i1, neg, le)
    v2 = jnp.max(le2, axis=1, keepdims=True)
    i2 = jnp.min(jnp.where(le2 == v2, lane_f, far), axis=1, keepdims=True)
    e2 = jnp.exp(v2 - v1)
    w1 = p_sel * (1.0 / (1.0 + e2))
    w2 = p_sel * (e2 / (1.0 + e2))
    return lane, lane_f, g_idx, i1, i2, w1, w2


def _outproj_kernel(ctx_tiles, yc_c, yr_c, x_c, yc_l, yr_l, x_l, *rest):
    @pl.when(pl.program_id(0) < ctx_tiles)
    def _():
        _outproj_tile(yc_c, yr_c, x_c, *rest)

    @pl.when(pl.program_id(0) >= ctx_tiles)
    def _():
        _outproj_tile(yc_l, yr_l, x_l, *rest)


def _outproj_tile(yc_ref, yr_ref, x_ref, g1_ref, sh_ref, sc_ref, ng_ref, wo_ref, wr_ref, br_ref,
                  x1_ref, xloc_ref, route_ref, cnt_ref):
    m = (jnp.dot(yc_ref[...], wo_ref[0:CONV_W, :], preferred_element_type=F32)
         + jnp.dot(yr_ref[...], wo_ref[CONV_W:, :], preferred_element_type=F32))
    x1 = x_ref[...] + g1_ref[0, 0] * m
    x1_ref[...] = x1
    xn = (_rms(x1) * ng_ref[...]) * (1.0 + sc_ref[0, 0]) + sh_ref[0, 0]
    xb = xn.astype(BF16)
    logits = jnp.dot(xb, wr_ref[...], preferred_element_type=F32) + br_ref[...]
    lane, lane_f, g_idx, i1, i2, w1, w2 = _route(logits)

    picks = jnp.where(lane_f == g_idx, 1.0, 0.0)
    rows = picks.shape[0]
    tri = (lax.broadcasted_iota(jnp.int32, (rows, rows), 0)
           > lax.broadcasted_iota(jnp.int32, (rows, rows), 1))
    before = jnp.dot(jnp.where(tri, 1.0, 0.0).astype(BF16), picks.astype(BF16),
                     preferred_element_type=F32)
    count = jnp.sum(picks, axis=0, keepdims=True)
    cnt_ref[0] = count
    count8 = jnp.broadcast_to(jnp.floor((count + (SUBLANES - 1)) * (1.0 / SUBLANES)) * SUBLANES,
                              (SUBLANES, LANES))
    lane8 = lane[:SUBLANES]
    start = sum(jnp.where(lane8 >= k, pltpu.roll(count8, k, 1), 0.0) for k in range(1, N_GROUPS))
    local = jnp.sum(jnp.where(lane_f == g_idx, before + start[0:1], 0.0), axis=1, keepdims=True)
    route = jnp.where(lane == ROUTE_GROUP, g_idx, jnp.where(lane == ROUTE_LOCAL, local, jnp.where(
        lane == ROUTE_E1, i1 - N_GROUPS, jnp.where(lane == ROUTE_E2, i2 - N_GROUPS, jnp.where(
            lane == ROUTE_W1, w1, jnp.where(lane == ROUTE_W2, w2, 0.0))))))
    route_ref[...] = route

    local_row = jnp.transpose(jnp.broadcast_to(local, (rows, LANES)))[0:1, :]
    place = jnp.where(lax.broadcasted_iota(jnp.int32, (XLOC_ROWS, rows), 0).astype(F32) == local_row,
                      1.0, 0.0).astype(BF16)
    xloc_ref[0, :, :D_MODEL] = jnp.dot(place, xb, preferred_element_type=F32)
    xloc_ref[0, :, D_MODEL:] = sum(
        jnp.dot(place, piece, preferred_element_type=F32) for piece in _split3(route))


def _outproj(ctx, lat, mod4, mod_row_of_tile, norm_g, w_out_bf, w_router_bf, b_router):
    ctx_tiles = ctx[2].shape[0] // OUTPROJ_TILE
    tiles = ctx_tiles + lat[2].shape[0] // OUTPROJ_TILE
    tokens = tiles * OUTPROJ_TILE

    def mod_spec(which):
        return pl.BlockSpec((1, 1, 1, D_MODEL), lambda i: (mod_row_of_tile(i), which, 0, 0))

    ctx_tok = lambda w: pl.BlockSpec((OUTPROJ_TILE, w), lambda i: (jnp.minimum(i, ctx_tiles - 1), 0))
    lat_tok = lambda w: pl.BlockSpec((OUTPROJ_TILE, w), lambda i: (jnp.maximum(i - ctx_tiles, 0), 0))
    tok = lambda w: pl.BlockSpec((OUTPROJ_TILE, w), lambda i: (i, 0))
    full = lambda a: pl.BlockSpec(a.shape, lambda i: (0,) * a.ndim)
    widths = (CONV_W, RET_W, D_MODEL)
    return pl.pallas_call(
        functools.partial(_outproj_kernel, ctx_tiles),
        grid=(tiles,),
        in_specs=[ctx_tok(w) for w in widths] + [lat_tok(w) for w in widths] + [
            mod_spec(2), mod_spec(3), mod_spec(4),
            full(norm_g), full(w_out_bf), full(w_router_bf), full(b_router)],
        out_specs=[tok(D_MODEL),
                   pl.BlockSpec((1, XLOC_ROWS, ROW_W), lambda i: (i, 0, 0)),
                   tok(ROUTER_COLS),
                   pl.BlockSpec((1, 1, ROUTER_COLS), lambda i: (i, 0, 0))],
        out_shape=[jax.ShapeDtypeStruct((tokens, D_MODEL), F32),
                   jax.ShapeDtypeStruct((tiles, XLOC_ROWS, ROW_W), F32),
                   jax.ShapeDtypeStruct((tokens, ROUTER_COLS), F32),
                   jax.ShapeDtypeStruct((tiles, 1, ROUTER_COLS), F32)],
        compiler_params=pltpu.CompilerParams(
            dimension_semantics=("parallel",), vmem_limit_bytes=VMEM_LIMIT,
            allow_input_fusion=[False] * 10 + [True, True, False]),
        name="outproj",
    )(*ctx, *lat, mod4, mod4, mod4, norm_g, w_out_bf, w_router_bf, b_router)


RUN_PIECES = tuple(SUBLANES << b for b in reversed(range((OUTPROJ_TILE // SUBLANES).bit_length())))


def _expert_kernel(tile_group_ref, n_used_ref, first_ref, last_ref, fill_ref,
                   run_len_ref, run_src_ref, run_dst_ref,
                   xloc_hbm, w1_ref, w3_ref, w2_ref, ys_ref, xbuf, xb, gate_tabs, sem):
    j = pl.program_id(0)
    step = pl.program_id(1)
    n_used = n_used_ref[0]

    def tile_fetch(tile):
        slot = tile % 2
        group = tile_group_ref[tile]
        row0 = tile * GROUP_TILE

        def from_token_tile(b, carry):
            run = b * N_GROUPS + group
            lo = jnp.maximum(run_dst_ref[run], row0)
            hi = jnp.minimum(run_dst_ref[run] + run_len_ref[run], row0 + GROUP_TILE)
            n = jnp.maximum(hi - lo, 0)
            src = run_src_ref[run] + lo - run_dst_ref[run]
            dst = lo - row0
            for size in RUN_PIECES:
                done = n & (-2 * size)

                @pl.when((n & size) != 0)
                def _():
                    pltpu.make_async_copy(
                        xloc_hbm.at[b, pl.ds(pl.multiple_of(src + done, SUBLANES), size)],
                        xbuf.at[slot, pl.ds(pl.multiple_of(dst + done, SUBLANES), size)], sem.at[slot]).start()
            return carry

        lax.fori_loop(first_ref[tile], last_ref[tile] + 1, from_token_tile, 0)

    def tile_wait(tile):
        slot = tile % 2
        for size in (GROUP_TILE,) + tuple(GROUP_TILE >> k for k in range(1, (GROUP_TILE // SUBLANES).bit_length())):
            @pl.when((fill_ref[tile] & size) != 0)
            def _():
                pltpu.make_async_copy(xbuf.at[1 - slot, pl.ds(0, size)], xbuf.at[slot, pl.ds(0, size)],
                                      sem.at[slot]).wait()

    @pl.when(j < n_used)
    def _():
        @pl.when(step == 0)
        def _():
            @pl.when(j == 0)
            def _():
                xbuf[...] = jnp.zeros_like(xbuf)
                tile_fetch(j)

            tile_wait(j)
            rows_in = xbuf[j % 2]
            xb[...] = rows_in[:, :D_MODEL].astype(BF16)
            route = rows_in[:, D_MODEL:]
            lane = lax.broadcasted_iota(jnp.int32, route.shape, 1)
            for n, which in enumerate((ROUTE_E1, ROUTE_E2, ROUTE_W1, ROUTE_W2)):
                col = jnp.sum(jnp.where(lane == which, route, 0.0), axis=1, keepdims=True)
                gate_tabs[n] = jnp.broadcast_to(col, route.shape)

            @pl.when(j + 1 < n_used)
            def _():
                tile_fetch(j + 1)

        def evaluate(rows):
            x = xb[:rows]
            total = None
            for s in range(EXPERTS_PER_STEP):
                expert = (tile_group_ref[j] * EXPERTS_PER_GROUP + step * EXPERTS_PER_STEP + s).astype(F32)
                gate = (jnp.where(gate_tabs[0, :rows] == expert, gate_tabs[2, :rows], 0.0)
                        + jnp.where(gate_tabs[1, :rows] == expert, gate_tabs[3, :rows], 0.0))
                hid = _silu(jnp.dot(x, w1_ref[0, s].astype(BF16), preferred_element_type=F32)) * jnp.dot(
                    x, w3_ref[0, s].astype(BF16), preferred_element_type=F32)
                y = jnp.dot(hid.astype(BF16), w2_ref[0, s].astype(BF16), preferred_element_type=F32)
                gated = jnp.concatenate(
                    [gate * y[:, c * LANES:(c + 1) * LANES] for c in range(D_MODEL // LANES)], axis=1)
                total = gated if total is None else total + gated

            @pl.when(step == 0)
            def _():
                ys_ref[:rows] = total
                if rows < GROUP_TILE:
                    ys_ref[rows:] = jnp.zeros((GROUP_TILE - rows, D_MODEL), F32)

            @pl.when(step > 0)
            def _():
                ys_ref[:rows] += total

        half = GROUP_TILE // 2

        @pl.when(fill_ref[j] > half)
        def _():
            evaluate(GROUP_TILE)

        @pl.when(fill_ref[j] <= half)
        def _():
            evaluate(half)

    @pl.when((j >= n_used) & (step == 0))
    def _():
        ys_ref[...] = jnp.zeros_like(ys_ref)


def _experts(xloc, tile_tables, runs, slots, w1, w3, w2):
    steps = EXPERTS_PER_GROUP // EXPERTS_PER_STEP
    per_step = lambda w: w.reshape((N_EXPERTS // EXPERTS_PER_STEP, EXPERTS_PER_STEP) + w.shape[1:])
    w_spec = lambda shape: pl.BlockSpec((1, EXPERTS_PER_STEP) + shape, lambda j, s, tg, nu, *_: (
        tg[jnp.minimum(j, nu[0] - 1)] * steps + jnp.where(j < nu[0], s, steps - 1), 0, 0, 0))
    grid_spec = pltpu.PrefetchScalarGridSpec(
        num_scalar_prefetch=8,
        grid=(slots // GROUP_TILE, steps),
        in_specs=[
            pl.BlockSpec(memory_space=pl.ANY),
            w_spec((D_MODEL, D_EXPERT)), w_spec((D_MODEL, D_EXPERT)), w_spec((D_EXPERT, D_MODEL)),
        ],
        out_specs=pl.BlockSpec((GROUP_TILE, D_MODEL), lambda j, s, *_: (j, 0)),
        scratch_shapes=[pltpu.VMEM((2, GROUP_TILE, ROW_W), F32), pltpu.VMEM((GROUP_TILE, D_MODEL), BF16),
                        pltpu.VMEM((4, GROUP_TILE, LANES), F32), pltpu.SemaphoreType.DMA((2,))],
    )
    return pl.pallas_call(
        _expert_kernel,
        grid_spec=grid_spec,
        out_shape=jax.ShapeDtypeStruct((slots, D_MODEL), F32),
        compiler_params=pltpu.CompilerParams(
            dimension_semantics=("arbitrary", "arbitrary"), vmem_limit_bytes=VMEM_LIMIT),
        name="experts",
    )(*tile_tables, *runs, xloc, per_step(w1), per_step(w3), per_step(w2))


def _combine_kernel(n_tiles, tile_base, run_len_ref, run_src_ref, run_dst_ref,
                    route_ref, g2_ref, fg_ref, x1_hbm, ys_hbm, o_ref, buf, x1_buf, sem, x1_sem):
    i = pl.program_id(0)
    slot = i % COMBINE_RING

    def x1_copy(local_tile):
        s = local_tile % COMBINE_RING
        rows = pl.ds(pl.multiple_of((local_tile + tile_base) * OUTPROJ_TILE, OUTPROJ_TILE), OUTPROJ_TILE)
        return pltpu.make_async_copy(x1_hbm.at[rows], x1_buf.at[s], x1_sem.at[s])

    def run_copies(local_tile, act):
        s = local_tile % COMBINE_RING
        tile = local_tile + tile_base
        for g in range(N_GROUPS):
            n = run_len_ref[tile * N_GROUPS + g]
            src = run_src_ref[tile * N_GROUPS + g]
            dst = run_dst_ref[tile * N_GROUPS + g]
            for size in RUN_PIECES:
                done = n & (-2 * size)

                @pl.when((n & size) != 0)
                def _():
                    act(pltpu.make_async_copy(
                        ys_hbm.at[pl.ds(pl.multiple_of(dst + done, SUBLANES), size)],
                        buf.at[s, pl.ds(pl.multiple_of(src + done, SUBLANES), size)], sem.at[s]))

    def fetch(local_tile):
        run_copies(local_tile, lambda cp: cp.start())
        x1_copy(local_tile).start()

    @pl.when(i == 0)
    def _():
        buf[...] = jnp.zeros_like(buf)
        for ahead in range(min(COMBINE_RING - 1, n_tiles)):
            fetch(i + ahead)

    @pl.when(i + COMBINE_RING - 1 < n_tiles)
    def _():
        fetch(i + COMBINE_RING - 1)

    run_copies(i, lambda cp: cp.wait())
    x1_copy(i).wait()
    route = route_ref[...]
    lane = lax.broadcasted_iota(jnp.int32, route.shape, 1)
    local = jnp.sum(jnp.where(lane == ROUTE_LOCAL, route, 0.0), axis=1, keepdims=True)
    pick = jnp.where(lax.broadcasted_iota(jnp.int32, (route.shape[0], LOCAL_ROWS), 1).astype(F32) == local,
                     1.0, 0.0).astype(BF16)
    moe = sum(jnp.dot(pick, piece, preferred_element_type=F32) for piece in _split3(buf[slot])[:2])
    y = x1_buf[slot] + g2_ref[0, 0] * moe
    o_ref[...] = _rms(y) * fg_ref[...]


def _combine(ys, runs, x1, route, tokens, mod4, mod_row_of_tile, final_g, tile_base):
    tiles = tokens // OUTPROJ_TILE
    tok = lambda w: pl.BlockSpec((OUTPROJ_TILE, w), lambda i, *_: (i + tile_base, 0))
    grid_spec = pltpu.PrefetchScalarGridSpec(
        num_scalar_prefetch=3,
        grid=(tiles,),
        in_specs=[
            tok(ROUTER_COLS),
            pl.BlockSpec((1, 1, 1, D_MODEL), lambda i, *_: (mod_row_of_tile(i + tile_base), 5, 0, 0)),
            pl.BlockSpec((1, D_MODEL), lambda i, *_: (0, 0)),
            pl.BlockSpec(memory_space=pl.ANY),
            pl.BlockSpec(memory_space=pl.ANY),
        ],
        out_specs=pl.BlockSpec((OUTPROJ_TILE, D_MODEL), lambda i, *_: (i, 0)),
        scratch_shapes=[pltpu.VMEM((COMBINE_RING, LOCAL_ROWS, D_MODEL), F32),
                        pltpu.VMEM((COMBINE_RING, OUTPROJ_TILE, D_MODEL), F32),
                        pltpu.SemaphoreType.DMA((COMBINE_RING,)), pltpu.SemaphoreType.DMA((COMBINE_RING,))],
    )
    return pl.pallas_call(
        functools.partial(_combine_kernel, tiles, tile_base),
        grid_spec=grid_spec,
        out_shape=jax.ShapeDtypeStruct((tokens, D_MODEL), F32),
        compiler_params=pltpu.CompilerParams(
            dimension_semantics=("arbitrary",), vmem_limit_bytes=VMEM_LIMIT),
        name="combine",
    )(*runs, route, mod4, final_g, x1, ys)


def _routing_tables(counts):
    counts = counts.astype(jnp.int32)
    tiles = counts.shape[0]
    run_len = ((counts + SUBLANES - 1) // SUBLANES) * SUBLANES
    run_src = jnp.cumsum(run_len, axis=1) - run_len
    group_rows = jnp.sum(run_len, axis=0)
    padded = ((group_rows + GROUP_TILE - 1) // GROUP_TILE) * GROUP_TILE
    ends = jnp.cumsum(padded)
    offs = ends - padded
    run_dst = offs[None, :] + jnp.cumsum(run_len, axis=0) - run_len

    n_used = ends[-1] // GROUP_TILE
    max_rows = tiles * (OUTPROJ_TILE + N_GROUPS * (SUBLANES - 1))
    max_tiles = max_rows // GROUP_TILE + N_GROUPS
    tile_ids = jnp.minimum(jnp.arange(max_tiles, dtype=jnp.int32), n_used - 1)
    tile_group = jnp.sum(tile_ids[:, None] * GROUP_TILE >= ends[None, :], axis=1).astype(jnp.int32)
    of_group = (tile_group[:, None] == jnp.arange(N_GROUPS, dtype=jnp.int32))[:, None, :]
    start = jnp.sum(jnp.where(of_group, run_dst[None], 0), axis=-1)
    stop = start + jnp.sum(jnp.where(of_group, run_len[None], 0), axis=-1)
    row0 = (tile_ids * GROUP_TILE)[:, None]
    first = jnp.sum(stop <= row0, axis=1).astype(jnp.int32)
    last = jnp.sum(start < row0 + GROUP_TILE, axis=1).astype(jnp.int32) - 1
    group_end = jnp.sum(jnp.where(of_group[:, 0, :], (offs + group_rows)[None, :], 0), axis=-1)
    fill = jnp.clip(group_end - row0[:, 0], 0, GROUP_TILE).astype(jnp.int32)
    flat = lambda a: a.reshape(-1)
    return ((tile_group, n_used.reshape(1), first, last, fill), (flat(run_len), flat(run_src), flat(run_dst)),
            max_tiles * GROUP_TILE)


def _mixer(x, mod4, mod_row, is_grid, s_f0, s_b0, p):
    norm_mix_g, w_in, conv_w, decay_rows = p
    y_conv, *qkvg, w_in_bf = _inproj(x, mod4, mod_row, norm_mix_g, w_in, conv_w, is_grid)
    per_seq = lambda a: a.reshape(x.shape[0], x.shape[1], a.shape[-1])
    ret = _retention(*map(per_seq, qkvg), decay_rows, s_f0, s_b0, emit_state=not is_grid)
    return y_conv, ret[0].reshape(-1, RET_W), ret[1:], w_in_bf


def kernel(x_prompt, x_sample, state_ret_fwd, state_ret_bwd, c, c_ctx, norm_mix_g, norm_ffn_g, w_ada, b_ada, w_in, conv_w, ret_decay_fwd, ret_decay_bwd, w_out, w_router_group, b_router_group, w_router_expert, b_router_expert, w_gate_e, w_up_e, w_down_e, final_norm_g):
    assert norm_mix_g.shape[0] == 1, "single-layer backbone"
    n_lat = c.shape[0]
    ctx_row = n_lat
    mod_rows = 8
    cvec = jnp.concatenate([c, c_ctx[None, :], jnp.zeros((mod_rows - n_lat - 1, D_MODEL), F32)], axis=0)
    mod4 = _modulation(cvec, w_ada[0], b_ada[0][None, :])

    pad = ROUTER_COLS - N_GROUPS - N_EXPERTS
    w_router = jnp.concatenate(
        [w_router_group[0], w_router_expert[0], jnp.zeros((D_MODEL, pad), F32)], axis=1).astype(BF16)
    b_router = jnp.concatenate([b_router_group[0], b_router_expert[0], jnp.zeros((pad,), F32)])[None, :]
    decay_rows = jnp.broadcast_to(
        jnp.concatenate([ret_decay_fwd[0], ret_decay_bwd[0]])[:, None], (2 * RET_HEADS, LANES))
    p_mix = (norm_mix_g, w_in[0], conv_w[0], decay_rows)
    w_out_bf = w_out[0].astype(BF16)
    final_g = final_norm_g[None, :]

    ctx_tokens = x_prompt.shape[0] * x_prompt.shape[1]
    lat_tokens = x_sample.shape[0] * x_sample.shape[1]
    ctx_tiles = ctx_tokens // OUTPROJ_TILE
    lat_tiles_per_seq = x_sample.shape[1] // OUTPROJ_TILE
    tile_mod = lambda i: jnp.where(i < ctx_tiles, ctx_row, (i - ctx_tiles) // lat_tiles_per_seq)
    flat = lambda a: a.reshape(-1, a.shape[-1])

    yc_c, yr_c, (s_f, s_b), w_in_bf = _mixer(x_prompt, mod4, lambda b: ctx_row, False, None, None, p_mix)
    p_mix = (norm_mix_g, w_in_bf, conv_w[0], decay_rows)
    yc_l, yr_l, _, _ = _mixer(x_sample, mod4, lambda b: b, True, state_ret_fwd, state_ret_bwd, p_mix)

    x1, xloc, route, cnt = _outproj((yc_c, yr_c, flat(x_prompt)), (yc_l, yr_l, flat(x_sample)), mod4, tile_mod,
                                    norm_ffn_g, w_out_bf, w_router, b_router)
    tile_tables, runs, slots = _routing_tables(cnt[:, 0, :N_GROUPS])
    ys = _experts(xloc, tile_tables, runs, slots, w_gate_e[0], w_up_e[0], w_down_e[0])
    y_prompt = _combine(ys, runs, x1, route, ctx_tokens, mod4, tile_mod, final_g, 0)
    y_sample = _combine(ys, runs, x1, route, lat_tokens, mod4, tile_mod, final_g, ctx_tiles)
    return (y_prompt.reshape(x_prompt.shape), y_sample.reshape(x_sample.shape),
            s_f.astype(x_prompt.dtype), s_b.astype(x_prompt.dtype))
```

```python
import functools

import jax
import jax.numpy as jnp
import numpy as np
from jax import lax
from jax.experimental import pallas as pl
from jax.experimental.pallas import tpu as pltpu

F32 = jnp.float32
BF16 = jnp.bfloat16

D_MODEL = 1024
GRID_W = 64
CONV_W = 512
RET_HEADS = 4
RET_DK = 128
RET_DV = 128
RET_W = RET_HEADS * RET_DV
QK_W = RET_HEADS * RET_DK
CHUNK = 128
N_GROUPS = 4
EXPERTS_PER_GROUP = 8
N_EXPERTS = N_GROUPS * EXPERTS_PER_GROUP
D_EXPERT = 256
ROPE_BASE = 10000.0
EPS = 1e-6

LANES = 128
TOKEN_TILE = 1024
OUTPROJ_TILE = 512
GROUP_TILE = 1024
EXPERTS_PER_STEP = 4
RET_UNROLL = 8
COMBINE_RING = 3
SUBLANES = 8
XLOC_ROWS = OUTPROJ_TILE + N_GROUPS * SUBLANES
LOCAL_ROWS = OUTPROJ_TILE + LANES
ROW_W = D_MODEL + LANES
ROUTE_GROUP, ROUTE_LOCAL, ROUTE_E1, ROUTE_E2, ROUTE_W1, ROUTE_W2 = range(6)
MOD_COLS = D_MODEL
CAST_ROWS = 128
ROUTER_COLS = LANES
VMEM_LIMIT = 56 * 1024 * 1024


def _silu(x):
    return x * jax.nn.sigmoid(x)


def _rms(x):
    return x * lax.rsqrt(jnp.mean(x * x, axis=-1, keepdims=True) + EPS)


def _bdot(a, b):
    return jnp.dot(a.astype(BF16), b.astype(BF16), preferred_element_type=F32)


def _split3(x):
    hi = x.astype(BF16)
    rest = x - hi.astype(F32)
    mid = rest.astype(BF16)
    return hi, mid, (rest - mid.astype(F32)).astype(BF16)


def _mod_kernel(c_ref, w_ref, b_ref, o_ref):
    res = _bdot(_silu(c_ref[...]), w_ref[...]) + b_ref[...]
    for r in range(res.shape[0]):
        o_ref[r, 0] = res[r:r + 1]


def _modulation(cvec, w_ada, b_ada):
    rows = cvec.shape[0]
    n = w_ada.shape[1]
    return pl.pallas_call(
        _mod_kernel,
        grid=(n // MOD_COLS,),
        in_specs=[
            pl.BlockSpec((rows, D_MODEL), lambda j: (0, 0)),
            pl.BlockSpec((D_MODEL, MOD_COLS), lambda j: (0, j)),
            pl.BlockSpec((1, MOD_COLS), lambda j: (0, j)),
        ],
        out_specs=pl.BlockSpec((rows, 1, 1, MOD_COLS), lambda j: (0, j, 0, 0)),
        out_shape=jax.ShapeDtypeStruct((rows, n // MOD_COLS, 1, MOD_COLS), F32),
        compiler_params=pltpu.CompilerParams(vmem_limit_bytes=VMEM_LIMIT),
        name="modulation",
    )(cvec, w_ada, b_ada)


def _inproj_kernel(seg, is_grid, cast_w, x_ref, sh_ref, sc_ref, ng_ref, w_ref, cw_ref, *rest):
    if cast_w:
        *rest, wout_ref, wbf_ref, stage_ref, sem = rest
        w_hbm = w_ref
        n_chunks = w_hbm.shape[0] // CAST_ROWS
        export = pltpu.make_async_copy(wbf_ref, wout_ref, sem.at[2])

        def chunk(c):
            return pltpu.make_async_copy(w_hbm.at[pl.ds(c * CAST_ROWS, CAST_ROWS)], stage_ref.at[c % 2],
                                         sem.at[c % 2])

        @pl.when(pl.program_id(0) == 0)
        def _():
            chunk(0).start()
            chunk(1).start()
            for c in range(n_chunks):
                chunk(c).wait()
                wbf_ref[pl.ds(c * CAST_ROWS, CAST_ROWS), :] = stage_ref[c % 2].astype(BF16)
                if c + 2 < n_chunks:
                    chunk(c + 2).start()
            export.start()

        @pl.when(pl.program_id(0) == pl.num_programs(0) - 1)
        def _():
            export.wait()

        w_ref = wbf_ref
    if is_grid:
        cos_ref, sa_ref, sb_ref, yc_ref, q_ref, k_ref, v_ref, g_ref = rest
    else:
        yc_ref, q_ref, k_ref, v_ref, g_ref = rest
    x = x_ref[...]
    xn = (_rms(x) * ng_ref[...]) * (1.0 + sc_ref[0, 0]) + sh_ref[0, 0]
    xb = xn.astype(BF16)

    def proj(c0, n):
        return jnp.dot(xb, w_ref[:, c0:c0 + n], preferred_element_type=F32)

    gate_b = proj(0, CONV_W)
    u = proj(CONV_W, CONV_W) * proj(2 * CONV_W, CONV_W)
    rows = u.shape[0]
    pos = lax.broadcasted_iota(jnp.int32, u.shape, 0) & (seg - 1)
    u_prev = jnp.where(pos != 0, pltpu.roll(u, 1, 0), 0.0)
    u_next = jnp.where(pos != seg - 1, pltpu.roll(u, rows - 1, 0), 0.0)
    conv = cw_ref[0:1, :] * u_prev + cw_ref[1:2, :] * u + cw_ref[2:3, :] * u_next
    yc_ref[...] = (gate_b * conv).astype(yc_ref.dtype)

    q0 = 3 * CONV_W
    q = proj(q0, QK_W)
    k = proj(q0 + QK_W, QK_W)
    if is_grid:
        cos, sa, sb = cos_ref[...], sa_ref[...], sb_ref[...]

        def rope(t):
            out = []
            for h in range(RET_HEADS):
                th = t[:, h * RET_DK:(h + 1) * RET_DK]
                out.append(th * cos + pltpu.roll(th, RET_DK - 1, 1) * sa + pltpu.roll(th, 1, 1) * sb)
            return jnp.concatenate(out, axis=1)

        q, k = rope(q), rope(k)
    q_ref[...] = q
    k_ref[...] = k
    v_ref[...] = proj(q0 + 2 * QK_W, RET_W)
    g_ref[...] = proj(q0 + 2 * QK_W + RET_W, RET_W)


def _rope_tables(length):
    pos = np.arange(length)
    row = (pos // GRID_W).astype(np.float64)
    col = (pos % GRID_W).astype(np.float64)
    n_pairs = RET_DK // 4
    freqs = ROPE_BASE ** (-(np.arange(n_pairs, dtype=np.float64) * 2.0 / (RET_DK // 2)))
    ang = np.concatenate([row[:, None] * freqs, col[:, None] * freqs], axis=-1)
    cos = np.repeat(np.cos(ang), 2, axis=-1)
    sin = np.repeat(np.sin(ang), 2, axis=-1)
    even = (np.arange(RET_DK) % 2) == 0
    return tuple(jnp.asarray(t, F32) for t in (cos, np.where(even, -sin, 0.0), np.where(even, 0.0, sin)))


def _inproj(x, mod4, mod_row, norm_g, w_in, conv_w, is_grid):
    bsz, length, _ = x.shape
    seg = GRID_W if is_grid else length
    assert TOKEN_TILE % seg == 0 and (length % TOKEN_TILE == 0 or TOKEN_TILE % length == 0)
    tokens = bsz * length
    tiles_per_seq = max(length // TOKEN_TILE, 1)
    seqs_per_tile = max(TOKEN_TILE // length, 1)
    batch_of = lambda i: (i // tiles_per_seq) * seqs_per_tile

    def mod_spec(which):
        return pl.BlockSpec((1, 1, 1, D_MODEL), lambda i: (mod_row(batch_of(i)), which, 0, 0))

    def tok_spec(width):
        return pl.BlockSpec((TOKEN_TILE, width), lambda i: (i, 0))

    cast_w = w_in.dtype != BF16
    in_specs = [
        tok_spec(D_MODEL), mod_spec(0), mod_spec(1),
        pl.BlockSpec((1, D_MODEL), lambda i: (0, 0)),
        pl.BlockSpec(memory_space=pl.ANY) if cast_w else pl.BlockSpec(w_in.shape, lambda i: (0, 0)),
        pl.BlockSpec(conv_w.shape, lambda i: (0, 0)),
    ]
    args = [x.reshape(tokens, D_MODEL), mod4, mod4, norm_g, w_in, conv_w]
    if is_grid:
        assert length % TOKEN_TILE == 0
        in_specs += [pl.BlockSpec((TOKEN_TILE, RET_DK), lambda i: (i % tiles_per_seq, 0))] * 3
        args += list(_rope_tables(length))
    shp = lambda w, dt: jax.ShapeDtypeStruct((tokens, w), dt)
    out_specs = [tok_spec(CONV_W), tok_spec(QK_W), tok_spec(QK_W), tok_spec(RET_W), tok_spec(RET_W)]
    out_shape = [shp(CONV_W, BF16), shp(QK_W, F32), shp(QK_W, F32), shp(RET_W, F32), shp(RET_W, F32)]
    scratch = []
    if cast_w:
        out_specs.append(pl.BlockSpec(memory_space=pl.ANY))
        out_shape.append(jax.ShapeDtypeStruct(w_in.shape, BF16))
        scratch = [pltpu.VMEM(w_in.shape, BF16), pltpu.VMEM((2, CAST_ROWS, w_in.shape[1]), F32),
                   pltpu.SemaphoreType.DMA((3,))]
    outs = pl.pallas_call(
        functools.partial(_inproj_kernel, seg, is_grid, cast_w),
        grid=(tokens // TOKEN_TILE,),
        in_specs=in_specs,
        out_specs=out_specs,
        out_shape=out_shape,
        scratch_shapes=scratch,
        compiler_params=pltpu.CompilerParams(
            dimension_semantics=("arbitrary" if cast_w else "parallel",), vmem_limit_bytes=VMEM_LIMIT),
        name="inproj_grid" if is_grid else "inproj_seq",
    )(*args)
    return outs if cast_w else [*outs, w_in]


def _ret_kernel(n_chunks, heads, has_init, emit_state, a_ref, q_ref, k_ref, v_ref, g_ref, *rest):
    rest = list(rest)
    if has_init:
        sf0_ref, sb0_ref = rest[:2]
        rest = rest[2:]
    y_ref = rest.pop(0)
    if emit_state:
        sf_out, sb_out = rest[:2]
        rest = rest[2:]
    st_f, st_b, dec = rest
    c = CHUNK
    sq = (c, c)
    head0 = pl.program_id(0) * heads

    def log_decays(hh):
        lg_f = jnp.log1p(-jnp.exp(a_ref[pl.ds(head0 + hh, 1), :]))
        lg_b = jnp.log1p(-jnp.exp(a_ref[pl.ds(head0 + hh + RET_HEADS, 1), :]))
        return lg_f, lg_b

    @pl.when(pl.program_id(1) == 0)
    def _():
        row = lax.broadcasted_iota(jnp.int32, sq, 0).astype(F32)
        col = lax.broadcasted_iota(jnp.int32, sq, 1).astype(F32)
        scale = RET_DK ** -0.5
        for hh in range(heads):
            lg_f, lg_b = log_decays(hh)
            dec[hh, 0] = scale * (
                jnp.where(row >= col, jnp.exp(jnp.where(row >= col, row - col, 0.0) * lg_f), 0.0)
                + jnp.where(col >= row, jnp.exp(jnp.where(col >= row, col - row, 0.0) * lg_b), 0.0))
            dec[hh, 1] = jnp.exp((row + 1.0) * lg_f)
            dec[hh, 2] = jnp.exp((c - row) * lg_b)
            dec[hh, 3] = scale * jnp.exp((c - 1.0 - col) * lg_f)
            dec[hh, 4] = scale * jnp.exp(col * lg_b)

    def rows(n):
        return pl.ds(pl.multiple_of(n * c, c), c) if not isinstance(n, int) else pl.ds(n * c, c)

    def cols(hh):
        return slice(hh * RET_DK, (hh + 1) * RET_DK)

    def kv_step(hh, n):
        kt = jnp.transpose(k_ref[0, rows(n), cols(hh)])
        lhs = jnp.concatenate([kt * dec[hh, 3], kt * dec[hh, 4]], axis=0)
        kv = _bdot(lhs, v_ref[0, rows(n), cols(hh)])
        st_f[hh, n] = kv[:RET_DK]
        st_b[hh, n] = kv[RET_DK:]

    def scan(hh, st, decay, order, s):
        def step(i, s):
            n = order(i)
            kv = st[hh, n]
            st[hh, n] = s
            return s * decay + kv
        if n_chunks <= RET_UNROLL:
            for i in range(n_chunks):
                s = step(i, s)
            return s
        return lax.fori_loop(0, n_chunks, step, s, unroll=RET_UNROLL)

    def out_step(hh, n):
        q = q_ref[0, rows(n), cols(hh)]
        scores = lax.dot_general(q.astype(BF16), k_ref[0, rows(n), cols(hh)].astype(BF16),
                                 (((1,), (1,)), ((), ())), preferred_element_type=F32)
        o = _bdot(scores * dec[hh, 0], v_ref[0, rows(n), cols(hh)])
        q_dec = jnp.concatenate([q * dec[hh, 1], q * dec[hh, 2]], axis=1)
        o = o + _bdot(q_dec, jnp.concatenate([st_f[hh, n], st_b[hh, n]], axis=0))
        y = _silu(g_ref[0, rows(n), cols(hh)]) * _rms(o)
        y_ref[0, rows(n), cols(hh)] = y.astype(y_ref.dtype)

    def over_chunks(step):
        if n_chunks * heads <= RET_UNROLL:
            for hh in range(heads):
                for n in range(n_chunks):
                    step(hh, n)
        else:
            for hh in range(heads):
                lax.fori_loop(0, n_chunks, lambda n, carry: (step(hh, n), carry)[1], 0, unroll=RET_UNROLL)

    over_chunks(kv_step)
    finals = []
    for hh in range(heads):
        lg_f, lg_b = log_decays(hh)
        s_f = sf0_ref[0, 0, hh] if has_init else jnp.zeros(sq, F32)
        s_b = sb0_ref[0, 0, hh] if has_init else jnp.zeros(sq, F32)
        s_f = scan(hh, st_f, jnp.exp(c * lg_f), lambda i: i, s_f)
        s_b = scan(hh, st_b, jnp.exp(c * lg_b), lambda i: n_chunks - 1 - i, s_b)
        finals.append((s_f, s_b))
    over_chunks(out_step)
    if emit_state:
        for hh, (s_f, s_b) in enumerate(finals):
            sf_out[0, 0, hh] = s_f
            sb_out[0, 0, hh] = s_b


def _retention(q, k, v, g, decay_rows, s_f0, s_b0, emit_state):
    bsz, length, _ = q.shape
    n_chunks = length // CHUNK
    has_init = s_f0 is not None
    heads = RET_HEADS if n_chunks * RET_HEADS <= RET_UNROLL else 1
    head_spec = pl.BlockSpec((1, length, heads * RET_DK), lambda h, b: (b, 0, h))
    st_spec = pl.BlockSpec((1, 1, heads, RET_DK, RET_DV), lambda h, b: (b, 0, h, 0, 0))
    in_specs = [pl.BlockSpec(decay_rows.shape, lambda h, b: (0, 0))] + [head_spec] * 4
    args = [decay_rows, q, k, v, g]
    if has_init:
        in_specs += [st_spec, st_spec]
        args += [s_f0, s_b0]
    out_specs = [head_spec]
    out_shape = [jax.ShapeDtypeStruct((bsz, length, RET_W), BF16)]
    if emit_state:
        st_shape = jax.ShapeDtypeStruct((bsz, 1, RET_HEADS, RET_DK, RET_DV), F32)
        out_specs += [st_spec, st_spec]
        out_shape += [st_shape, st_shape]
    return pl.pallas_call(
        functools.partial(_ret_kernel, n_chunks, heads, has_init, emit_state),
        grid=(RET_HEADS // heads, bsz),
        in_specs=in_specs,
        out_specs=out_specs,
        out_shape=out_shape,
        scratch_shapes=[
            pltpu.VMEM((heads, n_chunks, RET_DK, RET_DV), F32),
            pltpu.VMEM((heads, n_chunks, RET_DK, RET_DV), F32),
            pltpu.VMEM((heads, 5, CHUNK, CHUNK), F32),
        ],
        compiler_params=pltpu.CompilerParams(
            dimension_semantics=("arbitrary", "arbitrary"), vmem_limit_bytes=VMEM_LIMIT),
        name="retention_init" if has_init else "retention_zero",
    )(*args)


def _route(logits):
    lane = lax.broadcasted_iota(jnp.int32, logits.shape, 1)
    lane_f = lane.astype(F32)
    neg = -jnp.inf
    far = float(LANES)
    is_g = lane < N_GROUPS
    lg = jnp.where(is_g, logits, neg)
    g_max = jnp.max(lg, axis=1, keepdims=True)
    g_idx = jnp.min(jnp.where(lg == g_max, lane_f, far), axis=1, keepdims=True)
    p_sel = 1.0 / jnp.sum(jnp.where(is_g, jnp.exp(lg - g_max), 0.0), axis=1, keepdims=True)
    lane_group = ((lane - N_GROUPS) >> (EXPERTS_PER_GROUP.bit_length() - 1)).astype(F32)
    sel = (lane >= N_GROUPS) & (lane < N_GROUPS + N_EXPERTS) & (lane_group == g_idx)
    le = jnp.where(sel, logits, neg)
    v1 = jnp.max(le, axis=1, keepdims=True)
    i1 = jnp.min(jnp.where(le == v1, lane_f, far), axis=1, keepdims=True)
    le2 = jnp.where(lane_f == i1, neg, le)
    v2 = jnp.max(le2, axis=1, keepdims=True)
    i2 = jnp.min(jnp.where(le2 == v2, lane_f, far), axis=1, keepdims=True)
    e2 = jnp.exp(v2 - v1)
    w1 = p_sel * (1.0 / (1.0 + e2))
    w2 = p_sel * (e2 / (1.0 + e2))
    return lane, lane_f, g_idx, i1, i2, w1, w2


def _outproj_kernel(ctx_tiles, yc_c, yr_c, x_c, yc_l, yr_l, x_l, *rest):
    *rest, wo_bf_ref, stage_ref, sem = rest
    rest = list(rest)
    wo_slot = 4

    @pl.when(pl.program_id(0) == 0)
    def _():
        fetch = pltpu.make_async_copy(rest[wo_slot], stage_ref, sem.at[0])
        fetch.start()
        fetch.wait()
        wo_bf_ref[...] = stage_ref[...].astype(BF16)

    rest[wo_slot] = wo_bf_ref

    @pl.when(pl.program_id(0) < ctx_tiles)
    def _():
        _outproj_tile(yc_c, yr_c, x_c, *rest)

    @pl.when(pl.program_id(0) >= ctx_tiles)
    def _():
        _outproj_tile(yc_l, yr_l, x_l, *rest)


def _outproj_tile(yc_ref, yr_ref, x_ref, g1_ref, sh_ref, sc_ref, ng_ref, wo_ref, wr_ref, br_ref,
                  x1_ref, xloc_ref, route_ref, cnt_ref):
    m = (jnp.dot(yc_ref[...], wo_ref[0:CONV_W, :], preferred_element_type=F32)
         + jnp.dot(yr_ref[...], wo_ref[CONV_W:, :], preferred_element_type=F32))
    x1 = x_ref[...] + g1_ref[0, 0] * m
    x1_ref[...] = x1
    xn = (_rms(x1) * ng_ref[...]) * (1.0 + sc_ref[0, 0]) + sh_ref[0, 0]
    xb = xn.astype(BF16)
    logits = jnp.dot(xb, wr_ref[...], preferred_element_type=F32) + br_ref[...]
    lane, lane_f, g_idx, i1, i2, w1, w2 = _route(logits)

    picks = jnp.where(lane_f == g_idx, 1.0, 0.0)
    rows = picks.shape[0]
    tri = (lax.broadcasted_iota(jnp.int32, (rows, rows), 0)
           > lax.broadcasted_iota(jnp.int32, (rows, rows), 1))
    before = jnp.dot(jnp.where(tri, 1.0, 0.0).astype(BF16), picks.astype(BF16),
                     preferred_element_type=F32)
    count = jnp.sum(picks, axis=0, keepdims=True)
    cnt_ref[0] = count
    count8 = jnp.broadcast_to(jnp.floor((count + (SUBLANES - 1)) * (1.0 / SUBLANES)) * SUBLANES,
                              (SUBLANES, LANES))
    lane8 = lane[:SUBLANES]
    start = sum(jnp.where(lane8 >= k, pltpu.roll(count8, k, 1), 0.0) for k in range(1, N_GROUPS))
    local = jnp.sum(jnp.where(lane_f == g_idx, before + start[0:1], 0.0), axis=1, keepdims=True)
    route = jnp.where(lane == ROUTE_GROUP, g_idx, jnp.where(lane == ROUTE_LOCAL, local, jnp.where(
        lane == ROUTE_E1, i1 - N_GROUPS, jnp.where(lane == ROUTE_E2, i2 - N_GROUPS, jnp.where(
            lane == ROUTE_W1, w1, jnp.where(lane == ROUTE_W2, w2, 0.0))))))
    route_ref[...] = route

    local_row = jnp.transpose(jnp.broadcast_to(local, (rows, LANES)))[0:1, :]
    place = jnp.where(lax.broadcasted_iota(jnp.int32, (XLOC_ROWS, rows), 0).astype(F32) == local_row,
                      1.0, 0.0).astype(BF16)
    xloc_ref[0, :, :D_MODEL] = jnp.dot(place, xb, preferred_element_type=F32)
    xloc_ref[0, :, D_MODEL:] = sum(
        jnp.dot(place, piece, preferred_element_type=F32) for piece in _split3(route))


def _outproj(ctx, lat, mod4, mod_row_of_tile, norm_g, w_out, w_router_bf, b_router):
    ctx_tiles = ctx[2].shape[0] // OUTPROJ_TILE
    tiles = ctx_tiles + lat[2].shape[0] // OUTPROJ_TILE
    tokens = tiles * OUTPROJ_TILE

    def mod_spec(which):
        return pl.BlockSpec((1, 1, 1, D_MODEL), lambda i: (mod_row_of_tile(i), which, 0, 0))

    ctx_tok = lambda w: pl.BlockSpec((OUTPROJ_TILE, w), lambda i: (jnp.minimum(i, ctx_tiles - 1), 0))
    lat_tok = lambda w: pl.BlockSpec((OUTPROJ_TILE, w), lambda i: (jnp.maximum(i - ctx_tiles, 0), 0))
    tok = lambda w: pl.BlockSpec((OUTPROJ_TILE, w), lambda i: (i, 0))
    full = lambda a: pl.BlockSpec(a.shape, lambda i: (0,) * a.ndim)
    widths = (CONV_W, RET_W, D_MODEL)
    return pl.pallas_call(
        functools.partial(_outproj_kernel, ctx_tiles),
        grid=(tiles,),
        in_specs=[ctx_tok(w) for w in widths] + [lat_tok(w) for w in widths] + [
            mod_spec(2), mod_spec(3), mod_spec(4),
            full(norm_g), pl.BlockSpec(memory_space=pl.ANY), full(w_router_bf), full(b_router)],
        out_specs=[tok(D_MODEL),
                   pl.BlockSpec((1, XLOC_ROWS, ROW_W), lambda i: (i, 0, 0)),
                   tok(ROUTER_COLS),
                   pl.BlockSpec((1, 1, ROUTER_COLS), lambda i: (i, 0, 0))],
        out_shape=[jax.ShapeDtypeStruct((tokens, D_MODEL), F32),
                   jax.ShapeDtypeStruct((tiles, XLOC_ROWS, ROW_W), F32),
                   jax.ShapeDtypeStruct((tokens, ROUTER_COLS), F32),
                   jax.ShapeDtypeStruct((tiles, 1, ROUTER_COLS), F32)],
        scratch_shapes=[pltpu.VMEM(w_out.shape, BF16), pltpu.VMEM(w_out.shape, F32), pltpu.SemaphoreType.DMA((1,))],
        compiler_params=pltpu.CompilerParams(
            dimension_semantics=("arbitrary",), vmem_limit_bytes=VMEM_LIMIT,
            allow_input_fusion=[False] * 11 + [True, False]),
        name="outproj",
    )(*ctx, *lat, mod4, mod4, mod4, norm_g, w_out, w_router_bf, b_router)


RUN_PIECES = tuple(SUBLANES << b for b in reversed(range((OUTPROJ_TILE // SUBLANES).bit_length())))


def _expert_kernel(tile_group_ref, n_used_ref, first_ref, last_ref, fill_ref,
                   run_len_ref, run_src_ref, run_dst_ref,
                   xloc_hbm, w1_ref, w3_ref, w2_ref, ys_ref, xbuf, xb, gate_tabs, sem):
    j = pl.program_id(0)
    step = pl.program_id(1)
    n_used = n_used_ref[0]

    def tile_fetch(tile):
        slot = tile % 2
        group = tile_group_ref[tile]
        row0 = tile * GROUP_TILE

        def from_token_tile(b, carry):
            run = b * N_GROUPS + group
            lo = jnp.maximum(run_dst_ref[run], row0)
            hi = jnp.minimum(run_dst_ref[run] + run_len_ref[run], row0 + GROUP_TILE)
            n = jnp.maximum(hi - lo, 0)
            src = run_src_ref[run] + lo - run_dst_ref[run]
            dst = lo - row0
            for size in RUN_PIECES:
                done = n & (-2 * size)

                @pl.when((n & size) != 0)
                def _():
                    pltpu.make_async_copy(
                        xloc_hbm.at[b, pl.ds(pl.multiple_of(src + done, SUBLANES), size)],
                        xbuf.at[slot, pl.ds(pl.multiple_of(dst + done, SUBLANES), size)], sem.at[slot]).start()
            return carry

        lax.fori_loop(first_ref[tile], last_ref[tile] + 1, from_token_tile, 0)

    def tile_wait(tile):
        slot = tile % 2
        for size in (GROUP_TILE,) + tuple(GROUP_TILE >> k for k in range(1, (GROUP_TILE // SUBLANES).bit_length())):
            @pl.when((fill_ref[tile] & size) != 0)
            def _():
                pltpu.make_async_copy(xbuf.at[1 - slot, pl.ds(0, size)], xbuf.at[slot, pl.ds(0, size)],
                                      sem.at[slot]).wait()

    @pl.when(j < n_used)
    def _():
        @pl.when(step == 0)
        def _():
            @pl.when(j == 0)
            def _():
                xbuf[...] = jnp.zeros_like(xbuf)
                tile_fetch(j)

            tile_wait(j)
            rows_in = xbuf[j % 2]
            xb[...] = rows_in[:, :D_MODEL].astype(BF16)
            route = rows_in[:, D_MODEL:]
            lane = lax.broadcasted_iota(jnp.int32, route.shape, 1)
            for n, which in enumerate((ROUTE_E1, ROUTE_E2, ROUTE_W1, ROUTE_W2)):
                col = jnp.sum(jnp.where(lane == which, route, 0.0), axis=1, keepdims=True)
                gate_tabs[n] = jnp.broadcast_to(col, route.shape)

            @pl.when(j + 1 < n_used)
            def _():
                tile_fetch(j + 1)

        def evaluate(rows):
            x = xb[:rows]
            total = None
            for s in range(EXPERTS_PER_STEP):
                expert = (tile_group_ref[j] * EXPERTS_PER_GROUP + step * EXPERTS_PER_STEP + s).astype(F32)
                gate = (jnp.where(gate_tabs[0, :rows] == expert, gate_tabs[2, :rows], 0.0)
                        + jnp.where(gate_tabs[1, :rows] == expert, gate_tabs[3, :rows], 0.0))
                hid = _silu(jnp.dot(x, w1_ref[0, s].astype(BF16), preferred_element_type=F32)) * jnp.dot(
                    x, w3_ref[0, s].astype(BF16), preferred_element_type=F32)
                y = jnp.dot(hid.astype(BF16), w2_ref[0, s].astype(BF16), preferred_element_type=F32)
                gated = jnp.concatenate(
                    [gate * y[:, c * LANES:(c + 1) * LANES] for c in range(D_MODEL // LANES)], axis=1)
                total = gated if total is None else total + gated

            @pl.when(step == 0)
            def _():
                ys_ref[:rows] = total
                if rows < GROUP_TILE:
                    ys_ref[rows:] = jnp.zeros((GROUP_TILE - rows, D_MODEL), F32)

            @pl.when(step > 0)
            def _():
                ys_ref[:rows] += total

        half = GROUP_TILE // 2

        @pl.when(fill_ref[j] > half)
        def _():
            evaluate(GROUP_TILE)

        @pl.when(fill_ref[j] <= half)
        def _():
            evaluate(half)

    @pl.when((j >= n_used) & (step == 0))
    def _():
        ys_ref[...] = jnp.zeros_like(ys_ref)


def _experts(xloc, tile_tables, runs, slots, w1, w3, w2):
    steps = EXPERTS_PER_GROUP // EXPERTS_PER_STEP
    per_step = lambda w: w.reshape((N_EXPERTS // EXPERTS_PER_STEP, EXPERTS_PER_STEP) + w.shape[1:])
    w_spec = lambda shape: pl.BlockSpec((1, EXPERTS_PER_STEP) + shape, lambda j, s, tg, nu, *_: (
        tg[jnp.minimum(j, nu[0] - 1)] * steps + jnp.where(j < nu[0], s, steps - 1), 0, 0, 0))
    grid_spec = pltpu.PrefetchScalarGridSpec(
        num_scalar_prefetch=8,
        grid=(slots // GROUP_TILE, steps),
        in_specs=[
            pl.BlockSpec(memory_space=pl.ANY),
            w_spec((D_MODEL, D_EXPERT)), w_spec((D_MODEL, D_EXPERT)), w_spec((D_EXPERT, D_MODEL)),
        ],
        out_specs=pl.BlockSpec((GROUP_TILE, D_MODEL), lambda j, s, *_: (j, 0)),
        scratch_shapes=[pltpu.VMEM((2, GROUP_TILE, ROW_W), F32), pltpu.VMEM((GROUP_TILE, D_MODEL), BF16),
                        pltpu.VMEM((4, GROUP_TILE, LANES), F32), pltpu.SemaphoreType.DMA((2,))],
    )
    return pl.pallas_call(
        _expert_kernel,
        grid_spec=grid_spec,
        out_shape=jax.ShapeDtypeStruct((slots, D_MODEL), F32),
        compiler_params=pltpu.CompilerParams(
            dimension_semantics=("arbitrary", "arbitrary"), vmem_limit_bytes=VMEM_LIMIT),
        name="experts",
    )(*tile_tables, *runs, xloc, per_step(w1), per_step(w3), per_step(w2))


def _combine_kernel(n_tiles, tile_base, run_len_ref, run_src_ref, run_dst_ref,
                    route_ref, g2_ref, fg_ref, x1_hbm, ys_hbm, o_ref, buf, x1_buf, sem, x1_sem):
    i = pl.program_id(0)
    slot = i % COMBINE_RING

    def x1_copy(local_tile):
        s = local_tile % COMBINE_RING
        rows = pl.ds(pl.multiple_of((local_tile + tile_base) * OUTPROJ_TILE, OUTPROJ_TILE), OUTPROJ_TILE)
        return pltpu.make_async_copy(x1_hbm.at[rows], x1_buf.at[s], x1_sem.at[s])

    def run_copies(local_tile, act):
        s = local_tile % COMBINE_RING
        tile = local_tile + tile_base
        for g in range(N_GROUPS):
            n = run_len_ref[tile * N_GROUPS + g]
            src = run_src_ref[tile * N_GROUPS + g]
            dst = run_dst_ref[tile * N_GROUPS + g]
            for size in RUN_PIECES:
                done = n & (-2 * size)

                @pl.when((n & size) != 0)
                def _():
                    act(pltpu.make_async_copy(
                        ys_hbm.at[pl.ds(pl.multiple_of(dst + done, SUBLANES), size)],
                        buf.at[s, pl.ds(pl.multiple_of(src + done, SUBLANES), size)], sem.at[s]))

    def fetch(local_tile):
        run_copies(local_tile, lambda cp: cp.start())
        x1_copy(local_tile).start()

    @pl.when(i == 0)
    def _():
        buf[...] = jnp.zeros_like(buf)
        for ahead in range(min(COMBINE_RING - 1, n_tiles)):
            fetch(i + ahead)

    @pl.when(i + COMBINE_RING - 1 < n_tiles)
    def _():
        fetch(i + COMBINE_RING - 1)

    run_copies(i, lambda cp: cp.wait())
    x1_copy(i).wait()
    route = route_ref[...]
    lane = lax.broadcasted_iota(jnp.int32, route.shape, 1)
    local = jnp.sum(jnp.where(lane == ROUTE_LOCAL, route, 0.0), axis=1, keepdims=True)
    pick = jnp.where(lax.broadcasted_iota(jnp.int32, (route.shape[0], LOCAL_ROWS), 1).astype(F32) == local,
                     1.0, 0.0).astype(BF16)
    moe = sum(jnp.dot(pick, piece, preferred_element_type=F32) for piece in _split3(buf[slot])[:2])
    y = x1_buf[slot] + g2_ref[0, 0] * moe
    o_ref[...] = _rms(y) * fg_ref[...]


def _combine(ys, runs, x1, route, tokens, mod4, mod_row_of_tile, final_g, tile_base):
    tiles = tokens // OUTPROJ_TILE
    tok = lambda w: pl.BlockSpec((OUTPROJ_TILE, w), lambda i, *_: (i + tile_base, 0))
    grid_spec = pltpu.PrefetchScalarGridSpec(
        num_scalar_prefetch=3,
        grid=(tiles,),
        in_specs=[
            tok(ROUTER_COLS),
            pl.BlockSpec((1, 1, 1, D_MODEL), lambda i, *_: (mod_row_of_tile(i + tile_base), 5, 0, 0)),
            pl.BlockSpec((1, D_MODEL), lambda i, *_: (0, 0)),
            pl.BlockSpec(memory_space=pl.ANY),
            pl.BlockSpec(memory_space=pl.ANY),
        ],
        out_specs=pl.BlockSpec((OUTPROJ_TILE, D_MODEL), lambda i, *_: (i, 0)),
        scratch_shapes=[pltpu.VMEM((COMBINE_RING, LOCAL_ROWS, D_MODEL), F32),
                        pltpu.VMEM((COMBINE_RING, OUTPROJ_TILE, D_MODEL), F32),
                        pltpu.SemaphoreType.DMA((COMBINE_RING,)), pltpu.SemaphoreType.DMA((COMBINE_RING,))],
    )
    return pl.pallas_call(
        functools.partial(_combine_kernel, tiles, tile_base),
        grid_spec=grid_spec,
        out_shape=jax.ShapeDtypeStruct((tokens, D_MODEL), F32),
        compiler_params=pltpu.CompilerParams(
            dimension_semantics=("arbitrary",), vmem_limit_bytes=VMEM_LIMIT),
        name="combine",
    )(*runs, route, mod4, final_g, x1, ys)


def _routing_tables(counts):
    counts = counts.astype(jnp.int32)
    tiles = counts.shape[0]
    run_len = ((counts + SUBLANES - 1) // SUBLANES) * SUBLANES
    run_src = jnp.cumsum(run_len, axis=1) - run_len
    group_rows = jnp.sum(run_len, axis=0)
    padded = ((group_rows + GROUP_TILE - 1) // GROUP_TILE) * GROUP_TILE
    ends = jnp.cumsum(padded)
    offs = ends - padded
    run_dst = offs[None, :] + jnp.cumsum(run_len, axis=0) - run_len

    n_used = ends[-1] // GROUP_TILE
    max_rows = tiles * (OUTPROJ_TILE + N_GROUPS * (SUBLANES - 1))
    max_tiles = max_rows // GROUP_TILE + N_GROUPS
    tile_ids = jnp.minimum(jnp.arange(max_tiles, dtype=jnp.int32), n_used - 1)
    tile_group = jnp.sum(tile_ids[:, None] * GROUP_TILE >= ends[None, :], axis=1).astype(jnp.int32)
    of_group = (tile_group[:, None] == jnp.arange(N_GROUPS, dtype=jnp.int32))[:, None, :]
    start = jnp.sum(jnp.where(of_group, run_dst[None], 0), axis=-1)
    stop = start + jnp.sum(jnp.where(of_group, run_len[None], 0), axis=-1)
    row0 = (tile_ids * GROUP_TILE)[:, None]
    first = jnp.sum(stop <= row0, axis=1).astype(jnp.int32)
    last = jnp.sum(start < row0 + GROUP_TILE, axis=1).astype(jnp.int32) - 1
    group_end = jnp.sum(jnp.where(of_group[:, 0, :], (offs + group_rows)[None, :], 0), axis=-1)
    fill = jnp.clip(group_end - row0[:, 0], 0, GROUP_TILE).astype(jnp.int32)
    flat = lambda a: a.reshape(-1)
    return ((tile_group, n_used.reshape(1), first, last, fill), (flat(run_len), flat(run_src), flat(run_dst)),
            max_tiles * GROUP_TILE)


def _mixer(x, mod4, mod_row, is_grid, s_f0, s_b0, p):
    norm_mix_g, w_in, conv_w, decay_rows = p
    y_conv, *qkvg, w_in_bf = _inproj(x, mod4, mod_row, norm_mix_g, w_in, conv_w, is_grid)
    per_seq = lambda a: a.reshape(x.shape[0], x.shape[1], a.shape[-1])
    ret = _retention(*map(per_seq, qkvg), decay_rows, s_f0, s_b0, emit_state=not is_grid)
    return y_conv, ret[0].reshape(-1, RET_W), ret[1:], w_in_bf


def kernel(x_prompt, x_sample, state_ret_fwd, state_ret_bwd, c, c_ctx, norm_mix_g, norm_ffn_g, w_ada, b_ada, w_in, conv_w, ret_decay_fwd, ret_decay_bwd, w_out, w_router_group, b_router_group, w_router_expert, b_router_expert, w_gate_e, w_up_e, w_down_e, final_norm_g):
    assert norm_mix_g.shape[0] == 1, "single-layer backbone"
    n_lat = c.shape[0]
    ctx_row = n_lat
    mod_rows = 8
    cvec = jnp.concatenate([c, c_ctx[None, :], jnp.zeros((mod_rows - n_lat - 1, D_MODEL), F32)], axis=0)
    mod4 = _modulation(cvec, w_ada[0], b_ada[0][None, :])

    pad = ROUTER_COLS - N_GROUPS - N_EXPERTS
    w_router = jnp.concatenate(
        [w_router_group[0], w_router_expert[0], jnp.zeros((D_MODEL, pad), F32)], axis=1).astype(BF16)
    b_router = jnp.concatenate([b_router_group[0], b_router_expert[0], jnp.zeros((pad,), F32)])[None, :]
    decay_rows = jnp.broadcast_to(
        jnp.concatenate([ret_decay_fwd[0], ret_decay_bwd[0]])[:, None], (2 * RET_HEADS, LANES))
    p_mix = (norm_mix_g, w_in[0], conv_w[0], decay_rows)
    final_g = final_norm_g[None, :]

    ctx_tokens = x_prompt.shape[0] * x_prompt.shape[1]
    lat_tokens = x_sample.shape[0] * x_sample.shape[1]
    ctx_tiles = ctx_tokens // OUTPROJ_TILE
    lat_tiles_per_seq = x_sample.shape[1] // OUTPROJ_TILE
    tile_mod = lambda i: jnp.where(i < ctx_tiles, ctx_row, (i - ctx_tiles) // lat_tiles_per_seq)
    flat = lambda a: a.reshape(-1, a.shape[-1])

    yc_c, yr_c, (s_f, s_b), w_in_bf = _mixer(x_prompt, mod4, lambda b: ctx_row, False, None, None, p_mix)
    p_mix = (norm_mix_g, w_in_bf, conv_w[0], decay_rows)
    yc_l, yr_l, _, _ = _mixer(x_sample, mod4, lambda b: b, True, state_ret_fwd, state_ret_bwd, p_mix)

    x1, xloc, route, cnt = _outproj((yc_c, yr_c, flat(x_prompt)), (yc_l, yr_l, flat(x_sample)), mod4, tile_mod,
                                    norm_ffn_g, w_out[0], w_router, b_router)
    tile_tables, runs, slots = _routing_tables(cnt[:, 0, :N_GROUPS])
    ys = _experts(xloc, tile_tables, runs, slots, w_gate_e[0], w_up_e[0], w_down_e[0])
    y_prompt = _combine(ys, runs, x1, route, ctx_tokens, mod4, tile_mod, final_g, 0)
    y_sample = _combine(ys, runs, x1, route, lat_tokens, mod4, tile_mod, final_g, ctx_tiles)
    return (y_prompt.reshape(x_prompt.shape), y_sample.reshape(x_sample.shape),
            s_f.astype(x_prompt.dtype), s_b.astype(x_prompt.dtype))
```
